```python
import math
import jax, jax.numpy as jnp
from jax import lax
import numpy as np

D_MODEL = 1024
BATCH = 8
SEQ = 8192
DEPTH = 1

MEM_LEN = 256
ATT_W = D_MODEL // 2
ATT_HD = 64
ATT_HEADS = ATT_W // ATT_HD
DILATED = ((128, 1), (512, 4), (2048, 16))
BLK = 128
ML_W = D_MODEL // 2
ML_HEADS = 4
ML_HD = ML_W // ML_HEADS
CHUNK = 128
CONV_K = 4
MIX_W = ATT_W + ML_W
SPLITS = [ATT_W, 2 * ATT_W, 3 * ATT_W, 3 * ATT_W + 2 * ML_W, 3 * ATT_W + 3 * ML_W, 3 * ATT_W + 4 * ML_W]
W_IN = 3 * ATT_W + 4 * ML_W + 2 * ML_HEADS
XA_HEADS = 4
XA_HD = D_MODEL // XA_HEADS
D_FF = ((8 * D_MODEL // 3 + 255) // 256) * 256
REL_BUCKETS = 32
REL_MAX_DIST = 2048
ALPHA = (2 * DEPTH) ** 0.25
BETA = (8 * DEPTH) ** -0.25
LN_EPS = 1e-5
NEG = -1e30

kernel_name = 'hymba_dilated_mlstm_macaron_deepnorm'


def layer_norm(x, g, b):
    xf = x.astype(jnp.float32)
    mu = jnp.mean(xf, -1, keepdims=True)
    var = jnp.mean(jnp.square(xf - mu), -1, keepdims=True)
    y = (xf - mu) * lax.rsqrt(var + LN_EPS)
    return (y * g.astype(jnp.float32) + b.astype(jnp.float32)).astype(x.dtype)


def swiglu(x, w_gate, w_up, w_down):
    return (jax.nn.silu(x @ w_gate) * (x @ w_up)) @ w_down


def t5_bucket(dist):
    exact = REL_BUCKETS // 2
    df = jnp.maximum(dist, 1).astype(jnp.float32)
    large = exact + (jnp.log(df / exact) / math.log(REL_MAX_DIST / exact) * (REL_BUCKETS - exact)).astype(jnp.int32)
    large = jnp.minimum(large, REL_BUCKETS - 1)
    return jnp.where(dist < exact, dist, large)


def band_offsets():
    qi = jnp.arange(BLK)[:, None]
    ki = jnp.arange(2 * BLK)[None, :]
    return qi + BLK - ki, ki


def branch_bias(rel_table, dilation, n_keys):
    off, _ = band_offsets()
    bucket = t5_bucket(dilation * jnp.clip(off, 0, n_keys))
    return jnp.transpose(rel_table.astype(jnp.float32)[bucket], (2, 0, 1))


def banded_attention(q, k, v, bias, n_keys):
    B, H, R, M, E = q.shape
    nb = M // BLK
    blocks = lambda t: t.reshape(B, H, R, nb, BLK, E)
    shift = lambda t: jnp.concatenate([jnp.zeros_like(t[:, :, :, :1]), t[:, :, :, :-1]], axis=3)
    qb, kb, vb = blocks(q), blocks(k), blocks(v)
    kk = jnp.concatenate([shift(kb), kb], axis=4)
    vv = jnp.concatenate([shift(vb), vb], axis=4)
    logits = jnp.einsum('bhrnqe,bhrnke->bhrnqk', qb, kk).astype(jnp.float32) * (ATT_HD ** -0.5)
    logits = logits + bias[:, None, None]
    off, ki = band_offsets()
    band = (off >= 0) & (off <= n_keys)
    valid = band[None] & ((jnp.arange(nb)[:, None, None] > 0) | (ki[None] >= BLK))
    logits = jnp.where(valid, logits, NEG)
    mx = jnp.max(logits, -1, keepdims=True)
    p = jnp.exp(logits - mx)
    s = jnp.sum(p, -1)
    o = jnp.einsum('bhrnqk,bhrnke->bhrnqe', p, vv) / s[..., None]
    lse = mx[..., 0] + jnp.log(s)
    return o.reshape(B, H, R, M, E), lse.reshape(B, H, R, M)


def dilated_branch(q, k, v, rel_table, window, dilation):
    B, H, S, E = q.shape
    span = dilation * BLK
    s_pad = -(-S // span) * span
    m_len = s_pad // dilation
    n_keys = window // dilation

    def fold(t):
        t = jnp.pad(t, ((0, 0), (0, 0), (0, s_pad - S), (0, 0)))
        return t.reshape(B, H, m_len, dilation, E).transpose(0, 1, 3, 2, 4)

    o, lse = banded_attention(fold(q), fold(k), fold(v), branch_bias(rel_table, dilation, n_keys), n_keys)
    o = o.transpose(0, 1, 3, 2, 4).reshape(B, H, s_pad, E)[:, :, :S]
    lse = lse.transpose(0, 1, 3, 2).reshape(B, H, s_pad)[:, :, :S]
    return o, lse


def dilated_attention(q, k, v, rel_table):
    outs, lses = [], []
    for window, dilation in DILATED:
        o, l = dilated_branch(q, k, v, rel_table, window, dilation)
        outs.append(o)
        lses.append(l)
    wts = jax.nn.softmax(jnp.stack(lses, 0), axis=0)
    return jnp.sum(wts[..., None] * jnp.stack(outs, 0), 0)


def causal_conv(x, w, b):
    S = x.shape[1]
    xp = jnp.pad(x, ((0, 0), (CONV_K - 1, 0), (0, 0)))
    y = b
    for j in range(CONV_K):
        y = y + xp[:, j:j + S] * w[j]
    return y


def mlstm(q, k, v, ig, fg):
    B, H, S, E = q.shape
    nc = S // CHUNK
    ch = lambda t: t.reshape(B, H, nc, CHUNK, E)
    q, k, v = ch(q), ch(k) * (E ** -0.5), ch(v)
    ig = ig.astype(jnp.float32).reshape(B, H, nc, CHUNK)
    logf = jax.nn.log_sigmoid(fg.astype(jnp.float32)).reshape(B, H, nc, CHUNK)
    b = jnp.cumsum(logf, -1)
    g = b[..., -1]
    a = g[..., None] - b + ig
    m_loc = jnp.max(a, -1)
    wa = jnp.exp(a - m_loc[..., None])
    c_loc = jnp.einsum('bhcl,bhcld,bhcle->bhcde', wa, v, k)
    n_loc = jnp.einsum('bhcl,bhcle->bhce', wa, k)

    def step(carry, inp):
        c, n, m = carry
        g_c, cl, nl, ml = inp
        m_new = jnp.maximum(g_c + m, ml)
        sp = jnp.exp(g_c + m - m_new)
        sl = jnp.exp(ml - m_new)
        c_new = sp[..., None, None] * c + sl[..., None, None] * cl
        n_new = sp[..., None] * n + sl[..., None] * nl
        return (c_new, n_new, m_new), (c, n, m)

    init = (jnp.zeros((B, H, E, E), jnp.float32), jnp.zeros((B, H, E), jnp.float32), jnp.zeros((B, H), jnp.float32))
    mv = lambda t: jnp.moveaxis(t, 2, 0)
    _, (c_prev, n_prev, m_prev) = lax.scan(step, init, (mv(g), mv(c_loc), mv(n_loc), mv(m_loc)))
    c_prev = jnp.moveaxis(c_prev, 0, 2)
    n_prev = jnp.moveaxis(n_prev, 0, 2)
    m_prev = jnp.moveaxis(m_prev, 0, 2)

    causal = jnp.tril(jnp.ones((CHUNK, CHUNK), bool))
    d_log = jnp.where(causal, b[..., :, None] - b[..., None, :] + ig[..., None, :], -jnp.inf)
    e_log = b + m_prev[..., None]
    m_t = jnp.maximum(e_log, jnp.max(d_log, -1))
    d_w = jnp.exp(d_log - m_t[..., None])
    e_w = jnp.exp(e_log - m_t)
    s_qk = jnp.einsum('bhcte,bhcse->bhcts', q, k) * d_w
    num = e_w[..., None] * jnp.einsum('bhcde,bhcte->bhctd', c_prev, q) + jnp.einsum('bhcts,bhcsd->bhctd', s_qk, v)
    den = e_w * jnp.einsum('bhce,bhcte->bhct', n_prev, q) + jnp.sum(s_qk, -1)
    h = num / jnp.maximum(jnp.abs(den), jnp.exp(-m_t))[..., None]
    return h.reshape(B, H, S, E)


def head_norm(h, g, n_heads):
    B, S, W = h.shape
    hf = h.astype(jnp.float32).reshape(B, S, n_heads, W // n_heads)
    mu = jnp.mean(hf, -1, keepdims=True)
    var = jnp.mean(jnp.square(hf - mu), -1, keepdims=True)
    y = ((hf - mu) * lax.rsqrt(var + LN_EPS)).reshape(B, S, W)
    return y * g.astype(jnp.float32)


def hybrid_mixer(x, w_in, conv_w, conv_b, ig_b, fg_b, ml_g, w_out, rel_table):
    B, S, _ = x.shape
    proj = x @ w_in
    qa, ka, va, qk_m, vm, om, gates = jnp.split(proj, SPLITS, axis=-1)
    heads = lambda t, h: t.reshape(B, S, h, -1).transpose(0, 2, 1, 3)
    att = dilated_attention(heads(qa, ATT_HEADS), heads(ka, ATT_HEADS), heads(va, ATT_HEADS), rel_table)
    att = att.transpose(0, 2, 1, 3).reshape(B, S, ATT_W).astype(x.dtype)
    qk_m = jax.nn.silu(causal_conv(qk_m, conv_w, conv_b))
    qm, km = jnp.split(qk_m, 2, axis=-1)
    ig = (gates[..., :ML_HEADS] + ig_b).transpose(0, 2, 1)
    fg = (gates[..., ML_HEADS:] + fg_b).transpose(0, 2, 1)
    h = mlstm(heads(qm, ML_HEADS), heads(km, ML_HEADS), heads(vm, ML_HEADS), ig, fg)
    h = h.transpose(0, 2, 1, 3).reshape(B, S, ML_W)
    h = jax.nn.sigmoid(om.astype(jnp.float32)) * h
    h = head_norm(h, ml_g, ML_HEADS).astype(x.dtype)
    return jnp.concatenate([att, h], axis=-1) @ w_out


def memory_attention(x, mem, wq, wkv, wo):
    B, S, _ = x.shape
    L = mem.shape[1]
    q = (x @ wq).reshape(B, S, XA_HEADS, XA_HD)
    kv = (mem @ wkv).reshape(B, L, 2, XA_HEADS, XA_HD)
    k, v = kv[:, :, 0], kv[:, :, 1]
    logits = jnp.einsum('bshe,bmhe->bhsm', q, k).astype(jnp.float32) * (XA_HD ** -0.5)
    p = jax.nn.softmax(logits, axis=-1)
    o = jnp.einsum('bhsm,bmhe->bshe', p, v).reshape(B, S, D_MODEL).astype(x.dtype)
    return o @ wo


def _fwd_setup_inputs(seed: int = 0) -> dict:
    key = jax.random.key(seed)
    ks = jax.random.split(key, 20)
    f32 = jnp.float32
    nrm = lambda k, shape, scale: jax.random.normal(k, shape, f32) * scale
    return {
        'x': nrm(ks[0], (BATCH, SEQ, D_MODEL), 1.0),
        'mem': nrm(ks[1], (BATCH, MEM_LEN, D_MODEL), 1.0),
        'rel_bias': nrm(ks[2], (REL_BUCKETS, ATT_HEADS), 0.1),
        'ln_g': 1.0 + nrm(ks[3], (DEPTH, 4, D_MODEL), 0.02),
        'ln_b': nrm(ks[4], (DEPTH, 4, D_MODEL), 0.02),
        'ffn_w_gate': nrm(ks[5], (DEPTH, 2, D_MODEL, D_FF), D_MODEL ** -0.5),
        'ffn_w_up': nrm(ks[6], (DEPTH, 2, D_MODEL, D_FF), D_MODEL ** -0.5),
        'ffn_w_down': nrm(ks[7], (DEPTH, 2, D_FF, D_MODEL), BETA * D_FF ** -0.5),
        'w_in': nrm(ks[8], (DEPTH, D_MODEL, W_IN), D_MODEL ** -0.5),
        'conv_w': nrm(ks[9], (DEPTH, CONV_K, 2 * ML_W), CONV_K ** -0.5),
        'conv_b': nrm(ks[10], (DEPTH, 2 * ML_W), 0.02),
        'ig_bias': nrm(ks[11], (DEPTH, ML_HEADS), 0.1),
        'fg_bias': jnp.linspace(3.0, 6.0, ML_HEADS, dtype=f32)[None] + nrm(ks[12], (DEPTH, ML_HEADS), 0.1),
        'ml_norm_g': 1.0 + nrm(ks[13], (DEPTH, ML_W), 0.02),
        'w_out': nrm(ks[14], (DEPTH, MIX_W, D_MODEL), BETA * MIX_W ** -0.5),
        'xq_w': nrm(ks[15], (DEPTH, D_MODEL, D_MODEL), D_MODEL ** -0.5),
        'xkv_w': nrm(ks[16], (DEPTH, D_MODEL, 2 * D_MODEL), D_MODEL ** -0.5),
        'xo_w': nrm(ks[17], (DEPTH, D_MODEL, D_MODEL), BETA * D_MODEL ** -0.5),
    }


def _fwd_reference(x, mem, rel_bias, ln_g, ln_b, ffn_w_gate, ffn_w_up, ffn_w_down, w_in, conv_w, conv_b,
              ig_bias, fg_bias, ml_norm_g, w_out, xq_w, xkv_w, xo_w):
    for l in range(DEPTH):
        x = layer_norm(ALPHA * x + 0.5 * swiglu(x, ffn_w_gate[l, 0], ffn_w_up[l, 0], ffn_w_down[l, 0]), ln_g[l, 0], ln_b[l, 0])
        x = layer_norm(ALPHA * x + hybrid_mixer(x, w_in[l], conv_w[l], conv_b[l], ig_bias[l], fg_bias[l],
                                                ml_norm_g[l], w_out[l], rel_bias), ln_g[l, 1], ln_b[l, 1])
        x = layer_norm(ALPHA * x + memory_attention(x, mem, xq_w[l], xkv_w[l], xo_w[l]), ln_g[l, 2], ln_b[l, 2])
        x = layer_norm(ALPHA * x + 0.5 * swiglu(x, ffn_w_gate[l, 1], ffn_w_up[l, 1], ffn_w_down[l, 1]), ln_g[l, 3], ln_b[l, 3])
    return x


import jax as _jax
import jax.numpy as _jnp

TWIN_FORMAT = 'train_step'
FWD_PARAMS = ['x', 'mem', 'rel_bias', 'ln_g', 'ln_b', 'ffn_w_gate', 'ffn_w_up', 'ffn_w_down', 'w_in', 'conv_w', 'conv_b', 'ig_bias', 'fg_bias', 'ml_norm_g', 'w_out', 'xq_w', 'xkv_w', 'xo_w']
TWIN_WEIGHTS = ['rel_bias', 'ln_g', 'ln_b', 'ffn_w_gate', 'ffn_w_up', 'ffn_w_down', 'w_in', 'conv_w', 'conv_b', 'ig_bias', 'fg_bias', 'ml_norm_g', 'w_out', 'xq_w', 'xkv_w', 'xo_w']
TWIN_DIFF_INPUT = 'x'
TWIN_INPUTS = ['x', 'mem', 'rel_bias', 'ln_g', 'ln_b', 'ffn_w_gate', 'ffn_w_up', 'ffn_w_down', 'w_in', 'conv_w', 'conv_b', 'ig_bias', 'fg_bias', 'ml_norm_g', 'w_out', 'xq_w', 'xkv_w', 'xo_w', 'loss_target', 'm_rel_bias', 'm_ln_g', 'm_ln_b', 'm_ffn_w_gate', 'm_ffn_w_up', 'm_ffn_w_down', 'm_w_in', 'm_conv_w', 'm_conv_b', 'm_ig_bias', 'm_fg_bias', 'm_ml_norm_g', 'm_w_out', 'm_xq_w', 'm_xkv_w', 'm_xo_w', 'v_rel_bias', 'v_ln_g', 'v_ln_b', 'v_ffn_w_gate', 'v_ffn_w_up', 'v_ffn_w_down', 'v_w_in', 'v_conv_w', 'v_conv_b', 'v_ig_bias', 'v_fg_bias', 'v_ml_norm_g', 'v_w_out', 'v_xq_w', 'v_xkv_w', 'v_xo_w']
TWIN_OUTPUTS = ['loss', 'grad_x', 'grad_rel_bias', 'grad_ln_g', 'grad_ln_b', 'grad_ffn_w_gate', 'grad_ffn_w_up', 'grad_ffn_w_down', 'grad_w_in', 'grad_conv_w', 'grad_conv_b', 'grad_ig_bias', 'grad_fg_bias', 'grad_ml_norm_g', 'grad_w_out', 'grad_xq_w', 'grad_xkv_w', 'grad_xo_w', 'delta_rel_bias', 'delta_ln_g', 'delta_ln_b', 'delta_ffn_w_gate', 'delta_ffn_w_up', 'delta_ffn_w_down', 'delta_w_in', 'delta_conv_w', 'delta_conv_b', 'delta_ig_bias', 'delta_fg_bias', 'delta_ml_norm_g', 'delta_w_out', 'delta_xq_w', 'delta_xkv_w', 'delta_xo_w', 'new_m_rel_bias', 'new_m_ln_g', 'new_m_ln_b', 'new_m_ffn_w_gate', 'new_m_ffn_w_up', 'new_m_ffn_w_down', 'new_m_w_in', 'new_m_conv_w', 'new_m_conv_b', 'new_m_ig_bias', 'new_m_fg_bias', 'new_m_ml_norm_g', 'new_m_w_out', 'new_m_xq_w', 'new_m_xkv_w', 'new_m_xo_w', 'new_v_rel_bias', 'new_v_ln_g', 'new_v_ln_b', 'new_v_ffn_w_gate', 'new_v_ffn_w_up', 'new_v_ffn_w_down', 'new_v_w_in', 'new_v_conv_w', 'new_v_conv_b', 'new_v_ig_bias', 'new_v_fg_bias', 'new_v_ml_norm_g', 'new_v_w_out', 'new_v_xq_w', 'new_v_xkv_w', 'new_v_xo_w']
TWIN_LEAF_KINDS = {'loss': 'loss', 'grad_x': 'grad_x', 'grad_rel_bias': 'grad_w', 'grad_ln_g': 'grad_w', 'grad_ln_b': 'grad_w', 'grad_ffn_w_gate': 'grad_w', 'grad_ffn_w_up': 'grad_w', 'grad_ffn_w_down': 'grad_w', 'grad_w_in': 'grad_w', 'grad_conv_w': 'grad_w', 'grad_conv_b': 'grad_w', 'grad_ig_bias': 'grad_w', 'grad_fg_bias': 'grad_w', 'grad_ml_norm_g': 'grad_w', 'grad_w_out': 'grad_w', 'grad_xq_w': 'grad_w', 'grad_xkv_w': 'grad_w', 'grad_xo_w': 'grad_w', 'delta_rel_bias': 'delta_w', 'delta_ln_g': 'delta_w', 'delta_ln_b': 'delta_w', 'delta_ffn_w_gate': 'delta_w', 'delta_ffn_w_up': 'delta_w', 'delta_ffn_w_down': 'delta_w', 'delta_w_in': 'delta_w', 'delta_conv_w': 'delta_w', 'delta_conv_b': 'delta_w', 'delta_ig_bias': 'delta_w', 'delta_fg_bias': 'delta_w', 'delta_ml_norm_g': 'delta_w', 'delta_w_out': 'delta_w', 'delta_xq_w': 'delta_w', 'delta_xkv_w': 'delta_w', 'delta_xo_w': 'delta_w', 'new_m_rel_bias': 'new_m', 'new_m_ln_g': 'new_m', 'new_m_ln_b': 'new_m', 'new_m_ffn_w_gate': 'new_m', 'new_m_ffn_w_up': 'new_m', 'new_m_ffn_w_down': 'new_m', 'new_m_w_in': 'new_m', 'new_m_conv_w': 'new_m', 'new_m_conv_b': 'new_m', 'new_m_ig_bias': 'new_m', 'new_m_fg_bias': 'new_m', 'new_m_ml_norm_g': 'new_m', 'new_m_w_out': 'new_m', 'new_m_xq_w': 'new_m', 'new_m_xkv_w': 'new_m', 'new_m_xo_w': 'new_m', 'new_v_rel_bias': 'new_v', 'new_v_ln_g': 'new_v', 'new_v_ln_b': 'new_v', 'new_v_ffn_w_gate': 'new_v', 'new_v_ffn_w_up': 'new_v', 'new_v_ffn_w_down': 'new_v', 'new_v_w_in': 'new_v', 'new_v_conv_w': 'new_v', 'new_v_conv_b': 'new_v', 'new_v_ig_bias': 'new_v', 'new_v_fg_bias': 'new_v', 'new_v_ml_norm_g': 'new_v', 'new_v_w_out': 'new_v', 'new_v_xq_w': 'new_v', 'new_v_xkv_w': 'new_v', 'new_v_xo_w': 'new_v'}


def _forward(args):
    return _fwd_reference(*[args[k] for k in FWD_PARAMS])


def _output_shape():
    def fwd():
        inp = _fwd_setup_inputs(0)
        return _fwd_reference(*[inp[k] for k in FWD_PARAMS])
    out = _jax.eval_shape(fwd)
    return out.shape, out.dtype

N_MICROBATCH = 1
ADAM_LR = 0.001
ADAM_B1 = 0.9
ADAM_B2 = 0.999
ADAM_EPS = 1e-08
ADAM_WD = 0.01
ADAM_STEP = 10
PER_EXAMPLE_BATCH_AXIS = {'x': 0, 'mem': 0, 'loss_target': 0}
SHARED_INPUTS = []
_WEIGHT_DTYPES = {'rel_bias': _jnp.float32, 'ln_g': _jnp.float32, 'ln_b': _jnp.float32, 'ffn_w_gate': _jnp.float32, 'ffn_w_up': _jnp.float32, 'ffn_w_down': _jnp.float32, 'w_in': _jnp.float32, 'conv_w': _jnp.float32, 'conv_b': _jnp.float32, 'ig_bias': _jnp.float32, 'fg_bias': _jnp.float32, 'ml_norm_g': _jnp.float32, 'w_out': _jnp.float32, 'xq_w': _jnp.float32, 'xkv_w': _jnp.float32, 'xo_w': _jnp.float32}
MOMENT_SCALE = {'rel_bias': 2.754322e-02, 'ln_g': 3.210606e+01, 'ln_b': 3.068099e+00, 'ffn_w_gate': 2.441342e-02, 'ffn_w_up': 2.368979e-02, 'ffn_w_down': 6.642143e-02, 'w_in': 8.473321e-02, 'conv_w': 8.538967e-02, 'conv_b': 7.927169e-02, 'ig_bias': 1.239579e-02, 'fg_bias': 1.435192e+00, 'ml_norm_g': 1.407551e-01, 'w_out': 1.648643e-01, 'xq_w': 1.343164e-02, 'xkv_w': 1.460896e-02, 'xo_w': 2.604953e-02}


def _to_microbatches(a, axis):
    t = _jnp.moveaxis(a, axis, 0)
    t = t.reshape((N_MICROBATCH, t.shape[0] // N_MICROBATCH) + t.shape[1:])
    return _jnp.moveaxis(t, 1, axis + 1)


def setup_inputs(seed: int = 0) -> dict:
    inp = _fwd_setup_inputs(seed)
    key = _jax.random.fold_in(_jax.random.key(seed), 7919)
    shape, _ = _output_shape()
    out = dict(inp)
    out["loss_target"] = _jax.random.normal(_jax.random.fold_in(key, 0), shape, _jnp.float32)
    for i, name in enumerate(TWIN_WEIGHTS):
        w = inp[name].astype(_jnp.float32)
        if MOMENT_SCALE is None:
            s = _jnp.sqrt(_jnp.mean(_jnp.square(w)) + 1e-30)
        else:
            s = MOMENT_SCALE[name]
        km, kv = _jax.random.split(_jax.random.fold_in(key, i + 1))
        out[name] = w
        out["m_" + name] = s * _jax.random.normal(km, w.shape, _jnp.float32)
        out["v_" + name] = (s * s) * _jax.random.uniform(kv, w.shape, _jnp.float32, 0.5, 1.5)
    if N_MICROBATCH > 1:
        for name, axis in PER_EXAMPLE_BATCH_AXIS.items():
            out[name] = _to_microbatches(out[name], axis)
    return {'x': out['x'], 'mem': out['mem'], 'rel_bias': out['rel_bias'], 'ln_g': out['ln_g'], 'ln_b': out['ln_b'], 'ffn_w_gate': out['ffn_w_gate'], 'ffn_w_up': out['ffn_w_up'], 'ffn_w_down': out['ffn_w_down'], 'w_in': out['w_in'], 'conv_w': out['conv_w'], 'conv_b': out['conv_b'], 'ig_bias': out['ig_bias'], 'fg_bias': out['fg_bias'], 'ml_norm_g': out['ml_norm_g'], 'w_out': out['w_out'], 'xq_w': out['xq_w'], 'xkv_w': out['xkv_w'], 'xo_w': out['xo_w'], 'loss_target': out['loss_target'], 'm_rel_bias': out['m_rel_bias'], 'm_ln_g': out['m_ln_g'], 'm_ln_b': out['m_ln_b'], 'm_ffn_w_gate': out['m_ffn_w_gate'], 'm_ffn_w_up': out['m_ffn_w_up'], 'm_ffn_w_down': out['m_ffn_w_down'], 'm_w_in': out['m_w_in'], 'm_conv_w': out['m_conv_w'], 'm_conv_b': out['m_conv_b'], 'm_ig_bias': out['m_ig_bias'], 'm_fg_bias': out['m_fg_bias'], 'm_ml_norm_g': out['m_ml_norm_g'], 'm_w_out': out['m_w_out'], 'm_xq_w': out['m_xq_w'], 'm_xkv_w': out['m_xkv_w'], 'm_xo_w': out['m_xo_w'], 'v_rel_bias': out['v_rel_bias'], 'v_ln_g': out['v_ln_g'], 'v_ln_b': out['v_ln_b'], 'v_ffn_w_gate': out['v_ffn_w_gate'], 'v_ffn_w_up': out['v_ffn_w_up'], 'v_ffn_w_down': out['v_ffn_w_down'], 'v_w_in': out['v_w_in'], 'v_conv_w': out['v_conv_w'], 'v_conv_b': out['v_conv_b'], 'v_ig_bias': out['v_ig_bias'], 'v_fg_bias': out['v_fg_bias'], 'v_ml_norm_g': out['v_ml_norm_g'], 'v_w_out': out['v_w_out'], 'v_xq_w': out['v_xq_w'], 'v_xkv_w': out['v_xkv_w'], 'v_xo_w': out['v_xo_w']}


def _loss(weights, diff, rest, loss_target):
    with _jax.named_scope("forward"):
        args = {**rest, TWIN_DIFF_INPUT: diff, **{k: w.astype(_WEIGHT_DTYPES[k]) for k, w in weights.items()}}
        y = _forward(args)
    with _jax.named_scope("loss_head"):
        err = _jnp.square(y.astype(_jnp.float32) - loss_target)
        return 0.5 * _jnp.sum(_jnp.mean(err, axis=-1)) if err.ndim else 0.5 * err


def _adamw(w, g, m, v):
    m = ADAM_B1 * m + (1.0 - ADAM_B1) * g
    v = ADAM_B2 * v + (1.0 - ADAM_B2) * _jnp.square(g)
    m_hat = m / (1.0 - ADAM_B1 ** ADAM_STEP)
    v_hat = v / (1.0 - ADAM_B2 ** ADAM_STEP)
    delta = -ADAM_LR * (m_hat / (_jnp.sqrt(v_hat) + ADAM_EPS) + ADAM_WD * w)
    return delta, m, v


def reference(x, mem, rel_bias, ln_g, ln_b, ffn_w_gate, ffn_w_up, ffn_w_down, w_in, conv_w, conv_b, ig_bias, fg_bias, ml_norm_g, w_out, xq_w, xkv_w, xo_w, loss_target, m_rel_bias, m_ln_g, m_ln_b, m_ffn_w_gate, m_ffn_w_up, m_ffn_w_down, m_w_in, m_conv_w, m_conv_b, m_ig_bias, m_fg_bias, m_ml_norm_g, m_w_out, m_xq_w, m_xkv_w, m_xo_w, v_rel_bias, v_ln_g, v_ln_b, v_ffn_w_gate, v_ffn_w_up, v_ffn_w_down, v_w_in, v_conv_w, v_conv_b, v_ig_bias, v_fg_bias, v_ml_norm_g, v_w_out, v_xq_w, v_xkv_w, v_xo_w):
    given = dict(x=x, mem=mem, rel_bias=rel_bias, ln_g=ln_g, ln_b=ln_b, ffn_w_gate=ffn_w_gate, ffn_w_up=ffn_w_up, ffn_w_down=ffn_w_down, w_in=w_in, conv_w=conv_w, conv_b=conv_b, ig_bias=ig_bias, fg_bias=fg_bias, ml_norm_g=ml_norm_g, w_out=w_out, xq_w=xq_w, xkv_w=xkv_w, xo_w=xo_w, loss_target=loss_target, m_rel_bias=m_rel_bias, m_ln_g=m_ln_g, m_ln_b=m_ln_b, m_ffn_w_gate=m_ffn_w_gate, m_ffn_w_up=m_ffn_w_up, m_ffn_w_down=m_ffn_w_down, m_w_in=m_w_in, m_conv_w=m_conv_w, m_conv_b=m_conv_b, m_ig_bias=m_ig_bias, m_fg_bias=m_fg_bias, m_ml_norm_g=m_ml_norm_g, m_w_out=m_w_out, m_xq_w=m_xq_w, m_xkv_w=m_xkv_w, m_xo_w=m_xo_w, v_rel_bias=v_rel_bias, v_ln_g=v_ln_g, v_ln_b=v_ln_b, v_ffn_w_gate=v_ffn_w_gate, v_ffn_w_up=v_ffn_w_up, v_ffn_w_down=v_ffn_w_down, v_w_in=v_w_in, v_conv_w=v_conv_w, v_conv_b=v_conv_b, v_ig_bias=v_ig_bias, v_fg_bias=v_fg_bias, v_ml_norm_g=v_ml_norm_g, v_w_out=v_w_out, v_xq_w=v_xq_w, v_xkv_w=v_xkv_w, v_xo_w=v_xo_w)
    weights = {n: given[n] for n in TWIN_WEIGHTS}
    shared = {n: given[n] for n in SHARED_INPUTS}
    per_example = {n: given[n] for n in ['x', 'mem']}
    grad_fn = _jax.value_and_grad(_loss, argnums=(0, 1))

    def one_microbatch(ex, loss_target):
        ex = dict(ex)
        diff = ex.pop(TWIN_DIFF_INPUT)
        return grad_fn(weights, diff, {**shared, **ex}, loss_target)

    if N_MICROBATCH == 1:
        loss, (grad_w, grad_x) = one_microbatch(per_example, given["loss_target"])
    else:
        def body(carry, xs):
            loss_sum, grad_sum = carry
            l_k, (gw_k, gx_k) = one_microbatch(xs[0], xs[1])
            with _jax.named_scope("update"):
                return (loss_sum + l_k, _jax.tree.map(_jnp.add, grad_sum, gw_k)), gx_k

        init = (_jnp.zeros((), _jnp.float32), _jax.tree.map(_jnp.zeros_like, weights))
        (loss, grad_w), grad_x = _jax.lax.scan(body, init, (per_example, given["loss_target"]))
    with _jax.named_scope("update"):
        delta_w, new_m, new_v = {}, {}, {}
        for n in TWIN_WEIGHTS:
            delta_w[n], new_m[n], new_v[n] = _adamw(weights[n], grad_w[n], given["m_" + n], given["v_" + n])
    return (loss, grad_x, *[grad_w[n] for n in TWIN_WEIGHTS], *[delta_w[n] for n in TWIN_WEIGHTS],
            *[new_m[n] for n in TWIN_WEIGHTS], *[new_v[n] for n in TWIN_WEIGHTS])
```

```python
import functools
import math

import numpy as np
import jax
import jax.numpy as jnp
from jax import lax
from jax.experimental import pallas as pl
from jax.experimental.pallas import tpu as pltpu

F32 = jnp.float32
BF16 = jnp.bfloat16

N_DEV = 8
D_MODEL = 1024
D_FF = 2816
FF_SHARD = D_FF // N_DEV
FF_PAD = 384
ATT_W = 512
ATT_HEADS = 8
DILATED = ((128, 1), (512, 4), (2048, 16))
BLK = 128
ML_W = 512
ML_HEADS = 4
ML_HD = 128
CHUNK = 128
CONV_K = 4
W_IN = 3592
W_IN_SHARD = W_IN // N_DEV
W_IN_MAIN = 3584
XA_HEADS = 4
XA_HD = 256
MEM_LEN = 256
REL_BUCKETS = 32
REL_MAX_DIST = 2048
ALPHA = 2.0 ** 0.25
LN_EPS = 1e-5
NEG = -1e30
ADAM_LR = 0.001
ADAM_B1 = 0.9
ADAM_B2 = 0.999
ADAM_EPS = 1e-08
ADAM_WD = 0.01
ADAM_STEP = 10
LANES = 128
VMEM_LIMIT = 48 * 1024 * 1024

NN = (((1,), (0,)), ((), ()))
NT = (((1,), (1,)), ((), ()))
TN = (((0,), (0,)), ((), ()))


def _dot(a, b, dims):
    return lax.dot_general(a, b, dims, preferred_element_type=F32)


def _params(*sem):
    return pltpu.CompilerParams(dimension_semantics=sem, vmem_limit_bytes=VMEM_LIMIT)


def _sigmoid(x):
    return 1.0 / (1.0 + jnp.exp(-x))


def _rowsum8(x):
    t, c = x.shape
    return jnp.sum(x.reshape(t // 8, 8, c), axis=0)


def _matmul(a, b, mode, name, *, out_dtype=F32, tm=512, tn=512, tk=512, add=None, add_scale=1.0):
    if mode == "nn":
        (m, k), (_, n) = a.shape, b.shape
    elif mode == "nt":
        (m, k), (n, _) = a.shape, b.shape
    else:
        (k, m), (_, n) = a.shape, b.shape
    tm, tn, tk = min(tm, m), min(tn, n), min(tk, k)
    nk = k // tk
    dims = {"nn": NN, "nt": NT, "tn": TN}[mode]
    if mode == "tn":
        a_spec = pl.BlockSpec((tk, tm), lambda i, j, kk: (kk, i))
    else:
        a_spec = pl.BlockSpec((tm, tk), lambda i, j, kk: (i, kk))
    if mode == "nt":
        b_spec = pl.BlockSpec((tn, tk), lambda i, j, kk: (j, kk))
    else:
        b_spec = pl.BlockSpec((tk, tn), lambda i, j, kk: (kk, j))
    o_spec = pl.BlockSpec((tm, tn), lambda i, j, kk: (i, j))
    has_add = add is not None

    def body(*refs):
        if has_add:
            a_ref, b_ref, add_ref, o_ref, acc_ref = refs
        else:
            a_ref, b_ref, o_ref, acc_ref = refs
        kk = pl.program_id(2)
        part = _dot(a_ref[...].astype(BF16), b_ref[...].astype(BF16), dims)

        @pl.when(kk == 0)
        def _():
            acc_ref[...] = part

        @pl.when(kk > 0)
        def _():
            acc_ref[...] += part

        @pl.when(kk == nk - 1)
        def _():
            r = acc_ref[...]
            if has_add:
                r = r + add_scale * add_ref[...]
            o_ref[...] = r.astype(out_dtype)

    in_specs = [a_spec, b_spec] + ([o_spec] if has_add else [])
    args = (a, b) + ((add,) if has_add else ())
    return pl.pallas_call(
        body, name=name, grid=(m // tm, n // tn, nk),
        in_specs=in_specs, out_specs=o_spec,
        out_shape=jax.ShapeDtypeStruct((m, n), out_dtype),
        scratch_shapes=[pltpu.VMEM((tm, tn), F32)],
        compiler_params=_params("parallel", "parallel", "arbitrary"),
    )(*args)


def _ln_fwd_math(u, g, b):
    mu = jnp.mean(u, axis=-1, keepdims=True)
    uc = u - mu
    var = jnp.mean(uc * uc, axis=-1, keepdims=True)
    return uc * lax.rsqrt(var + LN_EPS) * g + b


def _ln_bwd_math(dy, u, g):
    mu = jnp.mean(u, axis=-1, keepdims=True)
    uc = u - mu
    var = jnp.mean(uc * uc, axis=-1, keepdims=True)
    rstd = lax.rsqrt(var + LN_EPS)
    xhat = uc * rstd
    dxh = dy * g
    m1 = jnp.mean(dxh, axis=-1, keepdims=True)
    m2 = jnp.mean(dxh * xhat, axis=-1, keepdims=True)
    return rstd * (dxh - m1 - xhat * m2), xhat


def _resid_ln(x, f, g, b, name, tm=512):
    s, d = x.shape

    def body(x_ref, f_ref, g_ref, b_ref, u_ref, y_ref):
        u = ALPHA * x_ref[...] + f_ref[...]
        u_ref[...] = u
        y_ref[...] = _ln_fwd_math(u, g_ref[...], b_ref[...])

    row = pl.BlockSpec((tm, d), lambda i: (i, 0))
    vec = pl.BlockSpec((1, d), lambda i: (0, 0))
    return pl.pallas_call(
        body, name=name, grid=(s // tm,),
        in_specs=[row, row, vec, vec], out_specs=[row, row],
        out_shape=[jax.ShapeDtypeStruct((s, d), F32)] * 2,
        compiler_params=_params("parallel"),
    )(x, f, g, b)


def _ln_bwd(dy, u, g, name, tm=512):
    s, d = dy.shape
    nt = s // tm

    def body(dy_ref, u_ref, g_ref, du_ref, dg_ref, db_ref, g8, b8):
        i = pl.program_id(0)
        dy_ = dy_ref[...]
        du, xhat = _ln_bwd_math(dy_, u_ref[...], g_ref[...])
        du_ref[...] = du

        @pl.when(i == 0)
        def _():
            g8[...] = jnp.zeros_like(g8)
            b8[...] = jnp.zeros_like(b8)

        g8[...] += _rowsum8(dy_ * xhat)
        b8[...] += _rowsum8(dy_)

        @pl.when(i == nt - 1)
        def _():
            dg_ref[...] = jnp.sum(g8[...], axis=0, keepdims=True)
            db_ref[...] = jnp.sum(b8[...], axis=0, keepdims=True)

    row = pl.BlockSpec((tm, d), lambda i: (i, 0))
    vec = pl.BlockSpec((1, d), lambda i: (0, 0))
    return pl.pallas_call(
        body, name=name, grid=(nt,),
        in_specs=[row, row, vec], out_specs=[row, vec, vec],
        out_shape=[jax.ShapeDtypeStruct((s, d), F32), jax.ShapeDtypeStruct((1, d), F32),
                   jax.ShapeDtypeStruct((1, d), F32)],
        scratch_shapes=[pltpu.VMEM((8, d), F32), pltpu.VMEM((8, d), F32)],
        compiler_params=_params("arbitrary"),
    )(dy, u, g)


def _ffn_fwd(x, wg, wu, wd, g, b, layer, name, tm=512):
    s, d = x.shape

    def body(x_ref, wg_ref, wu_ref, wd_ref, g_ref, b_ref, u_ref, y_ref, xb, acc):
        k = pl.program_id(1)

        @pl.when(k == 0)
        def _():
            xb[...] = x_ref[...].astype(BF16)

        a = _dot(xb[...], wg_ref[...], NN)
        bb = _dot(xb[...], wu_ref[...], NN)
        h = (a * _sigmoid(a) * bb).astype(BF16)
        part = _dot(h, wd_ref[...], NN)

        @pl.when(k == 0)
        def _():
            acc[...] = part

        @pl.when(k > 0)
        def _():
            acc[...] += part

        @pl.when(k == N_DEV - 1)
        def _():
            u = ALPHA * x_ref[...] + 0.5 * acc[...]
            u_ref[...] = u
            y_ref[...] = _ln_fwd_math(u, g_ref[...], b_ref[...])

    row = pl.BlockSpec((tm, d), lambda i, k: (i, 0))
    vec = pl.BlockSpec((1, d), lambda i, k: (0, 0))
    w_in = pl.BlockSpec((None, None, d, FF_PAD), lambda i, k: (k, layer, 0, 0))
    w_dn = pl.BlockSpec((None, None, FF_PAD, d), lambda i, k: (k, layer, 0, 0))
    return pl.pallas_call(
        body, name=name, grid=(s // tm, N_DEV),
        in_specs=[row, w_in, w_in, w_dn, vec, vec], out_specs=[row, row],
        out_shape=[jax.ShapeDtypeStruct((s, d), F32)] * 2,
        scratch_shapes=[pltpu.VMEM((tm, d), BF16), pltpu.VMEM((tm, d), F32)],
        compiler_params=_params("parallel", "arbitrary"),
    )(x, wg, wu, wd, g, b)


def _ffn_bwd_x(dy, u, x, wg, wu, wd, g, layer, name, tm=512):
    s, d = x.shape
    nt = s // tm
    ffp = N_DEV * FF_PAD

    def body(dy_ref, u_ref, x_ref, wg_ref, wu_ref, wd_ref, g_ref,
             dx_ref, df_ref, da_ref, db_ref, h_ref, dg_ref, dbl_ref,
             xb, dfb, du_s, acc, g8, b8):
        i = pl.program_id(0)
        k = pl.program_id(1)

        @pl.when(k == 0)
        def _():
            dy_ = dy_ref[...]
            du, xhat = _ln_bwd_math(dy_, u_ref[...], g_ref[...])
            du_s[...] = du
            dfb[...] = (0.5 * du).astype(BF16)
            df_ref[...] = dfb[...]
            xb[...] = x_ref[...].astype(BF16)

            @pl.when(i == 0)
            def _():
                g8[...] = jnp.zeros_like(g8)
                b8[...] = jnp.zeros_like(b8)

            g8[...] += _rowsum8(dy_ * xhat)
            b8[...] += _rowsum8(dy_)

        a = _dot(xb[...], wg_ref[...], NN)
        bb = _dot(xb[...], wu_ref[...], NN)
        sig = _sigmoid(a)
        sa = a * sig
        h_ref[...] = (sa * bb).astype(BF16)
        dh = _dot(dfb[...], wd_ref[...], NT)
        da = (dh * bb * (sig * (1.0 + a * (1.0 - sig)))).astype(BF16)
        db = (dh * sa).astype(BF16)
        da_ref[...] = da
        db_ref[...] = db
        part = _dot(da, wg_ref[...], NT) + _dot(db, wu_ref[...], NT)

        @pl.when(k == 0)
        def _():
            acc[...] = part

        @pl.when(k > 0)
        def _():
            acc[...] += part

        @pl.when(k == N_DEV - 1)
        def _():
            dx_ref[...] = ALPHA * du_s[...] + acc[...]

        @pl.when((k == N_DEV - 1) & (i == nt - 1))
        def _():
            dg_ref[...] = jnp.sum(g8[...], axis=0, keepdims=True)
            dbl_ref[...] = jnp.sum(b8[...], axis=0, keepdims=True)

    row = pl.BlockSpec((tm, d), lambda i, k: (i, 0))
    vec = pl.BlockSpec((1, d), lambda i, k: (0, 0))
    w_in = pl.BlockSpec((None, None, d, FF_PAD), lambda i, k: (k, layer, 0, 0))
    w_dn = pl.BlockSpec((None, None, FF_PAD, d), lambda i, k: (k, layer, 0, 0))
    hid = pl.BlockSpec((tm, FF_PAD), lambda i, k: (i, k))
    return pl.pallas_call(
        body, name=name, grid=(nt, N_DEV),
        in_specs=[row, row, row, w_in, w_in, w_dn, vec],
        out_specs=[row, row, hid, hid, hid, vec, vec],
        out_shape=[jax.ShapeDtypeStruct((s, d), F32), jax.ShapeDtypeStruct((s, d), BF16),
                   jax.ShapeDtypeStruct((s, ffp), BF16), jax.ShapeDtypeStruct((s, ffp), BF16),
                   jax.ShapeDtypeStruct((s, ffp), BF16),
                   jax.ShapeDtypeStruct((1, d), F32), jax.ShapeDtypeStruct((1, d), F32)],
        scratch_shapes=[pltpu.VMEM((tm, d), BF16), pltpu.VMEM((tm, d), BF16), pltpu.VMEM((tm, d), F32),
                        pltpu.VMEM((tm, d), F32), pltpu.VMEM((8, d), F32), pltpu.VMEM((8, d), F32)],
        compiler_params=_params("arbitrary", "arbitrary"),
    )(dy, u, x, wg, wu, wd, g)


def _ffn_bwd_w(x, df, da, db, h, name, tm=512):
    s, d = x.shape
    nt = s // tm

    def body(x_ref, df_ref, da_ref, db_ref, h_ref, dwg_ref, dwu_ref, dwd_ref):
        i = pl.program_id(1)
        xb = x_ref[...].astype(BF16)
        pg = _dot(xb, da_ref[...], TN)
        pu = _dot(xb, db_ref[...], TN)
        pd = _dot(h_ref[...], df_ref[...], TN)

        @pl.when(i == 0)
        def _():
            dwg_ref[...] = pg
            dwu_ref[...] = pu
            dwd_ref[...] = pd

        @pl.when(i > 0)
        def _():
            dwg_ref[...] += pg
            dwu_ref[...] += pu
            dwd_ref[...] += pd

    row = pl.BlockSpec((tm, d), lambda k, i: (i, 0))
    hid = pl.BlockSpec((tm, FF_PAD), lambda k, i: (i, k))
    w_in = pl.BlockSpec((None, d, FF_PAD), lambda k, i: (k, 0, 0))
    w_dn = pl.BlockSpec((None, FF_PAD, d), lambda k, i: (k, 0, 0))
    return pl.pallas_call(
        body, name=name, grid=(N_DEV, nt),
        in_specs=[row, row, hid, hid, hid], out_specs=[w_in, w_in, w_dn],
        out_shape=[jax.ShapeDtypeStruct((N_DEV, d, FF_PAD), F32), jax.ShapeDtypeStruct((N_DEV, d, FF_PAD), F32),
                   jax.ShapeDtypeStruct((N_DEV, FF_PAD, d), F32)],
        compiler_params=_params("parallel", "arbitrary"),
    )(x, df, da, db, h)


def _bucket_tables():
    qi = np.arange(BLK)[:, None]
    ki = np.arange(2 * BLK)[None, :]
    off = qi + BLK - ki
    out = []
    for window, dil in DILATED:
        n_keys = window // dil
        dist = dil * np.clip(off, 0, n_keys)
        exact = REL_BUCKETS // 2
        df = np.maximum(dist, 1).astype(np.float32)
        large = exact + (np.log(df / np.float32(exact)) / np.float32(math.log(REL_MAX_DIST / exact))
                         * np.float32(REL_BUCKETS - exact)).astype(np.int32)
        large = np.minimum(large, REL_BUCKETS - 1)
        bucket = np.where(dist < exact, dist, large).astype(np.int32)
        band = (off >= 0) & (off <= n_keys)
        out.append(np.where(band, bucket, -1))
    return np.stack(out).astype(np.int32)


def _bias_fwd(rel_bias, buckets, name="bias_fwd"):
    def body(tbl_ref, bkt_ref, out_ref):
        bkt = bkt_ref[...]
        for h in range(ATT_HEADS):
            acc = jnp.full((BLK, 2 * BLK), NEG, F32)
            for bb in range(REL_BUCKETS):
                acc = jnp.where(bkt == bb, tbl_ref[bb, h], acc)
            out_ref[h] = acc

    nbr = len(DILATED)
    return pl.pallas_call(
        body, name=name, grid=(nbr,),
        in_specs=[pl.BlockSpec(memory_space=pltpu.SMEM),
                  pl.BlockSpec((None, BLK, 2 * BLK), lambda r: (r, 0, 0))],
        out_specs=pl.BlockSpec((None, ATT_HEADS, BLK, 2 * BLK), lambda r: (r, 0, 0, 0)),
        out_shape=jax.ShapeDtypeStruct((nbr, ATT_HEADS, BLK, 2 * BLK), F32),
        compiler_params=_params("parallel"),
    )(rel_bias, buckets)


def _bias_bwd(dbias, buckets, name="bias_bwd"):
    nbr = len(DILATED)

    def body(db_ref, bkt_ref, out_ref):
        r = pl.program_id(0)

        @pl.when(r == 0)
        def _():
            out_ref[...] = jnp.zeros_like(out_ref)

        bkt = bkt_ref[...]
        rowi = lax.broadcasted_iota(jnp.int32, (REL_BUCKETS, LANES), 0)
        coli = lax.broadcasted_iota(jnp.int32, (REL_BUCKETS, LANES), 1)
        acc = jnp.zeros((REL_BUCKETS, LANES), F32)
        for h in range(ATT_HEADS):
            x = db_ref[h]
            for bb in range(REL_BUCKETS):
                part = jnp.sum(jnp.where(bkt == bb, x, 0.0), axis=0, keepdims=True)
                tot = jnp.sum(part, axis=1, keepdims=True)
                acc = acc + jnp.where((rowi == bb) & (coli == h), tot, 0.0)
        out_ref[...] += acc

    return pl.pallas_call(
        body, name=name, grid=(nbr,),
        in_specs=[pl.BlockSpec((None, ATT_HEADS, BLK, 2 * BLK), lambda r: (r, 0, 0, 0)),
                  pl.BlockSpec((None, BLK, 2 * BLK), lambda r: (r, 0, 0))],
        out_specs=pl.BlockSpec((REL_BUCKETS, LANES), lambda r: (0, 0)),
        out_shape=jax.ShapeDtypeStruct((REL_BUCKETS, LANES), F32),
        compiler_params=_params("arbitrary"),
    )(dbias, buckets)


def _att_scores(q_pair, k2, bias, first_ok, msk):
    qm = jnp.where(msk, q_pair, 0.0).astype(BF16)
    sc = _dot(qm, k2, NT) * (64 ** -0.5) + bias
    return jnp.where(first_ok, sc, NEG), qm


def _dil_specs(dil, nb, clamp):
    ncol = W_IN_MAIN // ATT_W

    def cur(col):
        return pl.BlockSpec((BLK, ATT_W), lambda r, n: (jnp.minimum(n, nb - 1) if clamp else n, r * ncol + col))

    def prev(col):
        return pl.BlockSpec(
            (BLK, ATT_W), lambda r, n: (jnp.maximum((jnp.minimum(n, nb - 1) if clamp else n) - 1, 0), r * ncol + col))

    return [cur(0), prev(1), cur(1), prev(2), cur(2)]


def _dil_fwd(proj, biasm, branch, name):
    s = proj.shape[0]
    dil = DILATED[branch][1]
    m = s // dil
    nb = m // BLK
    pv = proj.reshape(m, dil * W_IN_MAIN)

    def body(q_ref, kp_ref, kc_ref, vp_ref, vc_ref, bias_ref, o_ref, lse_ref):
        n = pl.program_id(1)
        lo = lax.broadcasted_iota(jnp.int32, (BLK, LANES), 1) < 64
        kidx = lax.broadcasted_iota(jnp.int32, (BLK, 2 * BLK), 1)
        first_ok = (n > 0) | (kidx >= BLK)
        for p in range(ATT_W // LANES):
            sl = slice(LANES * p, LANES * (p + 1))
            q_pair = q_ref[:, sl]
            k2 = jnp.concatenate([kp_ref[:, sl], kc_ref[:, sl]], axis=0).astype(BF16)
            v2 = jnp.concatenate([vp_ref[:, sl], vc_ref[:, sl]], axis=0).astype(BF16)
            outs, lses = [], []
            for hh in range(2):
                msk = lo if hh == 0 else jnp.logical_not(lo)
                sc, _ = _att_scores(q_pair, k2, bias_ref[2 * p + hh], first_ok, msk)
                mx = jnp.max(sc, axis=1, keepdims=True)
                pe = jnp.exp(sc - mx)
                l = jnp.sum(pe, axis=1, keepdims=True)
                outs.append(_dot(pe.astype(BF16), v2, NN) / l)
                lses.append(jnp.broadcast_to(mx + jnp.log(l), (BLK, LANES)))
            o_ref[:, sl] = jnp.where(lo, outs[0], outs[1])
            lse_ref[:, sl] = jnp.where(lo, lses[0], lses[1])

    out_spec = pl.BlockSpec((BLK, ATT_W), lambda r, n: (n, r))
    o, lse = pl.pallas_call(
        body, name=name, grid=(dil, nb),
        in_specs=_dil_specs(dil, nb, False) + [pl.BlockSpec((None, ATT_HEADS, BLK, 2 * BLK), lambda r, n: (branch, 0, 0, 0))],
        out_specs=[out_spec, out_spec],
        out_shape=[jax.ShapeDtypeStruct((m, dil * ATT_W), F32)] * 2,
        compiler_params=_params("parallel", "arbitrary"),
    )(pv, pv, pv, pv, pv, biasm)
    return o.reshape(s, ATT_W), lse.reshape(s, ATT_W)


def _dil_combine(os_, lses, name="dil_combine", tm=512):
    s = os_[0].shape[0]

    def body(o0, o1, o2, l0, l1, l2, att_ref, lse_ref):
        a, b, c = l0[...], l1[...], l2[...]
        mx = jnp.maximum(jnp.maximum(a, b), c)
        ea, eb, ec = jnp.exp(a - mx), jnp.exp(b - mx), jnp.exp(c - mx)
        tot = ea + eb + ec
        att_ref[...] = (ea * o0[...] + eb * o1[...] + ec * o2[...]) / tot
        lse_ref[...] = mx + jnp.log(tot)

    row = pl.BlockSpec((tm, ATT_W), lambda i: (i, 0))
    return pl.pallas_call(
        body, name=name, grid=(s // tm,),
        in_specs=[row] * 6, out_specs=[row, row],
        out_shape=[jax.ShapeDtypeStruct((s, ATT_W), F32)] * 2,
        compiler_params=_params("parallel"),
    )(*os_, *lses)


def _dil_bwd(proj, biasm, lse, att, datt, acc, branch, name):
    s = proj.shape[0]
    dil = DILATED[branch][1]
    m = s // dil
    nb = m // BLK
    pv = proj.reshape(m, dil * W_IN_MAIN)
    has_acc = acc is not None
    view = lambda t: t.reshape(m, dil * ATT_W)

    def body(*refs):
        q_ref, kp_ref, kc_ref, vp_ref, vc_ref, bias_ref, lse_ref, att_ref, datt_ref = refs[:9]
        refs = refs[9:]
        if has_acc:
            aq_ref, ak_ref, av_ref = refs[:3]
            refs = refs[3:]
        dq_ref, dk_ref, dv_ref, dbias_ref, kcar, vcar = refs
        r = pl.program_id(0)
        n = pl.program_id(1)

        @pl.when((r == 0) & (n == 0))
        def _():
            dbias_ref[...] = jnp.zeros_like(dbias_ref)

        @pl.when(n == 0)
        def _():
            kcar[...] = jnp.zeros_like(kcar)
            vcar[...] = jnp.zeros_like(vcar)

        @pl.when(n < nb)
        def _():
            lo = lax.broadcasted_iota(jnp.int32, (BLK, LANES), 1) < 64
            kidx = lax.broadcasted_iota(jnp.int32, (BLK, 2 * BLK), 1)
            first_ok = (n > 0) | (kidx >= BLK)
            for p in range(ATT_W // LANES):
                sl = slice(LANES * p, LANES * (p + 1))
                q_pair = q_ref[:, sl]
                k2 = jnp.concatenate([kp_ref[:, sl], kc_ref[:, sl]], axis=0).astype(BF16)
                v2 = jnp.concatenate([vp_ref[:, sl], vc_ref[:, sl]], axis=0).astype(BF16)
                lse_pair = lse_ref[:, sl]
                dd_pair = datt_ref[:, sl] * att_ref[:, sl]
                dat_pair = datt_ref[:, sl]
                dqs, dk2, dv2 = [], None, None
                for hh in range(2):
                    msk = lo if hh == 0 else jnp.logical_not(lo)
                    sc, qm = _att_scores(q_pair, k2, bias_ref[2 * p + hh], first_ok, msk)
                    lse_h = jnp.max(jnp.where(msk, lse_pair, -jnp.inf), axis=1, keepdims=True)
                    pr = jnp.exp(sc - lse_h)
                    dsum = jnp.sum(jnp.where(msk, dd_pair, 0.0), axis=1, keepdims=True)
                    dom = jnp.where(msk, dat_pair, 0.0).astype(BF16)
                    dp = _dot(dom, v2, NT)
                    ds = pr * (dp - dsum)
                    dbias_ref[2 * p + hh] += ds
                    dsb = (ds * (64 ** -0.5)).astype(BF16)
                    dqs.append(_dot(dsb, k2, NN))
                    dkh = _dot(dsb, qm, TN)
                    dvh = _dot(pr.astype(BF16), dom, TN)
                    dk2 = dkh if dk2 is None else dk2 + dkh
                    dv2 = dvh if dv2 is None else dv2 + dvh
                dq = jnp.where(lo, dqs[0], dqs[1])
                dkp = kcar[:, sl] + dk2[:BLK]
                dvp = vcar[:, sl] + dv2[:BLK]
                if has_acc:
                    dq = dq + aq_ref[:, sl]
                    dkp = dkp + ak_ref[:, sl]
                    dvp = dvp + av_ref[:, sl]
                dq_ref[:, sl] = dq
                dk_ref[:, sl] = dkp
                dv_ref[:, sl] = dvp
                kcar[:, sl] = dk2[BLK:]
                vcar[:, sl] = dv2[BLK:]

        @pl.when(n == nb)
        def _():
            dkp = kcar[...]
            dvp = vcar[...]
            if has_acc:
                dkp = dkp + ak_ref[...]
                dvp = dvp + av_ref[...]
            dk_ref[...] = dkp
            dv_ref[...] = dvp

    cur = pl.BlockSpec((BLK, ATT_W), lambda r, n: (jnp.minimum(n, nb - 1), r))
    prev = pl.BlockSpec((BLK, ATT_W), lambda r, n: (jnp.maximum(n - 1, 0), r))
    in_specs = _dil_specs(dil, nb, True) + [
        pl.BlockSpec((None, ATT_HEADS, BLK, 2 * BLK), lambda r, n: (branch, 0, 0, 0)), cur, cur, cur]
    args = [pv, pv, pv, pv, pv, biasm, view(lse), view(att), view(datt)]
    if has_acc:
        in_specs += [cur, prev, prev]
        args += [view(t) for t in acc]
    dq, dk, dv, dbias = pl.pallas_call(
        body, name=name, grid=(dil, nb + 1),
        in_specs=in_specs,
        out_specs=[cur, prev, prev, pl.BlockSpec((ATT_HEADS, BLK, 2 * BLK), lambda r, n: (0, 0, 0))],
        out_shape=[jax.ShapeDtypeStruct((m, dil * ATT_W), F32)] * 3
        + [jax.ShapeDtypeStruct((ATT_HEADS, BLK, 2 * BLK), F32)],
        scratch_shapes=[pltpu.VMEM((BLK, ATT_W), F32), pltpu.VMEM((BLK, ATT_W), F32)],
        compiler_params=_params("arbitrary", "arbitrary"),
    )(*args)
    return (dq.reshape(s, ATT_W), dk.reshape(s, ATT_W), dv.reshape(s, ATT_W)), dbias


QK_COL0 = (3 * ATT_W) // ATT_W


def _conv_shifted(prev, cur, j, row):
    sh = CONV_K - 1 - j
    if sh == 0:
        return cur
    return jnp.where(row < sh, pltpu.roll(prev, sh, 0), pltpu.roll(cur, sh, 0))


def _conv_z(prev, cur, w_ref, b_ref, row):
    z = b_ref[...] + cur * w_ref[CONV_K - 1:CONV_K, :]
    for j in range(CONV_K - 1):
        z = z + _conv_shifted(prev, cur, j, row) * w_ref[j:j + 1, :]
    return z


def _conv_fwd(proj, conv_w, conv_b, name="conv_fwd", tm=512):
    s = proj.shape[0]
    w = ATT_W

    def body(prev_ref, cur_ref, w_ref, b_ref, o_ref):
        i = pl.program_id(1)
        row = lax.broadcasted_iota(jnp.int32, (tm, w), 0)
        prev = jnp.where(i > 0, prev_ref[...], 0.0)
        z = _conv_z(prev, cur_ref[...], w_ref, b_ref, row)
        o_ref[...] = z * _sigmoid(z)

    return pl.pallas_call(
        body, name=name, grid=(2, s // tm),
        in_specs=[pl.BlockSpec((tm, w), lambda j, i: (jnp.maximum(i - 1, 0), QK_COL0 + j)),
                  pl.BlockSpec((tm, w), lambda j, i: (i, QK_COL0 + j)),
                  pl.BlockSpec((CONV_K, w), lambda j, i: (0, j)),
                  pl.BlockSpec((1, w), lambda j, i: (0, j))],
        out_specs=pl.BlockSpec((tm, w), lambda j, i: (i, j)),
        out_shape=jax.ShapeDtypeStruct((s, 2 * ML_W), F32),
        compiler_params=_params("parallel", "parallel"),
    )(proj, proj, conv_w, conv_b)


def _conv_bwd(proj, dqk, conv_w, conv_b, name="conv_bwd", tm=512):
    s = proj.shape[0]
    w = ATT_W
    nt = s // tm

    def body(xp_ref, xc_ref, xn_ref, dc_ref, dn_ref, w_ref, b_ref, dx_ref, dw_ref, db_ref):
        i = pl.program_id(1)
        row = lax.broadcasted_iota(jnp.int32, (tm, w), 0)
        prev = jnp.where(i > 0, xp_ref[...], 0.0)
        cur = xc_ref[...]

        def dz_of(pv, cv, dy):
            z = _conv_z(pv, cv, w_ref, b_ref, row)
            sig = _sigmoid(z)
            return dy * (sig * (1.0 + z * (1.0 - sig)))

        dzc = dz_of(prev, cur, dc_ref[...])
        dzn = jnp.where(i < nt - 1, dz_of(cur, xn_ref[...], dn_ref[...]), 0.0)
        dx = dzc * w_ref[CONV_K - 1:CONV_K, :]
        for j in range(CONV_K - 1):
            sh = CONV_K - 1 - j
            up = jnp.where(row >= tm - sh, pltpu.roll(dzn, tm - sh, 0), pltpu.roll(dzc, tm - sh, 0))
            dx = dx + up * w_ref[j:j + 1, :]
        dx_ref[...] = dx

        @pl.when(i == 0)
        def _():
            dw_ref[...] = jnp.zeros_like(dw_ref)
            db_ref[...] = jnp.zeros_like(db_ref)

        for j in range(CONV_K):
            dw_ref[j:j + 1, :] += jnp.sum(dzc * _conv_shifted(prev, cur, j, row), axis=0, keepdims=True)
        db_ref[...] += jnp.sum(dzc, axis=0, keepdims=True)

    xs = lambda f: pl.BlockSpec((tm, w), lambda j, i: (f(i), QK_COL0 + j))
    ds = lambda f: pl.BlockSpec((tm, w), lambda j, i: (f(i), j))
    return pl.pallas_call(
        body, name=name, grid=(2, nt),
        in_specs=[xs(lambda i: jnp.maximum(i - 1, 0)), xs(lambda i: i), xs(lambda i: jnp.minimum(i + 1, nt - 1)),
                  ds(lambda i: i), ds(lambda i: jnp.minimum(i + 1, nt - 1)),
                  pl.BlockSpec((CONV_K, w), lambda j, i: (0, j)), pl.BlockSpec((1, w), lambda j, i: (0, j))],
        out_specs=[ds(lambda i: i), pl.BlockSpec((CONV_K, w), lambda j, i: (0, j)),
                   pl.BlockSpec((1, w), lambda j, i: (0, j))],
        out_shape=[jax.ShapeDtypeStruct((s, 2 * ML_W), F32), jax.ShapeDtypeStruct((CONV_K, 2 * ML_W), F32),
                   jax.ShapeDtypeStruct((1, 2 * ML_W), F32)],
        compiler_params=_params("parallel", "arbitrary"),
    )(proj, proj, proj, dqk, dqk, conv_w, conv_b)


def _bf16_mm(dims_fwd):
    @jax.custom_vjp
    def mm(a, b):
        return _dot(a.astype(BF16), b.astype(BF16), dims_fwd)

    def fwd(a, b):
        return mm(a, b), (a, b)

    def bwd(res, g):
        a, b = res
        if dims_fwd is NN:
            return _mm_nt(g, b), _mm_tn(a, g)
        if dims_fwd is NT:
            return _mm_nn(g, b), _mm_tn(g, a)
        return _mm_nt(b, g), _mm_nn(a, g)

    mm.defvjp(fwd, bwd)
    return mm


_mm_nn = _bf16_mm(NN)
_mm_nt = _bf16_mm(NT)
_mm_tn = _bf16_mm(TN)


def _tri(lower):
    r = lax.broadcasted_iota(jnp.int32, (CHUNK, CHUNK), 0)
    c = lax.broadcasted_iota(jnp.int32, (CHUNK, CHUNK), 1)
    return ((r >= c) if lower else (r <= c)).astype(F32)


@jax.custom_vjp
def _cumsum_rows(x):
    return lax.dot_general(_tri(True), x, NN, precision=lax.Precision.HIGHEST, preferred_element_type=F32)


def _cumsum_fwd(x):
    return _cumsum_rows(x), None


def _cumsum_bwd(_, g):
    return (lax.dot_general(_tri(False), g, NN, precision=lax.Precision.HIGHEST, preferred_element_type=F32),)


_cumsum_rows.defvjp(_cumsum_fwd, _cumsum_bwd)


def _abs(x):
    return jnp.where(x >= 0, x, -x)


def _log_sigmoid(x):
    return jnp.minimum(x, 0.0) - jnp.log(1.0 + jnp.exp(-_abs(x)))


def _pick_col(x, lane):
    sel = lax.broadcasted_iota(jnp.int32, x.shape, 1) == lane
    return jnp.sum(jnp.where(sel, x, 0.0), axis=1, keepdims=True)


def _pick_row(x, r):
    sel = lax.broadcasted_iota(jnp.int32, x.shape, 0) == r
    return jnp.sum(jnp.where(sel, x, 0.0), axis=0, keepdims=True)


def _mlstm_chunk(qs, ks, vs, oms, gates, gate_bias, mlg, cs, ns, ms):
    gb = gates + gate_bias
    cum = _cumsum_rows(_log_sigmoid(gb))
    gbt = gb.T
    cumt = cum.T
    causal = lax.broadcasted_iota(jnp.int32, (CHUNK, CHUNK), 0) >= lax.broadcasted_iota(jnp.int32, (CHUNK, CHUNK), 1)
    ys, c_out, n_out, m_out = [], [], [], []
    for h in range(ML_HEADS):
        q, v, om, c, n, m = qs[h], vs[h], oms[h], cs[h], ns[h], ms[h]
        k = ks[h] * (ML_HD ** -0.5)
        ig_col = _pick_col(gb, h)
        ig_row = _pick_row(gbt, h)
        b_col = _pick_col(cum, ML_HEADS + h)
        b_row = _pick_row(cumt, ML_HEADS + h)
        g = _pick_row(b_col, CHUNK - 1)
        a = g - b_col + ig_col
        m_loc = jnp.max(a, axis=0, keepdims=True)
        wa = jnp.exp(a - m_loc)
        c_loc = _mm_tn(wa * v, k)
        n_loc = jnp.sum(wa * k, axis=0, keepdims=True)
        m_new = jnp.maximum(g + m, m_loc)
        sp = jnp.exp(g + m - m_new)
        sl = jnp.exp(m_loc - m_new)
        c_out.append(sp * c + sl * c_loc)
        n_out.append(sp * n + sl * n_loc)
        m_out.append(m_new)
        d_log = jnp.where(causal, b_col - b_row + ig_row, -jnp.inf)
        e_log = b_col + m
        m_t = jnp.maximum(e_log, jnp.max(d_log, axis=1, keepdims=True))
        d_w = jnp.exp(d_log - m_t)
        e_w = jnp.exp(e_log - m_t)
        s_qk = _mm_nt(q, k) * d_w
        num = e_w * _mm_nt(q, c) + _mm_nn(s_qk, v)
        den = e_w * jnp.sum(q * n, axis=1, keepdims=True) + jnp.sum(s_qk, axis=1, keepdims=True)
        hh = num / jnp.maximum(_abs(den), jnp.exp(-m_t))
        hg = _sigmoid(om) * hh
        mu = jnp.mean(hg, axis=1, keepdims=True)
        hc = hg - mu
        var = jnp.mean(hc * hc, axis=1, keepdims=True)
        ys.append(hc * lax.rsqrt(var + LN_EPS) * mlg[h])
    return ys, c_out, n_out, m_out


V_COL = 5
O_COL = 6


def _mlstm_fwd(qk, proj, gates, gate_bias, mlg, name="mlstm_fwd"):
    s = qk.shape[0]
    nc = s // CHUNK

    def body(q_ref, k_ref, v_ref, o_ref, g_ref, gb_ref, mlg_ref, y_ref, cp_ref, np_ref, mp_ref, c_s, n_s, m_s):
        ci = pl.program_id(0)

        @pl.when(ci == 0)
        def _():
            c_s[...] = jnp.zeros_like(c_s)
            n_s[...] = jnp.zeros_like(n_s)
            m_s[...] = jnp.zeros_like(m_s)

        cp_ref[...] = c_s[...]
        np_ref[...] = n_s[...]
        mp_ref[...] = m_s[...]
        hs = lambda ref: [ref[:, LANES * h:LANES * (h + 1)] for h in range(ML_HEADS)]
        ys, c_new, n_new, m_new = _mlstm_chunk(
            hs(q_ref), hs(k_ref), hs(v_ref), hs(o_ref), g_ref[...], gb_ref[...], hs(mlg_ref),
            [c_s[h] for h in range(ML_HEADS)], [n_s[h:h + 1, :] for h in range(ML_HEADS)],
            [m_s[h:h + 1, 0:1] for h in range(ML_HEADS)])
        for h in range(ML_HEADS):
            y_ref[:, LANES * h:LANES * (h + 1)] = ys[h]
            c_s[h] = c_new[h]
            n_s[h:h + 1, :] = n_new[h]
            m_s[h:h + 1, :] = jnp.broadcast_to(m_new[h], (1, LANES))

    blk = lambda col: pl.BlockSpec((CHUNK, ML_W), lambda ci: (ci, col))
    vec = lambda w: pl.BlockSpec((1, w), lambda ci: (0, 0))
    return pl.pallas_call(
        body, name=name, grid=(nc,),
        in_specs=[blk(0), blk(1), blk(V_COL), blk(O_COL), pl.BlockSpec((CHUNK, LANES), lambda ci: (ci, 0)),
                  vec(LANES), vec(ML_W)],
        out_specs=[blk(0), pl.BlockSpec((None, ML_HEADS, ML_HD, ML_HD), lambda ci: (ci, 0, 0, 0)),
                   pl.BlockSpec((None, 8, LANES), lambda ci: (ci, 0, 0)),
                   pl.BlockSpec((None, 8, LANES), lambda ci: (ci, 0, 0))],
        out_shape=[jax.ShapeDtypeStruct((s, ML_W), F32), jax.ShapeDtypeStruct((nc, ML_HEADS, ML_HD, ML_HD), F32),
                   jax.ShapeDtypeStruct((nc, 8, LANES), F32), jax.ShapeDtypeStruct((nc, 8, LANES), F32)],
        scratch_shapes=[pltpu.VMEM((ML_HEADS, ML_HD, ML_HD), F32), pltpu.VMEM((8, LANES), F32),
                        pltpu.VMEM((8, LANES), F32)],
        compiler_params=_params("arbitrary"),
    )(qk, qk, proj, proj, gates, gate_bias, mlg)


def _mlstm_bwd(qk, proj, gates, gate_bias, mlg, cprev, nprev, mprev, dy, name="mlstm_bwd"):
    s = qk.shape[0]
    nc = s // CHUNK

    def body(q_ref, k_ref, v_ref, o_ref, g_ref, gb_ref, mlg_ref, cp_ref, np_ref, mp_ref, dy_ref,
             dqk_ref, dv_ref, do_ref, dg_ref, dgb_ref, dmlg_ref, dc_s, dn_s, dm_s, gb8, mg8):
        ci = pl.program_id(0)

        @pl.when(ci == 0)
        def _():
            dc_s[...] = jnp.zeros_like(dc_s)
            dn_s[...] = jnp.zeros_like(dn_s)
            dm_s[...] = jnp.zeros_like(dm_s)
            gb8[...] = jnp.zeros_like(gb8)
            mg8[...] = jnp.zeros_like(mg8)

        hs = lambda ref: [ref[:, LANES * h:LANES * (h + 1)] for h in range(ML_HEADS)]
        prim = (hs(q_ref), hs(k_ref), hs(v_ref), hs(o_ref), g_ref[...], gb_ref[...], hs(mlg_ref),
                [cp_ref[h] for h in range(ML_HEADS)], [np_ref[h:h + 1, :] for h in range(ML_HEADS)],
                [mp_ref[h:h + 1, 0:1] for h in range(ML_HEADS)])
        _, vjp = jax.vjp(_mlstm_chunk, *prim)
        cot = (hs(dy_ref), [dc_s[h] for h in range(ML_HEADS)], [dn_s[h:h + 1, :] for h in range(ML_HEADS)],
               [dm_s[h:h + 1, 0:1] for h in range(ML_HEADS)])
        dqs, dks, dvs, dos, dg, dgb, dmlg, dcs, dns, dms = vjp(cot)
        dg_ref[...] = dg
        gb8[0:1, :] += dgb
        for h in range(ML_HEADS):
            sl = slice(LANES * h, LANES * (h + 1))
            dqk_ref[:, sl] = dqs[h]
            dqk_ref[:, ML_W + LANES * h:ML_W + LANES * (h + 1)] = dks[h]
            dv_ref[:, sl] = dvs[h]
            do_ref[:, sl] = dos[h]
            mg8[0:1, sl] += dmlg[h]
            dc_s[h] = dcs[h]
            dn_s[h:h + 1, :] = dns[h]
            dm_s[h:h + 1, :] = jnp.broadcast_to(dms[h], (1, LANES))

        @pl.when(ci == nc - 1)
        def _():
            dgb_ref[...] = gb8[0:1, :]
            dmlg_ref[...] = mg8[0:1, :]

    rev = lambda ci: nc - 1 - ci
    blk = lambda col: pl.BlockSpec((CHUNK, ML_W), lambda ci: (rev(ci), col))
    vec = lambda w: pl.BlockSpec((1, w), lambda ci: (0, 0))
    st8 = pl.BlockSpec((None, 8, LANES), lambda ci: (rev(ci), 0, 0))
    gsp = pl.BlockSpec((CHUNK, LANES), lambda ci: (rev(ci), 0))
    outs = pl.pallas_call(
        body, name=name, grid=(nc,),
        in_specs=[blk(0), blk(1), blk(V_COL), blk(O_COL), gsp, vec(LANES), vec(ML_W),
                  pl.BlockSpec((None, ML_HEADS, ML_HD, ML_HD), lambda ci: (rev(ci), 0, 0, 0)), st8, st8, blk(1)],
        out_specs=[pl.BlockSpec((CHUNK, 2 * ML_W), lambda ci: (rev(ci), 0)), blk(0), blk(0), gsp, vec(LANES), vec(ML_W)],
        out_shape=[jax.ShapeDtypeStruct((s, 2 * ML_W), F32),
                   jax.ShapeDtypeStruct((s, ML_W), F32), jax.ShapeDtypeStruct((s, ML_W), F32),
                   jax.ShapeDtypeStruct((s, LANES), F32), jax.ShapeDtypeStruct((1, LANES), F32),
                   jax.ShapeDtypeStruct((1, ML_W), F32)],
        scratch_shapes=[pltpu.VMEM((ML_HEADS, ML_HD, ML_HD), F32), pltpu.VMEM((8, LANES), F32),
                        pltpu.VMEM((8, LANES), F32), pltpu.VMEM((8, LANES), F32), pltpu.VMEM((8, ML_W), F32)],
        compiler_params=_params("arbitrary"),
    )(qk, qk, proj, proj, gates, gate_bias, mlg, cprev, nprev, mprev, dy)
    return outs


def _xattn_tile(qs, ks, vs):
    outs = []
    for q, k, v in zip(qs, ks, vs):
        sc = _mm_nt(q, k) * (XA_HD ** -0.5)
        mx = lax.stop_gradient(jnp.max(sc, axis=1, keepdims=True))
        pe = jnp.exp(sc - mx)
        outs.append(_mm_nn(pe / jnp.sum(pe, axis=1, keepdims=True), v))
    return outs


def _xa_heads(ref):
    return [ref[:, XA_HD * h:XA_HD * (h + 1)] for h in range(XA_HEADS)]


def _xattn_fwd(q, kv, name="xattn_fwd", tm=512):
    s, d = q.shape

    def body(q_ref, k_ref, v_ref, o_ref):
        outs = _xattn_tile(_xa_heads(q_ref), _xa_heads(k_ref), _xa_heads(v_ref))
        for h in range(XA_HEADS):
            o_ref[:, XA_HD * h:XA_HD * (h + 1)] = outs[h]

    row = pl.BlockSpec((tm, d), lambda i: (i, 0))
    return pl.pallas_call(
        body, name=name, grid=(s // tm,),
        in_specs=[row, pl.BlockSpec((MEM_LEN, d), lambda i: (0, 0)), pl.BlockSpec((MEM_LEN, d), lambda i: (0, 1))],
        out_specs=row, out_shape=jax.ShapeDtypeStruct((s, d), F32),
        compiler_params=_params("parallel"),
    )(q, kv, kv)


def _xattn_bwd(q, kv, do, name="xattn_bwd", tm=512):
    s, d = q.shape

    def body(q_ref, k_ref, v_ref, do_ref, dq_ref, dkv_ref):
        i = pl.program_id(0)
        _, vjp = jax.vjp(_xattn_tile, _xa_heads(q_ref), _xa_heads(k_ref), _xa_heads(v_ref))
        dqs, dks, dvs = vjp(_xa_heads(do_ref))

        @pl.when(i == 0)
        def _():
            dkv_ref[...] = jnp.zeros_like(dkv_ref)

        for h in range(XA_HEADS):
            sl = slice(XA_HD * h, XA_HD * (h + 1))
            dq_ref[:, sl] = dqs[h]
            dkv_ref[:, sl] += dks[h]
            dkv_ref[:, d + XA_HD * h:d + XA_HD * (h + 1)] += dvs[h]

    row = pl.BlockSpec((tm, d), lambda i: (i, 0))
    return pl.pallas_call(
        body, name=name, grid=(s // tm,),
        in_specs=[row, pl.BlockSpec((MEM_LEN, d), lambda i: (0, 0)), pl.BlockSpec((MEM_LEN, d), lambda i: (0, 1)), row],
        out_specs=[row, pl.BlockSpec((MEM_LEN, 2 * d), lambda i: (0, 0))],
        out_shape=[jax.ShapeDtypeStruct((s, d), F32), jax.ShapeDtypeStruct((MEM_LEN, 2 * d), F32)],
        compiler_params=_params("arbitrary"),
    )(q, kv, kv, do)


def _loss_head(y, target, name="loss_head", tm=512):
    s, d = y.shape
    nt = s // tm

    def body(y_ref, t_ref, dy_ref, loss_ref, acc):
        i = pl.program_id(0)
        err = y_ref[...] - t_ref[...]
        dy_ref[...] = err * (1.0 / d)

        @pl.when(i == 0)
        def _():
            acc[...] = jnp.zeros_like(acc)

        acc[...] += _rowsum8(err * err)

        @pl.when(i == nt - 1)
        def _():
            tot = jnp.sum(jnp.sum(acc[...], axis=0, keepdims=True), axis=1, keepdims=True)
            loss_ref[...] = jnp.broadcast_to(tot * (0.5 / d), (1, LANES))

    row = pl.BlockSpec((tm, d), lambda i: (i, 0))
    return pl.pallas_call(
        body, name=name, grid=(nt,),
        in_specs=[row, row], out_specs=[row, pl.BlockSpec((1, LANES), lambda i: (0, 0))],
        out_shape=[jax.ShapeDtypeStruct((s, d), F32), jax.ShapeDtypeStruct((1, LANES), F32)],
        scratch_shapes=[pltpu.VMEM((8, d), F32)],
        compiler_params=_params("arbitrary"),
    )(y, target)


def _mesh_pos():
    x, y, c = lax.axis_index("x"), lax.axis_index("y"), lax.axis_index("c")
    return x, y, c, 4 * x + 2 * y + c


def _peer(x, y, c, k):
    px = 1 - x if k & 4 else x
    py = 1 - y if k & 2 else y
    pc = 1 - c if k & 1 else c
    return (px, py, pc), 4 * px + 2 * py + pc


def _all_gather(shard_b, shard_f, name="weights_all_gather"):
    def body(sb_ref, sf_ref, ob_ref, of_ref, send_sems, recv_sems, loc_sems):
        x, y, c, me = _mesh_pos()
        lb = pltpu.make_async_copy(sb_ref, ob_ref.at[me], loc_sems.at[0])
        lf = pltpu.make_async_copy(sf_ref, of_ref.at[me], loc_sems.at[1])
        lb.start()
        lf.start()
        copies = []
        for k in range(1, N_DEV):
            peer, _ = _peer(x, y, c, k)
            for idx, (src, dst) in enumerate(((sb_ref, ob_ref), (sf_ref, of_ref))):
                cp = pltpu.make_async_remote_copy(
                    src_ref=src, dst_ref=dst.at[me], send_sem=send_sems.at[idx, k - 1],
                    recv_sem=recv_sems.at[idx, k - 1], device_id=peer, device_id_type=pl.DeviceIdType.MESH)
                cp.start()
                copies.append(cp)
        for cp in copies:
            cp.wait()
        lb.wait()
        lf.wait()

    hbm = pl.BlockSpec(memory_space=pl.ANY)
    return pl.pallas_call(
        body, name=name, in_specs=[hbm, hbm], out_specs=[hbm, hbm],
        out_shape=[jax.ShapeDtypeStruct((N_DEV,) + shard_b.shape, shard_b.dtype),
                   jax.ShapeDtypeStruct((N_DEV,) + shard_f.shape, shard_f.dtype)],
        scratch_shapes=[pltpu.SemaphoreType.DMA((2, N_DEV - 1)), pltpu.SemaphoreType.DMA((2, N_DEV - 1)),
                        pltpu.SemaphoreType.DMA((2,))],
    )(shard_b, shard_f)


def _all_to_all(send, name="grads_all_to_all"):
    def body(s_ref, r_ref, send_sems, recv_sems, loc_sem):
        x, y, c, me = _mesh_pos()
        loc = pltpu.make_async_copy(s_ref.at[me], r_ref.at[me], loc_sem)
        loc.start()
        copies = []
        for k in range(1, N_DEV):
            peer, pidx = _peer(x, y, c, k)
            cp = pltpu.make_async_remote_copy(
                src_ref=s_ref.at[pidx], dst_ref=r_ref.at[me], send_sem=send_sems.at[k - 1],
                recv_sem=recv_sems.at[k - 1], device_id=peer, device_id_type=pl.DeviceIdType.MESH)
            cp.start()
            copies.append(cp)
        for cp in copies:
            cp.wait()
        loc.wait()

    hbm = pl.BlockSpec(memory_space=pl.ANY)
    return pl.pallas_call(
        body, name=name, in_specs=[hbm], out_specs=hbm,
        out_shape=jax.ShapeDtypeStruct(send.shape, send.dtype),
        scratch_shapes=[pltpu.SemaphoreType.DMA((N_DEV - 1,)), pltpu.SemaphoreType.DMA((N_DEV - 1,)),
                        pltpu.SemaphoreType.DMA],
    )(send)


ADAM_ROWS = 1024


def _adamw(recv, w, m, v, name="adamw", tr=ADAM_ROWS):
    rows = w.shape[0]

    def body(r_ref, w_ref, m_ref, v_ref, g_ref, d_ref, mo_ref, vo_ref):
        g = r_ref[0]
        for j in range(1, N_DEV):
            g = g + r_ref[j]
        mn = ADAM_B1 * m_ref[...] + (1.0 - ADAM_B1) * g
        vn = ADAM_B2 * v_ref[...] + (1.0 - ADAM_B2) * jnp.square(g)
        m_hat = mn / (1.0 - ADAM_B1 ** ADAM_STEP)
        v_hat = vn / (1.0 - ADAM_B2 ** ADAM_STEP)
        g_ref[...] = g
        d_ref[...] = -ADAM_LR * (m_hat / (jnp.sqrt(v_hat) + ADAM_EPS) + ADAM_WD * w_ref[...])
        mo_ref[...] = mn
        vo_ref[...] = vn

    row = pl.BlockSpec((tr, LANES), lambda i: (i, 0))
    return pl.pallas_call(
        body, name=name, grid=(rows // tr,),
        in_specs=[pl.BlockSpec((N_DEV, tr, LANES), lambda i: (0, i, 0)), row, row, row],
        out_specs=[row] * 4, out_shape=[jax.ShapeDtypeStruct((rows, LANES), F32)] * 4,
        compiler_params=_params("parallel"),
    )(recv, w, m, v)


WEIGHTS = ("rel_bias", "ln_g", "ln_b", "ffn_w_gate", "ffn_w_up", "ffn_w_down", "w_in", "conv_w", "conv_b",
           "ig_bias", "fg_bias", "ml_norm_g", "w_out", "xq_w", "xkv_w", "xo_w")
SHARD_SHAPES = {
    "rel_bias": (REL_BUCKETS, ATT_HEADS), "ln_g": (1, 4, LANES), "ln_b": (1, 4, LANES),
    "ffn_w_gate": (1, 2, D_MODEL, FF_SHARD), "ffn_w_up": (1, 2, D_MODEL, FF_SHARD),
    "ffn_w_down": (1, 2, FF_SHARD, D_MODEL), "w_in": (1, D_MODEL, W_IN_SHARD), "conv_w": (1, CONV_K, LANES),
    "conv_b": (1, 2 * ML_W), "ig_bias": (1, ML_HEADS), "fg_bias": (1, ML_HEADS), "ml_norm_g": (1, ML_W),
    "w_out": (1, LANES, D_MODEL), "xq_w": (1, LANES, D_MODEL), "xkv_w": (1, D_MODEL, 2 * D_MODEL // N_DEV),
    "xo_w": (1, LANES, D_MODEL),
}
PACK_ALIGN = 8 * LANES


def _rows_of(n_elems):
    return -(-n_elems // PACK_ALIGN) * 8


def _pack_rows(parts, lead=()):
    out = []
    for p in parts:
        n = p.shape[-1]
        rows = _rows_of(n)
        p = jnp.pad(p, [(0, 0)] * len(lead) + [(0, rows * LANES - n)])
        out.append(p.reshape(lead + (rows, LANES)))
    total = sum(o.shape[-2] for o in out)
    pad = -total % ADAM_ROWS
    if pad:
        out.append(jnp.zeros(lead + (pad, LANES), out[0].dtype))
    return jnp.concatenate(out, axis=len(lead))


def _pack_shards(tree):
    return _pack_rows([tree[n].reshape(-1) for n in WEIGHTS])


def _unpack_shards(flat):
    out, r = {}, 0
    for n in WEIGHTS:
        cnt = int(np.prod(SHARD_SHAPES[n]))
        rows = _rows_of(cnt)
        out[n] = flat[r:r + rows].reshape(-1)[:cnt].reshape(SHARD_SHAPES[n])
        r += rows
    return out


def _split8(full, axis):
    shp = full.shape
    t = full.reshape(shp[:axis] + (N_DEV, shp[axis] // N_DEV) + shp[axis + 1:])
    return jnp.moveaxis(t, axis, 0).reshape(N_DEV, -1)


def _rep8(full):
    return jnp.broadcast_to(full.reshape(1, -1), (N_DEV, full.size))


def kernel(x, mem, rel_bias, ln_g, ln_b, ffn_w_gate, ffn_w_up, ffn_w_down, w_in, conv_w, conv_b, ig_bias, fg_bias, ml_norm_g, w_out, xq_w, xkv_w, xo_w, loss_target, m_rel_bias, m_ln_g, m_ln_b, m_ffn_w_gate, m_ffn_w_up, m_ffn_w_down, m_w_in, m_conv_w, m_conv_b, m_ig_bias, m_fg_bias, m_ml_norm_g, m_w_out, m_xq_w, m_xkv_w, m_xo_w, v_rel_bias, v_ln_g, v_ln_b, v_ffn_w_gate, v_ffn_w_up, v_ffn_w_down, v_w_in, v_conv_w, v_conv_b, v_ig_bias, v_fg_bias, v_ml_norm_g, v_w_out, v_xq_w, v_xkv_w, v_xo_w):
    w_tree = dict(rel_bias=rel_bias, ln_g=ln_g, ln_b=ln_b, ffn_w_gate=ffn_w_gate, ffn_w_up=ffn_w_up,
                  ffn_w_down=ffn_w_down, w_in=w_in, conv_w=conv_w, conv_b=conv_b, ig_bias=ig_bias, fg_bias=fg_bias,
                  ml_norm_g=ml_norm_g, w_out=w_out, xq_w=xq_w, xkv_w=xkv_w, xo_w=xo_w)
    m_tree = dict(rel_bias=m_rel_bias, ln_g=m_ln_g, ln_b=m_ln_b, ffn_w_gate=m_ffn_w_gate, ffn_w_up=m_ffn_w_up,
                  ffn_w_down=m_ffn_w_down, w_in=m_w_in, conv_w=m_conv_w, conv_b=m_conv_b, ig_bias=m_ig_bias,
                  fg_bias=m_fg_bias, ml_norm_g=m_ml_norm_g, w_out=m_w_out, xq_w=m_xq_w, xkv_w=m_xkv_w, xo_w=m_xo_w)
    v_tree = dict(rel_bias=v_rel_bias, ln_g=v_ln_g, ln_b=v_ln_b, ffn_w_gate=v_ffn_w_gate, ffn_w_up=v_ffn_w_up,
                  ffn_w_down=v_ffn_w_down, w_in=v_w_in, conv_w=v_conv_w, conv_b=v_conv_b, ig_bias=v_ig_bias,
                  fg_bias=v_fg_bias, ml_norm_g=v_ml_norm_g, w_out=v_w_out, xq_w=v_xq_w, xkv_w=v_xkv_w, xo_w=v_xo_w)
    s = x.shape[1]
    x0 = x[0]
    pad_ff = FF_PAD - FF_SHARD

    bparts = [
        jnp.pad(ffn_w_gate[0], ((0, 0), (0, 0), (0, pad_ff))), jnp.pad(ffn_w_up[0], ((0, 0), (0, 0), (0, pad_ff))),
        jnp.pad(ffn_w_down[0], ((0, 0), (0, pad_ff), (0, 0))), jnp.pad(w_in[0], ((0, 0), (0, ATT_W - W_IN_SHARD))),
        w_out[0], xq_w[0], xkv_w[0], xo_w[0]]
    bshapes = [p.shape for p in bparts]
    shard_b = jnp.concatenate([p.astype(BF16).reshape(-1, LANES) for p in bparts], axis=0)
    shard_f = jnp.concatenate([ln_g[0], ln_b[0], conv_w[0], jnp.zeros((4, LANES), F32)], axis=0)
    all_b, all_f = _all_gather(shard_b, shard_f)
    gathered, r = [], 0
    for shp in bshapes:
        rows = int(np.prod(shp)) // LANES
        gathered.append(all_b[:, r:r + rows].reshape((N_DEV,) + shp))
        r += rows
    wg_all, wu_all, wd_all, win_all, wout_all, xq_all, xkv_all, xo_all = gathered
    w_in_full = jnp.moveaxis(win_all[:, :, :W_IN_SHARD], 0, 1).reshape(D_MODEL, W_IN)
    w_main = w_in_full[:, :W_IN_MAIN]
    w_gate_cols = jnp.pad(w_in_full[:, W_IN_MAIN:], ((0, 0), (0, LANES - 2 * ML_HEADS)))
    w_out_full = wout_all.reshape(D_MODEL, D_MODEL)
    xq_full = xq_all.reshape(D_MODEL, D_MODEL)
    xo_full = xo_all.reshape(D_MODEL, D_MODEL)
    xkv_full = jnp.moveaxis(xkv_all, 0, 1).reshape(D_MODEL, 2 * D_MODEL)
    unshard = lambda t: jnp.moveaxis(t, 0, 1).reshape(4, D_MODEL)
    ln_g_full = unshard(all_f[:, 0:4])
    ln_b_full = unshard(all_f[:, 4:8])
    conv_w_full = unshard(all_f[:, 8:12])
    lng = lambda i: ln_g_full[i:i + 1]
    lnb = lambda i: ln_b_full[i:i + 1]
    gate_bias = jnp.pad(jnp.concatenate([ig_bias, fg_bias], axis=1), ((0, 0), (0, LANES - 2 * ML_HEADS)))
    buckets = _bucket_tables()

    u0, x1 = _ffn_fwd(x0, wg_all, wu_all, wd_all, lng(0), lnb(0), 0, "ffn1_fwd")
    proj = _matmul(x1, w_main, "nn", "proj_fwd", tk=D_MODEL)
    gates = _matmul(x1, w_gate_cols, "nn", "gates_fwd", tk=D_MODEL)
    biasm = _bias_fwd(rel_bias, buckets)
    branches = [_dil_fwd(proj, biasm, b, f"dil_fwd_{b}") for b in range(len(DILATED))]
    att, lse = _dil_combine([o for o, _ in branches], [l for _, l in branches])
    qk = _conv_fwd(proj, conv_w_full, conv_b)
    y_m, c_prev, n_prev, m_prev = _mlstm_fwd(qk, proj, gates, gate_bias, ml_norm_g)
    cat = jnp.concatenate([att, y_m], axis=1)
    f1 = _matmul(cat, w_out_full, "nn", "w_out_fwd", tn=D_MODEL, tk=D_MODEL)
    u1, x2 = _resid_ln(x1, f1, lng(1), lnb(1), "mixer_ln")
    q_x = _matmul(x2, xq_full, "nn", "xq_fwd", tn=D_MODEL, tk=D_MODEL)
    kv = _matmul(mem[0], xkv_full, "nn", "xkv_fwd", tk=D_MODEL)
    o_x = _xattn_fwd(q_x, kv)
    f2 = _matmul(o_x, xo_full, "nn", "xo_fwd", tn=D_MODEL, tk=D_MODEL)
    u2, x3 = _resid_ln(x2, f2, lng(2), lnb(2), "xattn_ln")
    u3, x4 = _ffn_fwd(x3, wg_all, wu_all, wd_all, lng(3), lnb(3), 1, "ffn2_fwd")
    dx4, loss_row = _loss_head(x4, loss_target[0])

    g = {}
    dx3, df, da, db, hh, dg3, db3 = _ffn_bwd_x(dx4, u3, x3, wg_all, wu_all, wd_all, lng(3), 1, "ffn2_bwd_x")
    dwg1, dwu1, dwd1 = _ffn_bwd_w(x3, df, da, db, hh, "ffn2_bwd_w")

    du2, dg2, db2 = _ln_bwd(dx3, u2, lng(2), "xattn_ln_bwd")
    do_x = _matmul(du2, xo_full, "nt", "xo_bwd_x", tn=D_MODEL)
    g["xo_w"] = _matmul(o_x, du2, "tn", "xo_bwd_w", tm=D_MODEL)
    dq_x, dkv = _xattn_bwd(q_x, kv, do_x)
    g["xq_w"] = _matmul(x2, dq_x, "tn", "xq_bwd_w", tm=D_MODEL)
    g["xkv_w"] = _matmul(mem[0], dkv, "tn", "xkv_bwd_w", tm=D_MODEL, tk=MEM_LEN)
    dx2 = _matmul(dq_x, xq_full, "nt", "xq_bwd_x", tn=D_MODEL, add=du2, add_scale=ALPHA)

    du1, dg1, db1 = _ln_bwd(dx2, u1, lng(1), "mixer_ln_bwd")
    dcat = _matmul(du1, w_out_full, "nt", "w_out_bwd_x", tn=D_MODEL)
    g["w_out"] = _matmul(cat, du1, "tn", "w_out_bwd_w", tm=D_MODEL)
    dqk, dv_m, do_m, dgates, dgate_bias, g_mlg = _mlstm_bwd(qk, proj, gates, gate_bias, ml_norm_g,
                                                          c_prev, n_prev, m_prev, dcat)
    dqk_pre, g_conv_w, g_conv_b = _conv_bwd(proj, dqk, conv_w_full, conv_b)
    datt = dcat[:, :ATT_W]
    acc, dbias = None, []
    for b in range(len(DILATED)):
        acc, dbb = _dil_bwd(proj, biasm, lse, att, datt, acc, b, f"dil_bwd_{b}")
        dbias.append(dbb)
    g_rel = _bias_bwd(jnp.stack(dbias), buckets)[:, :ATT_HEADS]
    dproj = jnp.concatenate([acc[0], acc[1], acc[2], dqk_pre, dv_m, do_m], axis=1)
    g_w_main = _matmul(x1, dproj, "tn", "proj_bwd_w", tm=D_MODEL)
    g_w_gates = _matmul(x1, dgates, "tn", "gates_bwd_w", tm=D_MODEL)
    g["w_in"] = jnp.concatenate([g_w_main, g_w_gates[:, :2 * ML_HEADS]], axis=1)
    dx1 = _matmul(dproj, w_main, "nt", "proj_bwd_x", tn=D_MODEL, add=du1, add_scale=ALPHA)
    dx1 = _matmul(dgates, w_gate_cols, "nt", "gates_bwd_x", tn=D_MODEL, add=dx1)

    dx0, df, da, db, hh, dg0, db0 = _ffn_bwd_x(dx1, u0, x0, wg_all, wu_all, wd_all, lng(0), 0, "ffn1_bwd_x")
    dwg0, dwu0, dwd0 = _ffn_bwd_w(x0, df, da, db, hh, "ffn1_bwd_w")

    blocks = {
        "rel_bias": _rep8(g_rel),
        "ln_g": _split8(jnp.concatenate([dg0, dg1, dg2, dg3], axis=0), 1),
        "ln_b": _split8(jnp.concatenate([db0, db1, db2, db3], axis=0), 1),
        "ffn_w_gate": jnp.stack([dwg0, dwg1], axis=1)[..., :FF_SHARD].reshape(N_DEV, -1),
        "ffn_w_up": jnp.stack([dwu0, dwu1], axis=1)[..., :FF_SHARD].reshape(N_DEV, -1),
        "ffn_w_down": jnp.stack([dwd0, dwd1], axis=1)[:, :, :FF_SHARD, :].reshape(N_DEV, -1),
        "w_in": _split8(g["w_in"], 1),
        "conv_w": _split8(g_conv_w, 1),
        "conv_b": _rep8(g_conv_b),
        "ig_bias": _rep8(dgate_bias[:, :ML_HEADS]),
        "fg_bias": _rep8(dgate_bias[:, ML_HEADS:2 * ML_HEADS]),
        "ml_norm_g": _rep8(g_mlg),
        "w_out": _split8(g["w_out"], 0),
        "xq_w": _split8(g["xq_w"], 0),
        "xkv_w": _split8(g["xkv_w"], 1),
        "xo_w": _split8(g["xo_w"], 0),
    }
    send = _pack_rows([blocks[n] for n in WEIGHTS], lead=(N_DEV,))
    recv = _all_to_all(send)
    g_flat, d_flat, m_flat, v_flat = _adamw(recv, _pack_shards(w_tree), _pack_shards(m_tree), _pack_shards(v_tree))
    grads, deltas, new_m, new_v = (_unpack_shards(t) for t in (g_flat, d_flat, m_flat, v_flat))

    loss = lax.psum(loss_row[0, 0], ("x", "y", "c"))
    return (loss, dx0[None], *[grads[n] for n in WEIGHTS], *[deltas[n] for n in WEIGHTS],
            *[new_m[n] for n in WEIGHTS], *[new_v[n] for n in WEIGHTS])
```

```python
import functools
import math

import numpy as np
import jax
import jax.numpy as jnp
from jax import lax
from jax.experimental import pallas as pl
from jax.experimental.pallas import tpu as pltpu

F32 = jnp.float32
BF16 = jnp.bfloat16

N_DEV = 8
D_MODEL = 1024
D_FF = 2816
FF_SHARD = D_FF // N_DEV
FF_PAD = 384
ATT_W = 512
ATT_HEADS = 8
DILATED = ((128, 1), (512, 4), (2048, 16))
BLK = 128
ML_W = 512
ML_HEADS = 4
ML_HD = 128
CHUNK = 128
CONV_K = 4
W_IN = 3592
W_IN_SHARD = W_IN // N_DEV
W_IN_MAIN = 3584
XA_HEADS = 4
XA_HD = 256
MEM_LEN = 256
REL_BUCKETS = 32
REL_MAX_DIST = 2048
ALPHA = 2.0 ** 0.25
LN_EPS = 1e-5
NEG = -1e30
ADAM_LR = 0.001
ADAM_B1 = 0.9
ADAM_B2 = 0.999
ADAM_EPS = 1e-08
ADAM_WD = 0.01
ADAM_STEP = 10
LANES = 128
VMEM_LIMIT = 48 * 1024 * 1024

NN = (((1,), (0,)), ((), ()))
NT = (((1,), (1,)), ((), ()))
TN = (((0,), (0,)), ((), ()))


def _dot(a, b, dims):
    return lax.dot_general(a, b, dims, preferred_element_type=F32)


def _params(*sem):
    return pltpu.CompilerParams(dimension_semantics=sem, vmem_limit_bytes=VMEM_LIMIT)


def _sigmoid(x):
    return 1.0 / (1.0 + jnp.exp(-x))


def _rowsum8(x):
    t, c = x.shape
    return jnp.sum(x.reshape(t // 8, 8, c), axis=0)


def _mesh_pos():
    x, y, c = lax.axis_index("x"), lax.axis_index("y"), lax.axis_index("c")
    return x, y, c, 4 * x + 2 * y + c


def _peer(x, y, c, k):
    px = 1 - x if k & 4 else x
    py = 1 - y if k & 2 else y
    pc = 1 - c if k & 1 else c
    return (px, py, pc), 4 * px + 2 * py + pc


def _call(body, *, name, grid, in_specs, out_specs, out_shape, args, scratch_shapes=(), sem=None,
          gather=(), exchange=()):
    in_specs, out_specs, out_shape, scratch = list(in_specs), list(out_specs), list(out_shape), list(scratch_shapes)
    ng, nc = len(gather), len(gather) + len(exchange)
    if nc == 0:
        return pl.pallas_call(body, name=name, grid=grid, in_specs=in_specs, out_specs=out_specs,
                              out_shape=out_shape, scratch_shapes=scratch, compiler_params=_params(*sem))(*args)
    n_in, n_out, n_scr = len(in_specs), len(out_specs), len(scratch)

    def wrapped(*refs):
        ins, cin = refs[:n_in], refs[n_in:n_in + nc]
        outs, cout = refs[n_in + nc:n_in + nc + n_out], refs[n_in + nc + n_out:n_in + 2 * nc + n_out]
        scr = refs[n_in + 2 * nc + n_out:n_in + 2 * nc + n_out + n_scr]
        send_sems, recv_sems, loc_sems = refs[-3:]
        first, last = None, None
        for ax, extent in enumerate(grid):
            f, l = pl.program_id(ax) == 0, pl.program_id(ax) == extent - 1
            first = f if first is None else first & f
            last = l if last is None else last & l

        def copies():
            x, y, c, me = _mesh_pos()
            out = []
            for a in range(nc):
                mine = cin[a] if a < ng else cin[a].at[me]
                out.append(pltpu.make_async_copy(mine, cout[a].at[me], loc_sems.at[a]))
                for k in range(1, N_DEV):
                    peer, pidx = _peer(x, y, c, k)
                    out.append(pltpu.make_async_remote_copy(
                        src_ref=cin[a] if a < ng else cin[a].at[pidx], dst_ref=cout[a].at[me],
                        send_sem=send_sems.at[a, k - 1], recv_sem=recv_sems.at[a, k - 1],
                        device_id=peer, device_id_type=pl.DeviceIdType.MESH))
            return out

        @pl.when(first)
        def _():
            for cp in copies():
                cp.start()

        body(*ins, *outs, *scr)

        @pl.when(last)
        def _():
            for cp in copies():
                cp.wait()

    hbm = pl.BlockSpec(memory_space=pl.ANY)
    comm_shapes = [jax.ShapeDtypeStruct((N_DEV,) + a.shape, a.dtype) for a in gather]
    comm_shapes += [jax.ShapeDtypeStruct(a.shape, a.dtype) for a in exchange]
    return pl.pallas_call(
        wrapped, name=name, grid=grid, in_specs=in_specs + [hbm] * nc, out_specs=out_specs + [hbm] * nc,
        out_shape=out_shape + comm_shapes,
        scratch_shapes=scratch + [pltpu.SemaphoreType.DMA((nc, N_DEV - 1)), pltpu.SemaphoreType.DMA((nc, N_DEV - 1)),
                                  pltpu.SemaphoreType.DMA((nc,))],
        compiler_params=_params(*(("arbitrary",) * len(grid))),
    )(*args, *gather, *exchange)


def _exchange_only(name, gather=(), exchange=()):
    return _call(lambda: None, name=name, grid=(1,), in_specs=[], out_specs=[], out_shape=[], args=(),
                 gather=gather, exchange=exchange)


def _matmul(a, b, mode, name, *, out_dtype=F32, tm=512, tn=512, tk=512, add=None, add_scale=1.0,
            blocked_out=False, gather=(), exchange=()):
    blocked_b = b.ndim == 3
    if blocked_b:
        (m, k), (nb, _, tn) = a.shape, b.shape
        n = nb * tn
    elif mode == "nn":
        (m, k), (_, n) = a.shape, b.shape
    elif mode == "nt":
        (m, k), (n, _) = a.shape, b.shape
    else:
        (k, m), (_, n) = a.shape, b.shape
    tm, tn, tk = min(tm, m), min(tn, n), min(tk, k)
    nk = k // tk
    dims = {"nn": NN, "nt": NT, "tn": TN}[mode]
    if mode == "tn":
        a_spec = pl.BlockSpec((tk, tm), lambda i, j, kk: (kk, i))
    else:
        a_spec = pl.BlockSpec((tm, tk), lambda i, j, kk: (i, kk))
    if blocked_b:
        b_spec = pl.BlockSpec((None, tk, tn), lambda i, j, kk: (j, kk, 0))
    elif mode == "nt":
        b_spec = pl.BlockSpec((tn, tk), lambda i, j, kk: (j, kk))
    else:
        b_spec = pl.BlockSpec((tk, tn), lambda i, j, kk: (kk, j))
    if blocked_out:
        o_spec = pl.BlockSpec((None, tm, tn), lambda i, j, kk: (j, i, 0))
        o_shape = jax.ShapeDtypeStruct((n // tn, m, tn), out_dtype)
    else:
        o_spec = pl.BlockSpec((tm, tn), lambda i, j, kk: (i, j))
        o_shape = jax.ShapeDtypeStruct((m, n), out_dtype)
    has_add = add is not None

    def body(*refs):
        if has_add:
            a_ref, b_ref, add_ref, o_ref, acc_ref = refs
        else:
            a_ref, b_ref, o_ref, acc_ref = refs
        kk = pl.program_id(2)
        part = _dot(a_ref[...].astype(BF16), b_ref[...].astype(BF16), dims)

        @pl.when(kk == 0)
        def _():
            acc_ref[...] = part

        @pl.when(kk > 0)
        def _():
            acc_ref[...] += part

        @pl.when(kk == nk - 1)
        def _():
            r = acc_ref[...]
            if has_add:
                r = r + add_scale * add_ref[...]
            o_ref[...] = r.astype(out_dtype)

    return _call(
        body, name=name, grid=(m // tm, n // tn, nk),
        in_specs=[a_spec, b_spec] + ([pl.BlockSpec((tm, tn), lambda i, j, kk: (i, j))] if has_add else []),
        out_specs=[o_spec], out_shape=[o_shape], args=(a, b) + ((add,) if has_add else ()),
        scratch_shapes=[pltpu.VMEM((tm, tn), F32)], sem=("parallel", "parallel", "arbitrary"),
        gather=gather, exchange=exchange)


def _ln_fwd_math(u, g, b):
    mu = jnp.mean(u, axis=-1, keepdims=True)
    uc = u - mu
    var = jnp.mean(uc * uc, axis=-1, keepdims=True)
    return uc * lax.rsqrt(var + LN_EPS) * g + b


def _ln_bwd_math(dy, u, g):
    mu = jnp.mean(u, axis=-1, keepdims=True)
    uc = u - mu
    var = jnp.mean(uc * uc, axis=-1, keepdims=True)
    rstd = lax.rsqrt(var + LN_EPS)
    xhat = uc * rstd
    dxh = dy * g
    m1 = jnp.mean(dxh, axis=-1, keepdims=True)
    m2 = jnp.mean(dxh * xhat, axis=-1, keepdims=True)
    return rstd * (dxh - m1 - xhat * m2), xhat


def _resid_ln(x, f, g, b, name, tm=512):
    s, d = x.shape

    def body(x_ref, f_ref, g_ref, b_ref, u_ref, y_ref):
        u = ALPHA * x_ref[...] + f_ref[...]
        u_ref[...] = u
        y_ref[...] = _ln_fwd_math(u, g_ref[...], b_ref[...])

    row = pl.BlockSpec((tm, d), lambda i: (i, 0))
    vec = pl.BlockSpec((1, d), lambda i: (0, 0))
    return pl.pallas_call(
        body, name=name, grid=(s // tm,),
        in_specs=[row, row, vec, vec], out_specs=[row, row],
        out_shape=[jax.ShapeDtypeStruct((s, d), F32)] * 2,
        compiler_params=_params("parallel"),
    )(x, f, g, b)


def _ln_bwd(dy, u, g, name, tm=512):
    s, d = dy.shape
    nt = s // tm

    def body(dy_ref, u_ref, g_ref, du_ref, dg_ref, db_ref, g8, b8):
        i = pl.program_id(0)
        dy_ = dy_ref[...]
        du, xhat = _ln_bwd_math(dy_, u_ref[...], g_ref[...])
        du_ref[...] = du

        @pl.when(i == 0)
        def _():
            g8[...] = jnp.zeros_like(g8)
            b8[...] = jnp.zeros_like(b8)

        g8[...] += _rowsum8(dy_ * xhat)
        b8[...] += _rowsum8(dy_)

        @pl.when(i == nt - 1)
        def _():
            dg_ref[...] = jnp.sum(g8[...], axis=0, keepdims=True)
            db_ref[...] = jnp.sum(b8[...], axis=0, keepdims=True)

    row = pl.BlockSpec((tm, d), lambda i: (i, 0))
    vec = pl.BlockSpec((1, d), lambda i: (0, 0))
    return pl.pallas_call(
        body, name=name, grid=(nt,),
        in_specs=[row, row, vec], out_specs=[row, vec, vec],
        out_shape=[jax.ShapeDtypeStruct((s, d), F32), jax.ShapeDtypeStruct((1, d), F32),
                   jax.ShapeDtypeStruct((1, d), F32)],
        scratch_shapes=[pltpu.VMEM((8, d), F32), pltpu.VMEM((8, d), F32)],
        compiler_params=_params("arbitrary"),
    )(dy, u, g)


def _ffn_fwd(x, wg, wu, wd, g, b, name, tm=512, gather=()):
    s, d = x.shape

    def body(x_ref, wg_ref, wu_ref, wd_ref, g_ref, b_ref, u_ref, y_ref, xb, acc):
        k = pl.program_id(1)

        @pl.when(k == 0)
        def _():
            xb[...] = x_ref[...].astype(BF16)

        a = _dot(xb[...], wg_ref[...], NN)
        bb = _dot(xb[...], wu_ref[...], NN)
        h = (a * _sigmoid(a) * bb).astype(BF16)
        part = _dot(h, wd_ref[...], NN)

        @pl.when(k == 0)
        def _():
            acc[...] = part

        @pl.when(k > 0)
        def _():
            acc[...] += part

        @pl.when(k == N_DEV - 1)
        def _():
            u = ALPHA * x_ref[...] + 0.5 * acc[...]
            u_ref[...] = u
            y_ref[...] = _ln_fwd_math(u, g_ref[...], b_ref[...])

    row = pl.BlockSpec((tm, d), lambda i, k: (i, 0))
    vec = pl.BlockSpec((1, d), lambda i, k: (0, 0))
    w_in = pl.BlockSpec((None, d, FF_PAD), lambda i, k: (k, 0, 0))
    w_dn = pl.BlockSpec((None, FF_PAD, d), lambda i, k: (k, 0, 0))
    return _call(
        body, name=name, grid=(s // tm, N_DEV),
        in_specs=[row, w_in, w_in, w_dn, vec, vec], out_specs=[row, row],
        out_shape=[jax.ShapeDtypeStruct((s, d), F32)] * 2, args=(x, wg, wu, wd, g, b),
        scratch_shapes=[pltpu.VMEM((tm, d), BF16), pltpu.VMEM((tm, d), F32)],
        sem=("parallel", "arbitrary"), gather=gather)


def _ffn_bwd_x(dy, u, x, wg, wu, wd, g, name, tm=512, exchange=()):
    s, d = x.shape
    nt = s // tm
    ffp = N_DEV * FF_PAD

    def body(dy_ref, u_ref, x_ref, wg_ref, wu_ref, wd_ref, g_ref,
             dx_ref, df_ref, da_ref, db_ref, h_ref, dg_ref, dbl_ref,
             xb, dfb, du_s, acc, g8, b8):
        i = pl.program_id(0)
        k = pl.program_id(1)

        @pl.when(k == 0)
        def _():
            dy_ = dy_ref[...]
            du, xhat = _ln_bwd_math(dy_, u_ref[...], g_ref[...])
            du_s[...] = du
            dfb[...] = (0.5 * du).astype(BF16)
            df_ref[...] = dfb[...]
            xb[...] = x_ref[...].astype(BF16)

            @pl.when(i == 0)
            def _():
                g8[...] = jnp.zeros_like(g8)
                b8[...] = jnp.zeros_like(b8)

            g8[...] += _rowsum8(dy_ * xhat)
            b8[...] += _rowsum8(dy_)

        a = _dot(xb[...], wg_ref[...], NN)
        bb = _dot(xb[...], wu_ref[...], NN)
        sig = _sigmoid(a)
        sa = a * sig
        h_ref[...] = (sa * bb).astype(BF16)
        dh = _dot(dfb[...], wd_ref[...], NT)
        da = (dh * bb * (sig * (1.0 + a * (1.0 - sig)))).astype(BF16)
        db = (dh * sa).astype(BF16)
        da_ref[...] = da
        db_ref[...] = db
        part = _dot(da, wg_ref[...], NT) + _dot(db, wu_ref[...], NT)

        @pl.when(k == 0)
        def _():
            acc[...] = part

        @pl.when(k > 0)
        def _():
            acc[...] += part

        @pl.when(k == N_DEV - 1)
        def _():
            dx_ref[...] = ALPHA * du_s[...] + acc[...]

        @pl.when((k == N_DEV - 1) & (i == nt - 1))
        def _():
            dg_ref[...] = jnp.sum(g8[...], axis=0, keepdims=True)
            dbl_ref[...] = jnp.sum(b8[...], axis=0, keepdims=True)

    row = pl.BlockSpec((tm, d), lambda i, k: (i, 0))
    vec = pl.BlockSpec((1, d), lambda i, k: (0, 0))
    w_in = pl.BlockSpec((None, d, FF_PAD), lambda i, k: (k, 0, 0))
    w_dn = pl.BlockSpec((None, FF_PAD, d), lambda i, k: (k, 0, 0))
    hid = pl.BlockSpec((tm, FF_PAD), lambda i, k: (i, k))
    return _call(
        body, name=name, grid=(nt, N_DEV),
        in_specs=[row, row, row, w_in, w_in, w_dn, vec],
        out_specs=[row, row, hid, hid, hid, vec, vec],
        out_shape=[jax.ShapeDtypeStruct((s, d), F32), jax.ShapeDtypeStruct((s, d), BF16),
                   jax.ShapeDtypeStruct((s, ffp), BF16), jax.ShapeDtypeStruct((s, ffp), BF16),
                   jax.ShapeDtypeStruct((s, ffp), BF16),
                   jax.ShapeDtypeStruct((1, d), F32), jax.ShapeDtypeStruct((1, d), F32)],
        args=(dy, u, x, wg, wu, wd, g),
        scratch_shapes=[pltpu.VMEM((tm, d), BF16), pltpu.VMEM((tm, d), BF16), pltpu.VMEM((tm, d), F32),
                        pltpu.VMEM((tm, d), F32), pltpu.VMEM((8, d), F32), pltpu.VMEM((8, d), F32)],
        sem=("arbitrary", "arbitrary"), exchange=exchange)


def _ffn_bwd_w(x, df, da, db, h, name, tm=512):
    s, d = x.shape
    nt = s // tm

    def body(x_ref, df_ref, da_ref, db_ref, h_ref, dwg_ref, dwu_ref, dwd_ref, ag, au, ad):
        i = pl.program_id(1)
        xb = x_ref[...].astype(BF16)
        pg = _dot(xb, da_ref[...], TN)
        pu = _dot(xb, db_ref[...], TN)
        pd = _dot(h_ref[...], df_ref[...], TN)

        @pl.when(i == 0)
        def _():
            ag[...] = pg
            au[...] = pu
            ad[...] = pd

        @pl.when(i > 0)
        def _():
            ag[...] += pg
            au[...] += pu
            ad[...] += pd

        @pl.when(i == nt - 1)
        def _():
            dwg_ref[...] = ag[:, :FF_SHARD].astype(BF16)
            dwu_ref[...] = au[:, :FF_SHARD].astype(BF16)
            dwd_ref[...] = ad[:FF_SHARD, :].astype(BF16)

    row = pl.BlockSpec((tm, d), lambda k, i: (i, 0))
    hid = pl.BlockSpec((tm, FF_PAD), lambda k, i: (i, k))
    w_in = pl.BlockSpec((None, d, FF_SHARD), lambda k, i: (k, 0, 0))
    w_dn = pl.BlockSpec((None, FF_SHARD, d), lambda k, i: (k, 0, 0))
    return _call(
        body, name=name, grid=(N_DEV, nt),
        in_specs=[row, row, hid, hid, hid], out_specs=[w_in, w_in, w_dn],
        out_shape=[jax.ShapeDtypeStruct((N_DEV, d, FF_SHARD), BF16), jax.ShapeDtypeStruct((N_DEV, d, FF_SHARD), BF16),
                   jax.ShapeDtypeStruct((N_DEV, FF_SHARD, d), BF16)],
        args=(x, df, da, db, h),
        scratch_shapes=[pltpu.VMEM((d, FF_PAD), F32), pltpu.VMEM((d, FF_PAD), F32), pltpu.VMEM((FF_PAD, d), F32)],
        sem=("parallel", "arbitrary"))


def _bucket_tables():
    qi = np.arange(BLK)[:, None]
    ki = np.arange(2 * BLK)[None, :]
    off = qi + BLK - ki
    out = []
    for window, dil in DILATED:
        n_keys = window // dil
        dist = dil * np.clip(off, 0, n_keys)
        exact = REL_BUCKETS // 2
        df = np.maximum(dist, 1).astype(np.float32)
        large = exact + (np.log(df / np.float32(exact)) / np.float32(math.log(REL_MAX_DIST / exact))
                         * np.float32(REL_BUCKETS - exact)).astype(np.int32)
        large = np.minimum(large, REL_BUCKETS - 1)
        bucket = np.where(dist < exact, dist, large).astype(np.int32)
        band = (off >= 0) & (off <= n_keys)
        out.append(np.where(band, bucket, -1))
    return np.stack(out).astype(np.int32)


def _bias_fwd(rel_bias, buckets, name="bias_fwd"):
    def body(tbl_ref, bkt_ref, out_ref):
        bkt = bkt_ref[...]
        for h in range(ATT_HEADS):
            acc = jnp.full((BLK, 2 * BLK), NEG, F32)
            for bb in range(REL_BUCKETS):
                acc = jnp.where(bkt == bb, tbl_ref[bb, h], acc)
            out_ref[h] = acc

    nbr = len(DILATED)
    return pl.pallas_call(
        body, name=name, grid=(nbr,),
        in_specs=[pl.BlockSpec(memory_space=pltpu.SMEM),
                  pl.BlockSpec((None, BLK, 2 * BLK), lambda r: (r, 0, 0))],
        out_specs=pl.BlockSpec((None, ATT_HEADS, BLK, 2 * BLK), lambda r: (r, 0, 0, 0)),
        out_shape=jax.ShapeDtypeStruct((nbr, ATT_HEADS, BLK, 2 * BLK), F32),
        compiler_params=_params("parallel"),
    )(rel_bias, buckets)


def _bias_bwd(dbias, buckets, name="bias_bwd"):
    nbr = len(DILATED)

    def body(db_ref, bkt_ref, out_ref):
        r = pl.program_id(0)

        @pl.when(r == 0)
        def _():
            out_ref[...] = jnp.zeros_like(out_ref)

        bkt = bkt_ref[...]
        rowi = lax.broadcasted_iota(jnp.int32, (REL_BUCKETS, LANES), 0)
        coli = lax.broadcasted_iota(jnp.int32, (REL_BUCKETS, LANES), 1)
        acc = jnp.zeros((REL_BUCKETS, LANES), F32)
        for h in range(ATT_HEADS):
            x = db_ref[h]
            for bb in range(REL_BUCKETS):
                part = jnp.sum(jnp.where(bkt == bb, x, 0.0), axis=0, keepdims=True)
                tot = jnp.sum(part, axis=1, keepdims=True)
                acc = acc + jnp.where((rowi == bb) & (coli == h), tot, 0.0)
        out_ref[...] += acc

    return pl.pallas_call(
        body, name=name, grid=(nbr,),
        in_specs=[pl.BlockSpec((None, ATT_HEADS, BLK, 2 * BLK), lambda r: (r, 0, 0, 0)),
                  pl.BlockSpec((None, BLK, 2 * BLK), lambda r: (r, 0, 0))],
        out_specs=pl.BlockSpec((REL_BUCKETS, LANES), lambda r: (0, 0)),
        out_shape=jax.ShapeDtypeStruct((REL_BUCKETS, LANES), F32),
        compiler_params=_params("arbitrary"),
    )(dbias, buckets)


def _att_scores(q_pair, k2, bias, first_ok, msk):
    qm = jnp.where(msk, q_pair, 0.0).astype(BF16)
    sc = _dot(qm, k2, NT) * (64 ** -0.5) + bias
    return jnp.where(first_ok, sc, NEG), qm


def _dil_specs(dil, nb, clamp):
    ncol = W_IN_MAIN // ATT_W

    def cur(col):
        return pl.BlockSpec((BLK, ATT_W), lambda r, n: (jnp.minimum(n, nb - 1) if clamp else n, r * ncol + col))

    def prev(col):
        return pl.BlockSpec(
            (BLK, ATT_W), lambda r, n: (jnp.maximum((jnp.minimum(n, nb - 1) if clamp else n) - 1, 0), r * ncol + col))

    return [cur(0), prev(1), cur(1), prev(2), cur(2)]


def _dil_fwd(proj, biasm, branch, name, gather=()):
    s = proj.shape[0]
    dil = DILATED[branch][1]
    m = s // dil
    nb = m // BLK
    pv = proj.reshape(m, dil * W_IN_MAIN)

    def body(q_ref, kp_ref, kc_ref, vp_ref, vc_ref, bias_ref, o_ref, lse_ref):
        n = pl.program_id(1)
        lo = lax.broadcasted_iota(jnp.int32, (BLK, LANES), 1) < 64
        kidx = lax.broadcasted_iota(jnp.int32, (BLK, 2 * BLK), 1)
        first_ok = (n > 0) | (kidx >= BLK)
        for p in range(ATT_W // LANES):
            sl = slice(LANES * p, LANES * (p + 1))
            q_pair = q_ref[:, sl]
            k2 = jnp.concatenate([kp_ref[:, sl], kc_ref[:, sl]], axis=0).astype(BF16)
            v2 = jnp.concatenate([vp_ref[:, sl], vc_ref[:, sl]], axis=0).astype(BF16)
            outs, lses = [], []
            for hh in range(2):
                msk = lo if hh == 0 else jnp.logical_not(lo)
                sc, _ = _att_scores(q_pair, k2, bias_ref[2 * p + hh], first_ok, msk)
                mx = jnp.max(sc, axis=1, keepdims=True)
                pe = jnp.exp(sc - mx)
                l = jnp.sum(pe, axis=1, keepdims=True)
                outs.append(_dot(pe.astype(BF16), v2, NN) / l)
                lses.append(jnp.broadcast_to(mx + jnp.log(l), (BLK, LANES)))
            o_ref[:, sl] = jnp.where(lo, outs[0], outs[1])
            lse_ref[:, sl] = jnp.where(lo, lses[0], lses[1])

    out_spec = pl.BlockSpec((BLK, ATT_W), lambda r, n: (n, r))
    o, lse, *rest = _call(
        body, name=name, grid=(dil, nb),
        in_specs=_dil_specs(dil, nb, False) + [pl.BlockSpec((None, ATT_HEADS, BLK, 2 * BLK), lambda r, n: (branch, 0, 0, 0))],
        out_specs=[out_spec, out_spec],
        out_shape=[jax.ShapeDtypeStruct((m, dil * ATT_W), F32)] * 2,
        args=(pv, pv, pv, pv, pv, biasm), sem=("parallel", "arbitrary"), gather=gather)
    return [o.reshape(s, ATT_W), lse.reshape(s, ATT_W)] + rest


def _dil_combine(os_, lses, name="dil_combine", tm=512):
    s = os_[0].shape[0]

    def body(o0, o1, o2, l0, l1, l2, att_ref, lse_ref):
        a, b, c = l0[...], l1[...], l2[...]
        mx = jnp.maximum(jnp.maximum(a, b), c)
        ea, eb, ec = jnp.exp(a - mx), jnp.exp(b - mx), jnp.exp(c - mx)
        tot = ea + eb + ec
        att_ref[...] = (ea * o0[...] + eb * o1[...] + ec * o2[...]) / tot
        lse_ref[...] = mx + jnp.log(tot)

    row = pl.BlockSpec((tm, ATT_W), lambda i: (i, 0))
    return pl.pallas_call(
        body, name=name, grid=(s // tm,),
        in_specs=[row] * 6, out_specs=[row, row],
        out_shape=[jax.ShapeDtypeStruct((s, ATT_W), F32)] * 2,
        compiler_params=_params("parallel"),
    )(*os_, *lses)


def _dil_bwd(proj, biasm, lse, att, datt, acc, branch, name):
    s = proj.shape[0]
    dil = DILATED[branch][1]
    m = s // dil
    nb = m // BLK
    pv = proj.reshape(m, dil * W_IN_MAIN)
    has_acc = acc is not None
    view = lambda t: t.reshape(m, dil * ATT_W)

    def body(*refs):
        q_ref, kp_ref, kc_ref, vp_ref, vc_ref, bias_ref, lse_ref, att_ref, datt_ref = refs[:9]
        refs = refs[9:]
        if has_acc:
            aq_ref, ak_ref, av_ref = refs[:3]
            refs = refs[3:]
        dq_ref, dk_ref, dv_ref, dbias_ref, kcar, vcar = refs
        r = pl.program_id(0)
        n = pl.program_id(1)

        @pl.when((r == 0) & (n == 0))
        def _():
            dbias_ref[...] = jnp.zeros_like(dbias_ref)

        @pl.when(n == 0)
        def _():
            kcar[...] = jnp.zeros_like(kcar)
            vcar[...] = jnp.zeros_like(vcar)

        @pl.when(n < nb)
        def _():
            lo = lax.broadcasted_iota(jnp.int32, (BLK, LANES), 1) < 64
            kidx = lax.broadcasted_iota(jnp.int32, (BLK, 2 * BLK), 1)
            first_ok = (n > 0) | (kidx >= BLK)
            for p in range(ATT_W // LANES):
                sl = slice(LANES * p, LANES * (p + 1))
                q_pair = q_ref[:, sl]
                k2 = jnp.concatenate([kp_ref[:, sl], kc_ref[:, sl]], axis=0).astype(BF16)
                v2 = jnp.concatenate([vp_ref[:, sl], vc_ref[:, sl]], axis=0).astype(BF16)
                lse_pair = lse_ref[:, sl]
                dd_pair = datt_ref[:, sl] * att_ref[:, sl]
                dat_pair = datt_ref[:, sl]
                dqs, dk2, dv2 = [], None, None
                for hh in range(2):
                    msk = lo if hh == 0 else jnp.logical_not(lo)
                    sc, qm = _att_scores(q_pair, k2, bias_ref[2 * p + hh], first_ok, msk)
                    lse_h = jnp.max(jnp.where(msk, lse_pair, -jnp.inf), axis=1, keepdims=True)
                    pr = jnp.exp(sc - lse_h)
                    dsum = jnp.sum(jnp.where(msk, dd_pair, 0.0), axis=1, keepdims=True)
                    dom = jnp.where(msk, dat_pair, 0.0).astype(BF16)
                    dp = _dot(dom, v2, NT)
                    ds = pr * (dp - dsum)
                    dbias_ref[2 * p + hh] += ds
                    dsb = (ds * (64 ** -0.5)).astype(BF16)
                    dqs.append(_dot(dsb, k2, NN))
                    dkh = _dot(dsb, qm, TN)
                    dvh = _dot(pr.astype(BF16), dom, TN)
                    dk2 = dkh if dk2 is None else dk2 + dkh
                    dv2 = dvh if dv2 is None else dv2 + dvh
                dq = jnp.where(lo, dqs[0], dqs[1])
                dkp = kcar[:, sl] + dk2[:BLK]
                dvp = vcar[:, sl] + dv2[:BLK]
                if has_acc:
                    dq = dq + aq_ref[:, sl]
                    dkp = dkp + ak_ref[:, sl]
                    dvp = dvp + av_ref[:, sl]
                dq_ref[:, sl] = dq
                dk_ref[:, sl] = dkp
                dv_ref[:, sl] = dvp
                kcar[:, sl] = dk2[BLK:]
                vcar[:, sl] = dv2[BLK:]

        @pl.when(n == nb)
        def _():
            dkp = kcar[...]
            dvp = vcar[...]
            if has_acc:
                dkp = dkp + ak_ref[...]
                dvp = dvp + av_ref[...]
            dk_ref[...] = dkp
            dv_ref[...] = dvp

    cur = pl.BlockSpec((BLK, ATT_W), lambda r, n: (jnp.minimum(n, nb - 1), r))
    prev = pl.BlockSpec((BLK, ATT_W), lambda r, n: (jnp.maximum(n - 1, 0), r))
    in_specs = _dil_specs(dil, nb, True) + [
        pl.BlockSpec((None, ATT_HEADS, BLK, 2 * BLK), lambda r, n: (branch, 0, 0, 0)), cur, cur, cur]
    args = [pv, pv, pv, pv, pv, biasm, view(lse), view(att), view(datt)]
    if has_acc:
        in_specs += [cur, prev, prev]
        args += [view(t) for t in acc]
    dq, dk, dv, dbias = pl.pallas_call(
        body, name=name, grid=(dil, nb + 1),
        in_specs=in_specs,
        out_specs=[cur, prev, prev, pl.BlockSpec((ATT_HEADS, BLK, 2 * BLK), lambda r, n: (0, 0, 0))],
        out_shape=[jax.ShapeDtypeStruct((m, dil * ATT_W), F32)] * 3
        + [jax.ShapeDtypeStruct((ATT_HEADS, BLK, 2 * BLK), F32)],
        scratch_shapes=[pltpu.VMEM((BLK, ATT_W), F32), pltpu.VMEM((BLK, ATT_W), F32)],
        compiler_params=_params("arbitrary", "arbitrary"),
    )(*args)
    return (dq.reshape(s, ATT_W), dk.reshape(s, ATT_W), dv.reshape(s, ATT_W)), dbias


QK_COL0 = (3 * ATT_W) // ATT_W


def _conv_shifted(prev, cur, j, row):
    sh = CONV_K - 1 - j
    if sh == 0:
        return cur
    return jnp.where(row < sh, pltpu.roll(prev, sh, 0), pltpu.roll(cur, sh, 0))


def _conv_z(prev, cur, w_ref, b_ref, row):
    z = b_ref[...] + cur * w_ref[CONV_K - 1:CONV_K, :]
    for j in range(CONV_K - 1):
        z = z + _conv_shifted(prev, cur, j, row) * w_ref[j:j + 1, :]
    return z


def _conv_fwd(proj, conv_w, conv_b, name="conv_fwd", tm=512):
    s = proj.shape[0]
    w = ATT_W

    def body(prev_ref, cur_ref, w_ref, b_ref, o_ref):
        i = pl.program_id(1)
        row = lax.broadcasted_iota(jnp.int32, (tm, w), 0)
        prev = jnp.where(i > 0, prev_ref[...], 0.0)
        z = _conv_z(prev, cur_ref[...], w_ref, b_ref, row)
        o_ref[...] = z * _sigmoid(z)

    return pl.pallas_call(
        body, name=name, grid=(2, s // tm),
        in_specs=[pl.BlockSpec((tm, w), lambda j, i: (jnp.maximum(i - 1, 0), QK_COL0 + j)),
                  pl.BlockSpec((tm, w), lambda j, i: (i, QK_COL0 + j)),
                  pl.BlockSpec((CONV_K, w), lambda j, i: (0, j)),
                  pl.BlockSpec((1, w), lambda j, i: (0, j))],
        out_specs=pl.BlockSpec((tm, w), lambda j, i: (i, j)),
        out_shape=jax.ShapeDtypeStruct((s, 2 * ML_W), F32),
        compiler_params=_params("parallel", "parallel"),
    )(proj, proj, conv_w, conv_b)


def _conv_bwd(proj, dqk, conv_w, conv_b, name="conv_bwd", tm=512):
    s = proj.shape[0]
    w = ATT_W
    nt = s // tm

    def body(xp_ref, xc_ref, xn_ref, dc_ref, dn_ref, w_ref, b_ref, dx_ref, dw_ref, db_ref):
        i = pl.program_id(1)
        row = lax.broadcasted_iota(jnp.int32, (tm, w), 0)
        prev = jnp.where(i > 0, xp_ref[...], 0.0)
        cur = xc_ref[...]

        def dz_of(pv, cv, dy):
            z = _conv_z(pv, cv, w_ref, b_ref, row)
            sig = _sigmoid(z)
            return dy * (sig * (1.0 + z * (1.0 - sig)))

        dzc = dz_of(prev, cur, dc_ref[...])
        dzn = jnp.where(i < nt - 1, dz_of(cur, xn_ref[...], dn_ref[...]), 0.0)
        dx = dzc * w_ref[CONV_K - 1:CONV_K, :]
        for j in range(CONV_K - 1):
            sh = CONV_K - 1 - j
            up = jnp.where(row >= tm - sh, pltpu.roll(dzn, tm - sh, 0), pltpu.roll(dzc, tm - sh, 0))
            dx = dx + up * w_ref[j:j + 1, :]
        dx_ref[...] = dx

        @pl.when(i == 0)
        def _():
            dw_ref[...] = jnp.zeros_like(dw_ref)
            db_ref[...] = jnp.zeros_like(db_ref)

        for j in range(CONV_K):
            dw_ref[j:j + 1, :] += jnp.sum(dzc * _conv_shifted(prev, cur, j, row), axis=0, keepdims=True)
        db_ref[...] += jnp.sum(dzc, axis=0, keepdims=True)

    xs = lambda f: pl.BlockSpec((tm, w), lambda j, i: (f(i), QK_COL0 + j))
    ds = lambda f: pl.BlockSpec((tm, w), lambda j, i: (f(i), j))
    return pl.pallas_call(
        body, name=name, grid=(2, nt),
        in_specs=[xs(lambda i: jnp.maximum(i - 1, 0)), xs(lambda i: i), xs(lambda i: jnp.minimum(i + 1, nt - 1)),
                  ds(lambda i: i), ds(lambda i: jnp.minimum(i + 1, nt - 1)),
                  pl.BlockSpec((CONV_K, w), lambda j, i: (0, j)), pl.BlockSpec((1, w), lambda j, i: (0, j))],
        out_specs=[ds(lambda i: i), pl.BlockSpec((CONV_K, w), lambda j, i: (0, j)),
                   pl.BlockSpec((1, w), lambda j, i: (0, j))],
        out_shape=[jax.ShapeDtypeStruct((s, 2 * ML_W), F32), jax.ShapeDtypeStruct((CONV_K, 2 * ML_W), F32),
                   jax.ShapeDtypeStruct((1, 2 * ML_W), F32)],
        compiler_params=_params("parallel", "arbitrary"),
    )(proj, proj, proj, dqk, dqk, conv_w, conv_b)


def _bf16_mm(dims_fwd):
    @jax.custom_vjp
    def mm(a, b):
        return _dot(a.astype(BF16), b.astype(BF16), dims_fwd)

    def fwd(a, b):
        return mm(a, b), (a, b)

    def bwd(res, g):
        a, b = res
        if dims_fwd is NN:
            return _mm_nt(g, b), _mm_tn(a, g)
        if dims_fwd is NT:
            return _mm_nn(g, b), _mm_tn(g, a)
        return _mm_nt(b, g), _mm_nn(a, g)

    mm.defvjp(fwd, bwd)
    return mm


_mm_nn = _bf16_mm(NN)
_mm_nt = _bf16_mm(NT)
_mm_tn = _bf16_mm(TN)


def _tri(lower):
    r = lax.broadcasted_iota(jnp.int32, (CHUNK, CHUNK), 0)
    c = lax.broadcasted_iota(jnp.int32, (CHUNK, CHUNK), 1)
    return ((r >= c) if lower else (r <= c)).astype(F32)


@jax.custom_vjp
def _cumsum_rows(x):
    return lax.dot_general(_tri(True), x, NN, precision=lax.Precision.HIGHEST, preferred_element_type=F32)


def _cumsum_fwd(x):
    return _cumsum_rows(x), None


def _cumsum_bwd(_, g):
    return (lax.dot_general(_tri(False), g, NN, precision=lax.Precision.HIGHEST, preferred_element_type=F32),)


_cumsum_rows.defvjp(_cumsum_fwd, _cumsum_bwd)


def _abs(x):
    return jnp.where(x >= 0, x, -x)


def _log_sigmoid(x):
    return jnp.minimum(x, 0.0) - jnp.log(1.0 + jnp.exp(-_abs(x)))


def _pick_col(x, lane):
    sel = lax.broadcasted_iota(jnp.int32, x.shape, 1) == lane
    return jnp.sum(jnp.where(sel, x, 0.0), axis=1, keepdims=True)


def _pick_row(x, r):
    sel = lax.broadcasted_iota(jnp.int32, x.shape, 0) == r
    return jnp.sum(jnp.where(sel, x, 0.0), axis=0, keepdims=True)


def _mlstm_chunk(qs, ks, vs, oms, gates, gate_bias, mlg, cs, ns, ms):
    gb = gates + gate_bias
    cum = _cumsum_rows(_log_sigmoid(gb))
    gbt = gb.T
    cumt = cum.T
    causal = lax.broadcasted_iota(jnp.int32, (CHUNK, CHUNK), 0) >= lax.broadcasted_iota(jnp.int32, (CHUNK, CHUNK), 1)
    ys, c_out, n_out, m_out = [], [], [], []
    for h in range(ML_HEADS):
        q, v, om, c, n, m = qs[h], vs[h], oms[h], cs[h], ns[h], ms[h]
        k = ks[h] * (ML_HD ** -0.5)
        ig_col = _pick_col(gb, h)
        ig_row = _pick_row(gbt, h)
        b_col = _pick_col(cum, ML_HEADS + h)
        b_row = _pick_row(cumt, ML_HEADS + h)
        g = _pick_row(b_col, CHUNK - 1)
        a = g - b_col + ig_col
        m_loc = jnp.max(a, axis=0, keepdims=True)
        wa = jnp.exp(a - m_loc)
        c_loc = _mm_tn(wa * v, k)
        n_loc = jnp.sum(wa * k, axis=0, keepdims=True)
        m_new = jnp.maximum(g + m, m_loc)
        sp = jnp.exp(g + m - m_new)
        sl = jnp.exp(m_loc - m_new)
        c_out.append(sp * c + sl * c_loc)
        n_out.append(sp * n + sl * n_loc)
        m_out.append(m_new)
        d_log = jnp.where(causal, b_col - b_row + ig_row, -jnp.inf)
        e_log = b_col + m
        m_t = jnp.maximum(e_log, jnp.max(d_log, axis=1, keepdims=True))
        d_w = jnp.exp(d_log - m_t)
        e_w = jnp.exp(e_log - m_t)
        s_qk = _mm_nt(q, k) * d_w
        num = e_w * _mm_nt(q, c) + _mm_nn(s_qk, v)
        den = e_w * jnp.sum(q * n, axis=1, keepdims=True) + jnp.sum(s_qk, axis=1, keepdims=True)
        hh = num / jnp.maximum(_abs(den), jnp.exp(-m_t))
        hg = _sigmoid(om) * hh
        mu = jnp.mean(hg, axis=1, keepdims=True)
        hc = hg - mu
        var = jnp.mean(hc * hc, axis=1, keepdims=True)
        ys.append(hc * lax.rsqrt(var + LN_EPS) * mlg[h])
    return ys, c_out, n_out, m_out


V_COL = 5
O_COL = 6


def _mlstm_fwd(qk, proj, gates, gate_bias, mlg, name="mlstm_fwd", gather=()):
    s = qk.shape[0]
    nc = s // CHUNK

    def body(q_ref, k_ref, v_ref, o_ref, g_ref, gb_ref, mlg_ref, y_ref, cp_ref, np_ref, mp_ref, c_s, n_s, m_s):
        ci = pl.program_id(0)

        @pl.when(ci == 0)
        def _():
            c_s[...] = jnp.zeros_like(c_s)
            n_s[...] = jnp.zeros_like(n_s)
            m_s[...] = jnp.zeros_like(m_s)

        cp_ref[...] = c_s[...]
        np_ref[...] = n_s[...]
        mp_ref[...] = m_s[...]
        hs = lambda ref: [ref[:, LANES * h:LANES * (h + 1)] for h in range(ML_HEADS)]
        ys, c_new, n_new, m_new = _mlstm_chunk(
            hs(q_ref), hs(k_ref), hs(v_ref), hs(o_ref), g_ref[...], gb_ref[...], hs(mlg_ref),
            [c_s[h] for h in range(ML_HEADS)], [n_s[h:h + 1, :] for h in range(ML_HEADS)],
            [m_s[h:h + 1, 0:1] for h in range(ML_HEADS)])
        for h in range(ML_HEADS):
            y_ref[:, LANES * h:LANES * (h + 1)] = ys[h]
            c_s[h] = c_new[h]
            n_s[h:h + 1, :] = n_new[h]
            m_s[h:h + 1, :] = jnp.broadcast_to(m_new[h], (1, LANES))

    blk = lambda col: pl.BlockSpec((CHUNK, ML_W), lambda ci: (ci, col))
    vec = lambda w: pl.BlockSpec((1, w), lambda ci: (0, 0))
    return _call(
        body, name=name, grid=(nc,), args=(qk, qk, proj, proj, gates, gate_bias, mlg), sem=("arbitrary",), gather=gather,
        in_specs=[blk(0), blk(1), blk(V_COL), blk(O_COL), pl.BlockSpec((CHUNK, LANES), lambda ci: (ci, 0)),
                  vec(LANES), vec(ML_W)],
        out_specs=[blk(0), pl.BlockSpec((None, ML_HEADS, ML_HD, ML_HD), lambda ci: (ci, 0, 0, 0)),
                   pl.BlockSpec((None, 8, LANES), lambda ci: (ci, 0, 0)),
                   pl.BlockSpec((None, 8, LANES), lambda ci: (ci, 0, 0))],
        out_shape=[jax.ShapeDtypeStruct((s, ML_W), F32), jax.ShapeDtypeStruct((nc, ML_HEADS, ML_HD, ML_HD), F32),
                   jax.ShapeDtypeStruct((nc, 8, LANES), F32), jax.ShapeDtypeStruct((nc, 8, LANES), F32)],
        scratch_shapes=[pltpu.VMEM((ML_HEADS, ML_HD, ML_HD), F32), pltpu.VMEM((8, LANES), F32),
                        pltpu.VMEM((8, LANES), F32)])


def _mlstm_bwd(qk, proj, gates, gate_bias, mlg, cprev, nprev, mprev, dy, name="mlstm_bwd", exchange=()):
    s = qk.shape[0]
    nc = s // CHUNK

    def body(q_ref, k_ref, v_ref, o_ref, g_ref, gb_ref, mlg_ref, cp_ref, np_ref, mp_ref, dy_ref,
             dqk_ref, dv_ref, do_ref, dg_ref, dgb_ref, dmlg_ref, dc_s, dn_s, dm_s, gb8, mg8):
        ci = pl.program_id(0)

        @pl.when(ci == 0)
        def _():
            dc_s[...] = jnp.zeros_like(dc_s)
            dn_s[...] = jnp.zeros_like(dn_s)
            dm_s[...] = jnp.zeros_like(dm_s)
            gb8[...] = jnp.zeros_like(gb8)
            mg8[...] = jnp.zeros_like(mg8)

        hs = lambda ref: [ref[:, LANES * h:LANES * (h + 1)] for h in range(ML_HEADS)]
        prim = (hs(q_ref), hs(k_ref), hs(v_ref), hs(o_ref), g_ref[...], gb_ref[...], hs(mlg_ref),
                [cp_ref[h] for h in range(ML_HEADS)], [np_ref[h:h + 1, :] for h in range(ML_HEADS)],
                [mp_ref[h:h + 1, 0:1] for h in range(ML_HEADS)])
        _, vjp = jax.vjp(_mlstm_chunk, *prim)
        cot = (hs(dy_ref), [dc_s[h] for h in range(ML_HEADS)], [dn_s[h:h + 1, :] for h in range(ML_HEADS)],
               [dm_s[h:h + 1, 0:1] for h in range(ML_HEADS)])
        dqs, dks, dvs, dos, dg, dgb, dmlg, dcs, dns, dms = vjp(cot)
        dg_ref[...] = dg
        gb8[0:1, :] += dgb
        for h in range(ML_HEADS):
            sl = slice(LANES * h, LANES * (h + 1))
            dqk_ref[:, sl] = dqs[h]
            dqk_ref[:, ML_W + LANES * h:ML_W + LANES * (h + 1)] = dks[h]
            dv_ref[:, sl] = dvs[h]
            do_ref[:, sl] = dos[h]
            mg8[0:1, sl] += dmlg[h]
            dc_s[h] = dcs[h]
            dn_s[h:h + 1, :] = dns[h]
            dm_s[h:h + 1, :] = jnp.broadcast_to(dms[h], (1, LANES))

        @pl.when(ci == nc - 1)
        def _():
            dgb_ref[...] = gb8[0:1, :]
            dmlg_ref[...] = mg8[0:1, :]

    rev = lambda ci: nc - 1 - ci
    blk = lambda col: pl.BlockSpec((CHUNK, ML_W), lambda ci: (rev(ci), col))
    vec = lambda w: pl.BlockSpec((1, w), lambda ci: (0, 0))
    st8 = pl.BlockSpec((None, 8, LANES), lambda ci: (rev(ci), 0, 0))
    gsp = pl.BlockSpec((CHUNK, LANES), lambda ci: (rev(ci), 0))
    return _call(
        body, name=name, grid=(nc,), sem=("arbitrary",), exchange=exchange,
        args=(qk, qk, proj, proj, gates, gate_bias, mlg, cprev, nprev, mprev, dy),
        in_specs=[blk(0), blk(1), blk(V_COL), blk(O_COL), gsp, vec(LANES), vec(ML_W),
                  pl.BlockSpec((None, ML_HEADS, ML_HD, ML_HD), lambda ci: (rev(ci), 0, 0, 0)), st8, st8, blk(1)],
        out_specs=[pl.BlockSpec((CHUNK, 2 * ML_W), lambda ci: (rev(ci), 0)), blk(0), blk(0), gsp, vec(LANES), vec(ML_W)],
        out_shape=[jax.ShapeDtypeStruct((s, 2 * ML_W), F32),
                   jax.ShapeDtypeStruct((s, ML_W), F32), jax.ShapeDtypeStruct((s, ML_W), F32),
                   jax.ShapeDtypeStruct((s, LANES), F32), jax.ShapeDtypeStruct((1, LANES), F32),
                   jax.ShapeDtypeStruct((1, ML_W), F32)],
        scratch_shapes=[pltpu.VMEM((ML_HEADS, ML_HD, ML_HD), F32), pltpu.VMEM((8, LANES), F32),
                        pltpu.VMEM((8, LANES), F32), pltpu.VMEM((8, LANES), F32), pltpu.VMEM((8, ML_W), F32)])


def _xattn_tile(qs, ks, vs):
    outs = []
    for q, k, v in zip(qs, ks, vs):
        sc = _mm_nt(q, k) * (XA_HD ** -0.5)
        mx = lax.stop_gradient(jnp.max(sc, axis=1, keepdims=True))
        pe = jnp.exp(sc - mx)
        outs.append(_mm_nn(pe / jnp.sum(pe, axis=1, keepdims=True), v))
    return outs


def _xa_heads(ref):
    return [ref[:, XA_HD * h:XA_HD * (h + 1)] for h in range(XA_HEADS)]


def _xattn_fwd(q, kv, name="xattn_fwd", tm=512):
    s, d = q.shape

    def body(q_ref, k_ref, v_ref, o_ref):
        outs = _xattn_tile(_xa_heads(q_ref), _xa_heads(k_ref), _xa_heads(v_ref))
        for h in range(XA_HEADS):
            o_ref[:, XA_HD * h:XA_HD * (h + 1)] = outs[h]

    row = pl.BlockSpec((tm, d), lambda i: (i, 0))
    return pl.pallas_call(
        body, name=name, grid=(s // tm,),
        in_specs=[row, pl.BlockSpec((MEM_LEN, d), lambda i: (0, 0)), pl.BlockSpec((MEM_LEN, d), lambda i: (0, 1))],
        out_specs=row, out_shape=jax.ShapeDtypeStruct((s, d), F32),
        compiler_params=_params("parallel"),
    )(q, kv, kv)


def _xattn_bwd(q, kv, do, name="xattn_bwd", tm=512):
    s, d = q.shape

    def body(q_ref, k_ref, v_ref, do_ref, dq_ref, dkv_ref):
        i = pl.program_id(0)
        _, vjp = jax.vjp(_xattn_tile, _xa_heads(q_ref), _xa_heads(k_ref), _xa_heads(v_ref))
        dqs, dks, dvs = vjp(_xa_heads(do_ref))

        @pl.when(i == 0)
        def _():
            dkv_ref[...] = jnp.zeros_like(dkv_ref)

        for h in range(XA_HEADS):
            sl = slice(XA_HD * h, XA_HD * (h + 1))
            dq_ref[:, sl] = dqs[h]
            dkv_ref[:, sl] += dks[h]
            dkv_ref[:, d + XA_HD * h:d + XA_HD * (h + 1)] += dvs[h]

    row = pl.BlockSpec((tm, d), lambda i: (i, 0))
    return pl.pallas_call(
        body, name=name, grid=(s // tm,),
        in_specs=[row, pl.BlockSpec((MEM_LEN, d), lambda i: (0, 0)), pl.BlockSpec((MEM_LEN, d), lambda i: (0, 1)), row],
        out_specs=[row, pl.BlockSpec((MEM_LEN, 2 * d), lambda i: (0, 0))],
        out_shape=[jax.ShapeDtypeStruct((s, d), F32), jax.ShapeDtypeStruct((MEM_LEN, 2 * d), F32)],
        compiler_params=_params("arbitrary"),
    )(q, kv, kv, do)


def _loss_head(y, target, name="loss_head", tm=512):
    s, d = y.shape
    nt = s // tm

    def body(y_ref, t_ref, dy_ref, loss_ref, acc):
        i = pl.program_id(0)
        err = y_ref[...] - t_ref[...]
        dy_ref[...] = err * (1.0 / d)

        @pl.when(i == 0)
        def _():
            acc[...] = jnp.zeros_like(acc)

        acc[...] += _rowsum8(err * err)

        @pl.when(i == nt - 1)
        def _():
            tot = jnp.sum(jnp.sum(acc[...], axis=0, keepdims=True), axis=1, keepdims=True)
            loss_ref[...] = jnp.broadcast_to(tot * (0.5 / d), (1, LANES))

    row = pl.BlockSpec((tm, d), lambda i: (i, 0))
    return pl.pallas_call(
        body, name=name, grid=(nt,),
        in_specs=[row, row], out_specs=[row, pl.BlockSpec((1, LANES), lambda i: (0, 0))],
        out_shape=[jax.ShapeDtypeStruct((s, d), F32), jax.ShapeDtypeStruct((1, LANES), F32)],
        scratch_shapes=[pltpu.VMEM((8, d), F32)],
        compiler_params=_params("arbitrary"),
    )(y, target)


def _adam2d(recv, w, m, v, name, layer=None):
    rows, cols = w.shape[-2:]
    fits = [t for t in range(16, rows + 1, 16) if rows % t == 0 and t * cols <= 128 * 1024]
    tr = max(fits) if fits else rows

    def body(r_ref, w_ref, m_ref, v_ref, g_ref, d_ref, mo_ref, vo_ref):
        g = r_ref[0].astype(F32)
        for j in range(1, N_DEV):
            g = g + r_ref[j].astype(F32)
        mn = ADAM_B1 * m_ref[...] + (1.0 - ADAM_B1) * g
        vn = ADAM_B2 * v_ref[...] + (1.0 - ADAM_B2) * jnp.square(g)
        m_hat = mn / (1.0 - ADAM_B1 ** ADAM_STEP)
        v_hat = vn / (1.0 - ADAM_B2 ** ADAM_STEP)
        g_ref[...] = g
        d_ref[...] = -ADAM_LR * (m_hat / (jnp.sqrt(v_hat) + ADAM_EPS) + ADAM_WD * w_ref[...])
        mo_ref[...] = mn
        vo_ref[...] = vn

    row = pl.BlockSpec((tr, cols), lambda i: (i, 0))
    if layer is None:
        wspec = row
    else:
        wspec = pl.BlockSpec((None, None, tr, cols), lambda i: (0, layer, i, 0))
    return pl.pallas_call(
        body, name=name, grid=(rows // tr,),
        in_specs=[pl.BlockSpec((N_DEV, tr, cols), lambda i: (0, i, 0)), wspec, wspec, wspec],
        out_specs=[row] * 4, out_shape=[jax.ShapeDtypeStruct((rows, cols), F32)] * 4,
        compiler_params=_params("parallel"),
    )(recv, w, m, v)


WEIGHTS = ("rel_bias", "ln_g", "ln_b", "ffn_w_gate", "ffn_w_up", "ffn_w_down", "w_in", "conv_w", "conv_b",
           "ig_bias", "fg_bias", "ml_norm_g", "w_out", "xq_w", "xkv_w", "xo_w")
SMALL = ("rel_bias", "ln_g", "ln_b", "conv_w", "conv_b", "ig_bias", "fg_bias", "ml_norm_g")
SMALL_SHAPES = {
    "rel_bias": (REL_BUCKETS, ATT_HEADS), "ln_g": (1, 4, LANES), "ln_b": (1, 4, LANES), "conv_w": (1, CONV_K, LANES),
    "conv_b": (1, 2 * ML_W), "ig_bias": (1, ML_HEADS), "fg_bias": (1, ML_HEADS), "ml_norm_g": (1, ML_W),
}
SMALL_ROWS = 8


def _pack_small(parts, lead=()):
    out = []
    for p in parts:
        p = jnp.pad(p, [(0, 0)] * len(lead) + [(0, SMALL_ROWS * LANES - p.shape[-1])])
        out.append(p.reshape(lead + (SMALL_ROWS, LANES)))
    return jnp.concatenate(out, axis=len(lead))


def _unpack_small(flat):
    out = {}
    for i, n in enumerate(SMALL):
        cnt = int(np.prod(SMALL_SHAPES[n]))
        out[n] = flat[SMALL_ROWS * i:SMALL_ROWS * (i + 1)].reshape(-1)[:cnt].reshape(SMALL_SHAPES[n])
    return out


def _split8(full, axis):
    shp = full.shape
    t = full.reshape(shp[:axis] + (N_DEV, shp[axis] // N_DEV) + shp[axis + 1:])
    return jnp.moveaxis(t, axis, 0).reshape(N_DEV, -1)


def _rep8(full):
    return jnp.broadcast_to(full.reshape(1, -1), (N_DEV, full.size))


def kernel(x, mem, rel_bias, ln_g, ln_b, ffn_w_gate, ffn_w_up, ffn_w_down, w_in, conv_w, conv_b, ig_bias, fg_bias, ml_norm_g, w_out, xq_w, xkv_w, xo_w, loss_target, m_rel_bias, m_ln_g, m_ln_b, m_ffn_w_gate, m_ffn_w_up, m_ffn_w_down, m_w_in, m_conv_w, m_conv_b, m_ig_bias, m_fg_bias, m_ml_norm_g, m_w_out, m_xq_w, m_xkv_w, m_xo_w, v_rel_bias, v_ln_g, v_ln_b, v_ffn_w_gate, v_ffn_w_up, v_ffn_w_down, v_w_in, v_conv_w, v_conv_b, v_ig_bias, v_fg_bias, v_ml_norm_g, v_w_out, v_xq_w, v_xkv_w, v_xo_w):
    w_tree = dict(rel_bias=rel_bias, ln_g=ln_g, ln_b=ln_b, ffn_w_gate=ffn_w_gate, ffn_w_up=ffn_w_up,
                  ffn_w_down=ffn_w_down, w_in=w_in, conv_w=conv_w, conv_b=conv_b, ig_bias=ig_bias, fg_bias=fg_bias,
                  ml_norm_g=ml_norm_g, w_out=w_out, xq_w=xq_w, xkv_w=xkv_w, xo_w=xo_w)
    m_tree = dict(rel_bias=m_rel_bias, ln_g=m_ln_g, ln_b=m_ln_b, ffn_w_gate=m_ffn_w_gate, ffn_w_up=m_ffn_w_up,
                  ffn_w_down=m_ffn_w_down, w_in=m_w_in, conv_w=m_conv_w, conv_b=m_conv_b, ig_bias=m_ig_bias,
                  fg_bias=m_fg_bias, ml_norm_g=m_ml_norm_g, w_out=m_w_out, xq_w=m_xq_w, xkv_w=m_xkv_w, xo_w=m_xo_w)
    v_tree = dict(rel_bias=v_rel_bias, ln_g=v_ln_g, ln_b=v_ln_b, ffn_w_gate=v_ffn_w_gate, ffn_w_up=v_ffn_w_up,
                  ffn_w_down=v_ffn_w_down, w_in=v_w_in, conv_w=v_conv_w, conv_b=v_conv_b, ig_bias=v_ig_bias,
                  fg_bias=v_fg_bias, ml_norm_g=v_ml_norm_g, w_out=v_w_out, xq_w=v_xq_w, xkv_w=v_xkv_w, xo_w=v_xo_w)
    x0 = x[0]
    pad_ff = FF_PAD - FF_SHARD
    bf = lambda t: t.astype(BF16)

    ffn_shards = [(jnp.pad(bf(ffn_w_gate[0, l]), ((0, 0), (0, pad_ff))), jnp.pad(bf(ffn_w_up[0, l]), ((0, 0), (0, pad_ff))),
                   jnp.pad(bf(ffn_w_down[0, l]), ((0, pad_ff), (0, 0)))) for l in range(2)]
    w_in_shard = jnp.pad(bf(w_in[0]), ((0, 0), (0, ATT_W - W_IN_SHARD)))
    small_shard = jnp.concatenate([ln_g[0], ln_b[0], conv_w[0], jnp.zeros((4, LANES), F32)], axis=0)
    gate_bias = jnp.pad(jnp.concatenate([ig_bias, fg_bias], axis=1), ((0, 0), (0, LANES - 2 * ML_HEADS)))
    buckets = _bucket_tables()

    wg0, wu0, wd0, small_all = _exchange_only("ffn1_weights_gather", gather=ffn_shards[0] + (small_shard,))
    unshard = lambda t: jnp.moveaxis(t, 0, 1).reshape(4, D_MODEL)
    ln_g_full, ln_b_full, conv_w_full = unshard(small_all[:, 0:4]), unshard(small_all[:, 4:8]), unshard(small_all[:, 8:12])
    lng = lambda i: ln_g_full[i:i + 1]
    lnb = lambda i: ln_b_full[i:i + 1]

    u0, x1, win_all, wout_all, xq_all, xo_all, xkv_all = _ffn_fwd(
        x0, wg0, wu0, wd0, lng(0), lnb(0), "ffn1_fwd",
        gather=(w_in_shard, bf(w_out[0]), bf(xq_w[0]), bf(xo_w[0]), bf(xkv_w[0])))
    w_in_full = jnp.moveaxis(win_all[:, :, :W_IN_SHARD], 0, 1).reshape(D_MODEL, W_IN)
    w_main = w_in_full[:, :W_IN_MAIN]
    w_gate_cols = jnp.pad(w_in_full[:, W_IN_MAIN:], ((0, 0), (0, LANES - 2 * ML_HEADS)))
    w_out_full = wout_all.reshape(D_MODEL, D_MODEL)
    xq_full = xq_all.reshape(D_MODEL, D_MODEL)
    xo_full = xo_all.reshape(D_MODEL, D_MODEL)

    proj, wg1 = _matmul(x1, w_main, "nn", "proj_fwd", tk=D_MODEL, gather=(ffn_shards[1][0],))
    gates, = _matmul(x1, w_gate_cols, "nn", "gates_fwd", tk=D_MODEL)
    biasm = _bias_fwd(rel_bias, buckets)
    o_0, l_0, wd1 = _dil_fwd(proj, biasm, 0, "dil_fwd_0", gather=(ffn_shards[1][2],))
    o_1, l_1 = _dil_fwd(proj, biasm, 1, "dil_fwd_1")
    o_2, l_2 = _dil_fwd(proj, biasm, 2, "dil_fwd_2")
    att, lse = _dil_combine([o_0, o_1, o_2], [l_0, l_1, l_2])
    qk = _conv_fwd(proj, conv_w_full, conv_b)
    y_m, c_prev, n_prev, m_prev, wu1 = _mlstm_fwd(qk, proj, gates, gate_bias, ml_norm_g, gather=(ffn_shards[1][1],))
    cat = jnp.concatenate([att, y_m], axis=1)
    f1, = _matmul(cat, w_out_full, "nn", "w_out_fwd", tn=D_MODEL, tk=D_MODEL)
    u1, x2 = _resid_ln(x1, f1, lng(1), lnb(1), "mixer_ln")
    q_x, = _matmul(x2, xq_full, "nn", "xq_fwd", tn=D_MODEL, tk=D_MODEL)
    kv, = _matmul(mem[0], xkv_all, "nn", "xkv_fwd", tk=D_MODEL)
    o_x = _xattn_fwd(q_x, kv)
    f2, = _matmul(o_x, xo_full, "nn", "xo_fwd", tn=D_MODEL, tk=D_MODEL)
    u2, x3 = _resid_ln(x2, f2, lng(2), lnb(2), "xattn_ln")
    u3, x4 = _ffn_fwd(x3, wg1, wu1, wd1, lng(3), lnb(3), "ffn2_fwd")
    dx4, loss_row = _loss_head(x4, loss_target[0])

    dx3, df, da, db, hh, dg3, db3 = _ffn_bwd_x(dx4, u3, x3, wg1, wu1, wd1, lng(3), "ffn2_bwd_x")
    ffn2_send = _ffn_bwd_w(x3, df, da, db, hh, "ffn2_bwd_w")

    du2, dg2, db2 = _ln_bwd(dx3, u2, lng(2), "xattn_ln_bwd")
    do_x, = _matmul(du2, xo_full, "nt", "xo_bwd_x", tn=D_MODEL)
    g_xo, = _matmul(o_x, du2, "tn", "xo_bwd_w", tm=D_MODEL, out_dtype=BF16)
    dq_x, dkv = _xattn_bwd(q_x, kv, do_x)
    g_xq, = _matmul(x2, dq_x, "tn", "xq_bwd_w", tm=D_MODEL, out_dtype=BF16)
    g_xkv, = _matmul(mem[0], dkv, "tn", "xkv_bwd_w", tm=D_MODEL, tn=2 * D_MODEL // N_DEV, tk=MEM_LEN,
                     out_dtype=BF16, blocked_out=True)
    dx2, = _matmul(dq_x, xq_full, "nt", "xq_bwd_x", tn=D_MODEL, add=du2, add_scale=ALPHA)

    du1, dg1, db1 = _ln_bwd(dx2, u1, lng(1), "mixer_ln_bwd")
    dcat, = _matmul(du1, w_out_full, "nt", "w_out_bwd_x", tn=D_MODEL)
    g_w_out, = _matmul(cat, du1, "tn", "w_out_bwd_w", tm=D_MODEL, out_dtype=BF16)
    dqk, dv_m, do_m, dgates, dgate_bias, g_mlg, *ffn2_recv = _mlstm_bwd(
        qk, proj, gates, gate_bias, ml_norm_g, c_prev, n_prev, m_prev, dcat, exchange=tuple(ffn2_send))
    dqk_pre, g_conv_w, g_conv_b = _conv_bwd(proj, dqk, conv_w_full, conv_b)
    datt = dcat[:, :ATT_W]
    acc, dbias = None, []
    for b in range(len(DILATED)):
        acc, dbb = _dil_bwd(proj, biasm, lse, att, datt, acc, b, f"dil_bwd_{b}")
        dbias.append(dbb)
    g_rel = _bias_bwd(jnp.stack(dbias), buckets)[:, :ATT_HEADS]
    dproj = jnp.concatenate([acc[0], acc[1], acc[2], dqk_pre, dv_m, do_m], axis=1)
    g_w_main, = _matmul(x1, dproj, "tn", "proj_bwd_w", tm=D_MODEL, out_dtype=BF16)
    g_w_gates, = _matmul(x1, dgates, "tn", "gates_bwd_w", tm=D_MODEL, out_dtype=BF16)
    g_w_in = jnp.concatenate([g_w_main, g_w_gates[:, :2 * ML_HEADS]], axis=1)
    dx1, = _matmul(dproj, w_main, "nt", "proj_bwd_x", tn=D_MODEL, add=du1, add_scale=ALPHA)
    dx1, = _matmul(dgates, w_gate_cols, "nt", "gates_bwd_x", tn=D_MODEL, add=dx1)

    rows8 = lambda t: t.reshape(N_DEV, D_MODEL // N_DEV, D_MODEL)
    mid_send = (rows8(g_xo), rows8(g_xq), g_xkv, rows8(g_w_out),
                jnp.moveaxis(g_w_in.reshape(D_MODEL, N_DEV, W_IN_SHARD), 1, 0))
    dx0, df, da, db, hh, dg0, db0, r_xo, r_xq, r_xkv, r_w_out, r_w_in = _ffn_bwd_x(
        dx1, u0, x0, wg0, wu0, wd0, lng(0), "ffn1_bwd_x", exchange=mid_send)
    ffn1_send = _ffn_bwd_w(x0, df, da, db, hh, "ffn1_bwd_w")
    small_blocks = {
        "rel_bias": _rep8(g_rel),
        "ln_g": _split8(jnp.concatenate([dg0, dg1, dg2, dg3], axis=0), 1),
        "ln_b": _split8(jnp.concatenate([db0, db1, db2, db3], axis=0), 1),
        "conv_w": _split8(g_conv_w, 1),
        "conv_b": _rep8(g_conv_b),
        "ig_bias": _rep8(dgate_bias[:, :ML_HEADS]),
        "fg_bias": _rep8(dgate_bias[:, ML_HEADS:2 * ML_HEADS]),
        "ml_norm_g": _rep8(g_mlg),
    }
    small_send = _pack_small([small_blocks[n] for n in SMALL], lead=(N_DEV,))
    *ffn1_recv, r_small = _exchange_only("ffn1_grads_exchange", exchange=tuple(ffn1_send) + (small_send,))

    res = {}
    for i, n in enumerate(("ffn_w_gate", "ffn_w_up", "ffn_w_down")):
        per_layer = [_adam2d(r[i], w_tree[n], m_tree[n], v_tree[n], f"adamw_{n}_{l}", layer=l)
                     for l, r in enumerate((ffn1_recv, ffn2_recv))]
        res[n] = [jnp.stack([per_layer[0][j], per_layer[1][j]])[None] for j in range(4)]
    for n, r in (("w_in", r_w_in), ("w_out", r_w_out), ("xq_w", r_xq), ("xkv_w", r_xkv), ("xo_w", r_xo)):
        res[n] = [t[None] for t in _adam2d(r, w_tree[n][0], m_tree[n][0], v_tree[n][0], f"adamw_{n}")]
    pack = lambda tree: _pack_small([tree[n].reshape(-1) for n in SMALL])
    small = [_unpack_small(t) for t in _adam2d(r_small, pack(w_tree), pack(m_tree), pack(v_tree), "adamw_small")]
    for n in SMALL:
        res[n] = [small[j][n] for j in range(4)]

    loss = lax.psum(loss_row[0, 0], ("x", "y", "c"))
    return (loss, dx0[None], *[res[n][0] for n in WEIGHTS], *[res[n][1] for n in WEIGHTS],
            *[res[n][2] for n in WEIGHTS], *[res[n][3] for n in WEIGHTS])
```

```python
import functools
import math

import numpy as np
import jax
import jax.numpy as jnp
from jax import lax
from jax.experimental import pallas as pl
from jax.experimental.pallas import tpu as pltpu

F32 = jnp.float32
BF16 = jnp.bfloat16

N_DEV = 8
D_MODEL = 1024
D_FF = 2816
FF_SHARD = D_FF // N_DEV
FF_PAD = 384
ATT_W = 512
ATT_HEADS = 8
DILATED = ((128, 1), (512, 4), (2048, 16))
BLK = 128
ML_W = 512
ML_HEADS = 4
ML_HD = 128
CHUNK = 128
CONV_K = 4
W_IN = 3592
W_IN_SHARD = W_IN // N_DEV
W_IN_MAIN = 3584
XA_HEADS = 4
XA_HD = 256
MEM_LEN = 256
REL_BUCKETS = 32
REL_MAX_DIST = 2048
ALPHA = 2.0 ** 0.25
LN_EPS = 1e-5
NEG = -1e30
ADAM_LR = 0.001
ADAM_B1 = 0.9
ADAM_B2 = 0.999
ADAM_EPS = 1e-08
ADAM_WD = 0.01
ADAM_STEP = 10
LANES = 128
VMEM_LIMIT = 58 * 1024 * 1024

NN = (((1,), (0,)), ((), ()))
NT = (((1,), (1,)), ((), ()))
TN = (((0,), (0,)), ((), ()))


def _dot(a, b, dims):
    return lax.dot_general(a, b, dims, preferred_element_type=F32)


def _params(*sem):
    return pltpu.CompilerParams(dimension_semantics=sem, vmem_limit_bytes=VMEM_LIMIT)


def _sigmoid(x):
    return 1.0 / (1.0 + jnp.exp(-x))


def _rowsum8(x):
    t, c = x.shape
    return jnp.sum(x.reshape(t // 8, 8, c), axis=0)


def _mesh_pos():
    x, y, c = lax.axis_index("x"), lax.axis_index("y"), lax.axis_index("c")
    return x, y, c, 4 * x + 2 * y + c


def _peer(x, y, c, k):
    px = 1 - x if k & 4 else x
    py = 1 - y if k & 2 else y
    pc = 1 - c if k & 1 else c
    return (px, py, pc), 4 * px + 2 * py + pc


def _call(body, *, name, grid, in_specs, out_specs, out_shape, args, scratch_shapes=(), sem=None,
          gather=(), exchange=()):
    in_specs, out_specs, out_shape, scratch = list(in_specs), list(out_specs), list(out_shape), list(scratch_shapes)
    ng, nc = len(gather), len(gather) + len(exchange)
    if nc == 0:
        return pl.pallas_call(body, name=name, grid=grid, in_specs=in_specs, out_specs=out_specs,
                              out_shape=out_shape, scratch_shapes=scratch, compiler_params=_params(*sem))(*args)
    n_in, n_out, n_scr = len(in_specs), len(out_specs), len(scratch)

    def wrapped(*refs):
        ins, cin = refs[:n_in], refs[n_in:n_in + nc]
        outs, cout = refs[n_in + nc:n_in + nc + n_out], refs[n_in + nc + n_out:n_in + 2 * nc + n_out]
        scr = refs[n_in + 2 * nc + n_out:n_in + 2 * nc + n_out + n_scr]
        send_sems, recv_sems, loc_sems = refs[-3:]
        first, last = None, None
        for ax, extent in enumerate(grid):
            f, l = pl.program_id(ax) == 0, pl.program_id(ax) == extent - 1
            first = f if first is None else first & f
            last = l if last is None else last & l

        def copies():
            x, y, c, me = _mesh_pos()
            out = []
            for a in range(nc):
                mine = cin[a] if a < ng else cin[a].at[me]
                out.append(pltpu.make_async_copy(mine, cout[a].at[me], loc_sems.at[a]))
                for k in range(1, N_DEV):
                    peer, pidx = _peer(x, y, c, k)
                    out.append(pltpu.make_async_remote_copy(
                        src_ref=cin[a] if a < ng else cin[a].at[pidx], dst_ref=cout[a].at[me],
                        send_sem=send_sems.at[a, k - 1], recv_sem=recv_sems.at[a, k - 1],
                        device_id=peer, device_id_type=pl.DeviceIdType.MESH))
            return out

        @pl.when(first)
        def _():
            for cp in copies():
                cp.start()

        body(*ins, *outs, *scr)

        @pl.when(last)
        def _():
            for cp in copies():
                cp.wait()

    hbm = pl.BlockSpec(memory_space=pl.ANY)
    comm_shapes = [jax.ShapeDtypeStruct((N_DEV,) + a.shape, a.dtype) for a in gather]
    comm_shapes += [jax.ShapeDtypeStruct(a.shape, a.dtype) for a in exchange]
    return pl.pallas_call(
        wrapped, name=name, grid=grid, in_specs=in_specs + [hbm] * nc, out_specs=out_specs + [hbm] * nc,
        out_shape=out_shape + comm_shapes,
        scratch_shapes=scratch + [pltpu.SemaphoreType.DMA((nc, N_DEV - 1)), pltpu.SemaphoreType.DMA((nc, N_DEV - 1)),
                                  pltpu.SemaphoreType.DMA((nc,))],
        compiler_params=_params(*(("arbitrary",) * len(grid))),
    )(*args, *gather, *exchange)


def _exchange_only(name, gather=(), exchange=()):
    return _call(lambda: None, name=name, grid=(1,), in_specs=[], out_specs=[], out_shape=[], args=(),
                 gather=gather, exchange=exchange)


def _matmul(a, b, mode, name, *, out_dtype=F32, tm=512, tn=512, tk=512, add=None, add_scale=1.0,
            blocked_out=False, gather=(), exchange=()):
    blocked_b = b.ndim == 3
    if blocked_b:
        (m, k), (nb, _, tn) = a.shape, b.shape
        n = nb * tn
    elif mode == "nn":
        (m, k), (_, n) = a.shape, b.shape
    elif mode == "nt":
        (m, k), (n, _) = a.shape, b.shape
    else:
        (k, m), (_, n) = a.shape, b.shape
    tm, tn, tk = min(tm, m), min(tn, n), min(tk, k)
    nk = k // tk
    dims = {"nn": NN, "nt": NT, "tn": TN}[mode]
    if mode == "tn":
        a_spec = pl.BlockSpec((tk, tm), lambda i, j, kk: (kk, i))
    else:
        a_spec = pl.BlockSpec((tm, tk), lambda i, j, kk: (i, kk))
    if blocked_b:
        b_spec = pl.BlockSpec((None, tk, tn), lambda i, j, kk: (j, kk, 0))
    elif mode == "nt":
        b_spec = pl.BlockSpec((tn, tk), lambda i, j, kk: (j, kk))
    else:
        b_spec = pl.BlockSpec((tk, tn), lambda i, j, kk: (kk, j))
    if blocked_out:
        o_spec = pl.BlockSpec((None, tm, tn), lambda i, j, kk: (j, i, 0))
        o_shape = jax.ShapeDtypeStruct((n // tn, m, tn), out_dtype)
    else:
        o_spec = pl.BlockSpec((tm, tn), lambda i, j, kk: (i, j))
        o_shape = jax.ShapeDtypeStruct((m, n), out_dtype)
    has_add = add is not None
    cache_a = nk == 1 and mode != "tn" and n // tn > 1 and a.dtype != BF16

    def body(*refs):
        if has_add:
            a_ref, b_ref, add_ref, o_ref, s_ref = refs
        else:
            a_ref, b_ref, o_ref, s_ref = refs
        kk = pl.program_id(2)
        if cache_a:
            @pl.when(pl.program_id(1) == 0)
            def _():
                s_ref[...] = a_ref[...].astype(BF16)

            lhs = s_ref[...]
        else:
            lhs = a_ref[...].astype(BF16)
        part = _dot(lhs, b_ref[...].astype(BF16), dims)

        def finish(r):
            if has_add:
                r = r + add_scale * add_ref[...]
            o_ref[...] = r.astype(out_dtype)

        if nk == 1:
            finish(part)
            return

        @pl.when(kk == 0)
        def _():
            s_ref[...] = part

        @pl.when(kk > 0)
        def _():
            s_ref[...] += part

        @pl.when(kk == nk - 1)
        def _():
            finish(s_ref[...])

    if nk > 1:
        scratch = [pltpu.VMEM((tm, tn), F32)]
    else:
        scratch = [pltpu.VMEM((tm, tk), BF16) if cache_a else pltpu.VMEM((8, LANES), F32)]
    return _call(
        body, name=name, grid=(m // tm, n // tn, nk),
        in_specs=[a_spec, b_spec] + ([pl.BlockSpec((tm, tn), lambda i, j, kk: (i, j))] if has_add else []),
        out_specs=[o_spec], out_shape=[o_shape], args=(a, b) + ((add,) if has_add else ()),
        scratch_shapes=scratch, sem=("parallel", "arbitrary", "arbitrary"),
        gather=gather, exchange=exchange)


def _ln_fwd_math(u, g, b):
    mu = jnp.mean(u, axis=-1, keepdims=True)
    uc = u - mu
    var = jnp.mean(uc * uc, axis=-1, keepdims=True)
    return uc * lax.rsqrt(var + LN_EPS) * g + b


def _ln_bwd_math(dy, u, g):
    mu = jnp.mean(u, axis=-1, keepdims=True)
    uc = u - mu
    var = jnp.mean(uc * uc, axis=-1, keepdims=True)
    rstd = lax.rsqrt(var + LN_EPS)
    xhat = uc * rstd
    dxh = dy * g
    m1 = jnp.mean(dxh, axis=-1, keepdims=True)
    m2 = jnp.mean(dxh * xhat, axis=-1, keepdims=True)
    return rstd * (dxh - m1 - xhat * m2), xhat


def _resid_ln(x, f, g, b, name, tm=512):
    s, d = x.shape

    def body(x_ref, f_ref, g_ref, b_ref, u_ref, y_ref):
        u = ALPHA * x_ref[...] + f_ref[...]
        u_ref[...] = u
        y_ref[...] = _ln_fwd_math(u, g_ref[...], b_ref[...])

    row = pl.BlockSpec((tm, d), lambda i: (i, 0))
    vec = pl.BlockSpec((1, d), lambda i: (0, 0))
    return pl.pallas_call(
        body, name=name, grid=(s // tm,),
        in_specs=[row, row, vec, vec], out_specs=[row, row],
        out_shape=[jax.ShapeDtypeStruct((s, d), F32)] * 2,
        compiler_params=_params("parallel"),
    )(x, f, g, b)


def _ln_bwd(dy, u, g, name, tm=512):
    s, d = dy.shape
    nt = s // tm

    def body(dy_ref, u_ref, g_ref, du_ref, dg_ref, db_ref, g8, b8):
        i = pl.program_id(0)
        dy_ = dy_ref[...]
        du, xhat = _ln_bwd_math(dy_, u_ref[...], g_ref[...])
        du_ref[...] = du

        @pl.when(i == 0)
        def _():
            g8[...] = jnp.zeros_like(g8)
            b8[...] = jnp.zeros_like(b8)

        g8[...] += _rowsum8(dy_ * xhat)
        b8[...] += _rowsum8(dy_)

        @pl.when(i == nt - 1)
        def _():
            dg_ref[...] = jnp.sum(g8[...], axis=0, keepdims=True)
            db_ref[...] = jnp.sum(b8[...], axis=0, keepdims=True)

    row = pl.BlockSpec((tm, d), lambda i: (i, 0))
    vec = pl.BlockSpec((1, d), lambda i: (0, 0))
    return pl.pallas_call(
        body, name=name, grid=(nt,),
        in_specs=[row, row, vec], out_specs=[row, vec, vec],
        out_shape=[jax.ShapeDtypeStruct((s, d), F32), jax.ShapeDtypeStruct((1, d), F32),
                   jax.ShapeDtypeStruct((1, d), F32)],
        scratch_shapes=[pltpu.VMEM((8, d), F32), pltpu.VMEM((8, d), F32)],
        compiler_params=_params("arbitrary"),
    )(dy, u, g)


FF_PAIR = 2 * FF_PAD
N_PAIR = N_DEV // 2


def _ffn_fwd(x, wgt, wut, wd, g, b, name, tm=512, gather=()):
    s, d = x.shape

    def body(x_ref, wg_ref, wu_ref, wd_ref, g_ref, b_ref, u_ref, y_ref, xb, acc):
        k = pl.program_id(1)

        @pl.when(k == 0)
        def _():
            xb[...] = x_ref[...].astype(BF16)

        a = _dot(xb[...], wg_ref[...], NT)
        bb = _dot(xb[...], wu_ref[...], NT)
        h = (a * _sigmoid(a) * bb).astype(BF16)
        part = _dot(h, wd_ref[...], NN)

        @pl.when(k == 0)
        def _():
            acc[...] = part

        @pl.when(k > 0)
        def _():
            acc[...] += part

        @pl.when(k == N_PAIR - 1)
        def _():
            u = ALPHA * x_ref[...] + 0.5 * acc[...]
            u_ref[...] = u
            y_ref[...] = _ln_fwd_math(u, g_ref[...], b_ref[...])

    row = pl.BlockSpec((tm, d), lambda i, k: (i, 0))
    vec = pl.BlockSpec((1, d), lambda i, k: (0, 0))
    w_in = pl.BlockSpec((None, FF_PAIR, d), lambda i, k: (k, 0, 0))
    w_dn = w_in
    return _call(
        body, name=name, grid=(s // tm, N_PAIR),
        in_specs=[row, w_in, w_in, w_dn, vec, vec], out_specs=[row, row],
        out_shape=[jax.ShapeDtypeStruct((s, d), F32)] * 2, args=(x, wgt, wut, wd, g, b),
        scratch_shapes=[pltpu.VMEM((tm, d), BF16), pltpu.VMEM((tm, d), F32)],
        sem=("parallel", "arbitrary"), gather=gather)


def _ffn_bwd_x(dy, u, x, wgt, wut, wd, g, name, tm=512, exchange=()):
    s, d = x.shape
    nt = s // tm
    ffp = N_DEV * FF_PAD

    def body(dy_ref, u_ref, x_ref, wg_ref, wu_ref, wd_ref, g_ref,
             dx_ref, xb, df_ref, da_ref, db_ref, h_ref, dg_ref, dbl_ref,
             dfb, du_s, acc, g8, b8):
        i = pl.program_id(0)
        k = pl.program_id(1)

        @pl.when(k == 0)
        def _():
            dy_ = dy_ref[...]
            du, xhat = _ln_bwd_math(dy_, u_ref[...], g_ref[...])
            du_s[...] = du
            dfb[...] = (0.5 * du).astype(BF16)
            df_ref[...] = dfb[...]
            xb[...] = x_ref[...].astype(BF16)

            @pl.when(i == 0)
            def _():
                g8[...] = jnp.zeros_like(g8)
                b8[...] = jnp.zeros_like(b8)

            g8[...] += _rowsum8(dy_ * xhat)
            b8[...] += _rowsum8(dy_)

        a = _dot(xb[...], wg_ref[...], NT)
        bb = _dot(xb[...], wu_ref[...], NT)
        sig = _sigmoid(a)
        sa = a * sig
        h_ref[...] = (sa * bb).astype(BF16)
        dh = _dot(dfb[...], wd_ref[...], NT)
        da = (dh * bb * (sig * (1.0 + a * (1.0 - sig)))).astype(BF16)
        db = (dh * sa).astype(BF16)
        da_ref[...] = da
        db_ref[...] = db
        part = _dot(da, wg_ref[...], NN) + _dot(db, wu_ref[...], NN)

        @pl.when(k == 0)
        def _():
            acc[...] = part

        @pl.when(k > 0)
        def _():
            acc[...] += part

        @pl.when(k == N_PAIR - 1)
        def _():
            dx_ref[...] = ALPHA * du_s[...] + acc[...]

        @pl.when((k == N_PAIR - 1) & (i == nt - 1))
        def _():
            dg_ref[...] = jnp.sum(g8[...], axis=0, keepdims=True)
            dbl_ref[...] = jnp.sum(b8[...], axis=0, keepdims=True)

    row = pl.BlockSpec((tm, d), lambda i, k: (i, 0))
    vec = pl.BlockSpec((1, d), lambda i, k: (0, 0))
    w_in = pl.BlockSpec((None, FF_PAIR, d), lambda i, k: (k, 0, 0))
    hid = pl.BlockSpec((tm, FF_PAIR), lambda i, k: (i, k))
    return _call(
        body, name=name, grid=(nt, N_PAIR),
        in_specs=[row, row, row, w_in, w_in, w_in, vec],
        out_specs=[row, row, row, hid, hid, hid, vec, vec],
        out_shape=[jax.ShapeDtypeStruct((s, d), F32), jax.ShapeDtypeStruct((s, d), BF16),
                   jax.ShapeDtypeStruct((s, d), BF16),
                   jax.ShapeDtypeStruct((s, ffp), BF16), jax.ShapeDtypeStruct((s, ffp), BF16),
                   jax.ShapeDtypeStruct((s, ffp), BF16),
                   jax.ShapeDtypeStruct((1, d), F32), jax.ShapeDtypeStruct((1, d), F32)],
        args=(dy, u, x, wgt, wut, wd, g),
        scratch_shapes=[pltpu.VMEM((tm, d), BF16), pltpu.VMEM((tm, d), F32),
                        pltpu.VMEM((tm, d), F32), pltpu.VMEM((8, d), F32), pltpu.VMEM((8, d), F32)],
        sem=("arbitrary", "arbitrary"), exchange=exchange)


def _ffn_bwd_w(xb, df, da, db, h, name, tm=512):
    s, d = xb.shape
    nt = s // tm

    def body(x_ref, df_ref, da_ref, db_ref, h_ref, dwg_ref, dwu_ref, dwd_ref, ag, au, ad):
        i = pl.program_id(1)
        pg = _dot(x_ref[...], da_ref[...], TN)
        pu = _dot(x_ref[...], db_ref[...], TN)
        pd = _dot(h_ref[...], df_ref[...], TN)

        @pl.when(i == 0)
        def _():
            ag[...] = pg
            au[...] = pu
            ad[...] = pd

        @pl.when(i > 0)
        def _():
            ag[...] += pg
            au[...] += pu
            ad[...] += pd

        @pl.when(i == nt - 1)
        def _():
            for j in range(2):
                lo = j * FF_PAD
                dwg_ref[j] = ag[:, lo:lo + FF_SHARD].astype(BF16)
                dwu_ref[j] = au[:, lo:lo + FF_SHARD].astype(BF16)
                dwd_ref[j] = ad[lo:lo + FF_SHARD, :].astype(BF16)

    row = pl.BlockSpec((tm, d), lambda k, i: (i, 0))
    hid = pl.BlockSpec((tm, FF_PAIR), lambda k, i: (i, k))
    w_in = pl.BlockSpec((2, d, FF_SHARD), lambda k, i: (k, 0, 0))
    w_dn = pl.BlockSpec((2, FF_SHARD, d), lambda k, i: (k, 0, 0))
    return _call(
        body, name=name, grid=(N_PAIR, nt),
        in_specs=[row, row, hid, hid, hid], out_specs=[w_in, w_in, w_dn],
        out_shape=[jax.ShapeDtypeStruct((N_DEV, d, FF_SHARD), BF16), jax.ShapeDtypeStruct((N_DEV, d, FF_SHARD), BF16),
                   jax.ShapeDtypeStruct((N_DEV, FF_SHARD, d), BF16)],
        args=(xb, df, da, db, h),
        scratch_shapes=[pltpu.VMEM((d, FF_PAIR), F32), pltpu.VMEM((d, FF_PAIR), F32), pltpu.VMEM((FF_PAIR, d), F32)],
        sem=("parallel", "arbitrary"))


def _bucket_tables():
    qi = np.arange(BLK)[:, None]
    ki = np.arange(2 * BLK)[None, :]
    off = qi + BLK - ki
    out = []
    for window, dil in DILATED:
        n_keys = window // dil
        dist = dil * np.clip(off, 0, n_keys)
        exact = REL_BUCKETS // 2
        df = np.maximum(dist, 1).astype(np.float32)
        large = exact + (np.log(df / np.float32(exact)) / np.float32(math.log(REL_MAX_DIST / exact))
                         * np.float32(REL_BUCKETS - exact)).astype(np.int32)
        large = np.minimum(large, REL_BUCKETS - 1)
        bucket = np.where(dist < exact, dist, large).astype(np.int32)
        band = (off >= 0) & (off <= n_keys)
        out.append(np.where(band, bucket, -1))
    return np.stack(out).astype(np.int32)


def _bias_fwd(rel_bias, buckets, name="bias_fwd"):
    def body(tbl_ref, bkt_ref, out_ref):
        bkt = bkt_ref[...]
        for h in range(ATT_HEADS):
            acc = jnp.full((BLK, 2 * BLK), NEG, F32)
            for bb in range(REL_BUCKETS):
                acc = jnp.where(bkt == bb, tbl_ref[bb, h], acc)
            out_ref[h] = acc

    nbr = len(DILATED)
    return pl.pallas_call(
        body, name=name, grid=(nbr,),
        in_specs=[pl.BlockSpec(memory_space=pltpu.SMEM),
                  pl.BlockSpec((None, BLK, 2 * BLK), lambda r: (r, 0, 0))],
        out_specs=pl.BlockSpec((None, ATT_HEADS, BLK, 2 * BLK), lambda r: (r, 0, 0, 0)),
        out_shape=jax.ShapeDtypeStruct((nbr, ATT_HEADS, BLK, 2 * BLK), F32),
        compiler_params=_params("parallel"),
    )(rel_bias, buckets)


def _bias_bwd(dbias, buckets, name="bias_bwd"):
    nbr = len(DILATED)

    def body(db_ref, bkt_ref, out_ref):
        r = pl.program_id(0)

        @pl.when(r == 0)
        def _():
            out_ref[...] = jnp.zeros_like(out_ref)

        bkt = bkt_ref[...]
        rowi = lax.broadcasted_iota(jnp.int32, (REL_BUCKETS, LANES), 0)
        coli = lax.broadcasted_iota(jnp.int32, (REL_BUCKETS, LANES), 1)
        acc = jnp.zeros((REL_BUCKETS, LANES), F32)
        for h in range(ATT_HEADS):
            x = db_ref[h]
            for bb in range(REL_BUCKETS):
                part = jnp.sum(jnp.where(bkt == bb, x, 0.0), axis=0, keepdims=True)
                tot = jnp.sum(part, axis=1, keepdims=True)
                acc = acc + jnp.where((rowi == bb) & (coli == h), tot, 0.0)
        out_ref[...] += acc

    return pl.pallas_call(
        body, name=name, grid=(nbr,),
        in_specs=[pl.BlockSpec((None, ATT_HEADS, BLK, 2 * BLK), lambda r: (r, 0, 0, 0)),
                  pl.BlockSpec((None, BLK, 2 * BLK), lambda r: (r, 0, 0))],
        out_specs=pl.BlockSpec((REL_BUCKETS, LANES), lambda r: (0, 0)),
        out_shape=jax.ShapeDtypeStruct((REL_BUCKETS, LANES), F32),
        compiler_params=_params("arbitrary"),
    )(dbias, buckets)


def _att_scores(q_pair, k2, bias, first_ok, msk):
    qm = jnp.where(msk, q_pair, 0.0).astype(BF16)
    sc = _dot(qm, k2, NT) * (64 ** -0.5) + bias
    return jnp.where(first_ok, sc, NEG), qm


def _dil_specs(dil, nb, clamp):
    ncol = W_IN_MAIN // ATT_W

    def cur(col):
        return pl.BlockSpec((BLK, ATT_W), lambda r, n: (jnp.minimum(n, nb - 1) if clamp else n, r * ncol + col))

    def prev(col):
        return pl.BlockSpec(
            (BLK, ATT_W), lambda r, n: (jnp.maximum((jnp.minimum(n, nb - 1) if clamp else n) - 1, 0), r * ncol + col))

    return [cur(0), prev(1), cur(1), prev(2), cur(2)]


def _dil_fwd(proj, biasm, branch, name, gather=()):
    s = proj.shape[0]
    dil = DILATED[branch][1]
    m = s // dil
    nb = m // BLK
    pv = proj.reshape(m, dil * W_IN_MAIN)

    def body(q_ref, kp_ref, kc_ref, vp_ref, vc_ref, bias_ref, o_ref, lse_ref):
        n = pl.program_id(1)
        lo = lax.broadcasted_iota(jnp.int32, (BLK, LANES), 1) < 64
        kidx = lax.broadcasted_iota(jnp.int32, (BLK, 2 * BLK), 1)
        first_ok = (n > 0) | (kidx >= BLK)
        for p in range(ATT_W // LANES):
            sl = slice(LANES * p, LANES * (p + 1))
            q_pair = q_ref[:, sl]
            k2 = jnp.concatenate([kp_ref[:, sl], kc_ref[:, sl]], axis=0).astype(BF16)
            v2 = jnp.concatenate([vp_ref[:, sl], vc_ref[:, sl]], axis=0).astype(BF16)
            outs, lses = [], []
            for hh in range(2):
                msk = lo if hh == 0 else jnp.logical_not(lo)
                sc, _ = _att_scores(q_pair, k2, bias_ref[2 * p + hh], first_ok, msk)
                mx = jnp.max(sc, axis=1, keepdims=True)
                pe = jnp.exp(sc - mx)
                l = jnp.sum(pe, axis=1, keepdims=True)
                outs.append(_dot(pe.astype(BF16), v2, NN) / l)
                lses.append(jnp.broadcast_to(mx + jnp.log(l), (BLK, LANES)))
            o_ref[:, sl] = jnp.where(lo, outs[0], outs[1])
            lse_ref[:, sl] = jnp.where(lo, lses[0], lses[1])

    out_spec = pl.BlockSpec((BLK, ATT_W), lambda r, n: (n, r))
    o, lse, *rest = _call(
        body, name=name, grid=(dil, nb),
        in_specs=_dil_specs(dil, nb, False) + [pl.BlockSpec((None, ATT_HEADS, BLK, 2 * BLK), lambda r, n: (branch, 0, 0, 0))],
        out_specs=[out_spec, out_spec],
        out_shape=[jax.ShapeDtypeStruct((m, dil * ATT_W), F32)] * 2,
        args=(pv, pv, pv, pv, pv, biasm), sem=("parallel", "arbitrary"), gather=gather)
    return [o.reshape(s, ATT_W), lse.reshape(s, ATT_W)] + rest


def _dil_combine(os_, lses, name="dil_combine", tm=512):
    s = os_[0].shape[0]

    def body(o0, o1, o2, l0, l1, l2, att_ref, lse_ref):
        a, b, c = l0[...], l1[...], l2[...]
        mx = jnp.maximum(jnp.maximum(a, b), c)
        ea, eb, ec = jnp.exp(a - mx), jnp.exp(b - mx), jnp.exp(c - mx)
        tot = ea + eb + ec
        att_ref[...] = (ea * o0[...] + eb * o1[...] + ec * o2[...]) / tot
        lse_ref[...] = mx + jnp.log(tot)

    row = pl.BlockSpec((tm, ATT_W), lambda i: (i, 0))
    return pl.pallas_call(
        body, name=name, grid=(s // tm,),
        in_specs=[row] * 6, out_specs=[row, row],
        out_shape=[jax.ShapeDtypeStruct((s, ATT_W), F32)] * 2,
        compiler_params=_params("parallel"),
    )(*os_, *lses)


def _dil_bwd(proj, biasm, lse, att, datt, acc, branch, name):
    s = proj.shape[0]
    dil = DILATED[branch][1]
    m = s // dil
    nb = m // BLK
    pv = proj.reshape(m, dil * W_IN_MAIN)
    has_acc = acc is not None
    view = lambda t: t.reshape(m, dil * ATT_W)

    def body(*refs):
        q_ref, kp_ref, kc_ref, vp_ref, vc_ref, bias_ref, lse_ref, att_ref, datt_ref = refs[:9]
        refs = refs[9:]
        if has_acc:
            aq_ref, ak_ref, av_ref = refs[:3]
            refs = refs[3:]
        dq_ref, dk_ref, dv_ref, dbias_ref, kcar, vcar = refs
        r = pl.program_id(0)
        n = pl.program_id(1)

        @pl.when((r == 0) & (n == 0))
        def _():
            dbias_ref[...] = jnp.zeros_like(dbias_ref)

        @pl.when(n == 0)
        def _():
            kcar[...] = jnp.zeros_like(kcar)
            vcar[...] = jnp.zeros_like(vcar)

        @pl.when(n < nb)
        def _():
            lo = lax.broadcasted_iota(jnp.int32, (BLK, LANES), 1) < 64
            kidx = lax.broadcasted_iota(jnp.int32, (BLK, 2 * BLK), 1)
            first_ok = (n > 0) | (kidx >= BLK)
            for p in range(ATT_W // LANES):
                sl = slice(LANES * p, LANES * (p + 1))
                q_pair = q_ref[:, sl]
                k2 = jnp.concatenate([kp_ref[:, sl], kc_ref[:, sl]], axis=0).astype(BF16)
                v2 = jnp.concatenate([vp_ref[:, sl], vc_ref[:, sl]], axis=0).astype(BF16)
                lse_pair = lse_ref[:, sl]
                dd_pair = datt_ref[:, sl] * att_ref[:, sl]
                dat_pair = datt_ref[:, sl]
                dqs, dk2, dv2 = [], None, None
                for hh in range(2):
                    msk = lo if hh == 0 else jnp.logical_not(lo)
                    sc, qm = _att_scores(q_pair, k2, bias_ref[2 * p + hh], first_ok, msk)
                    lse_h = jnp.max(jnp.where(msk, lse_pair, -jnp.inf), axis=1, keepdims=True)
                    pr = jnp.exp(sc - lse_h)
                    dsum = jnp.sum(jnp.where(msk, dd_pair, 0.0), axis=1, keepdims=True)
                    dom = jnp.where(msk, dat_pair, 0.0).astype(BF16)
                    dp = _dot(dom, v2, NT)
                    ds = pr * (dp - dsum)
                    dbias_ref[2 * p + hh] += ds
                    dsb = (ds * (64 ** -0.5)).astype(BF16)
                    dqs.append(_dot(dsb, k2, NN))
                    dkh = _dot(dsb, qm, TN)
                    dvh = _dot(pr.astype(BF16), dom, TN)
                    dk2 = dkh if dk2 is None else dk2 + dkh
                    dv2 = dvh if dv2 is None else dv2 + dvh
                dq = jnp.where(lo, dqs[0], dqs[1])
                dkp = kcar[:, sl] + dk2[:BLK]
                dvp = vcar[:, sl] + dv2[:BLK]
                if has_acc:
                    dq = dq + aq_ref[:, sl]
                    dkp = dkp + ak_ref[:, sl]
                    dvp = dvp + av_ref[:, sl]
                dq_ref[:, sl] = dq
                dk_ref[:, sl] = dkp
                dv_ref[:, sl] = dvp
                kcar[:, sl] = dk2[BLK:]
                vcar[:, sl] = dv2[BLK:]

        @pl.when(n == nb)
        def _():
            dkp = kcar[...]
            dvp = vcar[...]
            if has_acc:
                dkp = dkp + ak_ref[...]
                dvp = dvp + av_ref[...]
            dk_ref[...] = dkp
            dv_ref[...] = dvp

    cur = pl.BlockSpec((BLK, ATT_W), lambda r, n: (jnp.minimum(n, nb - 1), r))
    prev = pl.BlockSpec((BLK, ATT_W), lambda r, n: (jnp.maximum(n - 1, 0), r))
    in_specs = _dil_specs(dil, nb, True) + [
        pl.BlockSpec((None, ATT_HEADS, BLK, 2 * BLK), lambda r, n: (branch, 0, 0, 0)), cur, cur, cur]
    args = [pv, pv, pv, pv, pv, biasm, view(lse), view(att), view(datt)]
    if has_acc:
        in_specs += [cur, prev, prev]
        args += [view(t) for t in acc]
    dq, dk, dv, dbias = pl.pallas_call(
        body, name=name, grid=(dil, nb + 1),
        in_specs=in_specs,
        out_specs=[cur, prev, prev, pl.BlockSpec((ATT_HEADS, BLK, 2 * BLK), lambda r, n: (0, 0, 0))],
        out_shape=[jax.ShapeDtypeStruct((m, dil * ATT_W), F32)] * 3
        + [jax.ShapeDtypeStruct((ATT_HEADS, BLK, 2 * BLK), F32)],
        scratch_shapes=[pltpu.VMEM((BLK, ATT_W), F32), pltpu.VMEM((BLK, ATT_W), F32)],
        compiler_params=_params("arbitrary", "arbitrary"),
    )(*args)
    return (dq.reshape(s, ATT_W), dk.reshape(s, ATT_W), dv.reshape(s, ATT_W)), dbias


QK_COL0 = (3 * ATT_W) // ATT_W


def _conv_shifted(prev, cur, j, row):
    sh = CONV_K - 1 - j
    if sh == 0:
        return cur
    return jnp.where(row < sh, pltpu.roll(prev, sh, 0), pltpu.roll(cur, sh, 0))


def _conv_z(prev, cur, w_ref, b_ref, row):
    z = b_ref[...] + cur * w_ref[CONV_K - 1:CONV_K, :]
    for j in range(CONV_K - 1):
        z = z + _conv_shifted(prev, cur, j, row) * w_ref[j:j + 1, :]
    return z


def _conv_fwd(proj, conv_w, conv_b, name="conv_fwd", tm=512):
    s = proj.shape[0]
    w = ATT_W

    def body(prev_ref, cur_ref, w_ref, b_ref, o_ref):
        i = pl.program_id(1)
        row = lax.broadcasted_iota(jnp.int32, (tm, w), 0)
        prev = jnp.where(i > 0, prev_ref[...], 0.0)
        z = _conv_z(prev, cur_ref[...], w_ref, b_ref, row)
        o_ref[...] = z * _sigmoid(z)

    return pl.pallas_call(
        body, name=name, grid=(2, s // tm),
        in_specs=[pl.BlockSpec((tm, w), lambda j, i: (jnp.maximum(i - 1, 0), QK_COL0 + j)),
                  pl.BlockSpec((tm, w), lambda j, i: (i, QK_COL0 + j)),
                  pl.BlockSpec((CONV_K, w), lambda j, i: (0, j)),
                  pl.BlockSpec((1, w), lambda j, i: (0, j))],
        out_specs=pl.BlockSpec((tm, w), lambda j, i: (i, j)),
        out_shape=jax.ShapeDtypeStruct((s, 2 * ML_W), F32),
        compiler_params=_params("parallel", "parallel"),
    )(proj, proj, conv_w, conv_b)


def _conv_bwd(proj, dqk, conv_w, conv_b, name="conv_bwd", tm=512):
    s = proj.shape[0]
    w = ATT_W
    nt = s // tm

    def body(xp_ref, xc_ref, xn_ref, dc_ref, dn_ref, w_ref, b_ref, dx_ref, dw_ref, db_ref):
        i = pl.program_id(1)
        row = lax.broadcasted_iota(jnp.int32, (tm, w), 0)
        prev = jnp.where(i > 0, xp_ref[...], 0.0)
        cur = xc_ref[...]

        def dz_of(pv, cv, dy):
            z = _conv_z(pv, cv, w_ref, b_ref, row)
            sig = _sigmoid(z)
            return dy * (sig * (1.0 + z * (1.0 - sig)))

        dzc = dz_of(prev, cur, dc_ref[...])
        dzn = jnp.where(i < nt - 1, dz_of(cur, xn_ref[...], dn_ref[...]), 0.0)
        dx = dzc * w_ref[CONV_K - 1:CONV_K, :]
        for j in range(CONV_K - 1):
            sh = CONV_K - 1 - j
            up = jnp.where(row >= tm - sh, pltpu.roll(dzn, tm - sh, 0), pltpu.roll(dzc, tm - sh, 0))
            dx = dx + up * w_ref[j:j + 1, :]
        dx_ref[...] = dx

        @pl.when(i == 0)
        def _():
            dw_ref[...] = jnp.zeros_like(dw_ref)
            db_ref[...] = jnp.zeros_like(db_ref)

        for j in range(CONV_K):
            dw_ref[j:j + 1, :] += jnp.sum(dzc * _conv_shifted(prev, cur, j, row), axis=0, keepdims=True)
        db_ref[...] += jnp.sum(dzc, axis=0, keepdims=True)

    xs = lambda f: pl.BlockSpec((tm, w), lambda j, i: (f(i), QK_COL0 + j))
    ds = lambda f: pl.BlockSpec((tm, w), lambda j, i: (f(i), j))
    return pl.pallas_call(
        body, name=name, grid=(2, nt),
        in_specs=[xs(lambda i: jnp.maximum(i - 1, 0)), xs(lambda i: i), xs(lambda i: jnp.minimum(i + 1, nt - 1)),
                  ds(lambda i: i), ds(lambda i: jnp.minimum(i + 1, nt - 1)),
                  pl.BlockSpec((CONV_K, w), lambda j, i: (0, j)), pl.BlockSpec((1, w), lambda j, i: (0, j))],
        out_specs=[ds(lambda i: i), pl.BlockSpec((CONV_K, w), lambda j, i: (0, j)),
                   pl.BlockSpec((1, w), lambda j, i: (0, j))],
        out_shape=[jax.ShapeDtypeStruct((s, 2 * ML_W), F32), jax.ShapeDtypeStruct((CONV_K, 2 * ML_W), F32),
                   jax.ShapeDtypeStruct((1, 2 * ML_W), F32)],
        compiler_params=_params("parallel", "arbitrary"),
    )(proj, proj, proj, dqk, dqk, conv_w, conv_b)


def _bf16_mm(dims_fwd):
    @jax.custom_vjp
    def mm(a, b):
        return _dot(a.astype(BF16), b.astype(BF16), dims_fwd)

    def fwd(a, b):
        return mm(a, b), (a, b)

    def bwd(res, g):
        a, b = res
        if dims_fwd is NN:
            return _mm_nt(g, b), _mm_tn(a, g)
        if dims_fwd is NT:
            return _mm_nn(g, b), _mm_tn(g, a)
        return _mm_nt(b, g), _mm_nn(a, g)

    mm.defvjp(fwd, bwd)
    return mm


_mm_nn = _bf16_mm(NN)
_mm_nt = _bf16_mm(NT)
_mm_tn = _bf16_mm(TN)


def _tri(lower):
    r = lax.broadcasted_iota(jnp.int32, (CHUNK, CHUNK), 0)
    c = lax.broadcasted_iota(jnp.int32, (CHUNK, CHUNK), 1)
    return ((r >= c) if lower else (r <= c)).astype(F32)


@jax.custom_vjp
def _cumsum_rows(x):
    return lax.dot_general(_tri(True), x, NN, precision=lax.Precision.HIGHEST, preferred_element_type=F32)


def _cumsum_fwd(x):
    return _cumsum_rows(x), None


def _cumsum_bwd(_, g):
    return (lax.dot_general(_tri(False), g, NN, precision=lax.Precision.HIGHEST, preferred_element_type=F32),)


_cumsum_rows.defvjp(_cumsum_fwd, _cumsum_bwd)


def _abs(x):
    return jnp.where(x >= 0, x, -x)


def _log_sigmoid(x):
    return jnp.minimum(x, 0.0) - jnp.log(1.0 + jnp.exp(-_abs(x)))


def _pick_col(x, lane):
    sel = lax.broadcasted_iota(jnp.int32, x.shape, 1) == lane
    return jnp.sum(jnp.where(sel, x, 0.0), axis=1, keepdims=True)


def _pick_row(x, r):
    sel = lax.broadcasted_iota(jnp.int32, x.shape, 0) == r
    return jnp.sum(jnp.where(sel, x, 0.0), axis=0, keepdims=True)


def _mlstm_chunk(qs, ks, vs, oms, gates, gate_bias, mlg, cs, ns, ms):
    gb = gates + gate_bias
    cum = _cumsum_rows(_log_sigmoid(gb))
    gbt = gb.T
    cumt = cum.T
    causal = lax.broadcasted_iota(jnp.int32, (CHUNK, CHUNK), 0) >= lax.broadcasted_iota(jnp.int32, (CHUNK, CHUNK), 1)
    ys, c_out, n_out, m_out = [], [], [], []
    for h in range(ML_HEADS):
        q, v, om, c, n, m = qs[h], vs[h], oms[h], cs[h], ns[h], ms[h]
        k = ks[h] * (ML_HD ** -0.5)
        ig_col = _pick_col(gb, h)
        ig_row = _pick_row(gbt, h)
        b_col = _pick_col(cum, ML_HEADS + h)
        b_row = _pick_row(cumt, ML_HEADS + h)
        g = _pick_row(b_col, CHUNK - 1)
        a = g - b_col + ig_col
        m_loc = jnp.max(a, axis=0, keepdims=True)
        wa = jnp.exp(a - m_loc)
        c_loc = _mm_tn(wa * v, k)
        n_loc = jnp.sum(wa * k, axis=0, keepdims=True)
        m_new = jnp.maximum(g + m, m_loc)
        sp = jnp.exp(g + m - m_new)
        sl = jnp.exp(m_loc - m_new)
        c_out.append(sp * c + sl * c_loc)
        n_out.append(sp * n + sl * n_loc)
        m_out.append(m_new)
        d_log = jnp.where(causal, b_col - b_row + ig_row, -jnp.inf)
        e_log = b_col + m
        m_t = jnp.maximum(e_log, jnp.max(d_log, axis=1, keepdims=True))
        d_w = jnp.exp(d_log - m_t)
        e_w = jnp.exp(e_log - m_t)
        s_qk = _mm_nt(q, k) * d_w
        num = e_w * _mm_nt(q, c) + _mm_nn(s_qk, v)
        den = e_w * jnp.sum(q * n, axis=1, keepdims=True) + jnp.sum(s_qk, axis=1, keepdims=True)
        hh = num / jnp.maximum(_abs(den), jnp.exp(-m_t))
        hg = _sigmoid(om) * hh
        mu = jnp.mean(hg, axis=1, keepdims=True)
        hc = hg - mu
        var = jnp.mean(hc * hc, axis=1, keepdims=True)
        ys.append(hc * lax.rsqrt(var + LN_EPS) * mlg[h])
    return ys, c_out, n_out, m_out


V_COL = 5
O_COL = 6


def _mlstm_fwd(qk, proj, gates, gate_bias, mlg, name="mlstm_fwd", gather=()):
    s = qk.shape[0]
    nc = s // CHUNK

    def body(q_ref, k_ref, v_ref, o_ref, g_ref, gb_ref, mlg_ref, y_ref, cp_ref, np_ref, mp_ref, c_s, n_s, m_s):
        ci = pl.program_id(0)

        @pl.when(ci == 0)
        def _():
            c_s[...] = jnp.zeros_like(c_s)
            n_s[...] = jnp.zeros_like(n_s)
            m_s[...] = jnp.zeros_like(m_s)

        cp_ref[...] = c_s[...]
        np_ref[...] = n_s[...]
        mp_ref[...] = m_s[...]
        hs = lambda ref: [ref[:, LANES * h:LANES * (h + 1)] for h in range(ML_HEADS)]
        ys, c_new, n_new, m_new = _mlstm_chunk(
            hs(q_ref), hs(k_ref), hs(v_ref), hs(o_ref), g_ref[...], gb_ref[...], hs(mlg_ref),
            [c_s[h] for h in range(ML_HEADS)], [n_s[h:h + 1, :] for h in range(ML_HEADS)],
            [m_s[h:h + 1, 0:1] for h in range(ML_HEADS)])
        for h in range(ML_HEADS):
            y_ref[:, LANES * h:LANES * (h + 1)] = ys[h]
            c_s[h] = c_new[h]
            n_s[h:h + 1, :] = n_new[h]
            m_s[h:h + 1, :] = jnp.broadcast_to(m_new[h], (1, LANES))

    blk = lambda col: pl.BlockSpec((CHUNK, ML_W), lambda ci: (ci, col))
    vec = lambda w: pl.BlockSpec((1, w), lambda ci: (0, 0))
    return _call(
        body, name=name, grid=(nc,), args=(qk, qk, proj, proj, gates, gate_bias, mlg), sem=("arbitrary",), gather=gather,
        in_specs=[blk(0), blk(1), blk(V_COL), blk(O_COL), pl.BlockSpec((CHUNK, LANES), lambda ci: (ci, 0)),
                  vec(LANES), vec(ML_W)],
        out_specs=[blk(0), pl.BlockSpec((None, ML_HEADS, ML_HD, ML_HD), lambda ci: (ci, 0, 0, 0)),
                   pl.BlockSpec((None, 8, LANES), lambda ci: (ci, 0, 0)),
                   pl.BlockSpec((None, 8, LANES), lambda ci: (ci, 0, 0))],
        out_shape=[jax.ShapeDtypeStruct((s, ML_W), F32), jax.ShapeDtypeStruct((nc, ML_HEADS, ML_HD, ML_HD), F32),
                   jax.ShapeDtypeStruct((nc, 8, LANES), F32), jax.ShapeDtypeStruct((nc, 8, LANES), F32)],
        scratch_shapes=[pltpu.VMEM((ML_HEADS, ML_HD, ML_HD), F32), pltpu.VMEM((8, LANES), F32),
                        pltpu.VMEM((8, LANES), F32)])


def _mlstm_bwd(qk, proj, gates, gate_bias, mlg, cprev, nprev, mprev, dy, name="mlstm_bwd", exchange=()):
    s = qk.shape[0]
    nc = s // CHUNK

    def body(q_ref, k_ref, v_ref, o_ref, g_ref, gb_ref, mlg_ref, cp_ref, np_ref, mp_ref, dy_ref,
             dqk_ref, dv_ref, do_ref, dg_ref, dgb_ref, dmlg_ref, dc_s, dn_s, dm_s, gb8, mg8):
        ci = pl.program_id(0)

        @pl.when(ci == 0)
        def _():
            dc_s[...] = jnp.zeros_like(dc_s)
            dn_s[...] = jnp.zeros_like(dn_s)
            dm_s[...] = jnp.zeros_like(dm_s)
            gb8[...] = jnp.zeros_like(gb8)
            mg8[...] = jnp.zeros_like(mg8)

        hs = lambda ref: [ref[:, LANES * h:LANES * (h + 1)] for h in range(ML_HEADS)]
        prim = (hs(q_ref), hs(k_ref), hs(v_ref), hs(o_ref), g_ref[...], gb_ref[...], hs(mlg_ref),
                [cp_ref[h] for h in range(ML_HEADS)], [np_ref[h:h + 1, :] for h in range(ML_HEADS)],
                [mp_ref[h:h + 1, 0:1] for h in range(ML_HEADS)])
        _, vjp = jax.vjp(_mlstm_chunk, *prim)
        cot = (hs(dy_ref), [dc_s[h] for h in range(ML_HEADS)], [dn_s[h:h + 1, :] for h in range(ML_HEADS)],
               [dm_s[h:h + 1, 0:1] for h in range(ML_HEADS)])
        dqs, dks, dvs, dos, dg, dgb, dmlg, dcs, dns, dms = vjp(cot)
        dg_ref[...] = dg
        gb8[0:1, :] += dgb
        for h in range(ML_HEADS):
            sl = slice(LANES * h, LANES * (h + 1))
            dqk_ref[:, sl] = dqs[h]
            dqk_ref[:, ML_W + LANES * h:ML_W + LANES * (h + 1)] = dks[h]
            dv_ref[:, sl] = dvs[h]
            do_ref[:, sl] = dos[h]
            mg8[0:1, sl] += dmlg[h]
            dc_s[h] = dcs[h]
            dn_s[h:h + 1, :] = dns[h]
            dm_s[h:h + 1, :] = jnp.broadcast_to(dms[h], (1, LANES))

        @pl.when(ci == nc - 1)
        def _():
            dgb_ref[...] = gb8[0:1, :]
            dmlg_ref[...] = mg8[0:1, :]

    rev = lambda ci: nc - 1 - ci
    blk = lambda col: pl.BlockSpec((CHUNK, ML_W), lambda ci: (rev(ci), col))
    vec = lambda w: pl.BlockSpec((1, w), lambda ci: (0, 0))
    st8 = pl.BlockSpec((None, 8, LANES), lambda ci: (rev(ci), 0, 0))
    gsp = pl.BlockSpec((CHUNK, LANES), lambda ci: (rev(ci), 0))
    return _call(
        body, name=name, grid=(nc,), sem=("arbitrary",), exchange=exchange,
        args=(qk, qk, proj, proj, gates, gate_bias, mlg, cprev, nprev, mprev, dy),
        in_specs=[blk(0), blk(1), blk(V_COL), blk(O_COL), gsp, vec(LANES), vec(ML_W),
                  pl.BlockSpec((None, ML_HEADS, ML_HD, ML_HD), lambda ci: (rev(ci), 0, 0, 0)), st8, st8, blk(1)],
        out_specs=[pl.BlockSpec((CHUNK, 2 * ML_W), lambda ci: (rev(ci), 0)), blk(0), blk(0), gsp, vec(LANES), vec(ML_W)],
        out_shape=[jax.ShapeDtypeStruct((s, 2 * ML_W), F32),
                   jax.ShapeDtypeStruct((s, ML_W), F32), jax.ShapeDtypeStruct((s, ML_W), F32),
                   jax.ShapeDtypeStruct((s, LANES), F32), jax.ShapeDtypeStruct((1, LANES), F32),
                   jax.ShapeDtypeStruct((1, ML_W), F32)],
        scratch_shapes=[pltpu.VMEM((ML_HEADS, ML_HD, ML_HD), F32), pltpu.VMEM((8, LANES), F32),
                        pltpu.VMEM((8, LANES), F32), pltpu.VMEM((8, LANES), F32), pltpu.VMEM((8, ML_W), F32)])


def _xattn_tile(qs, ks, vs):
    outs = []
    for q, k, v in zip(qs, ks, vs):
        sc = _mm_nt(q, k) * (XA_HD ** -0.5)
        mx = lax.stop_gradient(jnp.max(sc, axis=1, keepdims=True))
        pe = jnp.exp(sc - mx)
        outs.append(_mm_nn(pe / jnp.sum(pe, axis=1, keepdims=True), v))
    return outs


def _xa_heads(ref):
    return [ref[:, XA_HD * h:XA_HD * (h + 1)] for h in range(XA_HEADS)]


def _xattn_fwd(q, kv, name="xattn_fwd", tm=512):
    s, d = q.shape

    def body(q_ref, k_ref, v_ref, o_ref):
        outs = _xattn_tile(_xa_heads(q_ref), _xa_heads(k_ref), _xa_heads(v_ref))
        for h in range(XA_HEADS):
            o_ref[:, XA_HD * h:XA_HD * (h + 1)] = outs[h]

    row = pl.BlockSpec((tm, d), lambda i: (i, 0))
    return pl.pallas_call(
        body, name=name, grid=(s // tm,),
        in_specs=[row, pl.BlockSpec((MEM_LEN, d), lambda i: (0, 0)), pl.BlockSpec((MEM_LEN, d), lambda i: (0, 1))],
        out_specs=row, out_shape=jax.ShapeDtypeStruct((s, d), F32),
        compiler_params=_params("parallel"),
    )(q, kv, kv)


def _xattn_bwd(q, kv, do, name="xattn_bwd", tm=512):
    s, d = q.shape

    def body(q_ref, k_ref, v_ref, do_ref, dq_ref, dkv_ref):
        i = pl.program_id(0)
        _, vjp = jax.vjp(_xattn_tile, _xa_heads(q_ref), _xa_heads(k_ref), _xa_heads(v_ref))
        dqs, dks, dvs = vjp(_xa_heads(do_ref))

        @pl.when(i == 0)
        def _():
            dkv_ref[...] = jnp.zeros_like(dkv_ref)

        for h in range(XA_HEADS):
            sl = slice(XA_HD * h, XA_HD * (h + 1))
            dq_ref[:, sl] = dqs[h]
            dkv_ref[:, sl] += dks[h]
            dkv_ref[:, d + XA_HD * h:d + XA_HD * (h + 1)] += dvs[h]

    row = pl.BlockSpec((tm, d), lambda i: (i, 0))
    return pl.pallas_call(
        body, name=name, grid=(s // tm,),
        in_specs=[row, pl.BlockSpec((MEM_LEN, d), lambda i: (0, 0)), pl.BlockSpec((MEM_LEN, d), lambda i: (0, 1)), row],
        out_specs=[row, pl.BlockSpec((MEM_LEN, 2 * d), lambda i: (0, 0))],
        out_shape=[jax.ShapeDtypeStruct((s, d), F32), jax.ShapeDtypeStruct((MEM_LEN, 2 * d), F32)],
        compiler_params=_params("arbitrary"),
    )(q, kv, kv, do)


def _loss_head(y, target, name="loss_head", tm=512):
    s, d = y.shape
    nt = s // tm

    def body(y_ref, t_ref, dy_ref, loss_ref, acc):
        i = pl.program_id(0)
        err = y_ref[...] - t_ref[...]
        dy_ref[...] = err * (1.0 / d)

        @pl.when(i == 0)
        def _():
            acc[...] = jnp.zeros_like(acc)

        acc[...] += _rowsum8(err * err)

        @pl.when(i == nt - 1)
        def _():
            tot = jnp.sum(jnp.sum(acc[...], axis=0, keepdims=True), axis=1, keepdims=True)
            loss_ref[...] = jnp.broadcast_to(tot * (0.5 / d), (1, LANES))

    row = pl.BlockSpec((tm, d), lambda i: (i, 0))
    return pl.pallas_call(
        body, name=name, grid=(nt,),
        in_specs=[row, row], out_specs=[row, pl.BlockSpec((1, LANES), lambda i: (0, 0))],
        out_shape=[jax.ShapeDtypeStruct((s, d), F32), jax.ShapeDtypeStruct((1, LANES), F32)],
        scratch_shapes=[pltpu.VMEM((8, d), F32)],
        compiler_params=_params("arbitrary"),
    )(y, target)


def _adam2d(recv, w, m, v, name, layer=None):
    rows, cols = w.shape[-2:]
    fits = [t for t in range(16, rows + 1, 16) if rows % t == 0 and t * cols <= 128 * 1024]
    tr = max(fits) if fits else rows

    def body(r_ref, w_ref, m_ref, v_ref, g_ref, d_ref, mo_ref, vo_ref):
        g = r_ref[0].astype(F32)
        for j in range(1, N_DEV):
            g = g + r_ref[j].astype(F32)
        mn = ADAM_B1 * m_ref[...] + (1.0 - ADAM_B1) * g
        vn = ADAM_B2 * v_ref[...] + (1.0 - ADAM_B2) * jnp.square(g)
        m_hat = mn / (1.0 - ADAM_B1 ** ADAM_STEP)
        v_hat = vn / (1.0 - ADAM_B2 ** ADAM_STEP)
        g_ref[...] = g
        d_ref[...] = -ADAM_LR * (m_hat / (jnp.sqrt(v_hat) + ADAM_EPS) + ADAM_WD * w_ref[...])
        mo_ref[...] = mn
        vo_ref[...] = vn

    row = pl.BlockSpec((tr, cols), lambda i: (i, 0))
    if layer is None:
        wspec = row
    else:
        wspec = pl.BlockSpec((None, None, tr, cols), lambda i: (0, layer, i, 0))
    return pl.pallas_call(
        body, name=name, grid=(rows // tr,),
        in_specs=[pl.BlockSpec((N_DEV, tr, cols), lambda i: (0, i, 0)), wspec, wspec, wspec],
        out_specs=[row] * 4, out_shape=[jax.ShapeDtypeStruct((rows, cols), F32)] * 4,
        compiler_params=_params("parallel"),
    )(recv, w, m, v)


WEIGHTS = ("rel_bias", "ln_g", "ln_b", "ffn_w_gate", "ffn_w_up", "ffn_w_down", "w_in", "conv_w", "conv_b",
           "ig_bias", "fg_bias", "ml_norm_g", "w_out", "xq_w", "xkv_w", "xo_w")
SMALL = ("rel_bias", "ln_g", "ln_b", "conv_w", "conv_b", "ig_bias", "fg_bias", "ml_norm_g")
SMALL_SHAPES = {
    "rel_bias": (REL_BUCKETS, ATT_HEADS), "ln_g": (1, 4, LANES), "ln_b": (1, 4, LANES), "conv_w": (1, CONV_K, LANES),
    "conv_b": (1, 2 * ML_W), "ig_bias": (1, ML_HEADS), "fg_bias": (1, ML_HEADS), "ml_norm_g": (1, ML_W),
}
SMALL_ROWS = 8


def _pack_small(parts, lead=()):
    out = []
    for p in parts:
        p = jnp.pad(p, [(0, 0)] * len(lead) + [(0, SMALL_ROWS * LANES - p.shape[-1])])
        out.append(p.reshape(lead + (SMALL_ROWS, LANES)))
    return jnp.concatenate(out, axis=len(lead))


def _unpack_small(flat):
    out = {}
    for i, n in enumerate(SMALL):
        cnt = int(np.prod(SMALL_SHAPES[n]))
        out[n] = flat[SMALL_ROWS * i:SMALL_ROWS * (i + 1)].reshape(-1)[:cnt].reshape(SMALL_SHAPES[n])
    return out


def _split8(full, axis):
    shp = full.shape
    t = full.reshape(shp[:axis] + (N_DEV, shp[axis] // N_DEV) + shp[axis + 1:])
    return jnp.moveaxis(t, axis, 0).reshape(N_DEV, -1)


def _rep8(full):
    return jnp.broadcast_to(full.reshape(1, -1), (N_DEV, full.size))


def kernel(x, mem, rel_bias, ln_g, ln_b, ffn_w_gate, ffn_w_up, ffn_w_down, w_in, conv_w, conv_b, ig_bias, fg_bias, ml_norm_g, w_out, xq_w, xkv_w, xo_w, loss_target, m_rel_bias, m_ln_g, m_ln_b, m_ffn_w_gate, m_ffn_w_up, m_ffn_w_down, m_w_in, m_conv_w, m_conv_b, m_ig_bias, m_fg_bias, m_ml_norm_g, m_w_out, m_xq_w, m_xkv_w, m_xo_w, v_rel_bias, v_ln_g, v_ln_b, v_ffn_w_gate, v_ffn_w_up, v_ffn_w_down, v_w_in, v_conv_w, v_conv_b, v_ig_bias, v_fg_bias, v_ml_norm_g, v_w_out, v_xq_w, v_xkv_w, v_xo_w):
    w_tree = dict(rel_bias=rel_bias, ln_g=ln_g, ln_b=ln_b, ffn_w_gate=ffn_w_gate, ffn_w_up=ffn_w_up,
                  ffn_w_down=ffn_w_down, w_in=w_in, conv_w=conv_w, conv_b=conv_b, ig_bias=ig_bias, fg_bias=fg_bias,
                  ml_norm_g=ml_norm_g, w_out=w_out, xq_w=xq_w, xkv_w=xkv_w, xo_w=xo_w)
    m_tree = dict(rel_bias=m_rel_bias, ln_g=m_ln_g, ln_b=m_ln_b, ffn_w_gate=m_ffn_w_gate, ffn_w_up=m_ffn_w_up,
                  ffn_w_down=m_ffn_w_down, w_in=m_w_in, conv_w=m_conv_w, conv_b=m_conv_b, ig_bias=m_ig_bias,
                  fg_bias=m_fg_bias, ml_norm_g=m_ml_norm_g, w_out=m_w_out, xq_w=m_xq_w, xkv_w=m_xkv_w, xo_w=m_xo_w)
    v_tree = dict(rel_bias=v_rel_bias, ln_g=v_ln_g, ln_b=v_ln_b, ffn_w_gate=v_ffn_w_gate, ffn_w_up=v_ffn_w_up,
                  ffn_w_down=v_ffn_w_down, w_in=v_w_in, conv_w=v_conv_w, conv_b=v_conv_b, ig_bias=v_ig_bias,
                  fg_bias=v_fg_bias, ml_norm_g=v_ml_norm_g, w_out=v_w_out, xq_w=v_xq_w, xkv_w=v_xkv_w, xo_w=v_xo_w)
    x0 = x[0]
    pad_ff = FF_PAD - FF_SHARD
    bf = lambda t: t.astype(BF16)

    pad_rows = lambda t: jnp.pad(t, ((0, pad_ff), (0, 0)))
    ffn_shards = [(pad_rows(bf(ffn_w_gate[0, l]).T), pad_rows(bf(ffn_w_up[0, l]).T), pad_rows(bf(ffn_w_down[0, l])))
                  for l in range(2)]
    pairs = lambda t: t.reshape(N_PAIR, FF_PAIR, D_MODEL)
    w_in_shard = jnp.pad(bf(w_in[0]), ((0, 0), (0, ATT_W - W_IN_SHARD)))
    small_shard = jnp.concatenate([ln_g[0], ln_b[0], conv_w[0], jnp.zeros((4, LANES), F32)], axis=0)
    gate_bias = jnp.pad(jnp.concatenate([ig_bias, fg_bias], axis=1), ((0, 0), (0, LANES - 2 * ML_HEADS)))
    buckets = _bucket_tables()

    wg0, wu0, wd0, small_all = _exchange_only("ffn1_weights_gather", gather=ffn_shards[0] + (small_shard,))
    wg0, wu0, wd0 = pairs(wg0), pairs(wu0), pairs(wd0)
    unshard = lambda t: jnp.moveaxis(t, 0, 1).reshape(4, D_MODEL)
    ln_g_full, ln_b_full, conv_w_full = unshard(small_all[:, 0:4]), unshard(small_all[:, 4:8]), unshard(small_all[:, 8:12])
    lng = lambda i: ln_g_full[i:i + 1]
    lnb = lambda i: ln_b_full[i:i + 1]

    u0, x1, win_all, wout_all, xq_all, xo_all, xkv_all = _ffn_fwd(
        x0, wg0, wu0, wd0, lng(0), lnb(0), "ffn1_fwd",
        gather=(w_in_shard, bf(w_out[0]), bf(xq_w[0]), bf(xo_w[0]), bf(xkv_w[0])))
    w_in_full = jnp.moveaxis(win_all[:, :, :W_IN_SHARD], 0, 1).reshape(D_MODEL, W_IN)
    w_main = w_in_full[:, :W_IN_MAIN]
    w_gate_cols = jnp.pad(w_in_full[:, W_IN_MAIN:], ((0, 0), (0, LANES - 2 * ML_HEADS)))
    w_out_full = wout_all.reshape(D_MODEL, D_MODEL)
    xq_full = xq_all.reshape(D_MODEL, D_MODEL)
    xo_full = xo_all.reshape(D_MODEL, D_MODEL)

    proj, wg1 = _matmul(x1, w_main, "nn", "proj_fwd", tk=D_MODEL, gather=(ffn_shards[1][0],))
    gates, = _matmul(x1, w_gate_cols, "nn", "gates_fwd", tk=D_MODEL)
    biasm = _bias_fwd(rel_bias, buckets)
    o_0, l_0, wd1 = _dil_fwd(proj, biasm, 0, "dil_fwd_0", gather=(ffn_shards[1][2],))
    o_1, l_1 = _dil_fwd(proj, biasm, 1, "dil_fwd_1")
    o_2, l_2 = _dil_fwd(proj, biasm, 2, "dil_fwd_2")
    att, lse = _dil_combine([o_0, o_1, o_2], [l_0, l_1, l_2])
    qk = _conv_fwd(proj, conv_w_full, conv_b)
    y_m, c_prev, n_prev, m_prev, wu1 = _mlstm_fwd(qk, proj, gates, gate_bias, ml_norm_g, gather=(ffn_shards[1][1],))
    cat = jnp.concatenate([att, y_m], axis=1)
    f1, = _matmul(cat, w_out_full, "nn", "w_out_fwd", tn=D_MODEL, tk=D_MODEL)
    u1, x2 = _resid_ln(x1, f1, lng(1), lnb(1), "mixer_ln")
    q_x, = _matmul(x2, xq_full, "nn", "xq_fwd", tn=D_MODEL, tk=D_MODEL)
    kv, = _matmul(mem[0], xkv_all, "nn", "xkv_fwd", tk=D_MODEL)
    o_x = _xattn_fwd(q_x, kv)
    f2, = _matmul(o_x, xo_full, "nn", "xo_fwd", tn=D_MODEL, tk=D_MODEL)
    u2, x3 = _resid_ln(x2, f2, lng(2), lnb(2), "xattn_ln")
    wg1, wu1, wd1 = pairs(wg1), pairs(wu1), pairs(wd1)
    u3, x4 = _ffn_fwd(x3, wg1, wu1, wd1, lng(3), lnb(3), "ffn2_fwd")
    dx4, loss_row = _loss_head(x4, loss_target[0])

    dx3, xb, df, da, db, hh, dg3, db3 = _ffn_bwd_x(dx4, u3, x3, wg1, wu1, wd1, lng(3), "ffn2_bwd_x")
    ffn2_send = _ffn_bwd_w(xb, df, da, db, hh, "ffn2_bwd_w")

    du2, dg2, db2 = _ln_bwd(dx3, u2, lng(2), "xattn_ln_bwd")
    do_x, = _matmul(du2, xo_full, "nt", "xo_bwd_x", tn=D_MODEL)
    g_xo, = _matmul(o_x, du2, "tn", "xo_bwd_w", tm=D_MODEL, out_dtype=BF16)
    dq_x, dkv = _xattn_bwd(q_x, kv, do_x)
    g_xq, = _matmul(x2, dq_x, "tn", "xq_bwd_w", tm=D_MODEL, out_dtype=BF16)
    g_xkv, = _matmul(mem[0], dkv, "tn", "xkv_bwd_w", tm=D_MODEL, tn=2 * D_MODEL // N_DEV, tk=MEM_LEN,
                     out_dtype=BF16, blocked_out=True)
    dx2, = _matmul(dq_x, xq_full, "nt", "xq_bwd_x", tn=D_MODEL, add=du2, add_scale=ALPHA)

    du1, dg1, db1 = _ln_bwd(dx2, u1, lng(1), "mixer_ln_bwd")
    dcat, = _matmul(du1, w_out_full, "nt", "w_out_bwd_x", tn=D_MODEL)
    g_w_out, = _matmul(cat, du1, "tn", "w_out_bwd_w", tm=D_MODEL, out_dtype=BF16)
    dqk, dv_m, do_m, dgates, dgate_bias, g_mlg, *ffn2_recv = _mlstm_bwd(
        qk, proj, gates, gate_bias, ml_norm_g, c_prev, n_prev, m_prev, dcat, exchange=tuple(ffn2_send))
    dqk_pre, g_conv_w, g_conv_b = _conv_bwd(proj, dqk, conv_w_full, conv_b)
    datt = dcat[:, :ATT_W]
    acc, dbias = None, []
    for b in range(len(DILATED)):
        acc, dbb = _dil_bwd(proj, biasm, lse, att, datt, acc, b, f"dil_bwd_{b}")
        dbias.append(dbb)
    g_rel = _bias_bwd(jnp.stack(dbias), buckets)[:, :ATT_HEADS]
    dproj = jnp.concatenate([acc[0], acc[1], acc[2], dqk_pre, dv_m, do_m], axis=1).astype(BF16)
    g_w_main, = _matmul(x1, dproj, "tn", "proj_bwd_w", tm=D_MODEL, tn=W_IN_MAIN // 2, out_dtype=BF16)
    g_w_gates, = _matmul(x1, dgates, "tn", "gates_bwd_w", tm=D_MODEL, out_dtype=BF16)
    g_w_in = jnp.concatenate([g_w_main, g_w_gates[:, :2 * ML_HEADS]], axis=1)
    dx1, = _matmul(dproj, w_main, "nt", "proj_bwd_x", tn=D_MODEL, add=du1, add_scale=ALPHA)
    dx1, = _matmul(dgates, w_gate_cols, "nt", "gates_bwd_x", tn=D_MODEL, add=dx1)

    rows8 = lambda t: t.reshape(N_DEV, D_MODEL // N_DEV, D_MODEL)
    mid_send = (rows8(g_xo), rows8(g_xq), g_xkv, rows8(g_w_out),
                jnp.moveaxis(g_w_in.reshape(D_MODEL, N_DEV, W_IN_SHARD), 1, 0))
    dx0, xb, df, da, db, hh, dg0, db0, r_xo, r_xq, r_xkv, r_w_out, r_w_in = _ffn_bwd_x(
        dx1, u0, x0, wg0, wu0, wd0, lng(0), "ffn1_bwd_x", exchange=mid_send)
    ffn1_send = _ffn_bwd_w(xb, df, da, db, hh, "ffn1_bwd_w")
    small_blocks = {
        "rel_bias": _rep8(g_rel),
        "ln_g": _split8(jnp.concatenate([dg0, dg1, dg2, dg3], axis=0), 1),
        "ln_b": _split8(jnp.concatenate([db0, db1, db2, db3], axis=0), 1),
        "conv_w": _split8(g_conv_w, 1),
        "conv_b": _rep8(g_conv_b),
        "ig_bias": _rep8(dgate_bias[:, :ML_HEADS]),
        "fg_bias": _rep8(dgate_bias[:, ML_HEADS:2 * ML_HEADS]),
        "ml_norm_g": _rep8(g_mlg),
    }
    small_send = _pack_small([small_blocks[n] for n in SMALL], lead=(N_DEV,))
    *ffn1_recv, r_small = _exchange_only("ffn1_grads_exchange", exchange=tuple(ffn1_send) + (small_send,))

    res = {}
    for i, n in enumerate(("ffn_w_gate", "ffn_w_up", "ffn_w_down")):
        per_layer = [_adam2d(r[i], w_tree[n], m_tree[n], v_tree[n], f"adamw_{n}_{l}", layer=l)
                     for l, r in enumerate((ffn1_recv, ffn2_recv))]
        res[n] = [jnp.stack([per_layer[0][j], per_layer[1][j]])[None] for j in range(4)]
    for n, r in (("w_in", r_w_in), ("w_out", r_w_out), ("xq_w", r_xq), ("xkv_w", r_xkv), ("xo_w", r_xo)):
        res[n] = [t[None] for t in _adam2d(r, w_tree[n][0], m_tree[n][0], v_tree[n][0], f"adamw_{n}")]
    pack = lambda tree: _pack_small([tree[n].reshape(-1) for n in SMALL])
    small = [_unpack_small(t) for t in _adam2d(r_small, pack(w_tree), pack(m_tree), pack(v_tree), "adamw_small")]
    for n in SMALL:
        res[n] = [small[j][n] for j in range(4)]

    loss = lax.psum(loss_row[0, 0], ("x", "y", "c"))
    return (loss, dx0[None], *[res[n][0] for n in WEIGHTS], *[res[n][1] for n in WEIGHTS],
            *[res[n][2] for n in WEIGHTS], *[res[n][3] for n in WEIGHTS])
```

```python
import functools
import math

import numpy as np
import jax
import jax.numpy as jnp
from jax import lax
from jax.experimental import pallas as pl
from jax.experimental.pallas import tpu as pltpu

F32 = jnp.float32
BF16 = jnp.bfloat16

N_DEV = 8
D_MODEL = 1024
D_FF = 2816
FF_SHARD = D_FF // N_DEV
FF_PAD = 384
ATT_W = 512
ATT_HEADS = 8
DILATED = ((128, 1), (512, 4), (2048, 16))
BLK = 128
ML_W = 512
ML_HEADS = 4
ML_HD = 128
CHUNK = 128
CONV_K = 4
W_IN = 3592
W_IN_SHARD = W_IN // N_DEV
W_IN_MAIN = 3584
XA_HEADS = 4
XA_HD = 256
MEM_LEN = 256
REL_BUCKETS = 32
REL_MAX_DIST = 2048
ALPHA = 2.0 ** 0.25
LN_EPS = 1e-5
NEG = -1e30
ADAM_LR = 0.001
ADAM_B1 = 0.9
ADAM_B2 = 0.999
ADAM_EPS = 1e-08
ADAM_WD = 0.01
ADAM_STEP = 10
LANES = 128
VMEM_LIMIT = 58 * 1024 * 1024

NN = (((1,), (0,)), ((), ()))
NT = (((1,), (1,)), ((), ()))
TN = (((0,), (0,)), ((), ()))


def _dot(a, b, dims):
    return lax.dot_general(a, b, dims, preferred_element_type=F32)


def _params(*sem):
    return pltpu.CompilerParams(dimension_semantics=sem, vmem_limit_bytes=VMEM_LIMIT)


def _sigmoid(x):
    return 1.0 / (1.0 + jnp.exp(-x))


def _rowsum8(x):
    t, c = x.shape
    return jnp.sum(x.reshape(t // 8, 8, c), axis=0)


def _mesh_pos():
    x, y, c = lax.axis_index("x"), lax.axis_index("y"), lax.axis_index("c")
    return x, y, c, 4 * x + 2 * y + c


def _peer(x, y, c, k):
    px = 1 - x if k & 4 else x
    py = 1 - y if k & 2 else y
    pc = 1 - c if k & 1 else c
    return (px, py, pc), 4 * px + 2 * py + pc


def _call(body, *, name, grid, in_specs, out_specs, out_shape, args, scratch_shapes=(), sem=None,
          gather=(), exchange=()):
    in_specs, out_specs, out_shape, scratch = list(in_specs), list(out_specs), list(out_shape), list(scratch_shapes)
    ng, nc = len(gather), len(gather) + len(exchange)
    if nc == 0:
        return pl.pallas_call(body, name=name, grid=grid, in_specs=in_specs, out_specs=out_specs,
                              out_shape=out_shape, scratch_shapes=scratch, compiler_params=_params(*sem))(*args)
    n_in, n_out, n_scr = len(in_specs), len(out_specs), len(scratch)

    def wrapped(*refs):
        ins, cin = refs[:n_in], refs[n_in:n_in + nc]
        outs, cout = refs[n_in + nc:n_in + nc + n_out], refs[n_in + nc + n_out:n_in + 2 * nc + n_out]
        scr = refs[n_in + 2 * nc + n_out:n_in + 2 * nc + n_out + n_scr]
        send_sems, recv_sems, loc_sems = refs[-3:]
        first, last = None, None
        for ax, extent in enumerate(grid):
            f, l = pl.program_id(ax) == 0, pl.program_id(ax) == extent - 1
            first = f if first is None else first & f
            last = l if last is None else last & l

        def copies():
            x, y, c, me = _mesh_pos()
            out = []
            for a in range(nc):
                mine = cin[a] if a < ng else cin[a].at[me]
                out.append(pltpu.make_async_copy(mine, cout[a].at[me], loc_sems.at[a]))
                for k in range(1, N_DEV):
                    peer, pidx = _peer(x, y, c, k)
                    out.append(pltpu.make_async_remote_copy(
                        src_ref=cin[a] if a < ng else cin[a].at[pidx], dst_ref=cout[a].at[me],
                        send_sem=send_sems.at[a, k - 1], recv_sem=recv_sems.at[a, k - 1],
                        device_id=peer, device_id_type=pl.DeviceIdType.MESH))
            return out

        @pl.when(first)
        def _():
            for cp in copies():
                cp.start()

        body(*ins, *outs, *scr)

        @pl.when(last)
        def _():
            for cp in copies():
                cp.wait()

    hbm = pl.BlockSpec(memory_space=pl.ANY)
    comm_shapes = [jax.ShapeDtypeStruct((N_DEV,) + a.shape, a.dtype) for a in gather]
    comm_shapes += [jax.ShapeDtypeStruct(a.shape, a.dtype) for a in exchange]
    return pl.pallas_call(
        wrapped, name=name, grid=grid, in_specs=in_specs + [hbm] * nc, out_specs=out_specs + [hbm] * nc,
        out_shape=out_shape + comm_shapes,
        scratch_shapes=scratch + [pltpu.SemaphoreType.DMA((nc, N_DEV - 1)), pltpu.SemaphoreType.DMA((nc, N_DEV - 1)),
                                  pltpu.SemaphoreType.DMA((nc,))],
        compiler_params=_params(*(("arbitrary",) * len(grid))),
    )(*args, *gather, *exchange)


def _exchange_only(name, gather=(), exchange=()):
    return _call(lambda: None, name=name, grid=(1,), in_specs=[], out_specs=[], out_shape=[], args=(),
                 gather=gather, exchange=exchange)


def _matmul(a, b, mode, name, *, out_dtype=F32, tm=512, tn=512, tk=512, add=None, add_scale=1.0,
            blocked_out=False, gather=(), exchange=()):
    blocked_b = b.ndim == 3
    if blocked_b:
        (m, k), (nb, _, tn) = a.shape, b.shape
        n = nb * tn
    elif mode == "nn":
        (m, k), (_, n) = a.shape, b.shape
    elif mode == "nt":
        (m, k), (n, _) = a.shape, b.shape
    else:
        (k, m), (_, n) = a.shape, b.shape
    tm, tn, tk = min(tm, m), min(tn, n), min(tk, k)
    nk = k // tk
    dims = {"nn": NN, "nt": NT, "tn": TN}[mode]
    if mode == "tn":
        a_spec = pl.BlockSpec((tk, tm), lambda i, j, kk: (kk, i))
    else:
        a_spec = pl.BlockSpec((tm, tk), lambda i, j, kk: (i, kk))
    if blocked_b:
        b_spec = pl.BlockSpec((None, tk, tn), lambda i, j, kk: (j, kk, 0))
    elif mode == "nt":
        b_spec = pl.BlockSpec((tn, tk), lambda i, j, kk: (j, kk))
    else:
        b_spec = pl.BlockSpec((tk, tn), lambda i, j, kk: (kk, j))
    if blocked_out:
        o_spec = pl.BlockSpec((None, tm, tn), lambda i, j, kk: (j, i, 0))
        o_shape = jax.ShapeDtypeStruct((n // tn, m, tn), out_dtype)
    else:
        o_spec = pl.BlockSpec((tm, tn), lambda i, j, kk: (i, j))
        o_shape = jax.ShapeDtypeStruct((m, n), out_dtype)
    has_add = add is not None
    cache_a = nk == 1 and mode != "tn" and n // tn > 1 and a.dtype != BF16

    def body(*refs):
        if has_add:
            a_ref, b_ref, add_ref, o_ref, s_ref = refs
        else:
            a_ref, b_ref, o_ref, s_ref = refs
        kk = pl.program_id(2)
        if cache_a:
            @pl.when(pl.program_id(1) == 0)
            def _():
                s_ref[...] = a_ref[...].astype(BF16)

            lhs = s_ref[...]
        else:
            lhs = a_ref[...].astype(BF16)
        part = _dot(lhs, b_ref[...].astype(BF16), dims)

        def finish(r):
            if has_add:
                r = r + add_scale * add_ref[...]
            o_ref[...] = r.astype(out_dtype)

        if nk == 1:
            finish(part)
            return

        @pl.when(kk == 0)
        def _():
            s_ref[...] = part

        @pl.when(kk > 0)
        def _():
            s_ref[...] += part

        @pl.when(kk == nk - 1)
        def _():
            finish(s_ref[...])

    if nk > 1:
        scratch = [pltpu.VMEM((tm, tn), F32)]
    else:
        scratch = [pltpu.VMEM((tm, tk), BF16) if cache_a else pltpu.VMEM((8, LANES), F32)]
    return _call(
        body, name=name, grid=(m // tm, n // tn, nk),
        in_specs=[a_spec, b_spec] + ([pl.BlockSpec((tm, tn), lambda i, j, kk: (i, j))] if has_add else []),
        out_specs=[o_spec], out_shape=[o_shape], args=(a, b) + ((add,) if has_add else ()),
        scratch_shapes=scratch, sem=("parallel", "arbitrary", "arbitrary"),
        gather=gather, exchange=exchange)


def _ln_fwd_math(u, g, b):
    mu = jnp.mean(u, axis=-1, keepdims=True)
    uc = u - mu
    var = jnp.mean(uc * uc, axis=-1, keepdims=True)
    return uc * lax.rsqrt(var + LN_EPS) * g + b


def _ln_bwd_math(dy, u, g):
    mu = jnp.mean(u, axis=-1, keepdims=True)
    uc = u - mu
    var = jnp.mean(uc * uc, axis=-1, keepdims=True)
    rstd = lax.rsqrt(var + LN_EPS)
    xhat = uc * rstd
    dxh = dy * g
    m1 = jnp.mean(dxh, axis=-1, keepdims=True)
    m2 = jnp.mean(dxh * xhat, axis=-1, keepdims=True)
    return rstd * (dxh - m1 - xhat * m2), xhat


def _resid_ln(x, f, g, b, name, tm=512):
    s, d = x.shape

    def body(x_ref, f_ref, g_ref, b_ref, u_ref, y_ref):
        u = ALPHA * x_ref[...] + f_ref[...]
        u_ref[...] = u
        y_ref[...] = _ln_fwd_math(u, g_ref[...], b_ref[...])

    row = pl.BlockSpec((tm, d), lambda i: (i, 0))
    vec = pl.BlockSpec((1, d), lambda i: (0, 0))
    return pl.pallas_call(
        body, name=name, grid=(s // tm,),
        in_specs=[row, row, vec, vec], out_specs=[row, row],
        out_shape=[jax.ShapeDtypeStruct((s, d), F32)] * 2,
        compiler_params=_params("parallel"),
    )(x, f, g, b)


def _ln_bwd(dy, u, g, name, tm=512):
    s, d = dy.shape
    nt = s // tm

    def body(dy_ref, u_ref, g_ref, du_ref, dg_ref, db_ref, g8, b8):
        i = pl.program_id(0)
        dy_ = dy_ref[...]
        du, xhat = _ln_bwd_math(dy_, u_ref[...], g_ref[...])
        du_ref[...] = du

        @pl.when(i == 0)
        def _():
            g8[...] = jnp.zeros_like(g8)
            b8[...] = jnp.zeros_like(b8)

        g8[...] += _rowsum8(dy_ * xhat)
        b8[...] += _rowsum8(dy_)

        @pl.when(i == nt - 1)
        def _():
            dg_ref[...] = jnp.sum(g8[...], axis=0, keepdims=True)
            db_ref[...] = jnp.sum(b8[...], axis=0, keepdims=True)

    row = pl.BlockSpec((tm, d), lambda i: (i, 0))
    vec = pl.BlockSpec((1, d), lambda i: (0, 0))
    return pl.pallas_call(
        body, name=name, grid=(nt,),
        in_specs=[row, row, vec], out_specs=[row, vec, vec],
        out_shape=[jax.ShapeDtypeStruct((s, d), F32), jax.ShapeDtypeStruct((1, d), F32),
                   jax.ShapeDtypeStruct((1, d), F32)],
        scratch_shapes=[pltpu.VMEM((8, d), F32), pltpu.VMEM((8, d), F32)],
        compiler_params=_params("arbitrary"),
    )(dy, u, g)


FF_PAIR = 2 * FF_PAD
N_PAIR = N_DEV // 2


def _ffn_fwd(x, wgt, wut, wd, g, b, name, tm=512, gather=()):
    s, d = x.shape

    def body(x_ref, wg_ref, wu_ref, wd_ref, g_ref, b_ref, u_ref, y_ref, xb, acc):
        k = pl.program_id(1)

        @pl.when(k == 0)
        def _():
            xb[...] = x_ref[...].astype(BF16)

        a = _dot(xb[...], wg_ref[...], NT)
        bb = _dot(xb[...], wu_ref[...], NT)
        h = (a * _sigmoid(a) * bb).astype(BF16)
        part = _dot(h, wd_ref[...], NN)

        @pl.when(k == 0)
        def _():
            acc[...] = part

        @pl.when(k > 0)
        def _():
            acc[...] += part

        @pl.when(k == N_PAIR - 1)
        def _():
            u = ALPHA * x_ref[...] + 0.5 * acc[...]
            u_ref[...] = u
            y_ref[...] = _ln_fwd_math(u, g_ref[...], b_ref[...])

    row = pl.BlockSpec((tm, d), lambda i, k: (i, 0))
    vec = pl.BlockSpec((1, d), lambda i, k: (0, 0))
    w_in = pl.BlockSpec((None, FF_PAIR, d), lambda i, k: (k, 0, 0))
    w_dn = w_in
    return _call(
        body, name=name, grid=(s // tm, N_PAIR),
        in_specs=[row, w_in, w_in, w_dn, vec, vec], out_specs=[row, row],
        out_shape=[jax.ShapeDtypeStruct((s, d), F32)] * 2, args=(x, wgt, wut, wd, g, b),
        scratch_shapes=[pltpu.VMEM((tm, d), BF16), pltpu.VMEM((tm, d), F32)],
        sem=("parallel", "arbitrary"), gather=gather)


def _ffn_bwd_x(dy, u, x, wgt, wut, wd, g, name, tm=512, exchange=()):
    s, d = x.shape
    nt = s // tm
    ffp = N_DEV * FF_PAD

    def body(dy_ref, u_ref, x_ref, wg_ref, wu_ref, wd_ref, g_ref,
             dx_ref, xb, df_ref, da_ref, db_ref, h_ref, dg_ref, dbl_ref,
             dfb, du_s, acc, g8, b8):
        i = pl.program_id(0)
        k = pl.program_id(1)

        @pl.when(k == 0)
        def _():
            dy_ = dy_ref[...]
            du, xhat = _ln_bwd_math(dy_, u_ref[...], g_ref[...])
            du_s[...] = du
            dfb[...] = (0.5 * du).astype(BF16)
            df_ref[...] = dfb[...]
            xb[...] = x_ref[...].astype(BF16)

            @pl.when(i == 0)
            def _():
                g8[...] = jnp.zeros_like(g8)
                b8[...] = jnp.zeros_like(b8)

            g8[...] += _rowsum8(dy_ * xhat)
            b8[...] += _rowsum8(dy_)

        a = _dot(xb[...], wg_ref[...], NT)
        bb = _dot(xb[...], wu_ref[...], NT)
        sig = _sigmoid(a)
        sa = a * sig
        h_ref[...] = (sa * bb).astype(BF16)
        dh = _dot(dfb[...], wd_ref[...], NT)
        da = (dh * bb * (sig * (1.0 + a * (1.0 - sig)))).astype(BF16)
        db = (dh * sa).astype(BF16)
        da_ref[...] = da
        db_ref[...] = db
        part = _dot(da, wg_ref[...], NN) + _dot(db, wu_ref[...], NN)

        @pl.when(k == 0)
        def _():
            acc[...] = part

        @pl.when(k > 0)
        def _():
            acc[...] += part

        @pl.when(k == N_PAIR - 1)
        def _():
            dx_ref[...] = ALPHA * du_s[...] + acc[...]

        @pl.when((k == N_PAIR - 1) & (i == nt - 1))
        def _():
            dg_ref[...] = jnp.sum(g8[...], axis=0, keepdims=True)
            dbl_ref[...] = jnp.sum(b8[...], axis=0, keepdims=True)

    row = pl.BlockSpec((tm, d), lambda i, k: (i, 0))
    vec = pl.BlockSpec((1, d), lambda i, k: (0, 0))
    w_in = pl.BlockSpec((None, FF_PAIR, d), lambda i, k: (k, 0, 0))
    hid = pl.BlockSpec((tm, FF_PAIR), lambda i, k: (i, k))
    return _call(
        body, name=name, grid=(nt, N_PAIR),
        in_specs=[row, row, row, w_in, w_in, w_in, vec],
        out_specs=[row, row, row, hid, hid, hid, vec, vec],
        out_shape=[jax.ShapeDtypeStruct((s, d), F32), jax.ShapeDtypeStruct((s, d), BF16),
                   jax.ShapeDtypeStruct((s, d), BF16),
                   jax.ShapeDtypeStruct((s, ffp), BF16), jax.ShapeDtypeStruct((s, ffp), BF16),
                   jax.ShapeDtypeStruct((s, ffp), BF16),
                   jax.ShapeDtypeStruct((1, d), F32), jax.ShapeDtypeStruct((1, d), F32)],
        args=(dy, u, x, wgt, wut, wd, g),
        scratch_shapes=[pltpu.VMEM((tm, d), BF16), pltpu.VMEM((tm, d), F32),
                        pltpu.VMEM((tm, d), F32), pltpu.VMEM((8, d), F32), pltpu.VMEM((8, d), F32)],
        sem=("arbitrary", "arbitrary"), exchange=exchange)


def _ffn_bwd_w(xb, df, da, db, h, name, tm=512):
    s, d = xb.shape
    nt = s // tm

    def body(x_ref, df_ref, da_ref, db_ref, h_ref, dwg_ref, dwu_ref, dwd_ref, ag, au, ad):
        i = pl.program_id(1)
        pg = _dot(x_ref[...], da_ref[...], TN)
        pu = _dot(x_ref[...], db_ref[...], TN)
        pd = _dot(h_ref[...], df_ref[...], TN)

        @pl.when(i == 0)
        def _():
            ag[...] = pg
            au[...] = pu
            ad[...] = pd

        @pl.when(i > 0)
        def _():
            ag[...] += pg
            au[...] += pu
            ad[...] += pd

        @pl.when(i == nt - 1)
        def _():
            for j in range(2):
                lo = j * FF_PAD
                dwg_ref[j] = ag[:, lo:lo + FF_SHARD].astype(BF16)
                dwu_ref[j] = au[:, lo:lo + FF_SHARD].astype(BF16)
                dwd_ref[j] = ad[lo:lo + FF_SHARD, :].astype(BF16)

    row = pl.BlockSpec((tm, d), lambda k, i: (i, 0))
    hid = pl.BlockSpec((tm, FF_PAIR), lambda k, i: (i, k))
    w_in = pl.BlockSpec((2, d, FF_SHARD), lambda k, i: (k, 0, 0))
    w_dn = pl.BlockSpec((2, FF_SHARD, d), lambda k, i: (k, 0, 0))
    return _call(
        body, name=name, grid=(N_PAIR, nt),
        in_specs=[row, row, hid, hid, hid], out_specs=[w_in, w_in, w_dn],
        out_shape=[jax.ShapeDtypeStruct((N_DEV, d, FF_SHARD), BF16), jax.ShapeDtypeStruct((N_DEV, d, FF_SHARD), BF16),
                   jax.ShapeDtypeStruct((N_DEV, FF_SHARD, d), BF16)],
        args=(xb, df, da, db, h),
        scratch_shapes=[pltpu.VMEM((d, FF_PAIR), F32), pltpu.VMEM((d, FF_PAIR), F32), pltpu.VMEM((FF_PAIR, d), F32)],
        sem=("parallel", "arbitrary"))


def _bucket_tables():
    qi = np.arange(BLK)[:, None]
    ki = np.arange(2 * BLK)[None, :]
    off = qi + BLK - ki
    out = []
    for window, dil in DILATED:
        n_keys = window // dil
        dist = dil * np.clip(off, 0, n_keys)
        exact = REL_BUCKETS // 2
        df = np.maximum(dist, 1).astype(np.float32)
        large = exact + (np.log(df / np.float32(exact)) / np.float32(math.log(REL_MAX_DIST / exact))
                         * np.float32(REL_BUCKETS - exact)).astype(np.int32)
        large = np.minimum(large, REL_BUCKETS - 1)
        bucket = np.where(dist < exact, dist, large).astype(np.int32)
        band = (off >= 0) & (off <= n_keys)
        out.append(np.where(band, bucket, -1))
    return np.stack(out).astype(np.int32)


def _bias_fwd(rel_bias, buckets, name="bias_fwd"):
    def body(tbl_ref, bkt_ref, out_ref):
        bkt = bkt_ref[...]
        for h in range(ATT_HEADS):
            acc = jnp.full((BLK, 2 * BLK), NEG, F32)
            for bb in range(REL_BUCKETS):
                acc = jnp.where(bkt == bb, tbl_ref[bb, h], acc)
            out_ref[h] = acc

    nbr = len(DILATED)
    return pl.pallas_call(
        body, name=name, grid=(nbr,),
        in_specs=[pl.BlockSpec(memory_space=pltpu.SMEM),
                  pl.BlockSpec((None, BLK, 2 * BLK), lambda r: (r, 0, 0))],
        out_specs=pl.BlockSpec((None, ATT_HEADS, BLK, 2 * BLK), lambda r: (r, 0, 0, 0)),
        out_shape=jax.ShapeDtypeStruct((nbr, ATT_HEADS, BLK, 2 * BLK), F32),
        compiler_params=_params("parallel"),
    )(rel_bias, buckets)


def _bias_bwd(dbias, buckets, name="bias_bwd"):
    nbr = len(DILATED)

    def body(db_ref, bkt_ref, out_ref):
        r = pl.program_id(0)

        @pl.when(r == 0)
        def _():
            out_ref[...] = jnp.zeros_like(out_ref)

        bkt = bkt_ref[...]
        rowi = lax.broadcasted_iota(jnp.int32, (REL_BUCKETS, LANES), 0)
        coli = lax.broadcasted_iota(jnp.int32, (REL_BUCKETS, LANES), 1)
        acc = jnp.zeros((REL_BUCKETS, LANES), F32)
        for h in range(ATT_HEADS):
            x = db_ref[h]
            for bb in range(REL_BUCKETS):
                part = jnp.sum(jnp.where(bkt == bb, x, 0.0), axis=0, keepdims=True)
                tot = jnp.sum(part, axis=1, keepdims=True)
                acc = acc + jnp.where((rowi == bb) & (coli == h), tot, 0.0)
        out_ref[...] += acc

    return pl.pallas_call(
        body, name=name, grid=(nbr,),
        in_specs=[pl.BlockSpec((None, ATT_HEADS, BLK, 2 * BLK), lambda r: (r, 0, 0, 0)),
                  pl.BlockSpec((None, BLK, 2 * BLK), lambda r: (r, 0, 0))],
        out_specs=pl.BlockSpec((REL_BUCKETS, LANES), lambda r: (0, 0)),
        out_shape=jax.ShapeDtypeStruct((REL_BUCKETS, LANES), F32),
        compiler_params=_params("arbitrary"),
    )(dbias, buckets)


def _att_scores(q_pair, k2, bias, first_ok, msk):
    qm = jnp.where(msk, q_pair, 0.0).astype(BF16)
    sc = _dot(qm, k2, NT) * (64 ** -0.5) + bias
    return jnp.where(first_ok, sc, NEG), qm


DIL_TILE = 2048
DIL_COLS = ATT_W // LANES


def _dil_rows(dil, n, r, base=0):
    start = base + n * (BLK * dil) + r
    return pl.ds(start, BLK, stride=dil) if dil > 1 else pl.ds(start, BLK)


def _dil_in_specs(tile_of):
    cur = lambda col: pl.BlockSpec((DIL_TILE, LANES), lambda p, i: (tile_of(i), col * DIL_COLS + p))
    prev = lambda col: pl.BlockSpec((DIL_TILE, LANES), lambda p, i: (jnp.maximum(tile_of(i) - 1, 0), col * DIL_COLS + p))
    bias = pl.BlockSpec((len(DILATED), 2, BLK, 2 * BLK), lambda p, i: (0, p, 0, 0))
    return [cur(0), prev(1), cur(1), prev(2), cur(2), bias]


def _dil_fwd(proj, biasm, name="dil_fwd", gather=()):
    s = proj.shape[0]
    nt = s // DIL_TILE
    tt = DIL_TILE

    def body(q_ref, kp_ref, kc_ref, vp_ref, vc_ref, bias_ref, att_ref, lse_ref, k2, v2, ob, lb):
        t = pl.program_id(1)
        k2[0:tt, :] = kp_ref[...]
        k2[tt:2 * tt, :] = kc_ref[...]
        v2[0:tt, :] = vp_ref[...]
        v2[tt:2 * tt, :] = vc_ref[...]
        lo = lax.broadcasted_iota(jnp.int32, (BLK, LANES), 1) < 64
        kidx = lax.broadcasted_iota(jnp.int32, (BLK, 2 * BLK), 1)
        for b, (_, dil) in enumerate(DILATED):
            nblk = tt // (BLK * dil)

            def step(j, carry, b=b, dil=dil, nblk=nblk):
                r, n = j // nblk, j % nblk
                cur, prev = _dil_rows(dil, n, r, tt), _dil_rows(dil, n - 1, r, tt)
                here = _dil_rows(dil, n, r)
                q_pair = q_ref[here, :]
                kk = jnp.concatenate([k2[prev, :], k2[cur, :]], axis=0).astype(BF16)
                vv = jnp.concatenate([v2[prev, :], v2[cur, :]], axis=0).astype(BF16)
                first_ok = (t > 0) | (n > 0) | (kidx >= BLK)
                outs, lses = [], []
                for hh in range(2):
                    msk = lo if hh == 0 else jnp.logical_not(lo)
                    sc, _ = _att_scores(q_pair, kk, bias_ref[b, hh], first_ok, msk)
                    mx = jnp.max(sc, axis=1, keepdims=True)
                    pe = jnp.exp(sc - mx)
                    l = jnp.sum(pe, axis=1, keepdims=True)
                    outs.append(_dot(pe.astype(BF16), vv, NN) / l)
                    lses.append(jnp.broadcast_to(mx + jnp.log(l), (BLK, LANES)))
                ob.at[b][here, :] = jnp.where(lo, outs[0], outs[1])
                lb.at[b][here, :] = jnp.where(lo, lses[0], lses[1])
                return carry

            lax.fori_loop(0, tt // BLK, step, 0)
        l0, l1, l2 = lb[0], lb[1], lb[2]
        mx = jnp.maximum(jnp.maximum(l0, l1), l2)
        e0, e1, e2 = jnp.exp(l0 - mx), jnp.exp(l1 - mx), jnp.exp(l2 - mx)
        tot = e0 + e1 + e2
        att_ref[...] = (e0 * ob[0] + e1 * ob[1] + e2 * ob[2]) / tot
        lse_ref[...] = mx + jnp.log(tot)

    out = pl.BlockSpec((tt, LANES), lambda p, i: (i, p))
    return _call(
        body, name=name, grid=(DIL_COLS, nt), in_specs=_dil_in_specs(lambda i: i), out_specs=[out, out],
        out_shape=[jax.ShapeDtypeStruct((s, ATT_W), F32)] * 2, args=(proj, proj, proj, proj, proj, biasm),
        scratch_shapes=[pltpu.VMEM((2 * tt, LANES), F32), pltpu.VMEM((2 * tt, LANES), F32),
                        pltpu.VMEM((len(DILATED), tt, LANES), F32), pltpu.VMEM((len(DILATED), tt, LANES), F32)],
        sem=("parallel", "parallel"), gather=gather)


def _dil_bwd(proj, biasm, lse, att, dcat, name="dil_bwd"):
    s = proj.shape[0]
    nt = s // DIL_TILE
    tt = DIL_TILE
    nbr = len(DILATED)

    def body(q_ref, kp_ref, kc_ref, vp_ref, vc_ref, bias_ref, lse_ref, att_ref, datt_ref,
             dq_ref, dk_ref, dv_ref, dbias_ref, k2, v2, dqa, dka, dva, kcar, vcar):
        i = pl.program_id(1)
        t = nt - 1 - i
        k2[0:tt, :] = kp_ref[...]
        k2[tt:2 * tt, :] = kc_ref[...]
        v2[0:tt, :] = vp_ref[...]
        v2[tt:2 * tt, :] = vc_ref[...]

        @pl.when(i == 0)
        def _():
            kcar[...] = jnp.zeros_like(kcar)
            vcar[...] = jnp.zeros_like(vcar)
            dbias_ref[...] = jnp.zeros_like(dbias_ref)

        dqa[...] = jnp.zeros_like(dqa)
        dka[0:tt, :] = jnp.zeros((tt, LANES), F32)
        dva[0:tt, :] = jnp.zeros((tt, LANES), F32)
        dka[tt:2 * tt, :] = kcar[...]
        dva[tt:2 * tt, :] = vcar[...]
        lo = lax.broadcasted_iota(jnp.int32, (BLK, LANES), 1) < 64
        kidx = lax.broadcasted_iota(jnp.int32, (BLK, 2 * BLK), 1)
        for b, (_, dil) in enumerate(DILATED):
            nblk = tt // (BLK * dil)

            def step(j, carry, b=b, dil=dil, nblk=nblk):
                r, n = j // nblk, j % nblk
                cur, prev = _dil_rows(dil, n, r, tt), _dil_rows(dil, n - 1, r, tt)
                here = _dil_rows(dil, n, r)
                q_pair = q_ref[here, :]
                kk = jnp.concatenate([k2[prev, :], k2[cur, :]], axis=0).astype(BF16)
                vv = jnp.concatenate([v2[prev, :], v2[cur, :]], axis=0).astype(BF16)
                first_ok = (t > 0) | (n > 0) | (kidx >= BLK)
                lse_pair = lse_ref[here, :]
                dat_pair = datt_ref[here, :]
                dd_pair = dat_pair * att_ref[here, :]
                dqs, dk2, dv2 = [], None, None
                for hh in range(2):
                    msk = lo if hh == 0 else jnp.logical_not(lo)
                    sc, qm = _att_scores(q_pair, kk, bias_ref[b, hh], first_ok, msk)
                    lse_h = jnp.max(jnp.where(msk, lse_pair, -jnp.inf), axis=1, keepdims=True)
                    pr = jnp.exp(sc - lse_h)
                    dsum = jnp.sum(jnp.where(msk, dd_pair, 0.0), axis=1, keepdims=True)
                    dom = jnp.where(msk, dat_pair, 0.0).astype(BF16)
                    ds = pr * (_dot(dom, vv, NT) - dsum)
                    dbias_ref[b, hh] += ds
                    dsb = (ds * (64 ** -0.5)).astype(BF16)
                    dqs.append(_dot(dsb, kk, NN))
                    dkh = _dot(dsb, qm, TN)
                    dvh = _dot(pr.astype(BF16), dom, TN)
                    dk2 = dkh if dk2 is None else dk2 + dkh
                    dv2 = dvh if dv2 is None else dv2 + dvh
                dqa[here, :] += jnp.where(lo, dqs[0], dqs[1])
                dka[prev, :] += dk2[:BLK]
                dka[cur, :] += dk2[BLK:]
                dva[prev, :] += dv2[:BLK]
                dva[cur, :] += dv2[BLK:]
                return carry

            lax.fori_loop(0, tt // BLK, step, 0)
        dq_ref[...] = dqa[...].astype(BF16)
        dk_ref[...] = dka[tt:2 * tt, :].astype(BF16)
        dv_ref[...] = dva[tt:2 * tt, :].astype(BF16)
        kcar[...] = dka[0:tt, :]
        vcar[...] = dva[0:tt, :]

    rev = lambda i: nt - 1 - i
    out = pl.BlockSpec((tt, LANES), lambda p, i: (rev(i), p))
    two = lambda: pltpu.VMEM((2 * tt, LANES), F32)
    one = lambda: pltpu.VMEM((tt, LANES), F32)
    return pl.pallas_call(
        body, name=name, grid=(DIL_COLS, nt),
        in_specs=_dil_in_specs(rev) + [out, out, out],
        out_specs=[out, out, out, pl.BlockSpec((nbr, 2, BLK, 2 * BLK), lambda p, i: (0, p, 0, 0))],
        out_shape=[jax.ShapeDtypeStruct((s, ATT_W), BF16)] * 3 + [jax.ShapeDtypeStruct((nbr, ATT_HEADS, BLK, 2 * BLK), F32)],
        scratch_shapes=[two(), two(), one(), two(), two(), one(), one()],
        compiler_params=_params("arbitrary", "arbitrary"),
    )(proj, proj, proj, proj, proj, biasm, lse, att, dcat)


QK_COL0 = (3 * ATT_W) // ATT_W


def _conv_shifted(prev, cur, j, row):
    sh = CONV_K - 1 - j
    if sh == 0:
        return cur
    return jnp.where(row < sh, pltpu.roll(prev, sh, 0), pltpu.roll(cur, sh, 0))


def _conv_z(prev, cur, w_ref, b_ref, row):
    z = b_ref[...] + cur * w_ref[CONV_K - 1:CONV_K, :]
    for j in range(CONV_K - 1):
        z = z + _conv_shifted(prev, cur, j, row) * w_ref[j:j + 1, :]
    return z


def _conv_fwd(proj, conv_w, conv_b, name="conv_fwd", tm=512):
    s = proj.shape[0]
    w = ATT_W

    def body(prev_ref, cur_ref, w_ref, b_ref, o_ref):
        i = pl.program_id(1)
        row = lax.broadcasted_iota(jnp.int32, (tm, w), 0)
        prev = jnp.where(i > 0, prev_ref[...], 0.0)
        z = _conv_z(prev, cur_ref[...], w_ref, b_ref, row)
        o_ref[...] = z * _sigmoid(z)

    return pl.pallas_call(
        body, name=name, grid=(2, s // tm),
        in_specs=[pl.BlockSpec((tm, w), lambda j, i: (jnp.maximum(i - 1, 0), QK_COL0 + j)),
                  pl.BlockSpec((tm, w), lambda j, i: (i, QK_COL0 + j)),
                  pl.BlockSpec((CONV_K, w), lambda j, i: (0, j)),
                  pl.BlockSpec((1, w), lambda j, i: (0, j))],
        out_specs=pl.BlockSpec((tm, w), lambda j, i: (i, j)),
        out_shape=jax.ShapeDtypeStruct((s, 2 * ML_W), F32),
        compiler_params=_params("parallel", "parallel"),
    )(proj, proj, conv_w, conv_b)


def _conv_bwd(proj, dqk, conv_w, conv_b, name="conv_bwd", tm=512):
    s = proj.shape[0]
    w = ATT_W
    nt = s // tm

    def body(xp_ref, xc_ref, xn_ref, dc_ref, dn_ref, w_ref, b_ref, dx_ref, dw_ref, db_ref):
        i = pl.program_id(1)
        row = lax.broadcasted_iota(jnp.int32, (tm, w), 0)
        prev = jnp.where(i > 0, xp_ref[...], 0.0)
        cur = xc_ref[...]

        def dz_of(pv, cv, dy):
            z = _conv_z(pv, cv, w_ref, b_ref, row)
            sig = _sigmoid(z)
            return dy * (sig * (1.0 + z * (1.0 - sig)))

        dzc = dz_of(prev, cur, dc_ref[...])
        dzn = jnp.where(i < nt - 1, dz_of(cur, xn_ref[...], dn_ref[...]), 0.0)
        dx = dzc * w_ref[CONV_K - 1:CONV_K, :]
        for j in range(CONV_K - 1):
            sh = CONV_K - 1 - j
            up = jnp.where(row >= tm - sh, pltpu.roll(dzn, tm - sh, 0), pltpu.roll(dzc, tm - sh, 0))
            dx = dx + up * w_ref[j:j + 1, :]
        dx_ref[...] = dx

        @pl.when(i == 0)
        def _():
            dw_ref[...] = jnp.zeros_like(dw_ref)
            db_ref[...] = jnp.zeros_like(db_ref)

        for j in range(CONV_K):
            dw_ref[j:j + 1, :] += jnp.sum(dzc * _conv_shifted(prev, cur, j, row), axis=0, keepdims=True)
        db_ref[...] += jnp.sum(dzc, axis=0, keepdims=True)

    xs = lambda f: pl.BlockSpec((tm, w), lambda j, i: (f(i), QK_COL0 + j))
    ds = lambda f: pl.BlockSpec((tm, w), lambda j, i: (f(i), j))
    return pl.pallas_call(
        body, name=name, grid=(2, nt),
        in_specs=[xs(lambda i: jnp.maximum(i - 1, 0)), xs(lambda i: i), xs(lambda i: jnp.minimum(i + 1, nt - 1)),
                  ds(lambda i: i), ds(lambda i: jnp.minimum(i + 1, nt - 1)),
                  pl.BlockSpec((CONV_K, w), lambda j, i: (0, j)), pl.BlockSpec((1, w), lambda j, i: (0, j))],
        out_specs=[ds(lambda i: i), pl.BlockSpec((CONV_K, w), lambda j, i: (0, j)),
                   pl.BlockSpec((1, w), lambda j, i: (0, j))],
        out_shape=[jax.ShapeDtypeStruct((s, 2 * ML_W), F32), jax.ShapeDtypeStruct((CONV_K, 2 * ML_W), F32),
                   jax.ShapeDtypeStruct((1, 2 * ML_W), F32)],
        compiler_params=_params("parallel", "arbitrary"),
    )(proj, proj, proj, dqk, dqk, conv_w, conv_b)


def _bf16_mm(dims_fwd):
    @jax.custom_vjp
    def mm(a, b):
        return _dot(a.astype(BF16), b.astype(BF16), dims_fwd)

    def fwd(a, b):
        return mm(a, b), (a, b)

    def bwd(res, g):
        a, b = res
        if dims_fwd is NN:
            return _mm_nt(g, b), _mm_tn(a, g)
        if dims_fwd is NT:
            return _mm_nn(g, b), _mm_tn(g, a)
        return _mm_nt(b, g), _mm_nn(a, g)

    mm.defvjp(fwd, bwd)
    return mm


_mm_nn = _bf16_mm(NN)
_mm_nt = _bf16_mm(NT)
_mm_tn = _bf16_mm(TN)


def _tri(lower):
    r = lax.broadcasted_iota(jnp.int32, (CHUNK, CHUNK), 0)
    c = lax.broadcasted_iota(jnp.int32, (CHUNK, CHUNK), 1)
    return ((r >= c) if lower else (r <= c)).astype(F32)


@jax.custom_vjp
def _cumsum_rows(x):
    return lax.dot_general(_tri(True), x, NN, precision=lax.Precision.HIGHEST, preferred_element_type=F32)


def _cumsum_fwd(x):
    return _cumsum_rows(x), None


def _cumsum_bwd(_, g):
    return (lax.dot_general(_tri(False), g, NN, precision=lax.Precision.HIGHEST, preferred_element_type=F32),)


_cumsum_rows.defvjp(_cumsum_fwd, _cumsum_bwd)


def _abs(x):
    return jnp.where(x >= 0, x, -x)


def _log_sigmoid(x):
    return jnp.minimum(x, 0.0) - jnp.log(1.0 + jnp.exp(-_abs(x)))


def _pick_col(x, lane):
    sel = lax.broadcasted_iota(jnp.int32, x.shape, 1) == lane
    return jnp.sum(jnp.where(sel, x, 0.0), axis=1, keepdims=True)


def _pick_row(x, r):
    sel = lax.broadcasted_iota(jnp.int32, x.shape, 0) == r
    return jnp.sum(jnp.where(sel, x, 0.0), axis=0, keepdims=True)


def _mlstm_chunk(qs, ks, vs, oms, gates, gate_bias, mlg, cs, ns, ms):
    gb = gates + gate_bias
    cum = _cumsum_rows(_log_sigmoid(gb))
    gbt = gb.T
    cumt = cum.T
    causal = lax.broadcasted_iota(jnp.int32, (CHUNK, CHUNK), 0) >= lax.broadcasted_iota(jnp.int32, (CHUNK, CHUNK), 1)
    ys, c_out, n_out, m_out = [], [], [], []
    for h in range(ML_HEADS):
        q, v, om, c, n, m = qs[h], vs[h], oms[h], cs[h], ns[h], ms[h]
        k = ks[h] * (ML_HD ** -0.5)
        ig_col = _pick_col(gb, h)
        ig_row = _pick_row(gbt, h)
        b_col = _pick_col(cum, ML_HEADS + h)
        b_row = _pick_row(cumt, ML_HEADS + h)
        g = _pick_row(b_col, CHUNK - 1)
        a = g - b_col + ig_col
        m_loc = jnp.max(a, axis=0, keepdims=True)
        wa = jnp.exp(a - m_loc)
        c_loc = _mm_tn(wa * v, k)
        n_loc = jnp.sum(wa * k, axis=0, keepdims=True)
        m_new = jnp.maximum(g + m, m_loc)
        sp = jnp.exp(g + m - m_new)
        sl = jnp.exp(m_loc - m_new)
        c_out.append(sp * c + sl * c_loc)
        n_out.append(sp * n + sl * n_loc)
        m_out.append(m_new)
        d_log = jnp.where(causal, b_col - b_row + ig_row, -jnp.inf)
        e_log = b_col + m
        m_t = jnp.maximum(e_log, jnp.max(d_log, axis=1, keepdims=True))
        d_w = jnp.exp(d_log - m_t)
        e_w = jnp.exp(e_log - m_t)
        s_qk = _mm_nt(q, k) * d_w
        num = e_w * _mm_nt(q, c) + _mm_nn(s_qk, v)
        den = e_w * jnp.sum(q * n, axis=1, keepdims=True) + jnp.sum(s_qk, axis=1, keepdims=True)
        hh = num / jnp.maximum(_abs(den), jnp.exp(-m_t))
        hg = _sigmoid(om) * hh
        mu = jnp.mean(hg, axis=1, keepdims=True)
        hc = hg - mu
        var = jnp.mean(hc * hc, axis=1, keepdims=True)
        ys.append(hc * lax.rsqrt(var + LN_EPS) * mlg[h])
    return ys, c_out, n_out, m_out


V_COL = 5
O_COL = 6


def _mlstm_fwd(qk, proj, gates, gate_bias, mlg, name="mlstm_fwd", gather=()):
    s = qk.shape[0]
    nc = s // CHUNK

    def body(q_ref, k_ref, v_ref, o_ref, g_ref, gb_ref, mlg_ref, y_ref, cp_ref, np_ref, mp_ref, c_s, n_s, m_s):
        ci = pl.program_id(0)

        @pl.when(ci == 0)
        def _():
            c_s[...] = jnp.zeros_like(c_s)
            n_s[...] = jnp.zeros_like(n_s)
            m_s[...] = jnp.zeros_like(m_s)

        cp_ref[...] = c_s[...]
        np_ref[...] = n_s[...]
        mp_ref[...] = m_s[...]
        hs = lambda ref: [ref[:, LANES * h:LANES * (h + 1)] for h in range(ML_HEADS)]
        ys, c_new, n_new, m_new = _mlstm_chunk(
            hs(q_ref), hs(k_ref), hs(v_ref), hs(o_ref), g_ref[...], gb_ref[...], hs(mlg_ref),
            [c_s[h] for h in range(ML_HEADS)], [n_s[h:h + 1, :] for h in range(ML_HEADS)],
            [m_s[h:h + 1, 0:1] for h in range(ML_HEADS)])
        for h in range(ML_HEADS):
            y_ref[:, LANES * h:LANES * (h + 1)] = ys[h]
            c_s[h] = c_new[h]
            n_s[h:h + 1, :] = n_new[h]
            m_s[h:h + 1, :] = jnp.broadcast_to(m_new[h], (1, LANES))

    blk = lambda col: pl.BlockSpec((CHUNK, ML_W), lambda ci: (ci, col))
    vec = lambda w: pl.BlockSpec((1, w), lambda ci: (0, 0))
    return _call(
        body, name=name, grid=(nc,), args=(qk, qk, proj, proj, gates, gate_bias, mlg), sem=("arbitrary",), gather=gather,
        in_specs=[blk(0), blk(1), blk(V_COL), blk(O_COL), pl.BlockSpec((CHUNK, LANES), lambda ci: (ci, 0)),
                  vec(LANES), vec(ML_W)],
        out_specs=[blk(0), pl.BlockSpec((None, ML_HEADS, ML_HD, ML_HD), lambda ci: (ci, 0, 0, 0)),
                   pl.BlockSpec((None, 8, LANES), lambda ci: (ci, 0, 0)),
                   pl.BlockSpec((None, 8, LANES), lambda ci: (ci, 0, 0))],
        out_shape=[jax.ShapeDtypeStruct((s, ML_W), F32), jax.ShapeDtypeStruct((nc, ML_HEADS, ML_HD, ML_HD), F32),
                   jax.ShapeDtypeStruct((nc, 8, LANES), F32), jax.ShapeDtypeStruct((nc, 8, LANES), F32)],
        scratch_shapes=[pltpu.VMEM((ML_HEADS, ML_HD, ML_HD), F32), pltpu.VMEM((8, LANES), F32),
                        pltpu.VMEM((8, LANES), F32)])


def _mlstm_bwd(qk, proj, gates, gate_bias, mlg, cprev, nprev, mprev, dy, name="mlstm_bwd", exchange=()):
    s = qk.shape[0]
    nc = s // CHUNK

    def body(q_ref, k_ref, v_ref, o_ref, g_ref, gb_ref, mlg_ref, cp_ref, np_ref, mp_ref, dy_ref,
             dqk_ref, dv_ref, do_ref, dg_ref, dgb_ref, dmlg_ref, dc_s, dn_s, dm_s, gb8, mg8):
        ci = pl.program_id(0)

        @pl.when(ci == 0)
        def _():
            dc_s[...] = jnp.zeros_like(dc_s)
            dn_s[...] = jnp.zeros_like(dn_s)
            dm_s[...] = jnp.zeros_like(dm_s)
            gb8[...] = jnp.zeros_like(gb8)
            mg8[...] = jnp.zeros_like(mg8)

        hs = lambda ref: [ref[:, LANES * h:LANES * (h + 1)] for h in range(ML_HEADS)]
        prim = (hs(q_ref), hs(k_ref), hs(v_ref), hs(o_ref), g_ref[...], gb_ref[...], hs(mlg_ref),
                [cp_ref[h] for h in range(ML_HEADS)], [np_ref[h:h + 1, :] for h in range(ML_HEADS)],
                [mp_ref[h:h + 1, 0:1] for h in range(ML_HEADS)])
        _, vjp = jax.vjp(_mlstm_chunk, *prim)
        cot = (hs(dy_ref), [dc_s[h] for h in range(ML_HEADS)], [dn_s[h:h + 1, :] for h in range(ML_HEADS)],
               [dm_s[h:h + 1, 0:1] for h in range(ML_HEADS)])
        dqs, dks, dvs, dos, dg, dgb, dmlg, dcs, dns, dms = vjp(cot)
        dg_ref[...] = dg
        gb8[0:1, :] += dgb
        for h in range(ML_HEADS):
            sl = slice(LANES * h, LANES * (h + 1))
            dqk_ref[:, sl] = dqs[h]
            dqk_ref[:, ML_W + LANES * h:ML_W + LANES * (h + 1)] = dks[h]
            dv_ref[:, sl] = dvs[h]
            do_ref[:, sl] = dos[h]
            mg8[0:1, sl] += dmlg[h]
            dc_s[h] = dcs[h]
            dn_s[h:h + 1, :] = dns[h]
            dm_s[h:h + 1, :] = jnp.broadcast_to(dms[h], (1, LANES))

        @pl.when(ci == nc - 1)
        def _():
            dgb_ref[...] = gb8[0:1, :]
            dmlg_ref[...] = mg8[0:1, :]

    rev = lambda ci: nc - 1 - ci
    blk = lambda col: pl.BlockSpec((CHUNK, ML_W), lambda ci: (rev(ci), col))
    vec = lambda w: pl.BlockSpec((1, w), lambda ci: (0, 0))
    st8 = pl.BlockSpec((None, 8, LANES), lambda ci: (rev(ci), 0, 0))
    gsp = pl.BlockSpec((CHUNK, LANES), lambda ci: (rev(ci), 0))
    return _call(
        body, name=name, grid=(nc,), sem=("arbitrary",), exchange=exchange,
        args=(qk, qk, proj, proj, gates, gate_bias, mlg, cprev, nprev, mprev, dy),
        in_specs=[blk(0), blk(1), blk(V_COL), blk(O_COL), gsp, vec(LANES), vec(ML_W),
                  pl.BlockSpec((None, ML_HEADS, ML_HD, ML_HD), lambda ci: (rev(ci), 0, 0, 0)), st8, st8, blk(1)],
        out_specs=[pl.BlockSpec((CHUNK, 2 * ML_W), lambda ci: (rev(ci), 0)), blk(0), blk(0), gsp, vec(LANES), vec(ML_W)],
        out_shape=[jax.ShapeDtypeStruct((s, 2 * ML_W), F32),
                   jax.ShapeDtypeStruct((s, ML_W), F32), jax.ShapeDtypeStruct((s, ML_W), F32),
                   jax.ShapeDtypeStruct((s, LANES), F32), jax.ShapeDtypeStruct((1, LANES), F32),
                   jax.ShapeDtypeStruct((1, ML_W), F32)],
        scratch_shapes=[pltpu.VMEM((ML_HEADS, ML_HD, ML_HD), F32), pltpu.VMEM((8, LANES), F32),
                        pltpu.VMEM((8, LANES), F32), pltpu.VMEM((8, LANES), F32), pltpu.VMEM((8, ML_W), F32)])


def _xattn_tile(qs, ks, vs):
    outs = []
    for q, k, v in zip(qs, ks, vs):
        sc = _mm_nt(q, k) * (XA_HD ** -0.5)
        mx = lax.stop_gradient(jnp.max(sc, axis=1, keepdims=True))
        pe = jnp.exp(sc - mx)
        outs.append(_mm_nn(pe / jnp.sum(pe, axis=1, keepdims=True), v))
    return outs


def _xa_heads(ref):
    return [ref[:, XA_HD * h:XA_HD * (h + 1)] for h in range(XA_HEADS)]


def _xattn_fwd(q, kv, name="xattn_fwd", tm=512):
    s, d = q.shape

    def body(q_ref, k_ref, v_ref, o_ref):
        outs = _xattn_tile(_xa_heads(q_ref), _xa_heads(k_ref), _xa_heads(v_ref))
        for h in range(XA_HEADS):
            o_ref[:, XA_HD * h:XA_HD * (h + 1)] = outs[h]

    row = pl.BlockSpec((tm, d), lambda i: (i, 0))
    return pl.pallas_call(
        body, name=name, grid=(s // tm,),
        in_specs=[row, pl.BlockSpec((MEM_LEN, d), lambda i: (0, 0)), pl.BlockSpec((MEM_LEN, d), lambda i: (0, 1))],
        out_specs=row, out_shape=jax.ShapeDtypeStruct((s, d), F32),
        compiler_params=_params("parallel"),
    )(q, kv, kv)


def _xattn_bwd(q, kv, do, name="xattn_bwd", tm=512):
    s, d = q.shape

    def body(q_ref, k_ref, v_ref, do_ref, dq_ref, dkv_ref):
        i = pl.program_id(0)
        _, vjp = jax.vjp(_xattn_tile, _xa_heads(q_ref), _xa_heads(k_ref), _xa_heads(v_ref))
        dqs, dks, dvs = vjp(_xa_heads(do_ref))

        @pl.when(i == 0)
        def _():
            dkv_ref[...] = jnp.zeros_like(dkv_ref)

        for h in range(XA_HEADS):
            sl = slice(XA_HD * h, XA_HD * (h + 1))
            dq_ref[:, sl] = dqs[h]
            dkv_ref[:, sl] += dks[h]
            dkv_ref[:, d + XA_HD * h:d + XA_HD * (h + 1)] += dvs[h]

    row = pl.BlockSpec((tm, d), lambda i: (i, 0))
    return pl.pallas_call(
        body, name=name, grid=(s // tm,),
        in_specs=[row, pl.BlockSpec((MEM_LEN, d), lambda i: (0, 0)), pl.BlockSpec((MEM_LEN, d), lambda i: (0, 1)), row],
        out_specs=[row, pl.BlockSpec((MEM_LEN, 2 * d), lambda i: (0, 0))],
        out_shape=[jax.ShapeDtypeStruct((s, d), F32), jax.ShapeDtypeStruct((MEM_LEN, 2 * d), F32)],
        compiler_params=_params("arbitrary"),
    )(q, kv, kv, do)


def _loss_head(y, target, name="loss_head", tm=512):
    s, d = y.shape
    nt = s // tm

    def body(y_ref, t_ref, dy_ref, loss_ref, acc):
        i = pl.program_id(0)
        err = y_ref[...] - t_ref[...]
        dy_ref[...] = err * (1.0 / d)

        @pl.when(i == 0)
        def _():
            acc[...] = jnp.zeros_like(acc)

        acc[...] += _rowsum8(err * err)

        @pl.when(i == nt - 1)
        def _():
            tot = jnp.sum(jnp.sum(acc[...], axis=0, keepdims=True), axis=1, keepdims=True)
            loss_ref[...] = jnp.broadcast_to(tot * (0.5 / d), (1, LANES))

    row = pl.BlockSpec((tm, d), lambda i: (i, 0))
    return pl.pallas_call(
        body, name=name, grid=(nt,),
        in_specs=[row, row], out_specs=[row, pl.BlockSpec((1, LANES), lambda i: (0, 0))],
        out_shape=[jax.ShapeDtypeStruct((s, d), F32), jax.ShapeDtypeStruct((1, LANES), F32)],
        scratch_shapes=[pltpu.VMEM((8, d), F32)],
        compiler_params=_params("arbitrary"),
    )(y, target)


def _adam2d(recv, w, m, v, name, layer=None):
    rows, cols = w.shape[-2:]
    fits = [t for t in range(16, rows + 1, 16) if rows % t == 0 and t * cols <= 128 * 1024]
    tr = max(fits) if fits else rows

    def body(r_ref, w_ref, m_ref, v_ref, g_ref, d_ref, mo_ref, vo_ref):
        g = r_ref[0].astype(F32)
        for j in range(1, N_DEV):
            g = g + r_ref[j].astype(F32)
        mn = ADAM_B1 * m_ref[...] + (1.0 - ADAM_B1) * g
        vn = ADAM_B2 * v_ref[...] + (1.0 - ADAM_B2) * jnp.square(g)
        m_hat = mn / (1.0 - ADAM_B1 ** ADAM_STEP)
        v_hat = vn / (1.0 - ADAM_B2 ** ADAM_STEP)
        g_ref[...] = g
        d_ref[...] = -ADAM_LR * (m_hat / (jnp.sqrt(v_hat) + ADAM_EPS) + ADAM_WD * w_ref[...])
        mo_ref[...] = mn
        vo_ref[...] = vn

    row = pl.BlockSpec((tr, cols), lambda i: (i, 0))
    if layer is None:
        wspec = row
    else:
        wspec = pl.BlockSpec((None, None, tr, cols), lambda i: (0, layer, i, 0))
    return pl.pallas_call(
        body, name=name, grid=(rows // tr,),
        in_specs=[pl.BlockSpec((N_DEV, tr, cols), lambda i: (0, i, 0)), wspec, wspec, wspec],
        out_specs=[row] * 4, out_shape=[jax.ShapeDtypeStruct((rows, cols), F32)] * 4,
        compiler_params=_params("parallel"),
    )(recv, w, m, v)


WEIGHTS = ("rel_bias", "ln_g", "ln_b", "ffn_w_gate", "ffn_w_up", "ffn_w_down", "w_in", "conv_w", "conv_b",
           "ig_bias", "fg_bias", "ml_norm_g", "w_out", "xq_w", "xkv_w", "xo_w")
SMALL = ("rel_bias", "ln_g", "ln_b", "conv_w", "conv_b", "ig_bias", "fg_bias", "ml_norm_g")
SMALL_SHAPES = {
    "rel_bias": (REL_BUCKETS, ATT_HEADS), "ln_g": (1, 4, LANES), "ln_b": (1, 4, LANES), "conv_w": (1, CONV_K, LANES),
    "conv_b": (1, 2 * ML_W), "ig_bias": (1, ML_HEADS), "fg_bias": (1, ML_HEADS), "ml_norm_g": (1, ML_W),
}
SMALL_ROWS = 8


def _pack_small(parts, lead=()):
    out = []
    for p in parts:
        p = jnp.pad(p, [(0, 0)] * len(lead) + [(0, SMALL_ROWS * LANES - p.shape[-1])])
        out.append(p.reshape(lead + (SMALL_ROWS, LANES)))
    return jnp.concatenate(out, axis=len(lead))


def _unpack_small(flat):
    out = {}
    for i, n in enumerate(SMALL):
        cnt = int(np.prod(SMALL_SHAPES[n]))
        out[n] = flat[SMALL_ROWS * i:SMALL_ROWS * (i + 1)].reshape(-1)[:cnt].reshape(SMALL_SHAPES[n])
    return out


def _split8(full, axis):
    shp = full.shape
    t = full.reshape(shp[:axis] + (N_DEV, shp[axis] // N_DEV) + shp[axis + 1:])
    return jnp.moveaxis(t, axis, 0).reshape(N_DEV, -1)


def _rep8(full):
    return jnp.broadcast_to(full.reshape(1, -1), (N_DEV, full.size))


def kernel(x, mem, rel_bias, ln_g, ln_b, ffn_w_gate, ffn_w_up, ffn_w_down, w_in, conv_w, conv_b, ig_bias, fg_bias, ml_norm_g, w_out, xq_w, xkv_w, xo_w, loss_target, m_rel_bias, m_ln_g, m_ln_b, m_ffn_w_gate, m_ffn_w_up, m_ffn_w_down, m_w_in, m_conv_w, m_conv_b, m_ig_bias, m_fg_bias, m_ml_norm_g, m_w_out, m_xq_w, m_xkv_w, m_xo_w, v_rel_bias, v_ln_g, v_ln_b, v_ffn_w_gate, v_ffn_w_up, v_ffn_w_down, v_w_in, v_conv_w, v_conv_b, v_ig_bias, v_fg_bias, v_ml_norm_g, v_w_out, v_xq_w, v_xkv_w, v_xo_w):
    w_tree = dict(rel_bias=rel_bias, ln_g=ln_g, ln_b=ln_b, ffn_w_gate=ffn_w_gate, ffn_w_up=ffn_w_up,
                  ffn_w_down=ffn_w_down, w_in=w_in, conv_w=conv_w, conv_b=conv_b, ig_bias=ig_bias, fg_bias=fg_bias,
                  ml_norm_g=ml_norm_g, w_out=w_out, xq_w=xq_w, xkv_w=xkv_w, xo_w=xo_w)
    m_tree = dict(rel_bias=m_rel_bias, ln_g=m_ln_g, ln_b=m_ln_b, ffn_w_gate=m_ffn_w_gate, ffn_w_up=m_ffn_w_up,
                  ffn_w_down=m_ffn_w_down, w_in=m_w_in, conv_w=m_conv_w, conv_b=m_conv_b, ig_bias=m_ig_bias,
                  fg_bias=m_fg_bias, ml_norm_g=m_ml_norm_g, w_out=m_w_out, xq_w=m_xq_w, xkv_w=m_xkv_w, xo_w=m_xo_w)
    v_tree = dict(rel_bias=v_rel_bias, ln_g=v_ln_g, ln_b=v_ln_b, ffn_w_gate=v_ffn_w_gate, ffn_w_up=v_ffn_w_up,
                  ffn_w_down=v_ffn_w_down, w_in=v_w_in, conv_w=v_conv_w, conv_b=v_conv_b, ig_bias=v_ig_bias,
                  fg_bias=v_fg_bias, ml_norm_g=v_ml_norm_g, w_out=v_w_out, xq_w=v_xq_w, xkv_w=v_xkv_w, xo_w=v_xo_w)
    x0 = x[0]
    pad_ff = FF_PAD - FF_SHARD
    bf = lambda t: t.astype(BF16)

    pad_rows = lambda t: jnp.pad(t, ((0, pad_ff), (0, 0)))
    ffn_shards = [(pad_rows(bf(ffn_w_gate[0, l]).T), pad_rows(bf(ffn_w_up[0, l]).T), pad_rows(bf(ffn_w_down[0, l])))
                  for l in range(2)]
    pairs = lambda t: t.reshape(N_PAIR, FF_PAIR, D_MODEL)
    w_in_shard = jnp.pad(bf(w_in[0]), ((0, 0), (0, ATT_W - W_IN_SHARD)))
    small_shard = jnp.concatenate([ln_g[0], ln_b[0], conv_w[0], jnp.zeros((4, LANES), F32)], axis=0)
    gate_bias = jnp.pad(jnp.concatenate([ig_bias, fg_bias], axis=1), ((0, 0), (0, LANES - 2 * ML_HEADS)))
    buckets = _bucket_tables()

    wg0, wu0, wd0, small_all = _exchange_only("ffn1_weights_gather", gather=ffn_shards[0] + (small_shard,))
    wg0, wu0, wd0 = pairs(wg0), pairs(wu0), pairs(wd0)
    unshard = lambda t: jnp.moveaxis(t, 0, 1).reshape(4, D_MODEL)
    ln_g_full, ln_b_full, conv_w_full = unshard(small_all[:, 0:4]), unshard(small_all[:, 4:8]), unshard(small_all[:, 8:12])
    lng = lambda i: ln_g_full[i:i + 1]
    lnb = lambda i: ln_b_full[i:i + 1]

    u0, x1, win_all, wout_all, xq_all, xo_all, xkv_all = _ffn_fwd(
        x0, wg0, wu0, wd0, lng(0), lnb(0), "ffn1_fwd",
        gather=(w_in_shard, bf(w_out[0]), bf(xq_w[0]), bf(xo_w[0]), bf(xkv_w[0])))
    w_in_full = jnp.moveaxis(win_all[:, :, :W_IN_SHARD], 0, 1).reshape(D_MODEL, W_IN)
    w_main = w_in_full[:, :W_IN_MAIN]
    w_gate_cols = jnp.pad(w_in_full[:, W_IN_MAIN:], ((0, 0), (0, LANES - 2 * ML_HEADS)))
    w_out_full = wout_all.reshape(D_MODEL, D_MODEL)
    xq_full = xq_all.reshape(D_MODEL, D_MODEL)
    xo_full = xo_all.reshape(D_MODEL, D_MODEL)

    proj, wg1 = _matmul(x1, w_main, "nn", "proj_fwd", tk=D_MODEL, gather=(ffn_shards[1][0],))
    gates, = _matmul(x1, w_gate_cols, "nn", "gates_fwd", tk=D_MODEL)
    biasm = _bias_fwd(rel_bias, buckets)
    att, lse, wd1 = _dil_fwd(proj, biasm, gather=(ffn_shards[1][2],))
    qk = _conv_fwd(proj, conv_w_full, conv_b)
    y_m, c_prev, n_prev, m_prev, wu1 = _mlstm_fwd(qk, proj, gates, gate_bias, ml_norm_g, gather=(ffn_shards[1][1],))
    cat = jnp.concatenate([att, y_m], axis=1)
    f1, = _matmul(cat, w_out_full, "nn", "w_out_fwd", tn=D_MODEL, tk=D_MODEL)
    u1, x2 = _resid_ln(x1, f1, lng(1), lnb(1), "mixer_ln")
    q_x, = _matmul(x2, xq_full, "nn", "xq_fwd", tn=D_MODEL, tk=D_MODEL)
    kv, = _matmul(mem[0], xkv_all, "nn", "xkv_fwd", tk=D_MODEL)
    o_x = _xattn_fwd(q_x, kv)
    f2, = _matmul(o_x, xo_full, "nn", "xo_fwd", tn=D_MODEL, tk=D_MODEL)
    u2, x3 = _resid_ln(x2, f2, lng(2), lnb(2), "xattn_ln")
    wg1, wu1, wd1 = pairs(wg1), pairs(wu1), pairs(wd1)
    u3, x4 = _ffn_fwd(x3, wg1, wu1, wd1, lng(3), lnb(3), "ffn2_fwd")
    dx4, loss_row = _loss_head(x4, loss_target[0])

    dx3, xb, df, da, db, hh, dg3, db3 = _ffn_bwd_x(dx4, u3, x3, wg1, wu1, wd1, lng(3), "ffn2_bwd_x")
    ffn2_send = _ffn_bwd_w(xb, df, da, db, hh, "ffn2_bwd_w")

    du2, dg2, db2 = _ln_bwd(dx3, u2, lng(2), "xattn_ln_bwd")
    do_x, = _matmul(du2, xo_full, "nt", "xo_bwd_x", tn=D_MODEL)
    g_xo, = _matmul(o_x, du2, "tn", "xo_bwd_w", tm=D_MODEL, out_dtype=BF16)
    dq_x, dkv = _xattn_bwd(q_x, kv, do_x)
    g_xq, = _matmul(x2, dq_x, "tn", "xq_bwd_w", tm=D_MODEL, out_dtype=BF16)
    g_xkv, = _matmul(mem[0], dkv, "tn", "xkv_bwd_w", tm=D_MODEL, tn=2 * D_MODEL // N_DEV, tk=MEM_LEN,
                     out_dtype=BF16, blocked_out=True)
    dx2, = _matmul(dq_x, xq_full, "nt", "xq_bwd_x", tn=D_MODEL, add=du2, add_scale=ALPHA)

    du1, dg1, db1 = _ln_bwd(dx2, u1, lng(1), "mixer_ln_bwd")
    dcat, = _matmul(du1, w_out_full, "nt", "w_out_bwd_x", tn=D_MODEL)
    g_w_out, = _matmul(cat, du1, "tn", "w_out_bwd_w", tm=D_MODEL, out_dtype=BF16)
    dqk, dv_m, do_m, dgates, dgate_bias, g_mlg, *ffn2_recv = _mlstm_bwd(
        qk, proj, gates, gate_bias, ml_norm_g, c_prev, n_prev, m_prev, dcat, exchange=tuple(ffn2_send))
    dqk_pre, g_conv_w, g_conv_b = _conv_bwd(proj, dqk, conv_w_full, conv_b)
    dq_a, dk_a, dv_a, dbias = _dil_bwd(proj, biasm, lse, att, dcat)
    g_rel = _bias_bwd(dbias, buckets)[:, :ATT_HEADS]
    dproj = jnp.concatenate([dq_a, dk_a, dv_a, bf(dqk_pre), bf(dv_m), bf(do_m)], axis=1)
    g_w_main, = _matmul(x1, dproj, "tn", "proj_bwd_w", tm=D_MODEL, tn=W_IN_MAIN // 2, out_dtype=BF16)
    g_w_gates, = _matmul(x1, dgates, "tn", "gates_bwd_w", tm=D_MODEL, out_dtype=BF16)
    g_w_in = jnp.concatenate([g_w_main, g_w_gates[:, :2 * ML_HEADS]], axis=1)
    dx1, = _matmul(dproj, w_main, "nt", "proj_bwd_x", tn=D_MODEL, add=du1, add_scale=ALPHA)
    dx1, = _matmul(dgates, w_gate_cols, "nt", "gates_bwd_x", tn=D_MODEL, add=dx1)

    rows8 = lambda t: t.reshape(N_DEV, D_MODEL // N_DEV, D_MODEL)
    mid_send = (rows8(g_xo), rows8(g_xq), g_xkv, rows8(g_w_out),
                jnp.moveaxis(g_w_in.reshape(D_MODEL, N_DEV, W_IN_SHARD), 1, 0))
    dx0, xb, df, da, db, hh, dg0, db0, r_xo, r_xq, r_xkv, r_w_out, r_w_in = _ffn_bwd_x(
        dx1, u0, x0, wg0, wu0, wd0, lng(0), "ffn1_bwd_x", exchange=mid_send)
    ffn1_send = _ffn_bwd_w(xb, df, da, db, hh, "ffn1_bwd_w")
    small_blocks = {
        "rel_bias": _rep8(g_rel),
        "ln_g": _split8(jnp.concatenate([dg0, dg1, dg2, dg3], axis=0), 1),
        "ln_b": _split8(jnp.concatenate([db0, db1, db2, db3], axis=0), 1),
        "conv_w": _split8(g_conv_w, 1),
        "conv_b": _rep8(g_conv_b),
        "ig_bias": _rep8(dgate_bias[:, :ML_HEADS]),
        "fg_bias": _rep8(dgate_bias[:, ML_HEADS:2 * ML_HEADS]),
        "ml_norm_g": _rep8(g_mlg),
    }
    small_send = _pack_small([small_blocks[n] for n in SMALL], lead=(N_DEV,))
    *ffn1_recv, r_small = _exchange_only("ffn1_grads_exchange", exchange=tuple(ffn1_send) + (small_send,))

    res = {}
    for i, n in enumerate(("ffn_w_gate", "ffn_w_up", "ffn_w_down")):
        per_layer = [_adam2d(r[i], w_tree[n], m_tree[n], v_tree[n], f"adamw_{n}_{l}", layer=l)
                     for l, r in enumerate((ffn1_recv, ffn2_recv))]
        res[n] = [jnp.stack([per_layer[0][j], per_layer[1][j]])[None] for j in range(4)]
    for n, r in (("w_in", r_w_in), ("w_out", r_w_out), ("xq_w", r_xq), ("xkv_w", r_xkv), ("xo_w", r_xo)):
        res[n] = [t[None] for t in _adam2d(r, w_tree[n][0], m_tree[n][0], v_tree[n][0], f"adamw_{n}")]
    pack = lambda tree: _pack_small([tree[n].reshape(-1) for n in SMALL])
    small = [_unpack_small(t) for t in _adam2d(r_small, pack(w_tree), pack(m_tree), pack(v_tree), "adamw_small")]
    for n in SMALL:
        res[n] = [small[j][n] for j in range(4)]

    loss = lax.psum(loss_row[0, 0], ("x", "y", "c"))
    return (loss, dx0[None], *[res[n][0] for n in WEIGHTS], *[res[n][1] for n in WEIGHTS],
            *[res[n][2] for n in WEIGHTS], *[res[n][3] for n in WEIGHTS])
```

```python
import functools
import math

import numpy as np
import jax
import jax.numpy as jnp
from jax import lax
from jax.experimental import pallas as pl
from jax.experimental.pallas import tpu as pltpu

F32 = jnp.float32
BF16 = jnp.bfloat16

N_DEV = 8
D_MODEL = 1024
D_FF = 2816
FF_SHARD = D_FF // N_DEV
FF_PAD = 384
ATT_W = 512
ATT_HEADS = 8
DILATED = ((128, 1), (512, 4), (2048, 16))
BLK = 128
ML_W = 512
ML_HEADS = 4
ML_HD = 128
CHUNK = 128
CONV_K = 4
W_IN = 3592
W_IN_SHARD = W_IN // N_DEV
W_IN_MAIN = 3584
XA_HEADS = 4
XA_HD = 256
MEM_LEN = 256
REL_BUCKETS = 32
REL_MAX_DIST = 2048
ALPHA = 2.0 ** 0.25
LN_EPS = 1e-5
NEG = -1e30
ADAM_LR = 0.001
ADAM_B1 = 0.9
ADAM_B2 = 0.999
ADAM_EPS = 1e-08
ADAM_WD = 0.01
ADAM_STEP = 10
LANES = 128
VMEM_LIMIT = 58 * 1024 * 1024

NN = (((1,), (0,)), ((), ()))
NT = (((1,), (1,)), ((), ()))
TN = (((0,), (0,)), ((), ()))


def _dot(a, b, dims):
    return lax.dot_general(a, b, dims, preferred_element_type=F32)


def _params(*sem):
    return pltpu.CompilerParams(dimension_semantics=sem, vmem_limit_bytes=VMEM_LIMIT)


def _sigmoid(x):
    return 1.0 / (1.0 + jnp.exp(-x))


def _rowsum8(x):
    t, c = x.shape
    return jnp.sum(x.reshape(t // 8, 8, c), axis=0)


def _mesh_pos():
    x, y, c = lax.axis_index("x"), lax.axis_index("y"), lax.axis_index("c")
    return x, y, c, 4 * x + 2 * y + c


def _peer(x, y, c, k):
    px = 1 - x if k & 4 else x
    py = 1 - y if k & 2 else y
    pc = 1 - c if k & 1 else c
    return (px, py, pc), 4 * px + 2 * py + pc


def _call(body, *, name, grid, in_specs, out_specs, out_shape, args, scratch_shapes=(), sem=None,
          gather=(), exchange=()):
    in_specs, out_specs, out_shape, scratch = list(in_specs), list(out_specs), list(out_shape), list(scratch_shapes)
    ng, nc = len(gather), len(gather) + len(exchange)
    if nc == 0:
        return pl.pallas_call(body, name=name, grid=grid, in_specs=in_specs, out_specs=out_specs,
                              out_shape=out_shape, scratch_shapes=scratch, compiler_params=_params(*sem))(*args)
    n_in, n_out, n_scr = len(in_specs), len(out_specs), len(scratch)

    def wrapped(*refs):
        ins, cin = refs[:n_in], refs[n_in:n_in + nc]
        outs, cout = refs[n_in + nc:n_in + nc + n_out], refs[n_in + nc + n_out:n_in + 2 * nc + n_out]
        scr = refs[n_in + 2 * nc + n_out:n_in + 2 * nc + n_out + n_scr]
        send_sems, recv_sems, loc_sems = refs[-3:]
        first, last = None, None
        for ax, extent in enumerate(grid):
            f, l = pl.program_id(ax) == 0, pl.program_id(ax) == extent - 1
            first = f if first is None else first & f
            last = l if last is None else last & l

        def copies():
            x, y, c, me = _mesh_pos()
            out = []
            for a in range(nc):
                mine = cin[a] if a < ng else cin[a].at[me]
                out.append(pltpu.make_async_copy(mine, cout[a].at[me], loc_sems.at[a]))
                for k in range(1, N_DEV):
                    peer, pidx = _peer(x, y, c, k)
                    out.append(pltpu.make_async_remote_copy(
                        src_ref=cin[a] if a < ng else cin[a].at[pidx], dst_ref=cout[a].at[me],
                        send_sem=send_sems.at[a, k - 1], recv_sem=recv_sems.at[a, k - 1],
                        device_id=peer, device_id_type=pl.DeviceIdType.MESH))
            return out

        @pl.when(first)
        def _():
            for cp in copies():
                cp.start()

        body(*ins, *outs, *scr)

        @pl.when(last)
        def _():
            for cp in copies():
                cp.wait()

    hbm = pl.BlockSpec(memory_space=pl.ANY)
    comm_shapes = [jax.ShapeDtypeStruct((N_DEV,) + a.shape, a.dtype) for a in gather]
    comm_shapes += [jax.ShapeDtypeStruct(a.shape, a.dtype) for a in exchange]
    return pl.pallas_call(
        wrapped, name=name, grid=grid, in_specs=in_specs + [hbm] * nc, out_specs=out_specs + [hbm] * nc,
        out_shape=out_shape + comm_shapes,
        scratch_shapes=scratch + [pltpu.SemaphoreType.DMA((nc, N_DEV - 1)), pltpu.SemaphoreType.DMA((nc, N_DEV - 1)),
                                  pltpu.SemaphoreType.DMA((nc,))],
        compiler_params=_params(*(("arbitrary",) * len(grid))),
    )(*args, *gather, *exchange)


def _gather_two_level(name, arrays):
    na = len(arrays)

    def body(*refs):
        srcs, outs = refs[:na], refs[na:2 * na]
        send_sems, recv_sems, loc_sems = refs[2 * na:]
        x, y, c, me = _mesh_pos()
        here, sib = (x, y, c), (x, y, 1 - c)
        chips = [(1 - x, y), (x, 1 - y), (1 - x, 1 - y)]
        pos = lambda px, py, pc: 4 * px + 2 * py + pc

        def copy(a, k, block, to, src=None):
            return pltpu.make_async_remote_copy(
                src_ref=outs[a].at[block] if src is None else src, dst_ref=outs[a].at[block],
                send_sem=send_sems.at[a, k], recv_sem=recv_sems.at[a, k], device_id=to,
                device_id_type=pl.DeviceIdType.MESH)

        locs = [pltpu.make_async_copy(srcs[a], outs[a].at[me], loc_sems.at[a]) for a in range(na)]
        for cp in locs:
            cp.start()
        first = []
        for a in range(na):
            first.append(copy(a, 0, me, sib, src=srcs[a]))
            first += [copy(a, 1 + j, me, (*chip, c), src=srcs[a]) for j, chip in enumerate(chips)]
        for cp in first:
            cp.start()
        passed = []
        for a in range(na):
            for j, chip in enumerate(chips):
                copy(a, 1 + j, pos(*chip, c), here).wait_recv()
                passed.append(copy(a, 4 + j, pos(*chip, c), sib))
                passed[-1].start()
        for a in range(na):
            copy(a, 0, pos(x, y, 1 - c), here).wait_recv()
            for j, chip in enumerate(chips):
                copy(a, 4 + j, pos(*chip, 1 - c), here).wait_recv()
        for cp in first + passed:
            cp.wait_send()
        for cp in locs:
            cp.wait()

    hbm = pl.BlockSpec(memory_space=pl.ANY)
    return pl.pallas_call(
        body, name=name, in_specs=[hbm] * na, out_specs=[hbm] * na,
        out_shape=[jax.ShapeDtypeStruct((N_DEV,) + a.shape, a.dtype) for a in arrays],
        scratch_shapes=[pltpu.SemaphoreType.DMA((na, N_DEV - 1)), pltpu.SemaphoreType.DMA((na, N_DEV - 1)),
                        pltpu.SemaphoreType.DMA((na,))],
    )(*arrays)


def _exchange_only(name, gather=(), exchange=()):
    return _call(lambda: None, name=name, grid=(1,), in_specs=[], out_specs=[], out_shape=[], args=(),
                 gather=gather, exchange=exchange)


def _matmul(a, b, mode, name, *, out_dtype=F32, tm=512, tn=512, tk=512, add=None, add_scale=1.0,
            blocked_out=False, gather=(), exchange=()):
    blocked_b = b.ndim == 3
    if blocked_b:
        (m, k), (nb, _, tn) = a.shape, b.shape
        n = nb * tn
    elif mode == "nn":
        (m, k), (_, n) = a.shape, b.shape
    elif mode == "nt":
        (m, k), (n, _) = a.shape, b.shape
    else:
        (k, m), (_, n) = a.shape, b.shape
    tm, tn, tk = min(tm, m), min(tn, n), min(tk, k)
    nk = k // tk
    dims = {"nn": NN, "nt": NT, "tn": TN}[mode]
    if mode == "tn":
        a_spec = pl.BlockSpec((tk, tm), lambda i, j, kk: (kk, i))
    else:
        a_spec = pl.BlockSpec((tm, tk), lambda i, j, kk: (i, kk))
    if blocked_b:
        b_spec = pl.BlockSpec((None, tk, tn), lambda i, j, kk: (j, kk, 0))
    elif mode == "nt":
        b_spec = pl.BlockSpec((tn, tk), lambda i, j, kk: (j, kk))
    else:
        b_spec = pl.BlockSpec((tk, tn), lambda i, j, kk: (kk, j))
    if blocked_out:
        o_spec = pl.BlockSpec((None, tm, tn), lambda i, j, kk: (j, i, 0))
        o_shape = jax.ShapeDtypeStruct((n // tn, m, tn), out_dtype)
    else:
        o_spec = pl.BlockSpec((tm, tn), lambda i, j, kk: (i, j))
        o_shape = jax.ShapeDtypeStruct((m, n), out_dtype)
    has_add = add is not None
    cache_a = nk == 1 and mode != "tn" and n // tn > 1 and a.dtype != BF16

    def body(*refs):
        if has_add:
            a_ref, b_ref, add_ref, o_ref, s_ref = refs
        else:
            a_ref, b_ref, o_ref, s_ref = refs
        kk = pl.program_id(2)
        if cache_a:
            @pl.when(pl.program_id(1) == 0)
            def _():
                s_ref[...] = a_ref[...].astype(BF16)

            lhs = s_ref[...]
        else:
            lhs = a_ref[...].astype(BF16)
        part = _dot(lhs, b_ref[...].astype(BF16), dims)

        def finish(r):
            if has_add:
                r = r + add_scale * add_ref[...]
            o_ref[...] = r.astype(out_dtype)

        if nk == 1:
            finish(part)
            return

        @pl.when(kk == 0)
        def _():
            s_ref[...] = part

        @pl.when(kk > 0)
        def _():
            s_ref[...] += part

        @pl.when(kk == nk - 1)
        def _():
            finish(s_ref[...])

    if nk > 1:
        scratch = [pltpu.VMEM((tm, tn), F32)]
    else:
        scratch = [pltpu.VMEM((tm, tk), BF16) if cache_a else pltpu.VMEM((8, LANES), F32)]
    return _call(
        body, name=name, grid=(m // tm, n // tn, nk),
        in_specs=[a_spec, b_spec] + ([pl.BlockSpec((tm, tn), lambda i, j, kk: (i, j))] if has_add else []),
        out_specs=[o_spec], out_shape=[o_shape], args=(a, b) + ((add,) if has_add else ()),
        scratch_shapes=scratch, sem=("parallel", "arbitrary", "arbitrary"),
        gather=gather, exchange=exchange)


def _ln_fwd_math(u, g, b):
    mu = jnp.mean(u, axis=-1, keepdims=True)
    uc = u - mu
    var = jnp.mean(uc * uc, axis=-1, keepdims=True)
    return uc * lax.rsqrt(var + LN_EPS) * g + b


def _ln_bwd_math(dy, u, g):
    mu = jnp.mean(u, axis=-1, keepdims=True)
    uc = u - mu
    var = jnp.mean(uc * uc, axis=-1, keepdims=True)
    rstd = lax.rsqrt(var + LN_EPS)
    xhat = uc * rstd
    dxh = dy * g
    m1 = jnp.mean(dxh, axis=-1, keepdims=True)
    m2 = jnp.mean(dxh * xhat, axis=-1, keepdims=True)
    return rstd * (dxh - m1 - xhat * m2), xhat


def _matmul_resid_ln(a, w, x, g, b, name, tm=512):
    s, k = a.shape
    d = w.shape[1]

    def body(a_ref, w_ref, x_ref, g_ref, b_ref, u_ref, y_ref):
        u = ALPHA * x_ref[...] + _dot(a_ref[...].astype(BF16), w_ref[...], NN)
        u_ref[...] = u
        y_ref[...] = _ln_fwd_math(u, g_ref[...], b_ref[...])

    row = pl.BlockSpec((tm, d), lambda i: (i, 0))
    vec = pl.BlockSpec((1, d), lambda i: (0, 0))
    return pl.pallas_call(
        body, name=name, grid=(s // tm,),
        in_specs=[pl.BlockSpec((tm, k), lambda i: (i, 0)), pl.BlockSpec((k, d), lambda i: (0, 0)), row, vec, vec],
        out_specs=[row, row], out_shape=[jax.ShapeDtypeStruct((s, d), F32)] * 2,
        compiler_params=_params("parallel"),
    )(a, w, x, g, b)


def _ln_bwd(dy, u, g, name, tm=512):
    s, d = dy.shape
    nt = s // tm

    def body(dy_ref, u_ref, g_ref, du_ref, dg_ref, db_ref, g8, b8):
        i = pl.program_id(0)
        dy_ = dy_ref[...]
        du, xhat = _ln_bwd_math(dy_, u_ref[...], g_ref[...])
        du_ref[...] = du

        @pl.when(i == 0)
        def _():
            g8[...] = jnp.zeros_like(g8)
            b8[...] = jnp.zeros_like(b8)

        g8[...] += _rowsum8(dy_ * xhat)
        b8[...] += _rowsum8(dy_)

        @pl.when(i == nt - 1)
        def _():
            dg_ref[...] = jnp.sum(g8[...], axis=0, keepdims=True)
            db_ref[...] = jnp.sum(b8[...], axis=0, keepdims=True)

    row = pl.BlockSpec((tm, d), lambda i: (i, 0))
    vec = pl.BlockSpec((1, d), lambda i: (0, 0))
    return pl.pallas_call(
        body, name=name, grid=(nt,),
        in_specs=[row, row, vec], out_specs=[row, vec, vec],
        out_shape=[jax.ShapeDtypeStruct((s, d), F32), jax.ShapeDtypeStruct((1, d), F32),
                   jax.ShapeDtypeStruct((1, d), F32)],
        scratch_shapes=[pltpu.VMEM((8, d), F32), pltpu.VMEM((8, d), F32)],
        compiler_params=_params("arbitrary"),
    )(dy, u, g)


FF_PAIR = 2 * FF_PAD
N_PAIR = N_DEV // 2


def _ffn_fwd(x, wgt, wut, wd, g, b, name, tm=512, gather=()):
    s, d = x.shape

    def body(x_ref, wg_ref, wu_ref, wd_ref, g_ref, b_ref, u_ref, y_ref, xb, acc):
        k = pl.program_id(1)

        @pl.when(k == 0)
        def _():
            xb[...] = x_ref[...].astype(BF16)

        a = _dot(xb[...], wg_ref[...], NT)
        bb = _dot(xb[...], wu_ref[...], NT)
        h = (a * _sigmoid(a) * bb).astype(BF16)
        part = _dot(h, wd_ref[...], NN)

        @pl.when(k == 0)
        def _():
            acc[...] = part

        @pl.when(k > 0)
        def _():
            acc[...] += part

        @pl.when(k == N_PAIR - 1)
        def _():
            u = ALPHA * x_ref[...] + 0.5 * acc[...]
            u_ref[...] = u
            y_ref[...] = _ln_fwd_math(u, g_ref[...], b_ref[...])

    row = pl.BlockSpec((tm, d), lambda i, k: (i, 0))
    vec = pl.BlockSpec((1, d), lambda i, k: (0, 0))
    w_in = pl.BlockSpec((None, FF_PAIR, d), lambda i, k: (k, 0, 0))
    w_dn = w_in
    return _call(
        body, name=name, grid=(s // tm, N_PAIR),
        in_specs=[row, w_in, w_in, w_dn, vec, vec], out_specs=[row, row],
        out_shape=[jax.ShapeDtypeStruct((s, d), F32)] * 2, args=(x, wgt, wut, wd, g, b),
        scratch_shapes=[pltpu.VMEM((tm, d), BF16), pltpu.VMEM((tm, d), F32)],
        sem=("parallel", "arbitrary"), gather=gather)


def _ffn_bwd_x(dy, u, x, wgt, wut, wd, g, name, tm=512, exchange=()):
    s, d = x.shape
    nt = s // tm
    ffp = N_DEV * FF_PAD

    def body(dy_ref, u_ref, x_ref, wg_ref, wu_ref, wd_ref, g_ref,
             dx_ref, xb, df_ref, da_ref, db_ref, h_ref, dg_ref, dbl_ref,
             dfb, du_s, acc, g8, b8):
        i = pl.program_id(0)
        k = pl.program_id(1)

        @pl.when(k == 0)
        def _():
            dy_ = dy_ref[...]
            du, xhat = _ln_bwd_math(dy_, u_ref[...], g_ref[...])
            du_s[...] = du
            dfb[...] = (0.5 * du).astype(BF16)
            df_ref[...] = dfb[...]
            xb[...] = x_ref[...].astype(BF16)

            @pl.when(i == 0)
            def _():
                g8[...] = jnp.zeros_like(g8)
                b8[...] = jnp.zeros_like(b8)

            g8[...] += _rowsum8(dy_ * xhat)
            b8[...] += _rowsum8(dy_)

        a = _dot(xb[...], wg_ref[...], NT)
        bb = _dot(xb[...], wu_ref[...], NT)
        sig = _sigmoid(a)
        sa = a * sig
        h_ref[...] = (sa * bb).astype(BF16)
        dh = _dot(dfb[...], wd_ref[...], NT)
        da = (dh * bb * (sig * (1.0 + a * (1.0 - sig)))).astype(BF16)
        db = (dh * sa).astype(BF16)
        da_ref[...] = da
        db_ref[...] = db
        part = _dot(da, wg_ref[...], NN) + _dot(db, wu_ref[...], NN)

        @pl.when(k == 0)
        def _():
            acc[...] = part

        @pl.when(k > 0)
        def _():
            acc[...] += part

        @pl.when(k == N_PAIR - 1)
        def _():
            dx_ref[...] = ALPHA * du_s[...] + acc[...]

        @pl.when((k == N_PAIR - 1) & (i == nt - 1))
        def _():
            dg_ref[...] = jnp.sum(g8[...], axis=0, keepdims=True)
            dbl_ref[...] = jnp.sum(b8[...], axis=0, keepdims=True)

    row = pl.BlockSpec((tm, d), lambda i, k: (i, 0))
    vec = pl.BlockSpec((1, d), lambda i, k: (0, 0))
    w_in = pl.BlockSpec((None, FF_PAIR, d), lambda i, k: (k, 0, 0))
    hid = pl.BlockSpec((tm, FF_PAIR), lambda i, k: (i, k))
    return _call(
        body, name=name, grid=(nt, N_PAIR),
        in_specs=[row, row, row, w_in, w_in, w_in, vec],
        out_specs=[row, row, row, hid, hid, hid, vec, vec],
        out_shape=[jax.ShapeDtypeStruct((s, d), F32), jax.ShapeDtypeStruct((s, d), BF16),
                   jax.ShapeDtypeStruct((s, d), BF16),
                   jax.ShapeDtypeStruct((s, ffp), BF16), jax.ShapeDtypeStruct((s, ffp), BF16),
                   jax.ShapeDtypeStruct((s, ffp), BF16),
                   jax.ShapeDtypeStruct((1, d), F32), jax.ShapeDtypeStruct((1, d), F32)],
        args=(dy, u, x, wgt, wut, wd, g),
        scratch_shapes=[pltpu.VMEM((tm, d), BF16), pltpu.VMEM((tm, d), F32),
                        pltpu.VMEM((tm, d), F32), pltpu.VMEM((8, d), F32), pltpu.VMEM((8, d), F32)],
        sem=("arbitrary", "arbitrary"), exchange=exchange)


def _ffn_bwd_w(xb, df, da, db, h, name, tm=512):
    s, d = xb.shape
    nt = s // tm

    def body(x_ref, df_ref, da_ref, db_ref, h_ref, dwg_ref, dwu_ref, dwd_ref, ag, au, ad):
        i = pl.program_id(1)
        pg = _dot(x_ref[...], da_ref[...], TN)
        pu = _dot(x_ref[...], db_ref[...], TN)
        pd = _dot(h_ref[...], df_ref[...], TN)

        @pl.when(i == 0)
        def _():
            ag[...] = pg
            au[...] = pu
            ad[...] = pd

        @pl.when(i > 0)
        def _():
            ag[...] += pg
            au[...] += pu
            ad[...] += pd

        @pl.when(i == nt - 1)
        def _():
            for j in range(2):
                lo = j * FF_PAD
                dwg_ref[j] = ag[:, lo:lo + FF_SHARD].astype(BF16)
                dwu_ref[j] = au[:, lo:lo + FF_SHARD].astype(BF16)
                dwd_ref[j] = ad[lo:lo + FF_SHARD, :].astype(BF16)

    row = pl.BlockSpec((tm, d), lambda k, i: (i, 0))
    hid = pl.BlockSpec((tm, FF_PAIR), lambda k, i: (i, k))
    w_in = pl.BlockSpec((2, d, FF_SHARD), lambda k, i: (k, 0, 0))
    w_dn = pl.BlockSpec((2, FF_SHARD, d), lambda k, i: (k, 0, 0))
    return _call(
        body, name=name, grid=(N_PAIR, nt),
        in_specs=[row, row, hid, hid, hid], out_specs=[w_in, w_in, w_dn],
        out_shape=[jax.ShapeDtypeStruct((N_DEV, d, FF_SHARD), BF16), jax.ShapeDtypeStruct((N_DEV, d, FF_SHARD), BF16),
                   jax.ShapeDtypeStruct((N_DEV, FF_SHARD, d), BF16)],
        args=(xb, df, da, db, h),
        scratch_shapes=[pltpu.VMEM((d, FF_PAIR), F32), pltpu.VMEM((d, FF_PAIR), F32), pltpu.VMEM((FF_PAIR, d), F32)],
        sem=("parallel", "arbitrary"))


def _bucket_tables():
    qi = np.arange(BLK)[:, None]
    ki = np.arange(2 * BLK)[None, :]
    off = qi + BLK - ki
    out = []
    for window, dil in DILATED:
        n_keys = window // dil
        dist = dil * np.clip(off, 0, n_keys)
        exact = REL_BUCKETS // 2
        df = np.maximum(dist, 1).astype(np.float32)
        large = exact + (np.log(df / np.float32(exact)) / np.float32(math.log(REL_MAX_DIST / exact))
                         * np.float32(REL_BUCKETS - exact)).astype(np.int32)
        large = np.minimum(large, REL_BUCKETS - 1)
        bucket = np.where(dist < exact, dist, large).astype(np.int32)
        band = (off >= 0) & (off <= n_keys)
        out.append(np.where(band, bucket, -1))
    return np.stack(out).astype(np.int32)


def _bias_fwd(rel_bias, buckets, name="bias_fwd"):
    def body(tbl_ref, bkt_ref, out_ref):
        bkt = bkt_ref[...]
        for h in range(ATT_HEADS):
            acc = jnp.full((BLK, 2 * BLK), NEG, F32)
            for bb in range(REL_BUCKETS):
                acc = jnp.where(bkt == bb, tbl_ref[bb, h], acc)
            out_ref[h] = acc

    nbr = len(DILATED)
    return pl.pallas_call(
        body, name=name, grid=(nbr,),
        in_specs=[pl.BlockSpec(memory_space=pltpu.SMEM),
                  pl.BlockSpec((None, BLK, 2 * BLK), lambda r: (r, 0, 0))],
        out_specs=pl.BlockSpec((None, ATT_HEADS, BLK, 2 * BLK), lambda r: (r, 0, 0, 0)),
        out_shape=jax.ShapeDtypeStruct((nbr, ATT_HEADS, BLK, 2 * BLK), F32),
        compiler_params=_params("parallel"),
    )(rel_bias, buckets)


def _bias_bwd(dbias, buckets, name="bias_bwd"):
    nbr = len(DILATED)

    def body(db_ref, bkt_ref, out_ref):
        r = pl.program_id(0)

        @pl.when(r == 0)
        def _():
            out_ref[...] = jnp.zeros_like(out_ref)

        bkt = bkt_ref[...]
        rowi = lax.broadcasted_iota(jnp.int32, (REL_BUCKETS, LANES), 0)
        coli = lax.broadcasted_iota(jnp.int32, (REL_BUCKETS, LANES), 1)
        acc = jnp.zeros((REL_BUCKETS, LANES), F32)
        for h in range(ATT_HEADS):
            x = db_ref[h]
            for bb in range(REL_BUCKETS):
                part = jnp.sum(jnp.where(bkt == bb, x, 0.0), axis=0, keepdims=True)
                tot = jnp.sum(part, axis=1, keepdims=True)
                acc = acc + jnp.where((rowi == bb) & (coli == h), tot, 0.0)
        out_ref[...] += acc

    return pl.pallas_call(
        body, name=name, grid=(nbr,),
        in_specs=[pl.BlockSpec((None, ATT_HEADS, BLK, 2 * BLK), lambda r: (r, 0, 0, 0)),
                  pl.BlockSpec((None, BLK, 2 * BLK), lambda r: (r, 0, 0))],
        out_specs=pl.BlockSpec((REL_BUCKETS, LANES), lambda r: (0, 0)),
        out_shape=jax.ShapeDtypeStruct((REL_BUCKETS, LANES), F32),
        compiler_params=_params("arbitrary"),
    )(dbias, buckets)


def _att_scores(q_pair, k2, bias, first_ok, msk):
    qm = jnp.where(msk, q_pair, 0.0).astype(BF16)
    sc = _dot(qm, k2, NT) * (64 ** -0.5) + bias
    return jnp.where(first_ok, sc, NEG), qm


DIL_TILE = 2048
DIL_COLS = ATT_W // LANES
DIL_UNROLL = 4


def _dil_rows(dil, n, r, base=0):
    start = base + n * (BLK * dil) + r
    return pl.ds(start, BLK, stride=dil) if dil > 1 else pl.ds(start, BLK)


def _dil_in_specs(tile_of):
    cur = lambda col: pl.BlockSpec((DIL_TILE, LANES), lambda p, i: (tile_of(i), col * DIL_COLS + p))
    prev = lambda col: pl.BlockSpec((DIL_TILE, LANES), lambda p, i: (jnp.maximum(tile_of(i) - 1, 0), col * DIL_COLS + p))
    bias = pl.BlockSpec((len(DILATED), 2, BLK, 2 * BLK), lambda p, i: (0, p, 0, 0))
    return [cur(0), prev(1), cur(1), prev(2), cur(2), bias]


def _dil_fwd(proj, biasm, name="dil_fwd", gather=()):
    s = proj.shape[0]
    nt = s // DIL_TILE
    tt = DIL_TILE

    def body(q_ref, kp_ref, kc_ref, vp_ref, vc_ref, bias_ref, att_ref, lse_ref, k2, v2, ob, lb):
        t = pl.program_id(1)
        k2[0:tt, :] = kp_ref[...]
        k2[tt:2 * tt, :] = kc_ref[...]
        v2[0:tt, :] = vp_ref[...]
        v2[tt:2 * tt, :] = vc_ref[...]
        lo = lax.broadcasted_iota(jnp.int32, (BLK, LANES), 1) < 64
        kidx = lax.broadcasted_iota(jnp.int32, (BLK, 2 * BLK), 1)
        for b, (_, dil) in enumerate(DILATED):
            nblk = tt // (BLK * dil)

            def step(j, carry, b=b, dil=dil, nblk=nblk):
                r, n = j % dil, j // dil
                cur, prev = _dil_rows(dil, n, r, tt), _dil_rows(dil, n - 1, r, tt)
                here = _dil_rows(dil, n, r)
                q_pair = q_ref[here, :]
                kk = jnp.concatenate([k2[prev, :], k2[cur, :]], axis=0).astype(BF16)
                vv = jnp.concatenate([v2[prev, :], v2[cur, :]], axis=0).astype(BF16)
                first_ok = (t > 0) | (n > 0) | (kidx >= BLK)
                outs, lses = [], []
                for hh in range(2):
                    msk = lo if hh == 0 else jnp.logical_not(lo)
                    sc, _ = _att_scores(q_pair, kk, bias_ref[b, hh], first_ok, msk)
                    mx = jnp.max(sc, axis=1, keepdims=True)
                    pe = jnp.exp(sc - mx)
                    l = jnp.sum(pe, axis=1, keepdims=True)
                    outs.append(_dot(pe.astype(BF16), vv, NN) / l)
                    lses.append(jnp.broadcast_to(mx + jnp.log(l), (BLK, LANES)))
                ob.at[b][here, :] = jnp.where(lo, outs[0], outs[1])
                lb.at[b][here, :] = jnp.where(lo, lses[0], lses[1])
                return carry

            lax.fori_loop(0, tt // BLK, step, 0, unroll=DIL_UNROLL)
        l0, l1, l2 = lb[0], lb[1], lb[2]
        mx = jnp.maximum(jnp.maximum(l0, l1), l2)
        e0, e1, e2 = jnp.exp(l0 - mx), jnp.exp(l1 - mx), jnp.exp(l2 - mx)
        tot = e0 + e1 + e2
        att_ref[...] = (e0 * ob[0] + e1 * ob[1] + e2 * ob[2]) / tot
        lse_ref[...] = mx + jnp.log(tot)

    out = pl.BlockSpec((tt, LANES), lambda p, i: (i, p))
    return _call(
        body, name=name, grid=(DIL_COLS, nt), in_specs=_dil_in_specs(lambda i: i), out_specs=[out, out],
        out_shape=[jax.ShapeDtypeStruct((s, ATT_W), F32)] * 2, args=(proj, proj, proj, proj, proj, biasm),
        scratch_shapes=[pltpu.VMEM((2 * tt, LANES), F32), pltpu.VMEM((2 * tt, LANES), F32),
                        pltpu.VMEM((len(DILATED), tt, LANES), F32), pltpu.VMEM((len(DILATED), tt, LANES), F32)],
        sem=("parallel", "parallel"), gather=gather)


def _dil_bwd(proj, biasm, lse, att, dcat, name="dil_bwd"):
    s = proj.shape[0]
    nt = s // DIL_TILE
    tt = DIL_TILE
    nbr = len(DILATED)

    def body(q_ref, kp_ref, kc_ref, vp_ref, vc_ref, bias_ref, lse_ref, att_ref, datt_ref,
             dq_ref, dk_ref, dv_ref, dbias_ref, k2, v2, dqa, dka, dva, kcar, vcar):
        i = pl.program_id(1)
        t = nt - 1 - i
        k2[0:tt, :] = kp_ref[...]
        k2[tt:2 * tt, :] = kc_ref[...]
        v2[0:tt, :] = vp_ref[...]
        v2[tt:2 * tt, :] = vc_ref[...]

        @pl.when(i == 0)
        def _():
            kcar[...] = jnp.zeros_like(kcar)
            vcar[...] = jnp.zeros_like(vcar)
            dbias_ref[...] = jnp.zeros_like(dbias_ref)

        dqa[...] = jnp.zeros_like(dqa)
        dka[0:tt, :] = jnp.zeros((tt, LANES), F32)
        dva[0:tt, :] = jnp.zeros((tt, LANES), F32)
        dka[tt:2 * tt, :] = kcar[...]
        dva[tt:2 * tt, :] = vcar[...]
        lo = lax.broadcasted_iota(jnp.int32, (BLK, LANES), 1) < 64
        kidx = lax.broadcasted_iota(jnp.int32, (BLK, 2 * BLK), 1)
        for b, (_, dil) in enumerate(DILATED):
            nblk = tt // (BLK * dil)

            def step(j, carry, b=b, dil=dil, nblk=nblk):
                r, n = j % dil, j // dil
                cur, prev = _dil_rows(dil, n, r, tt), _dil_rows(dil, n - 1, r, tt)
                here = _dil_rows(dil, n, r)
                q_pair = q_ref[here, :]
                kk = jnp.concatenate([k2[prev, :], k2[cur, :]], axis=0).astype(BF16)
                vv = jnp.concatenate([v2[prev, :], v2[cur, :]], axis=0).astype(BF16)
                first_ok = (t > 0) | (n > 0) | (kidx >= BLK)
                lse_pair = lse_ref[here, :]
                dat_pair = datt_ref[here, :]
                dd_pair = dat_pair * att_ref[here, :]
                dqs, dk2, dv2 = [], None, None
                for hh in range(2):
                    msk = lo if hh == 0 else jnp.logical_not(lo)
                    sc, qm = _att_scores(q_pair, kk, bias_ref[b, hh], first_ok, msk)
                    lse_h = jnp.max(jnp.where(msk, lse_pair, -jnp.inf), axis=1, keepdims=True)
                    pr = jnp.exp(sc - lse_h)
                    dsum = jnp.sum(jnp.where(msk, dd_pair, 0.0), axis=1, keepdims=True)
                    dom = jnp.where(msk, dat_pair, 0.0).astype(BF16)
                    ds = pr * (_dot(dom, vv, NT) - dsum)
                    dbias_ref[b, hh] += ds
                    dsb = (ds * (64 ** -0.5)).astype(BF16)
                    dqs.append(_dot(dsb, kk, NN))
                    dkh = _dot(dsb, qm, TN)
                    dvh = _dot(pr.astype(BF16), dom, TN)
                    dk2 = dkh if dk2 is None else dk2 + dkh
                    dv2 = dvh if dv2 is None else dv2 + dvh
                dqa[here, :] += jnp.where(lo, dqs[0], dqs[1])
                dka[prev, :] += dk2[:BLK]
                dka[cur, :] += dk2[BLK:]
                dva[prev, :] += dv2[:BLK]
                dva[cur, :] += dv2[BLK:]
                return carry

            lax.fori_loop(0, tt // BLK, step, 0, unroll=DIL_UNROLL)
        dq_ref[...] = dqa[...].astype(BF16)
        dk_ref[...] = dka[tt:2 * tt, :].astype(BF16)
        dv_ref[...] = dva[tt:2 * tt, :].astype(BF16)
        kcar[...] = dka[0:tt, :]
        vcar[...] = dva[0:tt, :]

    rev = lambda i: nt - 1 - i
    out = pl.BlockSpec((tt, LANES), lambda p, i: (rev(i), p))
    two = lambda: pltpu.VMEM((2 * tt, LANES), F32)
    one = lambda: pltpu.VMEM((tt, LANES), F32)
    return pl.pallas_call(
        body, name=name, grid=(DIL_COLS, nt),
        in_specs=_dil_in_specs(rev) + [out, out, out],
        out_specs=[out, out, out, pl.BlockSpec((nbr, 2, BLK, 2 * BLK), lambda p, i: (0, p, 0, 0))],
        out_shape=[jax.ShapeDtypeStruct((s, ATT_W), BF16)] * 3 + [jax.ShapeDtypeStruct((nbr, ATT_HEADS, BLK, 2 * BLK), F32)],
        scratch_shapes=[two(), two(), one(), two(), two(), one(), one()],
        compiler_params=_params("arbitrary", "arbitrary"),
    )(proj, proj, proj, proj, proj, biasm, lse, att, dcat)


QK_COL0 = (3 * ATT_W) // ATT_W


def _conv_shifted(prev, cur, j, row):
    sh = CONV_K - 1 - j
    if sh == 0:
        return cur
    return jnp.where(row < sh, pltpu.roll(prev, sh, 0), pltpu.roll(cur, sh, 0))


def _conv_z(prev, cur, w_ref, b_ref, row):
    z = b_ref[...] + cur * w_ref[CONV_K - 1:CONV_K, :]
    for j in range(CONV_K - 1):
        z = z + _conv_shifted(prev, cur, j, row) * w_ref[j:j + 1, :]
    return z


def _conv_fwd(proj, conv_w, conv_b, name="conv_fwd", tm=512):
    s = proj.shape[0]
    w = ATT_W

    def body(prev_ref, cur_ref, w_ref, b_ref, o_ref):
        i = pl.program_id(1)
        row = lax.broadcasted_iota(jnp.int32, (tm, w), 0)
        prev = jnp.where(i > 0, prev_ref[...], 0.0)
        z = _conv_z(prev, cur_ref[...], w_ref, b_ref, row)
        o_ref[...] = z * _sigmoid(z)

    return pl.pallas_call(
        body, name=name, grid=(2, s // tm),
        in_specs=[pl.BlockSpec((tm, w), lambda j, i: (jnp.maximum(i - 1, 0), QK_COL0 + j)),
                  pl.BlockSpec((tm, w), lambda j, i: (i, QK_COL0 + j)),
                  pl.BlockSpec((CONV_K, w), lambda j, i: (0, j)),
                  pl.BlockSpec((1, w), lambda j, i: (0, j))],
        out_specs=pl.BlockSpec((tm, w), lambda j, i: (i, j)),
        out_shape=jax.ShapeDtypeStruct((s, 2 * ML_W), F32),
        compiler_params=_params("parallel", "parallel"),
    )(proj, proj, conv_w, conv_b)


def _conv_bwd(proj, dqk, conv_w, conv_b, name="conv_bwd", tm=512):
    s = proj.shape[0]
    w = ATT_W
    nt = s // tm

    def body(xp_ref, xc_ref, xn_ref, dc_ref, dn_ref, w_ref, b_ref, dx_ref, dw_ref, db_ref):
        i = pl.program_id(1)
        row = lax.broadcasted_iota(jnp.int32, (tm, w), 0)
        prev = jnp.where(i > 0, xp_ref[...], 0.0)
        cur = xc_ref[...]

        def dz_of(pv, cv, dy):
            z = _conv_z(pv, cv, w_ref, b_ref, row)
            sig = _sigmoid(z)
            return dy * (sig * (1.0 + z * (1.0 - sig)))

        dzc = dz_of(prev, cur, dc_ref[...])
        dzn = jnp.where(i < nt - 1, dz_of(cur, xn_ref[...], dn_ref[...]), 0.0)
        dx = dzc * w_ref[CONV_K - 1:CONV_K, :]
        for j in range(CONV_K - 1):
            sh = CONV_K - 1 - j
            up = jnp.where(row >= tm - sh, pltpu.roll(dzn, tm - sh, 0), pltpu.roll(dzc, tm - sh, 0))
            dx = dx + up * w_ref[j:j + 1, :]
        dx_ref[...] = dx

        @pl.when(i == 0)
        def _():
            dw_ref[...] = jnp.zeros_like(dw_ref)
            db_ref[...] = jnp.zeros_like(db_ref)

        for j in range(CONV_K):
            dw_ref[j:j + 1, :] += jnp.sum(dzc * _conv_shifted(prev, cur, j, row), axis=0, keepdims=True)
        db_ref[...] += jnp.sum(dzc, axis=0, keepdims=True)

    xs = lambda f: pl.BlockSpec((tm, w), lambda j, i: (f(i), QK_COL0 + j))
    ds = lambda f: pl.BlockSpec((tm, w), lambda j, i: (f(i), j))
    return pl.pallas_call(
        body, name=name, grid=(2, nt),
        in_specs=[xs(lambda i: jnp.maximum(i - 1, 0)), xs(lambda i: i), xs(lambda i: jnp.minimum(i + 1, nt - 1)),
                  ds(lambda i: i), ds(lambda i: jnp.minimum(i + 1, nt - 1)),
                  pl.BlockSpec((CONV_K, w), lambda j, i: (0, j)), pl.BlockSpec((1, w), lambda j, i: (0, j))],
        out_specs=[ds(lambda i: i), pl.BlockSpec((CONV_K, w), lambda j, i: (0, j)),
                   pl.BlockSpec((1, w), lambda j, i: (0, j))],
        out_shape=[jax.ShapeDtypeStruct((s, 2 * ML_W), F32), jax.ShapeDtypeStruct((CONV_K, 2 * ML_W), F32),
                   jax.ShapeDtypeStruct((1, 2 * ML_W), F32)],
        compiler_params=_params("parallel", "arbitrary"),
    )(proj, proj, proj, dqk, dqk, conv_w, conv_b)


def _bf16_mm(dims_fwd):
    @jax.custom_vjp
    def mm(a, b):
        return _dot(a.astype(BF16), b.astype(BF16), dims_fwd)

    def fwd(a, b):
        return mm(a, b), (a, b)

    def bwd(res, g):
        a, b = res
        if dims_fwd is NN:
            return _mm_nt(g, b), _mm_tn(a, g)
        if dims_fwd is NT:
            return _mm_nn(g, b), _mm_tn(g, a)
        return _mm_nt(b, g), _mm_nn(a, g)

    mm.defvjp(fwd, bwd)
    return mm


_mm_nn = _bf16_mm(NN)
_mm_nt = _bf16_mm(NT)
_mm_tn = _bf16_mm(TN)


def _tri(lower):
    r = lax.broadcasted_iota(jnp.int32, (CHUNK, CHUNK), 0)
    c = lax.broadcasted_iota(jnp.int32, (CHUNK, CHUNK), 1)
    return ((r >= c) if lower else (r <= c)).astype(F32)


@jax.custom_vjp
def _cumsum_rows(x):
    return lax.dot_general(_tri(True), x, NN, precision=lax.Precision.HIGHEST, preferred_element_type=F32)


def _cumsum_fwd(x):
    return _cumsum_rows(x), None


def _cumsum_bwd(_, g):
    return (lax.dot_general(_tri(False), g, NN, precision=lax.Precision.HIGHEST, preferred_element_type=F32),)


_cumsum_rows.defvjp(_cumsum_fwd, _cumsum_bwd)


def _abs(x):
    return jnp.where(x >= 0, x, -x)


def _log_sigmoid(x):
    return jnp.minimum(x, 0.0) - jnp.log(1.0 + jnp.exp(-_abs(x)))


def _pick_col(x, lane):
    sel = lax.broadcasted_iota(jnp.int32, x.shape, 1) == lane
    return jnp.sum(jnp.where(sel, x, 0.0), axis=1, keepdims=True)


def _pick_row(x, r):
    sel = lax.broadcasted_iota(jnp.int32, x.shape, 0) == r
    return jnp.sum(jnp.where(sel, x, 0.0), axis=0, keepdims=True)


def _mlstm_chunk(qs, ks, vs, oms, gates, gate_bias, mlg, cs, ns, ms):
    gb = gates + gate_bias
    cum = _cumsum_rows(_log_sigmoid(gb))
    gbt = gb.T
    cumt = cum.T
    causal = lax.broadcasted_iota(jnp.int32, (CHUNK, CHUNK), 0) >= lax.broadcasted_iota(jnp.int32, (CHUNK, CHUNK), 1)
    ys, c_out, n_out, m_out = [], [], [], []
    for h in range(ML_HEADS):
        q, v, om, c, n, m = qs[h], vs[h], oms[h], cs[h], ns[h], ms[h]
        k = ks[h] * (ML_HD ** -0.5)
        ig_col = _pick_col(gb, h)
        ig_row = _pick_row(gbt, h)
        b_col = _pick_col(cum, ML_HEADS + h)
        b_row = _pick_row(cumt, ML_HEADS + h)
        g = _pick_row(b_col, CHUNK - 1)
        a = g - b_col + ig_col
        m_loc = jnp.max(a, axis=0, keepdims=True)
        wa = jnp.exp(a - m_loc)
        c_loc = _mm_tn(wa * v, k)
        n_loc = jnp.sum(wa * k, axis=0, keepdims=True)
        m_new = jnp.maximum(g + m, m_loc)
        sp = jnp.exp(g + m - m_new)
        sl = jnp.exp(m_loc - m_new)
        c_out.append(sp * c + sl * c_loc)
        n_out.append(sp * n + sl * n_loc)
        m_out.append(m_new)
        d_log = jnp.where(causal, b_col - b_row + ig_row, -jnp.inf)
        e_log = b_col + m
        m_t = jnp.maximum(e_log, jnp.max(d_log, axis=1, keepdims=True))
        d_w = jnp.exp(d_log - m_t)
        e_w = jnp.exp(e_log - m_t)
        s_qk = _mm_nt(q, k) * d_w
        num = e_w * _mm_nt(q, c) + _mm_nn(s_qk, v)
        den = e_w * jnp.sum(q * n, axis=1, keepdims=True) + jnp.sum(s_qk, axis=1, keepdims=True)
        hh = num / jnp.maximum(_abs(den), jnp.exp(-m_t))
        hg = _sigmoid(om) * hh
        mu = jnp.mean(hg, axis=1, keepdims=True)
        hc = hg - mu
        var = jnp.mean(hc * hc, axis=1, keepdims=True)
        ys.append(hc * lax.rsqrt(var + LN_EPS) * mlg[h])
    return ys, c_out, n_out, m_out


V_COL = 5
O_COL = 6


def _mlstm_fwd(qk, proj, gates, gate_bias, mlg, name="mlstm_fwd", gather=()):
    s = qk.shape[0]
    nc = s // CHUNK

    def body(q_ref, k_ref, v_ref, o_ref, g_ref, gb_ref, mlg_ref, y_ref, cp_ref, np_ref, mp_ref, c_s, n_s, m_s):
        ci = pl.program_id(0)

        @pl.when(ci == 0)
        def _():
            c_s[...] = jnp.zeros_like(c_s)
            n_s[...] = jnp.zeros_like(n_s)
            m_s[...] = jnp.zeros_like(m_s)

        cp_ref[...] = c_s[...]
        np_ref[...] = n_s[...]
        mp_ref[...] = m_s[...]
        hs = lambda ref: [ref[:, LANES * h:LANES * (h + 1)] for h in range(ML_HEADS)]
        ys, c_new, n_new, m_new = _mlstm_chunk(
            hs(q_ref), hs(k_ref), hs(v_ref), hs(o_ref), g_ref[...], gb_ref[...], hs(mlg_ref),
            [c_s[h] for h in range(ML_HEADS)], [n_s[h:h + 1, :] for h in range(ML_HEADS)],
            [m_s[h:h + 1, 0:1] for h in range(ML_HEADS)])
        for h in range(ML_HEADS):
            y_ref[:, LANES * h:LANES * (h + 1)] = ys[h]
            c_s[h] = c_new[h]
            n_s[h:h + 1, :] = n_new[h]
            m_s[h:h + 1, :] = jnp.broadcast_to(m_new[h], (1, LANES))

    blk = lambda col: pl.BlockSpec((CHUNK, ML_W), lambda ci: (ci, col))
    vec = lambda w: pl.BlockSpec((1, w), lambda ci: (0, 0))
    return _call(
        body, name=name, grid=(nc,), args=(qk, qk, proj, proj, gates, gate_bias, mlg), sem=("arbitrary",), gather=gather,
        in_specs=[blk(0), blk(1), blk(V_COL), blk(O_COL), pl.BlockSpec((CHUNK, LANES), lambda ci: (ci, 0)),
                  vec(LANES), vec(ML_W)],
        out_specs=[blk(0), pl.BlockSpec((None, ML_HEADS, ML_HD, ML_HD), lambda ci: (ci, 0, 0, 0)),
                   pl.BlockSpec((None, 8, LANES), lambda ci: (ci, 0, 0)),
                   pl.BlockSpec((None, 8, LANES), lambda ci: (ci, 0, 0))],
        out_shape=[jax.ShapeDtypeStruct((s, ML_W), F32), jax.ShapeDtypeStruct((nc, ML_HEADS, ML_HD, ML_HD), F32),
                   jax.ShapeDtypeStruct((nc, 8, LANES), F32), jax.ShapeDtypeStruct((nc, 8, LANES), F32)],
        scratch_shapes=[pltpu.VMEM((ML_HEADS, ML_HD, ML_HD), F32), pltpu.VMEM((8, LANES), F32),
                        pltpu.VMEM((8, LANES), F32)])


def _mlstm_bwd(qk, proj, gates, gate_bias, mlg, cprev, nprev, mprev, dy, name="mlstm_bwd", exchange=()):
    s = qk.shape[0]
    nc = s // CHUNK

    def body(q_ref, k_ref, v_ref, o_ref, g_ref, gb_ref, mlg_ref, cp_ref, np_ref, mp_ref, dy_ref,
             dqk_ref, dv_ref, do_ref, dg_ref, dgb_ref, dmlg_ref, dc_s, dn_s, dm_s, gb8, mg8):
        ci = pl.program_id(0)

        @pl.when(ci == 0)
        def _():
            dc_s[...] = jnp.zeros_like(dc_s)
            dn_s[...] = jnp.zeros_like(dn_s)
            dm_s[...] = jnp.zeros_like(dm_s)
            gb8[...] = jnp.zeros_like(gb8)
            mg8[...] = jnp.zeros_like(mg8)

        hs = lambda ref: [ref[:, LANES * h:LANES * (h + 1)] for h in range(ML_HEADS)]
        prim = (hs(q_ref), hs(k_ref), hs(v_ref), hs(o_ref), g_ref[...], gb_ref[...], hs(mlg_ref),
                [cp_ref[h] for h in range(ML_HEADS)], [np_ref[h:h + 1, :] for h in range(ML_HEADS)],
                [mp_ref[h:h + 1, 0:1] for h in range(ML_HEADS)])
        _, vjp = jax.vjp(_mlstm_chunk, *prim)
        cot = (hs(dy_ref), [dc_s[h] for h in range(ML_HEADS)], [dn_s[h:h + 1, :] for h in range(ML_HEADS)],
               [dm_s[h:h + 1, 0:1] for h in range(ML_HEADS)])
        dqs, dks, dvs, dos, dg, dgb, dmlg, dcs, dns, dms = vjp(cot)
        dg_ref[...] = dg
        gb8[0:1, :] += dgb
        for h in range(ML_HEADS):
            sl = slice(LANES * h, LANES * (h + 1))
            dqk_ref[:, sl] = dqs[h]
            dqk_ref[:, ML_W + LANES * h:ML_W + LANES * (h + 1)] = dks[h]
            dv_ref[:, sl] = dvs[h]
            do_ref[:, sl] = dos[h]
            mg8[0:1, sl] += dmlg[h]
            dc_s[h] = dcs[h]
            dn_s[h:h + 1, :] = dns[h]
            dm_s[h:h + 1, :] = jnp.broadcast_to(dms[h], (1, LANES))

        @pl.when(ci == nc - 1)
        def _():
            dgb_ref[...] = gb8[0:1, :]
            dmlg_ref[...] = mg8[0:1, :]

    rev = lambda ci: nc - 1 - ci
    blk = lambda col: pl.BlockSpec((CHUNK, ML_W), lambda ci: (rev(ci), col))
    vec = lambda w: pl.BlockSpec((1, w), lambda ci: (0, 0))
    st8 = pl.BlockSpec((None, 8, LANES), lambda ci: (rev(ci), 0, 0))
    gsp = pl.BlockSpec((CHUNK, LANES), lambda ci: (rev(ci), 0))
    return _call(
        body, name=name, grid=(nc,), sem=("arbitrary",), exchange=exchange,
        args=(qk, qk, proj, proj, gates, gate_bias, mlg, cprev, nprev, mprev, dy),
        in_specs=[blk(0), blk(1), blk(V_COL), blk(O_COL), gsp, vec(LANES), vec(ML_W),
                  pl.BlockSpec((None, ML_HEADS, ML_HD, ML_HD), lambda ci: (rev(ci), 0, 0, 0)), st8, st8, blk(1)],
        out_specs=[pl.BlockSpec((CHUNK, 2 * ML_W), lambda ci: (rev(ci), 0)), blk(0), blk(0), gsp, vec(LANES), vec(ML_W)],
        out_shape=[jax.ShapeDtypeStruct((s, 2 * ML_W), F32),
                   jax.ShapeDtypeStruct((s, ML_W), F32), jax.ShapeDtypeStruct((s, ML_W), F32),
                   jax.ShapeDtypeStruct((s, LANES), F32), jax.ShapeDtypeStruct((1, LANES), F32),
                   jax.ShapeDtypeStruct((1, ML_W), F32)],
        scratch_shapes=[pltpu.VMEM((ML_HEADS, ML_HD, ML_HD), F32), pltpu.VMEM((8, LANES), F32),
                        pltpu.VMEM((8, LANES), F32), pltpu.VMEM((8, LANES), F32), pltpu.VMEM((8, ML_W), F32)])


def _xattn_tile(qs, ks, vs):
    outs = []
    for q, k, v in zip(qs, ks, vs):
        sc = _mm_nt(q, k) * (XA_HD ** -0.5)
        mx = lax.stop_gradient(jnp.max(sc, axis=1, keepdims=True))
        pe = jnp.exp(sc - mx)
        outs.append(_mm_nn(pe / jnp.sum(pe, axis=1, keepdims=True), v))
    return outs


def _xa_heads(ref):
    return [ref[:, XA_HD * h:XA_HD * (h + 1)] for h in range(XA_HEADS)]


def _xattn_fwd(q, kv, name="xattn_fwd", tm=512):
    s, d = q.shape

    def body(q_ref, k_ref, v_ref, o_ref):
        outs = _xattn_tile(_xa_heads(q_ref), _xa_heads(k_ref), _xa_heads(v_ref))
        for h in range(XA_HEADS):
            o_ref[:, XA_HD * h:XA_HD * (h + 1)] = outs[h]

    row = pl.BlockSpec((tm, d), lambda i: (i, 0))
    return pl.pallas_call(
        body, name=name, grid=(s // tm,),
        in_specs=[row, pl.BlockSpec((MEM_LEN, d), lambda i: (0, 0)), pl.BlockSpec((MEM_LEN, d), lambda i: (0, 1))],
        out_specs=row, out_shape=jax.ShapeDtypeStruct((s, d), F32),
        compiler_params=_params("parallel"),
    )(q, kv, kv)


def _xattn_bwd(q, kv, do, name="xattn_bwd", tm=512):
    s, d = q.shape

    def body(q_ref, k_ref, v_ref, do_ref, dq_ref, dkv_ref):
        i = pl.program_id(0)
        _, vjp = jax.vjp(_xattn_tile, _xa_heads(q_ref), _xa_heads(k_ref), _xa_heads(v_ref))
        dqs, dks, dvs = vjp(_xa_heads(do_ref))

        @pl.when(i == 0)
        def _():
            dkv_ref[...] = jnp.zeros_like(dkv_ref)

        for h in range(XA_HEADS):
            sl = slice(XA_HD * h, XA_HD * (h + 1))
            dq_ref[:, sl] = dqs[h]
            dkv_ref[:, sl] += dks[h]
            dkv_ref[:, d + XA_HD * h:d + XA_HD * (h + 1)] += dvs[h]

    row = pl.BlockSpec((tm, d), lambda i: (i, 0))
    return pl.pallas_call(
        body, name=name, grid=(s // tm,),
        in_specs=[row, pl.BlockSpec((MEM_LEN, d), lambda i: (0, 0)), pl.BlockSpec((MEM_LEN, d), lambda i: (0, 1)), row],
        out_specs=[row, pl.BlockSpec((MEM_LEN, 2 * d), lambda i: (0, 0))],
        out_shape=[jax.ShapeDtypeStruct((s, d), F32), jax.ShapeDtypeStruct((MEM_LEN, 2 * d), F32)],
        compiler_params=_params("arbitrary"),
    )(q, kv, kv, do)


def _loss_head(y, target, name="loss_head", tm=512):
    s, d = y.shape
    nt = s // tm

    def body(y_ref, t_ref, dy_ref, loss_ref, acc):
        i = pl.program_id(0)
        err = y_ref[...] - t_ref[...]
        dy_ref[...] = err * (1.0 / d)

        @pl.when(i == 0)
        def _():
            acc[...] = jnp.zeros_like(acc)

        acc[...] += _rowsum8(err * err)

        @pl.when(i == nt - 1)
        def _():
            tot = jnp.sum(jnp.sum(acc[...], axis=0, keepdims=True), axis=1, keepdims=True)
            loss_ref[...] = jnp.broadcast_to(tot * (0.5 / d), (1, LANES))

    row = pl.BlockSpec((tm, d), lambda i: (i, 0))
    return pl.pallas_call(
        body, name=name, grid=(nt,),
        in_specs=[row, row], out_specs=[row, pl.BlockSpec((1, LANES), lambda i: (0, 0))],
        out_shape=[jax.ShapeDtypeStruct((s, d), F32), jax.ShapeDtypeStruct((1, LANES), F32)],
        scratch_shapes=[pltpu.VMEM((8, d), F32)],
        compiler_params=_params("arbitrary"),
    )(y, target)


def _adam2d(recv, w, m, v, name, layer=None):
    rows, cols = w.shape[-2:]
    fits = [t for t in range(16, rows + 1, 16) if rows % t == 0 and t * cols <= 128 * 1024]
    tr = max(fits) if fits else rows

    def body(r_ref, w_ref, m_ref, v_ref, g_ref, d_ref, mo_ref, vo_ref):
        g = r_ref[0].astype(F32)
        for j in range(1, N_DEV):
            g = g + r_ref[j].astype(F32)
        mn = ADAM_B1 * m_ref[...] + (1.0 - ADAM_B1) * g
        vn = ADAM_B2 * v_ref[...] + (1.0 - ADAM_B2) * jnp.square(g)
        m_hat = mn / (1.0 - ADAM_B1 ** ADAM_STEP)
        v_hat = vn / (1.0 - ADAM_B2 ** ADAM_STEP)
        g_ref[...] = g
        d_ref[...] = -ADAM_LR * (m_hat / (jnp.sqrt(v_hat) + ADAM_EPS) + ADAM_WD * w_ref[...])
        mo_ref[...] = mn
        vo_ref[...] = vn

    row = pl.BlockSpec((tr, cols), lambda i: (i, 0))
    if layer is None:
        wspec = row
    else:
        wspec = pl.BlockSpec((None, None, tr, cols), lambda i: (0, layer, i, 0))
    return pl.pallas_call(
        body, name=name, grid=(rows // tr,),
        in_specs=[pl.BlockSpec((N_DEV, tr, cols), lambda i: (0, i, 0)), wspec, wspec, wspec],
        out_specs=[row] * 4, out_shape=[jax.ShapeDtypeStruct((rows, cols), F32)] * 4,
        compiler_params=_params("parallel"),
    )(recv, w, m, v)


WEIGHTS = ("rel_bias", "ln_g", "ln_b", "ffn_w_gate", "ffn_w_up", "ffn_w_down", "w_in", "conv_w", "conv_b",
           "ig_bias", "fg_bias", "ml_norm_g", "w_out", "xq_w", "xkv_w", "xo_w")
SMALL = ("rel_bias", "ln_g", "ln_b", "conv_w", "conv_b", "ig_bias", "fg_bias", "ml_norm_g")
SMALL_SHAPES = {
    "rel_bias": (REL_BUCKETS, ATT_HEADS), "ln_g": (1, 4, LANES), "ln_b": (1, 4, LANES), "conv_w": (1, CONV_K, LANES),
    "conv_b": (1, 2 * ML_W), "ig_bias": (1, ML_HEADS), "fg_bias": (1, ML_HEADS), "ml_norm_g": (1, ML_W),
}
SMALL_ROWS = 8


def _pack_small(parts, lead=()):
    out = []
    for p in parts:
        p = jnp.pad(p, [(0, 0)] * len(lead) + [(0, SMALL_ROWS * LANES - p.shape[-1])])
        out.append(p.reshape(lead + (SMALL_ROWS, LANES)))
    return jnp.concatenate(out, axis=len(lead))


def _unpack_small(flat):
    out = {}
    for i, n in enumerate(SMALL):
        cnt = int(np.prod(SMALL_SHAPES[n]))
        out[n] = flat[SMALL_ROWS * i:SMALL_ROWS * (i + 1)].reshape(-1)[:cnt].reshape(SMALL_SHAPES[n])
    return out


def _split8(full, axis):
    shp = full.shape
    t = full.reshape(shp[:axis] + (N_DEV, shp[axis] // N_DEV) + shp[axis + 1:])
    return jnp.moveaxis(t, axis, 0).reshape(N_DEV, -1)


def _rep8(full):
    return jnp.broadcast_to(full.reshape(1, -1), (N_DEV, full.size))


def kernel(x, mem, rel_bias, ln_g, ln_b, ffn_w_gate, ffn_w_up, ffn_w_down, w_in, conv_w, conv_b, ig_bias, fg_bias, ml_norm_g, w_out, xq_w, xkv_w, xo_w, loss_target, m_rel_bias, m_ln_g, m_ln_b, m_ffn_w_gate, m_ffn_w_up, m_ffn_w_down, m_w_in, m_conv_w, m_conv_b, m_ig_bias, m_fg_bias, m_ml_norm_g, m_w_out, m_xq_w, m_xkv_w, m_xo_w, v_rel_bias, v_ln_g, v_ln_b, v_ffn_w_gate, v_ffn_w_up, v_ffn_w_down, v_w_in, v_conv_w, v_conv_b, v_ig_bias, v_fg_bias, v_ml_norm_g, v_w_out, v_xq_w, v_xkv_w, v_xo_w):
    w_tree = dict(rel_bias=rel_bias, ln_g=ln_g, ln_b=ln_b, ffn_w_gate=ffn_w_gate, ffn_w_up=ffn_w_up,
                  ffn_w_down=ffn_w_down, w_in=w_in, conv_w=conv_w, conv_b=conv_b, ig_bias=ig_bias, fg_bias=fg_bias,
                  ml_norm_g=ml_norm_g, w_out=w_out, xq_w=xq_w, xkv_w=xkv_w, xo_w=xo_w)
    m_tree = dict(rel_bias=m_rel_bias, ln_g=m_ln_g, ln_b=m_ln_b, ffn_w_gate=m_ffn_w_gate, ffn_w_up=m_ffn_w_up,
                  ffn_w_down=m_ffn_w_down, w_in=m_w_in, conv_w=m_conv_w, conv_b=m_conv_b, ig_bias=m_ig_bias,
                  fg_bias=m_fg_bias, ml_norm_g=m_ml_norm_g, w_out=m_w_out, xq_w=m_xq_w, xkv_w=m_xkv_w, xo_w=m_xo_w)
    v_tree = dict(rel_bias=v_rel_bias, ln_g=v_ln_g, ln_b=v_ln_b, ffn_w_gate=v_ffn_w_gate, ffn_w_up=v_ffn_w_up,
                  ffn_w_down=v_ffn_w_down, w_in=v_w_in, conv_w=v_conv_w, conv_b=v_conv_b, ig_bias=v_ig_bias,
                  fg_bias=v_fg_bias, ml_norm_g=v_ml_norm_g, w_out=v_w_out, xq_w=v_xq_w, xkv_w=v_xkv_w, xo_w=v_xo_w)
    x0 = x[0]
    pad_ff = FF_PAD - FF_SHARD
    bf = lambda t: t.astype(BF16)

    pad_rows = lambda t: jnp.pad(t, ((0, pad_ff), (0, 0)))
    ffn_shards = [(pad_rows(bf(ffn_w_gate[0, l]).T), pad_rows(bf(ffn_w_up[0, l]).T), pad_rows(bf(ffn_w_down[0, l])))
                  for l in range(2)]
    pairs = lambda t: t.reshape(N_PAIR, FF_PAIR, D_MODEL)
    w_in_shard = jnp.pad(bf(w_in[0]), ((0, 0), (0, ATT_W - W_IN_SHARD)))
    small_shard = jnp.concatenate([ln_g[0], ln_b[0], conv_w[0], jnp.zeros((4, LANES), F32)], axis=0)
    gate_bias = jnp.pad(jnp.concatenate([ig_bias, fg_bias], axis=1), ((0, 0), (0, LANES - 2 * ML_HEADS)))
    buckets = _bucket_tables()

    wg0, wu0, wd0, small_all = _gather_two_level("ffn1_weights_gather", ffn_shards[0] + (small_shard,))
    wg0, wu0, wd0 = pairs(wg0), pairs(wu0), pairs(wd0)
    unshard = lambda t: jnp.moveaxis(t, 0, 1).reshape(4, D_MODEL)
    ln_g_full, ln_b_full, conv_w_full = unshard(small_all[:, 0:4]), unshard(small_all[:, 4:8]), unshard(small_all[:, 8:12])
    lng = lambda i: ln_g_full[i:i + 1]
    lnb = lambda i: ln_b_full[i:i + 1]

    u0, x1, win_all, wout_all, xq_all, xo_all, xkv_all = _ffn_fwd(
        x0, wg0, wu0, wd0, lng(0), lnb(0), "ffn1_fwd",
        gather=(w_in_shard, bf(w_out[0]), bf(xq_w[0]), bf(xo_w[0]), bf(xkv_w[0])))
    w_in_full = jnp.moveaxis(win_all[:, :, :W_IN_SHARD], 0, 1).reshape(D_MODEL, W_IN)
    w_main = w_in_full[:, :W_IN_MAIN]
    w_gate_cols = jnp.pad(w_in_full[:, W_IN_MAIN:], ((0, 0), (0, LANES - 2 * ML_HEADS)))
    w_out_full = wout_all.reshape(D_MODEL, D_MODEL)
    xq_full = xq_all.reshape(D_MODEL, D_MODEL)
    xo_full = xo_all.reshape(D_MODEL, D_MODEL)

    proj, wg1 = _matmul(x1, w_main, "nn", "proj_fwd", tk=D_MODEL, gather=(ffn_shards[1][0],))
    gates, = _matmul(x1, w_gate_cols, "nn", "gates_fwd", tk=D_MODEL)
    biasm = _bias_fwd(rel_bias, buckets)
    att, lse, wd1 = _dil_fwd(proj, biasm, gather=(ffn_shards[1][2],))
    qk = _conv_fwd(proj, conv_w_full, conv_b)
    y_m, c_prev, n_prev, m_prev, wu1 = _mlstm_fwd(qk, proj, gates, gate_bias, ml_norm_g, gather=(ffn_shards[1][1],))
    cat = jnp.concatenate([att, y_m], axis=1)
    u1, x2 = _matmul_resid_ln(cat, w_out_full, x1, lng(1), lnb(1), "w_out_fwd")
    q_x, = _matmul(x2, xq_full, "nn", "xq_fwd", tn=D_MODEL, tk=D_MODEL)
    kv, = _matmul(mem[0], xkv_all, "nn", "xkv_fwd", tk=D_MODEL)
    o_x = _xattn_fwd(q_x, kv)
    u2, x3 = _matmul_resid_ln(o_x, xo_full, x2, lng(2), lnb(2), "xo_fwd")
    wg1, wu1, wd1 = pairs(wg1), pairs(wu1), pairs(wd1)
    u3, x4 = _ffn_fwd(x3, wg1, wu1, wd1, lng(3), lnb(3), "ffn2_fwd")
    dx4, loss_row = _loss_head(x4, loss_target[0])

    dx3, xb, df, da, db, hh, dg3, db3 = _ffn_bwd_x(dx4, u3, x3, wg1, wu1, wd1, lng(3), "ffn2_bwd_x")
    ffn2_send = _ffn_bwd_w(xb, df, da, db, hh, "ffn2_bwd_w")

    du2, dg2, db2 = _ln_bwd(dx3, u2, lng(2), "xattn_ln_bwd")
    do_x, = _matmul(du2, xo_full, "nt", "xo_bwd_x", tn=D_MODEL, tk=D_MODEL)
    g_xo, = _matmul(o_x, du2, "tn", "xo_bwd_w", tm=D_MODEL, tn=D_MODEL, out_dtype=BF16)
    dq_x, dkv = _xattn_bwd(q_x, kv, do_x)
    g_xq, = _matmul(x2, dq_x, "tn", "xq_bwd_w", tm=D_MODEL, tn=D_MODEL, out_dtype=BF16)
    g_xkv, = _matmul(mem[0], dkv, "tn", "xkv_bwd_w", tm=D_MODEL, tn=2 * D_MODEL // N_DEV, tk=MEM_LEN,
                     out_dtype=BF16, blocked_out=True)
    dx2, = _matmul(dq_x, xq_full, "nt", "xq_bwd_x", tn=D_MODEL, tk=D_MODEL, add=du2, add_scale=ALPHA)

    du1, dg1, db1 = _ln_bwd(dx2, u1, lng(1), "mixer_ln_bwd")
    dcat, = _matmul(du1, w_out_full, "nt", "w_out_bwd_x", tn=D_MODEL, tk=D_MODEL)
    g_w_out, = _matmul(cat, du1, "tn", "w_out_bwd_w", tm=D_MODEL, tn=D_MODEL, out_dtype=BF16)
    dqk, dv_m, do_m, dgates, dgate_bias, g_mlg, *ffn2_recv = _mlstm_bwd(
        qk, proj, gates, gate_bias, ml_norm_g, c_prev, n_prev, m_prev, dcat, exchange=tuple(ffn2_send))
    dqk_pre, g_conv_w, g_conv_b = _conv_bwd(proj, dqk, conv_w_full, conv_b)
    dq_a, dk_a, dv_a, dbias = _dil_bwd(proj, biasm, lse, att, dcat)
    g_rel = _bias_bwd(dbias, buckets)[:, :ATT_HEADS]
    dproj = jnp.concatenate([dq_a, dk_a, dv_a, bf(dqk_pre), bf(dv_m), bf(do_m)], axis=1)
    g_w_main, = _matmul(x1, dproj, "tn", "proj_bwd_w", tm=D_MODEL, tn=W_IN_MAIN // 2, out_dtype=BF16)
    g_w_gates, = _matmul(x1, dgates, "tn", "gates_bwd_w", tm=D_MODEL, out_dtype=BF16)
    g_w_in = jnp.concatenate([g_w_main, g_w_gates[:, :2 * ML_HEADS]], axis=1)
    dx1, = _matmul(dproj, w_main, "nt", "proj_bwd_x", tn=D_MODEL, add=du1, add_scale=ALPHA)
    dx1, = _matmul(dgates, w_gate_cols, "nt", "gates_bwd_x", tn=D_MODEL, add=dx1)

    rows8 = lambda t: t.reshape(N_DEV, D_MODEL // N_DEV, D_MODEL)
    mid_send = (rows8(g_xo), rows8(g_xq), g_xkv, rows8(g_w_out),
                jnp.moveaxis(g_w_in.reshape(D_MODEL, N_DEV, W_IN_SHARD), 1, 0))
    dx0, xb, df, da, db, hh, dg0, db0, r_xo, r_xq, r_xkv, r_w_out, r_w_in = _ffn_bwd_x(
        dx1, u0, x0, wg0, wu0, wd0, lng(0), "ffn1_bwd_x", exchange=mid_send)
    ffn1_send = _ffn_bwd_w(xb, df, da, db, hh, "ffn1_bwd_w")
    small_blocks = {
        "rel_bias": _rep8(g_rel),
        "ln_g": _split8(jnp.concatenate([dg0, dg1, dg2, dg3], axis=0), 1),
        "ln_b": _split8(jnp.concatenate([db0, db1, db2, db3], axis=0), 1),
        "conv_w": _split8(g_conv_w, 1),
        "conv_b": _rep8(g_conv_b),
        "ig_bias": _rep8(dgate_bias[:, :ML_HEADS]),
        "fg_bias": _rep8(dgate_bias[:, ML_HEADS:2 * ML_HEADS]),
        "ml_norm_g": _rep8(g_mlg),
    }
    small_send = _pack_small([small_blocks[n] for n in SMALL], lead=(N_DEV,))
    *ffn1_recv, r_small = _exchange_only("ffn1_grads_exchange", exchange=tuple(ffn1_send) + (small_send,))

    res = {}
    for i, n in enumerate(("ffn_w_gate", "ffn_w_up", "ffn_w_down")):
        per_layer = [_adam2d(r[i], w_tree[n], m_tree[n], v_tree[n], f"adamw_{n}_{l}", layer=l)
                     for l, r in enumerate((ffn1_recv, ffn2_recv))]
        res[n] = [jnp.stack([per_layer[0][j], per_layer[1][j]])[None] for j in range(4)]
    for n, r in (("w_in", r_w_in), ("w_out", r_w_out), ("xq_w", r_xq), ("xkv_w", r_xkv), ("xo_w", r_xo)):
        res[n] = [t[None] for t in _adam2d(r, w_tree[n][0], m_tree[n][0], v_tree[n][0], f"adamw_{n}")]
    pack = lambda tree: _pack_small([tree[n].reshape(-1) for n in SMALL])
    small = [_unpack_small(t) for t in _adam2d(r_small, pack(w_tree), pack(m_tree), pack(v_tree), "adamw_small")]
    for n in SMALL:
        res[n] = [small[j][n] for j in range(4)]

    loss = lax.psum(loss_row[0, 0], ("x", "y", "c"))
    return (loss, dx0[None], *[res[n][0] for n in WEIGHTS], *[res[n][1] for n in WEIGHTS],
            *[res[n][2] for n in WEIGHTS], *[res[n][3] for n in WEIGHTS])
```

```python
import functools
import math

import numpy as np
import jax
import jax.numpy as jnp
from jax import lax
from jax.experimental import pallas as pl
from jax.experimental.pallas import tpu as pltpu

F32 = jnp.float32
BF16 = jnp.bfloat16

N_DEV = 8
D_MODEL = 1024
D_FF = 2816
FF_SHARD = D_FF // N_DEV
FF_PAD = 384
ATT_W = 512
ATT_HEADS = 8
DILATED = ((128, 1), (512, 4), (2048, 16))
BLK = 128
ML_W = 512
ML_HEADS = 4
ML_HD = 128
CHUNK = 128
CONV_K = 4
W_IN = 3592
W_IN_SHARD = W_IN // N_DEV
W_IN_MAIN = 3584
XA_HEADS = 4
XA_HD = 256
MEM_LEN = 256
REL_BUCKETS = 32
REL_MAX_DIST = 2048
ALPHA = 2.0 ** 0.25
LN_EPS = 1e-5
NEG = -1e30
ADAM_LR = 0.001
ADAM_B1 = 0.9
ADAM_B2 = 0.999
ADAM_EPS = 1e-08
ADAM_WD = 0.01
ADAM_STEP = 10
LANES = 128
VMEM_LIMIT = 58 * 1024 * 1024

NN = (((1,), (0,)), ((), ()))
NT = (((1,), (1,)), ((), ()))
TN = (((0,), (0,)), ((), ()))


def _dot(a, b, dims):
    return lax.dot_general(a, b, dims, preferred_element_type=F32)


def _params(*sem):
    return pltpu.CompilerParams(dimension_semantics=sem, vmem_limit_bytes=VMEM_LIMIT)


def _sigmoid(x):
    return 1.0 / (1.0 + jnp.exp(-x))


def _rowsum8(x):
    t, c = x.shape
    return jnp.sum(x.reshape(t // 8, 8, c), axis=0)


def _mesh_pos():
    x, y, c = lax.axis_index("x"), lax.axis_index("y"), lax.axis_index("c")
    return x, y, c, 4 * x + 2 * y + c


def _peer(x, y, c, k):
    px = 1 - x if k & 4 else x
    py = 1 - y if k & 2 else y
    pc = 1 - c if k & 1 else c
    return (px, py, pc), 4 * px + 2 * py + pc


def _guarded(cond, fn):
    if cond is None:
        fn()
    else:
        pl.when(cond)(fn)


def _call(body, *, name, grid, in_specs, out_specs, out_shape, args, scratch_shapes=(), sem=None,
          gather=(), exchange=(), half=None):
    in_specs, out_specs, out_shape, scratch = list(in_specs), list(out_specs), list(out_shape), list(scratch_shapes)
    ng, nc = len(gather), len(gather) + len(exchange)
    if nc == 0:
        return pl.pallas_call(body, name=name, grid=grid, in_specs=in_specs, out_specs=out_specs,
                              out_shape=out_shape, scratch_shapes=scratch, compiler_params=_params(*sem))(*args)
    n_in, n_out, n_scr = len(in_specs), len(out_specs), len(scratch)

    def wrapped(*refs):
        ins, cin = refs[:n_in], refs[n_in:n_in + nc]
        outs, cout = refs[n_in + nc:n_in + nc + n_out], refs[n_in + nc + n_out:n_in + 2 * nc + n_out]
        scr = refs[n_in + 2 * nc + n_out:n_in + 2 * nc + n_out + n_scr]
        send_sems, recv_sems, loc_sems = refs[-3:]
        first, last = None, None
        for ax, extent in enumerate(grid):
            f, l = pl.program_id(ax) == 0, pl.program_id(ax) == extent - 1
            first = f if first is None else first & f
            last = l if last is None else last & l

        def copies():
            x, y, c, me = _mesh_pos()
            out = []
            for a in range(nc):
                halved = a >= ng and half is not None
                base = N_DEV // 2 * half if halved else 0
                i_own = (x == half) if halved else None
                block = (lambda idx, a=a, base=base: cin[a] if a < ng else cin[a].at[idx - base])
                out.append((pltpu.make_async_copy(block(me), cout[a].at[me], loc_sems.at[a]), False, i_own, i_own))
                for k in range(1, N_DEV):
                    peer, pidx = _peer(x, y, c, k)
                    cp = pltpu.make_async_remote_copy(
                        src_ref=block(pidx), dst_ref=cout[a].at[me],
                        send_sem=send_sems.at[a, k - 1], recv_sem=recv_sems.at[a, k - 1],
                        device_id=peer, device_id_type=pl.DeviceIdType.MESH)
                    out.append((cp, True, (peer[0] == half) if halved else None, i_own))
            return out

        @pl.when(first)
        def _():
            for cp, _, to_send, _ in copies():
                _guarded(to_send, cp.start)

        body(*ins, *outs, *scr)

        @pl.when(last)
        def _():
            for cp, remote, to_send, to_recv in copies():
                if remote:
                    _guarded(to_send, cp.wait_send)
                    _guarded(to_recv, cp.wait_recv)
                else:
                    _guarded(to_send, cp.wait)

    hbm = pl.BlockSpec(memory_space=pl.ANY)
    comm_shapes = [jax.ShapeDtypeStruct((N_DEV,) + a.shape, a.dtype) for a in gather]
    comm_shapes += [jax.ShapeDtypeStruct(((N_DEV,) if half is not None else a.shape[:1]) + a.shape[1:], a.dtype)
                    for a in exchange]
    return pl.pallas_call(
        wrapped, name=name, grid=grid, in_specs=in_specs + [hbm] * nc, out_specs=out_specs + [hbm] * nc,
        out_shape=out_shape + comm_shapes,
        scratch_shapes=scratch + [pltpu.SemaphoreType.DMA((nc, N_DEV - 1)), pltpu.SemaphoreType.DMA((nc, N_DEV - 1)),
                                  pltpu.SemaphoreType.DMA((nc,))],
        compiler_params=_params(*(("arbitrary",) * len(grid))),
    )(*args, *gather, *exchange)


def _gather_two_level(name, arrays):
    na = len(arrays)

    def body(*refs):
        srcs, outs = refs[:na], refs[na:2 * na]
        send_sems, recv_sems, loc_sems = refs[2 * na:]
        x, y, c, me = _mesh_pos()
        here, sib = (x, y, c), (x, y, 1 - c)
        chips = [(1 - x, y), (x, 1 - y), (1 - x, 1 - y)]
        pos = lambda px, py, pc: 4 * px + 2 * py + pc

        def copy(a, k, block, to, src=None):
            return pltpu.make_async_remote_copy(
                src_ref=outs[a].at[block] if src is None else src, dst_ref=outs[a].at[block],
                send_sem=send_sems.at[a, k], recv_sem=recv_sems.at[a, k], device_id=to,
                device_id_type=pl.DeviceIdType.MESH)

        locs = [pltpu.make_async_copy(srcs[a], outs[a].at[me], loc_sems.at[a]) for a in range(na)]
        for cp in locs:
            cp.start()
        first = []
        for a in range(na):
            first.append(copy(a, 0, me, sib, src=srcs[a]))
            first += [copy(a, 1 + j, me, (*chip, c), src=srcs[a]) for j, chip in enumerate(chips)]
        for cp in first:
            cp.start()
        passed = []
        for a in range(na):
            for j, chip in enumerate(chips):
                copy(a, 1 + j, pos(*chip, c), here).wait_recv()
                passed.append(copy(a, 4 + j, pos(*chip, c), sib))
                passed[-1].start()
        for a in range(na):
            copy(a, 0, pos(x, y, 1 - c), here).wait_recv()
            for j, chip in enumerate(chips):
                copy(a, 4 + j, pos(*chip, 1 - c), here).wait_recv()
        for cp in first + passed:
            cp.wait_send()
        for cp in locs:
            cp.wait()

    hbm = pl.BlockSpec(memory_space=pl.ANY)
    return pl.pallas_call(
        body, name=name, in_specs=[hbm] * na, out_specs=[hbm] * na,
        out_shape=[jax.ShapeDtypeStruct((N_DEV,) + a.shape, a.dtype) for a in arrays],
        scratch_shapes=[pltpu.SemaphoreType.DMA((na, N_DEV - 1)), pltpu.SemaphoreType.DMA((na, N_DEV - 1)),
                        pltpu.SemaphoreType.DMA((na,))],
    )(*arrays)


def _exchange_only(name, gather=(), exchange=(), half=None):
    return _call(lambda: None, name=name, grid=(1,), in_specs=[], out_specs=[], out_shape=[], args=(),
                 gather=gather, exchange=exchange, half=half)


def _matmul(a, b, mode, name, *, out_dtype=F32, tm=512, tn=512, tk=512, add=None, add_scale=1.0,
            blocked_out=False, gather=(), exchange=()):
    blocked_b = b.ndim == 3
    if blocked_b:
        (m, k), (nb, _, tn) = a.shape, b.shape
        n = nb * tn
    elif mode == "nn":
        (m, k), (_, n) = a.shape, b.shape
    elif mode == "nt":
        (m, k), (n, _) = a.shape, b.shape
    else:
        (k, m), (_, n) = a.shape, b.shape
    tm, tn, tk = min(tm, m), min(tn, n), min(tk, k)
    nk = k // tk
    dims = {"nn": NN, "nt": NT, "tn": TN}[mode]
    if mode == "tn":
        a_spec = pl.BlockSpec((tk, tm), lambda i, j, kk: (kk, i))
    else:
        a_spec = pl.BlockSpec((tm, tk), lambda i, j, kk: (i, kk))
    if blocked_b:
        b_spec = pl.BlockSpec((None, tk, tn), lambda i, j, kk: (j, kk, 0))
    elif mode == "nt":
        b_spec = pl.BlockSpec((tn, tk), lambda i, j, kk: (j, kk))
    else:
        b_spec = pl.BlockSpec((tk, tn), lambda i, j, kk: (kk, j))
    if blocked_out:
        o_spec = pl.BlockSpec((None, tm, tn), lambda i, j, kk: (j, i, 0))
        o_shape = jax.ShapeDtypeStruct((n // tn, m, tn), out_dtype)
    else:
        o_spec = pl.BlockSpec((tm, tn), lambda i, j, kk: (i, j))
        o_shape = jax.ShapeDtypeStruct((m, n), out_dtype)
    has_add = add is not None
    cache_a = nk == 1 and mode != "tn" and n // tn > 1 and a.dtype != BF16

    def body(*refs):
        if has_add:
            a_ref, b_ref, add_ref, o_ref, s_ref = refs
        else:
            a_ref, b_ref, o_ref, s_ref = refs
        kk = pl.program_id(2)
        if cache_a:
            @pl.when(pl.program_id(1) == 0)
            def _():
                s_ref[...] = a_ref[...].astype(BF16)

            lhs = s_ref[...]
        else:
            lhs = a_ref[...].astype(BF16)
        part = _dot(lhs, b_ref[...].astype(BF16), dims)

        def finish(r):
            if has_add:
                r = r + add_scale * add_ref[...]
            o_ref[...] = r.astype(out_dtype)

        if nk == 1:
            finish(part)
            return

        @pl.when(kk == 0)
        def _():
            s_ref[...] = part

        @pl.when(kk > 0)
        def _():
            s_ref[...] += part

        @pl.when(kk == nk - 1)
        def _():
            finish(s_ref[...])

    if nk > 1:
        scratch = [pltpu.VMEM((tm, tn), F32)]
    else:
        scratch = [pltpu.VMEM((tm, tk), BF16) if cache_a else pltpu.VMEM((8, LANES), F32)]
    return _call(
        body, name=name, grid=(m // tm, n // tn, nk),
        in_specs=[a_spec, b_spec] + ([pl.BlockSpec((tm, tn), lambda i, j, kk: (i, j))] if has_add else []),
        out_specs=[o_spec], out_shape=[o_shape], args=(a, b) + ((add,) if has_add else ()),
        scratch_shapes=scratch, sem=("parallel", "arbitrary", "arbitrary"),
        gather=gather, exchange=exchange)


def _ln_fwd_math(u, g, b):
    mu = jnp.mean(u, axis=-1, keepdims=True)
    uc = u - mu
    var = jnp.mean(uc * uc, axis=-1, keepdims=True)
    return uc * lax.rsqrt(var + LN_EPS) * g + b


def _ln_bwd_math(dy, u, g):
    mu = jnp.mean(u, axis=-1, keepdims=True)
    uc = u - mu
    var = jnp.mean(uc * uc, axis=-1, keepdims=True)
    rstd = lax.rsqrt(var + LN_EPS)
    xhat = uc * rstd
    dxh = dy * g
    m1 = jnp.mean(dxh, axis=-1, keepdims=True)
    m2 = jnp.mean(dxh * xhat, axis=-1, keepdims=True)
    return rstd * (dxh - m1 - xhat * m2), xhat


def _matmul_resid_ln(a, w, x, g, b, name, tm=512):
    s, k = a.shape
    d = w.shape[1]

    def body(a_ref, w_ref, x_ref, g_ref, b_ref, u_ref, y_ref):
        u = ALPHA * x_ref[...] + _dot(a_ref[...].astype(BF16), w_ref[...], NN)
        u_ref[...] = u
        y_ref[...] = _ln_fwd_math(u, g_ref[...], b_ref[...])

    row = pl.BlockSpec((tm, d), lambda i: (i, 0))
    vec = pl.BlockSpec((1, d), lambda i: (0, 0))
    return pl.pallas_call(
        body, name=name, grid=(s // tm,),
        in_specs=[pl.BlockSpec((tm, k), lambda i: (i, 0)), pl.BlockSpec((k, d), lambda i: (0, 0)), row, vec, vec],
        out_specs=[row, row], out_shape=[jax.ShapeDtypeStruct((s, d), F32)] * 2,
        compiler_params=_params("parallel"),
    )(a, w, x, g, b)


def _ln_bwd(dy, u, g, name, tm=512):
    s, d = dy.shape
    nt = s // tm

    def body(dy_ref, u_ref, g_ref, du_ref, dg_ref, db_ref, g8, b8):
        i = pl.program_id(0)
        dy_ = dy_ref[...]
        du, xhat = _ln_bwd_math(dy_, u_ref[...], g_ref[...])
        du_ref[...] = du

        @pl.when(i == 0)
        def _():
            g8[...] = jnp.zeros_like(g8)
            b8[...] = jnp.zeros_like(b8)

        g8[...] += _rowsum8(dy_ * xhat)
        b8[...] += _rowsum8(dy_)

        @pl.when(i == nt - 1)
        def _():
            dg_ref[...] = jnp.sum(g8[...], axis=0, keepdims=True)
            db_ref[...] = jnp.sum(b8[...], axis=0, keepdims=True)

    row = pl.BlockSpec((tm, d), lambda i: (i, 0))
    vec = pl.BlockSpec((1, d), lambda i: (0, 0))
    return pl.pallas_call(
        body, name=name, grid=(nt,),
        in_specs=[row, row, vec], out_specs=[row, vec, vec],
        out_shape=[jax.ShapeDtypeStruct((s, d), F32), jax.ShapeDtypeStruct((1, d), F32),
                   jax.ShapeDtypeStruct((1, d), F32)],
        scratch_shapes=[pltpu.VMEM((8, d), F32), pltpu.VMEM((8, d), F32)],
        compiler_params=_params("arbitrary"),
    )(dy, u, g)


FF_PAIR = 2 * FF_PAD
N_PAIR = N_DEV // 2


def _ffn_fwd(x, wgt, wut, wd, g, b, name, tm=512, gather=()):
    s, d = x.shape

    def body(x_ref, wg_ref, wu_ref, wd_ref, g_ref, b_ref, u_ref, y_ref, xb, acc):
        k = pl.program_id(1)

        @pl.when(k == 0)
        def _():
            xb[...] = x_ref[...].astype(BF16)

        a = _dot(xb[...], wg_ref[...], NT)
        bb = _dot(xb[...], wu_ref[...], NT)
        h = (a * _sigmoid(a) * bb).astype(BF16)
        part = _dot(h, wd_ref[...], NN)

        @pl.when(k == 0)
        def _():
            acc[...] = part

        @pl.when(k > 0)
        def _():
            acc[...] += part

        @pl.when(k == N_PAIR - 1)
        def _():
            u = ALPHA * x_ref[...] + 0.5 * acc[...]
            u_ref[...] = u
            y_ref[...] = _ln_fwd_math(u, g_ref[...], b_ref[...])

    row = pl.BlockSpec((tm, d), lambda i, k: (i, 0))
    vec = pl.BlockSpec((1, d), lambda i, k: (0, 0))
    w_in = pl.BlockSpec((None, FF_PAIR, d), lambda i, k: (k, 0, 0))
    w_dn = w_in
    return _call(
        body, name=name, grid=(s // tm, N_PAIR),
        in_specs=[row, w_in, w_in, w_dn, vec, vec], out_specs=[row, row],
        out_shape=[jax.ShapeDtypeStruct((s, d), F32)] * 2, args=(x, wgt, wut, wd, g, b),
        scratch_shapes=[pltpu.VMEM((tm, d), BF16), pltpu.VMEM((tm, d), F32)],
        sem=("parallel", "arbitrary"), gather=gather)


def _ffn_bwd_x(dy, u, x, wgt, wut, wd, g, name, tm=512, exchange=()):
    s, d = x.shape
    nt = s // tm
    ffp = N_DEV * FF_PAD

    def body(dy_ref, u_ref, x_ref, wg_ref, wu_ref, wd_ref, g_ref,
             dx_ref, xb, df_ref, da_ref, db_ref, h_ref, dg_ref, dbl_ref,
             dfb, du_s, acc, g8, b8):
        i = pl.program_id(0)
        k = pl.program_id(1)

        @pl.when(k == 0)
        def _():
            dy_ = dy_ref[...]
            du, xhat = _ln_bwd_math(dy_, u_ref[...], g_ref[...])
            du_s[...] = du
            dfb[...] = (0.5 * du).astype(BF16)
            df_ref[...] = dfb[...]
            xb[...] = x_ref[...].astype(BF16)

            @pl.when(i == 0)
            def _():
                g8[...] = jnp.zeros_like(g8)
                b8[...] = jnp.zeros_like(b8)

            g8[...] += _rowsum8(dy_ * xhat)
            b8[...] += _rowsum8(dy_)

        a = _dot(xb[...], wg_ref[...], NT)
        bb = _dot(xb[...], wu_ref[...], NT)
        sig = _sigmoid(a)
        sa = a * sig
        h_ref[...] = (sa * bb).astype(BF16)
        dh = _dot(dfb[...], wd_ref[...], NT)
        da = (dh * bb * (sig * (1.0 + a * (1.0 - sig)))).astype(BF16)
        db = (dh * sa).astype(BF16)
        da_ref[...] = da
        db_ref[...] = db
        part = _dot(da, wg_ref[...], NN) + _dot(db, wu_ref[...], NN)

        @pl.when(k == 0)
        def _():
            acc[...] = part

        @pl.when(k > 0)
        def _():
            acc[...] += part

        @pl.when(k == N_PAIR - 1)
        def _():
            dx_ref[...] = ALPHA * du_s[...] + acc[...]

        @pl.when((k == N_PAIR - 1) & (i == nt - 1))
        def _():
            dg_ref[...] = jnp.sum(g8[...], axis=0, keepdims=True)
            dbl_ref[...] = jnp.sum(b8[...], axis=0, keepdims=True)

    row = pl.BlockSpec((tm, d), lambda i, k: (i, 0))
    vec = pl.BlockSpec((1, d), lambda i, k: (0, 0))
    w_in = pl.BlockSpec((None, FF_PAIR, d), lambda i, k: (k, 0, 0))
    hid = pl.BlockSpec((tm, FF_PAIR), lambda i, k: (i, k))
    return _call(
        body, name=name, grid=(nt, N_PAIR),
        in_specs=[row, row, row, w_in, w_in, w_in, vec],
        out_specs=[row, row, row, hid, hid, hid, vec, vec],
        out_shape=[jax.ShapeDtypeStruct((s, d), F32), jax.ShapeDtypeStruct((s, d), BF16),
                   jax.ShapeDtypeStruct((s, d), BF16),
                   jax.ShapeDtypeStruct((s, ffp), BF16), jax.ShapeDtypeStruct((s, ffp), BF16),
                   jax.ShapeDtypeStruct((s, ffp), BF16),
                   jax.ShapeDtypeStruct((1, d), F32), jax.ShapeDtypeStruct((1, d), F32)],
        args=(dy, u, x, wgt, wut, wd, g),
        scratch_shapes=[pltpu.VMEM((tm, d), BF16), pltpu.VMEM((tm, d), F32),
                        pltpu.VMEM((tm, d), F32), pltpu.VMEM((8, d), F32), pltpu.VMEM((8, d), F32)],
        sem=("arbitrary", "arbitrary"), exchange=exchange)


def _ffn_bwd_w(xb, df, da, db, h, name, tm=512, pairs=(0, N_PAIR), exchange=(), half=None):
    s, d = xb.shape
    nt = s // tm
    p0, npair = pairs[0], pairs[1] - pairs[0]

    def body(x_ref, df_ref, da_ref, db_ref, h_ref, dwg_ref, dwu_ref, dwd_ref, ag, au, ad):
        i = pl.program_id(1)
        pg = _dot(x_ref[...], da_ref[...], TN)
        pu = _dot(x_ref[...], db_ref[...], TN)
        pd = _dot(h_ref[...], df_ref[...], TN)

        @pl.when(i == 0)
        def _():
            ag[...] = pg
            au[...] = pu
            ad[...] = pd

        @pl.when(i > 0)
        def _():
            ag[...] += pg
            au[...] += pu
            ad[...] += pd

        @pl.when(i == nt - 1)
        def _():
            for j in range(2):
                lo = j * FF_PAD
                dwg_ref[j] = ag[:, lo:lo + FF_SHARD].astype(BF16)
                dwu_ref[j] = au[:, lo:lo + FF_SHARD].astype(BF16)
                dwd_ref[j] = ad[lo:lo + FF_SHARD, :].astype(BF16)

    row = pl.BlockSpec((tm, d), lambda k, i: (i, 0))
    hid = pl.BlockSpec((tm, FF_PAIR), lambda k, i: (i, p0 + k))
    w_in = pl.BlockSpec((2, d, FF_SHARD), lambda k, i: (k, 0, 0))
    w_dn = pl.BlockSpec((2, FF_SHARD, d), lambda k, i: (k, 0, 0))
    return _call(
        body, name=name, grid=(npair, nt),
        in_specs=[row, row, hid, hid, hid], out_specs=[w_in, w_in, w_dn],
        out_shape=[jax.ShapeDtypeStruct((2 * npair, d, FF_SHARD), BF16), jax.ShapeDtypeStruct((2 * npair, d, FF_SHARD), BF16),
                   jax.ShapeDtypeStruct((2 * npair, FF_SHARD, d), BF16)],
        args=(xb, df, da, db, h),
        scratch_shapes=[pltpu.VMEM((d, FF_PAIR), F32), pltpu.VMEM((d, FF_PAIR), F32), pltpu.VMEM((FF_PAIR, d), F32)],
        sem=("parallel", "arbitrary"), exchange=exchange, half=half)


def _bucket_tables():
    qi = np.arange(BLK)[:, None]
    ki = np.arange(2 * BLK)[None, :]
    off = qi + BLK - ki
    out = []
    for window, dil in DILATED:
        n_keys = window // dil
        dist = dil * np.clip(off, 0, n_keys)
        exact = REL_BUCKETS // 2
        df = np.maximum(dist, 1).astype(np.float32)
        large = exact + (np.log(df / np.float32(exact)) / np.float32(math.log(REL_MAX_DIST / exact))
                         * np.float32(REL_BUCKETS - exact)).astype(np.int32)
        large = np.minimum(large, REL_BUCKETS - 1)
        bucket = np.where(dist < exact, dist, large).astype(np.int32)
        band = (off >= 0) & (off <= n_keys)
        out.append(np.where(band, bucket, -1))
    return np.stack(out).astype(np.int32)


def _bias_fwd(rel_bias, buckets, name="bias_fwd"):
    def body(tbl_ref, bkt_ref, out_ref):
        bkt = bkt_ref[...]
        for h in range(ATT_HEADS):
            acc = jnp.full((BLK, 2 * BLK), NEG, F32)
            for bb in range(REL_BUCKETS):
                acc = jnp.where(bkt == bb, tbl_ref[bb, h], acc)
            out_ref[h] = acc

    nbr = len(DILATED)
    return pl.pallas_call(
        body, name=name, grid=(nbr,),
        in_specs=[pl.BlockSpec(memory_space=pltpu.SMEM),
                  pl.BlockSpec((None, BLK, 2 * BLK), lambda r: (r, 0, 0))],
        out_specs=pl.BlockSpec((None, ATT_HEADS, BLK, 2 * BLK), lambda r: (r, 0, 0, 0)),
        out_shape=jax.ShapeDtypeStruct((nbr, ATT_HEADS, BLK, 2 * BLK), F32),
        compiler_params=_params("parallel"),
    )(rel_bias, buckets)


def _bias_bwd(dbias, buckets, name="bias_bwd"):
    nbr = len(DILATED)

    def body(db_ref, bkt_ref, out_ref):
        r = pl.program_id(0)

        @pl.when(r == 0)
        def _():
            out_ref[...] = jnp.zeros_like(out_ref)

        bkt = bkt_ref[...]
        rowi = lax.broadcasted_iota(jnp.int32, (REL_BUCKETS, LANES), 0)
        coli = lax.broadcasted_iota(jnp.int32, (REL_BUCKETS, LANES), 1)
        acc = jnp.zeros((REL_BUCKETS, LANES), F32)
        for h in range(ATT_HEADS):
            x = db_ref[h]
            for bb in range(REL_BUCKETS):
                part = jnp.sum(jnp.where(bkt == bb, x, 0.0), axis=0, keepdims=True)
                tot = jnp.sum(part, axis=1, keepdims=True)
                acc = acc + jnp.where((rowi == bb) & (coli == h), tot, 0.0)
        out_ref[...] += acc

    return pl.pallas_call(
        body, name=name, grid=(nbr,),
        in_specs=[pl.BlockSpec((None, ATT_HEADS, BLK, 2 * BLK), lambda r: (r, 0, 0, 0)),
                  pl.BlockSpec((None, BLK, 2 * BLK), lambda r: (r, 0, 0))],
        out_specs=pl.BlockSpec((REL_BUCKETS, LANES), lambda r: (0, 0)),
        out_shape=jax.ShapeDtypeStruct((REL_BUCKETS, LANES), F32),
        compiler_params=_params("arbitrary"),
    )(dbias, buckets)


def _stack_heads(pair, lo):
    return jnp.concatenate([jnp.where(lo, pair, 0.0), jnp.where(lo, 0.0, pair)], axis=0)


def _head_cols(pair, lo, reduce):
    fill = -jnp.inf if reduce is jnp.max else 0.0
    return jnp.concatenate([reduce(jnp.where(lo, pair, fill), axis=1, keepdims=True),
                            reduce(jnp.where(lo, fill, pair), axis=1, keepdims=True)], axis=0)


def _unstack_heads(x2, lo):
    return jnp.where(lo, x2[:BLK], x2[BLK:])


def _att_scores(q2, kk, bias2, first_ok):
    sc = _dot(q2, kk, NT) * (64 ** -0.5) + bias2
    return jnp.where(first_ok, sc, NEG)


DIL_TILE = 2048
DIL_COLS = ATT_W // LANES
DIL_UNROLL = 4


def _dil_rows(dil, n, r, base=0):
    start = base + n * (BLK * dil) + r
    return pl.ds(start, BLK, stride=dil) if dil > 1 else pl.ds(start, BLK)


def _dil_in_specs(tile_of):
    cur = lambda col: pl.BlockSpec((DIL_TILE, LANES), lambda p, i: (tile_of(i), col * DIL_COLS + p))
    prev = lambda col: pl.BlockSpec((DIL_TILE, LANES), lambda p, i: (jnp.maximum(tile_of(i) - 1, 0), col * DIL_COLS + p))
    bias = pl.BlockSpec((len(DILATED), None, 2 * BLK, 2 * BLK), lambda p, i: (0, p, 0, 0))
    return [cur(0), prev(1), cur(1), prev(2), cur(2), bias]


def _pair_bias(biasm):
    return biasm.reshape(len(DILATED), DIL_COLS, 2 * BLK, 2 * BLK)


def _dil_fwd(proj, biasm, name="dil_fwd", gather=()):
    s = proj.shape[0]
    nt = s // DIL_TILE
    tt = DIL_TILE

    def body(q_ref, kp_ref, kc_ref, vp_ref, vc_ref, bias_ref, att_ref, lse_ref, k2, v2, ob, lb):
        t = pl.program_id(1)
        k2[0:tt, :] = kp_ref[...]
        k2[tt:2 * tt, :] = kc_ref[...]
        v2[0:tt, :] = vp_ref[...]
        v2[tt:2 * tt, :] = vc_ref[...]
        lo = lax.broadcasted_iota(jnp.int32, (BLK, LANES), 1) < 64
        kidx = lax.broadcasted_iota(jnp.int32, (2 * BLK, 2 * BLK), 1)
        for b, (_, dil) in enumerate(DILATED):
            nblk = tt // (BLK * dil)

            def step(j, carry, b=b, dil=dil, nblk=nblk):
                r, n = j % dil, j // dil
                cur, prev = _dil_rows(dil, n, r, tt), _dil_rows(dil, n - 1, r, tt)
                here = _dil_rows(dil, n, r)
                q_pair = q_ref[here, :]
                kk = jnp.concatenate([k2[prev, :], k2[cur, :]], axis=0).astype(BF16)
                vv = jnp.concatenate([v2[prev, :], v2[cur, :]], axis=0).astype(BF16)
                first_ok = (t > 0) | (n > 0) | (kidx >= BLK)
                sc = _att_scores(_stack_heads(q_pair, lo).astype(BF16), kk, bias_ref[b], first_ok)
                mx = jnp.max(sc, axis=1, keepdims=True)
                pe = jnp.exp(sc - mx)
                l = jnp.sum(pe, axis=1, keepdims=True)
                ob.at[b][here, :] = _unstack_heads(_dot(pe.astype(BF16), vv, NN) / l, lo)
                lb.at[b][here, :] = _unstack_heads(jnp.broadcast_to(mx + jnp.log(l), (2 * BLK, LANES)), lo)
                return carry

            lax.fori_loop(0, tt // BLK, step, 0, unroll=DIL_UNROLL)
        l0, l1, l2 = lb[0], lb[1], lb[2]
        mx = jnp.maximum(jnp.maximum(l0, l1), l2)
        e0, e1, e2 = jnp.exp(l0 - mx), jnp.exp(l1 - mx), jnp.exp(l2 - mx)
        tot = e0 + e1 + e2
        att_ref[...] = (e0 * ob[0] + e1 * ob[1] + e2 * ob[2]) / tot
        lse_ref[...] = mx + jnp.log(tot)

    out = pl.BlockSpec((tt, LANES), lambda p, i: (i, p))
    return _call(
        body, name=name, grid=(DIL_COLS, nt), in_specs=_dil_in_specs(lambda i: i), out_specs=[out, out],
        out_shape=[jax.ShapeDtypeStruct((s, ATT_W), F32)] * 2, args=(proj, proj, proj, proj, proj, _pair_bias(biasm)),
        scratch_shapes=[pltpu.VMEM((2 * tt, LANES), F32), pltpu.VMEM((2 * tt, LANES), F32),
                        pltpu.VMEM((len(DILATED), tt, LANES), F32), pltpu.VMEM((len(DILATED), tt, LANES), F32)],
        sem=("parallel", "parallel"), gather=gather)


def _dil_bwd(proj, biasm, lse, att, dcat, name="dil_bwd"):
    s = proj.shape[0]
    nt = s // DIL_TILE
    tt = DIL_TILE
    nbr = len(DILATED)

    def body(q_ref, kp_ref, kc_ref, vp_ref, vc_ref, bias_ref, lse_ref, att_ref, datt_ref,
             dq_ref, dk_ref, dv_ref, dbias_ref, k2, v2, dqa, dka, dva, kcar, vcar):
        i = pl.program_id(1)
        t = nt - 1 - i
        k2[0:tt, :] = kp_ref[...]
        k2[tt:2 * tt, :] = kc_ref[...]
        v2[0:tt, :] = vp_ref[...]
        v2[tt:2 * tt, :] = vc_ref[...]

        @pl.when(i == 0)
        def _():
            kcar[...] = jnp.zeros_like(kcar)
            vcar[...] = jnp.zeros_like(vcar)
            dbias_ref[...] = jnp.zeros_like(dbias_ref)

        dqa[...] = jnp.zeros_like(dqa)
        dka[0:tt, :] = jnp.zeros((tt, LANES), F32)
        dva[0:tt, :] = jnp.zeros((tt, LANES), F32)
        dka[tt:2 * tt, :] = kcar[...]
        dva[tt:2 * tt, :] = vcar[...]
        lo = lax.broadcasted_iota(jnp.int32, (BLK, LANES), 1) < 64
        kidx = lax.broadcasted_iota(jnp.int32, (2 * BLK, 2 * BLK), 1)
        for b, (_, dil) in enumerate(DILATED):
            nblk = tt // (BLK * dil)

            def step(j, carry, b=b, dil=dil, nblk=nblk):
                r, n = j % dil, j // dil
                cur, prev = _dil_rows(dil, n, r, tt), _dil_rows(dil, n - 1, r, tt)
                here = _dil_rows(dil, n, r)
                q_pair = q_ref[here, :]
                kk = jnp.concatenate([k2[prev, :], k2[cur, :]], axis=0).astype(BF16)
                vv = jnp.concatenate([v2[prev, :], v2[cur, :]], axis=0).astype(BF16)
                first_ok = (t > 0) | (n > 0) | (kidx >= BLK)
                dat_pair = datt_ref[here, :]
                q2 = _stack_heads(q_pair, lo).astype(BF16)
                dom = _stack_heads(dat_pair, lo).astype(BF16)
                sc = _att_scores(q2, kk, bias_ref[b], first_ok)
                pr = jnp.exp(sc - _head_cols(lse_ref[here, :], lo, jnp.max))
                ds = pr * (_dot(dom, vv, NT) - _head_cols(dat_pair * att_ref[here, :], lo, jnp.sum))
                dbias_ref[b] += ds
                dsb = (ds * (64 ** -0.5)).astype(BF16)
                dk2 = _dot(dsb, q2, TN)
                dv2 = _dot(pr.astype(BF16), dom, TN)
                dqa[here, :] += _unstack_heads(_dot(dsb, kk, NN), lo)
                dka[prev, :] += dk2[:BLK]
                dka[cur, :] += dk2[BLK:]
                dva[prev, :] += dv2[:BLK]
                dva[cur, :] += dv2[BLK:]
                return carry

            lax.fori_loop(0, tt // BLK, step, 0, unroll=DIL_UNROLL)
        dq_ref[...] = dqa[...].astype(BF16)
        dk_ref[...] = dka[tt:2 * tt, :].astype(BF16)
        dv_ref[...] = dva[tt:2 * tt, :].astype(BF16)
        kcar[...] = dka[0:tt, :]
        vcar[...] = dva[0:tt, :]

    rev = lambda i: nt - 1 - i
    out = pl.BlockSpec((tt, LANES), lambda p, i: (rev(i), p))
    two = lambda: pltpu.VMEM((2 * tt, LANES), F32)
    one = lambda: pltpu.VMEM((tt, LANES), F32)
    return pl.pallas_call(
        body, name=name, grid=(DIL_COLS, nt),
        in_specs=_dil_in_specs(rev) + [out, out, out],
        out_specs=[out, out, out, pl.BlockSpec((nbr, None, 2 * BLK, 2 * BLK), lambda p, i: (0, p, 0, 0))],
        out_shape=[jax.ShapeDtypeStruct((s, ATT_W), BF16)] * 3
        + [jax.ShapeDtypeStruct((nbr, DIL_COLS, 2 * BLK, 2 * BLK), F32)],
        scratch_shapes=[two(), two(), one(), two(), two(), one(), one()],
        compiler_params=_params("arbitrary", "arbitrary"),
    )(proj, proj, proj, proj, proj, _pair_bias(biasm), lse, att, dcat)


QK_COL0 = (3 * ATT_W) // ATT_W


def _conv_shifted(prev, cur, j, row):
    sh = CONV_K - 1 - j
    if sh == 0:
        return cur
    return jnp.where(row < sh, pltpu.roll(prev, sh, 0), pltpu.roll(cur, sh, 0))


def _conv_z(prev, cur, w_ref, b_ref, row):
    z = b_ref[...] + cur * w_ref[CONV_K - 1:CONV_K, :]
    for j in range(CONV_K - 1):
        z = z + _conv_shifted(prev, cur, j, row) * w_ref[j:j + 1, :]
    return z


def _conv_fwd(proj, conv_w, conv_b, name="conv_fwd", tm=512):
    s = proj.shape[0]
    w = ATT_W

    def body(prev_ref, cur_ref, w_ref, b_ref, o_ref):
        i = pl.program_id(1)
        row = lax.broadcasted_iota(jnp.int32, (tm, w), 0)
        prev = jnp.where(i > 0, prev_ref[...], 0.0)
        z = _conv_z(prev, cur_ref[...], w_ref, b_ref, row)
        o_ref[...] = z * _sigmoid(z)

    return pl.pallas_call(
        body, name=name, grid=(2, s // tm),
        in_specs=[pl.BlockSpec((tm, w), lambda j, i: (jnp.maximum(i - 1, 0), QK_COL0 + j)),
                  pl.BlockSpec((tm, w), lambda j, i: (i, QK_COL0 + j)),
                  pl.BlockSpec((CONV_K, w), lambda j, i: (0, j)),
                  pl.BlockSpec((1, w), lambda j, i: (0, j))],
        out_specs=pl.BlockSpec((tm, w), lambda j, i: (i, j)),
        out_shape=jax.ShapeDtypeStruct((s, 2 * ML_W), F32),
        compiler_params=_params("parallel", "parallel"),
    )(proj, proj, conv_w, conv_b)


def _conv_bwd(proj, dqk, conv_w, conv_b, name="conv_bwd", tm=512):
    s = proj.shape[0]
    w = ATT_W
    nt = s // tm

    def body(xp_ref, xc_ref, xn_ref, dc_ref, dn_ref, w_ref, b_ref, dx_ref, dw_ref, db_ref):
        i = pl.program_id(1)
        row = lax.broadcasted_iota(jnp.int32, (tm, w), 0)
        prev = jnp.where(i > 0, xp_ref[...], 0.0)
        cur = xc_ref[...]

        def dz_of(pv, cv, dy):
            z = _conv_z(pv, cv, w_ref, b_ref, row)
            sig = _sigmoid(z)
            return dy * (sig * (1.0 + z * (1.0 - sig)))

        dzc = dz_of(prev, cur, dc_ref[...])
        dzn = jnp.where(i < nt - 1, dz_of(cur, xn_ref[...], dn_ref[...]), 0.0)
        dx = dzc * w_ref[CONV_K - 1:CONV_K, :]
        for j in range(CONV_K - 1):
            sh = CONV_K - 1 - j
            up = jnp.where(row >= tm - sh, pltpu.roll(dzn, tm - sh, 0), pltpu.roll(dzc, tm - sh, 0))
            dx = dx + up * w_ref[j:j + 1, :]
        dx_ref[...] = dx

        @pl.when(i == 0)
        def _():
            dw_ref[...] = jnp.zeros_like(dw_ref)
            db_ref[...] = jnp.zeros_like(db_ref)

        for j in range(CONV_K):
            dw_ref[j:j + 1, :] += jnp.sum(dzc * _conv_shifted(prev, cur, j, row), axis=0, keepdims=True)
        db_ref[...] += jnp.sum(dzc, axis=0, keepdims=True)

    xs = lambda f: pl.BlockSpec((tm, w), lambda j, i: (f(i), QK_COL0 + j))
    ds = lambda f: pl.BlockSpec((tm, w), lambda j, i: (f(i), j))
    return pl.pallas_call(
        body, name=name, grid=(2, nt),
        in_specs=[xs(lambda i: jnp.maximum(i - 1, 0)), xs(lambda i: i), xs(lambda i: jnp.minimum(i + 1, nt - 1)),
                  ds(lambda i: i), ds(lambda i: jnp.minimum(i + 1, nt - 1)),
                  pl.BlockSpec((CONV_K, w), lambda j, i: (0, j)), pl.BlockSpec((1, w), lambda j, i: (0, j))],
        out_specs=[ds(lambda i: i), pl.BlockSpec((CONV_K, w), lambda j, i: (0, j)),
                   pl.BlockSpec((1, w), lambda j, i: (0, j))],
        out_shape=[jax.ShapeDtypeStruct((s, 2 * ML_W), F32), jax.ShapeDtypeStruct((CONV_K, 2 * ML_W), F32),
                   jax.ShapeDtypeStruct((1, 2 * ML_W), F32)],
        compiler_params=_params("parallel", "arbitrary"),
    )(proj, proj, proj, dqk, dqk, conv_w, conv_b)


def _bf16_mm(dims_fwd):
    @jax.custom_vjp
    def mm(a, b):
        return _dot(a.astype(BF16), b.astype(BF16), dims_fwd)

    def fwd(a, b):
        return mm(a, b), (a, b)

    def bwd(res, g):
        a, b = res
        if dims_fwd is NN:
            return _mm_nt(g, b), _mm_tn(a, g)
        if dims_fwd is NT:
            return _mm_nn(g, b), _mm_tn(g, a)
        return _mm_nt(b, g), _mm_nn(a, g)

    mm.defvjp(fwd, bwd)
    return mm


_mm_nn = _bf16_mm(NN)
_mm_nt = _bf16_mm(NT)
_mm_tn = _bf16_mm(TN)


def _tri(lower):
    r = lax.broadcasted_iota(jnp.int32, (CHUNK, CHUNK), 0)
    c = lax.broadcasted_iota(jnp.int32, (CHUNK, CHUNK), 1)
    return ((r >= c) if lower else (r <= c)).astype(F32)


@jax.custom_vjp
def _cumsum_rows(x):
    return lax.dot_general(_tri(True), x, NN, precision=lax.Precision.HIGHEST, preferred_element_type=F32)


def _cumsum_fwd(x):
    return _cumsum_rows(x), None


def _cumsum_bwd(_, g):
    return (lax.dot_general(_tri(False), g, NN, precision=lax.Precision.HIGHEST, preferred_element_type=F32),)


_cumsum_rows.defvjp(_cumsum_fwd, _cumsum_bwd)


def _abs(x):
    return jnp.where(x >= 0, x, -x)


def _log_sigmoid(x):
    return jnp.minimum(x, 0.0) - jnp.log(1.0 + jnp.exp(-_abs(x)))


def _pick_col(x, lane):
    sel = lax.broadcasted_iota(jnp.int32, x.shape, 1) == lane
    return jnp.sum(jnp.where(sel, x, 0.0), axis=1, keepdims=True)


def _pick_row(x, r):
    sel = lax.broadcasted_iota(jnp.int32, x.shape, 0) == r
    return jnp.sum(jnp.where(sel, x, 0.0), axis=0, keepdims=True)


def _mlstm_chunk(qs, ks, vs, oms, gates, gate_bias, mlg, cs, ns, ms):
    gb = gates + gate_bias
    cum = _cumsum_rows(_log_sigmoid(gb))
    gbt = gb.T
    cumt = cum.T
    causal = lax.broadcasted_iota(jnp.int32, (CHUNK, CHUNK), 0) >= lax.broadcasted_iota(jnp.int32, (CHUNK, CHUNK), 1)
    ys, c_out, n_out, m_out = [], [], [], []
    for h in range(ML_HEADS):
        q, v, om, c, n, m = qs[h], vs[h], oms[h], cs[h], ns[h], ms[h]
        k = ks[h] * (ML_HD ** -0.5)
        ig_col = _pick_col(gb, h)
        ig_row = _pick_row(gbt, h)
        b_col = _pick_col(cum, ML_HEADS + h)
        b_row = _pick_row(cumt, ML_HEADS + h)
        g = _pick_row(b_col, CHUNK - 1)
        a = g - b_col + ig_col
        m_loc = jnp.max(a, axis=0, keepdims=True)
        wa = jnp.exp(a - m_loc)
        c_loc = _mm_tn(wa * v, k)
        n_loc = jnp.sum(wa * k, axis=0, keepdims=True)
        m_new = jnp.maximum(g + m, m_loc)
        sp = jnp.exp(g + m - m_new)
        sl = jnp.exp(m_loc - m_new)
        c_out.append(sp * c + sl * c_loc)
        n_out.append(sp * n + sl * n_loc)
        m_out.append(m_new)
        d_log = jnp.where(causal, b_col - b_row + ig_row, -jnp.inf)
        e_log = b_col + m
        m_t = jnp.maximum(e_log, jnp.max(d_log, axis=1, keepdims=True))
        d_w = jnp.exp(d_log - m_t)
        e_w = jnp.exp(e_log - m_t)
        s_qk = _mm_nt(q, k) * d_w
        num = e_w * _mm_nt(q, c) + _mm_nn(s_qk, v)
        den = e_w * jnp.sum(q * n, axis=1, keepdims=True) + jnp.sum(s_qk, axis=1, keepdims=True)
        hh = num / jnp.maximum(_abs(den), jnp.exp(-m_t))
        hg = _sigmoid(om) * hh
        mu = jnp.mean(hg, axis=1, keepdims=True)
        hc = hg - mu
        var = jnp.mean(hc * hc, axis=1, keepdims=True)
        ys.append(hc * lax.rsqrt(var + LN_EPS) * mlg[h])
    return ys, c_out, n_out, m_out


V_COL = 5
O_COL = 6


def _mlstm_fwd(qk, proj, gates, gate_bias, mlg, name="mlstm_fwd", gather=()):
    s = qk.shape[0]
    nc = s // CHUNK

    def body(q_ref, k_ref, v_ref, o_ref, g_ref, gb_ref, mlg_ref, y_ref, cp_ref, np_ref, mp_ref, c_s, n_s, m_s):
        ci = pl.program_id(0)

        @pl.when(ci == 0)
        def _():
            c_s[...] = jnp.zeros_like(c_s)
            n_s[...] = jnp.zeros_like(n_s)
            m_s[...] = jnp.zeros_like(m_s)

        cp_ref[...] = c_s[...]
        np_ref[...] = n_s[...]
        mp_ref[...] = m_s[...]
        hs = lambda ref: [ref[:, LANES * h:LANES * (h + 1)] for h in range(ML_HEADS)]
        ys, c_new, n_new, m_new = _mlstm_chunk(
            hs(q_ref), hs(k_ref), hs(v_ref), hs(o_ref), g_ref[...], gb_ref[...], hs(mlg_ref),
            [c_s[h] for h in range(ML_HEADS)], [n_s[h:h + 1, :] for h in range(ML_HEADS)],
            [m_s[h:h + 1, 0:1] for h in range(ML_HEADS)])
        for h in range(ML_HEADS):
            y_ref[:, LANES * h:LANES * (h + 1)] = ys[h]
            c_s[h] = c_new[h]
            n_s[h:h + 1, :] = n_new[h]
            m_s[h:h + 1, :] = jnp.broadcast_to(m_new[h], (1, LANES))

    blk = lambda col: pl.BlockSpec((CHUNK, ML_W), lambda ci: (ci, col))
    vec = lambda w: pl.BlockSpec((1, w), lambda ci: (0, 0))
    return _call(
        body, name=name, grid=(nc,), args=(qk, qk, proj, proj, gates, gate_bias, mlg), sem=("arbitrary",), gather=gather,
        in_specs=[blk(0), blk(1), blk(V_COL), blk(O_COL), pl.BlockSpec((CHUNK, LANES), lambda ci: (ci, 0)),
                  vec(LANES), vec(ML_W)],
        out_specs=[blk(0), pl.BlockSpec((None, ML_HEADS, ML_HD, ML_HD), lambda ci: (ci, 0, 0, 0)),
                   pl.BlockSpec((None, 8, LANES), lambda ci: (ci, 0, 0)),
                   pl.BlockSpec((None, 8, LANES), lambda ci: (ci, 0, 0))],
        out_shape=[jax.ShapeDtypeStruct((s, ML_W), F32), jax.ShapeDtypeStruct((nc, ML_HEADS, ML_HD, ML_HD), F32),
                   jax.ShapeDtypeStruct((nc, 8, LANES), F32), jax.ShapeDtypeStruct((nc, 8, LANES), F32)],
        scratch_shapes=[pltpu.VMEM((ML_HEADS, ML_HD, ML_HD), F32), pltpu.VMEM((8, LANES), F32),
                        pltpu.VMEM((8, LANES), F32)])


def _mlstm_bwd(qk, proj, gates, gate_bias, mlg, cprev, nprev, mprev, dy, name="mlstm_bwd", exchange=()):
    s = qk.shape[0]
    nc = s // CHUNK

    def body(q_ref, k_ref, v_ref, o_ref, g_ref, gb_ref, mlg_ref, cp_ref, np_ref, mp_ref, dy_ref,
             dqk_ref, dv_ref, do_ref, dg_ref, dgb_ref, dmlg_ref, dc_s, dn_s, dm_s, gb8, mg8):
        ci = pl.program_id(0)

        @pl.when(ci == 0)
        def _():
            dc_s[...] = jnp.zeros_like(dc_s)
            dn_s[...] = jnp.zeros_like(dn_s)
            dm_s[...] = jnp.zeros_like(dm_s)
            gb8[...] = jnp.zeros_like(gb8)
            mg8[...] = jnp.zeros_like(mg8)

        hs = lambda ref: [ref[:, LANES * h:LANES * (h + 1)] for h in range(ML_HEADS)]
        prim = (hs(q_ref), hs(k_ref), hs(v_ref), hs(o_ref), g_ref[...], gb_ref[...], hs(mlg_ref),
                [cp_ref[h] for h in range(ML_HEADS)], [np_ref[h:h + 1, :] for h in range(ML_HEADS)],
                [mp_ref[h:h + 1, 0:1] for h in range(ML_HEADS)])
        _, vjp = jax.vjp(_mlstm_chunk, *prim)
        cot = (hs(dy_ref), [dc_s[h] for h in range(ML_HEADS)], [dn_s[h:h + 1, :] for h in range(ML_HEADS)],
               [dm_s[h:h + 1, 0:1] for h in range(ML_HEADS)])
        dqs, dks, dvs, dos, dg, dgb, dmlg, dcs, dns, dms = vjp(cot)
        dg_ref[...] = dg
        gb8[0:1, :] += dgb
        for h in range(ML_HEADS):
            sl = slice(LANES * h, LANES * (h + 1))
            dqk_ref[:, sl] = dqs[h]
            dqk_ref[:, ML_W + LANES * h:ML_W + LANES * (h + 1)] = dks[h]
            dv_ref[:, sl] = dvs[h]
            do_ref[:, sl] = dos[h]
            mg8[0:1, sl] += dmlg[h]
            dc_s[h] = dcs[h]
            dn_s[h:h + 1, :] = dns[h]
            dm_s[h:h + 1, :] = jnp.broadcast_to(dms[h], (1, LANES))

        @pl.when(ci == nc - 1)
        def _():
            dgb_ref[...] = gb8[0:1, :]
            dmlg_ref[...] = mg8[0:1, :]

    rev = lambda ci: nc - 1 - ci
    blk = lambda col: pl.BlockSpec((CHUNK, ML_W), lambda ci: (rev(ci), col))
    vec = lambda w: pl.BlockSpec((1, w), lambda ci: (0, 0))
    st8 = pl.BlockSpec((None, 8, LANES), lambda ci: (rev(ci), 0, 0))
    gsp = pl.BlockSpec((CHUNK, LANES), lambda ci: (rev(ci), 0))
    return _call(
        body, name=name, grid=(nc,), sem=("arbitrary",), exchange=exchange,
        args=(qk, qk, proj, proj, gates, gate_bias, mlg, cprev, nprev, mprev, dy),
        in_specs=[blk(0), blk(1), blk(V_COL), blk(O_COL), gsp, vec(LANES), vec(ML_W),
                  pl.BlockSpec((None, ML_HEADS, ML_HD, ML_HD), lambda ci: (rev(ci), 0, 0, 0)), st8, st8, blk(1)],
        out_specs=[pl.BlockSpec((CHUNK, 2 * ML_W), lambda ci: (rev(ci), 0)), blk(0), blk(0), gsp, vec(LANES), vec(ML_W)],
        out_shape=[jax.ShapeDtypeStruct((s, 2 * ML_W), F32),
                   jax.ShapeDtypeStruct((s, ML_W), F32), jax.ShapeDtypeStruct((s, ML_W), F32),
                   jax.ShapeDtypeStruct((s, LANES), F32), jax.ShapeDtypeStruct((1, LANES), F32),
                   jax.ShapeDtypeStruct((1, ML_W), F32)],
        scratch_shapes=[pltpu.VMEM((ML_HEADS, ML_HD, ML_HD), F32), pltpu.VMEM((8, LANES), F32),
                        pltpu.VMEM((8, LANES), F32), pltpu.VMEM((8, LANES), F32), pltpu.VMEM((8, ML_W), F32)])


def _xattn_tile(qs, ks, vs):
    outs = []
    for q, k, v in zip(qs, ks, vs):
        sc = _mm_nt(q, k) * (XA_HD ** -0.5)
        mx = lax.stop_gradient(jnp.max(sc, axis=1, keepdims=True))
        pe = jnp.exp(sc - mx)
        outs.append(_mm_nn(pe / jnp.sum(pe, axis=1, keepdims=True), v))
    return outs


def _xa_heads(ref):
    return [ref[:, XA_HD * h:XA_HD * (h + 1)] for h in range(XA_HEADS)]


def _xattn_fwd(q, kv, name="xattn_fwd", tm=512):
    s, d = q.shape

    def body(q_ref, k_ref, v_ref, o_ref):
        outs = _xattn_tile(_xa_heads(q_ref), _xa_heads(k_ref), _xa_heads(v_ref))
        for h in range(XA_HEADS):
            o_ref[:, XA_HD * h:XA_HD * (h + 1)] = outs[h]

    row = pl.BlockSpec((tm, d), lambda i: (i, 0))
    return pl.pallas_call(
        body, name=name, grid=(s // tm,),
        in_specs=[row, pl.BlockSpec((MEM_LEN, d), lambda i: (0, 0)), pl.BlockSpec((MEM_LEN, d), lambda i: (0, 1))],
        out_specs=row, out_shape=jax.ShapeDtypeStruct((s, d), F32),
        compiler_params=_params("parallel"),
    )(q, kv, kv)


def _xattn_bwd(q, kv, do, name="xattn_bwd", tm=512):
    s, d = q.shape

    def body(q_ref, k_ref, v_ref, do_ref, dq_ref, dkv_ref):
        i = pl.program_id(0)
        _, vjp = jax.vjp(_xattn_tile, _xa_heads(q_ref), _xa_heads(k_ref), _xa_heads(v_ref))
        dqs, dks, dvs = vjp(_xa_heads(do_ref))

        @pl.when(i == 0)
        def _():
            dkv_ref[...] = jnp.zeros_like(dkv_ref)

        for h in range(XA_HEADS):
            sl = slice(XA_HD * h, XA_HD * (h + 1))
            dq_ref[:, sl] = dqs[h]
            dkv_ref[:, sl] += dks[h]
            dkv_ref[:, d + XA_HD * h:d + XA_HD * (h + 1)] += dvs[h]

    row = pl.BlockSpec((tm, d), lambda i: (i, 0))
    return pl.pallas_call(
        body, name=name, grid=(s // tm,),
        in_specs=[row, pl.BlockSpec((MEM_LEN, d), lambda i: (0, 0)), pl.BlockSpec((MEM_LEN, d), lambda i: (0, 1)), row],
        out_specs=[row, pl.BlockSpec((MEM_LEN, 2 * d), lambda i: (0, 0))],
        out_shape=[jax.ShapeDtypeStruct((s, d), F32), jax.ShapeDtypeStruct((MEM_LEN, 2 * d), F32)],
        compiler_params=_params("arbitrary"),
    )(q, kv, kv, do)


def _loss_head(y, target, name="loss_head", tm=512):
    s, d = y.shape
    nt = s // tm

    def body(y_ref, t_ref, dy_ref, loss_ref, acc):
        i = pl.program_id(0)
        err = y_ref[...] - t_ref[...]
        dy_ref[...] = err * (1.0 / d)

        @pl.when(i == 0)
        def _():
            acc[...] = jnp.zeros_like(acc)

        acc[...] += _rowsum8(err * err)

        @pl.when(i == nt - 1)
        def _():
            tot = jnp.sum(jnp.sum(acc[...], axis=0, keepdims=True), axis=1, keepdims=True)
            loss_ref[...] = jnp.broadcast_to(tot * (0.5 / d), (1, LANES))

    row = pl.BlockSpec((tm, d), lambda i: (i, 0))
    return pl.pallas_call(
        body, name=name, grid=(nt,),
        in_specs=[row, row], out_specs=[row, pl.BlockSpec((1, LANES), lambda i: (0, 0))],
        out_shape=[jax.ShapeDtypeStruct((s, d), F32), jax.ShapeDtypeStruct((1, LANES), F32)],
        scratch_shapes=[pltpu.VMEM((8, d), F32)],
        compiler_params=_params("arbitrary"),
    )(y, target)


def _adam2d(recv, w, m, v, name, layer=None):
    rows, cols = w.shape[-2:]
    fits = [t for t in range(16, rows + 1, 16) if rows % t == 0 and t * cols <= 128 * 1024]
    tr = max(fits) if fits else rows

    def body(r_ref, w_ref, m_ref, v_ref, g_ref, d_ref, mo_ref, vo_ref):
        g = r_ref[0].astype(F32)
        for j in range(1, N_DEV):
            g = g + r_ref[j].astype(F32)
        mn = ADAM_B1 * m_ref[...] + (1.0 - ADAM_B1) * g
        vn = ADAM_B2 * v_ref[...] + (1.0 - ADAM_B2) * jnp.square(g)
        m_hat = mn / (1.0 - ADAM_B1 ** ADAM_STEP)
        v_hat = vn / (1.0 - ADAM_B2 ** ADAM_STEP)
        g_ref[...] = g
        d_ref[...] = -ADAM_LR * (m_hat / (jnp.sqrt(v_hat) + ADAM_EPS) + ADAM_WD * w_ref[...])
        mo_ref[...] = mn
        vo_ref[...] = vn

    row = pl.BlockSpec((tr, cols), lambda i: (i, 0))
    if layer is None:
        wspec = row
    else:
        wspec = pl.BlockSpec((None, None, tr, cols), lambda i: (0, layer, i, 0))
    return pl.pallas_call(
        body, name=name, grid=(rows // tr,),
        in_specs=[pl.BlockSpec((N_DEV, tr, cols), lambda i: (0, i, 0)), wspec, wspec, wspec],
        out_specs=[row] * 4, out_shape=[jax.ShapeDtypeStruct((rows, cols), F32)] * 4,
        compiler_params=_params("parallel"),
    )(recv, w, m, v)


WEIGHTS = ("rel_bias", "ln_g", "ln_b", "ffn_w_gate", "ffn_w_up", "ffn_w_down", "w_in", "conv_w", "conv_b",
           "ig_bias", "fg_bias", "ml_norm_g", "w_out", "xq_w", "xkv_w", "xo_w")
SMALL = ("rel_bias", "ln_g", "ln_b", "conv_w", "conv_b", "ig_bias", "fg_bias", "ml_norm_g")
SMALL_SHAPES = {
    "rel_bias": (REL_BUCKETS, ATT_HEADS), "ln_g": (1, 4, LANES), "ln_b": (1, 4, LANES), "conv_w": (1, CONV_K, LANES),
    "conv_b": (1, 2 * ML_W), "ig_bias": (1, ML_HEADS), "fg_bias": (1, ML_HEADS), "ml_norm_g": (1, ML_W),
}
SMALL_ROWS = 8


def _pack_small(parts, lead=()):
    out = []
    for p in parts:
        p = jnp.pad(p, [(0, 0)] * len(lead) + [(0, SMALL_ROWS * LANES - p.shape[-1])])
        out.append(p.reshape(lead + (SMALL_ROWS, LANES)))
    return jnp.concatenate(out, axis=len(lead))


def _unpack_small(flat):
    out = {}
    for i, n in enumerate(SMALL):
        cnt = int(np.prod(SMALL_SHAPES[n]))
        out[n] = flat[SMALL_ROWS * i:SMALL_ROWS * (i + 1)].reshape(-1)[:cnt].reshape(SMALL_SHAPES[n])
    return out


def _split8(full, axis):
    shp = full.shape
    t = full.reshape(shp[:axis] + (N_DEV, shp[axis] // N_DEV) + shp[axis + 1:])
    return jnp.moveaxis(t, axis, 0).reshape(N_DEV, -1)


def _rep8(full):
    return jnp.broadcast_to(full.reshape(1, -1), (N_DEV, full.size))


def kernel(x, mem, rel_bias, ln_g, ln_b, ffn_w_gate, ffn_w_up, ffn_w_down, w_in, conv_w, conv_b, ig_bias, fg_bias, ml_norm_g, w_out, xq_w, xkv_w, xo_w, loss_target, m_rel_bias, m_ln_g, m_ln_b, m_ffn_w_gate, m_ffn_w_up, m_ffn_w_down, m_w_in, m_conv_w, m_conv_b, m_ig_bias, m_fg_bias, m_ml_norm_g, m_w_out, m_xq_w, m_xkv_w, m_xo_w, v_rel_bias, v_ln_g, v_ln_b, v_ffn_w_gate, v_ffn_w_up, v_ffn_w_down, v_w_in, v_conv_w, v_conv_b, v_ig_bias, v_fg_bias, v_ml_norm_g, v_w_out, v_xq_w, v_xkv_w, v_xo_w):
    w_tree = dict(rel_bias=rel_bias, ln_g=ln_g, ln_b=ln_b, ffn_w_gate=ffn_w_gate, ffn_w_up=ffn_w_up,
                  ffn_w_down=ffn_w_down, w_in=w_in, conv_w=conv_w, conv_b=conv_b, ig_bias=ig_bias, fg_bias=fg_bias,
                  ml_norm_g=ml_norm_g, w_out=w_out, xq_w=xq_w, xkv_w=xkv_w, xo_w=xo_w)
    m_tree = dict(rel_bias=m_rel_bias, ln_g=m_ln_g, ln_b=m_ln_b, ffn_w_gate=m_ffn_w_gate, ffn_w_up=m_ffn_w_up,
                  ffn_w_down=m_ffn_w_down, w_in=m_w_in, conv_w=m_conv_w, conv_b=m_conv_b, ig_bias=m_ig_bias,
                  fg_bias=m_fg_bias, ml_norm_g=m_ml_norm_g, w_out=m_w_out, xq_w=m_xq_w, xkv_w=m_xkv_w, xo_w=m_xo_w)
    v_tree = dict(rel_bias=v_rel_bias, ln_g=v_ln_g, ln_b=v_ln_b, ffn_w_gate=v_ffn_w_gate, ffn_w_up=v_ffn_w_up,
                  ffn_w_down=v_ffn_w_down, w_in=v_w_in, conv_w=v_conv_w, conv_b=v_conv_b, ig_bias=v_ig_bias,
                  fg_bias=v_fg_bias, ml_norm_g=v_ml_norm_g, w_out=v_w_out, xq_w=v_xq_w, xkv_w=v_xkv_w, xo_w=v_xo_w)
    x0 = x[0]
    pad_ff = FF_PAD - FF_SHARD
    bf = lambda t: t.astype(BF16)

    pad_rows = lambda t: jnp.pad(t, ((0, pad_ff), (0, 0)))
    ffn_shards = [(pad_rows(bf(ffn_w_gate[0, l]).T), pad_rows(bf(ffn_w_up[0, l]).T), pad_rows(bf(ffn_w_down[0, l])))
                  for l in range(2)]
    pairs = lambda t: t.reshape(N_PAIR, FF_PAIR, D_MODEL)
    w_in_shard = jnp.pad(bf(w_in[0]), ((0, 0), (0, ATT_W - W_IN_SHARD)))
    small_shard = jnp.concatenate([ln_g[0], ln_b[0], conv_w[0], jnp.zeros((4, LANES), F32)], axis=0)
    gate_bias = jnp.pad(jnp.concatenate([ig_bias, fg_bias], axis=1), ((0, 0), (0, LANES - 2 * ML_HEADS)))
    buckets = _bucket_tables()

    wg0, wu0, wd0, small_all = _gather_two_level("ffn1_weights_gather", ffn_shards[0] + (small_shard,))
    wg0, wu0, wd0 = pairs(wg0), pairs(wu0), pairs(wd0)
    unshard = lambda t: jnp.moveaxis(t, 0, 1).reshape(4, D_MODEL)
    ln_g_full, ln_b_full, conv_w_full = unshard(small_all[:, 0:4]), unshard(small_all[:, 4:8]), unshard(small_all[:, 8:12])
    lng = lambda i: ln_g_full[i:i + 1]
    lnb = lambda i: ln_b_full[i:i + 1]

    u0, x1, win_all, wout_all, xq_all, xo_all, xkv_all = _ffn_fwd(
        x0, wg0, wu0, wd0, lng(0), lnb(0), "ffn1_fwd",
        gather=(w_in_shard, bf(w_out[0]), bf(xq_w[0]), bf(xo_w[0]), bf(xkv_w[0])))
    w_in_full = jnp.moveaxis(win_all[:, :, :W_IN_SHARD], 0, 1).reshape(D_MODEL, W_IN)
    w_main = w_in_full[:, :W_IN_MAIN]
    w_gate_cols = jnp.pad(w_in_full[:, W_IN_MAIN:], ((0, 0), (0, LANES - 2 * ML_HEADS)))
    w_out_full = wout_all.reshape(D_MODEL, D_MODEL)
    xq_full = xq_all.reshape(D_MODEL, D_MODEL)
    xo_full = xo_all.reshape(D_MODEL, D_MODEL)

    proj, wg1 = _matmul(x1, w_main, "nn", "proj_fwd", tk=D_MODEL, gather=(ffn_shards[1][0],))
    gates, = _matmul(x1, w_gate_cols, "nn", "gates_fwd", tk=D_MODEL)
    biasm = _bias_fwd(rel_bias, buckets)
    att, lse, wd1 = _dil_fwd(proj, biasm, gather=(ffn_shards[1][2],))
    qk = _conv_fwd(proj, conv_w_full, conv_b)
    y_m, c_prev, n_prev, m_prev, wu1 = _mlstm_fwd(qk, proj, gates, gate_bias, ml_norm_g, gather=(ffn_shards[1][1],))
    cat = jnp.concatenate([att, y_m], axis=1)
    u1, x2 = _matmul_resid_ln(cat, w_out_full, x1, lng(1), lnb(1), "w_out_fwd")
    q_x, = _matmul(x2, xq_full, "nn", "xq_fwd", tn=D_MODEL, tk=D_MODEL)
    kv, = _matmul(mem[0], xkv_all, "nn", "xkv_fwd", tk=D_MODEL)
    o_x = _xattn_fwd(q_x, kv)
    u2, x3 = _matmul_resid_ln(o_x, xo_full, x2, lng(2), lnb(2), "xo_fwd")
    wg1, wu1, wd1 = pairs(wg1), pairs(wu1), pairs(wd1)
    u3, x4 = _ffn_fwd(x3, wg1, wu1, wd1, lng(3), lnb(3), "ffn2_fwd")
    dx4, loss_row = _loss_head(x4, loss_target[0])

    dx3, xb, df, da, db, hh, dg3, db3 = _ffn_bwd_x(dx4, u3, x3, wg1, wu1, wd1, lng(3), "ffn2_bwd_x")
    ffn2_send = _ffn_bwd_w(xb, df, da, db, hh, "ffn2_bwd_w")

    du2, dg2, db2 = _ln_bwd(dx3, u2, lng(2), "xattn_ln_bwd")
    do_x, = _matmul(du2, xo_full, "nt", "xo_bwd_x", tn=D_MODEL, tk=D_MODEL)
    g_xo, = _matmul(o_x, du2, "tn", "xo_bwd_w", tm=D_MODEL, tn=D_MODEL, out_dtype=BF16)
    dq_x, dkv = _xattn_bwd(q_x, kv, do_x)
    g_xq, = _matmul(x2, dq_x, "tn", "xq_bwd_w", tm=D_MODEL, tn=D_MODEL, out_dtype=BF16)
    g_xkv, = _matmul(mem[0], dkv, "tn", "xkv_bwd_w", tm=D_MODEL, tn=2 * D_MODEL // N_DEV, tk=MEM_LEN,
                     out_dtype=BF16, blocked_out=True)
    dx2, = _matmul(dq_x, xq_full, "nt", "xq_bwd_x", tn=D_MODEL, tk=D_MODEL, add=du2, add_scale=ALPHA)

    du1, dg1, db1 = _ln_bwd(dx2, u1, lng(1), "mixer_ln_bwd")
    dcat, = _matmul(du1, w_out_full, "nt", "w_out_bwd_x", tn=D_MODEL, tk=D_MODEL)
    g_w_out, = _matmul(cat, du1, "tn", "w_out_bwd_w", tm=D_MODEL, tn=D_MODEL, out_dtype=BF16)
    dqk, dv_m, do_m, dgates, dgate_bias, g_mlg, *ffn2_recv = _mlstm_bwd(
        qk, proj, gates, gate_bias, ml_norm_g, c_prev, n_prev, m_prev, dcat, exchange=tuple(ffn2_send))
    dqk_pre, g_conv_w, g_conv_b = _conv_bwd(proj, dqk, conv_w_full, conv_b)
    dq_a, dk_a, dv_a, dbias = _dil_bwd(proj, biasm, lse, att, dcat)
    g_rel = _bias_bwd(dbias.reshape(biasm.shape), buckets)[:, :ATT_HEADS]
    dproj = jnp.concatenate([dq_a, dk_a, dv_a, bf(dqk_pre), bf(dv_m), bf(do_m)], axis=1)
    g_w_main, = _matmul(x1, dproj, "tn", "proj_bwd_w", tm=D_MODEL, tn=W_IN_MAIN // 2, out_dtype=BF16)
    g_w_gates, = _matmul(x1, dgates, "tn", "gates_bwd_w", tm=D_MODEL, out_dtype=BF16)
    g_w_in = jnp.concatenate([g_w_main, g_w_gates[:, :2 * ML_HEADS]], axis=1)
    dx1, = _matmul(dproj, w_main, "nt", "proj_bwd_x", tn=D_MODEL, add=du1, add_scale=ALPHA)
    dx1, = _matmul(dgates, w_gate_cols, "nt", "gates_bwd_x", tn=D_MODEL, add=dx1)

    rows8 = lambda t: t.reshape(N_DEV, D_MODEL // N_DEV, D_MODEL)
    mid_send = (rows8(g_xo), rows8(g_xq), g_xkv, rows8(g_w_out),
                jnp.moveaxis(g_w_in.reshape(D_MODEL, N_DEV, W_IN_SHARD), 1, 0))
    dx0, xb, df, da, db, hh, dg0, db0, r_xo, r_xq, r_xkv, r_w_out, r_w_in = _ffn_bwd_x(
        dx1, u0, x0, wg0, wu0, wd0, lng(0), "ffn1_bwd_x", exchange=mid_send)
    small_blocks = {
        "rel_bias": _rep8(g_rel),
        "ln_g": _split8(jnp.concatenate([dg0, dg1, dg2, dg3], axis=0), 1),
        "ln_b": _split8(jnp.concatenate([db0, db1, db2, db3], axis=0), 1),
        "conv_w": _split8(g_conv_w, 1),
        "conv_b": _rep8(g_conv_b),
        "ig_bias": _rep8(dgate_bias[:, :ML_HEADS]),
        "fg_bias": _rep8(dgate_bias[:, ML_HEADS:2 * ML_HEADS]),
        "ml_norm_g": _rep8(g_mlg),
    }
    small_send = _pack_small([small_blocks[n] for n in SMALL], lead=(N_DEV,))
    *ffn1_lo, r_small = _ffn_bwd_w(xb, df, da, db, hh, "ffn1_bwd_w_lo", pairs=(0, N_PAIR // 2), exchange=(small_send,))
    *ffn1_hi, rl_g, rl_u, rl_d = _ffn_bwd_w(xb, df, da, db, hh, "ffn1_bwd_w_hi", pairs=(N_PAIR // 2, N_PAIR),
                                            exchange=tuple(ffn1_lo), half=0)
    rh = _exchange_only("ffn1_grads_exchange", exchange=tuple(ffn1_hi), half=1)
    i_am_lo = lax.axis_index("x") == 0
    ffn1_recv = [jnp.where(i_am_lo, lo_, hi_) for lo_, hi_ in zip((rl_g, rl_u, rl_d), rh)]

    res = {}
    for i, n in enumerate(("ffn_w_gate", "ffn_w_up", "ffn_w_down")):
        per_layer = [_adam2d(r[i], w_tree[n], m_tree[n], v_tree[n], f"adamw_{n}_{l}", layer=l)
                     for l, r in enumerate((ffn1_recv, ffn2_recv))]
        res[n] = [jnp.stack([per_layer[0][j], per_layer[1][j]])[None] for j in range(4)]
    for n, r in (("w_in", r_w_in), ("w_out", r_w_out), ("xq_w", r_xq), ("xkv_w", r_xkv), ("xo_w", r_xo)):
        res[n] = [t[None] for t in _adam2d(r, w_tree[n][0], m_tree[n][0], v_tree[n][0], f"adamw_{n}")]
    pack = lambda tree: _pack_small([tree[n].reshape(-1) for n in SMALL])
    small = [_unpack_small(t) for t in _adam2d(r_small, pack(w_tree), pack(m_tree), pack(v_tree), "adamw_small")]
    for n in SMALL:
        res[n] = [small[j][n] for j in range(4)]

    loss = lax.psum(loss_row[0, 0], ("x", "y", "c"))
    return (loss, dx0[None], *[res[n][0] for n in WEIGHTS], *[res[n][1] for n in WEIGHTS],
            *[res[n][2] for n in WEIGHTS], *[res[n][3] for n in WEIGHTS])
```

```python
import functools
import math

import numpy as np
import jax
import jax.numpy as jnp
from jax import lax
from jax.experimental import pallas as pl
from jax.experimental.pallas import tpu as pltpu

F32 = jnp.float32
BF16 = jnp.bfloat16

N_DEV = 8
D_MODEL = 1024
D_FF = 2816
FF_SHARD = D_FF // N_DEV
FF_PAD = 384
ATT_W = 512
ATT_HEADS = 8
DILATED = ((128, 1), (512, 4), (2048, 16))
BLK = 128
ML_W = 512
ML_HEADS = 4
ML_HD = 128
CHUNK = 128
CONV_K = 4
W_IN = 3592
W_IN_SHARD = W_IN // N_DEV
W_IN_MAIN = 3584
XA_HEADS = 4
XA_HD = 256
MEM_LEN = 256
REL_BUCKETS = 32
REL_MAX_DIST = 2048
ALPHA = 2.0 ** 0.25
LN_EPS = 1e-5
NEG = -1e30
ADAM_LR = 0.001
ADAM_B1 = 0.9
ADAM_B2 = 0.999
ADAM_EPS = 1e-08
ADAM_WD = 0.01
ADAM_STEP = 10
LANES = 128
VMEM_LIMIT = 58 * 1024 * 1024

NN = (((1,), (0,)), ((), ()))
NT = (((1,), (1,)), ((), ()))
TN = (((0,), (0,)), ((), ()))


def _dot(a, b, dims):
    return lax.dot_general(a, b, dims, preferred_element_type=F32)


def _params(*sem):
    return pltpu.CompilerParams(dimension_semantics=sem, vmem_limit_bytes=VMEM_LIMIT)


def _sigmoid(x):
    return 1.0 / (1.0 + jnp.exp(-x))


def _rowsum8(x):
    t, c = x.shape
    return jnp.sum(x.reshape(t // 8, 8, c), axis=0)


def _mesh_pos():
    x, y, c = lax.axis_index("x"), lax.axis_index("y"), lax.axis_index("c")
    return x, y, c, 4 * x + 2 * y + c


def _peer(x, y, c, k):
    px = 1 - x if k & 4 else x
    py = 1 - y if k & 2 else y
    pc = 1 - c if k & 1 else c
    return (px, py, pc), 4 * px + 2 * py + pc


def _call(body, *, name, grid, in_specs, out_specs, out_shape, args, scratch_shapes=(), sem=None,
          gather=(), exchange=()):
    in_specs, out_specs, out_shape, scratch = list(in_specs), list(out_specs), list(out_shape), list(scratch_shapes)
    ng, nc = len(gather), len(gather) + len(exchange)
    if nc == 0:
        return pl.pallas_call(body, name=name, grid=grid, in_specs=in_specs, out_specs=out_specs,
                              out_shape=out_shape, scratch_shapes=scratch, compiler_params=_params(*sem))(*args)
    n_in, n_out, n_scr = len(in_specs), len(out_specs), len(scratch)

    def wrapped(*refs):
        ins, cin = refs[:n_in], refs[n_in:n_in + nc]
        outs, cout = refs[n_in + nc:n_in + nc + n_out], refs[n_in + nc + n_out:n_in + 2 * nc + n_out]
        scr = refs[n_in + 2 * nc + n_out:n_in + 2 * nc + n_out + n_scr]
        send_sems, recv_sems, loc_sems = refs[-3:]
        first, last = None, None
        for ax, extent in enumerate(grid):
            f, l = pl.program_id(ax) == 0, pl.program_id(ax) == extent - 1
            first = f if first is None else first & f
            last = l if last is None else last & l

        def copies():
            x, y, c, me = _mesh_pos()
            out = []
            for a in range(nc):
                mine = cin[a] if a < ng else cin[a].at[me]
                out.append(pltpu.make_async_copy(mine, cout[a].at[me], loc_sems.at[a]))
                for k in range(1, N_DEV):
                    peer, pidx = _peer(x, y, c, k)
                    out.append(pltpu.make_async_remote_copy(
                        src_ref=cin[a] if a < ng else cin[a].at[pidx], dst_ref=cout[a].at[me],
                        send_sem=send_sems.at[a, k - 1], recv_sem=recv_sems.at[a, k - 1],
                        device_id=peer, device_id_type=pl.DeviceIdType.MESH))
            return out

        @pl.when(first)
        def _():
            for cp in copies():
                cp.start()

        body(*ins, *outs, *scr)

        @pl.when(last)
        def _():
            for cp in copies():
                cp.wait()

    hbm = pl.BlockSpec(memory_space=pl.ANY)
    comm_shapes = [jax.ShapeDtypeStruct((N_DEV,) + a.shape, a.dtype) for a in gather]
    comm_shapes += [jax.ShapeDtypeStruct(a.shape, a.dtype) for a in exchange]
    return pl.pallas_call(
        wrapped, name=name, grid=grid, in_specs=in_specs + [hbm] * nc, out_specs=out_specs + [hbm] * nc,
        out_shape=out_shape + comm_shapes,
        scratch_shapes=scratch + [pltpu.SemaphoreType.DMA((nc, N_DEV - 1)), pltpu.SemaphoreType.DMA((nc, N_DEV - 1)),
                                  pltpu.SemaphoreType.DMA((nc,))],
        compiler_params=_params(*(("arbitrary",) * len(grid))),
    )(*args, *gather, *exchange)


def _gather_two_level(name, arrays):
    na = len(arrays)

    def body(*refs):
        srcs, outs = refs[:na], refs[na:2 * na]
        send_sems, recv_sems, loc_sems = refs[2 * na:]
        x, y, c, me = _mesh_pos()
        here, sib = (x, y, c), (x, y, 1 - c)
        chips = [(1 - x, y), (x, 1 - y), (1 - x, 1 - y)]
        pos = lambda px, py, pc: 4 * px + 2 * py + pc

        def copy(a, k, block, to, src=None):
            return pltpu.make_async_remote_copy(
                src_ref=outs[a].at[block] if src is None else src, dst_ref=outs[a].at[block],
                send_sem=send_sems.at[a, k], recv_sem=recv_sems.at[a, k], device_id=to,
                device_id_type=pl.DeviceIdType.MESH)

        locs = [pltpu.make_async_copy(srcs[a], outs[a].at[me], loc_sems.at[a]) for a in range(na)]
        for cp in locs:
            cp.start()
        first = []
        for a in range(na):
            first.append(copy(a, 0, me, sib, src=srcs[a]))
            first += [copy(a, 1 + j, me, (*chip, c), src=srcs[a]) for j, chip in enumerate(chips)]
        for cp in first:
            cp.start()
        passed = []
        for a in range(na):
            for j, chip in enumerate(chips):
                copy(a, 1 + j, pos(*chip, c), here).wait_recv()
                passed.append(copy(a, 4 + j, pos(*chip, c), sib))
                passed[-1].start()
        for a in range(na):
            copy(a, 0, pos(x, y, 1 - c), here).wait_recv()
            for j, chip in enumerate(chips):
                copy(a, 4 + j, pos(*chip, 1 - c), here).wait_recv()
        for cp in first + passed:
            cp.wait_send()
        for cp in locs:
            cp.wait()

    hbm = pl.BlockSpec(memory_space=pl.ANY)
    return pl.pallas_call(
        body, name=name, in_specs=[hbm] * na, out_specs=[hbm] * na,
        out_shape=[jax.ShapeDtypeStruct((N_DEV,) + a.shape, a.dtype) for a in arrays],
        scratch_shapes=[pltpu.SemaphoreType.DMA((na, N_DEV - 1)), pltpu.SemaphoreType.DMA((na, N_DEV - 1)),
                        pltpu.SemaphoreType.DMA((na,))],
    )(*arrays)


def _exchange_only(name, gather=(), exchange=()):
    return _call(lambda: None, name=name, grid=(1,), in_specs=[], out_specs=[], out_shape=[], args=(),
                 gather=gather, exchange=exchange)


def _matmul(a, b, mode, name, *, out_dtype=F32, tm=512, tn=512, tk=512, add=None, add_scale=1.0,
            blocked_out=False, gather=(), exchange=()):
    blocked_b = b.ndim == 3
    if blocked_b:
        (m, k), (nb, _, tn) = a.shape, b.shape
        n = nb * tn
    elif mode == "nn":
        (m, k), (_, n) = a.shape, b.shape
    elif mode == "nt":
        (m, k), (n, _) = a.shape, b.shape
    else:
        (k, m), (_, n) = a.shape, b.shape
    tm, tn, tk = min(tm, m), min(tn, n), min(tk, k)
    nk = k // tk
    dims = {"nn": NN, "nt": NT, "tn": TN}[mode]
    if mode == "tn":
        a_spec = pl.BlockSpec((tk, tm), lambda i, j, kk: (kk, i))
    else:
        a_spec = pl.BlockSpec((tm, tk), lambda i, j, kk: (i, kk))
    if blocked_b:
        b_spec = pl.BlockSpec((None, tk, tn), lambda i, j, kk: (j, kk, 0))
    elif mode == "nt":
        b_spec = pl.BlockSpec((tn, tk), lambda i, j, kk: (j, kk))
    else:
        b_spec = pl.BlockSpec((tk, tn), lambda i, j, kk: (kk, j))
    if blocked_out:
        o_spec = pl.BlockSpec((None, tm, tn), lambda i, j, kk: (j, i, 0))
        o_shape = jax.ShapeDtypeStruct((n // tn, m, tn), out_dtype)
    else:
        o_spec = pl.BlockSpec((tm, tn), lambda i, j, kk: (i, j))
        o_shape = jax.ShapeDtypeStruct((m, n), out_dtype)
    has_add = add is not None
    cache_a = nk == 1 and mode != "tn" and n // tn > 1 and a.dtype != BF16

    def body(*refs):
        if has_add:
            a_ref, b_ref, add_ref, o_ref, s_ref = refs
        else:
            a_ref, b_ref, o_ref, s_ref = refs
        kk = pl.program_id(2)
        if cache_a:
            @pl.when(pl.program_id(1) == 0)
            def _():
                s_ref[...] = a_ref[...].astype(BF16)

            lhs = s_ref[...]
        else:
            lhs = a_ref[...].astype(BF16)
        part = _dot(lhs, b_ref[...].astype(BF16), dims)

        def finish(r):
            if has_add:
                r = r + add_scale * add_ref[...]
            o_ref[...] = r.astype(out_dtype)

        if nk == 1:
            finish(part)
            return

        @pl.when(kk == 0)
        def _():
            s_ref[...] = part

        @pl.when(kk > 0)
        def _():
            s_ref[...] += part

        @pl.when(kk == nk - 1)
        def _():
            finish(s_ref[...])

    if nk > 1:
        scratch = [pltpu.VMEM((tm, tn), F32)]
    else:
        scratch = [pltpu.VMEM((tm, tk), BF16) if cache_a else pltpu.VMEM((8, LANES), F32)]
    return _call(
        body, name=name, grid=(m // tm, n // tn, nk),
        in_specs=[a_spec, b_spec] + ([pl.BlockSpec((tm, tn), lambda i, j, kk: (i, j))] if has_add else []),
        out_specs=[o_spec], out_shape=[o_shape], args=(a, b) + ((add,) if has_add else ()),
        scratch_shapes=scratch, sem=("parallel", "arbitrary", "arbitrary"),
        gather=gather, exchange=exchange)


def _ln_fwd_math(u, g, b):
    mu = jnp.mean(u, axis=-1, keepdims=True)
    uc = u - mu
    var = jnp.mean(uc * uc, axis=-1, keepdims=True)
    return uc * lax.rsqrt(var + LN_EPS) * g + b


def _ln_bwd_math(dy, u, g):
    mu = jnp.mean(u, axis=-1, keepdims=True)
    uc = u - mu
    var = jnp.mean(uc * uc, axis=-1, keepdims=True)
    rstd = lax.rsqrt(var + LN_EPS)
    xhat = uc * rstd
    dxh = dy * g
    m1 = jnp.mean(dxh, axis=-1, keepdims=True)
    m2 = jnp.mean(dxh * xhat, axis=-1, keepdims=True)
    return rstd * (dxh - m1 - xhat * m2), xhat


def _matmul_resid_ln(a, w, x, g, b, name, tm=512):
    s, k = a.shape
    d = w.shape[1]

    def body(a_ref, w_ref, x_ref, g_ref, b_ref, u_ref, y_ref):
        u = ALPHA * x_ref[...] + _dot(a_ref[...].astype(BF16), w_ref[...], NN)
        u_ref[...] = u
        y_ref[...] = _ln_fwd_math(u, g_ref[...], b_ref[...])

    row = pl.BlockSpec((tm, d), lambda i: (i, 0))
    vec = pl.BlockSpec((1, d), lambda i: (0, 0))
    return pl.pallas_call(
        body, name=name, grid=(s // tm,),
        in_specs=[pl.BlockSpec((tm, k), lambda i: (i, 0)), pl.BlockSpec((k, d), lambda i: (0, 0)), row, vec, vec],
        out_specs=[row, row], out_shape=[jax.ShapeDtypeStruct((s, d), F32)] * 2,
        compiler_params=_params("parallel"),
    )(a, w, x, g, b)


def _ln_bwd(dy, u, g, name, tm=512):
    s, d = dy.shape
    nt = s // tm

    def body(dy_ref, u_ref, g_ref, du_ref, dg_ref, db_ref, g8, b8):
        i = pl.program_id(0)
        dy_ = dy_ref[...]
        du, xhat = _ln_bwd_math(dy_, u_ref[...], g_ref[...])
        du_ref[...] = du

        @pl.when(i == 0)
        def _():
            g8[...] = jnp.zeros_like(g8)
            b8[...] = jnp.zeros_like(b8)

        g8[...] += _rowsum8(dy_ * xhat)
        b8[...] += _rowsum8(dy_)

        @pl.when(i == nt - 1)
        def _():
            dg_ref[...] = jnp.sum(g8[...], axis=0, keepdims=True)
            db_ref[...] = jnp.sum(b8[...], axis=0, keepdims=True)

    row = pl.BlockSpec((tm, d), lambda i: (i, 0))
    vec = pl.BlockSpec((1, d), lambda i: (0, 0))
    return pl.pallas_call(
        body, name=name, grid=(nt,),
        in_specs=[row, row, vec], out_specs=[row, vec, vec],
        out_shape=[jax.ShapeDtypeStruct((s, d), F32), jax.ShapeDtypeStruct((1, d), F32),
                   jax.ShapeDtypeStruct((1, d), F32)],
        scratch_shapes=[pltpu.VMEM((8, d), F32), pltpu.VMEM((8, d), F32)],
        compiler_params=_params("arbitrary"),
    )(dy, u, g)


FF_PAIR = 2 * FF_PAD
N_PAIR = N_DEV // 2


def _ffn_fwd(x, wgt, wut, wd, g, b, name, tm=512, gather=()):
    s, d = x.shape

    def body(x_ref, wg_ref, wu_ref, wd_ref, g_ref, b_ref, u_ref, y_ref, xb, acc):
        k = pl.program_id(1)

        @pl.when(k == 0)
        def _():
            xb[...] = x_ref[...].astype(BF16)

        a = _dot(xb[...], wg_ref[...], NT)
        bb = _dot(xb[...], wu_ref[...], NT)
        h = (a * _sigmoid(a) * bb).astype(BF16)
        part = _dot(h, wd_ref[...], NN)

        @pl.when(k == 0)
        def _():
            acc[...] = part

        @pl.when(k > 0)
        def _():
            acc[...] += part

        @pl.when(k == N_PAIR - 1)
        def _():
            u = ALPHA * x_ref[...] + 0.5 * acc[...]
            u_ref[...] = u
            y_ref[...] = _ln_fwd_math(u, g_ref[...], b_ref[...])

    row = pl.BlockSpec((tm, d), lambda i, k: (i, 0))
    vec = pl.BlockSpec((1, d), lambda i, k: (0, 0))
    w_in = pl.BlockSpec((None, FF_PAIR, d), lambda i, k: (k, 0, 0))
    w_dn = w_in
    return _call(
        body, name=name, grid=(s // tm, N_PAIR),
        in_specs=[row, w_in, w_in, w_dn, vec, vec], out_specs=[row, row],
        out_shape=[jax.ShapeDtypeStruct((s, d), F32)] * 2, args=(x, wgt, wut, wd, g, b),
        scratch_shapes=[pltpu.VMEM((tm, d), BF16), pltpu.VMEM((tm, d), F32)],
        sem=("parallel", "arbitrary"), gather=gather)


def _ffn_bwd_x(dy, u, x, wgt, wut, wd, g, name, tm=512, exchange=()):
    s, d = x.shape
    nt = s // tm
    ffp = N_DEV * FF_PAD

    def body(dy_ref, u_ref, x_ref, wg_ref, wu_ref, wd_ref, g_ref,
             dx_ref, xb, df_ref, da_ref, db_ref, h_ref, dg_ref, dbl_ref,
             dfb, du_s, acc, g8, b8):
        i = pl.program_id(0)
        k = pl.program_id(1)

        @pl.when(k == 0)
        def _():
            dy_ = dy_ref[...]
            du, xhat = _ln_bwd_math(dy_, u_ref[...], g_ref[...])
            du_s[...] = du
            dfb[...] = (0.5 * du).astype(BF16)
            df_ref[...] = dfb[...]
            xb[...] = x_ref[...].astype(BF16)

            @pl.when(i == 0)
            def _():
                g8[...] = jnp.zeros_like(g8)
                b8[...] = jnp.zeros_like(b8)

            g8[...] += _rowsum8(dy_ * xhat)
            b8[...] += _rowsum8(dy_)

        a = _dot(xb[...], wg_ref[...], NT)
        bb = _dot(xb[...], wu_ref[...], NT)
        sig = _sigmoid(a)
        sa = a * sig
        h_ref[...] = (sa * bb).astype(BF16)
        dh = _dot(dfb[...], wd_ref[...], NT)
        da = (dh * bb * (sig * (1.0 + a * (1.0 - sig)))).astype(BF16)
        db = (dh * sa).astype(BF16)
        da_ref[...] = da
        db_ref[...] = db
        part = _dot(da, wg_ref[...], NN) + _dot(db, wu_ref[...], NN)

        @pl.when(k == 0)
        def _():
            acc[...] = part

        @pl.when(k > 0)
        def _():
            acc[...] += part

        @pl.when(k == N_PAIR - 1)
        def _():
            dx_ref[...] = ALPHA * du_s[...] + acc[...]

        @pl.when((k == N_PAIR - 1) & (i == nt - 1))
        def _():
            dg_ref[...] = jnp.sum(g8[...], axis=0, keepdims=True)
            dbl_ref[...] = jnp.sum(b8[...], axis=0, keepdims=True)

    row = pl.BlockSpec((tm, d), lambda i, k: (i, 0))
    vec = pl.BlockSpec((1, d), lambda i, k: (0, 0))
    w_in = pl.BlockSpec((None, FF_PAIR, d), lambda i, k: (k, 0, 0))
    hid = pl.BlockSpec((tm, FF_PAIR), lambda i, k: (i, k))
    return _call(
        body, name=name, grid=(nt, N_PAIR),
        in_specs=[row, row, row, w_in, w_in, w_in, vec],
        out_specs=[row, row, row, hid, hid, hid, vec, vec],
        out_shape=[jax.ShapeDtypeStruct((s, d), F32), jax.ShapeDtypeStruct((s, d), BF16),
                   jax.ShapeDtypeStruct((s, d), BF16),
                   jax.ShapeDtypeStruct((s, ffp), BF16), jax.ShapeDtypeStruct((s, ffp), BF16),
                   jax.ShapeDtypeStruct((s, ffp), BF16),
                   jax.ShapeDtypeStruct((1, d), F32), jax.ShapeDtypeStruct((1, d), F32)],
        args=(dy, u, x, wgt, wut, wd, g),
        scratch_shapes=[pltpu.VMEM((tm, d), BF16), pltpu.VMEM((tm, d), F32),
                        pltpu.VMEM((tm, d), F32), pltpu.VMEM((8, d), F32), pltpu.VMEM((8, d), F32)],
        sem=("arbitrary", "arbitrary"), exchange=exchange)


def _ffn_bwd_w(tok, hid, name, *, down, tm=512, exchange=()):
    s, d = tok.shape
    nt = s // tm

    def body(t_ref, h_ref, dw_ref, acc):
        i = pl.program_id(1)
        part = _dot(h_ref[...], t_ref[...], TN) if down else _dot(t_ref[...], h_ref[...], TN)

        @pl.when(i == 0)
        def _():
            acc[...] = part

        @pl.when(i > 0)
        def _():
            acc[...] += part

        @pl.when(i == nt - 1)
        def _():
            for j in range(2):
                lo = j * FF_PAD
                dw_ref[j] = (acc[lo:lo + FF_SHARD, :] if down else acc[:, lo:lo + FF_SHARD]).astype(BF16)

    blk = (FF_SHARD, d) if down else (d, FF_SHARD)
    return _call(
        body, name=name, grid=(N_PAIR, nt),
        in_specs=[pl.BlockSpec((tm, d), lambda k, i: (i, 0)), pl.BlockSpec((tm, FF_PAIR), lambda k, i: (i, k))],
        out_specs=[pl.BlockSpec((2,) + blk, lambda k, i: (k, 0, 0))],
        out_shape=[jax.ShapeDtypeStruct((N_DEV,) + blk, BF16)], args=(tok, hid),
        scratch_shapes=[pltpu.VMEM((FF_PAIR, d) if down else (d, FF_PAIR), F32)],
        sem=("parallel", "arbitrary"), exchange=exchange)


def _bucket_tables():
    qi = np.arange(BLK)[:, None]
    ki = np.arange(2 * BLK)[None, :]
    off = qi + BLK - ki
    out = []
    for window, dil in DILATED:
        n_keys = window // dil
        dist = dil * np.clip(off, 0, n_keys)
        exact = REL_BUCKETS // 2
        df = np.maximum(dist, 1).astype(np.float32)
        large = exact + (np.log(df / np.float32(exact)) / np.float32(math.log(REL_MAX_DIST / exact))
                         * np.float32(REL_BUCKETS - exact)).astype(np.int32)
        large = np.minimum(large, REL_BUCKETS - 1)
        bucket = np.where(dist < exact, dist, large).astype(np.int32)
        band = (off >= 0) & (off <= n_keys)
        out.append(np.where(band, bucket, -1))
    return np.stack(out).astype(np.int32)


def _bias_fwd(rel_bias, buckets, name="bias_fwd"):
    def body(tbl_ref, bkt_ref, out_ref):
        bkt = bkt_ref[...]
        for h in range(ATT_HEADS):
            acc = jnp.full((BLK, 2 * BLK), NEG, F32)
            for bb in range(REL_BUCKETS):
                acc = jnp.where(bkt == bb, tbl_ref[bb, h], acc)
            out_ref[h] = acc

    nbr = len(DILATED)
    return pl.pallas_call(
        body, name=name, grid=(nbr,),
        in_specs=[pl.BlockSpec(memory_space=pltpu.SMEM),
                  pl.BlockSpec((None, BLK, 2 * BLK), lambda r: (r, 0, 0))],
        out_specs=pl.BlockSpec((None, ATT_HEADS, BLK, 2 * BLK), lambda r: (r, 0, 0, 0)),
        out_shape=jax.ShapeDtypeStruct((nbr, ATT_HEADS, BLK, 2 * BLK), F32),
        compiler_params=_params("parallel"),
    )(rel_bias, buckets)


def _bias_bwd(dbias, buckets, name="bias_bwd"):
    nbr = len(DILATED)

    def body(db_ref, bkt_ref, out_ref):
        r = pl.program_id(0)

        @pl.when(r == 0)
        def _():
            out_ref[...] = jnp.zeros_like(out_ref)

        bkt = bkt_ref[...]
        rowi = lax.broadcasted_iota(jnp.int32, (REL_BUCKETS, LANES), 0)
        coli = lax.broadcasted_iota(jnp.int32, (REL_BUCKETS, LANES), 1)
        acc = jnp.zeros((REL_BUCKETS, LANES), F32)
        for h in range(ATT_HEADS):
            x = db_ref[h]
            for bb in range(REL_BUCKETS):
                part = jnp.sum(jnp.where(bkt == bb, x, 0.0), axis=0, keepdims=True)
                tot = jnp.sum(part, axis=1, keepdims=True)
                acc = acc + jnp.where((rowi == bb) & (coli == h), tot, 0.0)
        out_ref[...] += acc

    return pl.pallas_call(
        body, name=name, grid=(nbr,),
        in_specs=[pl.BlockSpec((None, ATT_HEADS, BLK, 2 * BLK), lambda r: (r, 0, 0, 0)),
                  pl.BlockSpec((None, BLK, 2 * BLK), lambda r: (r, 0, 0))],
        out_specs=pl.BlockSpec((REL_BUCKETS, LANES), lambda r: (0, 0)),
        out_shape=jax.ShapeDtypeStruct((REL_BUCKETS, LANES), F32),
        compiler_params=_params("arbitrary"),
    )(dbias, buckets)


def _stack_heads(pair, lo):
    return jnp.concatenate([jnp.where(lo, pair, 0.0), jnp.where(lo, 0.0, pair)], axis=0)


def _head_cols(pair, lo, reduce):
    fill = -jnp.inf if reduce is jnp.max else 0.0
    return jnp.concatenate([reduce(jnp.where(lo, pair, fill), axis=1, keepdims=True),
                            reduce(jnp.where(lo, fill, pair), axis=1, keepdims=True)], axis=0)


def _unstack_heads(x2, lo):
    return jnp.where(lo, x2[:BLK], x2[BLK:])


def _att_scores(q2, kk, bias2, first_ok):
    sc = _dot(q2, kk, NT) * (64 ** -0.5) + bias2
    return jnp.where(first_ok, sc, NEG)


DIL_TILE = 2048
DIL_COLS = ATT_W // LANES
DIL_UNROLL = 4


def _dil_rows(dil, n, r, base=0):
    start = base + n * (BLK * dil) + r
    return pl.ds(start, BLK, stride=dil) if dil > 1 else pl.ds(start, BLK)


def _dil_in_specs(tile_of):
    cur = lambda col: pl.BlockSpec((DIL_TILE, LANES), lambda p, i: (tile_of(i), col * DIL_COLS + p))
    prev = lambda col: pl.BlockSpec((DIL_TILE, LANES), lambda p, i: (jnp.maximum(tile_of(i) - 1, 0), col * DIL_COLS + p))
    bias = pl.BlockSpec((len(DILATED), None, 2 * BLK, 2 * BLK), lambda p, i: (0, p, 0, 0))
    return [cur(0), prev(1), cur(1), prev(2), cur(2), bias]


def _pair_bias(biasm):
    return biasm.reshape(len(DILATED), DIL_COLS, 2 * BLK, 2 * BLK)


def _dil_fwd(proj, biasm, name="dil_fwd", gather=()):
    s = proj.shape[0]
    nt = s // DIL_TILE
    tt = DIL_TILE

    def body(q_ref, kp_ref, kc_ref, vp_ref, vc_ref, bias_ref, att_ref, lse_ref, k2, v2, ob, lb):
        t = pl.program_id(1)
        k2[0:tt, :] = kp_ref[...]
        k2[tt:2 * tt, :] = kc_ref[...]
        v2[0:tt, :] = vp_ref[...]
        v2[tt:2 * tt, :] = vc_ref[...]
        lo = lax.broadcasted_iota(jnp.int32, (BLK, LANES), 1) < 64
        kidx = lax.broadcasted_iota(jnp.int32, (2 * BLK, 2 * BLK), 1)
        for b, (_, dil) in enumerate(DILATED):
            nblk = tt // (BLK * dil)

            def step(j, carry, b=b, dil=dil, nblk=nblk):
                r, n = j % dil, j // dil
                cur, prev = _dil_rows(dil, n, r, tt), _dil_rows(dil, n - 1, r, tt)
                here = _dil_rows(dil, n, r)
                q_pair = q_ref[here, :]
                kk = jnp.concatenate([k2[prev, :], k2[cur, :]], axis=0).astype(BF16)
                vv = jnp.concatenate([v2[prev, :], v2[cur, :]], axis=0).astype(BF16)
                first_ok = (t > 0) | (n > 0) | (kidx >= BLK)
                sc = _att_scores(_stack_heads(q_pair, lo).astype(BF16), kk, bias_ref[b], first_ok)
                mx = jnp.max(sc, axis=1, keepdims=True)
                pe = jnp.exp(sc - mx)
                l = jnp.sum(pe, axis=1, keepdims=True)
                ob.at[b][here, :] = _unstack_heads(_dot(pe.astype(BF16), vv, NN) / l, lo)
                lb.at[b][here, :] = _unstack_heads(jnp.broadcast_to(mx + jnp.log(l), (2 * BLK, LANES)), lo)
                return carry

            lax.fori_loop(0, tt // BLK, step, 0, unroll=DIL_UNROLL)
        l0, l1, l2 = lb[0], lb[1], lb[2]
        mx = jnp.maximum(jnp.maximum(l0, l1), l2)
        e0, e1, e2 = jnp.exp(l0 - mx), jnp.exp(l1 - mx), jnp.exp(l2 - mx)
        tot = e0 + e1 + e2
        att_ref[...] = (e0 * ob[0] + e1 * ob[1] + e2 * ob[2]) / tot
        lse_ref[...] = mx + jnp.log(tot)

    out = pl.BlockSpec((tt, LANES), lambda p, i: (i, p))
    return _call(
        body, name=name, grid=(DIL_COLS, nt), in_specs=_dil_in_specs(lambda i: i), out_specs=[out, out],
        out_shape=[jax.ShapeDtypeStruct((s, ATT_W), F32)] * 2, args=(proj, proj, proj, proj, proj, _pair_bias(biasm)),
        scratch_shapes=[pltpu.VMEM((2 * tt, LANES), F32), pltpu.VMEM((2 * tt, LANES), F32),
                        pltpu.VMEM((len(DILATED), tt, LANES), F32), pltpu.VMEM((len(DILATED), tt, LANES), F32)],
        sem=("parallel", "parallel"), gather=gather)


def _dil_bwd(proj, biasm, lse, att, dcat, name="dil_bwd"):
    s = proj.shape[0]
    nt = s // DIL_TILE
    tt = DIL_TILE
    nbr = len(DILATED)

    def body(q_ref, kp_ref, kc_ref, vp_ref, vc_ref, bias_ref, lse_ref, att_ref, datt_ref,
             dq_ref, dk_ref, dv_ref, dbias_ref, k2, v2, dqa, dka, dva, kcar, vcar):
        i = pl.program_id(1)
        t = nt - 1 - i
        k2[0:tt, :] = kp_ref[...]
        k2[tt:2 * tt, :] = kc_ref[...]
        v2[0:tt, :] = vp_ref[...]
        v2[tt:2 * tt, :] = vc_ref[...]

        @pl.when(i == 0)
        def _():
            kcar[...] = jnp.zeros_like(kcar)
            vcar[...] = jnp.zeros_like(vcar)
            dbias_ref[...] = jnp.zeros_like(dbias_ref)

        dqa[...] = jnp.zeros_like(dqa)
        dka[0:tt, :] = jnp.zeros((tt, LANES), F32)
        dva[0:tt, :] = jnp.zeros((tt, LANES), F32)
        dka[tt:2 * tt, :] = kcar[...]
        dva[tt:2 * tt, :] = vcar[...]
        lo = lax.broadcasted_iota(jnp.int32, (BLK, LANES), 1) < 64
        kidx = lax.broadcasted_iota(jnp.int32, (2 * BLK, 2 * BLK), 1)
        for b, (_, dil) in enumerate(DILATED):
            nblk = tt // (BLK * dil)

            def step(j, carry, b=b, dil=dil, nblk=nblk):
                r, n = j % dil, j // dil
                cur, prev = _dil_rows(dil, n, r, tt), _dil_rows(dil, n - 1, r, tt)
                here = _dil_rows(dil, n, r)
                q_pair = q_ref[here, :]
                kk = jnp.concatenate([k2[prev, :], k2[cur, :]], axis=0).astype(BF16)
                vv = jnp.concatenate([v2[prev, :], v2[cur, :]], axis=0).astype(BF16)
                first_ok = (t > 0) | (n > 0) | (kidx >= BLK)
                dat_pair = datt_ref[here, :]
                q2 = _stack_heads(q_pair, lo).astype(BF16)
                dom = _stack_heads(dat_pair, lo).astype(BF16)
                sc = _att_scores(q2, kk, bias_ref[b], first_ok)
                pr = jnp.exp(sc - _head_cols(lse_ref[here, :], lo, jnp.max))
                ds = pr * (_dot(dom, vv, NT) - _head_cols(dat_pair * att_ref[here, :], lo, jnp.sum))
                dbias_ref[b] += ds
                dsb = (ds * (64 ** -0.5)).astype(BF16)
                dk2 = _dot(dsb, q2, TN)
                dv2 = _dot(pr.astype(BF16), dom, TN)
                dqa[here, :] += _unstack_heads(_dot(dsb, kk, NN), lo)
                dka[prev, :] += dk2[:BLK]
                dka[cur, :] += dk2[BLK:]
                dva[prev, :] += dv2[:BLK]
                dva[cur, :] += dv2[BLK:]
                return carry

            lax.fori_loop(0, tt // BLK, step, 0, unroll=DIL_UNROLL)
        dq_ref[...] = dqa[...].astype(BF16)
        dk_ref[...] = dka[tt:2 * tt, :].astype(BF16)
        dv_ref[...] = dva[tt:2 * tt, :].astype(BF16)
        kcar[...] = dka[0:tt, :]
        vcar[...] = dva[0:tt, :]

    rev = lambda i: nt - 1 - i
    out = pl.BlockSpec((tt, LANES), lambda p, i: (rev(i), p))
    two = lambda: pltpu.VMEM((2 * tt, LANES), F32)
    one = lambda: pltpu.VMEM((tt, LANES), F32)
    return pl.pallas_call(
        body, name=name, grid=(DIL_COLS, nt),
        in_specs=_dil_in_specs(rev) + [out, out, out],
        out_specs=[out, out, out, pl.BlockSpec((nbr, None, 2 * BLK, 2 * BLK), lambda p, i: (0, p, 0, 0))],
        out_shape=[jax.ShapeDtypeStruct((s, ATT_W), BF16)] * 3
        + [jax.ShapeDtypeStruct((nbr, DIL_COLS, 2 * BLK, 2 * BLK), F32)],
        scratch_shapes=[two(), two(), one(), two(), two(), one(), one()],
        compiler_params=_params("arbitrary", "arbitrary"),
    )(proj, proj, proj, proj, proj, _pair_bias(biasm), lse, att, dcat)


QK_COL0 = (3 * ATT_W) // ATT_W


def _conv_shifted(prev, cur, j, row):
    sh = CONV_K - 1 - j
    if sh == 0:
        return cur
    return jnp.where(row < sh, pltpu.roll(prev, sh, 0), pltpu.roll(cur, sh, 0))


def _conv_z(prev, cur, w_ref, b_ref, row):
    z = b_ref[...] + cur * w_ref[CONV_K - 1:CONV_K, :]
    for j in range(CONV_K - 1):
        z = z + _conv_shifted(prev, cur, j, row) * w_ref[j:j + 1, :]
    return z


def _conv_fwd(proj, conv_w, conv_b, name="conv_fwd", tm=512):
    s = proj.shape[0]
    w = ATT_W

    def body(prev_ref, cur_ref, w_ref, b_ref, o_ref):
        i = pl.program_id(1)
        row = lax.broadcasted_iota(jnp.int32, (tm, w), 0)
        prev = jnp.where(i > 0, prev_ref[...], 0.0)
        z = _conv_z(prev, cur_ref[...], w_ref, b_ref, row)
        o_ref[...] = z * _sigmoid(z)

    return pl.pallas_call(
        body, name=name, grid=(2, s // tm),
        in_specs=[pl.BlockSpec((tm, w), lambda j, i: (jnp.maximum(i - 1, 0), QK_COL0 + j)),
                  pl.BlockSpec((tm, w), lambda j, i: (i, QK_COL0 + j)),
                  pl.BlockSpec((CONV_K, w), lambda j, i: (0, j)),
                  pl.BlockSpec((1, w), lambda j, i: (0, j))],
        out_specs=pl.BlockSpec((tm, w), lambda j, i: (i, j)),
        out_shape=jax.ShapeDtypeStruct((s, 2 * ML_W), F32),
        compiler_params=_params("parallel", "parallel"),
    )(proj, proj, conv_w, conv_b)


def _conv_bwd(proj, dqk, conv_w, conv_b, name="conv_bwd", tm=512):
    s = proj.shape[0]
    w = ATT_W
    nt = s // tm

    def body(xp_ref, xc_ref, xn_ref, dc_ref, dn_ref, w_ref, b_ref, dx_ref, dw_ref, db_ref):
        i = pl.program_id(1)
        row = lax.broadcasted_iota(jnp.int32, (tm, w), 0)
        prev = jnp.where(i > 0, xp_ref[...], 0.0)
        cur = xc_ref[...]

        def dz_of(pv, cv, dy):
            z = _conv_z(pv, cv, w_ref, b_ref, row)
            sig = _sigmoid(z)
            return dy * (sig * (1.0 + z * (1.0 - sig)))

        dzc = dz_of(prev, cur, dc_ref[...])
        dzn = jnp.where(i < nt - 1, dz_of(cur, xn_ref[...], dn_ref[...]), 0.0)
        dx = dzc * w_ref[CONV_K - 1:CONV_K, :]
        for j in range(CONV_K - 1):
            sh = CONV_K - 1 - j
            up = jnp.where(row >= tm - sh, pltpu.roll(dzn, tm - sh, 0), pltpu.roll(dzc, tm - sh, 0))
            dx = dx + up * w_ref[j:j + 1, :]
        dx_ref[...] = dx

        @pl.when(i == 0)
        def _():
            dw_ref[...] = jnp.zeros_like(dw_ref)
            db_ref[...] = jnp.zeros_like(db_ref)

        for j in range(CONV_K):
            dw_ref[j:j + 1, :] += jnp.sum(dzc * _conv_shifted(prev, cur, j, row), axis=0, keepdims=True)
        db_ref[...] += jnp.sum(dzc, axis=0, keepdims=True)

    xs = lambda f: pl.BlockSpec((tm, w), lambda j, i: (f(i), QK_COL0 + j))
    ds = lambda f: pl.BlockSpec((tm, w), lambda j, i: (f(i), j))
    return pl.pallas_call(
        body, name=name, grid=(2, nt),
        in_specs=[xs(lambda i: jnp.maximum(i - 1, 0)), xs(lambda i: i), xs(lambda i: jnp.minimum(i + 1, nt - 1)),
                  ds(lambda i: i), ds(lambda i: jnp.minimum(i + 1, nt - 1)),
                  pl.BlockSpec((CONV_K, w), lambda j, i: (0, j)), pl.BlockSpec((1, w), lambda j, i: (0, j))],
        out_specs=[ds(lambda i: i), pl.BlockSpec((CONV_K, w), lambda j, i: (0, j)),
                   pl.BlockSpec((1, w), lambda j, i: (0, j))],
        out_shape=[jax.ShapeDtypeStruct((s, 2 * ML_W), F32), jax.ShapeDtypeStruct((CONV_K, 2 * ML_W), F32),
                   jax.ShapeDtypeStruct((1, 2 * ML_W), F32)],
        compiler_params=_params("parallel", "arbitrary"),
    )(proj, proj, proj, dqk, dqk, conv_w, conv_b)


def _bf16_mm(dims_fwd):
    @jax.custom_vjp
    def mm(a, b):
        return _dot(a.astype(BF16), b.astype(BF16), dims_fwd)

    def fwd(a, b):
        return mm(a, b), (a, b)

    def bwd(res, g):
        a, b = res
        if dims_fwd is NN:
            return _mm_nt(g, b), _mm_tn(a, g)
        if dims_fwd is NT:
            return _mm_nn(g, b), _mm_tn(g, a)
        return _mm_nt(b, g), _mm_nn(a, g)

    mm.defvjp(fwd, bwd)
    return mm


_mm_nn = _bf16_mm(NN)
_mm_nt = _bf16_mm(NT)
_mm_tn = _bf16_mm(TN)


def _tri(lower):
    r = lax.broadcasted_iota(jnp.int32, (CHUNK, CHUNK), 0)
    c = lax.broadcasted_iota(jnp.int32, (CHUNK, CHUNK), 1)
    return ((r >= c) if lower else (r <= c)).astype(F32)


@jax.custom_vjp
def _cumsum_rows(x):
    return lax.dot_general(_tri(True), x, NN, precision=lax.Precision.HIGHEST, preferred_element_type=F32)


def _cumsum_fwd(x):
    return _cumsum_rows(x), None


def _cumsum_bwd(_, g):
    return (lax.dot_general(_tri(False), g, NN, precision=lax.Precision.HIGHEST, preferred_element_type=F32),)


_cumsum_rows.defvjp(_cumsum_fwd, _cumsum_bwd)


def _abs(x):
    return jnp.where(x >= 0, x, -x)


def _log_sigmoid(x):
    return jnp.minimum(x, 0.0) - jnp.log(1.0 + jnp.exp(-_abs(x)))


def _pick_col(x, lane):
    sel = lax.broadcasted_iota(jnp.int32, x.shape, 1) == lane
    return jnp.sum(jnp.where(sel, x, 0.0), axis=1, keepdims=True)


def _pick_row(x, r):
    sel = lax.broadcasted_iota(jnp.int32, x.shape, 0) == r
    return jnp.sum(jnp.where(sel, x, 0.0), axis=0, keepdims=True)


def _mlstm_chunk(qs, ks, vs, oms, gates, gate_bias, mlg, cs, ns, ms):
    gb = gates + gate_bias
    cum = _cumsum_rows(_log_sigmoid(gb))
    gbt = gb.T
    cumt = cum.T
    causal = lax.broadcasted_iota(jnp.int32, (CHUNK, CHUNK), 0) >= lax.broadcasted_iota(jnp.int32, (CHUNK, CHUNK), 1)
    ys, c_out, n_out, m_out = [], [], [], []
    for h in range(ML_HEADS):
        q, v, om, c, n, m = qs[h], vs[h], oms[h], cs[h], ns[h], ms[h]
        k = ks[h] * (ML_HD ** -0.5)
        ig_col = _pick_col(gb, h)
        ig_row = _pick_row(gbt, h)
        b_col = _pick_col(cum, ML_HEADS + h)
        b_row = _pick_row(cumt, ML_HEADS + h)
        g = _pick_row(b_col, CHUNK - 1)
        a = g - b_col + ig_col
        m_loc = jnp.max(a, axis=0, keepdims=True)
        wa = jnp.exp(a - m_loc)
        c_loc = _mm_tn(wa * v, k)
        n_loc = jnp.sum(wa * k, axis=0, keepdims=True)
        m_new = jnp.maximum(g + m, m_loc)
        sp = jnp.exp(g + m - m_new)
        sl = jnp.exp(m_loc - m_new)
        c_out.append(sp * c + sl * c_loc)
        n_out.append(sp * n + sl * n_loc)
        m_out.append(m_new)
        d_log = jnp.where(causal, b_col - b_row + ig_row, -jnp.inf)
        e_log = b_col + m
        m_t = jnp.maximum(e_log, jnp.max(d_log, axis=1, keepdims=True))
        d_w = jnp.exp(d_log - m_t)
        e_w = jnp.exp(e_log - m_t)
        s_qk = _mm_nt(q, k) * d_w
        num = e_w * _mm_nt(q, c) + _mm_nn(s_qk, v)
        den = e_w * jnp.sum(q * n, axis=1, keepdims=True) + jnp.sum(s_qk, axis=1, keepdims=True)
        hh = num / jnp.maximum(_abs(den), jnp.exp(-m_t))
        hg = _sigmoid(om) * hh
        mu = jnp.mean(hg, axis=1, keepdims=True)
        hc = hg - mu
        var = jnp.mean(hc * hc, axis=1, keepdims=True)
        ys.append(hc * lax.rsqrt(var + LN_EPS) * mlg[h])
    return ys, c_out, n_out, m_out


V_COL = 5
O_COL = 6


def _mlstm_fwd(qk, proj, gates, gate_bias, mlg, name="mlstm_fwd", gather=()):
    s = qk.shape[0]
    nc = s // CHUNK

    def body(q_ref, k_ref, v_ref, o_ref, g_ref, gb_ref, mlg_ref, y_ref, cp_ref, np_ref, mp_ref, c_s, n_s, m_s):
        ci = pl.program_id(0)

        @pl.when(ci == 0)
        def _():
            c_s[...] = jnp.zeros_like(c_s)
            n_s[...] = jnp.zeros_like(n_s)
            m_s[...] = jnp.zeros_like(m_s)

        cp_ref[...] = c_s[...]
        np_ref[...] = n_s[...]
        mp_ref[...] = m_s[...]
        hs = lambda ref: [ref[:, LANES * h:LANES * (h + 1)] for h in range(ML_HEADS)]
        ys, c_new, n_new, m_new = _mlstm_chunk(
            hs(q_ref), hs(k_ref), hs(v_ref), hs(o_ref), g_ref[...], gb_ref[...], hs(mlg_ref),
            [c_s[h] for h in range(ML_HEADS)], [n_s[h:h + 1, :] for h in range(ML_HEADS)],
            [m_s[h:h + 1, 0:1] for h in range(ML_HEADS)])
        for h in range(ML_HEADS):
            y_ref[:, LANES * h:LANES * (h + 1)] = ys[h]
            c_s[h] = c_new[h]
            n_s[h:h + 1, :] = n_new[h]
            m_s[h:h + 1, :] = jnp.broadcast_to(m_new[h], (1, LANES))

    blk = lambda col: pl.BlockSpec((CHUNK, ML_W), lambda ci: (ci, col))
    vec = lambda w: pl.BlockSpec((1, w), lambda ci: (0, 0))
    return _call(
        body, name=name, grid=(nc,), args=(qk, qk, proj, proj, gates, gate_bias, mlg), sem=("arbitrary",), gather=gather,
        in_specs=[blk(0), blk(1), blk(V_COL), blk(O_COL), pl.BlockSpec((CHUNK, LANES), lambda ci: (ci, 0)),
                  vec(LANES), vec(ML_W)],
        out_specs=[blk(0), pl.BlockSpec((None, ML_HEADS, ML_HD, ML_HD), lambda ci: (ci, 0, 0, 0)),
                   pl.BlockSpec((None, 8, LANES), lambda ci: (ci, 0, 0)),
                   pl.BlockSpec((None, 8, LANES), lambda ci: (ci, 0, 0))],
        out_shape=[jax.ShapeDtypeStruct((s, ML_W), F32), jax.ShapeDtypeStruct((nc, ML_HEADS, ML_HD, ML_HD), F32),
                   jax.ShapeDtypeStruct((nc, 8, LANES), F32), jax.ShapeDtypeStruct((nc, 8, LANES), F32)],
        scratch_shapes=[pltpu.VMEM((ML_HEADS, ML_HD, ML_HD), F32), pltpu.VMEM((8, LANES), F32),
                        pltpu.VMEM((8, LANES), F32)])


def _mlstm_bwd(qk, proj, gates, gate_bias, mlg, cprev, nprev, mprev, dy, name="mlstm_bwd", exchange=()):
    s = qk.shape[0]
    nc = s // CHUNK

    def body(q_ref, k_ref, v_ref, o_ref, g_ref, gb_ref, mlg_ref, cp_ref, np_ref, mp_ref, dy_ref,
             dqk_ref, dv_ref, do_ref, dg_ref, dgb_ref, dmlg_ref, dc_s, dn_s, dm_s, gb8, mg8):
        ci = pl.program_id(0)

        @pl.when(ci == 0)
        def _():
            dc_s[...] = jnp.zeros_like(dc_s)
            dn_s[...] = jnp.zeros_like(dn_s)
            dm_s[...] = jnp.zeros_like(dm_s)
            gb8[...] = jnp.zeros_like(gb8)
            mg8[...] = jnp.zeros_like(mg8)

        hs = lambda ref: [ref[:, LANES * h:LANES * (h + 1)] for h in range(ML_HEADS)]
        prim = (hs(q_ref), hs(k_ref), hs(v_ref), hs(o_ref), g_ref[...], gb_ref[...], hs(mlg_ref),
                [cp_ref[h] for h in range(ML_HEADS)], [np_ref[h:h + 1, :] for h in range(ML_HEADS)],
                [mp_ref[h:h + 1, 0:1] for h in range(ML_HEADS)])
        _, vjp = jax.vjp(_mlstm_chunk, *prim)
        cot = (hs(dy_ref), [dc_s[h] for h in range(ML_HEADS)], [dn_s[h:h + 1, :] for h in range(ML_HEADS)],
               [dm_s[h:h + 1, 0:1] for h in range(ML_HEADS)])
        dqs, dks, dvs, dos, dg, dgb, dmlg, dcs, dns, dms = vjp(cot)
        dg_ref[...] = dg
        gb8[0:1, :] += dgb
        for h in range(ML_HEADS):
            sl = slice(LANES * h, LANES * (h + 1))
            dqk_ref[:, sl] = dqs[h]
            dqk_ref[:, ML_W + LANES * h:ML_W + LANES * (h + 1)] = dks[h]
            dv_ref[:, sl] = dvs[h]
            do_ref[:, sl] = dos[h]
            mg8[0:1, sl] += dmlg[h]
            dc_s[h] = dcs[h]
            dn_s[h:h + 1, :] = dns[h]
            dm_s[h:h + 1, :] = jnp.broadcast_to(dms[h], (1, LANES))

        @pl.when(ci == nc - 1)
        def _():
            dgb_ref[...] = gb8[0:1, :]
            dmlg_ref[...] = mg8[0:1, :]

    rev = lambda ci: nc - 1 - ci
    blk = lambda col: pl.BlockSpec((CHUNK, ML_W), lambda ci: (rev(ci), col))
    vec = lambda w: pl.BlockSpec((1, w), lambda ci: (0, 0))
    st8 = pl.BlockSpec((None, 8, LANES), lambda ci: (rev(ci), 0, 0))
    gsp = pl.BlockSpec((CHUNK, LANES), lambda ci: (rev(ci), 0))
    return _call(
        body, name=name, grid=(nc,), sem=("arbitrary",), exchange=exchange,
        args=(qk, qk, proj, proj, gates, gate_bias, mlg, cprev, nprev, mprev, dy),
        in_specs=[blk(0), blk(1), blk(V_COL), blk(O_COL), gsp, vec(LANES), vec(ML_W),
                  pl.BlockSpec((None, ML_HEADS, ML_HD, ML_HD), lambda ci: (rev(ci), 0, 0, 0)), st8, st8, blk(1)],
        out_specs=[pl.BlockSpec((CHUNK, 2 * ML_W), lambda ci: (rev(ci), 0)), blk(0), blk(0), gsp, vec(LANES), vec(ML_W)],
        out_shape=[jax.ShapeDtypeStruct((s, 2 * ML_W), F32),
                   jax.ShapeDtypeStruct((s, ML_W), F32), jax.ShapeDtypeStruct((s, ML_W), F32),
                   jax.ShapeDtypeStruct((s, LANES), F32), jax.ShapeDtypeStruct((1, LANES), F32),
                   jax.ShapeDtypeStruct((1, ML_W), F32)],
        scratch_shapes=[pltpu.VMEM((ML_HEADS, ML_HD, ML_HD), F32), pltpu.VMEM((8, LANES), F32),
                        pltpu.VMEM((8, LANES), F32), pltpu.VMEM((8, LANES), F32), pltpu.VMEM((8, ML_W), F32)])


def _xattn_tile(qs, ks, vs):
    outs = []
    for q, k, v in zip(qs, ks, vs):
        sc = _mm_nt(q, k) * (XA_HD ** -0.5)
        mx = lax.stop_gradient(jnp.max(sc, axis=1, keepdims=True))
        pe = jnp.exp(sc - mx)
        outs.append(_mm_nn(pe / jnp.sum(pe, axis=1, keepdims=True), v))
    return outs


def _xa_heads(ref):
    return [ref[:, XA_HD * h:XA_HD * (h + 1)] for h in range(XA_HEADS)]


def _xattn_fwd(q, kv, name="xattn_fwd", tm=512):
    s, d = q.shape

    def body(q_ref, k_ref, v_ref, o_ref):
        outs = _xattn_tile(_xa_heads(q_ref), _xa_heads(k_ref), _xa_heads(v_ref))
        for h in range(XA_HEADS):
            o_ref[:, XA_HD * h:XA_HD * (h + 1)] = outs[h]

    row = pl.BlockSpec((tm, d), lambda i: (i, 0))
    return pl.pallas_call(
        body, name=name, grid=(s // tm,),
        in_specs=[row, pl.BlockSpec((MEM_LEN, d), lambda i: (0, 0)), pl.BlockSpec((MEM_LEN, d), lambda i: (0, 1))],
        out_specs=row, out_shape=jax.ShapeDtypeStruct((s, d), F32),
        compiler_params=_params("parallel"),
    )(q, kv, kv)


def _xattn_bwd(q, kv, do, name="xattn_bwd", tm=512):
    s, d = q.shape

    def body(q_ref, k_ref, v_ref, do_ref, dq_ref, dkv_ref):
        i = pl.program_id(0)
        _, vjp = jax.vjp(_xattn_tile, _xa_heads(q_ref), _xa_heads(k_ref), _xa_heads(v_ref))
        dqs, dks, dvs = vjp(_xa_heads(do_ref))

        @pl.when(i == 0)
        def _():
            dkv_ref[...] = jnp.zeros_like(dkv_ref)

        for h in range(XA_HEADS):
            sl = slice(XA_HD * h, XA_HD * (h + 1))
            dq_ref[:, sl] = dqs[h]
            dkv_ref[:, sl] += dks[h]
            dkv_ref[:, d + XA_HD * h:d + XA_HD * (h + 1)] += dvs[h]

    row = pl.BlockSpec((tm, d), lambda i: (i, 0))
    return pl.pallas_call(
        body, name=name, grid=(s // tm,),
        in_specs=[row, pl.BlockSpec((MEM_LEN, d), lambda i: (0, 0)), pl.BlockSpec((MEM_LEN, d), lambda i: (0, 1)), row],
        out_specs=[row, pl.BlockSpec((MEM_LEN, 2 * d), lambda i: (0, 0))],
        out_shape=[jax.ShapeDtypeStruct((s, d), F32), jax.ShapeDtypeStruct((MEM_LEN, 2 * d), F32)],
        compiler_params=_params("arbitrary"),
    )(q, kv, kv, do)


def _loss_head(y, target, name="loss_head", tm=512):
    s, d = y.shape
    nt = s // tm

    def body(y_ref, t_ref, dy_ref, loss_ref, acc):
        i = pl.program_id(0)
        err = y_ref[...] - t_ref[...]
        dy_ref[...] = err * (1.0 / d)

        @pl.when(i == 0)
        def _():
            acc[...] = jnp.zeros_like(acc)

        acc[...] += _rowsum8(err * err)

        @pl.when(i == nt - 1)
        def _():
            tot = jnp.sum(jnp.sum(acc[...], axis=0, keepdims=True), axis=1, keepdims=True)
            loss_ref[...] = jnp.broadcast_to(tot * (0.5 / d), (1, LANES))

    row = pl.BlockSpec((tm, d), lambda i: (i, 0))
    return pl.pallas_call(
        body, name=name, grid=(nt,),
        in_specs=[row, row], out_specs=[row, pl.BlockSpec((1, LANES), lambda i: (0, 0))],
        out_shape=[jax.ShapeDtypeStruct((s, d), F32), jax.ShapeDtypeStruct((1, LANES), F32)],
        scratch_shapes=[pltpu.VMEM((8, d), F32)],
        compiler_params=_params("arbitrary"),
    )(y, target)


def _adam2d(recv, w, m, v, name, layer=None):
    rows, cols = w.shape[-2:]
    fits = [t for t in range(16, rows + 1, 16) if rows % t == 0 and t * cols <= 128 * 1024]
    tr = max(fits) if fits else rows

    def body(r_ref, w_ref, m_ref, v_ref, g_ref, d_ref, mo_ref, vo_ref):
        g = r_ref[0].astype(F32)
        for j in range(1, N_DEV):
            g = g + r_ref[j].astype(F32)
        mn = ADAM_B1 * m_ref[...] + (1.0 - ADAM_B1) * g
        vn = ADAM_B2 * v_ref[...] + (1.0 - ADAM_B2) * jnp.square(g)
        m_hat = mn / (1.0 - ADAM_B1 ** ADAM_STEP)
        v_hat = vn / (1.0 - ADAM_B2 ** ADAM_STEP)
        g_ref[...] = g
        d_ref[...] = -ADAM_LR * (m_hat / (jnp.sqrt(v_hat) + ADAM_EPS) + ADAM_WD * w_ref[...])
        mo_ref[...] = mn
        vo_ref[...] = vn

    row = pl.BlockSpec((tr, cols), lambda i: (i, 0))
    if layer is None:
        wspec = row
    else:
        wspec = pl.BlockSpec((None, None, tr, cols), lambda i: (0, layer, i, 0))
    return pl.pallas_call(
        body, name=name, grid=(rows // tr,),
        in_specs=[pl.BlockSpec((N_DEV, tr, cols), lambda i: (0, i, 0)), wspec, wspec, wspec],
        out_specs=[row] * 4, out_shape=[jax.ShapeDtypeStruct((rows, cols), F32)] * 4,
        compiler_params=_params("parallel"),
    )(recv, w, m, v)


WEIGHTS = ("rel_bias", "ln_g", "ln_b", "ffn_w_gate", "ffn_w_up", "ffn_w_down", "w_in", "conv_w", "conv_b",
           "ig_bias", "fg_bias", "ml_norm_g", "w_out", "xq_w", "xkv_w", "xo_w")
SMALL = ("rel_bias", "ln_g", "ln_b", "conv_w", "conv_b", "ig_bias", "fg_bias", "ml_norm_g")
SMALL_SHAPES = {
    "rel_bias": (REL_BUCKETS, ATT_HEADS), "ln_g": (1, 4, LANES), "ln_b": (1, 4, LANES), "conv_w": (1, CONV_K, LANES),
    "conv_b": (1, 2 * ML_W), "ig_bias": (1, ML_HEADS), "fg_bias": (1, ML_HEADS), "ml_norm_g": (1, ML_W),
}
SMALL_ROWS = 8


def _pack_small(parts, lead=()):
    out = []
    for p in parts:
        p = jnp.pad(p, [(0, 0)] * len(lead) + [(0, SMALL_ROWS * LANES - p.shape[-1])])
        out.append(p.reshape(lead + (SMALL_ROWS, LANES)))
    return jnp.concatenate(out, axis=len(lead))


def _unpack_small(flat):
    out = {}
    for i, n in enumerate(SMALL):
        cnt = int(np.prod(SMALL_SHAPES[n]))
        out[n] = flat[SMALL_ROWS * i:SMALL_ROWS * (i + 1)].reshape(-1)[:cnt].reshape(SMALL_SHAPES[n])
    return out


def _split8(full, axis):
    shp = full.shape
    t = full.reshape(shp[:axis] + (N_DEV, shp[axis] // N_DEV) + shp[axis + 1:])
    return jnp.moveaxis(t, axis, 0).reshape(N_DEV, -1)


def _rep8(full):
    return jnp.broadcast_to(full.reshape(1, -1), (N_DEV, full.size))


def kernel(x, mem, rel_bias, ln_g, ln_b, ffn_w_gate, ffn_w_up, ffn_w_down, w_in, conv_w, conv_b, ig_bias, fg_bias, ml_norm_g, w_out, xq_w, xkv_w, xo_w, loss_target, m_rel_bias, m_ln_g, m_ln_b, m_ffn_w_gate, m_ffn_w_up, m_ffn_w_down, m_w_in, m_conv_w, m_conv_b, m_ig_bias, m_fg_bias, m_ml_norm_g, m_w_out, m_xq_w, m_xkv_w, m_xo_w, v_rel_bias, v_ln_g, v_ln_b, v_ffn_w_gate, v_ffn_w_up, v_ffn_w_down, v_w_in, v_conv_w, v_conv_b, v_ig_bias, v_fg_bias, v_ml_norm_g, v_w_out, v_xq_w, v_xkv_w, v_xo_w):
    w_tree = dict(rel_bias=rel_bias, ln_g=ln_g, ln_b=ln_b, ffn_w_gate=ffn_w_gate, ffn_w_up=ffn_w_up,
                  ffn_w_down=ffn_w_down, w_in=w_in, conv_w=conv_w, conv_b=conv_b, ig_bias=ig_bias, fg_bias=fg_bias,
                  ml_norm_g=ml_norm_g, w_out=w_out, xq_w=xq_w, xkv_w=xkv_w, xo_w=xo_w)
    m_tree = dict(rel_bias=m_rel_bias, ln_g=m_ln_g, ln_b=m_ln_b, ffn_w_gate=m_ffn_w_gate, ffn_w_up=m_ffn_w_up,
                  ffn_w_down=m_ffn_w_down, w_in=m_w_in, conv_w=m_conv_w, conv_b=m_conv_b, ig_bias=m_ig_bias,
                  fg_bias=m_fg_bias, ml_norm_g=m_ml_norm_g, w_out=m_w_out, xq_w=m_xq_w, xkv_w=m_xkv_w, xo_w=m_xo_w)
    v_tree = dict(rel_bias=v_rel_bias, ln_g=v_ln_g, ln_b=v_ln_b, ffn_w_gate=v_ffn_w_gate, ffn_w_up=v_ffn_w_up,
                  ffn_w_down=v_ffn_w_down, w_in=v_w_in, conv_w=v_conv_w, conv_b=v_conv_b, ig_bias=v_ig_bias,
                  fg_bias=v_fg_bias, ml_norm_g=v_ml_norm_g, w_out=v_w_out, xq_w=v_xq_w, xkv_w=v_xkv_w, xo_w=v_xo_w)
    x0 = x[0]
    pad_ff = FF_PAD - FF_SHARD
    bf = lambda t: t.astype(BF16)

    pad_rows = lambda t: jnp.pad(t, ((0, pad_ff), (0, 0)))
    ffn_shards = [(pad_rows(bf(ffn_w_gate[0, l]).T), pad_rows(bf(ffn_w_up[0, l]).T), pad_rows(bf(ffn_w_down[0, l])))
                  for l in range(2)]
    pairs = lambda t: t.reshape(N_PAIR, FF_PAIR, D_MODEL)
    w_in_shard = jnp.pad(bf(w_in[0]), ((0, 0), (0, ATT_W - W_IN_SHARD)))
    small_shard = jnp.concatenate([ln_g[0], ln_b[0], conv_w[0], jnp.zeros((4, LANES), F32)], axis=0)
    gate_bias = jnp.pad(jnp.concatenate([ig_bias, fg_bias], axis=1), ((0, 0), (0, LANES - 2 * ML_HEADS)))
    buckets = _bucket_tables()

    wg0, wu0, wd0, small_all = _gather_two_level("ffn1_weights_gather", ffn_shards[0] + (small_shard,))
    wg0, wu0, wd0 = pairs(wg0), pairs(wu0), pairs(wd0)
    unshard = lambda t: jnp.moveaxis(t, 0, 1).reshape(4, D_MODEL)
    ln_g_full, ln_b_full, conv_w_full = unshard(small_all[:, 0:4]), unshard(small_all[:, 4:8]), unshard(small_all[:, 8:12])
    lng = lambda i: ln_g_full[i:i + 1]
    lnb = lambda i: ln_b_full[i:i + 1]

    u0, x1, win_all, wout_all, xq_all, xo_all, xkv_all = _ffn_fwd(
        x0, wg0, wu0, wd0, lng(0), lnb(0), "ffn1_fwd",
        gather=(w_in_shard, bf(w_out[0]), bf(xq_w[0]), bf(xo_w[0]), bf(xkv_w[0])))
    w_in_full = jnp.moveaxis(win_all[:, :, :W_IN_SHARD], 0, 1).reshape(D_MODEL, W_IN)
    w_main = w_in_full[:, :W_IN_MAIN]
    w_gate_cols = jnp.pad(w_in_full[:, W_IN_MAIN:], ((0, 0), (0, LANES - 2 * ML_HEADS)))
    w_out_full = wout_all.reshape(D_MODEL, D_MODEL)
    xq_full = xq_all.reshape(D_MODEL, D_MODEL)
    xo_full = xo_all.reshape(D_MODEL, D_MODEL)

    proj, wg1 = _matmul(x1, w_main, "nn", "proj_fwd", tk=D_MODEL, gather=(ffn_shards[1][0],))
    gates, = _matmul(x1, w_gate_cols, "nn", "gates_fwd", tk=D_MODEL)
    biasm = _bias_fwd(rel_bias, buckets)
    att, lse, wd1 = _dil_fwd(proj, biasm, gather=(ffn_shards[1][2],))
    qk = _conv_fwd(proj, conv_w_full, conv_b)
    y_m, c_prev, n_prev, m_prev, wu1 = _mlstm_fwd(qk, proj, gates, gate_bias, ml_norm_g, gather=(ffn_shards[1][1],))
    cat = jnp.concatenate([att, y_m], axis=1)
    u1, x2 = _matmul_resid_ln(cat, w_out_full, x1, lng(1), lnb(1), "w_out_fwd")
    q_x, = _matmul(x2, xq_full, "nn", "xq_fwd", tn=D_MODEL, tk=D_MODEL)
    kv, = _matmul(mem[0], xkv_all, "nn", "xkv_fwd", tk=D_MODEL)
    o_x = _xattn_fwd(q_x, kv)
    u2, x3 = _matmul_resid_ln(o_x, xo_full, x2, lng(2), lnb(2), "xo_fwd")
    wg1, wu1, wd1 = pairs(wg1), pairs(wu1), pairs(wd1)
    u3, x4 = _ffn_fwd(x3, wg1, wu1, wd1, lng(3), lnb(3), "ffn2_fwd")
    dx4, loss_row = _loss_head(x4, loss_target[0])

    dx3, xb, df, da, db, hh, dg3, db3 = _ffn_bwd_x(dx4, u3, x3, wg1, wu1, wd1, lng(3), "ffn2_bwd_x")
    ffn2_send = (_ffn_bwd_w(xb, da, "ffn2_bwd_wg", down=False)[0], _ffn_bwd_w(xb, db, "ffn2_bwd_wu", down=False)[0],
                 _ffn_bwd_w(df, hh, "ffn2_bwd_wd", down=True)[0])

    du2, dg2, db2 = _ln_bwd(dx3, u2, lng(2), "xattn_ln_bwd")
    do_x, = _matmul(du2, xo_full, "nt", "xo_bwd_x", tn=D_MODEL, tk=D_MODEL)
    g_xo, = _matmul(o_x, du2, "tn", "xo_bwd_w", tm=D_MODEL, tn=D_MODEL, out_dtype=BF16)
    dq_x, dkv = _xattn_bwd(q_x, kv, do_x)
    g_xq, = _matmul(x2, dq_x, "tn", "xq_bwd_w", tm=D_MODEL, tn=D_MODEL, out_dtype=BF16)
    g_xkv, = _matmul(mem[0], dkv, "tn", "xkv_bwd_w", tm=D_MODEL, tn=2 * D_MODEL // N_DEV, tk=MEM_LEN,
                     out_dtype=BF16, blocked_out=True)
    dx2, = _matmul(dq_x, xq_full, "nt", "xq_bwd_x", tn=D_MODEL, tk=D_MODEL, add=du2, add_scale=ALPHA)

    du1, dg1, db1 = _ln_bwd(dx2, u1, lng(1), "mixer_ln_bwd")
    dcat, = _matmul(du1, w_out_full, "nt", "w_out_bwd_x", tn=D_MODEL, tk=D_MODEL)
    g_w_out, = _matmul(cat, du1, "tn", "w_out_bwd_w", tm=D_MODEL, tn=D_MODEL, out_dtype=BF16)
    dqk, dv_m, do_m, dgates, dgate_bias, g_mlg, *ffn2_recv = _mlstm_bwd(
        qk, proj, gates, gate_bias, ml_norm_g, c_prev, n_prev, m_prev, dcat, exchange=tuple(ffn2_send))
    dqk_pre, g_conv_w, g_conv_b = _conv_bwd(proj, dqk, conv_w_full, conv_b)
    dq_a, dk_a, dv_a, dbias = _dil_bwd(proj, biasm, lse, att, dcat)
    g_rel = _bias_bwd(dbias.reshape(biasm.shape), buckets)[:, :ATT_HEADS]
    dproj = jnp.concatenate([dq_a, dk_a, dv_a, bf(dqk_pre), bf(dv_m), bf(do_m)], axis=1)
    g_w_main, = _matmul(x1, dproj, "tn", "proj_bwd_w", tm=D_MODEL, tn=W_IN_MAIN // 2, out_dtype=BF16)
    g_w_gates, = _matmul(x1, dgates, "tn", "gates_bwd_w", tm=D_MODEL, out_dtype=BF16)
    g_w_in = jnp.concatenate([g_w_main, g_w_gates[:, :2 * ML_HEADS]], axis=1)
    dx1, = _matmul(dproj, w_main, "nt", "proj_bwd_x", tn=D_MODEL, add=du1, add_scale=ALPHA)
    dx1, = _matmul(dgates, w_gate_cols, "nt", "gates_bwd_x", tn=D_MODEL, add=dx1)

    rows8 = lambda t: t.reshape(N_DEV, D_MODEL // N_DEV, D_MODEL)
    mid_send = (rows8(g_xo), rows8(g_xq), g_xkv, rows8(g_w_out),
                jnp.moveaxis(g_w_in.reshape(D_MODEL, N_DEV, W_IN_SHARD), 1, 0))
    dx0, xb, df, da, db, hh, dg0, db0, r_xo, r_xq, r_xkv, r_w_out, r_w_in = _ffn_bwd_x(
        dx1, u0, x0, wg0, wu0, wd0, lng(0), "ffn1_bwd_x", exchange=mid_send)
    small_blocks = {
        "rel_bias": _rep8(g_rel),
        "ln_g": _split8(jnp.concatenate([dg0, dg1, dg2, dg3], axis=0), 1),
        "ln_b": _split8(jnp.concatenate([db0, db1, db2, db3], axis=0), 1),
        "conv_w": _split8(g_conv_w, 1),
        "conv_b": _rep8(g_conv_b),
        "ig_bias": _rep8(dgate_bias[:, :ML_HEADS]),
        "fg_bias": _rep8(dgate_bias[:, ML_HEADS:2 * ML_HEADS]),
        "ml_norm_g": _rep8(g_mlg),
    }
    small_send = _pack_small([small_blocks[n] for n in SMALL], lead=(N_DEV,))
    g_wg, r_small = _ffn_bwd_w(xb, da, "ffn1_bwd_wg", down=False, exchange=(small_send,))
    g_wu, r_wg = _ffn_bwd_w(xb, db, "ffn1_bwd_wu", down=False, exchange=(g_wg,))
    g_wd, r_wu = _ffn_bwd_w(df, hh, "ffn1_bwd_wd", down=True, exchange=(g_wu,))
    r_wd, = _exchange_only("ffn1_grads_exchange", exchange=(g_wd,))
    ffn1_recv = [r_wg, r_wu, r_wd]

    res = {}
    for i, n in enumerate(("ffn_w_gate", "ffn_w_up", "ffn_w_down")):
        per_layer = [_adam2d(r[i], w_tree[n], m_tree[n], v_tree[n], f"adamw_{n}_{l}", layer=l)
                     for l, r in enumerate((ffn1_recv, ffn2_recv))]
        res[n] = [jnp.stack([per_layer[0][j], per_layer[1][j]])[None] for j in range(4)]
    for n, r in (("w_in", r_w_in), ("w_out", r_w_out), ("xq_w", r_xq), ("xkv_w", r_xkv), ("xo_w", r_xo)):
        res[n] = [t[None] for t in _adam2d(r, w_tree[n][0], m_tree[n][0], v_tree[n][0], f"adamw_{n}")]
    pack = lambda tree: _pack_small([tree[n].reshape(-1) for n in SMALL])
    small = [_unpack_small(t) for t in _adam2d(r_small, pack(w_tree), pack(m_tree), pack(v_tree), "adamw_small")]
    for n in SMALL:
        res[n] = [small[j][n] for j in range(4)]

    loss = lax.psum(loss_row[0, 0], ("x", "y", "c"))
    return (loss, dx0[None], *[res[n][0] for n in WEIGHTS], *[res[n][1] for n in WEIGHTS],
            *[res[n][2] for n in WEIGHTS], *[res[n][3] for n in WEIGHTS])
```

```python
import functools
import math

import numpy as np
import jax
import jax.numpy as jnp
from jax import lax
from jax.experimental import pallas as pl
from jax.experimental.pallas import tpu as pltpu

F32 = jnp.float32
BF16 = jnp.bfloat16

N_DEV = 8
D_MODEL = 1024
D_FF = 2816
FF_SHARD = D_FF // N_DEV
FF_PAD = 384
ATT_W = 512
ATT_HEADS = 8
DILATED = ((128, 1), (512, 4), (2048, 16))
BLK = 128
ML_W = 512
ML_HEADS = 4
ML_HD = 128
CHUNK = 128
CONV_K = 4
W_IN = 3592
W_IN_SHARD = W_IN // N_DEV
W_IN_MAIN = 3584
XA_HEADS = 4
XA_HD = 256
MEM_LEN = 256
REL_BUCKETS = 32
REL_MAX_DIST = 2048
ALPHA = 2.0 ** 0.25
LN_EPS = 1e-5
NEG = -1e30
ADAM_LR = 0.001
ADAM_B1 = 0.9
ADAM_B2 = 0.999
ADAM_EPS = 1e-08
ADAM_WD = 0.01
ADAM_STEP = 10
LANES = 128
VMEM_LIMIT = 58 * 1024 * 1024

NN = (((1,), (0,)), ((), ()))
NT = (((1,), (1,)), ((), ()))
TN = (((0,), (0,)), ((), ()))


def _dot(a, b, dims):
    return lax.dot_general(a, b, dims, preferred_element_type=F32)


def _params(*sem):
    return pltpu.CompilerParams(dimension_semantics=sem, vmem_limit_bytes=VMEM_LIMIT)


def _sigmoid(x):
    return 1.0 / (1.0 + jnp.exp(-x))


def _rowsum8(x):
    t, c = x.shape
    return jnp.sum(x.reshape(t // 8, 8, c), axis=0)


def _mesh_pos():
    x, y, c = lax.axis_index("x"), lax.axis_index("y"), lax.axis_index("c")
    return x, y, c, 4 * x + 2 * y + c


def _peer(x, y, c, k):
    px = 1 - x if k & 4 else x
    py = 1 - y if k & 2 else y
    pc = 1 - c if k & 1 else c
    return (px, py, pc), 4 * px + 2 * py + pc


def _call(body, *, name, grid, in_specs, out_specs, out_shape, args, scratch_shapes=(), sem=None,
          gather=(), exchange=()):
    in_specs, out_specs, out_shape, scratch = list(in_specs), list(out_specs), list(out_shape), list(scratch_shapes)
    ng, nc = len(gather), len(gather) + len(exchange)
    if nc == 0:
        return pl.pallas_call(body, name=name, grid=grid, in_specs=in_specs, out_specs=out_specs,
                              out_shape=out_shape, scratch_shapes=scratch, compiler_params=_params(*sem))(*args)
    n_in, n_out, n_scr = len(in_specs), len(out_specs), len(scratch)

    def wrapped(*refs):
        ins, cin = refs[:n_in], refs[n_in:n_in + nc]
        outs, cout = refs[n_in + nc:n_in + nc + n_out], refs[n_in + nc + n_out:n_in + 2 * nc + n_out]
        scr = refs[n_in + 2 * nc + n_out:n_in + 2 * nc + n_out + n_scr]
        send_sems, recv_sems, loc_sems = refs[-3:]
        first, last = None, None
        for ax, extent in enumerate(grid):
            f, l = pl.program_id(ax) == 0, pl.program_id(ax) == extent - 1
            first = f if first is None else first & f
            last = l if last is None else last & l

        def copies():
            x, y, c, me = _mesh_pos()
            out = []
            for a in range(nc):
                mine = cin[a] if a < ng else cin[a].at[me]
                out.append(pltpu.make_async_copy(mine, cout[a].at[me], loc_sems.at[a]))
                for k in range(1, N_DEV):
                    peer, pidx = _peer(x, y, c, k)
                    out.append(pltpu.make_async_remote_copy(
                        src_ref=cin[a] if a < ng else cin[a].at[pidx], dst_ref=cout[a].at[me],
                        send_sem=send_sems.at[a, k - 1], recv_sem=recv_sems.at[a, k - 1],
                        device_id=peer, device_id_type=pl.DeviceIdType.MESH))
            return out

        @pl.when(first)
        def _():
            for cp in copies():
                cp.start()

        body(*ins, *outs, *scr)

        @pl.when(last)
        def _():
            for cp in copies():
                cp.wait()

    hbm = pl.BlockSpec(memory_space=pl.ANY)
    comm_shapes = [jax.ShapeDtypeStruct((N_DEV,) + a.shape, a.dtype) for a in gather]
    comm_shapes += [jax.ShapeDtypeStruct(a.shape, a.dtype) for a in exchange]
    return pl.pallas_call(
        wrapped, name=name, grid=grid, in_specs=in_specs + [hbm] * nc, out_specs=out_specs + [hbm] * nc,
        out_shape=out_shape + comm_shapes,
        scratch_shapes=scratch + [pltpu.SemaphoreType.DMA((nc, N_DEV - 1)), pltpu.SemaphoreType.DMA((nc, N_DEV - 1)),
                                  pltpu.SemaphoreType.DMA((nc,))],
        compiler_params=_params(*(("arbitrary",) * len(grid))),
    )(*args, *gather, *exchange)


def _gather_two_level(name, arrays):
    na = len(arrays)

    def body(*refs):
        srcs, outs = refs[:na], refs[na:2 * na]
        send_sems, recv_sems, loc_sems = refs[2 * na:]
        x, y, c, me = _mesh_pos()
        here, sib = (x, y, c), (x, y, 1 - c)
        chips = [(1 - x, y), (x, 1 - y), (1 - x, 1 - y)]
        pos = lambda px, py, pc: 4 * px + 2 * py + pc

        def copy(a, k, block, to, src=None):
            return pltpu.make_async_remote_copy(
                src_ref=outs[a].at[block] if src is None else src, dst_ref=outs[a].at[block],
                send_sem=send_sems.at[a, k], recv_sem=recv_sems.at[a, k], device_id=to,
                device_id_type=pl.DeviceIdType.MESH)

        locs = [pltpu.make_async_copy(srcs[a], outs[a].at[me], loc_sems.at[a]) for a in range(na)]
        for cp in locs:
            cp.start()
        first = []
        for a in range(na):
            first.append(copy(a, 0, me, sib, src=srcs[a]))
            first += [copy(a, 1 + j, me, (*chip, c), src=srcs[a]) for j, chip in enumerate(chips)]
        for cp in first:
            cp.start()
        passed = []
        for a in range(na):
            for j, chip in enumerate(chips):
                copy(a, 1 + j, pos(*chip, c), here).wait_recv()
                passed.append(copy(a, 4 + j, pos(*chip, c), sib))
                passed[-1].start()
        for a in range(na):
            copy(a, 0, pos(x, y, 1 - c), here).wait_recv()
            for j, chip in enumerate(chips):
                copy(a, 4 + j, pos(*chip, 1 - c), here).wait_recv()
        for cp in first + passed:
            cp.wait_send()
        for cp in locs:
            cp.wait()

    hbm = pl.BlockSpec(memory_space=pl.ANY)
    return pl.pallas_call(
        body, name=name, in_specs=[hbm] * na, out_specs=[hbm] * na,
        out_shape=[jax.ShapeDtypeStruct((N_DEV,) + a.shape, a.dtype) for a in arrays],
        scratch_shapes=[pltpu.SemaphoreType.DMA((na, N_DEV - 1)), pltpu.SemaphoreType.DMA((na, N_DEV - 1)),
                        pltpu.SemaphoreType.DMA((na,))],
    )(*arrays)


def _exchange_only(name, gather=(), exchange=()):
    return _call(lambda: None, name=name, grid=(1,), in_specs=[], out_specs=[], out_shape=[], args=(),
                 gather=gather, exchange=exchange)


def _matmul(a, b, mode, name, *, out_dtype=F32, tm=512, tn=512, tk=512, add=None, add_scale=1.0,
            blocked_out=False, gather=(), exchange=()):
    blocked_b = b.ndim == 3
    if blocked_b:
        (m, k), (nb, _, tn) = a.shape, b.shape
        n = nb * tn
    elif mode == "nn":
        (m, k), (_, n) = a.shape, b.shape
    elif mode == "nt":
        (m, k), (n, _) = a.shape, b.shape
    else:
        (k, m), (_, n) = a.shape, b.shape
    tm, tn, tk = min(tm, m), min(tn, n), min(tk, k)
    nk = k // tk
    dims = {"nn": NN, "nt": NT, "tn": TN}[mode]
    if mode == "tn":
        a_spec = pl.BlockSpec((tk, tm), lambda i, j, kk: (kk, i))
    else:
        a_spec = pl.BlockSpec((tm, tk), lambda i, j, kk: (i, kk))
    if blocked_b:
        b_spec = pl.BlockSpec((None, tk, tn), lambda i, j, kk: (j, kk, 0))
    elif mode == "nt":
        b_spec = pl.BlockSpec((tn, tk), lambda i, j, kk: (j, kk))
    else:
        b_spec = pl.BlockSpec((tk, tn), lambda i, j, kk: (kk, j))
    if blocked_out:
        o_spec = pl.BlockSpec((None, tm, tn), lambda i, j, kk: (j, i, 0))
        o_shape = jax.ShapeDtypeStruct((n // tn, m, tn), out_dtype)
    else:
        o_spec = pl.BlockSpec((tm, tn), lambda i, j, kk: (i, j))
        o_shape = jax.ShapeDtypeStruct((m, n), out_dtype)
    has_add = add is not None
    cache_a = nk == 1 and mode != "tn" and n // tn > 1 and a.dtype != BF16

    def body(*refs):
        if has_add:
            a_ref, b_ref, add_ref, o_ref, s_ref = refs
        else:
            a_ref, b_ref, o_ref, s_ref = refs
        kk = pl.program_id(2)
        if cache_a:
            @pl.when(pl.program_id(1) == 0)
            def _():
                s_ref[...] = a_ref[...].astype(BF16)

            lhs = s_ref[...]
        else:
            lhs = a_ref[...].astype(BF16)
        part = _dot(lhs, b_ref[...].astype(BF16), dims)

        def finish(r):
            if has_add:
                r = r + add_scale * add_ref[...]
            o_ref[...] = r.astype(out_dtype)

        if nk == 1:
            finish(part)
            return

        @pl.when(kk == 0)
        def _():
            s_ref[...] = part

        @pl.when(kk > 0)
        def _():
            s_ref[...] += part

        @pl.when(kk == nk - 1)
        def _():
            finish(s_ref[...])

    if nk > 1:
        scratch = [pltpu.VMEM((tm, tn), F32)]
    else:
        scratch = [pltpu.VMEM((tm, tk), BF16) if cache_a else pltpu.VMEM((8, LANES), F32)]
    return _call(
        body, name=name, grid=(m // tm, n // tn, nk),
        in_specs=[a_spec, b_spec] + ([pl.BlockSpec((tm, tn), lambda i, j, kk: (i, j))] if has_add else []),
        out_specs=[o_spec], out_shape=[o_shape], args=(a, b) + ((add,) if has_add else ()),
        scratch_shapes=scratch, sem=("parallel", "arbitrary", "arbitrary"),
        gather=gather, exchange=exchange)


def _ln_fwd_math(u, g, b):
    mu = jnp.mean(u, axis=-1, keepdims=True)
    uc = u - mu
    var = jnp.mean(uc * uc, axis=-1, keepdims=True)
    return uc * lax.rsqrt(var + LN_EPS) * g + b


def _ln_bwd_math(dy, u, g):
    mu = jnp.mean(u, axis=-1, keepdims=True)
    uc = u - mu
    var = jnp.mean(uc * uc, axis=-1, keepdims=True)
    rstd = lax.rsqrt(var + LN_EPS)
    xhat = uc * rstd
    dxh = dy * g
    m1 = jnp.mean(dxh, axis=-1, keepdims=True)
    m2 = jnp.mean(dxh * xhat, axis=-1, keepdims=True)
    return rstd * (dxh - m1 - xhat * m2), xhat


def _matmul_resid_ln(a, w, x, g, b, name, tm=512):
    s, k = a.shape
    d = w.shape[1]

    def body(a_ref, w_ref, x_ref, g_ref, b_ref, u_ref, y_ref):
        u = ALPHA * x_ref[...] + _dot(a_ref[...].astype(BF16), w_ref[...], NN)
        u_ref[...] = u
        y_ref[...] = _ln_fwd_math(u, g_ref[...], b_ref[...])

    row = pl.BlockSpec((tm, d), lambda i: (i, 0))
    vec = pl.BlockSpec((1, d), lambda i: (0, 0))
    return pl.pallas_call(
        body, name=name, grid=(s // tm,),
        in_specs=[pl.BlockSpec((tm, k), lambda i: (i, 0)), pl.BlockSpec((k, d), lambda i: (0, 0)), row, vec, vec],
        out_specs=[row, row], out_shape=[jax.ShapeDtypeStruct((s, d), F32)] * 2,
        compiler_params=_params("parallel"),
    )(a, w, x, g, b)


def _ln_bwd(dy, u, g, name, tm=512):
    s, d = dy.shape
    nt = s // tm

    def body(dy_ref, u_ref, g_ref, du_ref, dg_ref, db_ref, g8, b8):
        i = pl.program_id(0)
        dy_ = dy_ref[...]
        du, xhat = _ln_bwd_math(dy_, u_ref[...], g_ref[...])
        du_ref[...] = du

        @pl.when(i == 0)
        def _():
            g8[...] = jnp.zeros_like(g8)
            b8[...] = jnp.zeros_like(b8)

        g8[...] += _rowsum8(dy_ * xhat)
        b8[...] += _rowsum8(dy_)

        @pl.when(i == nt - 1)
        def _():
            dg_ref[...] = jnp.sum(g8[...], axis=0, keepdims=True)
            db_ref[...] = jnp.sum(b8[...], axis=0, keepdims=True)

    row = pl.BlockSpec((tm, d), lambda i: (i, 0))
    vec = pl.BlockSpec((1, d), lambda i: (0, 0))
    return pl.pallas_call(
        body, name=name, grid=(nt,),
        in_specs=[row, row, vec], out_specs=[row, vec, vec],
        out_shape=[jax.ShapeDtypeStruct((s, d), F32), jax.ShapeDtypeStruct((1, d), F32),
                   jax.ShapeDtypeStruct((1, d), F32)],
        scratch_shapes=[pltpu.VMEM((8, d), F32), pltpu.VMEM((8, d), F32)],
        compiler_params=_params("arbitrary"),
    )(dy, u, g)


FF_PAIR = 2 * FF_PAD
N_PAIR = N_DEV // 2


def _ffn_fwd(x, wgt, wut, wd, g, b, name, tm=1024, gather=()):
    s, d = x.shape

    def body(x_ref, wg_ref, wu_ref, wd_ref, g_ref, b_ref, u_ref, y_ref, xb, acc):
        k = pl.program_id(1)

        @pl.when(k == 0)
        def _():
            xb[...] = x_ref[...].astype(BF16)

        a = _dot(xb[...], wg_ref[...], NT)
        bb = _dot(xb[...], wu_ref[...], NT)
        h = (a * _sigmoid(a) * bb).astype(BF16)
        part = _dot(h, wd_ref[...], NN)

        @pl.when(k == 0)
        def _():
            acc[...] = part

        @pl.when(k > 0)
        def _():
            acc[...] += part

        @pl.when(k == N_PAIR - 1)
        def _():
            u = ALPHA * x_ref[...] + 0.5 * acc[...]
            u_ref[...] = u
            y_ref[...] = _ln_fwd_math(u, g_ref[...], b_ref[...])

    row = pl.BlockSpec((tm, d), lambda i, k: (i, 0))
    vec = pl.BlockSpec((1, d), lambda i, k: (0, 0))
    w_in = pl.BlockSpec((None, FF_PAIR, d), lambda i, k: (k, 0, 0))
    w_dn = w_in
    return _call(
        body, name=name, grid=(s // tm, N_PAIR),
        in_specs=[row, w_in, w_in, w_dn, vec, vec], out_specs=[row, row],
        out_shape=[jax.ShapeDtypeStruct((s, d), F32)] * 2, args=(x, wgt, wut, wd, g, b),
        scratch_shapes=[pltpu.VMEM((tm, d), BF16), pltpu.VMEM((tm, d), F32)],
        sem=("parallel", "arbitrary"), gather=gather)


def _ffn_bwd_x(dy, u, x, wgt, wut, wd, g, name, tm=512, exchange=()):
    s, d = x.shape
    nt = s // tm
    ffp = N_DEV * FF_PAD

    def body(dy_ref, u_ref, x_ref, wg_ref, wu_ref, wd_ref, g_ref,
             dx_ref, xb, df_ref, da_ref, db_ref, h_ref, dg_ref, dbl_ref,
             dfb, du_s, acc, g8, b8):
        i = pl.program_id(0)
        k = pl.program_id(1)

        @pl.when(k == 0)
        def _():
            dy_ = dy_ref[...]
            du, xhat = _ln_bwd_math(dy_, u_ref[...], g_ref[...])
            du_s[...] = du
            dfb[...] = (0.5 * du).astype(BF16)
            df_ref[...] = dfb[...]
            xb[...] = x_ref[...].astype(BF16)

            @pl.when(i == 0)
            def _():
                g8[...] = jnp.zeros_like(g8)
                b8[...] = jnp.zeros_like(b8)

            g8[...] += _rowsum8(dy_ * xhat)
            b8[...] += _rowsum8(dy_)

        a = _dot(xb[...], wg_ref[...], NT)
        bb = _dot(xb[...], wu_ref[...], NT)
        sig = _sigmoid(a)
        sa = a * sig
        h_ref[...] = (sa * bb).astype(BF16)
        dh = _dot(dfb[...], wd_ref[...], NT)
        da = (dh * bb * (sig * (1.0 + a * (1.0 - sig)))).astype(BF16)
        db = (dh * sa).astype(BF16)
        da_ref[...] = da
        db_ref[...] = db
        part = _dot(da, wg_ref[...], NN) + _dot(db, wu_ref[...], NN)

        @pl.when(k == 0)
        def _():
            acc[...] = part

        @pl.when(k > 0)
        def _():
            acc[...] += part

        @pl.when(k == N_PAIR - 1)
        def _():
            dx_ref[...] = ALPHA * du_s[...] + acc[...]

        @pl.when((k == N_PAIR - 1) & (i == nt - 1))
        def _():
            dg_ref[...] = jnp.sum(g8[...], axis=0, keepdims=True)
            dbl_ref[...] = jnp.sum(b8[...], axis=0, keepdims=True)

    row = pl.BlockSpec((tm, d), lambda i, k: (i, 0))
    vec = pl.BlockSpec((1, d), lambda i, k: (0, 0))
    w_in = pl.BlockSpec((None, FF_PAIR, d), lambda i, k: (k, 0, 0))
    hid = pl.BlockSpec((tm, FF_PAIR), lambda i, k: (i, k))
    return _call(
        body, name=name, grid=(nt, N_PAIR),
        in_specs=[row, row, row, w_in, w_in, w_in, vec],
        out_specs=[row, row, row, hid, hid, hid, vec, vec],
        out_shape=[jax.ShapeDtypeStruct((s, d), F32), jax.ShapeDtypeStruct((s, d), BF16),
                   jax.ShapeDtypeStruct((s, d), BF16),
                   jax.ShapeDtypeStruct((s, ffp), BF16), jax.ShapeDtypeStruct((s, ffp), BF16),
                   jax.ShapeDtypeStruct((s, ffp), BF16),
                   jax.ShapeDtypeStruct((1, d), F32), jax.ShapeDtypeStruct((1, d), F32)],
        args=(dy, u, x, wgt, wut, wd, g),
        scratch_shapes=[pltpu.VMEM((tm, d), BF16), pltpu.VMEM((tm, d), F32),
                        pltpu.VMEM((tm, d), F32), pltpu.VMEM((8, d), F32), pltpu.VMEM((8, d), F32)],
        sem=("arbitrary", "arbitrary"), exchange=exchange)


def _ffn_bwd_w(tok, hid, name, *, down, tm=1024, exchange=()):
    s, d = tok.shape
    nt = s // tm

    def body(t_ref, h_ref, dw_ref, acc):
        i = pl.program_id(1)
        part = _dot(h_ref[...], t_ref[...], TN) if down else _dot(t_ref[...], h_ref[...], TN)

        @pl.when(i == 0)
        def _():
            acc[...] = part

        @pl.when(i > 0)
        def _():
            acc[...] += part

        @pl.when(i == nt - 1)
        def _():
            for j in range(2):
                lo = j * FF_PAD
                dw_ref[j] = (acc[lo:lo + FF_SHARD, :] if down else acc[:, lo:lo + FF_SHARD]).astype(BF16)

    blk = (FF_SHARD, d) if down else (d, FF_SHARD)
    return _call(
        body, name=name, grid=(N_PAIR, nt),
        in_specs=[pl.BlockSpec((tm, d), lambda k, i: (i, 0)), pl.BlockSpec((tm, FF_PAIR), lambda k, i: (i, k))],
        out_specs=[pl.BlockSpec((2,) + blk, lambda k, i: (k, 0, 0))],
        out_shape=[jax.ShapeDtypeStruct((N_DEV,) + blk, BF16)], args=(tok, hid),
        scratch_shapes=[pltpu.VMEM((FF_PAIR, d) if down else (d, FF_PAIR), F32)],
        sem=("parallel", "arbitrary"), exchange=exchange)


def _bucket_tables():
    qi = np.arange(BLK)[:, None]
    ki = np.arange(2 * BLK)[None, :]
    off = qi + BLK - ki
    out = []
    for window, dil in DILATED:
        n_keys = window // dil
        dist = dil * np.clip(off, 0, n_keys)
        exact = REL_BUCKETS // 2
        df = np.maximum(dist, 1).astype(np.float32)
        large = exact + (np.log(df / np.float32(exact)) / np.float32(math.log(REL_MAX_DIST / exact))
                         * np.float32(REL_BUCKETS - exact)).astype(np.int32)
        large = np.minimum(large, REL_BUCKETS - 1)
        bucket = np.where(dist < exact, dist, large).astype(np.int32)
        band = (off >= 0) & (off <= n_keys)
        out.append(np.where(band, bucket, -1))
    return np.stack(out).astype(np.int32)


def _bias_fwd(rel_bias, buckets, name="bias_fwd"):
    def body(tbl_ref, bkt_ref, out_ref):
        bkt = bkt_ref[...]
        for h in range(ATT_HEADS):
            acc = jnp.full((BLK, 2 * BLK), NEG, F32)
            for bb in range(REL_BUCKETS):
                acc = jnp.where(bkt == bb, tbl_ref[bb, h], acc)
            out_ref[h] = acc

    nbr = len(DILATED)
    return pl.pallas_call(
        body, name=name, grid=(nbr,),
        in_specs=[pl.BlockSpec(memory_space=pltpu.SMEM),
                  pl.BlockSpec((None, BLK, 2 * BLK), lambda r: (r, 0, 0))],
        out_specs=pl.BlockSpec((None, ATT_HEADS, BLK, 2 * BLK), lambda r: (r, 0, 0, 0)),
        out_shape=jax.ShapeDtypeStruct((nbr, ATT_HEADS, BLK, 2 * BLK), F32),
        compiler_params=_params("parallel"),
    )(rel_bias, buckets)


def _bias_bwd(dbias, buckets, name="bias_bwd"):
    nbr = len(DILATED)

    def body(db_ref, bkt_ref, out_ref):
        r = pl.program_id(0)

        @pl.when(r == 0)
        def _():
            out_ref[...] = jnp.zeros_like(out_ref)

        bkt = bkt_ref[...]
        rowi = lax.broadcasted_iota(jnp.int32, (REL_BUCKETS, LANES), 0)
        coli = lax.broadcasted_iota(jnp.int32, (REL_BUCKETS, LANES), 1)
        acc = jnp.zeros((REL_BUCKETS, LANES), F32)
        for h in range(ATT_HEADS):
            x = db_ref[h]
            for bb in range(REL_BUCKETS):
                part = jnp.sum(jnp.where(bkt == bb, x, 0.0), axis=0, keepdims=True)
                tot = jnp.sum(part, axis=1, keepdims=True)
                acc = acc + jnp.where((rowi == bb) & (coli == h), tot, 0.0)
        out_ref[...] += acc

    return pl.pallas_call(
        body, name=name, grid=(nbr,),
        in_specs=[pl.BlockSpec((None, ATT_HEADS, BLK, 2 * BLK), lambda r: (r, 0, 0, 0)),
                  pl.BlockSpec((None, BLK, 2 * BLK), lambda r: (r, 0, 0))],
        out_specs=pl.BlockSpec((REL_BUCKETS, LANES), lambda r: (0, 0)),
        out_shape=jax.ShapeDtypeStruct((REL_BUCKETS, LANES), F32),
        compiler_params=_params("arbitrary"),
    )(dbias, buckets)


def _stack_heads(pair, lo):
    return jnp.concatenate([jnp.where(lo, pair, 0.0), jnp.where(lo, 0.0, pair)], axis=0)


def _head_cols(pair, lo, reduce):
    fill = -jnp.inf if reduce is jnp.max else 0.0
    return jnp.concatenate([reduce(jnp.where(lo, pair, fill), axis=1, keepdims=True),
                            reduce(jnp.where(lo, fill, pair), axis=1, keepdims=True)], axis=0)


def _unstack_heads(x2, lo):
    return jnp.where(lo, x2[:BLK], x2[BLK:])


def _att_scores(q2, kk, bias2, first_ok):
    sc = _dot(q2, kk, NT) * (64 ** -0.5) + bias2
    return jnp.where(first_ok, sc, NEG)


DIL_TILE = 2048
DIL_COLS = ATT_W // LANES
DIL_UNROLL_FWD = 16
DIL_UNROLL_BWD = 8


def _dil_rows(dil, n, r, base=0):
    start = base + n * (BLK * dil) + r
    return pl.ds(start, BLK, stride=dil) if dil > 1 else pl.ds(start, BLK)


def _dil_in_specs(tile_of):
    cur = lambda col: pl.BlockSpec((DIL_TILE, LANES), lambda p, i: (tile_of(i), col * DIL_COLS + p))
    prev = lambda col: pl.BlockSpec((DIL_TILE, LANES), lambda p, i: (jnp.maximum(tile_of(i) - 1, 0), col * DIL_COLS + p))
    bias = pl.BlockSpec((len(DILATED), None, 2 * BLK, 2 * BLK), lambda p, i: (0, p, 0, 0))
    return [cur(0), prev(1), cur(1), prev(2), cur(2), bias]


def _pair_bias(biasm):
    return biasm.reshape(len(DILATED), DIL_COLS, 2 * BLK, 2 * BLK)


def _dil_fwd(proj, biasm, name="dil_fwd", gather=()):
    s = proj.shape[0]
    nt = s // DIL_TILE
    tt = DIL_TILE

    def body(q_ref, kp_ref, kc_ref, vp_ref, vc_ref, bias_ref, att_ref, lse_ref, k2, v2, ob, lb):
        t = pl.program_id(1)
        k2[0:tt, :] = kp_ref[...]
        k2[tt:2 * tt, :] = kc_ref[...]
        v2[0:tt, :] = vp_ref[...]
        v2[tt:2 * tt, :] = vc_ref[...]
        lo = lax.broadcasted_iota(jnp.int32, (BLK, LANES), 1) < 64
        kidx = lax.broadcasted_iota(jnp.int32, (2 * BLK, 2 * BLK), 1)
        for b, (_, dil) in enumerate(DILATED):
            nblk = tt // (BLK * dil)

            def step(j, carry, b=b, dil=dil, nblk=nblk):
                r, n = j % dil, j // dil
                cur, prev = _dil_rows(dil, n, r, tt), _dil_rows(dil, n - 1, r, tt)
                here = _dil_rows(dil, n, r)
                q_pair = q_ref[here, :]
                kk = jnp.concatenate([k2[prev, :], k2[cur, :]], axis=0).astype(BF16)
                vv = jnp.concatenate([v2[prev, :], v2[cur, :]], axis=0).astype(BF16)
                first_ok = (t > 0) | (n > 0) | (kidx >= BLK)
                sc = _att_scores(_stack_heads(q_pair, lo).astype(BF16), kk, bias_ref[b], first_ok)
                mx = jnp.max(sc, axis=1, keepdims=True)
                pe = jnp.exp(sc - mx)
                l = jnp.sum(pe, axis=1, keepdims=True)
                ob.at[b][here, :] = _unstack_heads(_dot(pe.astype(BF16), vv, NN) / l, lo)
                lb.at[b][here, :] = _unstack_heads(jnp.broadcast_to(mx + jnp.log(l), (2 * BLK, LANES)), lo)
                return carry

            lax.fori_loop(0, tt // BLK, step, 0, unroll=DIL_UNROLL_FWD)
        l0, l1, l2 = lb[0], lb[1], lb[2]
        mx = jnp.maximum(jnp.maximum(l0, l1), l2)
        e0, e1, e2 = jnp.exp(l0 - mx), jnp.exp(l1 - mx), jnp.exp(l2 - mx)
        tot = e0 + e1 + e2
        att_ref[...] = (e0 * ob[0] + e1 * ob[1] + e2 * ob[2]) / tot
        lse_ref[...] = mx + jnp.log(tot)

    out = pl.BlockSpec((tt, LANES), lambda p, i: (i, p))
    return _call(
        body, name=name, grid=(DIL_COLS, nt), in_specs=_dil_in_specs(lambda i: i), out_specs=[out, out],
        out_shape=[jax.ShapeDtypeStruct((s, ATT_W), F32)] * 2, args=(proj, proj, proj, proj, proj, _pair_bias(biasm)),
        scratch_shapes=[pltpu.VMEM((2 * tt, LANES), F32), pltpu.VMEM((2 * tt, LANES), F32),
                        pltpu.VMEM((len(DILATED), tt, LANES), F32), pltpu.VMEM((len(DILATED), tt, LANES), F32)],
        sem=("parallel", "parallel"), gather=gather)


def _dil_bwd(proj, biasm, lse, att, dcat, name="dil_bwd"):
    s = proj.shape[0]
    nt = s // DIL_TILE
    tt = DIL_TILE
    nbr = len(DILATED)

    def body(q_ref, kp_ref, kc_ref, vp_ref, vc_ref, bias_ref, lse_ref, att_ref, datt_ref,
             dq_ref, dk_ref, dv_ref, dbias_ref, k2, v2, dqa, dka, dva, kcar, vcar):
        i = pl.program_id(1)
        t = nt - 1 - i
        k2[0:tt, :] = kp_ref[...]
        k2[tt:2 * tt, :] = kc_ref[...]
        v2[0:tt, :] = vp_ref[...]
        v2[tt:2 * tt, :] = vc_ref[...]

        @pl.when(i == 0)
        def _():
            kcar[...] = jnp.zeros_like(kcar)
            vcar[...] = jnp.zeros_like(vcar)
            dbias_ref[...] = jnp.zeros_like(dbias_ref)

        dqa[...] = jnp.zeros_like(dqa)
        dka[0:tt, :] = jnp.zeros((tt, LANES), F32)
        dva[0:tt, :] = jnp.zeros((tt, LANES), F32)
        dka[tt:2 * tt, :] = kcar[...]
        dva[tt:2 * tt, :] = vcar[...]
        lo = lax.broadcasted_iota(jnp.int32, (BLK, LANES), 1) < 64
        kidx = lax.broadcasted_iota(jnp.int32, (2 * BLK, 2 * BLK), 1)
        for b, (_, dil) in enumerate(DILATED):
            nblk = tt // (BLK * dil)

            def step(j, carry, b=b, dil=dil, nblk=nblk):
                r, n = j % dil, j // dil
                cur, prev = _dil_rows(dil, n, r, tt), _dil_rows(dil, n - 1, r, tt)
                here = _dil_rows(dil, n, r)
                q_pair = q_ref[here, :]
                kk = jnp.concatenate([k2[prev, :], k2[cur, :]], axis=0).astype(BF16)
                vv = jnp.concatenate([v2[prev, :], v2[cur, :]], axis=0).astype(BF16)
                first_ok = (t > 0) | (n > 0) | (kidx >= BLK)
                dat_pair = datt_ref[here, :]
                q2 = _stack_heads(q_pair, lo).astype(BF16)
                dom = _stack_heads(dat_pair, lo).astype(BF16)
                sc = _att_scores(q2, kk, bias_ref[b], first_ok)
                pr = jnp.exp(sc - _head_cols(lse_ref[here, :], lo, jnp.max))
                ds = pr * (_dot(dom, vv, NT) - _head_cols(dat_pair * att_ref[here, :], lo, jnp.sum))
                dbias_ref[b] += ds
                dsb = (ds * (64 ** -0.5)).astype(BF16)
                dk2 = _dot(dsb, q2, TN)
                dv2 = _dot(pr.astype(BF16), dom, TN)
                dqa[here, :] += _unstack_heads(_dot(dsb, kk, NN), lo)
                dka[prev, :] += dk2[:BLK]
                dka[cur, :] += dk2[BLK:]
                dva[prev, :] += dv2[:BLK]
                dva[cur, :] += dv2[BLK:]
                return carry

            lax.fori_loop(0, tt // BLK, step, 0, unroll=DIL_UNROLL_BWD)
        dq_ref[...] = dqa[...].astype(BF16)
        dk_ref[...] = dka[tt:2 * tt, :].astype(BF16)
        dv_ref[...] = dva[tt:2 * tt, :].astype(BF16)
        kcar[...] = dka[0:tt, :]
        vcar[...] = dva[0:tt, :]

    rev = lambda i: nt - 1 - i
    out = pl.BlockSpec((tt, LANES), lambda p, i: (rev(i), p))
    two = lambda: pltpu.VMEM((2 * tt, LANES), F32)
    one = lambda: pltpu.VMEM((tt, LANES), F32)
    return pl.pallas_call(
        body, name=name, grid=(DIL_COLS, nt),
        in_specs=_dil_in_specs(rev) + [out, out, out],
        out_specs=[out, out, out, pl.BlockSpec((nbr, None, 2 * BLK, 2 * BLK), lambda p, i: (0, p, 0, 0))],
        out_shape=[jax.ShapeDtypeStruct((s, ATT_W), BF16)] * 3
        + [jax.ShapeDtypeStruct((nbr, DIL_COLS, 2 * BLK, 2 * BLK), F32)],
        scratch_shapes=[two(), two(), one(), two(), two(), one(), one()],
        compiler_params=_params("arbitrary", "arbitrary"),
    )(proj, proj, proj, proj, proj, _pair_bias(biasm), lse, att, dcat)


QK_COL0 = (3 * ATT_W) // ATT_W


def _conv_shifted(prev, cur, j, row):
    sh = CONV_K - 1 - j
    if sh == 0:
        return cur
    return jnp.where(row < sh, pltpu.roll(prev, sh, 0), pltpu.roll(cur, sh, 0))


def _conv_z(prev, cur, w_ref, b_ref, row):
    z = b_ref[...] + cur * w_ref[CONV_K - 1:CONV_K, :]
    for j in range(CONV_K - 1):
        z = z + _conv_shifted(prev, cur, j, row) * w_ref[j:j + 1, :]
    return z


def _conv_fwd(proj, conv_w, conv_b, name="conv_fwd", tm=512):
    s = proj.shape[0]
    w = ATT_W

    def body(prev_ref, cur_ref, w_ref, b_ref, o_ref):
        i = pl.program_id(1)
        row = lax.broadcasted_iota(jnp.int32, (tm, w), 0)
        prev = jnp.where(i > 0, prev_ref[...], 0.0)
        z = _conv_z(prev, cur_ref[...], w_ref, b_ref, row)
        o_ref[...] = z * _sigmoid(z)

    return pl.pallas_call(
        body, name=name, grid=(2, s // tm),
        in_specs=[pl.BlockSpec((tm, w), lambda j, i: (jnp.maximum(i - 1, 0), QK_COL0 + j)),
                  pl.BlockSpec((tm, w), lambda j, i: (i, QK_COL0 + j)),
                  pl.BlockSpec((CONV_K, w), lambda j, i: (0, j)),
                  pl.BlockSpec((1, w), lambda j, i: (0, j))],
        out_specs=pl.BlockSpec((tm, w), lambda j, i: (i, j)),
        out_shape=jax.ShapeDtypeStruct((s, 2 * ML_W), F32),
        compiler_params=_params("parallel", "parallel"),
    )(proj, proj, conv_w, conv_b)


def _conv_bwd(proj, dqk, conv_w, conv_b, name="conv_bwd", tm=512):
    s = proj.shape[0]
    w = ATT_W
    nt = s // tm

    def body(xp_ref, xc_ref, xn_ref, dc_ref, dn_ref, w_ref, b_ref, dx_ref, dw_ref, db_ref):
        i = pl.program_id(1)
        row = lax.broadcasted_iota(jnp.int32, (tm, w), 0)
        prev = jnp.where(i > 0, xp_ref[...], 0.0)
        cur = xc_ref[...]

        def dz_of(pv, cv, dy):
            z = _conv_z(pv, cv, w_ref, b_ref, row)
            sig = _sigmoid(z)
            return dy * (sig * (1.0 + z * (1.0 - sig)))

        dzc = dz_of(prev, cur, dc_ref[...])
        dzn = jnp.where(i < nt - 1, dz_of(cur, xn_ref[...], dn_ref[...]), 0.0)
        dx = dzc * w_ref[CONV_K - 1:CONV_K, :]
        for j in range(CONV_K - 1):
            sh = CONV_K - 1 - j
            up = jnp.where(row >= tm - sh, pltpu.roll(dzn, tm - sh, 0), pltpu.roll(dzc, tm - sh, 0))
            dx = dx + up * w_ref[j:j + 1, :]
        dx_ref[...] = dx

        @pl.when(i == 0)
        def _():
            dw_ref[...] = jnp.zeros_like(dw_ref)
            db_ref[...] = jnp.zeros_like(db_ref)

        for j in range(CONV_K):
            dw_ref[j:j + 1, :] += jnp.sum(dzc * _conv_shifted(prev, cur, j, row), axis=0, keepdims=True)
        db_ref[...] += jnp.sum(dzc, axis=0, keepdims=True)

    xs = lambda f: pl.BlockSpec((tm, w), lambda j, i: (f(i), QK_COL0 + j))
    ds = lambda f: pl.BlockSpec((tm, w), lambda j, i: (f(i), j))
    return pl.pallas_call(
        body, name=name, grid=(2, nt),
        in_specs=[xs(lambda i: jnp.maximum(i - 1, 0)), xs(lambda i: i), xs(lambda i: jnp.minimum(i + 1, nt - 1)),
                  ds(lambda i: i), ds(lambda i: jnp.minimum(i + 1, nt - 1)),
                  pl.BlockSpec((CONV_K, w), lambda j, i: (0, j)), pl.BlockSpec((1, w), lambda j, i: (0, j))],
        out_specs=[ds(lambda i: i), pl.BlockSpec((CONV_K, w), lambda j, i: (0, j)),
                   pl.BlockSpec((1, w), lambda j, i: (0, j))],
        out_shape=[jax.ShapeDtypeStruct((s, 2 * ML_W), F32), jax.ShapeDtypeStruct((CONV_K, 2 * ML_W), F32),
                   jax.ShapeDtypeStruct((1, 2 * ML_W), F32)],
        compiler_params=_params("parallel", "arbitrary"),
    )(proj, proj, proj, dqk, dqk, conv_w, conv_b)


def _bf16_mm(dims_fwd):
    @jax.custom_vjp
    def mm(a, b):
        return _dot(a.astype(BF16), b.astype(BF16), dims_fwd)

    def fwd(a, b):
        return mm(a, b), (a, b)

    def bwd(res, g):
        a, b = res
        if dims_fwd is NN:
            return _mm_nt(g, b), _mm_tn(a, g)
        if dims_fwd is NT:
            return _mm_nn(g, b), _mm_tn(g, a)
        return _mm_nt(b, g), _mm_nn(a, g)

    mm.defvjp(fwd, bwd)
    return mm


_mm_nn = _bf16_mm(NN)
_mm_nt = _bf16_mm(NT)
_mm_tn = _bf16_mm(TN)


def _tri(lower):
    r = lax.broadcasted_iota(jnp.int32, (CHUNK, CHUNK), 0)
    c = lax.broadcasted_iota(jnp.int32, (CHUNK, CHUNK), 1)
    return ((r >= c) if lower else (r <= c)).astype(F32)


@jax.custom_vjp
def _cumsum_rows(x):
    return lax.dot_general(_tri(True), x, NN, precision=lax.Precision.HIGHEST, preferred_element_type=F32)


def _cumsum_fwd(x):
    return _cumsum_rows(x), None


def _cumsum_bwd(_, g):
    return (lax.dot_general(_tri(False), g, NN, precision=lax.Precision.HIGHEST, preferred_element_type=F32),)


_cumsum_rows.defvjp(_cumsum_fwd, _cumsum_bwd)


def _abs(x):
    return jnp.where(x >= 0, x, -x)


def _log_sigmoid(x):
    return jnp.minimum(x, 0.0) - jnp.log(1.0 + jnp.exp(-_abs(x)))


def _pick_col(x, lane):
    sel = lax.broadcasted_iota(jnp.int32, x.shape, 1) == lane
    return jnp.sum(jnp.where(sel, x, 0.0), axis=1, keepdims=True)


def _pick_row(x, r):
    sel = lax.broadcasted_iota(jnp.int32, x.shape, 0) == r
    return jnp.sum(jnp.where(sel, x, 0.0), axis=0, keepdims=True)


def _mlstm_chunk(qs, ks, vs, oms, gates, gate_bias, mlg, cs, ns, ms):
    gb = gates + gate_bias
    cum = _cumsum_rows(_log_sigmoid(gb))
    gbt = gb.T
    cumt = cum.T
    causal = lax.broadcasted_iota(jnp.int32, (CHUNK, CHUNK), 0) >= lax.broadcasted_iota(jnp.int32, (CHUNK, CHUNK), 1)
    ys, c_out, n_out, m_out = [], [], [], []
    for h in range(ML_HEADS):
        q, v, om, c, n, m = qs[h], vs[h], oms[h], cs[h], ns[h], ms[h]
        k = ks[h] * (ML_HD ** -0.5)
        ig_col = _pick_col(gb, h)
        ig_row = _pick_row(gbt, h)
        b_col = _pick_col(cum, ML_HEADS + h)
        b_row = _pick_row(cumt, ML_HEADS + h)
        g = _pick_row(b_col, CHUNK - 1)
        a = g - b_col + ig_col
        m_loc = jnp.max(a, axis=0, keepdims=True)
        wa = jnp.exp(a - m_loc)
        c_loc = _mm_tn(wa * v, k)
        n_loc = jnp.sum(wa * k, axis=0, keepdims=True)
        m_new = jnp.maximum(g + m, m_loc)
        sp = jnp.exp(g + m - m_new)
        sl = jnp.exp(m_loc - m_new)
        c_out.append(sp * c + sl * c_loc)
        n_out.append(sp * n + sl * n_loc)
        m_out.append(m_new)
        d_log = jnp.where(causal, b_col - b_row + ig_row, -jnp.inf)
        e_log = b_col + m
        m_t = jnp.maximum(e_log, jnp.max(d_log, axis=1, keepdims=True))
        d_w = jnp.exp(d_log - m_t)
        e_w = jnp.exp(e_log - m_t)
        s_qk = _mm_nt(q, k) * d_w
        num = e_w * _mm_nt(q, c) + _mm_nn(s_qk, v)
        den = e_w * jnp.sum(q * n, axis=1, keepdims=True) + jnp.sum(s_qk, axis=1, keepdims=True)
        hh = num / jnp.maximum(_abs(den), jnp.exp(-m_t))
        hg = _sigmoid(om) * hh
        mu = jnp.mean(hg, axis=1, keepdims=True)
        hc = hg - mu
        var = jnp.mean(hc * hc, axis=1, keepdims=True)
        ys.append(hc * lax.rsqrt(var + LN_EPS) * mlg[h])
    return ys, c_out, n_out, m_out


V_COL = 5
O_COL = 6
ML_SUB = 1


def _mlstm_fwd(qk, proj, gates, gate_bias, mlg, name="mlstm_fwd", gather=()):
    s = qk.shape[0]
    nc = s // CHUNK

    def body(q_ref, k_ref, v_ref, o_ref, g_ref, gb_ref, mlg_ref, y_ref, cp_ref, np_ref, mp_ref, c_s, n_s, m_s):
        ci = pl.program_id(0)

        @pl.when(ci == 0)
        def _():
            c_s[...] = jnp.zeros_like(c_s)
            n_s[...] = jnp.zeros_like(n_s)
            m_s[...] = jnp.zeros_like(m_s)

        for sub in range(ML_SUB):
            rows = slice(CHUNK * sub, CHUNK * (sub + 1))
            hs = lambda ref: [ref[rows, LANES * h:LANES * (h + 1)] for h in range(ML_HEADS)]
            cp_ref[sub] = c_s[...]
            np_ref[sub] = n_s[...]
            mp_ref[sub] = m_s[...]
            ys, c_new, n_new, m_new = _mlstm_chunk(
                hs(q_ref), hs(k_ref), hs(v_ref), hs(o_ref), g_ref[rows, :], gb_ref[...],
                [mlg_ref[:, LANES * h:LANES * (h + 1)] for h in range(ML_HEADS)],
                [c_s[h] for h in range(ML_HEADS)], [n_s[h:h + 1, :] for h in range(ML_HEADS)],
                [m_s[h:h + 1, 0:1] for h in range(ML_HEADS)])
            for h in range(ML_HEADS):
                y_ref[rows, LANES * h:LANES * (h + 1)] = ys[h]
                c_s[h] = c_new[h]
                n_s[h:h + 1, :] = n_new[h]
                m_s[h:h + 1, :] = jnp.broadcast_to(m_new[h], (1, LANES))

    blk = lambda col: pl.BlockSpec((ML_SUB * CHUNK, ML_W), lambda ci: (ci, col))
    vec = lambda w: pl.BlockSpec((1, w), lambda ci: (0, 0))
    return _call(
        body, name=name, grid=(nc // ML_SUB,), args=(qk, qk, proj, proj, gates, gate_bias, mlg), sem=("arbitrary",),
        gather=gather,
        in_specs=[blk(0), blk(1), blk(V_COL), blk(O_COL), pl.BlockSpec((ML_SUB * CHUNK, LANES), lambda ci: (ci, 0)),
                  vec(LANES), vec(ML_W)],
        out_specs=[blk(0), pl.BlockSpec((ML_SUB, ML_HEADS, ML_HD, ML_HD), lambda ci: (ci, 0, 0, 0)),
                   pl.BlockSpec((ML_SUB, 8, LANES), lambda ci: (ci, 0, 0)),
                   pl.BlockSpec((ML_SUB, 8, LANES), lambda ci: (ci, 0, 0))],
        out_shape=[jax.ShapeDtypeStruct((s, ML_W), F32), jax.ShapeDtypeStruct((nc, ML_HEADS, ML_HD, ML_HD), F32),
                   jax.ShapeDtypeStruct((nc, 8, LANES), F32), jax.ShapeDtypeStruct((nc, 8, LANES), F32)],
        scratch_shapes=[pltpu.VMEM((ML_HEADS, ML_HD, ML_HD), F32), pltpu.VMEM((8, LANES), F32),
                        pltpu.VMEM((8, LANES), F32)])


def _mlstm_bwd(qk, proj, gates, gate_bias, mlg, cprev, nprev, mprev, dy, name="mlstm_bwd", exchange=()):
    s = qk.shape[0]
    nc = s // CHUNK

    def body(q_ref, k_ref, v_ref, o_ref, g_ref, gb_ref, mlg_ref, cp_ref, np_ref, mp_ref, dy_ref,
             dqk_ref, dv_ref, do_ref, dg_ref, dgb_ref, dmlg_ref, dc_s, dn_s, dm_s, gb8, mg8):
        ci = pl.program_id(0)

        @pl.when(ci == 0)
        def _():
            dc_s[...] = jnp.zeros_like(dc_s)
            dn_s[...] = jnp.zeros_like(dn_s)
            dm_s[...] = jnp.zeros_like(dm_s)
            gb8[...] = jnp.zeros_like(gb8)
            mg8[...] = jnp.zeros_like(mg8)

        for sub in reversed(range(ML_SUB)):
            rows = slice(CHUNK * sub, CHUNK * (sub + 1))
            hs = lambda ref: [ref[rows, LANES * h:LANES * (h + 1)] for h in range(ML_HEADS)]
            prim = (hs(q_ref), hs(k_ref), hs(v_ref), hs(o_ref), g_ref[rows, :], gb_ref[...],
                    [mlg_ref[:, LANES * h:LANES * (h + 1)] for h in range(ML_HEADS)],
                    [cp_ref[sub, h] for h in range(ML_HEADS)], [np_ref[sub, h:h + 1, :] for h in range(ML_HEADS)],
                    [mp_ref[sub, h:h + 1, 0:1] for h in range(ML_HEADS)])
            _, vjp = jax.vjp(_mlstm_chunk, *prim)
            cot = (hs(dy_ref), [dc_s[h] for h in range(ML_HEADS)], [dn_s[h:h + 1, :] for h in range(ML_HEADS)],
                   [dm_s[h:h + 1, 0:1] for h in range(ML_HEADS)])
            dqs, dks, dvs, dos, dg, dgb, dmlg, dcs, dns, dms = vjp(cot)
            dg_ref[rows, :] = dg
            gb8[0:1, :] += dgb
            for h in range(ML_HEADS):
                sl = slice(LANES * h, LANES * (h + 1))
                dqk_ref[rows, sl] = dqs[h]
                dqk_ref[rows, ML_W + LANES * h:ML_W + LANES * (h + 1)] = dks[h]
                dv_ref[rows, sl] = dvs[h]
                do_ref[rows, sl] = dos[h]
                mg8[0:1, sl] += dmlg[h]
                dc_s[h] = dcs[h]
                dn_s[h:h + 1, :] = dns[h]
                dm_s[h:h + 1, :] = jnp.broadcast_to(dms[h], (1, LANES))

        @pl.when(ci == nb - 1)
        def _():
            dgb_ref[...] = gb8[0:1, :]
            dmlg_ref[...] = mg8[0:1, :]

    nb = nc // ML_SUB
    rev = lambda ci: nb - 1 - ci
    blk = lambda col: pl.BlockSpec((ML_SUB * CHUNK, ML_W), lambda ci: (rev(ci), col))
    vec = lambda w: pl.BlockSpec((1, w), lambda ci: (0, 0))
    st8 = pl.BlockSpec((ML_SUB, 8, LANES), lambda ci: (rev(ci), 0, 0))
    gsp = pl.BlockSpec((ML_SUB * CHUNK, LANES), lambda ci: (rev(ci), 0))
    return _call(
        body, name=name, grid=(nb,), sem=("arbitrary",), exchange=exchange,
        args=(qk, qk, proj, proj, gates, gate_bias, mlg, cprev, nprev, mprev, dy),
        in_specs=[blk(0), blk(1), blk(V_COL), blk(O_COL), gsp, vec(LANES), vec(ML_W),
                  pl.BlockSpec((ML_SUB, ML_HEADS, ML_HD, ML_HD), lambda ci: (rev(ci), 0, 0, 0)), st8, st8, blk(1)],
        out_specs=[pl.BlockSpec((ML_SUB * CHUNK, 2 * ML_W), lambda ci: (rev(ci), 0)), blk(0), blk(0), gsp, vec(LANES),
                   vec(ML_W)],
        out_shape=[jax.ShapeDtypeStruct((s, 2 * ML_W), F32),
                   jax.ShapeDtypeStruct((s, ML_W), F32), jax.ShapeDtypeStruct((s, ML_W), F32),
                   jax.ShapeDtypeStruct((s, LANES), F32), jax.ShapeDtypeStruct((1, LANES), F32),
                   jax.ShapeDtypeStruct((1, ML_W), F32)],
        scratch_shapes=[pltpu.VMEM((ML_HEADS, ML_HD, ML_HD), F32), pltpu.VMEM((8, LANES), F32),
                        pltpu.VMEM((8, LANES), F32), pltpu.VMEM((8, LANES), F32), pltpu.VMEM((8, ML_W), F32)])


def _xattn_tile(qs, ks, vs):
    outs = []
    for q, k, v in zip(qs, ks, vs):
        sc = _mm_nt(q, k) * (XA_HD ** -0.5)
        mx = lax.stop_gradient(jnp.max(sc, axis=1, keepdims=True))
        pe = jnp.exp(sc - mx)
        outs.append(_mm_nn(pe / jnp.sum(pe, axis=1, keepdims=True), v))
    return outs


def _xa_heads(ref):
    return [ref[:, XA_HD * h:XA_HD * (h + 1)] for h in range(XA_HEADS)]


def _xattn_fwd(q, kv, name="xattn_fwd", tm=512):
    s, d = q.shape

    def body(q_ref, k_ref, v_ref, o_ref):
        outs = _xattn_tile(_xa_heads(q_ref), _xa_heads(k_ref), _xa_heads(v_ref))
        for h in range(XA_HEADS):
            o_ref[:, XA_HD * h:XA_HD * (h + 1)] = outs[h]

    row = pl.BlockSpec((tm, d), lambda i: (i, 0))
    return pl.pallas_call(
        body, name=name, grid=(s // tm,),
        in_specs=[row, pl.BlockSpec((MEM_LEN, d), lambda i: (0, 0)), pl.BlockSpec((MEM_LEN, d), lambda i: (0, 1))],
        out_specs=row, out_shape=jax.ShapeDtypeStruct((s, d), F32),
        compiler_params=_params("parallel"),
    )(q, kv, kv)


def _xattn_bwd(q, kv, do, name="xattn_bwd", tm=512):
    s, d = q.shape

    def body(q_ref, k_ref, v_ref, do_ref, dq_ref, dkv_ref):
        i = pl.program_id(0)
        _, vjp = jax.vjp(_xattn_tile, _xa_heads(q_ref), _xa_heads(k_ref), _xa_heads(v_ref))
        dqs, dks, dvs = vjp(_xa_heads(do_ref))

        @pl.when(i == 0)
        def _():
            dkv_ref[...] = jnp.zeros_like(dkv_ref)

        for h in range(XA_HEADS):
            sl = slice(XA_HD * h, XA_HD * (h + 1))
            dq_ref[:, sl] = dqs[h]
            dkv_ref[:, sl] += dks[h]
            dkv_ref[:, d + XA_HD * h:d + XA_HD * (h + 1)] += dvs[h]

    row = pl.BlockSpec((tm, d), lambda i: (i, 0))
    return pl.pallas_call(
        body, name=name, grid=(s // tm,),
        in_specs=[row, pl.BlockSpec((MEM_LEN, d), lambda i: (0, 0)), pl.BlockSpec((MEM_LEN, d), lambda i: (0, 1)), row],
        out_specs=[row, pl.BlockSpec((MEM_LEN, 2 * d), lambda i: (0, 0))],
        out_shape=[jax.ShapeDtypeStruct((s, d), F32), jax.ShapeDtypeStruct((MEM_LEN, 2 * d), F32)],
        compiler_params=_params("arbitrary"),
    )(q, kv, kv, do)


def _loss_head(y, target, name="loss_head", tm=512):
    s, d = y.shape
    nt = s // tm

    def body(y_ref, t_ref, dy_ref, loss_ref, acc):
        i = pl.program_id(0)
        err = y_ref[...] - t_ref[...]
        dy_ref[...] = err * (1.0 / d)

        @pl.when(i == 0)
        def _():
            acc[...] = jnp.zeros_like(acc)

        acc[...] += _rowsum8(err * err)

        @pl.when(i == nt - 1)
        def _():
            tot = jnp.sum(jnp.sum(acc[...], axis=0, keepdims=True), axis=1, keepdims=True)
            loss_ref[...] = jnp.broadcast_to(tot * (0.5 / d), (1, LANES))

    row = pl.BlockSpec((tm, d), lambda i: (i, 0))
    return pl.pallas_call(
        body, name=name, grid=(nt,),
        in_specs=[row, row], out_specs=[row, pl.BlockSpec((1, LANES), lambda i: (0, 0))],
        out_shape=[jax.ShapeDtypeStruct((s, d), F32), jax.ShapeDtypeStruct((1, LANES), F32)],
        scratch_shapes=[pltpu.VMEM((8, d), F32)],
        compiler_params=_params("arbitrary"),
    )(y, target)


def _adam2d(recv, w, m, v, name, layer=None):
    rows, cols = w.shape[-2:]
    fits = [t for t in range(16, rows + 1, 16) if rows % t == 0 and t * cols <= 128 * 1024]
    tr = max(fits) if fits else rows

    def body(r_ref, w_ref, m_ref, v_ref, g_ref, d_ref, mo_ref, vo_ref):
        g = r_ref[0].astype(F32)
        for j in range(1, N_DEV):
            g = g + r_ref[j].astype(F32)
        mn = ADAM_B1 * m_ref[...] + (1.0 - ADAM_B1) * g
        vn = ADAM_B2 * v_ref[...] + (1.0 - ADAM_B2) * jnp.square(g)
        m_hat = mn / (1.0 - ADAM_B1 ** ADAM_STEP)
        v_hat = vn / (1.0 - ADAM_B2 ** ADAM_STEP)
        g_ref[...] = g
        d_ref[...] = -ADAM_LR * (m_hat / (jnp.sqrt(v_hat) + ADAM_EPS) + ADAM_WD * w_ref[...])
        mo_ref[...] = mn
        vo_ref[...] = vn

    row = pl.BlockSpec((tr, cols), lambda i: (i, 0))
    if layer is None:
        wspec = row
    else:
        wspec = pl.BlockSpec((None, None, tr, cols), lambda i: (0, layer, i, 0))
    return pl.pallas_call(
        body, name=name, grid=(rows // tr,),
        in_specs=[pl.BlockSpec((N_DEV, tr, cols), lambda i: (0, i, 0)), wspec, wspec, wspec],
        out_specs=[row] * 4, out_shape=[jax.ShapeDtypeStruct((rows, cols), F32)] * 4,
        compiler_params=_params("parallel"),
    )(recv, w, m, v)


WEIGHTS = ("rel_bias", "ln_g", "ln_b", "ffn_w_gate", "ffn_w_up", "ffn_w_down", "w_in", "conv_w", "conv_b",
           "ig_bias", "fg_bias", "ml_norm_g", "w_out", "xq_w", "xkv_w", "xo_w")
SMALL = ("rel_bias", "ln_g", "ln_b", "conv_w", "conv_b", "ig_bias", "fg_bias", "ml_norm_g")
SMALL_SHAPES = {
    "rel_bias": (REL_BUCKETS, ATT_HEADS), "ln_g": (1, 4, LANES), "ln_b": (1, 4, LANES), "conv_w": (1, CONV_K, LANES),
    "conv_b": (1, 2 * ML_W), "ig_bias": (1, ML_HEADS), "fg_bias": (1, ML_HEADS), "ml_norm_g": (1, ML_W),
}
SMALL_ROWS = 8


def _pack_small(parts, lead=()):
    out = []
    for p in parts:
        p = jnp.pad(p, [(0, 0)] * len(lead) + [(0, SMALL_ROWS * LANES - p.shape[-1])])
        out.append(p.reshape(lead + (SMALL_ROWS, LANES)))
    return jnp.concatenate(out, axis=len(lead))


def _unpack_small(flat):
    out = {}
    for i, n in enumerate(SMALL):
        cnt = int(np.prod(SMALL_SHAPES[n]))
        out[n] = flat[SMALL_ROWS * i:SMALL_ROWS * (i + 1)].reshape(-1)[:cnt].reshape(SMALL_SHAPES[n])
    return out


def _split8(full, axis):
    shp = full.shape
    t = full.reshape(shp[:axis] + (N_DEV, shp[axis] // N_DEV) + shp[axis + 1:])
    return jnp.moveaxis(t, axis, 0).reshape(N_DEV, -1)


def _rep8(full):
    return jnp.broadcast_to(full.reshape(1, -1), (N_DEV, full.size))


def kernel(x, mem, rel_bias, ln_g, ln_b, ffn_w_gate, ffn_w_up, ffn_w_down, w_in, conv_w, conv_b, ig_bias, fg_bias, ml_norm_g, w_out, xq_w, xkv_w, xo_w, loss_target, m_rel_bias, m_ln_g, m_ln_b, m_ffn_w_gate, m_ffn_w_up, m_ffn_w_down, m_w_in, m_conv_w, m_conv_b, m_ig_bias, m_fg_bias, m_ml_norm_g, m_w_out, m_xq_w, m_xkv_w, m_xo_w, v_rel_bias, v_ln_g, v_ln_b, v_ffn_w_gate, v_ffn_w_up, v_ffn_w_down, v_w_in, v_conv_w, v_conv_b, v_ig_bias, v_fg_bias, v_ml_norm_g, v_w_out, v_xq_w, v_xkv_w, v_xo_w):
    w_tree = dict(rel_bias=rel_bias, ln_g=ln_g, ln_b=ln_b, ffn_w_gate=ffn_w_gate, ffn_w_up=ffn_w_up,
                  ffn_w_down=ffn_w_down, w_in=w_in, conv_w=conv_w, conv_b=conv_b, ig_bias=ig_bias, fg_bias=fg_bias,
                  ml_norm_g=ml_norm_g, w_out=w_out, xq_w=xq_w, xkv_w=xkv_w, xo_w=xo_w)
    m_tree = dict(rel_bias=m_rel_bias, ln_g=m_ln_g, ln_b=m_ln_b, ffn_w_gate=m_ffn_w_gate, ffn_w_up=m_ffn_w_up,
                  ffn_w_down=m_ffn_w_down, w_in=m_w_in, conv_w=m_conv_w, conv_b=m_conv_b, ig_bias=m_ig_bias,
                  fg_bias=m_fg_bias, ml_norm_g=m_ml_norm_g, w_out=m_w_out, xq_w=m_xq_w, xkv_w=m_xkv_w, xo_w=m_xo_w)
    v_tree = dict(rel_bias=v_rel_bias, ln_g=v_ln_g, ln_b=v_ln_b, ffn_w_gate=v_ffn_w_gate, ffn_w_up=v_ffn_w_up,
                  ffn_w_down=v_ffn_w_down, w_in=v_w_in, conv_w=v_conv_w, conv_b=v_conv_b, ig_bias=v_ig_bias,
                  fg_bias=v_fg_bias, ml_norm_g=v_ml_norm_g, w_out=v_w_out, xq_w=v_xq_w, xkv_w=v_xkv_w, xo_w=v_xo_w)
    x0 = x[0]
    pad_ff = FF_PAD - FF_SHARD
    bf = lambda t: t.astype(BF16)

    pad_rows = lambda t: jnp.pad(t, ((0, pad_ff), (0, 0)))
    ffn_shards = [(pad_rows(bf(ffn_w_gate[0, l]).T), pad_rows(bf(ffn_w_up[0, l]).T), pad_rows(bf(ffn_w_down[0, l])))
                  for l in range(2)]
    pairs = lambda t: t.reshape(N_PAIR, FF_PAIR, D_MODEL)
    w_in_shard = jnp.pad(bf(w_in[0]), ((0, 0), (0, ATT_W - W_IN_SHARD)))
    small_shard = jnp.concatenate([ln_g[0], ln_b[0], conv_w[0], jnp.zeros((4, LANES), F32)], axis=0)
    gate_bias = jnp.pad(jnp.concatenate([ig_bias, fg_bias], axis=1), ((0, 0), (0, LANES - 2 * ML_HEADS)))
    buckets = _bucket_tables()

    wg0, wu0, wd0, small_all = _gather_two_level("ffn1_weights_gather", ffn_shards[0] + (small_shard,))
    wg0, wu0, wd0 = pairs(wg0), pairs(wu0), pairs(wd0)
    unshard = lambda t: jnp.moveaxis(t, 0, 1).reshape(4, D_MODEL)
    ln_g_full, ln_b_full, conv_w_full = unshard(small_all[:, 0:4]), unshard(small_all[:, 4:8]), unshard(small_all[:, 8:12])
    lng = lambda i: ln_g_full[i:i + 1]
    lnb = lambda i: ln_b_full[i:i + 1]

    u0, x1, win_all, wout_all, xq_all, xo_all, xkv_all = _ffn_fwd(
        x0, wg0, wu0, wd0, lng(0), lnb(0), "ffn1_fwd",
        gather=(w_in_shard, bf(w_out[0]), bf(xq_w[0]), bf(xo_w[0]), bf(xkv_w[0])))
    w_in_full = jnp.moveaxis(win_all[:, :, :W_IN_SHARD], 0, 1).reshape(D_MODEL, W_IN)
    w_main = w_in_full[:, :W_IN_MAIN]
    w_gate_cols = jnp.pad(w_in_full[:, W_IN_MAIN:], ((0, 0), (0, LANES - 2 * ML_HEADS)))
    w_out_full = wout_all.reshape(D_MODEL, D_MODEL)
    xq_full = xq_all.reshape(D_MODEL, D_MODEL)
    xo_full = xo_all.reshape(D_MODEL, D_MODEL)

    proj, wg1 = _matmul(x1, w_main, "nn", "proj_fwd", tm=1024, tk=D_MODEL, gather=(ffn_shards[1][0],))
    gates, = _matmul(x1, w_gate_cols, "nn", "gates_fwd", tk=D_MODEL)
    biasm = _bias_fwd(rel_bias, buckets)
    att, lse, wd1 = _dil_fwd(proj, biasm, gather=(ffn_shards[1][2],))
    qk = _conv_fwd(proj, conv_w_full, conv_b)
    y_m, c_prev, n_prev, m_prev, wu1 = _mlstm_fwd(qk, proj, gates, gate_bias, ml_norm_g, gather=(ffn_shards[1][1],))
    cat = jnp.concatenate([att, y_m], axis=1)
    u1, x2 = _matmul_resid_ln(cat, w_out_full, x1, lng(1), lnb(1), "w_out_fwd")
    q_x, = _matmul(x2, xq_full, "nn", "xq_fwd", tn=D_MODEL, tk=D_MODEL)
    kv, = _matmul(mem[0], xkv_all, "nn", "xkv_fwd", tk=D_MODEL)
    o_x = _xattn_fwd(q_x, kv)
    u2, x3 = _matmul_resid_ln(o_x, xo_full, x2, lng(2), lnb(2), "xo_fwd")
    wg1, wu1, wd1 = pairs(wg1), pairs(wu1), pairs(wd1)
    u3, x4 = _ffn_fwd(x3, wg1, wu1, wd1, lng(3), lnb(3), "ffn2_fwd")
    dx4, loss_row = _loss_head(x4, loss_target[0])

    dx3, xb, df, da, db, hh, dg3, db3 = _ffn_bwd_x(dx4, u3, x3, wg1, wu1, wd1, lng(3), "ffn2_bwd_x")
    ffn2_send = (_ffn_bwd_w(xb, da, "ffn2_bwd_wg", down=False)[0], _ffn_bwd_w(xb, db, "ffn2_bwd_wu", down=False)[0],
                 _ffn_bwd_w(df, hh, "ffn2_bwd_wd", down=True)[0])

    du2, dg2, db2 = _ln_bwd(dx3, u2, lng(2), "xattn_ln_bwd")
    do_x, = _matmul(du2, xo_full, "nt", "xo_bwd_x", tn=D_MODEL, tk=D_MODEL)
    g_xo, = _matmul(o_x, du2, "tn", "xo_bwd_w", tm=D_MODEL, tn=D_MODEL, out_dtype=BF16)
    dq_x, dkv = _xattn_bwd(q_x, kv, do_x)
    g_xq, = _matmul(x2, dq_x, "tn", "xq_bwd_w", tm=D_MODEL, tn=D_MODEL, out_dtype=BF16)
    g_xkv, = _matmul(mem[0], dkv, "tn", "xkv_bwd_w", tm=D_MODEL, tn=2 * D_MODEL // N_DEV, tk=MEM_LEN,
                     out_dtype=BF16, blocked_out=True)
    dx2, = _matmul(dq_x, xq_full, "nt", "xq_bwd_x", tn=D_MODEL, tk=D_MODEL, add=du2, add_scale=ALPHA)

    du1, dg1, db1 = _ln_bwd(dx2, u1, lng(1), "mixer_ln_bwd")
    dcat, = _matmul(du1, w_out_full, "nt", "w_out_bwd_x", tn=D_MODEL, tk=D_MODEL)
    g_w_out, = _matmul(cat, du1, "tn", "w_out_bwd_w", tm=D_MODEL, tn=D_MODEL, out_dtype=BF16)
    dqk, dv_m, do_m, dgates, dgate_bias, g_mlg, *ffn2_recv = _mlstm_bwd(
        qk, proj, gates, gate_bias, ml_norm_g, c_prev, n_prev, m_prev, dcat, exchange=tuple(ffn2_send))
    dqk_pre, g_conv_w, g_conv_b = _conv_bwd(proj, dqk, conv_w_full, conv_b)
    dq_a, dk_a, dv_a, dbias = _dil_bwd(proj, biasm, lse, att, dcat)
    g_rel = _bias_bwd(dbias.reshape(biasm.shape), buckets)[:, :ATT_HEADS]
    dproj = jnp.concatenate([dq_a, dk_a, dv_a, bf(dqk_pre), bf(dv_m), bf(do_m)], axis=1)
    g_w_main, = _matmul(x1, dproj, "tn", "proj_bwd_w", tm=D_MODEL, tn=W_IN_MAIN // 2, out_dtype=BF16)
    g_w_gates, = _matmul(x1, dgates, "tn", "gates_bwd_w", tm=D_MODEL, out_dtype=BF16)
    g_w_in = jnp.concatenate([g_w_main, g_w_gates[:, :2 * ML_HEADS]], axis=1)
    dx1, = _matmul(dproj, w_main, "nt", "proj_bwd_x", tn=D_MODEL, tk=W_IN_MAIN // 2, add=du1, add_scale=ALPHA)
    dx1, = _matmul(dgates, w_gate_cols, "nt", "gates_bwd_x", tn=D_MODEL, add=dx1)

    rows8 = lambda t: t.reshape(N_DEV, D_MODEL // N_DEV, D_MODEL)
    mid_send = (rows8(g_xo), rows8(g_xq), g_xkv, rows8(g_w_out),
                jnp.moveaxis(g_w_in.reshape(D_MODEL, N_DEV, W_IN_SHARD), 1, 0))
    dx0, xb, df, da, db, hh, dg0, db0, r_xo, r_xq, r_xkv, r_w_out, r_w_in = _ffn_bwd_x(
        dx1, u0, x0, wg0, wu0, wd0, lng(0), "ffn1_bwd_x", exchange=mid_send)
    small_blocks = {
        "rel_bias": _rep8(g_rel),
        "ln_g": _split8(jnp.concatenate([dg0, dg1, dg2, dg3], axis=0), 1),
        "ln_b": _split8(jnp.concatenate([db0, db1, db2, db3], axis=0), 1),
        "conv_w": _split8(g_conv_w, 1),
        "conv_b": _rep8(g_conv_b),
        "ig_bias": _rep8(dgate_bias[:, :ML_HEADS]),
        "fg_bias": _rep8(dgate_bias[:, ML_HEADS:2 * ML_HEADS]),
        "ml_norm_g": _rep8(g_mlg),
    }
    small_send = _pack_small([small_blocks[n] for n in SMALL], lead=(N_DEV,))
    g_wg, r_small = _ffn_bwd_w(xb, da, "ffn1_bwd_wg", down=False, exchange=(small_send,))
    g_wu, r_wg = _ffn_bwd_w(xb, db, "ffn1_bwd_wu", down=False, exchange=(g_wg,))
    g_wd, r_wu = _ffn_bwd_w(df, hh, "ffn1_bwd_wd", down=True, exchange=(g_wu,))
    r_wd, = _exchange_only("ffn1_grads_exchange", exchange=(g_wd,))
    ffn1_recv = [r_wg, r_wu, r_wd]

    res = {}
    for i, n in enumerate(("ffn_w_gate", "ffn_w_up", "ffn_w_down")):
        per_layer = [_adam2d(r[i], w_tree[n], m_tree[n], v_tree[n], f"adamw_{n}_{l}", layer=l)
                     for l, r in enumerate((ffn1_recv, ffn2_recv))]
        res[n] = [jnp.stack([per_layer[0][j], per_layer[1][j]])[None] for j in range(4)]
    for n, r in (("w_in", r_w_in), ("w_out", r_w_out), ("xq_w", r_xq), ("xkv_w", r_xkv), ("xo_w", r_xo)):
        res[n] = [t[None] for t in _adam2d(r, w_tree[n][0], m_tree[n][0], v_tree[n][0], f"adamw_{n}")]
    pack = lambda tree: _pack_small([tree[n].reshape(-1) for n in SMALL])
    small = [_unpack_small(t) for t in _adam2d(r_small, pack(w_tree), pack(m_tree), pack(v_tree), "adamw_small")]
    for n in SMALL:
        res[n] = [small[j][n] for j in range(4)]

    loss = lax.psum(loss_row[0, 0], ("x", "y", "c"))
    return (loss, dx0[None], *[res[n][0] for n in WEIGHTS], *[res[n][1] for n in WEIGHTS],
            *[res[n][2] for n in WEIGHTS], *[res[n][3] for n in WEIGHTS])
```

```python
import functools
import math

import numpy as np
import jax
import jax.numpy as jnp
from jax import lax
from jax.experimental import pallas as pl
from jax.experimental.pallas import tpu as pltpu

F32 = jnp.float32
BF16 = jnp.bfloat16

N_DEV = 8
D_MODEL = 1024
D_FF = 2816
FF_SHARD = D_FF // N_DEV
FF_PAD = 384
ATT_W = 512
ATT_HEADS = 8
DILATED = ((128, 1), (512, 4), (2048, 16))
BLK = 128
ML_W = 512
ML_HEADS = 4
ML_HD = 128
CHUNK = 128
CONV_K = 4
W_IN = 3592
W_IN_SHARD = W_IN // N_DEV
W_IN_MAIN = 3584
XA_HEADS = 4
XA_HD = 256
MEM_LEN = 256
REL_BUCKETS = 32
REL_MAX_DIST = 2048
ALPHA = 2.0 ** 0.25
LN_EPS = 1e-5
NEG = -1e30
ADAM_LR = 0.001
ADAM_B1 = 0.9
ADAM_B2 = 0.999
ADAM_EPS = 1e-08
ADAM_WD = 0.01
ADAM_STEP = 10
LANES = 128
VMEM_LIMIT = 58 * 1024 * 1024

NN = (((1,), (0,)), ((), ()))
NT = (((1,), (1,)), ((), ()))
TN = (((0,), (0,)), ((), ()))


def _dot(a, b, dims):
    return lax.dot_general(a, b, dims, preferred_element_type=F32)


def _params(*sem):
    return pltpu.CompilerParams(dimension_semantics=sem, vmem_limit_bytes=VMEM_LIMIT)


def _sigmoid(x):
    return 1.0 / (1.0 + jnp.exp(-x))


def _rowsum8(x):
    t, c = x.shape
    return jnp.sum(x.reshape(t // 8, 8, c), axis=0)


def _mesh_pos():
    x, y, c = lax.axis_index("x"), lax.axis_index("y"), lax.axis_index("c")
    return x, y, c, 4 * x + 2 * y + c


def _peer(x, y, c, k):
    px = 1 - x if k & 4 else x
    py = 1 - y if k & 2 else y
    pc = 1 - c if k & 1 else c
    return (px, py, pc), 4 * px + 2 * py + pc


def _call(body, *, name, grid, in_specs, out_specs, out_shape, args, scratch_shapes=(), sem=None,
          gather=(), exchange=()):
    in_specs, out_specs, out_shape, scratch = list(in_specs), list(out_specs), list(out_shape), list(scratch_shapes)
    ng, nc = len(gather), len(gather) + len(exchange)
    if nc == 0:
        return pl.pallas_call(body, name=name, grid=grid, in_specs=in_specs, out_specs=out_specs,
                              out_shape=out_shape, scratch_shapes=scratch, compiler_params=_params(*sem))(*args)
    n_in, n_out, n_scr = len(in_specs), len(out_specs), len(scratch)

    def wrapped(*refs):
        ins, cin = refs[:n_in], refs[n_in:n_in + nc]
        outs, cout = refs[n_in + nc:n_in + nc + n_out], refs[n_in + nc + n_out:n_in + 2 * nc + n_out]
        scr = refs[n_in + 2 * nc + n_out:n_in + 2 * nc + n_out + n_scr]
        send_sems, recv_sems, loc_sems = refs[-3:]
        first, last = None, None
        for ax, extent in enumerate(grid):
            f, l = pl.program_id(ax) == 0, pl.program_id(ax) == extent - 1
            first = f if first is None else first & f
            last = l if last is None else last & l

        def copies():
            x, y, c, me = _mesh_pos()
            out = []
            for a in range(nc):
                mine = cin[a] if a < ng else cin[a].at[me]
                out.append(pltpu.make_async_copy(mine, cout[a].at[me], loc_sems.at[a]))
                for k in range(1, N_DEV):
                    peer, pidx = _peer(x, y, c, k)
                    out.append(pltpu.make_async_remote_copy(
                        src_ref=cin[a] if a < ng else cin[a].at[pidx], dst_ref=cout[a].at[me],
                        send_sem=send_sems.at[a, k - 1], recv_sem=recv_sems.at[a, k - 1],
                        device_id=peer, device_id_type=pl.DeviceIdType.MESH))
            return out

        @pl.when(first)
        def _():
            for cp in copies():
                cp.start()

        body(*ins, *outs, *scr)

        @pl.when(last)
        def _():
            for cp in copies():
                cp.wait()

    hbm = pl.BlockSpec(memory_space=pl.ANY)
    comm_shapes = [jax.ShapeDtypeStruct((N_DEV,) + a.shape, a.dtype) for a in gather]
    comm_shapes += [jax.ShapeDtypeStruct(a.shape, a.dtype) for a in exchange]
    return pl.pallas_call(
        wrapped, name=name, grid=grid, in_specs=in_specs + [hbm] * nc, out_specs=out_specs + [hbm] * nc,
        out_shape=out_shape + comm_shapes,
        scratch_shapes=scratch + [pltpu.SemaphoreType.DMA((nc, N_DEV - 1)), pltpu.SemaphoreType.DMA((nc, N_DEV - 1)),
                                  pltpu.SemaphoreType.DMA((nc,))],
        compiler_params=_params(*(("arbitrary",) * len(grid))),
    )(*args, *gather, *exchange)


def _gather_two_level(name, arrays):
    na = len(arrays)

    def body(*refs):
        srcs, outs = refs[:na], refs[na:2 * na]
        send_sems, recv_sems, loc_sems = refs[2 * na:]
        x, y, c, me = _mesh_pos()
        here, sib = (x, y, c), (x, y, 1 - c)
        chips = [(1 - x, y), (x, 1 - y), (1 - x, 1 - y)]
        pos = lambda px, py, pc: 4 * px + 2 * py + pc

        def copy(a, k, block, to, src=None):
            return pltpu.make_async_remote_copy(
                src_ref=outs[a].at[block] if src is None else src, dst_ref=outs[a].at[block],
                send_sem=send_sems.at[a, k], recv_sem=recv_sems.at[a, k], device_id=to,
                device_id_type=pl.DeviceIdType.MESH)

        locs = [pltpu.make_async_copy(srcs[a], outs[a].at[me], loc_sems.at[a]) for a in range(na)]
        for cp in locs:
            cp.start()
        first = []
        for a in range(na):
            first.append(copy(a, 0, me, sib, src=srcs[a]))
            first += [copy(a, 1 + j, me, (*chip, c), src=srcs[a]) for j, chip in enumerate(chips)]
        for cp in first:
            cp.start()
        passed = []
        for a in range(na):
            for j, chip in enumerate(chips):
                copy(a, 1 + j, pos(*chip, c), here).wait_recv()
                passed.append(copy(a, 4 + j, pos(*chip, c), sib))
                passed[-1].start()
        for a in range(na):
            copy(a, 0, pos(x, y, 1 - c), here).wait_recv()
            for j, chip in enumerate(chips):
                copy(a, 4 + j, pos(*chip, 1 - c), here).wait_recv()
        for cp in first + passed:
            cp.wait_send()
        for cp in locs:
            cp.wait()

    hbm = pl.BlockSpec(memory_space=pl.ANY)
    return pl.pallas_call(
        body, name=name, in_specs=[hbm] * na, out_specs=[hbm] * na,
        out_shape=[jax.ShapeDtypeStruct((N_DEV,) + a.shape, a.dtype) for a in arrays],
        scratch_shapes=[pltpu.SemaphoreType.DMA((na, N_DEV - 1)), pltpu.SemaphoreType.DMA((na, N_DEV - 1)),
                        pltpu.SemaphoreType.DMA((na,))],
    )(*arrays)


def _exchange_only(name, gather=(), exchange=()):
    return _call(lambda: None, name=name, grid=(1,), in_specs=[], out_specs=[], out_shape=[], args=(),
                 gather=gather, exchange=exchange)


def _matmul(a, b, mode, name, *, out_dtype=F32, tm=1024, tn=512, tk=512, add=None, add_scale=1.0,
            blocked_out=False, gather=(), exchange=()):
    blocked_b = b.ndim == 3
    if blocked_b:
        (m, k), (nb, _, tn) = a.shape, b.shape
        n = nb * tn
    elif mode == "nn":
        (m, k), (_, n) = a.shape, b.shape
    elif mode == "nt":
        (m, k), (n, _) = a.shape, b.shape
    else:
        (k, m), (_, n) = a.shape, b.shape
    tm, tn, tk = min(tm, m), min(tn, n), min(tk, k)
    nk = k // tk
    dims = {"nn": NN, "nt": NT, "tn": TN}[mode]
    if mode == "tn":
        a_spec = pl.BlockSpec((tk, tm), lambda i, j, kk: (kk, i))
    else:
        a_spec = pl.BlockSpec((tm, tk), lambda i, j, kk: (i, kk))
    if blocked_b:
        b_spec = pl.BlockSpec((None, tk, tn), lambda i, j, kk: (j, kk, 0))
    elif mode == "nt":
        b_spec = pl.BlockSpec((tn, tk), lambda i, j, kk: (j, kk))
    else:
        b_spec = pl.BlockSpec((tk, tn), lambda i, j, kk: (kk, j))
    if blocked_out:
        o_spec = pl.BlockSpec((None, tm, tn), lambda i, j, kk: (j, i, 0))
        o_shape = jax.ShapeDtypeStruct((n // tn, m, tn), out_dtype)
    else:
        o_spec = pl.BlockSpec((tm, tn), lambda i, j, kk: (i, j))
        o_shape = jax.ShapeDtypeStruct((m, n), out_dtype)
    has_add = add is not None
    cache_a = nk == 1 and mode != "tn" and n // tn > 1 and a.dtype != BF16

    def body(*refs):
        if has_add:
            a_ref, b_ref, add_ref, o_ref, s_ref = refs
        else:
            a_ref, b_ref, o_ref, s_ref = refs
        kk = pl.program_id(2)
        if cache_a:
            @pl.when(pl.program_id(1) == 0)
            def _():
                s_ref[...] = a_ref[...].astype(BF16)

            lhs = s_ref[...]
        else:
            lhs = a_ref[...].astype(BF16)
        part = _dot(lhs, b_ref[...].astype(BF16), dims)

        def finish(r):
            if has_add:
                r = r + add_scale * add_ref[...]
            o_ref[...] = r.astype(out_dtype)

        if nk == 1:
            finish(part)
            return

        @pl.when(kk == 0)
        def _():
            s_ref[...] = part

        @pl.when(kk > 0)
        def _():
            s_ref[...] += part

        @pl.when(kk == nk - 1)
        def _():
            finish(s_ref[...])

    if nk > 1:
        scratch = [pltpu.VMEM((tm, tn), F32)]
    else:
        scratch = [pltpu.VMEM((tm, tk), BF16) if cache_a else pltpu.VMEM((8, LANES), F32)]
    return _call(
        body, name=name, grid=(m // tm, n // tn, nk),
        in_specs=[a_spec, b_spec] + ([pl.BlockSpec((tm, tn), lambda i, j, kk: (i, j))] if has_add else []),
        out_specs=[o_spec], out_shape=[o_shape], args=(a, b) + ((add,) if has_add else ()),
        scratch_shapes=scratch, sem=("parallel", "arbitrary", "arbitrary"),
        gather=gather, exchange=exchange)


def _ln_fwd_math(u, g, b):
    mu = jnp.mean(u, axis=-1, keepdims=True)
    uc = u - mu
    var = jnp.mean(uc * uc, axis=-1, keepdims=True)
    return uc * lax.rsqrt(var + LN_EPS) * g + b


def _ln_bwd_math(dy, u, g):
    mu = jnp.mean(u, axis=-1, keepdims=True)
    uc = u - mu
    var = jnp.mean(uc * uc, axis=-1, keepdims=True)
    rstd = lax.rsqrt(var + LN_EPS)
    xhat = uc * rstd
    dxh = dy * g
    m1 = jnp.mean(dxh, axis=-1, keepdims=True)
    m2 = jnp.mean(dxh * xhat, axis=-1, keepdims=True)
    return rstd * (dxh - m1 - xhat * m2), xhat


def _matmul_resid_ln(a, w, x, g, b, name, tm=1024):
    s, k = a.shape
    d = w.shape[1]

    def body(a_ref, w_ref, x_ref, g_ref, b_ref, u_ref, y_ref):
        u = ALPHA * x_ref[...] + _dot(a_ref[...].astype(BF16), w_ref[...], NN)
        u_ref[...] = u
        y_ref[...] = _ln_fwd_math(u, g_ref[...], b_ref[...])

    row = pl.BlockSpec((tm, d), lambda i: (i, 0))
    vec = pl.BlockSpec((1, d), lambda i: (0, 0))
    return pl.pallas_call(
        body, name=name, grid=(s // tm,),
        in_specs=[pl.BlockSpec((tm, k), lambda i: (i, 0)), pl.BlockSpec((k, d), lambda i: (0, 0)), row, vec, vec],
        out_specs=[row, row], out_shape=[jax.ShapeDtypeStruct((s, d), F32)] * 2,
        compiler_params=_params("parallel"),
    )(a, w, x, g, b)


def _ln_bwd(dy, u, g, name, tm=1024):
    s, d = dy.shape
    nt = s // tm

    def body(dy_ref, u_ref, g_ref, du_ref, dg_ref, db_ref, g8, b8):
        i = pl.program_id(0)
        dy_ = dy_ref[...]
        du, xhat = _ln_bwd_math(dy_, u_ref[...], g_ref[...])
        du_ref[...] = du

        @pl.when(i == 0)
        def _():
            g8[...] = jnp.zeros_like(g8)
            b8[...] = jnp.zeros_like(b8)

        g8[...] += _rowsum8(dy_ * xhat)
        b8[...] += _rowsum8(dy_)

        @pl.when(i == nt - 1)
        def _():
            dg_ref[...] = jnp.sum(g8[...], axis=0, keepdims=True)
            db_ref[...] = jnp.sum(b8[...], axis=0, keepdims=True)

    row = pl.BlockSpec((tm, d), lambda i: (i, 0))
    vec = pl.BlockSpec((1, d), lambda i: (0, 0))
    return pl.pallas_call(
        body, name=name, grid=(nt,),
        in_specs=[row, row, vec], out_specs=[row, vec, vec],
        out_shape=[jax.ShapeDtypeStruct((s, d), F32), jax.ShapeDtypeStruct((1, d), F32),
                   jax.ShapeDtypeStruct((1, d), F32)],
        scratch_shapes=[pltpu.VMEM((8, d), F32), pltpu.VMEM((8, d), F32)],
        compiler_params=_params("arbitrary"),
    )(dy, u, g)


FF_PAIR = 2 * FF_PAD
N_PAIR = N_DEV // 2


def _ffn_fwd(x, wgt, wut, wd, g, b, name, tm=1024, gather=()):
    s, d = x.shape

    def body(x_ref, wg_ref, wu_ref, wd_ref, g_ref, b_ref, u_ref, y_ref, xb, acc):
        k = pl.program_id(1)

        @pl.when(k == 0)
        def _():
            xb[...] = x_ref[...].astype(BF16)

        a = _dot(xb[...], wg_ref[...], NT)
        bb = _dot(xb[...], wu_ref[...], NT)
        h = (a * _sigmoid(a) * bb).astype(BF16)
        part = _dot(h, wd_ref[...], NN)

        @pl.when(k == 0)
        def _():
            acc[...] = part

        @pl.when(k > 0)
        def _():
            acc[...] += part

        @pl.when(k == N_PAIR - 1)
        def _():
            u = ALPHA * x_ref[...] + 0.5 * acc[...]
            u_ref[...] = u
            y_ref[...] = _ln_fwd_math(u, g_ref[...], b_ref[...])

    row = pl.BlockSpec((tm, d), lambda i, k: (i, 0))
    vec = pl.BlockSpec((1, d), lambda i, k: (0, 0))
    w_in = pl.BlockSpec((None, FF_PAIR, d), lambda i, k: (k, 0, 0))
    w_dn = w_in
    return _call(
        body, name=name, grid=(s // tm, N_PAIR),
        in_specs=[row, w_in, w_in, w_dn, vec, vec], out_specs=[row, row],
        out_shape=[jax.ShapeDtypeStruct((s, d), F32)] * 2, args=(x, wgt, wut, wd, g, b),
        scratch_shapes=[pltpu.VMEM((tm, d), BF16), pltpu.VMEM((tm, d), F32)],
        sem=("parallel", "arbitrary"), gather=gather)


def _ffn_bwd_x(dy, u, x, wgt, wut, wd, g, name, tm=512, exchange=()):
    s, d = x.shape
    nt = s // tm
    ffp = N_DEV * FF_PAD

    def body(dy_ref, u_ref, x_ref, wg_ref, wu_ref, wd_ref, g_ref,
             dx_ref, xb, df_ref, da_ref, db_ref, h_ref, dg_ref, dbl_ref,
             dfb, du_s, acc, g8, b8):
        i = pl.program_id(0)
        k = pl.program_id(1)

        @pl.when(k == 0)
        def _():
            dy_ = dy_ref[...]
            du, xhat = _ln_bwd_math(dy_, u_ref[...], g_ref[...])
            du_s[...] = du
            dfb[...] = (0.5 * du).astype(BF16)
            df_ref[...] = dfb[...]
            xb[...] = x_ref[...].astype(BF16)

            @pl.when(i == 0)
            def _():
                g8[...] = jnp.zeros_like(g8)
                b8[...] = jnp.zeros_like(b8)

            g8[...] += _rowsum8(dy_ * xhat)
            b8[...] += _rowsum8(dy_)

        a = _dot(xb[...], wg_ref[...], NT)
        bb = _dot(xb[...], wu_ref[...], NT)
        sig = _sigmoid(a)
        sa = a * sig
        h_ref[...] = (sa * bb).astype(BF16)
        dh = _dot(dfb[...], wd_ref[...], NT)
        da = (dh * bb * (sig * (1.0 + a * (1.0 - sig)))).astype(BF16)
        db = (dh * sa).astype(BF16)
        da_ref[...] = da
        db_ref[...] = db
        part = _dot(da, wg_ref[...], NN) + _dot(db, wu_ref[...], NN)

        @pl.when(k == 0)
        def _():
            acc[...] = part

        @pl.when(k > 0)
        def _():
            acc[...] += part

        @pl.when(k == N_PAIR - 1)
        def _():
            dx_ref[...] = ALPHA * du_s[...] + acc[...]

        @pl.when((k == N_PAIR - 1) & (i == nt - 1))
        def _():
            dg_ref[...] = jnp.sum(g8[...], axis=0, keepdims=True)
            dbl_ref[...] = jnp.sum(b8[...], axis=0, keepdims=True)

    row = pl.BlockSpec((tm, d), lambda i, k: (i, 0))
    vec = pl.BlockSpec((1, d), lambda i, k: (0, 0))
    w_in = pl.BlockSpec((None, FF_PAIR, d), lambda i, k: (k, 0, 0))
    hid = pl.BlockSpec((tm, FF_PAIR), lambda i, k: (i, k))
    return _call(
        body, name=name, grid=(nt, N_PAIR),
        in_specs=[row, row, row, w_in, w_in, w_in, vec],
        out_specs=[row, row, row, hid, hid, hid, vec, vec],
        out_shape=[jax.ShapeDtypeStruct((s, d), F32), jax.ShapeDtypeStruct((s, d), BF16),
                   jax.ShapeDtypeStruct((s, d), BF16),
                   jax.ShapeDtypeStruct((s, ffp), BF16), jax.ShapeDtypeStruct((s, ffp), BF16),
                   jax.ShapeDtypeStruct((s, ffp), BF16),
                   jax.ShapeDtypeStruct((1, d), F32), jax.ShapeDtypeStruct((1, d), F32)],
        args=(dy, u, x, wgt, wut, wd, g),
        scratch_shapes=[pltpu.VMEM((tm, d), BF16), pltpu.VMEM((tm, d), F32),
                        pltpu.VMEM((tm, d), F32), pltpu.VMEM((8, d), F32), pltpu.VMEM((8, d), F32)],
        sem=("arbitrary", "arbitrary"), exchange=exchange)


def _ffn_bwd_w(tok, hid, name, *, down, tm=2048, exchange=()):
    s, d = tok.shape
    nt = s // tm

    def body(t_ref, h_ref, dw_ref, acc):
        i = pl.program_id(1)
        part = _dot(h_ref[...], t_ref[...], TN) if down else _dot(t_ref[...], h_ref[...], TN)

        @pl.when(i == 0)
        def _():
            acc[...] = part

        @pl.when(i > 0)
        def _():
            acc[...] += part

        @pl.when(i == nt - 1)
        def _():
            for j in range(2):
                lo = j * FF_PAD
                dw_ref[j] = (acc[lo:lo + FF_SHARD, :] if down else acc[:, lo:lo + FF_SHARD]).astype(BF16)

    blk = (FF_SHARD, d) if down else (d, FF_SHARD)
    return _call(
        body, name=name, grid=(N_PAIR, nt),
        in_specs=[pl.BlockSpec((tm, d), lambda k, i: (i, 0)), pl.BlockSpec((tm, FF_PAIR), lambda k, i: (i, k))],
        out_specs=[pl.BlockSpec((2,) + blk, lambda k, i: (k, 0, 0))],
        out_shape=[jax.ShapeDtypeStruct((N_DEV,) + blk, BF16)], args=(tok, hid),
        scratch_shapes=[pltpu.VMEM((FF_PAIR, d) if down else (d, FF_PAIR), F32)],
        sem=("parallel", "arbitrary"), exchange=exchange)


def _bucket_tables():
    qi = np.arange(BLK)[:, None]
    ki = np.arange(2 * BLK)[None, :]
    off = qi + BLK - ki
    out = []
    for window, dil in DILATED:
        n_keys = window // dil
        dist = dil * np.clip(off, 0, n_keys)
        exact = REL_BUCKETS // 2
        df = np.maximum(dist, 1).astype(np.float32)
        large = exact + (np.log(df / np.float32(exact)) / np.float32(math.log(REL_MAX_DIST / exact))
                         * np.float32(REL_BUCKETS - exact)).astype(np.int32)
        large = np.minimum(large, REL_BUCKETS - 1)
        bucket = np.where(dist < exact, dist, large).astype(np.int32)
        band = (off >= 0) & (off <= n_keys)
        out.append(np.where(band, bucket, -1))
    return np.stack(out).astype(np.int32)


def _bias_fwd(rel_bias, buckets, name="bias_fwd"):
    def body(tbl_ref, bkt_ref, out_ref):
        bkt = bkt_ref[...]
        for h in range(ATT_HEADS):
            acc = jnp.full((BLK, 2 * BLK), NEG, F32)
            for bb in range(REL_BUCKETS):
                acc = jnp.where(bkt == bb, tbl_ref[bb, h], acc)
            out_ref[h] = acc

    nbr = len(DILATED)
    return pl.pallas_call(
        body, name=name, grid=(nbr,),
        in_specs=[pl.BlockSpec(memory_space=pltpu.SMEM),
                  pl.BlockSpec((None, BLK, 2 * BLK), lambda r: (r, 0, 0))],
        out_specs=pl.BlockSpec((None, ATT_HEADS, BLK, 2 * BLK), lambda r: (r, 0, 0, 0)),
        out_shape=jax.ShapeDtypeStruct((nbr, ATT_HEADS, BLK, 2 * BLK), F32),
        compiler_params=_params("parallel"),
    )(rel_bias, buckets)


def _bias_bwd(dbias, buckets, name="bias_bwd"):
    nbr = len(DILATED)

    def body(db_ref, bkt_ref, out_ref):
        r = pl.program_id(0)

        @pl.when(r == 0)
        def _():
            out_ref[...] = jnp.zeros_like(out_ref)

        bkt = bkt_ref[...]
        rowi = lax.broadcasted_iota(jnp.int32, (REL_BUCKETS, LANES), 0)
        coli = lax.broadcasted_iota(jnp.int32, (REL_BUCKETS, LANES), 1)
        acc = jnp.zeros((REL_BUCKETS, LANES), F32)
        for h in range(ATT_HEADS):
            x = db_ref[h]
            for bb in range(REL_BUCKETS):
                part = jnp.sum(jnp.where(bkt == bb, x, 0.0), axis=0, keepdims=True)
                tot = jnp.sum(part, axis=1, keepdims=True)
                acc = acc + jnp.where((rowi == bb) & (coli == h), tot, 0.0)
        out_ref[...] += acc

    return pl.pallas_call(
        body, name=name, grid=(nbr,),
        in_specs=[pl.BlockSpec((None, ATT_HEADS, BLK, 2 * BLK), lambda r: (r, 0, 0, 0)),
                  pl.BlockSpec((None, BLK, 2 * BLK), lambda r: (r, 0, 0))],
        out_specs=pl.BlockSpec((REL_BUCKETS, LANES), lambda r: (0, 0)),
        out_shape=jax.ShapeDtypeStruct((REL_BUCKETS, LANES), F32),
        compiler_params=_params("arbitrary"),
    )(dbias, buckets)


def _stack_heads(pair, lo):
    return jnp.concatenate([jnp.where(lo, pair, 0.0), jnp.where(lo, 0.0, pair)], axis=0)


def _head_cols(pair, lo, reduce):
    fill = -jnp.inf if reduce is jnp.max else 0.0
    return jnp.concatenate([reduce(jnp.where(lo, pair, fill), axis=1, keepdims=True),
                            reduce(jnp.where(lo, fill, pair), axis=1, keepdims=True)], axis=0)


def _unstack_heads(x2, lo):
    return jnp.where(lo, x2[:BLK], x2[BLK:])


def _att_scores(q2, kk, bias2, first_ok):
    sc = _dot(q2, kk, NT) * (64 ** -0.5) + bias2
    return jnp.where(first_ok, sc, NEG)


DIL_TILE = 2048
DIL_COLS = ATT_W // LANES
DIL_UNROLL_FWD = 16
DIL_UNROLL_BWD = 16


def _dil_rows(dil, n, r, base=0):
    start = base + n * (BLK * dil) + r
    return pl.ds(start, BLK, stride=dil) if dil > 1 else pl.ds(start, BLK)


def _dil_in_specs(tile_of):
    cur = lambda col: pl.BlockSpec((DIL_TILE, LANES), lambda p, i: (tile_of(i), col * DIL_COLS + p))
    prev = lambda col: pl.BlockSpec((DIL_TILE, LANES), lambda p, i: (jnp.maximum(tile_of(i) - 1, 0), col * DIL_COLS + p))
    bias = pl.BlockSpec((len(DILATED), None, 2 * BLK, 2 * BLK), lambda p, i: (0, p, 0, 0))
    return [cur(0), prev(1), cur(1), prev(2), cur(2), bias]


def _pair_bias(biasm):
    return biasm.reshape(len(DILATED), DIL_COLS, 2 * BLK, 2 * BLK)


def _dil_fwd(proj, biasm, name="dil_fwd", gather=()):
    s = proj.shape[0]
    nt = s // DIL_TILE
    tt = DIL_TILE

    def body(q_ref, kp_ref, kc_ref, vp_ref, vc_ref, bias_ref, att_ref, lse_ref, k2, v2, ob, lb):
        t = pl.program_id(1)
        k2[0:tt, :] = kp_ref[...]
        k2[tt:2 * tt, :] = kc_ref[...]
        v2[0:tt, :] = vp_ref[...]
        v2[tt:2 * tt, :] = vc_ref[...]
        lo = lax.broadcasted_iota(jnp.int32, (BLK, LANES), 1) < 64
        kidx = lax.broadcasted_iota(jnp.int32, (2 * BLK, 2 * BLK), 1)
        for b, (_, dil) in enumerate(DILATED):
            nblk = tt // (BLK * dil)

            def step(j, carry, b=b, dil=dil, nblk=nblk):
                r, n = j % dil, j // dil
                cur, prev = _dil_rows(dil, n, r, tt), _dil_rows(dil, n - 1, r, tt)
                here = _dil_rows(dil, n, r)
                q_pair = q_ref[here, :]
                kk = jnp.concatenate([k2[prev, :], k2[cur, :]], axis=0).astype(BF16)
                vv = jnp.concatenate([v2[prev, :], v2[cur, :]], axis=0).astype(BF16)
                first_ok = (t > 0) | (n > 0) | (kidx >= BLK)
                sc = _att_scores(_stack_heads(q_pair, lo).astype(BF16), kk, bias_ref[b], first_ok)
                mx = jnp.max(sc, axis=1, keepdims=True)
                pe = jnp.exp(sc - mx)
                l = jnp.sum(pe, axis=1, keepdims=True)
                ob.at[b][here, :] = _unstack_heads(_dot(pe.astype(BF16), vv, NN) / l, lo)
                lb.at[b][here, :] = _unstack_heads(jnp.broadcast_to(mx + jnp.log(l), (2 * BLK, LANES)), lo)
                return carry

            lax.fori_loop(0, tt // BLK, step, 0, unroll=DIL_UNROLL_FWD)
        l0, l1, l2 = lb[0], lb[1], lb[2]
        mx = jnp.maximum(jnp.maximum(l0, l1), l2)
        e0, e1, e2 = jnp.exp(l0 - mx), jnp.exp(l1 - mx), jnp.exp(l2 - mx)
        tot = e0 + e1 + e2
        att_ref[...] = (e0 * ob[0] + e1 * ob[1] + e2 * ob[2]) / tot
        lse_ref[...] = mx + jnp.log(tot)

    out = pl.BlockSpec((tt, LANES), lambda p, i: (i, p))
    return _call(
        body, name=name, grid=(DIL_COLS, nt), in_specs=_dil_in_specs(lambda i: i), out_specs=[out, out],
        out_shape=[jax.ShapeDtypeStruct((s, ATT_W), F32)] * 2, args=(proj, proj, proj, proj, proj, _pair_bias(biasm)),
        scratch_shapes=[pltpu.VMEM((2 * tt, LANES), F32), pltpu.VMEM((2 * tt, LANES), F32),
                        pltpu.VMEM((len(DILATED), tt, LANES), F32), pltpu.VMEM((len(DILATED), tt, LANES), F32)],
        sem=("parallel", "parallel"), gather=gather)


def _dil_bwd(proj, biasm, lse, att, dcat, name="dil_bwd"):
    s = proj.shape[0]
    nt = s // DIL_TILE
    tt = DIL_TILE
    nbr = len(DILATED)

    def body(q_ref, kp_ref, kc_ref, vp_ref, vc_ref, bias_ref, lse_ref, att_ref, datt_ref,
             dq_ref, dk_ref, dv_ref, dbias_ref, k2, v2, dqa, dka, dva, kcar, vcar):
        i = pl.program_id(1)
        t = nt - 1 - i
        k2[0:tt, :] = kp_ref[...]
        k2[tt:2 * tt, :] = kc_ref[...]
        v2[0:tt, :] = vp_ref[...]
        v2[tt:2 * tt, :] = vc_ref[...]

        @pl.when(i == 0)
        def _():
            kcar[...] = jnp.zeros_like(kcar)
            vcar[...] = jnp.zeros_like(vcar)
            dbias_ref[...] = jnp.zeros_like(dbias_ref)

        dqa[...] = jnp.zeros_like(dqa)
        dka[0:tt, :] = jnp.zeros((tt, LANES), F32)
        dva[0:tt, :] = jnp.zeros((tt, LANES), F32)
        dka[tt:2 * tt, :] = kcar[...]
        dva[tt:2 * tt, :] = vcar[...]
        lo = lax.broadcasted_iota(jnp.int32, (BLK, LANES), 1) < 64
        kidx = lax.broadcasted_iota(jnp.int32, (2 * BLK, 2 * BLK), 1)
        for b, (_, dil) in enumerate(DILATED):
            nblk = tt // (BLK * dil)

            def step(j, carry, b=b, dil=dil, nblk=nblk):
                r, n = j % dil, j // dil
                cur, prev = _dil_rows(dil, n, r, tt), _dil_rows(dil, n - 1, r, tt)
                here = _dil_rows(dil, n, r)
                q_pair = q_ref[here, :]
                kk = jnp.concatenate([k2[prev, :], k2[cur, :]], axis=0).astype(BF16)
                vv = jnp.concatenate([v2[prev, :], v2[cur, :]], axis=0).astype(BF16)
                first_ok = (t > 0) | (n > 0) | (kidx >= BLK)
                dat_pair = datt_ref[here, :]
                q2 = _stack_heads(q_pair, lo).astype(BF16)
                dom = _stack_heads(dat_pair, lo).astype(BF16)
                sc = _att_scores(q2, kk, bias_ref[b], first_ok)
                pr = jnp.exp(sc - _head_cols(lse_ref[here, :], lo, jnp.max))
                ds = pr * (_dot(dom, vv, NT) - _head_cols(dat_pair * att_ref[here, :], lo, jnp.sum))
                dbias_ref[b] += ds
                dsb = (ds * (64 ** -0.5)).astype(BF16)
                dk2 = _dot(dsb, q2, TN)
                dv2 = _dot(pr.astype(BF16), dom, TN)
                dqa[here, :] += _unstack_heads(_dot(dsb, kk, NN), lo)
                dka[prev, :] += dk2[:BLK]
                dka[cur, :] += dk2[BLK:]
                dva[prev, :] += dv2[:BLK]
                dva[cur, :] += dv2[BLK:]
                return carry

            lax.fori_loop(0, tt // BLK, step, 0, unroll=DIL_UNROLL_BWD)
        dq_ref[...] = dqa[...].astype(BF16)
        dk_ref[...] = dka[tt:2 * tt, :].astype(BF16)
        dv_ref[...] = dva[tt:2 * tt, :].astype(BF16)
        kcar[...] = dka[0:tt, :]
        vcar[...] = dva[0:tt, :]

    rev = lambda i: nt - 1 - i
    out = pl.BlockSpec((tt, LANES), lambda p, i: (rev(i), p))
    two = lambda: pltpu.VMEM((2 * tt, LANES), F32)
    one = lambda: pltpu.VMEM((tt, LANES), F32)
    return pl.pallas_call(
        body, name=name, grid=(DIL_COLS, nt),
        in_specs=_dil_in_specs(rev) + [out, out, out],
        out_specs=[out, out, out, pl.BlockSpec((nbr, None, 2 * BLK, 2 * BLK), lambda p, i: (0, p, 0, 0))],
        out_shape=[jax.ShapeDtypeStruct((s, ATT_W), BF16)] * 3
        + [jax.ShapeDtypeStruct((nbr, DIL_COLS, 2 * BLK, 2 * BLK), F32)],
        scratch_shapes=[two(), two(), one(), two(), two(), one(), one()],
        compiler_params=_params("arbitrary", "arbitrary"),
    )(proj, proj, proj, proj, proj, _pair_bias(biasm), lse, att, dcat)


QK_COL0 = (3 * ATT_W) // ATT_W


def _conv_shifted(prev, cur, j, row):
    sh = CONV_K - 1 - j
    if sh == 0:
        return cur
    return jnp.where(row < sh, pltpu.roll(prev, sh, 0), pltpu.roll(cur, sh, 0))


def _conv_z(prev, cur, w_ref, b_ref, row):
    z = b_ref[...] + cur * w_ref[CONV_K - 1:CONV_K, :]
    for j in range(CONV_K - 1):
        z = z + _conv_shifted(prev, cur, j, row) * w_ref[j:j + 1, :]
    return z


def _conv_fwd(proj, conv_w, conv_b, name="conv_fwd", tm=512):
    s = proj.shape[0]
    w = ATT_W

    def body(prev_ref, cur_ref, w_ref, b_ref, o_ref):
        i = pl.program_id(1)
        row = lax.broadcasted_iota(jnp.int32, (tm, w), 0)
        prev = jnp.where(i > 0, prev_ref[...], 0.0)
        z = _conv_z(prev, cur_ref[...], w_ref, b_ref, row)
        o_ref[...] = z * _sigmoid(z)

    return pl.pallas_call(
        body, name=name, grid=(2, s // tm),
        in_specs=[pl.BlockSpec((tm, w), lambda j, i: (jnp.maximum(i - 1, 0), QK_COL0 + j)),
                  pl.BlockSpec((tm, w), lambda j, i: (i, QK_COL0 + j)),
                  pl.BlockSpec((CONV_K, w), lambda j, i: (0, j)),
                  pl.BlockSpec((1, w), lambda j, i: (0, j))],
        out_specs=pl.BlockSpec((tm, w), lambda j, i: (i, j)),
        out_shape=jax.ShapeDtypeStruct((s, 2 * ML_W), F32),
        compiler_params=_params("parallel", "parallel"),
    )(proj, proj, conv_w, conv_b)


def _conv_bwd(proj, dqk, conv_w, conv_b, name="conv_bwd", tm=512):
    s = proj.shape[0]
    w = ATT_W
    nt = s // tm

    def body(xp_ref, xc_ref, xn_ref, dc_ref, dn_ref, w_ref, b_ref, dx_ref, dw_ref, db_ref):
        i = pl.program_id(1)
        row = lax.broadcasted_iota(jnp.int32, (tm, w), 0)
        prev = jnp.where(i > 0, xp_ref[...], 0.0)
        cur = xc_ref[...]

        def dz_of(pv, cv, dy):
            z = _conv_z(pv, cv, w_ref, b_ref, row)
            sig = _sigmoid(z)
            return dy * (sig * (1.0 + z * (1.0 - sig)))

        dzc = dz_of(prev, cur, dc_ref[...])
        dzn = jnp.where(i < nt - 1, dz_of(cur, xn_ref[...], dn_ref[...]), 0.0)
        dx = dzc * w_ref[CONV_K - 1:CONV_K, :]
        for j in range(CONV_K - 1):
            sh = CONV_K - 1 - j
            up = jnp.where(row >= tm - sh, pltpu.roll(dzn, tm - sh, 0), pltpu.roll(dzc, tm - sh, 0))
            dx = dx + up * w_ref[j:j + 1, :]
        dx_ref[...] = dx

        @pl.when(i == 0)
        def _():
            dw_ref[...] = jnp.zeros_like(dw_ref)
            db_ref[...] = jnp.zeros_like(db_ref)

        for j in range(CONV_K):
            dw_ref[j:j + 1, :] += jnp.sum(dzc * _conv_shifted(prev, cur, j, row), axis=0, keepdims=True)
        db_ref[...] += jnp.sum(dzc, axis=0, keepdims=True)

    xs = lambda f: pl.BlockSpec((tm, w), lambda j, i: (f(i), QK_COL0 + j))
    ds = lambda f: pl.BlockSpec((tm, w), lambda j, i: (f(i), j))
    return pl.pallas_call(
        body, name=name, grid=(2, nt),
        in_specs=[xs(lambda i: jnp.maximum(i - 1, 0)), xs(lambda i: i), xs(lambda i: jnp.minimum(i + 1, nt - 1)),
                  ds(lambda i: i), ds(lambda i: jnp.minimum(i + 1, nt - 1)),
                  pl.BlockSpec((CONV_K, w), lambda j, i: (0, j)), pl.BlockSpec((1, w), lambda j, i: (0, j))],
        out_specs=[ds(lambda i: i), pl.BlockSpec((CONV_K, w), lambda j, i: (0, j)),
                   pl.BlockSpec((1, w), lambda j, i: (0, j))],
        out_shape=[jax.ShapeDtypeStruct((s, 2 * ML_W), F32), jax.ShapeDtypeStruct((CONV_K, 2 * ML_W), F32),
                   jax.ShapeDtypeStruct((1, 2 * ML_W), F32)],
        compiler_params=_params("parallel", "arbitrary"),
    )(proj, proj, proj, dqk, dqk, conv_w, conv_b)


def _bf16_mm(dims_fwd):
    @jax.custom_vjp
    def mm(a, b):
        return _dot(a.astype(BF16), b.astype(BF16), dims_fwd)

    def fwd(a, b):
        return mm(a, b), (a, b)

    def bwd(res, g):
        a, b = res
        if dims_fwd is NN:
            return _mm_nt(g, b), _mm_tn(a, g)
        if dims_fwd is NT:
            return _mm_nn(g, b), _mm_tn(g, a)
        return _mm_nt(b, g), _mm_nn(a, g)

    mm.defvjp(fwd, bwd)
    return mm


_mm_nn = _bf16_mm(NN)
_mm_nt = _bf16_mm(NT)
_mm_tn = _bf16_mm(TN)


def _tri(lower):
    r = lax.broadcasted_iota(jnp.int32, (CHUNK, CHUNK), 0)
    c = lax.broadcasted_iota(jnp.int32, (CHUNK, CHUNK), 1)
    return ((r >= c) if lower else (r <= c)).astype(F32)


@jax.custom_vjp
def _cumsum_rows(x):
    return lax.dot_general(_tri(True), x, NN, precision=lax.Precision.HIGHEST, preferred_element_type=F32)


def _cumsum_fwd(x):
    return _cumsum_rows(x), None


def _cumsum_bwd(_, g):
    return (lax.dot_general(_tri(False), g, NN, precision=lax.Precision.HIGHEST, preferred_element_type=F32),)


_cumsum_rows.defvjp(_cumsum_fwd, _cumsum_bwd)


def _abs(x):
    return jnp.where(x >= 0, x, -x)


def _log_sigmoid(x):
    return jnp.minimum(x, 0.0) - jnp.log(1.0 + jnp.exp(-_abs(x)))


def _pick_col(x, lane):
    sel = lax.broadcasted_iota(jnp.int32, x.shape, 1) == lane
    return jnp.sum(jnp.where(sel, x, 0.0), axis=1, keepdims=True)


def _pick_row(x, r):
    sel = lax.broadcasted_iota(jnp.int32, x.shape, 0) == r
    return jnp.sum(jnp.where(sel, x, 0.0), axis=0, keepdims=True)


def _mlstm_chunk(qs, ks, vs, oms, gates, gate_bias, mlg, cs, ns, ms):
    gb = gates + gate_bias
    cum = _cumsum_rows(_log_sigmoid(gb))
    gbt = gb.T
    cumt = cum.T
    causal = lax.broadcasted_iota(jnp.int32, (CHUNK, CHUNK), 0) >= lax.broadcasted_iota(jnp.int32, (CHUNK, CHUNK), 1)
    ys, c_out, n_out, m_out = [], [], [], []
    for h in range(ML_HEADS):
        q, v, om, c, n, m = qs[h], vs[h], oms[h], cs[h], ns[h], ms[h]
        k = ks[h] * (ML_HD ** -0.5)
        ig_col = _pick_col(gb, h)
        ig_row = _pick_row(gbt, h)
        b_col = _pick_col(cum, ML_HEADS + h)
        b_row = _pick_row(cumt, ML_HEADS + h)
        g = _pick_row(b_col, CHUNK - 1)
        a = g - b_col + ig_col
        m_loc = jnp.max(a, axis=0, keepdims=True)
        wa = jnp.exp(a - m_loc)
        c_loc = _mm_tn(wa * v, k)
        n_loc = jnp.sum(wa * k, axis=0, keepdims=True)
        m_new = jnp.maximum(g + m, m_loc)
        sp = jnp.exp(g + m - m_new)
        sl = jnp.exp(m_loc - m_new)
        c_out.append(sp * c + sl * c_loc)
        n_out.append(sp * n + sl * n_loc)
        m_out.append(m_new)
        d_log = jnp.where(causal, b_col - b_row + ig_row, -jnp.inf)
        e_log = b_col + m
        m_t = jnp.maximum(e_log, jnp.max(d_log, axis=1, keepdims=True))
        d_w = jnp.exp(d_log - m_t)
        e_w = jnp.exp(e_log - m_t)
        s_qk = _mm_nt(q, k) * d_w
        num = e_w * _mm_nt(q, c) + _mm_nn(s_qk, v)
        den = e_w * jnp.sum(q * n, axis=1, keepdims=True) + jnp.sum(s_qk, axis=1, keepdims=True)
        hh = num / jnp.maximum(_abs(den), jnp.exp(-m_t))
        hg = _sigmoid(om) * hh
        mu = jnp.mean(hg, axis=1, keepdims=True)
        hc = hg - mu
        var = jnp.mean(hc * hc, axis=1, keepdims=True)
        ys.append(hc * lax.rsqrt(var + LN_EPS) * mlg[h])
    return ys, c_out, n_out, m_out


V_COL = 5
O_COL = 6
ML_SUB = 1


def _mlstm_fwd(qk, proj, gates, gate_bias, mlg, name="mlstm_fwd", gather=()):
    s = qk.shape[0]
    nc = s // CHUNK

    def body(q_ref, k_ref, v_ref, o_ref, g_ref, gb_ref, mlg_ref, y_ref, cp_ref, np_ref, mp_ref, c_s, n_s, m_s):
        ci = pl.program_id(0)

        @pl.when(ci == 0)
        def _():
            c_s[...] = jnp.zeros_like(c_s)
            n_s[...] = jnp.zeros_like(n_s)
            m_s[...] = jnp.zeros_like(m_s)

        for sub in range(ML_SUB):
            rows = slice(CHUNK * sub, CHUNK * (sub + 1))
            hs = lambda ref: [ref[rows, LANES * h:LANES * (h + 1)] for h in range(ML_HEADS)]
            cp_ref[sub] = c_s[...]
            np_ref[sub] = n_s[...]
            mp_ref[sub] = m_s[...]
            ys, c_new, n_new, m_new = _mlstm_chunk(
                hs(q_ref), hs(k_ref), hs(v_ref), hs(o_ref), g_ref[rows, :], gb_ref[...],
                [mlg_ref[:, LANES * h:LANES * (h + 1)] for h in range(ML_HEADS)],
                [c_s[h] for h in range(ML_HEADS)], [n_s[h:h + 1, :] for h in range(ML_HEADS)],
                [m_s[h:h + 1, 0:1] for h in range(ML_HEADS)])
            for h in range(ML_HEADS):
                y_ref[rows, LANES * h:LANES * (h + 1)] = ys[h]
                c_s[h] = c_new[h]
                n_s[h:h + 1, :] = n_new[h]
                m_s[h:h + 1, :] = jnp.broadcast_to(m_new[h], (1, LANES))

    blk = lambda col: pl.BlockSpec((ML_SUB * CHUNK, ML_W), lambda ci: (ci, col))
    vec = lambda w: pl.BlockSpec((1, w), lambda ci: (0, 0))
    return _call(
        body, name=name, grid=(nc // ML_SUB,), args=(qk, qk, proj, proj, gates, gate_bias, mlg), sem=("arbitrary",),
        gather=gather,
        in_specs=[blk(0), blk(1), blk(V_COL), blk(O_COL), pl.BlockSpec((ML_SUB * CHUNK, LANES), lambda ci: (ci, 0)),
                  vec(LANES), vec(ML_W)],
        out_specs=[blk(0), pl.BlockSpec((ML_SUB, ML_HEADS, ML_HD, ML_HD), lambda ci: (ci, 0, 0, 0)),
                   pl.BlockSpec((ML_SUB, 8, LANES), lambda ci: (ci, 0, 0)),
                   pl.BlockSpec((ML_SUB, 8, LANES), lambda ci: (ci, 0, 0))],
        out_shape=[jax.ShapeDtypeStruct((s, ML_W), F32), jax.ShapeDtypeStruct((nc, ML_HEADS, ML_HD, ML_HD), F32),
                   jax.ShapeDtypeStruct((nc, 8, LANES), F32), jax.ShapeDtypeStruct((nc, 8, LANES), F32)],
        scratch_shapes=[pltpu.VMEM((ML_HEADS, ML_HD, ML_HD), F32), pltpu.VMEM((8, LANES), F32),
                        pltpu.VMEM((8, LANES), F32)])


def _mlstm_bwd(qk, proj, gates, gate_bias, mlg, cprev, nprev, mprev, dy, name="mlstm_bwd", exchange=()):
    s = qk.shape[0]
    nc = s // CHUNK

    def body(q_ref, k_ref, v_ref, o_ref, g_ref, gb_ref, mlg_ref, cp_ref, np_ref, mp_ref, dy_ref,
             dqk_ref, dv_ref, do_ref, dg_ref, dgb_ref, dmlg_ref, dc_s, dn_s, dm_s, gb8, mg8):
        ci = pl.program_id(0)

        @pl.when(ci == 0)
        def _():
            dc_s[...] = jnp.zeros_like(dc_s)
            dn_s[...] = jnp.zeros_like(dn_s)
            dm_s[...] = jnp.zeros_like(dm_s)
            gb8[...] = jnp.zeros_like(gb8)
            mg8[...] = jnp.zeros_like(mg8)

        for sub in reversed(range(ML_SUB)):
            rows = slice(CHUNK * sub, CHUNK * (sub + 1))
            hs = lambda ref: [ref[rows, LANES * h:LANES * (h + 1)] for h in range(ML_HEADS)]
            prim = (hs(q_ref), hs(k_ref), hs(v_ref), hs(o_ref), g_ref[rows, :], gb_ref[...],
                    [mlg_ref[:, LANES * h:LANES * (h + 1)] for h in range(ML_HEADS)],
                    [cp_ref[sub, h] for h in range(ML_HEADS)], [np_ref[sub, h:h + 1, :] for h in range(ML_HEADS)],
                    [mp_ref[sub, h:h + 1, 0:1] for h in range(ML_HEADS)])
            _, vjp = jax.vjp(_mlstm_chunk, *prim)
            cot = (hs(dy_ref), [dc_s[h] for h in range(ML_HEADS)], [dn_s[h:h + 1, :] for h in range(ML_HEADS)],
                   [dm_s[h:h + 1, 0:1] for h in range(ML_HEADS)])
            dqs, dks, dvs, dos, dg, dgb, dmlg, dcs, dns, dms = vjp(cot)
            dg_ref[rows, :] = dg
            gb8[0:1, :] += dgb
            for h in range(ML_HEADS):
                sl = slice(LANES * h, LANES * (h + 1))
                dqk_ref[rows, sl] = dqs[h]
                dqk_ref[rows, ML_W + LANES * h:ML_W + LANES * (h + 1)] = dks[h]
                dv_ref[rows, sl] = dvs[h]
                do_ref[rows, sl] = dos[h]
                mg8[0:1, sl] += dmlg[h]
                dc_s[h] = dcs[h]
                dn_s[h:h + 1, :] = dns[h]
                dm_s[h:h + 1, :] = jnp.broadcast_to(dms[h], (1, LANES))

        @pl.when(ci == nb - 1)
        def _():
            dgb_ref[...] = gb8[0:1, :]
            dmlg_ref[...] = mg8[0:1, :]

    nb = nc // ML_SUB
    rev = lambda ci: nb - 1 - ci
    blk = lambda col: pl.BlockSpec((ML_SUB * CHUNK, ML_W), lambda ci: (rev(ci), col))
    vec = lambda w: pl.BlockSpec((1, w), lambda ci: (0, 0))
    st8 = pl.BlockSpec((ML_SUB, 8, LANES), lambda ci: (rev(ci), 0, 0))
    gsp = pl.BlockSpec((ML_SUB * CHUNK, LANES), lambda ci: (rev(ci), 0))
    return _call(
        body, name=name, grid=(nb,), sem=("arbitrary",), exchange=exchange,
        args=(qk, qk, proj, proj, gates, gate_bias, mlg, cprev, nprev, mprev, dy),
        in_specs=[blk(0), blk(1), blk(V_COL), blk(O_COL), gsp, vec(LANES), vec(ML_W),
                  pl.BlockSpec((ML_SUB, ML_HEADS, ML_HD, ML_HD), lambda ci: (rev(ci), 0, 0, 0)), st8, st8, blk(1)],
        out_specs=[pl.BlockSpec((ML_SUB * CHUNK, 2 * ML_W), lambda ci: (rev(ci), 0)), blk(0), blk(0), gsp, vec(LANES),
                   vec(ML_W)],
        out_shape=[jax.ShapeDtypeStruct((s, 2 * ML_W), F32),
                   jax.ShapeDtypeStruct((s, ML_W), F32), jax.ShapeDtypeStruct((s, ML_W), F32),
                   jax.ShapeDtypeStruct((s, LANES), F32), jax.ShapeDtypeStruct((1, LANES), F32),
                   jax.ShapeDtypeStruct((1, ML_W), F32)],
        scratch_shapes=[pltpu.VMEM((ML_HEADS, ML_HD, ML_HD), F32), pltpu.VMEM((8, LANES), F32),
                        pltpu.VMEM((8, LANES), F32), pltpu.VMEM((8, LANES), F32), pltpu.VMEM((8, ML_W), F32)])


def _xattn_tile(qs, ks, vs):
    outs = []
    for q, k, v in zip(qs, ks, vs):
        sc = _mm_nt(q, k) * (XA_HD ** -0.5)
        mx = lax.stop_gradient(jnp.max(sc, axis=1, keepdims=True))
        pe = jnp.exp(sc - mx)
        outs.append(_mm_nn(pe / jnp.sum(pe, axis=1, keepdims=True), v))
    return outs


def _xa_heads(ref):
    return [ref[:, XA_HD * h:XA_HD * (h + 1)] for h in range(XA_HEADS)]


def _xattn_fwd(q, kv, name="xattn_fwd", tm=512):
    s, d = q.shape

    def body(q_ref, k_ref, v_ref, o_ref):
        outs = _xattn_tile(_xa_heads(q_ref), _xa_heads(k_ref), _xa_heads(v_ref))
        for h in range(XA_HEADS):
            o_ref[:, XA_HD * h:XA_HD * (h + 1)] = outs[h]

    row = pl.BlockSpec((tm, d), lambda i: (i, 0))
    return pl.pallas_call(
        body, name=name, grid=(s // tm,),
        in_specs=[row, pl.BlockSpec((MEM_LEN, d), lambda i: (0, 0)), pl.BlockSpec((MEM_LEN, d), lambda i: (0, 1))],
        out_specs=row, out_shape=jax.ShapeDtypeStruct((s, d), F32),
        compiler_params=_params("parallel"),
    )(q, kv, kv)


def _xattn_bwd(q, kv, do, name="xattn_bwd", tm=512):
    s, d = q.shape

    def body(q_ref, k_ref, v_ref, do_ref, dq_ref, dkv_ref):
        i = pl.program_id(0)
        _, vjp = jax.vjp(_xattn_tile, _xa_heads(q_ref), _xa_heads(k_ref), _xa_heads(v_ref))
        dqs, dks, dvs = vjp(_xa_heads(do_ref))

        @pl.when(i == 0)
        def _():
            dkv_ref[...] = jnp.zeros_like(dkv_ref)

        for h in range(XA_HEADS):
            sl = slice(XA_HD * h, XA_HD * (h + 1))
            dq_ref[:, sl] = dqs[h]
            dkv_ref[:, sl] += dks[h]
            dkv_ref[:, d + XA_HD * h:d + XA_HD * (h + 1)] += dvs[h]

    row = pl.BlockSpec((tm, d), lambda i: (i, 0))
    return pl.pallas_call(
        body, name=name, grid=(s // tm,),
        in_specs=[row, pl.BlockSpec((MEM_LEN, d), lambda i: (0, 0)), pl.BlockSpec((MEM_LEN, d), lambda i: (0, 1)), row],
        out_specs=[row, pl.BlockSpec((MEM_LEN, 2 * d), lambda i: (0, 0))],
        out_shape=[jax.ShapeDtypeStruct((s, d), F32), jax.ShapeDtypeStruct((MEM_LEN, 2 * d), F32)],
        compiler_params=_params("arbitrary"),
    )(q, kv, kv, do)


def _loss_head(y, target, name="loss_head", tm=1024):
    s, d = y.shape
    nt = s // tm

    def body(y_ref, t_ref, dy_ref, loss_ref, acc):
        i = pl.program_id(0)
        err = y_ref[...] - t_ref[...]
        dy_ref[...] = err * (1.0 / d)

        @pl.when(i == 0)
        def _():
            acc[...] = jnp.zeros_like(acc)

        acc[...] += _rowsum8(err * err)

        @pl.when(i == nt - 1)
        def _():
            tot = jnp.sum(jnp.sum(acc[...], axis=0, keepdims=True), axis=1, keepdims=True)
            loss_ref[...] = jnp.broadcast_to(tot * (0.5 / d), (1, LANES))

    row = pl.BlockSpec((tm, d), lambda i: (i, 0))
    return pl.pallas_call(
        body, name=name, grid=(nt,),
        in_specs=[row, row], out_specs=[row, pl.BlockSpec((1, LANES), lambda i: (0, 0))],
        out_shape=[jax.ShapeDtypeStruct((s, d), F32), jax.ShapeDtypeStruct((1, LANES), F32)],
        scratch_shapes=[pltpu.VMEM((8, d), F32)],
        compiler_params=_params("arbitrary"),
    )(y, target)


def _adam2d(recv, w, m, v, name, layer=None):
    rows, cols = w.shape[-2:]
    fits = [t for t in range(16, rows + 1, 16) if rows % t == 0 and t * cols <= 128 * 1024]
    tr = max(fits) if fits else rows

    def body(r_ref, w_ref, m_ref, v_ref, g_ref, d_ref, mo_ref, vo_ref):
        g = r_ref[0].astype(F32)
        for j in range(1, N_DEV):
            g = g + r_ref[j].astype(F32)
        mn = ADAM_B1 * m_ref[...] + (1.0 - ADAM_B1) * g
        vn = ADAM_B2 * v_ref[...] + (1.0 - ADAM_B2) * jnp.square(g)
        m_hat = mn / (1.0 - ADAM_B1 ** ADAM_STEP)
        v_hat = vn / (1.0 - ADAM_B2 ** ADAM_STEP)
        g_ref[...] = g
        d_ref[...] = -ADAM_LR * (m_hat / (jnp.sqrt(v_hat) + ADAM_EPS) + ADAM_WD * w_ref[...])
        mo_ref[...] = mn
        vo_ref[...] = vn

    row = pl.BlockSpec((tr, cols), lambda i: (i, 0))
    if layer is None:
        wspec = row
    else:
        wspec = pl.BlockSpec((None, None, tr, cols), lambda i: (0, layer, i, 0))
    return pl.pallas_call(
        body, name=name, grid=(rows // tr,),
        in_specs=[pl.BlockSpec((N_DEV, tr, cols), lambda i: (0, i, 0)), wspec, wspec, wspec],
        out_specs=[row] * 4, out_shape=[jax.ShapeDtypeStruct((rows, cols), F32)] * 4,
        compiler_params=_params("parallel"),
    )(recv, w, m, v)


WEIGHTS = ("rel_bias", "ln_g", "ln_b", "ffn_w_gate", "ffn_w_up", "ffn_w_down", "w_in", "conv_w", "conv_b",
           "ig_bias", "fg_bias", "ml_norm_g", "w_out", "xq_w", "xkv_w", "xo_w")
SMALL = ("rel_bias", "ln_g", "ln_b", "conv_w", "conv_b", "ig_bias", "fg_bias", "ml_norm_g")
SMALL_SHAPES = {
    "rel_bias": (REL_BUCKETS, ATT_HEADS), "ln_g": (1, 4, LANES), "ln_b": (1, 4, LANES), "conv_w": (1, CONV_K, LANES),
    "conv_b": (1, 2 * ML_W), "ig_bias": (1, ML_HEADS), "fg_bias": (1, ML_HEADS), "ml_norm_g": (1, ML_W),
}
SMALL_ROWS = 8


def _pack_small(parts, lead=()):
    out = []
    for p in parts:
        p = jnp.pad(p, [(0, 0)] * len(lead) + [(0, SMALL_ROWS * LANES - p.shape[-1])])
        out.append(p.reshape(lead + (SMALL_ROWS, LANES)))
    return jnp.concatenate(out, axis=len(lead))


def _unpack_small(flat):
    out = {}
    for i, n in enumerate(SMALL):
        cnt = int(np.prod(SMALL_SHAPES[n]))
        out[n] = flat[SMALL_ROWS * i:SMALL_ROWS * (i + 1)].reshape(-1)[:cnt].reshape(SMALL_SHAPES[n])
    return out


def _split8(full, axis):
    shp = full.shape
    t = full.reshape(shp[:axis] + (N_DEV, shp[axis] // N_DEV) + shp[axis + 1:])
    return jnp.moveaxis(t, axis, 0).reshape(N_DEV, -1)


def _rep8(full):
    return jnp.broadcast_to(full.reshape(1, -1), (N_DEV, full.size))


def kernel(x, mem, rel_bias, ln_g, ln_b, ffn_w_gate, ffn_w_up, ffn_w_down, w_in, conv_w, conv_b, ig_bias, fg_bias, ml_norm_g, w_out, xq_w, xkv_w, xo_w, loss_target, m_rel_bias, m_ln_g, m_ln_b, m_ffn_w_gate, m_ffn_w_up, m_ffn_w_down, m_w_in, m_conv_w, m_conv_b, m_ig_bias, m_fg_bias, m_ml_norm_g, m_w_out, m_xq_w, m_xkv_w, m_xo_w, v_rel_bias, v_ln_g, v_ln_b, v_ffn_w_gate, v_ffn_w_up, v_ffn_w_down, v_w_in, v_conv_w, v_conv_b, v_ig_bias, v_fg_bias, v_ml_norm_g, v_w_out, v_xq_w, v_xkv_w, v_xo_w):
    w_tree = dict(rel_bias=rel_bias, ln_g=ln_g, ln_b=ln_b, ffn_w_gate=ffn_w_gate, ffn_w_up=ffn_w_up,
                  ffn_w_down=ffn_w_down, w_in=w_in, conv_w=conv_w, conv_b=conv_b, ig_bias=ig_bias, fg_bias=fg_bias,
                  ml_norm_g=ml_norm_g, w_out=w_out, xq_w=xq_w, xkv_w=xkv_w, xo_w=xo_w)
    m_tree = dict(rel_bias=m_rel_bias, ln_g=m_ln_g, ln_b=m_ln_b, ffn_w_gate=m_ffn_w_gate, ffn_w_up=m_ffn_w_up,
                  ffn_w_down=m_ffn_w_down, w_in=m_w_in, conv_w=m_conv_w, conv_b=m_conv_b, ig_bias=m_ig_bias,
                  fg_bias=m_fg_bias, ml_norm_g=m_ml_norm_g, w_out=m_w_out, xq_w=m_xq_w, xkv_w=m_xkv_w, xo_w=m_xo_w)
    v_tree = dict(rel_bias=v_rel_bias, ln_g=v_ln_g, ln_b=v_ln_b, ffn_w_gate=v_ffn_w_gate, ffn_w_up=v_ffn_w_up,
                  ffn_w_down=v_ffn_w_down, w_in=v_w_in, conv_w=v_conv_w, conv_b=v_conv_b, ig_bias=v_ig_bias,
                  fg_bias=v_fg_bias, ml_norm_g=v_ml_norm_g, w_out=v_w_out, xq_w=v_xq_w, xkv_w=v_xkv_w, xo_w=v_xo_w)
    x0 = x[0]
    pad_ff = FF_PAD - FF_SHARD
    bf = lambda t: t.astype(BF16)

    pad_rows = lambda t: jnp.pad(t, ((0, pad_ff), (0, 0)))
    ffn_shards = [(pad_rows(bf(ffn_w_gate[0, l]).T), pad_rows(bf(ffn_w_up[0, l]).T), pad_rows(bf(ffn_w_down[0, l])))
                  for l in range(2)]
    pairs = lambda t: t.reshape(N_PAIR, FF_PAIR, D_MODEL)
    w_in_shard = jnp.pad(bf(w_in[0]), ((0, 0), (0, ATT_W - W_IN_SHARD)))
    small_shard = jnp.concatenate([ln_g[0], ln_b[0], conv_w[0], jnp.zeros((4, LANES), F32)], axis=0)
    gate_bias = jnp.pad(jnp.concatenate([ig_bias, fg_bias], axis=1), ((0, 0), (0, LANES - 2 * ML_HEADS)))
    buckets = _bucket_tables()

    wg0, wu0, wd0, small_all = _gather_two_level("ffn1_weights_gather", ffn_shards[0] + (small_shard,))
    wg0, wu0, wd0 = pairs(wg0), pairs(wu0), pairs(wd0)
    unshard = lambda t: jnp.moveaxis(t, 0, 1).reshape(4, D_MODEL)
    ln_g_full, ln_b_full, conv_w_full = unshard(small_all[:, 0:4]), unshard(small_all[:, 4:8]), unshard(small_all[:, 8:12])
    lng = lambda i: ln_g_full[i:i + 1]
    lnb = lambda i: ln_b_full[i:i + 1]

    u0, x1, win_all, wout_all, xq_all, xo_all, xkv_all = _ffn_fwd(
        x0, wg0, wu0, wd0, lng(0), lnb(0), "ffn1_fwd",
        gather=(w_in_shard, bf(w_out[0]), bf(xq_w[0]), bf(xo_w[0]), bf(xkv_w[0])))
    w_in_full = jnp.moveaxis(win_all[:, :, :W_IN_SHARD], 0, 1).reshape(D_MODEL, W_IN)
    w_main = w_in_full[:, :W_IN_MAIN]
    w_gate_cols = jnp.pad(w_in_full[:, W_IN_MAIN:], ((0, 0), (0, LANES - 2 * ML_HEADS)))
    w_out_full = wout_all.reshape(D_MODEL, D_MODEL)
    xq_full = xq_all.reshape(D_MODEL, D_MODEL)
    xo_full = xo_all.reshape(D_MODEL, D_MODEL)

    proj, wg1 = _matmul(x1, w_main, "nn", "proj_fwd", tn=W_IN_MAIN // 2, tk=D_MODEL, gather=(ffn_shards[1][0],))
    gates, = _matmul(x1, w_gate_cols, "nn", "gates_fwd", tk=D_MODEL)
    biasm = _bias_fwd(rel_bias, buckets)
    att, lse, wd1 = _dil_fwd(proj, biasm, gather=(ffn_shards[1][2],))
    qk = _conv_fwd(proj, conv_w_full, conv_b)
    y_m, c_prev, n_prev, m_prev, wu1 = _mlstm_fwd(qk, proj, gates, gate_bias, ml_norm_g, gather=(ffn_shards[1][1],))
    cat = jnp.concatenate([att, y_m], axis=1)
    u1, x2 = _matmul_resid_ln(cat, w_out_full, x1, lng(1), lnb(1), "w_out_fwd")
    q_x, = _matmul(x2, xq_full, "nn", "xq_fwd", tn=D_MODEL, tk=D_MODEL)
    kv, = _matmul(mem[0], xkv_all, "nn", "xkv_fwd", tk=D_MODEL)
    o_x = _xattn_fwd(q_x, kv)
    u2, x3 = _matmul_resid_ln(o_x, xo_full, x2, lng(2), lnb(2), "xo_fwd")
    wg1, wu1, wd1 = pairs(wg1), pairs(wu1), pairs(wd1)
    u3, x4 = _ffn_fwd(x3, wg1, wu1, wd1, lng(3), lnb(3), "ffn2_fwd")
    dx4, loss_row = _loss_head(x4, loss_target[0])

    dx3, xb, df, da, db, hh, dg3, db3 = _ffn_bwd_x(dx4, u3, x3, wg1, wu1, wd1, lng(3), "ffn2_bwd_x")
    ffn2_send = (_ffn_bwd_w(xb, da, "ffn2_bwd_wg", down=False)[0], _ffn_bwd_w(xb, db, "ffn2_bwd_wu", down=False)[0],
                 _ffn_bwd_w(df, hh, "ffn2_bwd_wd", down=True)[0])

    du2, dg2, db2 = _ln_bwd(dx3, u2, lng(2), "xattn_ln_bwd")
    do_x, = _matmul(du2, xo_full, "nt", "xo_bwd_x", tn=D_MODEL, tk=D_MODEL)
    g_xo, = _matmul(o_x, du2, "tn", "xo_bwd_w", tm=D_MODEL, tn=D_MODEL, out_dtype=BF16)
    dq_x, dkv = _xattn_bwd(q_x, kv, do_x)
    g_xq, = _matmul(x2, dq_x, "tn", "xq_bwd_w", tm=D_MODEL, tn=D_MODEL, out_dtype=BF16)
    g_xkv, = _matmul(mem[0], dkv, "tn", "xkv_bwd_w", tm=D_MODEL, tn=2 * D_MODEL // N_DEV, tk=MEM_LEN,
                     out_dtype=BF16, blocked_out=True)
    dx2, = _matmul(dq_x, xq_full, "nt", "xq_bwd_x", tn=D_MODEL, tk=D_MODEL, add=du2, add_scale=ALPHA)

    du1, dg1, db1 = _ln_bwd(dx2, u1, lng(1), "mixer_ln_bwd")
    dcat, = _matmul(du1, w_out_full, "nt", "w_out_bwd_x", tn=D_MODEL, tk=D_MODEL)
    g_w_out, = _matmul(cat, du1, "tn", "w_out_bwd_w", tm=D_MODEL, tn=D_MODEL, out_dtype=BF16)
    dqk, dv_m, do_m, dgates, dgate_bias, g_mlg, *ffn2_recv = _mlstm_bwd(
        qk, proj, gates, gate_bias, ml_norm_g, c_prev, n_prev, m_prev, dcat, exchange=tuple(ffn2_send))
    dqk_pre, g_conv_w, g_conv_b = _conv_bwd(proj, dqk, conv_w_full, conv_b)
    dq_a, dk_a, dv_a, dbias = _dil_bwd(proj, biasm, lse, att, dcat)
    g_rel = _bias_bwd(dbias.reshape(biasm.shape), buckets)[:, :ATT_HEADS]
    dproj = jnp.concatenate([dq_a, dk_a, dv_a, bf(dqk_pre), bf(dv_m), bf(do_m)], axis=1)
    g_w_main, = _matmul(x1, dproj, "tn", "proj_bwd_w", tm=D_MODEL, tn=W_IN_MAIN // 2, tk=1024, out_dtype=BF16)
    g_w_gates, = _matmul(x1, dgates, "tn", "gates_bwd_w", tm=D_MODEL, out_dtype=BF16)
    g_w_in = jnp.concatenate([g_w_main, g_w_gates[:, :2 * ML_HEADS]], axis=1)
    dx1, = _matmul(dproj, w_main, "nt", "proj_bwd_x", tn=D_MODEL, tk=W_IN_MAIN // 2, add=du1, add_scale=ALPHA)
    dx1, = _matmul(dgates, w_gate_cols, "nt", "gates_bwd_x", tn=D_MODEL, add=dx1)

    rows8 = lambda t: t.reshape(N_DEV, D_MODEL // N_DEV, D_MODEL)
    mid_send = (rows8(g_xo), rows8(g_xq), g_xkv, rows8(g_w_out),
                jnp.moveaxis(g_w_in.reshape(D_MODEL, N_DEV, W_IN_SHARD), 1, 0))
    dx0, xb, df, da, db, hh, dg0, db0, r_xo, r_xq, r_xkv, r_w_out, r_w_in = _ffn_bwd_x(
        dx1, u0, x0, wg0, wu0, wd0, lng(0), "ffn1_bwd_x", exchange=mid_send)
    small_blocks = {
        "rel_bias": _rep8(g_rel),
        "ln_g": _split8(jnp.concatenate([dg0, dg1, dg2, dg3], axis=0), 1),
        "ln_b": _split8(jnp.concatenate([db0, db1, db2, db3], axis=0), 1),
        "conv_w": _split8(g_conv_w, 1),
        "conv_b": _rep8(g_conv_b),
        "ig_bias": _rep8(dgate_bias[:, :ML_HEADS]),
        "fg_bias": _rep8(dgate_bias[:, ML_HEADS:2 * ML_HEADS]),
        "ml_norm_g": _rep8(g_mlg),
    }
    small_send = _pack_small([small_blocks[n] for n in SMALL], lead=(N_DEV,))
    g_wg, r_small = _ffn_bwd_w(xb, da, "ffn1_bwd_wg", down=False, exchange=(small_send,))
    g_wu, r_wg = _ffn_bwd_w(xb, db, "ffn1_bwd_wu", down=False, exchange=(g_wg,))
    g_wd, r_wu = _ffn_bwd_w(df, hh, "ffn1_bwd_wd", down=True, exchange=(g_wu,))
    r_wd, = _exchange_only("ffn1_grads_exchange", exchange=(g_wd,))
    ffn1_recv = [r_wg, r_wu, r_wd]

    res = {}
    for i, n in enumerate(("ffn_w_gate", "ffn_w_up", "ffn_w_down")):
        per_layer = [_adam2d(r[i], w_tree[n], m_tree[n], v_tree[n], f"adamw_{n}_{l}", layer=l)
                     for l, r in enumerate((ffn1_recv, ffn2_recv))]
        res[n] = [jnp.stack([per_layer[0][j], per_layer[1][j]])[None] for j in range(4)]
    for n, r in (("w_in", r_w_in), ("w_out", r_w_out), ("xq_w", r_xq), ("xkv_w", r_xkv), ("xo_w", r_xo)):
        res[n] = [t[None] for t in _adam2d(r, w_tree[n][0], m_tree[n][0], v_tree[n][0], f"adamw_{n}")]
    pack = lambda tree: _pack_small([tree[n].reshape(-1) for n in SMALL])
    small = [_unpack_small(t) for t in _adam2d(r_small, pack(w_tree), pack(m_tree), pack(v_tree), "adamw_small")]
    for n in SMALL:
        res[n] = [small[j][n] for j in range(4)]

    loss = lax.psum(loss_row[0, 0], ("x", "y", "c"))
    return (loss, dx0[None], *[res[n][0] for n in WEIGHTS], *[res[n][1] for n in WEIGHTS],
            *[res[n][2] for n in WEIGHTS], *[res[n][3] for n in WEIGHTS])
```

```python
import functools
import math

import numpy as np
import jax
import jax.numpy as jnp
from jax import lax
from jax.experimental import pallas as pl
from jax.experimental.pallas import tpu as pltpu

F32 = jnp.float32
BF16 = jnp.bfloat16

N_DEV = 8
D_MODEL = 1024
D_FF = 2816
FF_SHARD = D_FF // N_DEV
FF_PAD = 384
ATT_W = 512
ATT_HEADS = 8
DILATED = ((128, 1), (512, 4), (2048, 16))
BLK = 128
ML_W = 512
ML_HEADS = 4
ML_HD = 128
CHUNK = 128
CONV_K = 4
W_IN = 3592
W_IN_SHARD = W_IN // N_DEV
W_IN_MAIN = 3584
XA_HEADS = 4
XA_HD = 256
MEM_LEN = 256
REL_BUCKETS = 32
REL_MAX_DIST = 2048
ALPHA = 2.0 ** 0.25
LN_EPS = 1e-5
NEG = -1e30
ADAM_LR = 0.001
ADAM_B1 = 0.9
ADAM_B2 = 0.999
ADAM_EPS = 1e-08
ADAM_WD = 0.01
ADAM_STEP = 10
LANES = 128
VMEM_LIMIT = 58 * 1024 * 1024

NN = (((1,), (0,)), ((), ()))
NT = (((1,), (1,)), ((), ()))
TN = (((0,), (0,)), ((), ()))


def _dot(a, b, dims):
    return lax.dot_general(a, b, dims, preferred_element_type=F32)


def _params(*sem):
    return pltpu.CompilerParams(dimension_semantics=sem, vmem_limit_bytes=VMEM_LIMIT)


def _sigmoid(x):
    return 1.0 / (1.0 + jnp.exp(-x))


def _rowsum8(x):
    t, c = x.shape
    return jnp.sum(x.reshape(t // 8, 8, c), axis=0)


def _mesh_pos():
    x, y, c = lax.axis_index("x"), lax.axis_index("y"), lax.axis_index("c")
    return x, y, c, 4 * x + 2 * y + c


def _peer(x, y, c, k):
    px = 1 - x if k & 4 else x
    py = 1 - y if k & 2 else y
    pc = 1 - c if k & 1 else c
    return (px, py, pc), 4 * px + 2 * py + pc


def _call(body, *, name, grid, in_specs, out_specs, out_shape, args, scratch_shapes=(), sem=None,
          gather=(), exchange=()):
    in_specs, out_specs, out_shape, scratch = list(in_specs), list(out_specs), list(out_shape), list(scratch_shapes)
    ng, nc = len(gather), len(gather) + len(exchange)
    if nc == 0:
        return pl.pallas_call(body, name=name, grid=grid, in_specs=in_specs, out_specs=out_specs,
                              out_shape=out_shape, scratch_shapes=scratch, compiler_params=_params(*sem))(*args)
    n_in, n_out, n_scr = len(in_specs), len(out_specs), len(scratch)

    def wrapped(*refs):
        ins, cin = refs[:n_in], refs[n_in:n_in + nc]
        outs, cout = refs[n_in + nc:n_in + nc + n_out], refs[n_in + nc + n_out:n_in + 2 * nc + n_out]
        scr = refs[n_in + 2 * nc + n_out:n_in + 2 * nc + n_out + n_scr]
        send_sems, recv_sems, loc_sems = refs[-3:]
        first, last = None, None
        for ax, extent in enumerate(grid):
            f, l = pl.program_id(ax) == 0, pl.program_id(ax) == extent - 1
            first = f if first is None else first & f
            last = l if last is None else last & l

        def copies():
            x, y, c, me = _mesh_pos()
            out = []
            for a in range(nc):
                mine = cin[a] if a < ng else cin[a].at[me]
                out.append(pltpu.make_async_copy(mine, cout[a].at[me], loc_sems.at[a]))
                for k in range(1, N_DEV):
                    peer, pidx = _peer(x, y, c, k)
                    out.append(pltpu.make_async_remote_copy(
                        src_ref=cin[a] if a < ng else cin[a].at[pidx], dst_ref=cout[a].at[me],
                        send_sem=send_sems.at[a, k - 1], recv_sem=recv_sems.at[a, k - 1],
                        device_id=peer, device_id_type=pl.DeviceIdType.MESH))
            return out

        @pl.when(first)
        def _():
            for cp in copies():
                cp.start()

        body(*ins, *outs, *scr)

        @pl.when(last)
        def _():
            for cp in copies():
                cp.wait()

    hbm = pl.BlockSpec(memory_space=pl.ANY)
    comm_shapes = [jax.ShapeDtypeStruct((N_DEV,) + a.shape, a.dtype) for a in gather]
    comm_shapes += [jax.ShapeDtypeStruct(a.shape, a.dtype) for a in exchange]
    return pl.pallas_call(
        wrapped, name=name, grid=grid, in_specs=in_specs + [hbm] * nc, out_specs=out_specs + [hbm] * nc,
        out_shape=out_shape + comm_shapes,
        scratch_shapes=scratch + [pltpu.SemaphoreType.DMA((nc, N_DEV - 1)), pltpu.SemaphoreType.DMA((nc, N_DEV - 1)),
                                  pltpu.SemaphoreType.DMA((nc,))],
        compiler_params=_params(*(("arbitrary",) * len(grid))),
    )(*args, *gather, *exchange)


def _gather_two_level(name, arrays):
    na = len(arrays)

    def body(*refs):
        srcs, outs = refs[:na], refs[na:2 * na]
        send_sems, recv_sems, loc_sems = refs[2 * na:]
        x, y, c, me = _mesh_pos()
        here, sib = (x, y, c), (x, y, 1 - c)
        chips = [(1 - x, y), (x, 1 - y), (1 - x, 1 - y)]
        pos = lambda px, py, pc: 4 * px + 2 * py + pc

        def copy(a, k, block, to, src=None):
            return pltpu.make_async_remote_copy(
                src_ref=outs[a].at[block] if src is None else src, dst_ref=outs[a].at[block],
                send_sem=send_sems.at[a, k], recv_sem=recv_sems.at[a, k], device_id=to,
                device_id_type=pl.DeviceIdType.MESH)

        locs = [pltpu.make_async_copy(srcs[a], outs[a].at[me], loc_sems.at[a]) for a in range(na)]
        for cp in locs:
            cp.start()
        first = []
        for a in range(na):
            first.append(copy(a, 0, me, sib, src=srcs[a]))
            first += [copy(a, 1 + j, me, (*chip, c), src=srcs[a]) for j, chip in enumerate(chips)]
        for cp in first:
            cp.start()
        passed = []
        for a in range(na):
            for j, chip in enumerate(chips):
                copy(a, 1 + j, pos(*chip, c), here).wait_recv()
                passed.append(copy(a, 4 + j, pos(*chip, c), sib))
                passed[-1].start()
        for a in range(na):
            copy(a, 0, pos(x, y, 1 - c), here).wait_recv()
            for j, chip in enumerate(chips):
                copy(a, 4 + j, pos(*chip, 1 - c), here).wait_recv()
        for cp in first + passed:
            cp.wait_send()
        for cp in locs:
            cp.wait()

    hbm = pl.BlockSpec(memory_space=pl.ANY)
    return pl.pallas_call(
        body, name=name, in_specs=[hbm] * na, out_specs=[hbm] * na,
        out_shape=[jax.ShapeDtypeStruct((N_DEV,) + a.shape, a.dtype) for a in arrays],
        scratch_shapes=[pltpu.SemaphoreType.DMA((na, N_DEV - 1)), pltpu.SemaphoreType.DMA((na, N_DEV - 1)),
                        pltpu.SemaphoreType.DMA((na,))],
    )(*arrays)


def _exchange_only(name, gather=(), exchange=()):
    return _call(lambda: None, name=name, grid=(1,), in_specs=[], out_specs=[], out_shape=[], args=(),
                 gather=gather, exchange=exchange)


def _matmul(a, b, mode, name, *, out_dtype=F32, tm=1024, tn=512, tk=512, add=None, add_scale=1.0,
            blocked_out=False, gather=(), exchange=()):
    blocked_b = b.ndim == 3
    if blocked_b:
        (m, k), (nb, _, tn) = a.shape, b.shape
        n = nb * tn
    elif mode == "nn":
        (m, k), (_, n) = a.shape, b.shape
    elif mode == "nt":
        (m, k), (n, _) = a.shape, b.shape
    else:
        (k, m), (_, n) = a.shape, b.shape
    tm, tn, tk = min(tm, m), min(tn, n), min(tk, k)
    nk = k // tk
    dims = {"nn": NN, "nt": NT, "tn": TN}[mode]
    if mode == "tn":
        a_spec = pl.BlockSpec((tk, tm), lambda i, j, kk: (kk, i))
    else:
        a_spec = pl.BlockSpec((tm, tk), lambda i, j, kk: (i, kk))
    if blocked_b:
        b_spec = pl.BlockSpec((None, tk, tn), lambda i, j, kk: (j, kk, 0))
    elif mode == "nt":
        b_spec = pl.BlockSpec((tn, tk), lambda i, j, kk: (j, kk))
    else:
        b_spec = pl.BlockSpec((tk, tn), lambda i, j, kk: (kk, j))
    if blocked_out:
        o_spec = pl.BlockSpec((None, tm, tn), lambda i, j, kk: (j, i, 0))
        o_shape = jax.ShapeDtypeStruct((n // tn, m, tn), out_dtype)
    else:
        o_spec = pl.BlockSpec((tm, tn), lambda i, j, kk: (i, j))
        o_shape = jax.ShapeDtypeStruct((m, n), out_dtype)
    has_add = add is not None
    cache_a = nk == 1 and mode != "tn" and n // tn > 1 and a.dtype != BF16

    def body(*refs):
        if has_add:
            a_ref, b_ref, add_ref, o_ref, s_ref = refs
        else:
            a_ref, b_ref, o_ref, s_ref = refs
        kk = pl.program_id(2)
        if cache_a:
            @pl.when(pl.program_id(1) == 0)
            def _():
                s_ref[...] = a_ref[...].astype(BF16)

            lhs = s_ref[...]
        else:
            lhs = a_ref[...].astype(BF16)
        part = _dot(lhs, b_ref[...].astype(BF16), dims)

        def finish(r):
            if has_add:
                r = r + add_scale * add_ref[...]
            o_ref[...] = r.astype(out_dtype)

        if nk == 1:
            finish(part)
            return

        @pl.when(kk == 0)
        def _():
            s_ref[...] = part

        @pl.when(kk > 0)
        def _():
            s_ref[...] += part

        @pl.when(kk == nk - 1)
        def _():
            finish(s_ref[...])

    if nk > 1:
        scratch = [pltpu.VMEM((tm, tn), F32)]
    else:
        scratch = [pltpu.VMEM((tm, tk), BF16) if cache_a else pltpu.VMEM((8, LANES), F32)]
    return _call(
        body, name=name, grid=(m // tm, n // tn, nk),
        in_specs=[a_spec, b_spec] + ([pl.BlockSpec((tm, tn), lambda i, j, kk: (i, j))] if has_add else []),
        out_specs=[o_spec], out_shape=[o_shape], args=(a, b) + ((add,) if has_add else ()),
        scratch_shapes=scratch, sem=("parallel", "arbitrary", "arbitrary"),
        gather=gather, exchange=exchange)


def _ln_fwd_math(u, g, b):
    mu = jnp.mean(u, axis=-1, keepdims=True)
    uc = u - mu
    var = jnp.mean(uc * uc, axis=-1, keepdims=True)
    return uc * lax.rsqrt(var + LN_EPS) * g + b


def _ln_bwd_math(dy, u, g):
    mu = jnp.mean(u, axis=-1, keepdims=True)
    uc = u - mu
    var = jnp.mean(uc * uc, axis=-1, keepdims=True)
    rstd = lax.rsqrt(var + LN_EPS)
    xhat = uc * rstd
    dxh = dy * g
    m1 = jnp.mean(dxh, axis=-1, keepdims=True)
    m2 = jnp.mean(dxh * xhat, axis=-1, keepdims=True)
    return rstd * (dxh - m1 - xhat * m2), xhat


def _matmul_resid_ln(pieces, w, x, g, b, name, tm=1024):
    s = pieces[0].shape[0]
    k, d = w.shape
    widths = [p.shape[1] for p in pieces]

    def body(*refs):
        a_refs = refs[:len(pieces)]
        w_ref, x_ref, g_ref, b_ref, u_ref, y_ref = refs[len(pieces):]
        u = ALPHA * x_ref[...]
        lo = 0
        for a_ref, width in zip(a_refs, widths):
            u = u + _dot(a_ref[...].astype(BF16), w_ref[lo:lo + width, :], NN)
            lo += width
        u_ref[...] = u
        y_ref[...] = _ln_fwd_math(u, g_ref[...], b_ref[...])

    row = pl.BlockSpec((tm, d), lambda i: (i, 0))
    vec = pl.BlockSpec((1, d), lambda i: (0, 0))
    return pl.pallas_call(
        body, name=name, grid=(s // tm,),
        in_specs=[pl.BlockSpec((tm, width), lambda i: (i, 0)) for width in widths]
        + [pl.BlockSpec((k, d), lambda i: (0, 0)), row, vec, vec],
        out_specs=[row, row], out_shape=[jax.ShapeDtypeStruct((s, d), F32)] * 2,
        compiler_params=_params("parallel"),
    )(*pieces, w, x, g, b)


def _ln_bwd(dy, u, g, name, tm=1024):
    s, d = dy.shape
    nt = s // tm

    def body(dy_ref, u_ref, g_ref, du_ref, dg_ref, db_ref, g8, b8):
        i = pl.program_id(0)
        dy_ = dy_ref[...]
        du, xhat = _ln_bwd_math(dy_, u_ref[...], g_ref[...])
        du_ref[...] = du

        @pl.when(i == 0)
        def _():
            g8[...] = jnp.zeros_like(g8)
            b8[...] = jnp.zeros_like(b8)

        g8[...] += _rowsum8(dy_ * xhat)
        b8[...] += _rowsum8(dy_)

        @pl.when(i == nt - 1)
        def _():
            dg_ref[...] = jnp.sum(g8[...], axis=0, keepdims=True)
            db_ref[...] = jnp.sum(b8[...], axis=0, keepdims=True)

    row = pl.BlockSpec((tm, d), lambda i: (i, 0))
    vec = pl.BlockSpec((1, d), lambda i: (0, 0))
    return pl.pallas_call(
        body, name=name, grid=(nt,),
        in_specs=[row, row, vec], out_specs=[row, vec, vec],
        out_shape=[jax.ShapeDtypeStruct((s, d), F32), jax.ShapeDtypeStruct((1, d), F32),
                   jax.ShapeDtypeStruct((1, d), F32)],
        scratch_shapes=[pltpu.VMEM((8, d), F32), pltpu.VMEM((8, d), F32)],
        compiler_params=_params("arbitrary"),
    )(dy, u, g)


FF_PAIR = 2 * FF_PAD
N_PAIR = N_DEV // 2


def _ffn_fwd(x, wgt, wut, wd, g, b, name, tm=1024, gather=()):
    s, d = x.shape

    def body(x_ref, wg_ref, wu_ref, wd_ref, g_ref, b_ref, u_ref, y_ref, a_ref, bb_ref, xb, acc):
        k = pl.program_id(1)

        @pl.when(k == 0)
        def _():
            xb[...] = x_ref[...].astype(BF16)

        a = _dot(xb[...], wg_ref[...], NT)
        bb = _dot(xb[...], wu_ref[...], NT)
        a_ref[...] = a.astype(BF16)
        bb_ref[...] = bb.astype(BF16)
        h = (a * _sigmoid(a) * bb).astype(BF16)
        part = _dot(h, wd_ref[...], NN)

        @pl.when(k == 0)
        def _():
            acc[...] = part

        @pl.when(k > 0)
        def _():
            acc[...] += part

        @pl.when(k == N_PAIR - 1)
        def _():
            u = ALPHA * x_ref[...] + 0.5 * acc[...]
            u_ref[...] = u
            y_ref[...] = _ln_fwd_math(u, g_ref[...], b_ref[...])

    row = pl.BlockSpec((tm, d), lambda i, k: (i, 0))
    vec = pl.BlockSpec((1, d), lambda i, k: (0, 0))
    w_in = pl.BlockSpec((None, FF_PAIR, d), lambda i, k: (k, 0, 0))
    w_dn = w_in
    hid = pl.BlockSpec((tm, FF_PAIR), lambda i, k: (i, k))
    return _call(
        body, name=name, grid=(s // tm, N_PAIR),
        in_specs=[row, w_in, w_in, w_dn, vec, vec], out_specs=[row, row, hid, hid],
        out_shape=[jax.ShapeDtypeStruct((s, d), F32)] * 2 + [jax.ShapeDtypeStruct((s, N_DEV * FF_PAD), BF16)] * 2,
        args=(x, wgt, wut, wd, g, b),
        scratch_shapes=[pltpu.VMEM((tm, d), BF16), pltpu.VMEM((tm, d), F32)],
        sem=("parallel", "arbitrary"), gather=gather)


def _ffn_bwd_x(dy, u, x, wgt, wut, wd, g, a_fwd, b_fwd, name, tm=512, exchange=()):
    s, d = x.shape
    nt = s // tm
    ffp = N_DEV * FF_PAD

    def body(dy_ref, u_ref, x_ref, wg_ref, wu_ref, wd_ref, g_ref, a_ref, bb_ref,
             dx_ref, xb, df_ref, da_ref, db_ref, h_ref, dg_ref, dbl_ref,
             dfb, du_s, acc, g8, b8):
        i = pl.program_id(0)
        k = pl.program_id(1)

        @pl.when(k == 0)
        def _():
            dy_ = dy_ref[...]
            du, xhat = _ln_bwd_math(dy_, u_ref[...], g_ref[...])
            du_s[...] = du
            dfb[...] = (0.5 * du).astype(BF16)
            df_ref[...] = dfb[...]
            xb[...] = x_ref[...].astype(BF16)

            @pl.when(i == 0)
            def _():
                g8[...] = jnp.zeros_like(g8)
                b8[...] = jnp.zeros_like(b8)

            g8[...] += _rowsum8(dy_ * xhat)
            b8[...] += _rowsum8(dy_)

        a = a_ref[...].astype(F32)
        bb = bb_ref[...].astype(F32)
        sig = _sigmoid(a)
        sa = a * sig
        h_ref[...] = (sa * bb).astype(BF16)
        dh = _dot(dfb[...], wd_ref[...], NT)
        da = (dh * bb * (sig * (1.0 + a * (1.0 - sig)))).astype(BF16)
        db = (dh * sa).astype(BF16)
        da_ref[...] = da
        db_ref[...] = db
        part = _dot(da, wg_ref[...], NN) + _dot(db, wu_ref[...], NN)

        @pl.when(k == 0)
        def _():
            acc[...] = part

        @pl.when(k > 0)
        def _():
            acc[...] += part

        @pl.when(k == N_PAIR - 1)
        def _():
            dx_ref[...] = ALPHA * du_s[...] + acc[...]

        @pl.when((k == N_PAIR - 1) & (i == nt - 1))
        def _():
            dg_ref[...] = jnp.sum(g8[...], axis=0, keepdims=True)
            dbl_ref[...] = jnp.sum(b8[...], axis=0, keepdims=True)

    row = pl.BlockSpec((tm, d), lambda i, k: (i, 0))
    vec = pl.BlockSpec((1, d), lambda i, k: (0, 0))
    w_in = pl.BlockSpec((None, FF_PAIR, d), lambda i, k: (k, 0, 0))
    hid = pl.BlockSpec((tm, FF_PAIR), lambda i, k: (i, k))
    return _call(
        body, name=name, grid=(nt, N_PAIR),
        in_specs=[row, row, row, w_in, w_in, w_in, vec, hid, hid],
        out_specs=[row, row, row, hid, hid, hid, vec, vec],
        out_shape=[jax.ShapeDtypeStruct((s, d), F32), jax.ShapeDtypeStruct((s, d), BF16),
                   jax.ShapeDtypeStruct((s, d), BF16),
                   jax.ShapeDtypeStruct((s, ffp), BF16), jax.ShapeDtypeStruct((s, ffp), BF16),
                   jax.ShapeDtypeStruct((s, ffp), BF16),
                   jax.ShapeDtypeStruct((1, d), F32), jax.ShapeDtypeStruct((1, d), F32)],
        args=(dy, u, x, wgt, wut, wd, g, a_fwd, b_fwd),
        scratch_shapes=[pltpu.VMEM((tm, d), BF16), pltpu.VMEM((tm, d), F32),
                        pltpu.VMEM((tm, d), F32), pltpu.VMEM((8, d), F32), pltpu.VMEM((8, d), F32)],
        sem=("arbitrary", "arbitrary"), exchange=exchange)


def _ffn_bwd_w(tok, hid, name, *, down, tm=2048, exchange=()):
    s, d = tok.shape
    nt = s // tm

    def body(t_ref, h_ref, dw_ref, acc):
        i = pl.program_id(1)
        part = _dot(h_ref[...], t_ref[...], TN) if down else _dot(t_ref[...], h_ref[...], TN)

        @pl.when(i == 0)
        def _():
            acc[...] = part

        @pl.when(i > 0)
        def _():
            acc[...] += part

        @pl.when(i == nt - 1)
        def _():
            for j in range(2):
                lo = j * FF_PAD
                dw_ref[j] = (acc[lo:lo + FF_SHARD, :] if down else acc[:, lo:lo + FF_SHARD]).astype(BF16)

    blk = (FF_SHARD, d) if down else (d, FF_SHARD)
    return _call(
        body, name=name, grid=(N_PAIR, nt),
        in_specs=[pl.BlockSpec((tm, d), lambda k, i: (i, 0)), pl.BlockSpec((tm, FF_PAIR), lambda k, i: (i, k))],
        out_specs=[pl.BlockSpec((2,) + blk, lambda k, i: (k, 0, 0))],
        out_shape=[jax.ShapeDtypeStruct((N_DEV,) + blk, BF16)], args=(tok, hid),
        scratch_shapes=[pltpu.VMEM((FF_PAIR, d) if down else (d, FF_PAIR), F32)],
        sem=("parallel", "arbitrary"), exchange=exchange)


def _bucket_tables():
    qi = np.arange(BLK)[:, None]
    ki = np.arange(2 * BLK)[None, :]
    off = qi + BLK - ki
    out = []
    for window, dil in DILATED:
        n_keys = window // dil
        dist = dil * np.clip(off, 0, n_keys)
        exact = REL_BUCKETS // 2
        df = np.maximum(dist, 1).astype(np.float32)
        large = exact + (np.log(df / np.float32(exact)) / np.float32(math.log(REL_MAX_DIST / exact))
                         * np.float32(REL_BUCKETS - exact)).astype(np.int32)
        large = np.minimum(large, REL_BUCKETS - 1)
        bucket = np.where(dist < exact, dist, large).astype(np.int32)
        band = (off >= 0) & (off <= n_keys)
        out.append(np.where(band, bucket, -1))
    return np.stack(out).astype(np.int32)


def _bias_fwd(rel_bias, buckets, name="bias_fwd"):
    def body(tbl_ref, bkt_ref, out_ref):
        bkt = bkt_ref[...]
        for h in range(ATT_HEADS):
            acc = jnp.full((BLK, 2 * BLK), NEG, F32)
            for bb in range(REL_BUCKETS):
                acc = jnp.where(bkt == bb, tbl_ref[bb, h], acc)
            out_ref[h] = acc

    nbr = len(DILATED)
    return pl.pallas_call(
        body, name=name, grid=(nbr,),
        in_specs=[pl.BlockSpec(memory_space=pltpu.SMEM),
                  pl.BlockSpec((None, BLK, 2 * BLK), lambda r: (r, 0, 0))],
        out_specs=pl.BlockSpec((None, ATT_HEADS, BLK, 2 * BLK), lambda r: (r, 0, 0, 0)),
        out_shape=jax.ShapeDtypeStruct((nbr, ATT_HEADS, BLK, 2 * BLK), F32),
        compiler_params=_params("parallel"),
    )(rel_bias, buckets)


def _bias_bwd(dbias, buckets, name="bias_bwd"):
    nbr = len(DILATED)

    def body(db_ref, bkt_ref, out_ref):
        r = pl.program_id(0)

        @pl.when(r == 0)
        def _():
            out_ref[...] = jnp.zeros_like(out_ref)

        bkt = bkt_ref[...]
        rowi = lax.broadcasted_iota(jnp.int32, (REL_BUCKETS, LANES), 0)
        coli = lax.broadcasted_iota(jnp.int32, (REL_BUCKETS, LANES), 1)
        acc = jnp.zeros((REL_BUCKETS, LANES), F32)
        for h in range(ATT_HEADS):
            x = db_ref[h]
            for bb in range(REL_BUCKETS):
                part = jnp.sum(jnp.where(bkt == bb, x, 0.0), axis=0, keepdims=True)
                tot = jnp.sum(part, axis=1, keepdims=True)
                acc = acc + jnp.where((rowi == bb) & (coli == h), tot, 0.0)
        out_ref[...] += acc

    return pl.pallas_call(
        body, name=name, grid=(nbr,),
        in_specs=[pl.BlockSpec((None, ATT_HEADS, BLK, 2 * BLK), lambda r: (r, 0, 0, 0)),
                  pl.BlockSpec((None, BLK, 2 * BLK), lambda r: (r, 0, 0))],
        out_specs=pl.BlockSpec((REL_BUCKETS, LANES), lambda r: (0, 0)),
        out_shape=jax.ShapeDtypeStruct((REL_BUCKETS, LANES), F32),
        compiler_params=_params("arbitrary"),
    )(dbias, buckets)


def _stack_heads(pair, lo):
    return jnp.concatenate([jnp.where(lo, pair, 0.0), jnp.where(lo, 0.0, pair)], axis=0)


def _head_cols(pair, lo, reduce):
    fill = -jnp.inf if reduce is jnp.max else 0.0
    return jnp.concatenate([reduce(jnp.where(lo, pair, fill), axis=1, keepdims=True),
                            reduce(jnp.where(lo, fill, pair), axis=1, keepdims=True)], axis=0)


def _unstack_heads(x2, lo):
    return jnp.where(lo, x2[:BLK], x2[BLK:])


def _att_scores(q2, kk, bias2, first_ok):
    sc = _dot(q2, kk, NT) * (64 ** -0.5) + bias2
    return jnp.where(first_ok, sc, NEG)


DIL_TILE = 2048
DIL_COLS = ATT_W // LANES
DIL_UNROLL_FWD = 16
DIL_UNROLL_BWD = 16


def _dil_rows(dil, n, r, base=0):
    start = base + n * (BLK * dil) + r
    return pl.ds(start, BLK, stride=dil) if dil > 1 else pl.ds(start, BLK)


def _dil_in_specs(tile_of):
    cur = lambda col: pl.BlockSpec((DIL_TILE, LANES), lambda p, i: (tile_of(i), col * DIL_COLS + p))
    prev = lambda col: pl.BlockSpec((DIL_TILE, LANES), lambda p, i: (jnp.maximum(tile_of(i) - 1, 0), col * DIL_COLS + p))
    bias = pl.BlockSpec((len(DILATED), None, 2 * BLK, 2 * BLK), lambda p, i: (0, p, 0, 0))
    return [cur(0), prev(1), cur(1), prev(2), cur(2), bias]


def _pair_bias(biasm):
    return biasm.reshape(len(DILATED), DIL_COLS, 2 * BLK, 2 * BLK)


def _dil_fwd(proj, biasm, name="dil_fwd", gather=()):
    s = proj.shape[0]
    nt = s // DIL_TILE
    tt = DIL_TILE

    def body(q_ref, kp_ref, kc_ref, vp_ref, vc_ref, bias_ref, att_ref, lse_ref, k2, v2, ob, lb):
        t = pl.program_id(1)
        k2[0:tt, :] = kp_ref[...]
        k2[tt:2 * tt, :] = kc_ref[...]
        v2[0:tt, :] = vp_ref[...]
        v2[tt:2 * tt, :] = vc_ref[...]
        lo = lax.broadcasted_iota(jnp.int32, (BLK, LANES), 1) < 64
        kidx = lax.broadcasted_iota(jnp.int32, (2 * BLK, 2 * BLK), 1)
        for b, (_, dil) in enumerate(DILATED):
            nblk = tt // (BLK * dil)

            def step(j, carry, b=b, dil=dil, nblk=nblk):
                r, n = j % dil, j // dil
                cur, prev = _dil_rows(dil, n, r, tt), _dil_rows(dil, n - 1, r, tt)
                here = _dil_rows(dil, n, r)
                q_pair = q_ref[here, :]
                kk = jnp.concatenate([k2[prev, :], k2[cur, :]], axis=0).astype(BF16)
                vv = jnp.concatenate([v2[prev, :], v2[cur, :]], axis=0).astype(BF16)
                first_ok = (t > 0) | (n > 0) | (kidx >= BLK)
                sc = _att_scores(_stack_heads(q_pair, lo).astype(BF16), kk, bias_ref[b], first_ok)
                mx = jnp.max(sc, axis=1, keepdims=True)
                pe = jnp.exp(sc - mx)
                l = jnp.sum(pe, axis=1, keepdims=True)
                ob.at[b][here, :] = _unstack_heads(_dot(pe.astype(BF16), vv, NN) / l, lo)
                lb.at[b][here, :] = _unstack_heads(jnp.broadcast_to(mx + jnp.log(l), (2 * BLK, LANES)), lo)
                return carry

            lax.fori_loop(0, tt // BLK, step, 0, unroll=DIL_UNROLL_FWD)
        l0, l1, l2 = lb[0], lb[1], lb[2]
        mx = jnp.maximum(jnp.maximum(l0, l1), l2)
        e0, e1, e2 = jnp.exp(l0 - mx), jnp.exp(l1 - mx), jnp.exp(l2 - mx)
        tot = e0 + e1 + e2
        att_ref[...] = (e0 * ob[0] + e1 * ob[1] + e2 * ob[2]) / tot
        lse_ref[...] = mx + jnp.log(tot)

    out = pl.BlockSpec((tt, LANES), lambda p, i: (i, p))
    return _call(
        body, name=name, grid=(DIL_COLS, nt), in_specs=_dil_in_specs(lambda i: i), out_specs=[out, out],
        out_shape=[jax.ShapeDtypeStruct((s, ATT_W), F32)] * 2, args=(proj, proj, proj, proj, proj, _pair_bias(biasm)),
        scratch_shapes=[pltpu.VMEM((2 * tt, LANES), F32), pltpu.VMEM((2 * tt, LANES), F32),
                        pltpu.VMEM((len(DILATED), tt, LANES), F32), pltpu.VMEM((len(DILATED), tt, LANES), F32)],
        sem=("parallel", "parallel"), gather=gather)


def _dil_bwd(proj, biasm, lse, att, dcat, name="dil_bwd"):
    s = proj.shape[0]
    nt = s // DIL_TILE
    tt = DIL_TILE
    nbr = len(DILATED)

    def body(q_ref, kp_ref, kc_ref, vp_ref, vc_ref, bias_ref, lse_ref, att_ref, datt_ref,
             dq_ref, dk_ref, dv_ref, dbias_ref, k2, v2, dqa, dka, dva, kcar, vcar):
        i = pl.program_id(1)
        t = nt - 1 - i
        k2[0:tt, :] = kp_ref[...]
        k2[tt:2 * tt, :] = kc_ref[...]
        v2[0:tt, :] = vp_ref[...]
        v2[tt:2 * tt, :] = vc_ref[...]

        @pl.when(i == 0)
        def _():
            kcar[...] = jnp.zeros_like(kcar)
            vcar[...] = jnp.zeros_like(vcar)
            dbias_ref[...] = jnp.zeros_like(dbias_ref)

        dqa[...] = jnp.zeros_like(dqa)
        dka[0:tt, :] = jnp.zeros((tt, LANES), F32)
        dva[0:tt, :] = jnp.zeros((tt, LANES), F32)
        dka[tt:2 * tt, :] = kcar[...]
        dva[tt:2 * tt, :] = vcar[...]
        lo = lax.broadcasted_iota(jnp.int32, (BLK, LANES), 1) < 64
        kidx = lax.broadcasted_iota(jnp.int32, (2 * BLK, 2 * BLK), 1)
        for b, (_, dil) in enumerate(DILATED):
            nblk = tt // (BLK * dil)

            def step(j, carry, b=b, dil=dil, nblk=nblk):
                r, n = j % dil, j // dil
                cur, prev = _dil_rows(dil, n, r, tt), _dil_rows(dil, n - 1, r, tt)
                here = _dil_rows(dil, n, r)
                q_pair = q_ref[here, :]
                kk = jnp.concatenate([k2[prev, :], k2[cur, :]], axis=0).astype(BF16)
                vv = jnp.concatenate([v2[prev, :], v2[cur, :]], axis=0).astype(BF16)
                first_ok = (t > 0) | (n > 0) | (kidx >= BLK)
                dat_pair = datt_ref[here, :]
                q2 = _stack_heads(q_pair, lo).astype(BF16)
                dom = _stack_heads(dat_pair, lo).astype(BF16)
                sc = _att_scores(q2, kk, bias_ref[b], first_ok)
                pr = jnp.exp(sc - _head_cols(lse_ref[here, :], lo, jnp.max))
                ds = pr * (_dot(dom, vv, NT) - _head_cols(dat_pair * att_ref[here, :], lo, jnp.sum))
                dbias_ref[b] += ds
                dsb = (ds * (64 ** -0.5)).astype(BF16)
                dk2 = _dot(dsb, q2, TN)
                dv2 = _dot(pr.astype(BF16), dom, TN)
                dqa[here, :] += _unstack_heads(_dot(dsb, kk, NN), lo)
                dka[prev, :] += dk2[:BLK]
                dka[cur, :] += dk2[BLK:]
                dva[prev, :] += dv2[:BLK]
                dva[cur, :] += dv2[BLK:]
                return carry

            lax.fori_loop(0, tt // BLK, step, 0, unroll=DIL_UNROLL_BWD)
        dq_ref[...] = dqa[...].astype(BF16)
        dk_ref[...] = dka[tt:2 * tt, :].astype(BF16)
        dv_ref[...] = dva[tt:2 * tt, :].astype(BF16)
        kcar[...] = dka[0:tt, :]
        vcar[...] = dva[0:tt, :]

    rev = lambda i: nt - 1 - i
    out = pl.BlockSpec((tt, LANES), lambda p, i: (rev(i), p))
    two = lambda: pltpu.VMEM((2 * tt, LANES), F32)
    one = lambda: pltpu.VMEM((tt, LANES), F32)
    return pl.pallas_call(
        body, name=name, grid=(DIL_COLS, nt),
        in_specs=_dil_in_specs(rev) + [out, out, out],
        out_specs=[out, out, out, pl.BlockSpec((nbr, None, 2 * BLK, 2 * BLK), lambda p, i: (0, p, 0, 0))],
        out_shape=[jax.ShapeDtypeStruct((s, ATT_W), BF16)] * 3
        + [jax.ShapeDtypeStruct((nbr, DIL_COLS, 2 * BLK, 2 * BLK), F32)],
        scratch_shapes=[two(), two(), one(), two(), two(), one(), one()],
        compiler_params=_params("arbitrary", "arbitrary"),
    )(proj, proj, proj, proj, proj, _pair_bias(biasm), lse, att, dcat)


QK_COL0 = (3 * ATT_W) // ATT_W


def _conv_shifted(prev, cur, j, row):
    sh = CONV_K - 1 - j
    if sh == 0:
        return cur
    return jnp.where(row < sh, pltpu.roll(prev, sh, 0), pltpu.roll(cur, sh, 0))


def _conv_z(prev, cur, w_ref, b_ref, row):
    z = b_ref[...] + cur * w_ref[CONV_K - 1:CONV_K, :]
    for j in range(CONV_K - 1):
        z = z + _conv_shifted(prev, cur, j, row) * w_ref[j:j + 1, :]
    return z


def _conv_fwd(proj, conv_w, conv_b, name="conv_fwd", tm=512):
    s = proj.shape[0]
    w = ATT_W

    def body(prev_ref, cur_ref, w_ref, b_ref, o_ref):
        i = pl.program_id(1)
        row = lax.broadcasted_iota(jnp.int32, (tm, w), 0)
        prev = jnp.where(i > 0, prev_ref[...], 0.0)
        z = _conv_z(prev, cur_ref[...], w_ref, b_ref, row)
        o_ref[...] = z * _sigmoid(z)

    return pl.pallas_call(
        body, name=name, grid=(2, s // tm),
        in_specs=[pl.BlockSpec((tm, w), lambda j, i: (jnp.maximum(i - 1, 0), QK_COL0 + j)),
                  pl.BlockSpec((tm, w), lambda j, i: (i, QK_COL0 + j)),
                  pl.BlockSpec((CONV_K, w), lambda j, i: (0, j)),
                  pl.BlockSpec((1, w), lambda j, i: (0, j))],
        out_specs=pl.BlockSpec((tm, w), lambda j, i: (i, j)),
        out_shape=jax.ShapeDtypeStruct((s, 2 * ML_W), F32),
        compiler_params=_params("parallel", "parallel"),
    )(proj, proj, conv_w, conv_b)


def _conv_bwd(proj, dqk, conv_w, conv_b, name="conv_bwd", tm=512):
    s = proj.shape[0]
    w = ATT_W
    nt = s // tm

    def body(xp_ref, xc_ref, xn_ref, dc_ref, dn_ref, w_ref, b_ref, dx_ref, dw_ref, db_ref):
        i = pl.program_id(1)
        row = lax.broadcasted_iota(jnp.int32, (tm, w), 0)
        prev = jnp.where(i > 0, xp_ref[...], 0.0)
        cur = xc_ref[...]

        def dz_of(pv, cv, dy):
            z = _conv_z(pv, cv, w_ref, b_ref, row)
            sig = _sigmoid(z)
            return dy * (sig * (1.0 + z * (1.0 - sig)))

        dzc = dz_of(prev, cur, dc_ref[...])
        dzn = jnp.where(i < nt - 1, dz_of(cur, xn_ref[...], dn_ref[...]), 0.0)
        dx = dzc * w_ref[CONV_K - 1:CONV_K, :]
        for j in range(CONV_K - 1):
            sh = CONV_K - 1 - j
            up = jnp.where(row >= tm - sh, pltpu.roll(dzn, tm - sh, 0), pltpu.roll(dzc, tm - sh, 0))
            dx = dx + up * w_ref[j:j + 1, :]
        dx_ref[...] = dx

        @pl.when(i == 0)
        def _():
            dw_ref[...] = jnp.zeros_like(dw_ref)
            db_ref[...] = jnp.zeros_like(db_ref)

        for j in range(CONV_K):
            dw_ref[j:j + 1, :] += jnp.sum(dzc * _conv_shifted(prev, cur, j, row), axis=0, keepdims=True)
        db_ref[...] += jnp.sum(dzc, axis=0, keepdims=True)

    xs = lambda f: pl.BlockSpec((tm, w), lambda j, i: (f(i), QK_COL0 + j))
    ds = lambda f: pl.BlockSpec((tm, w), lambda j, i: (f(i), j))
    return pl.pallas_call(
        body, name=name, grid=(2, nt),
        in_specs=[xs(lambda i: jnp.maximum(i - 1, 0)), xs(lambda i: i), xs(lambda i: jnp.minimum(i + 1, nt - 1)),
                  ds(lambda i: i), ds(lambda i: jnp.minimum(i + 1, nt - 1)),
                  pl.BlockSpec((CONV_K, w), lambda j, i: (0, j)), pl.BlockSpec((1, w), lambda j, i: (0, j))],
        out_specs=[ds(lambda i: i), pl.BlockSpec((CONV_K, w), lambda j, i: (0, j)),
                   pl.BlockSpec((1, w), lambda j, i: (0, j))],
        out_shape=[jax.ShapeDtypeStruct((s, 2 * ML_W), F32), jax.ShapeDtypeStruct((CONV_K, 2 * ML_W), F32),
                   jax.ShapeDtypeStruct((1, 2 * ML_W), F32)],
        compiler_params=_params("parallel", "arbitrary"),
    )(proj, proj, proj, dqk, dqk, conv_w, conv_b)


def _bf16_mm(dims_fwd):
    @jax.custom_vjp
    def mm(a, b):
        return _dot(a.astype(BF16), b.astype(BF16), dims_fwd)

    def fwd(a, b):
        return mm(a, b), (a, b)

    def bwd(res, g):
        a, b = res
        if dims_fwd is NN:
            return _mm_nt(g, b), _mm_tn(a, g)
        if dims_fwd is NT:
            return _mm_nn(g, b), _mm_tn(g, a)
        return _mm_nt(b, g), _mm_nn(a, g)

    mm.defvjp(fwd, bwd)
    return mm


_mm_nn = _bf16_mm(NN)
_mm_nt = _bf16_mm(NT)
_mm_tn = _bf16_mm(TN)


def _tri(lower):
    r = lax.broadcasted_iota(jnp.int32, (CHUNK, CHUNK), 0)
    c = lax.broadcasted_iota(jnp.int32, (CHUNK, CHUNK), 1)
    return ((r >= c) if lower else (r <= c)).astype(F32)


@jax.custom_vjp
def _cumsum_rows(x):
    return lax.dot_general(_tri(True), x, NN, precision=lax.Precision.HIGHEST, preferred_element_type=F32)


def _cumsum_fwd(x):
    return _cumsum_rows(x), None


def _cumsum_bwd(_, g):
    return (lax.dot_general(_tri(False), g, NN, precision=lax.Precision.HIGHEST, preferred_element_type=F32),)


_cumsum_rows.defvjp(_cumsum_fwd, _cumsum_bwd)


def _abs(x):
    return jnp.where(x >= 0, x, -x)


def _log_sigmoid(x):
    return jnp.minimum(x, 0.0) - jnp.log(1.0 + jnp.exp(-_abs(x)))


def _pick_col(x, lane):
    sel = lax.broadcasted_iota(jnp.int32, x.shape, 1) == lane
    return jnp.sum(jnp.where(sel, x, 0.0), axis=1, keepdims=True)


def _pick_row(x, r):
    sel = lax.broadcasted_iota(jnp.int32, x.shape, 0) == r
    return jnp.sum(jnp.where(sel, x, 0.0), axis=0, keepdims=True)


def _mlstm_chunk(qs, ks, vs, oms, gates, gate_bias, mlg, cs, ns, ms):
    gb = gates + gate_bias
    cum = _cumsum_rows(_log_sigmoid(gb))
    gbt = gb.T
    cumt = cum.T
    causal = lax.broadcasted_iota(jnp.int32, (CHUNK, CHUNK), 0) >= lax.broadcasted_iota(jnp.int32, (CHUNK, CHUNK), 1)
    ys, c_out, n_out, m_out = [], [], [], []
    for h in range(ML_HEADS):
        q, v, om, c, n, m = qs[h], vs[h], oms[h], cs[h], ns[h], ms[h]
        k = ks[h] * (ML_HD ** -0.5)
        ig_col = _pick_col(gb, h)
        ig_row = _pick_row(gbt, h)
        b_col = _pick_col(cum, ML_HEADS + h)
        b_row = _pick_row(cumt, ML_HEADS + h)
        g = _pick_row(b_col, CHUNK - 1)
        a = g - b_col + ig_col
        m_loc = jnp.max(a, axis=0, keepdims=True)
        wa = jnp.exp(a - m_loc)
        c_loc = _mm_tn(wa * v, k)
        n_loc = jnp.sum(wa * k, axis=0, keepdims=True)
        m_new = jnp.maximum(g + m, m_loc)
        sp = jnp.exp(g + m - m_new)
        sl = jnp.exp(m_loc - m_new)
        c_out.append(sp * c + sl * c_loc)
        n_out.append(sp * n + sl * n_loc)
        m_out.append(m_new)
        d_log = jnp.where(causal, b_col - b_row + ig_row, -jnp.inf)
        e_log = b_col + m
        m_t = jnp.maximum(e_log, jnp.max(d_log, axis=1, keepdims=True))
        d_w = jnp.exp(d_log - m_t)
        e_w = jnp.exp(e_log - m_t)
        s_qk = _mm_nt(q, k) * d_w
        num = e_w * _mm_nt(q, c) + _mm_nn(s_qk, v)
        den = e_w * jnp.sum(q * n, axis=1, keepdims=True) + jnp.sum(s_qk, axis=1, keepdims=True)
        hh = num / jnp.maximum(_abs(den), jnp.exp(-m_t))
        hg = _sigmoid(om) * hh
        mu = jnp.mean(hg, axis=1, keepdims=True)
        hc = hg - mu
        var = jnp.mean(hc * hc, axis=1, keepdims=True)
        ys.append(hc * lax.rsqrt(var + LN_EPS) * mlg[h])
    return ys, c_out, n_out, m_out


V_COL = 5
O_COL = 6
ML_SUB = 1


def _mlstm_fwd(qk, proj, gates, gate_bias, mlg, name="mlstm_fwd", gather=()):
    s = qk.shape[0]
    nc = s // CHUNK

    def body(q_ref, k_ref, v_ref, o_ref, g_ref, gb_ref, mlg_ref, y_ref, cp_ref, np_ref, mp_ref, c_s, n_s, m_s):
        ci = pl.program_id(0)

        @pl.when(ci == 0)
        def _():
            c_s[...] = jnp.zeros_like(c_s)
            n_s[...] = jnp.zeros_like(n_s)
            m_s[...] = jnp.zeros_like(m_s)

        for sub in range(ML_SUB):
            rows = slice(CHUNK * sub, CHUNK * (sub + 1))
            hs = lambda ref: [ref[rows, LANES * h:LANES * (h + 1)] for h in range(ML_HEADS)]
            cp_ref[sub] = c_s[...]
            np_ref[sub] = n_s[...]
            mp_ref[sub] = m_s[...]
            ys, c_new, n_new, m_new = _mlstm_chunk(
                hs(q_ref), hs(k_ref), hs(v_ref), hs(o_ref), g_ref[rows, :], gb_ref[...],
                [mlg_ref[:, LANES * h:LANES * (h + 1)] for h in range(ML_HEADS)],
                [c_s[h] for h in range(ML_HEADS)], [n_s[h:h + 1, :] for h in range(ML_HEADS)],
                [m_s[h:h + 1, 0:1] for h in range(ML_HEADS)])
            for h in range(ML_HEADS):
                y_ref[rows, LANES * h:LANES * (h + 1)] = ys[h]
                c_s[h] = c_new[h]
                n_s[h:h + 1, :] = n_new[h]
                m_s[h:h + 1, :] = jnp.broadcast_to(m_new[h], (1, LANES))

    blk = lambda col: pl.BlockSpec((ML_SUB * CHUNK, ML_W), lambda ci: (ci, col))
    vec = lambda w: pl.BlockSpec((1, w), lambda ci: (0, 0))
    return _call(
        body, name=name, grid=(nc // ML_SUB,), args=(qk, qk, proj, proj, gates, gate_bias, mlg), sem=("arbitrary",),
        gather=gather,
        in_specs=[blk(0), blk(1), blk(V_COL), blk(O_COL), pl.BlockSpec((ML_SUB * CHUNK, LANES), lambda ci: (ci, 0)),
                  vec(LANES), vec(ML_W)],
        out_specs=[blk(0), pl.BlockSpec((ML_SUB, ML_HEADS, ML_HD, ML_HD), lambda ci: (ci, 0, 0, 0)),
                   pl.BlockSpec((ML_SUB, 8, LANES), lambda ci: (ci, 0, 0)),
                   pl.BlockSpec((ML_SUB, 8, LANES), lambda ci: (ci, 0, 0))],
        out_shape=[jax.ShapeDtypeStruct((s, ML_W), F32), jax.ShapeDtypeStruct((nc, ML_HEADS, ML_HD, ML_HD), F32),
                   jax.ShapeDtypeStruct((nc, 8, LANES), F32), jax.ShapeDtypeStruct((nc, 8, LANES), F32)],
        scratch_shapes=[pltpu.VMEM((ML_HEADS, ML_HD, ML_HD), F32), pltpu.VMEM((8, LANES), F32),
                        pltpu.VMEM((8, LANES), F32)])


def _mlstm_bwd(qk, proj, gates, gate_bias, mlg, cprev, nprev, mprev, dy, name="mlstm_bwd", exchange=()):
    s = qk.shape[0]
    nc = s // CHUNK

    def body(q_ref, k_ref, v_ref, o_ref, g_ref, gb_ref, mlg_ref, cp_ref, np_ref, mp_ref, dy_ref,
             dqk_ref, dv_ref, do_ref, dg_ref, dgb_ref, dmlg_ref, dc_s, dn_s, dm_s, gb8, mg8):
        ci = pl.program_id(0)

        @pl.when(ci == 0)
        def _():
            dc_s[...] = jnp.zeros_like(dc_s)
            dn_s[...] = jnp.zeros_like(dn_s)
            dm_s[...] = jnp.zeros_like(dm_s)
            gb8[...] = jnp.zeros_like(gb8)
            mg8[...] = jnp.zeros_like(mg8)

        for sub in reversed(range(ML_SUB)):
            rows = slice(CHUNK * sub, CHUNK * (sub + 1))
            hs = lambda ref: [ref[rows, LANES * h:LANES * (h + 1)] for h in range(ML_HEADS)]
            prim = (hs(q_ref), hs(k_ref), hs(v_ref), hs(o_ref), g_ref[rows, :], gb_ref[...],
                    [mlg_ref[:, LANES * h:LANES * (h + 1)] for h in range(ML_HEADS)],
                    [cp_ref[sub, h] for h in range(ML_HEADS)], [np_ref[sub, h:h + 1, :] for h in range(ML_HEADS)],
                    [mp_ref[sub, h:h + 1, 0:1] for h in range(ML_HEADS)])
            _, vjp = jax.vjp(_mlstm_chunk, *prim)
            cot = (hs(dy_ref), [dc_s[h] for h in range(ML_HEADS)], [dn_s[h:h + 1, :] for h in range(ML_HEADS)],
                   [dm_s[h:h + 1, 0:1] for h in range(ML_HEADS)])
            dqs, dks, dvs, dos, dg, dgb, dmlg, dcs, dns, dms = vjp(cot)
            dg_ref[rows, :] = dg
            gb8[0:1, :] += dgb
            for h in range(ML_HEADS):
                sl = slice(LANES * h, LANES * (h + 1))
                dqk_ref[rows, sl] = dqs[h]
                dqk_ref[rows, ML_W + LANES * h:ML_W + LANES * (h + 1)] = dks[h]
                dv_ref[rows, sl] = dvs[h]
                do_ref[rows, sl] = dos[h]
                mg8[0:1, sl] += dmlg[h]
                dc_s[h] = dcs[h]
                dn_s[h:h + 1, :] = dns[h]
                dm_s[h:h + 1, :] = jnp.broadcast_to(dms[h], (1, LANES))

        @pl.when(ci == nb - 1)
        def _():
            dgb_ref[...] = gb8[0:1, :]
            dmlg_ref[...] = mg8[0:1, :]

    nb = nc // ML_SUB
    rev = lambda ci: nb - 1 - ci
    blk = lambda col: pl.BlockSpec((ML_SUB * CHUNK, ML_W), lambda ci: (rev(ci), col))
    vec = lambda w: pl.BlockSpec((1, w), lambda ci: (0, 0))
    st8 = pl.BlockSpec((ML_SUB, 8, LANES), lambda ci: (rev(ci), 0, 0))
    gsp = pl.BlockSpec((ML_SUB * CHUNK, LANES), lambda ci: (rev(ci), 0))
    return _call(
        body, name=name, grid=(nb,), sem=("arbitrary",), exchange=exchange,
        args=(qk, qk, proj, proj, gates, gate_bias, mlg, cprev, nprev, mprev, dy),
        in_specs=[blk(0), blk(1), blk(V_COL), blk(O_COL), gsp, vec(LANES), vec(ML_W),
                  pl.BlockSpec((ML_SUB, ML_HEADS, ML_HD, ML_HD), lambda ci: (rev(ci), 0, 0, 0)), st8, st8, blk(1)],
        out_specs=[pl.BlockSpec((ML_SUB * CHUNK, 2 * ML_W), lambda ci: (rev(ci), 0)), blk(0), blk(0), gsp, vec(LANES),
                   vec(ML_W)],
        out_shape=[jax.ShapeDtypeStruct((s, 2 * ML_W), F32),
                   jax.ShapeDtypeStruct((s, ML_W), F32), jax.ShapeDtypeStruct((s, ML_W), F32),
                   jax.ShapeDtypeStruct((s, LANES), F32), jax.ShapeDtypeStruct((1, LANES), F32),
                   jax.ShapeDtypeStruct((1, ML_W), F32)],
        scratch_shapes=[pltpu.VMEM((ML_HEADS, ML_HD, ML_HD), F32), pltpu.VMEM((8, LANES), F32),
                        pltpu.VMEM((8, LANES), F32), pltpu.VMEM((8, LANES), F32), pltpu.VMEM((8, ML_W), F32)])


def _xattn_tile(qs, ks, vs):
    outs = []
    for q, k, v in zip(qs, ks, vs):
        sc = _mm_nt(q, k) * (XA_HD ** -0.5)
        mx = lax.stop_gradient(jnp.max(sc, axis=1, keepdims=True))
        pe = jnp.exp(sc - mx)
        outs.append(_mm_nn(pe / jnp.sum(pe, axis=1, keepdims=True), v))
    return outs


def _xa_heads(ref):
    return [ref[:, XA_HD * h:XA_HD * (h + 1)] for h in range(XA_HEADS)]


def _xattn_fwd(q, kv, name="xattn_fwd", tm=512):
    s, d = q.shape

    def body(q_ref, k_ref, v_ref, o_ref):
        outs = _xattn_tile(_xa_heads(q_ref), _xa_heads(k_ref), _xa_heads(v_ref))
        for h in range(XA_HEADS):
            o_ref[:, XA_HD * h:XA_HD * (h + 1)] = outs[h]

    row = pl.BlockSpec((tm, d), lambda i: (i, 0))
    return pl.pallas_call(
        body, name=name, grid=(s // tm,),
        in_specs=[row, pl.BlockSpec((MEM_LEN, d), lambda i: (0, 0)), pl.BlockSpec((MEM_LEN, d), lambda i: (0, 1))],
        out_specs=row, out_shape=jax.ShapeDtypeStruct((s, d), F32),
        compiler_params=_params("parallel"),
    )(q, kv, kv)


def _xattn_bwd(q, kv, do, name="xattn_bwd", tm=512):
    s, d = q.shape

    def body(q_ref, k_ref, v_ref, do_ref, dq_ref, dkv_ref):
        i = pl.program_id(0)
        _, vjp = jax.vjp(_xattn_tile, _xa_heads(q_ref), _xa_heads(k_ref), _xa_heads(v_ref))
        dqs, dks, dvs = vjp(_xa_heads(do_ref))

        @pl.when(i == 0)
        def _():
            dkv_ref[...] = jnp.zeros_like(dkv_ref)

        for h in range(XA_HEADS):
            sl = slice(XA_HD * h, XA_HD * (h + 1))
            dq_ref[:, sl] = dqs[h]
            dkv_ref[:, sl] += dks[h]
            dkv_ref[:, d + XA_HD * h:d + XA_HD * (h + 1)] += dvs[h]

    row = pl.BlockSpec((tm, d), lambda i: (i, 0))
    return pl.pallas_call(
        body, name=name, grid=(s // tm,),
        in_specs=[row, pl.BlockSpec((MEM_LEN, d), lambda i: (0, 0)), pl.BlockSpec((MEM_LEN, d), lambda i: (0, 1)), row],
        out_specs=[row, pl.BlockSpec((MEM_LEN, 2 * d), lambda i: (0, 0))],
        out_shape=[jax.ShapeDtypeStruct((s, d), F32), jax.ShapeDtypeStruct((MEM_LEN, 2 * d), F32)],
        compiler_params=_params("arbitrary"),
    )(q, kv, kv, do)


def _loss_head(y, target, name="loss_head", tm=1024):
    s, d = y.shape
    nt = s // tm

    def body(y_ref, t_ref, dy_ref, loss_ref, acc):
        i = pl.program_id(0)
        err = y_ref[...] - t_ref[...]
        dy_ref[...] = err * (1.0 / d)

        @pl.when(i == 0)
        def _():
            acc[...] = jnp.zeros_like(acc)

        acc[...] += _rowsum8(err * err)

        @pl.when(i == nt - 1)
        def _():
            tot = jnp.sum(jnp.sum(acc[...], axis=0, keepdims=True), axis=1, keepdims=True)
            loss_ref[...] = jnp.broadcast_to(tot * (0.5 / d), (1, LANES))

    row = pl.BlockSpec((tm, d), lambda i: (i, 0))
    return pl.pallas_call(
        body, name=name, grid=(nt,),
        in_specs=[row, row], out_specs=[row, pl.BlockSpec((1, LANES), lambda i: (0, 0))],
        out_shape=[jax.ShapeDtypeStruct((s, d), F32), jax.ShapeDtypeStruct((1, LANES), F32)],
        scratch_shapes=[pltpu.VMEM((8, d), F32)],
        compiler_params=_params("arbitrary"),
    )(y, target)


def _adam2d(recv, w, m, v, name, layer=None):
    rows, cols = w.shape[-2:]
    fits = [t for t in range(16, rows + 1, 16) if rows % t == 0 and t * cols <= 128 * 1024]
    tr = max(fits) if fits else rows

    def body(r_ref, w_ref, m_ref, v_ref, g_ref, d_ref, mo_ref, vo_ref):
        g = r_ref[0].astype(F32)
        for j in range(1, N_DEV):
            g = g + r_ref[j].astype(F32)
        mn = ADAM_B1 * m_ref[...] + (1.0 - ADAM_B1) * g
        vn = ADAM_B2 * v_ref[...] + (1.0 - ADAM_B2) * jnp.square(g)
        m_hat = mn / (1.0 - ADAM_B1 ** ADAM_STEP)
        v_hat = vn / (1.0 - ADAM_B2 ** ADAM_STEP)
        g_ref[...] = g
        d_ref[...] = -ADAM_LR * (m_hat / (jnp.sqrt(v_hat) + ADAM_EPS) + ADAM_WD * w_ref[...])
        mo_ref[...] = mn
        vo_ref[...] = vn

    row = pl.BlockSpec((tr, cols), lambda i: (i, 0))
    if layer is None:
        wspec = row
    else:
        wspec = pl.BlockSpec((None, None, tr, cols), lambda i: (0, layer, i, 0))
    return pl.pallas_call(
        body, name=name, grid=(rows // tr,),
        in_specs=[pl.BlockSpec((N_DEV, tr, cols), lambda i: (0, i, 0)), wspec, wspec, wspec],
        out_specs=[row] * 4, out_shape=[jax.ShapeDtypeStruct((rows, cols), F32)] * 4,
        compiler_params=_params("parallel"),
    )(recv, w, m, v)


WEIGHTS = ("rel_bias", "ln_g", "ln_b", "ffn_w_gate", "ffn_w_up", "ffn_w_down", "w_in", "conv_w", "conv_b",
           "ig_bias", "fg_bias", "ml_norm_g", "w_out", "xq_w", "xkv_w", "xo_w")
SMALL = ("rel_bias", "ln_g", "ln_b", "conv_w", "conv_b", "ig_bias", "fg_bias", "ml_norm_g")
SMALL_SHAPES = {
    "rel_bias": (REL_BUCKETS, ATT_HEADS), "ln_g": (1, 4, LANES), "ln_b": (1, 4, LANES), "conv_w": (1, CONV_K, LANES),
    "conv_b": (1, 2 * ML_W), "ig_bias": (1, ML_HEADS), "fg_bias": (1, ML_HEADS), "ml_norm_g": (1, ML_W),
}
SMALL_ROWS = 8


def _pack_small(parts, lead=()):
    out = []
    for p in parts:
        p = jnp.pad(p, [(0, 0)] * len(lead) + [(0, SMALL_ROWS * LANES - p.shape[-1])])
        out.append(p.reshape(lead + (SMALL_ROWS, LANES)))
    return jnp.concatenate(out, axis=len(lead))


def _unpack_small(flat):
    out = {}
    for i, n in enumerate(SMALL):
        cnt = int(np.prod(SMALL_SHAPES[n]))
        out[n] = flat[SMALL_ROWS * i:SMALL_ROWS * (i + 1)].reshape(-1)[:cnt].reshape(SMALL_SHAPES[n])
    return out


def _split8(full, axis):
    shp = full.shape
    t = full.reshape(shp[:axis] + (N_DEV, shp[axis] // N_DEV) + shp[axis + 1:])
    return jnp.moveaxis(t, axis, 0).reshape(N_DEV, -1)


def _rep8(full):
    return jnp.broadcast_to(full.reshape(1, -1), (N_DEV, full.size))


def kernel(x, mem, rel_bias, ln_g, ln_b, ffn_w_gate, ffn_w_up, ffn_w_down, w_in, conv_w, conv_b, ig_bias, fg_bias, ml_norm_g, w_out, xq_w, xkv_w, xo_w, loss_target, m_rel_bias, m_ln_g, m_ln_b, m_ffn_w_gate, m_ffn_w_up, m_ffn_w_down, m_w_in, m_conv_w, m_conv_b, m_ig_bias, m_fg_bias, m_ml_norm_g, m_w_out, m_xq_w, m_xkv_w, m_xo_w, v_rel_bias, v_ln_g, v_ln_b, v_ffn_w_gate, v_ffn_w_up, v_ffn_w_down, v_w_in, v_conv_w, v_conv_b, v_ig_bias, v_fg_bias, v_ml_norm_g, v_w_out, v_xq_w, v_xkv_w, v_xo_w):
    w_tree = dict(rel_bias=rel_bias, ln_g=ln_g, ln_b=ln_b, ffn_w_gate=ffn_w_gate, ffn_w_up=ffn_w_up,
                  ffn_w_down=ffn_w_down, w_in=w_in, conv_w=conv_w, conv_b=conv_b, ig_bias=ig_bias, fg_bias=fg_bias,
                  ml_norm_g=ml_norm_g, w_out=w_out, xq_w=xq_w, xkv_w=xkv_w, xo_w=xo_w)
    m_tree = dict(rel_bias=m_rel_bias, ln_g=m_ln_g, ln_b=m_ln_b, ffn_w_gate=m_ffn_w_gate, ffn_w_up=m_ffn_w_up,
                  ffn_w_down=m_ffn_w_down, w_in=m_w_in, conv_w=m_conv_w, conv_b=m_conv_b, ig_bias=m_ig_bias,
                  fg_bias=m_fg_bias, ml_norm_g=m_ml_norm_g, w_out=m_w_out, xq_w=m_xq_w, xkv_w=m_xkv_w, xo_w=m_xo_w)
    v_tree = dict(rel_bias=v_rel_bias, ln_g=v_ln_g, ln_b=v_ln_b, ffn_w_gate=v_ffn_w_gate, ffn_w_up=v_ffn_w_up,
                  ffn_w_down=v_ffn_w_down, w_in=v_w_in, conv_w=v_conv_w, conv_b=v_conv_b, ig_bias=v_ig_bias,
                  fg_bias=v_fg_bias, ml_norm_g=v_ml_norm_g, w_out=v_w_out, xq_w=v_xq_w, xkv_w=v_xkv_w, xo_w=v_xo_w)
    x0 = x[0]
    pad_ff = FF_PAD - FF_SHARD
    bf = lambda t: t.astype(BF16)

    pad_rows = lambda t: jnp.pad(t, ((0, pad_ff), (0, 0)))
    ffn_shards = [(pad_rows(bf(ffn_w_gate[0, l]).T), pad_rows(bf(ffn_w_up[0, l]).T), pad_rows(bf(ffn_w_down[0, l])))
                  for l in range(2)]
    pairs = lambda t: t.reshape(N_PAIR, FF_PAIR, D_MODEL)
    w_in_shard = jnp.pad(bf(w_in[0]), ((0, 0), (0, ATT_W - W_IN_SHARD)))
    small_shard = jnp.concatenate([ln_g[0], ln_b[0], conv_w[0], jnp.zeros((4, LANES), F32)], axis=0)
    gate_bias = jnp.pad(jnp.concatenate([ig_bias, fg_bias], axis=1), ((0, 0), (0, LANES - 2 * ML_HEADS)))
    buckets = _bucket_tables()

    wg0, wu0, wd0, small_all = _gather_two_level("ffn1_weights_gather", ffn_shards[0] + (small_shard,))
    wg0, wu0, wd0 = pairs(wg0), pairs(wu0), pairs(wd0)
    unshard = lambda t: jnp.moveaxis(t, 0, 1).reshape(4, D_MODEL)
    ln_g_full, ln_b_full, conv_w_full = unshard(small_all[:, 0:4]), unshard(small_all[:, 4:8]), unshard(small_all[:, 8:12])
    lng = lambda i: ln_g_full[i:i + 1]
    lnb = lambda i: ln_b_full[i:i + 1]

    u0, x1, a0, b0, win_all, wout_all, xq_all, xo_all, xkv_all = _ffn_fwd(
        x0, wg0, wu0, wd0, lng(0), lnb(0), "ffn1_fwd",
        gather=(w_in_shard, bf(w_out[0]), bf(xq_w[0]), bf(xo_w[0]), bf(xkv_w[0])))
    w_in_full = jnp.moveaxis(win_all[:, :, :W_IN_SHARD], 0, 1).reshape(D_MODEL, W_IN)
    w_main = w_in_full[:, :W_IN_MAIN]
    w_gate_cols = jnp.pad(w_in_full[:, W_IN_MAIN:], ((0, 0), (0, LANES - 2 * ML_HEADS)))
    w_out_full = wout_all.reshape(D_MODEL, D_MODEL)
    xq_full = xq_all.reshape(D_MODEL, D_MODEL)
    xo_full = xo_all.reshape(D_MODEL, D_MODEL)

    proj, wg1 = _matmul(x1, w_main, "nn", "proj_fwd", tn=W_IN_MAIN // 2, tk=D_MODEL, gather=(ffn_shards[1][0],))
    gates, = _matmul(x1, w_gate_cols, "nn", "gates_fwd", tk=D_MODEL)
    biasm = _bias_fwd(rel_bias, buckets)
    att, lse, wd1 = _dil_fwd(proj, biasm, gather=(ffn_shards[1][2],))
    qk = _conv_fwd(proj, conv_w_full, conv_b)
    y_m, c_prev, n_prev, m_prev, wu1 = _mlstm_fwd(qk, proj, gates, gate_bias, ml_norm_g, gather=(ffn_shards[1][1],))
    u1, x2 = _matmul_resid_ln((att, y_m), w_out_full, x1, lng(1), lnb(1), "w_out_fwd")
    q_x, = _matmul(x2, xq_full, "nn", "xq_fwd", tn=D_MODEL, tk=D_MODEL)
    kv, = _matmul(mem[0], xkv_all, "nn", "xkv_fwd", tk=D_MODEL)
    o_x = _xattn_fwd(q_x, kv)
    u2, x3 = _matmul_resid_ln((o_x,), xo_full, x2, lng(2), lnb(2), "xo_fwd")
    wg1, wu1, wd1 = pairs(wg1), pairs(wu1), pairs(wd1)
    u3, x4, a3, b3 = _ffn_fwd(x3, wg1, wu1, wd1, lng(3), lnb(3), "ffn2_fwd")
    dx4, loss_row = _loss_head(x4, loss_target[0])

    dx3, xb, df, da, db, hh, dg3, db3 = _ffn_bwd_x(dx4, u3, x3, wg1, wu1, wd1, lng(3), a3, b3, "ffn2_bwd_x")
    ffn2_send = (_ffn_bwd_w(xb, da, "ffn2_bwd_wg", down=False)[0], _ffn_bwd_w(xb, db, "ffn2_bwd_wu", down=False)[0],
                 _ffn_bwd_w(df, hh, "ffn2_bwd_wd", down=True)[0])

    du2, dg2, db2 = _ln_bwd(dx3, u2, lng(2), "xattn_ln_bwd")
    do_x, = _matmul(du2, xo_full, "nt", "xo_bwd_x", tn=D_MODEL, tk=D_MODEL)
    g_xo, = _matmul(o_x, du2, "tn", "xo_bwd_w", tm=D_MODEL, tn=D_MODEL, out_dtype=BF16)
    dq_x, dkv = _xattn_bwd(q_x, kv, do_x)
    g_xq, = _matmul(x2, dq_x, "tn", "xq_bwd_w", tm=D_MODEL, tn=D_MODEL, out_dtype=BF16)
    g_xkv, = _matmul(mem[0], dkv, "tn", "xkv_bwd_w", tm=D_MODEL, tn=2 * D_MODEL // N_DEV, tk=MEM_LEN,
                     out_dtype=BF16, blocked_out=True)
    dx2, = _matmul(dq_x, xq_full, "nt", "xq_bwd_x", tn=D_MODEL, tk=D_MODEL, add=du2, add_scale=ALPHA)

    du1, dg1, db1 = _ln_bwd(dx2, u1, lng(1), "mixer_ln_bwd")
    dcat, = _matmul(du1, w_out_full, "nt", "w_out_bwd_x", tn=D_MODEL, tk=D_MODEL)
    g_w_out = jnp.concatenate(
        [_matmul(half, du1, "tn", f"w_out_bwd_w_{i}", tn=D_MODEL, out_dtype=BF16)[0] for i, half in enumerate((att, y_m))],
        axis=0)
    dqk, dv_m, do_m, dgates, dgate_bias, g_mlg, *ffn2_recv = _mlstm_bwd(
        qk, proj, gates, gate_bias, ml_norm_g, c_prev, n_prev, m_prev, dcat, exchange=tuple(ffn2_send))
    dqk_pre, g_conv_w, g_conv_b = _conv_bwd(proj, dqk, conv_w_full, conv_b)
    dq_a, dk_a, dv_a, dbias = _dil_bwd(proj, biasm, lse, att, dcat)
    g_rel = _bias_bwd(dbias.reshape(biasm.shape), buckets)[:, :ATT_HEADS]
    dproj = jnp.concatenate([dq_a, dk_a, dv_a, bf(dqk_pre), bf(dv_m), bf(do_m)], axis=1)
    g_w_main, = _matmul(x1, dproj, "tn", "proj_bwd_w", tm=D_MODEL, tn=W_IN_MAIN // 2, tk=1024, out_dtype=BF16)
    g_w_gates, = _matmul(x1, dgates, "tn", "gates_bwd_w", tm=D_MODEL, out_dtype=BF16)
    g_w_in = jnp.concatenate([g_w_main, g_w_gates[:, :2 * ML_HEADS]], axis=1)
    dx1, = _matmul(dproj, w_main, "nt", "proj_bwd_x", tn=D_MODEL, tk=W_IN_MAIN // 2, add=du1, add_scale=ALPHA)
    dx1, = _matmul(dgates, w_gate_cols, "nt", "gates_bwd_x", tn=D_MODEL, add=dx1)

    rows8 = lambda t: t.reshape(N_DEV, D_MODEL // N_DEV, D_MODEL)
    mid_send = (rows8(g_xo), rows8(g_xq), g_xkv, rows8(g_w_out),
                jnp.moveaxis(g_w_in.reshape(D_MODEL, N_DEV, W_IN_SHARD), 1, 0))
    dx0, xb, df, da, db, hh, dg0, db0, r_xo, r_xq, r_xkv, r_w_out, r_w_in = _ffn_bwd_x(
        dx1, u0, x0, wg0, wu0, wd0, lng(0), a0, b0, "ffn1_bwd_x", exchange=mid_send)
    small_blocks = {
        "rel_bias": _rep8(g_rel),
        "ln_g": _split8(jnp.concatenate([dg0, dg1, dg2, dg3], axis=0), 1),
        "ln_b": _split8(jnp.concatenate([db0, db1, db2, db3], axis=0), 1),
        "conv_w": _split8(g_conv_w, 1),
        "conv_b": _rep8(g_conv_b),
        "ig_bias": _rep8(dgate_bias[:, :ML_HEADS]),
        "fg_bias": _rep8(dgate_bias[:, ML_HEADS:2 * ML_HEADS]),
        "ml_norm_g": _rep8(g_mlg),
    }
    small_send = _pack_small([small_blocks[n] for n in SMALL], lead=(N_DEV,))
    g_wg, r_small = _ffn_bwd_w(xb, da, "ffn1_bwd_wg", down=False, exchange=(small_send,))
    g_wu, r_wg = _ffn_bwd_w(xb, db, "ffn1_bwd_wu", down=False, exchange=(g_wg,))
    g_wd, r_wu = _ffn_bwd_w(df, hh, "ffn1_bwd_wd", down=True, exchange=(g_wu,))
    r_wd, = _exchange_only("ffn1_grads_exchange", exchange=(g_wd,))
    ffn1_recv = [r_wg, r_wu, r_wd]

    res = {}
    for i, n in enumerate(("ffn_w_gate", "ffn_w_up", "ffn_w_down")):
        per_layer = [_adam2d(r[i], w_tree[n], m_tree[n], v_tree[n], f"adamw_{n}_{l}", layer=l)
                     for l, r in enumerate((ffn1_recv, ffn2_recv))]
        res[n] = [jnp.stack([per_layer[0][j], per_layer[1][j]])[None] for j in range(4)]
    for n, r in (("w_in", r_w_in), ("w_out", r_w_out), ("xq_w", r_xq), ("xkv_w", r_xkv), ("xo_w", r_xo)):
        res[n] = [t[None] for t in _adam2d(r, w_tree[n][0], m_tree[n][0], v_tree[n][0], f"adamw_{n}")]
    pack = lambda tree: _pack_small([tree[n].reshape(-1) for n in SMALL])
    small = [_unpack_small(t) for t in _adam2d(r_small, pack(w_tree), pack(m_tree), pack(v_tree), "adamw_small")]
    for n in SMALL:
        res[n] = [small[j][n] for j in range(4)]

    loss = lax.psum(loss_row[0, 0], ("x", "y", "c"))
    return (loss, dx0[None], *[res[n][0] for n in WEIGHTS], *[res[n][1] for n in WEIGHTS],
            *[res[n][2] for n in WEIGHTS], *[res[n][3] for n in WEIGHTS])
```

```python
import functools
import math

import numpy as np
import jax
import jax.numpy as jnp
from jax import lax
from jax.experimental import pallas as pl
from jax.experimental.pallas import tpu as pltpu

F32 = jnp.float32
BF16 = jnp.bfloat16

N_DEV = 8
D_MODEL = 1024
D_FF = 2816
FF_SHARD = D_FF // N_DEV
FF_PAD = 384
ATT_W = 512
ATT_HEADS = 8
DILATED = ((128, 1), (512, 4), (2048, 16))
BLK = 128
ML_W = 512
ML_HEADS = 4
ML_HD = 128
CHUNK = 128
CONV_K = 4
W_IN = 3592
W_IN_SHARD = W_IN // N_DEV
W_IN_MAIN = 3584
XA_HEADS = 4
XA_HD = 256
MEM_LEN = 256
REL_BUCKETS = 32
REL_MAX_DIST = 2048
ALPHA = 2.0 ** 0.25
LN_EPS = 1e-5
NEG = -1e30
ADAM_LR = 0.001
ADAM_B1 = 0.9
ADAM_B2 = 0.999
ADAM_EPS = 1e-08
ADAM_WD = 0.01
ADAM_STEP = 10
LANES = 128
VMEM_LIMIT = 58 * 1024 * 1024

NN = (((1,), (0,)), ((), ()))
NT = (((1,), (1,)), ((), ()))
TN = (((0,), (0,)), ((), ()))


def _dot(a, b, dims):
    return lax.dot_general(a, b, dims, preferred_element_type=F32)


def _params(*sem):
    return pltpu.CompilerParams(dimension_semantics=sem, vmem_limit_bytes=VMEM_LIMIT)


def _sigmoid(x):
    return 1.0 / (1.0 + jnp.exp(-x))


def _rowsum8(x):
    t, c = x.shape
    return jnp.sum(x.reshape(t // 8, 8, c), axis=0)


def _mesh_pos():
    x, y, c = lax.axis_index("x"), lax.axis_index("y"), lax.axis_index("c")
    return x, y, c, 4 * x + 2 * y + c


def _peer(x, y, c, k):
    px = 1 - x if k & 4 else x
    py = 1 - y if k & 2 else y
    pc = 1 - c if k & 1 else c
    return (px, py, pc), 4 * px + 2 * py + pc


def _call(body, *, name, grid, in_specs, out_specs, out_shape, args, scratch_shapes=(), sem=None,
          gather=(), exchange=()):
    in_specs, out_specs, out_shape, scratch = list(in_specs), list(out_specs), list(out_shape), list(scratch_shapes)
    ng, nc = len(gather), len(gather) + len(exchange)
    if nc == 0:
        return pl.pallas_call(body, name=name, grid=grid, in_specs=in_specs, out_specs=out_specs,
                              out_shape=out_shape, scratch_shapes=scratch, compiler_params=_params(*sem))(*args)
    n_in, n_out, n_scr = len(in_specs), len(out_specs), len(scratch)

    def wrapped(*refs):
        ins, cin = refs[:n_in], refs[n_in:n_in + nc]
        outs, cout = refs[n_in + nc:n_in + nc + n_out], refs[n_in + nc + n_out:n_in + 2 * nc + n_out]
        scr = refs[n_in + 2 * nc + n_out:n_in + 2 * nc + n_out + n_scr]
        send_sems, recv_sems, loc_sems = refs[-3:]
        first, last = None, None
        for ax, extent in enumerate(grid):
            f, l = pl.program_id(ax) == 0, pl.program_id(ax) == extent - 1
            first = f if first is None else first & f
            last = l if last is None else last & l

        def copies():
            x, y, c, me = _mesh_pos()
            out = []
            for a in range(nc):
                mine = cin[a] if a < ng else cin[a].at[me]
                out.append(pltpu.make_async_copy(mine, cout[a].at[me], loc_sems.at[a]))
                for k in range(1, N_DEV):
                    peer, pidx = _peer(x, y, c, k)
                    out.append(pltpu.make_async_remote_copy(
                        src_ref=cin[a] if a < ng else cin[a].at[pidx], dst_ref=cout[a].at[me],
                        send_sem=send_sems.at[a, k - 1], recv_sem=recv_sems.at[a, k - 1],
                        device_id=peer, device_id_type=pl.DeviceIdType.MESH))
            return out

        @pl.when(first)
        def _():
            for cp in copies():
                cp.start()

        body(*ins, *outs, *scr)

        @pl.when(last)
        def _():
            for cp in copies():
                cp.wait()

    hbm = pl.BlockSpec(memory_space=pl.ANY)
    comm_shapes = [jax.ShapeDtypeStruct((N_DEV,) + a.shape, a.dtype) for a in gather]
    comm_shapes += [jax.ShapeDtypeStruct(a.shape, a.dtype) for a in exchange]
    return pl.pallas_call(
        wrapped, name=name, grid=grid, in_specs=in_specs + [hbm] * nc, out_specs=out_specs + [hbm] * nc,
        out_shape=out_shape + comm_shapes,
        scratch_shapes=scratch + [pltpu.SemaphoreType.DMA((nc, N_DEV - 1)), pltpu.SemaphoreType.DMA((nc, N_DEV - 1)),
                                  pltpu.SemaphoreType.DMA((nc,))],
        compiler_params=_params(*(("arbitrary",) * len(grid))),
    )(*args, *gather, *exchange)


def _gather_two_level(name, arrays):
    na = len(arrays)

    def body(*refs):
        srcs, outs = refs[:na], refs[na:2 * na]
        send_sems, recv_sems, loc_sems = refs[2 * na:]
        x, y, c, me = _mesh_pos()
        here, sib = (x, y, c), (x, y, 1 - c)
        chips = [(1 - x, y), (x, 1 - y), (1 - x, 1 - y)]
        pos = lambda px, py, pc: 4 * px + 2 * py + pc

        def copy(a, k, block, to, src=None):
            return pltpu.make_async_remote_copy(
                src_ref=outs[a].at[block] if src is None else src, dst_ref=outs[a].at[block],
                send_sem=send_sems.at[a, k], recv_sem=recv_sems.at[a, k], device_id=to,
                device_id_type=pl.DeviceIdType.MESH)

        locs = [pltpu.make_async_copy(srcs[a], outs[a].at[me], loc_sems.at[a]) for a in range(na)]
        for cp in locs:
            cp.start()
        first = []
        for a in range(na):
            first.append(copy(a, 0, me, sib, src=srcs[a]))
            first += [copy(a, 1 + j, me, (*chip, c), src=srcs[a]) for j, chip in enumerate(chips)]
        for cp in first:
            cp.start()
        passed = []
        for a in range(na):
            for j, chip in enumerate(chips):
                copy(a, 1 + j, pos(*chip, c), here).wait_recv()
                passed.append(copy(a, 4 + j, pos(*chip, c), sib))
                passed[-1].start()
        for a in range(na):
            copy(a, 0, pos(x, y, 1 - c), here).wait_recv()
            for j, chip in enumerate(chips):
                copy(a, 4 + j, pos(*chip, 1 - c), here).wait_recv()
        for cp in first + passed:
            cp.wait_send()
        for cp in locs:
            cp.wait()

    hbm = pl.BlockSpec(memory_space=pl.ANY)
    return pl.pallas_call(
        body, name=name, in_specs=[hbm] * na, out_specs=[hbm] * na,
        out_shape=[jax.ShapeDtypeStruct((N_DEV,) + a.shape, a.dtype) for a in arrays],
        scratch_shapes=[pltpu.SemaphoreType.DMA((na, N_DEV - 1)), pltpu.SemaphoreType.DMA((na, N_DEV - 1)),
                        pltpu.SemaphoreType.DMA((na,))],
    )(*arrays)


def _exchange_only(name, gather=(), exchange=()):
    return _call(lambda: None, name=name, grid=(1,), in_specs=[], out_specs=[], out_shape=[], args=(),
                 gather=gather, exchange=exchange)


def _matmul(a, b, mode, name, *, out_dtype=F32, tm=1024, tn=512, tk=512, add=None, add_scale=1.0,
            blocked_out=False, gather=(), exchange=()):
    blocked_b = b.ndim == 3
    if blocked_b:
        (m, k), (nb, _, tn) = a.shape, b.shape
        n = nb * tn
    elif mode == "nn":
        (m, k), (_, n) = a.shape, b.shape
    elif mode == "nt":
        (m, k), (n, _) = a.shape, b.shape
    else:
        (k, m), (_, n) = a.shape, b.shape
    tm, tn, tk = min(tm, m), min(tn, n), min(tk, k)
    nk = k // tk
    dims = {"nn": NN, "nt": NT, "tn": TN}[mode]
    if mode == "tn":
        a_spec = pl.BlockSpec((tk, tm), lambda i, j, kk: (kk, i))
    else:
        a_spec = pl.BlockSpec((tm, tk), lambda i, j, kk: (i, kk))
    if blocked_b:
        b_spec = pl.BlockSpec((None, tk, tn), lambda i, j, kk: (j, kk, 0))
    elif mode == "nt":
        b_spec = pl.BlockSpec((tn, tk), lambda i, j, kk: (j, kk))
    else:
        b_spec = pl.BlockSpec((tk, tn), lambda i, j, kk: (kk, j))
    if blocked_out:
        o_spec = pl.BlockSpec((None, tm, tn), lambda i, j, kk: (j, i, 0))
        o_shape = jax.ShapeDtypeStruct((n // tn, m, tn), out_dtype)
    else:
        o_spec = pl.BlockSpec((tm, tn), lambda i, j, kk: (i, j))
        o_shape = jax.ShapeDtypeStruct((m, n), out_dtype)
    has_add = add is not None
    cache_a = nk == 1 and mode != "tn" and n // tn > 1 and a.dtype != BF16

    def body(*refs):
        if has_add:
            a_ref, b_ref, add_ref, o_ref, s_ref = refs
        else:
            a_ref, b_ref, o_ref, s_ref = refs
        kk = pl.program_id(2)
        if cache_a:
            @pl.when(pl.program_id(1) == 0)
            def _():
                s_ref[...] = a_ref[...].astype(BF16)

            lhs = s_ref[...]
        else:
            lhs = a_ref[...].astype(BF16)
        part = _dot(lhs, b_ref[...].astype(BF16), dims)

        def finish(r):
            if has_add:
                r = r + add_scale * add_ref[...]
            o_ref[...] = r.astype(out_dtype)

        if nk == 1:
            finish(part)
            return

        @pl.when(kk == 0)
        def _():
            s_ref[...] = part

        @pl.when(kk > 0)
        def _():
            s_ref[...] += part

        @pl.when(kk == nk - 1)
        def _():
            finish(s_ref[...])

    if nk > 1:
        scratch = [pltpu.VMEM((tm, tn), F32)]
    else:
        scratch = [pltpu.VMEM((tm, tk), BF16) if cache_a else pltpu.VMEM((8, LANES), F32)]
    return _call(
        body, name=name, grid=(m // tm, n // tn, nk),
        in_specs=[a_spec, b_spec] + ([pl.BlockSpec((tm, tn), lambda i, j, kk: (i, j))] if has_add else []),
        out_specs=[o_spec], out_shape=[o_shape], args=(a, b) + ((add,) if has_add else ()),
        scratch_shapes=scratch, sem=("parallel", "arbitrary", "arbitrary"),
        gather=gather, exchange=exchange)


def _ln_fwd_math(u, g, b):
    mu = jnp.mean(u, axis=-1, keepdims=True)
    uc = u - mu
    var = jnp.mean(uc * uc, axis=-1, keepdims=True)
    return uc * lax.rsqrt(var + LN_EPS) * g + b


def _ln_bwd_math(dy, u, g):
    mu = jnp.mean(u, axis=-1, keepdims=True)
    uc = u - mu
    var = jnp.mean(uc * uc, axis=-1, keepdims=True)
    rstd = lax.rsqrt(var + LN_EPS)
    xhat = uc * rstd
    dxh = dy * g
    m1 = jnp.mean(dxh, axis=-1, keepdims=True)
    m2 = jnp.mean(dxh * xhat, axis=-1, keepdims=True)
    return rstd * (dxh - m1 - xhat * m2), xhat


def _matmul_resid_ln(pieces, w, x, g, b, name, tm=1024):
    s = pieces[0].shape[0]
    k, d = w.shape
    widths = [p.shape[1] for p in pieces]

    def body(*refs):
        a_refs = refs[:len(pieces)]
        w_ref, x_ref, g_ref, b_ref, u_ref, y_ref = refs[len(pieces):]
        u = ALPHA * x_ref[...]
        lo = 0
        for a_ref, width in zip(a_refs, widths):
            u = u + _dot(a_ref[...].astype(BF16), w_ref[lo:lo + width, :], NN)
            lo += width
        u_ref[...] = u
        y_ref[...] = _ln_fwd_math(u, g_ref[...], b_ref[...])

    row = pl.BlockSpec((tm, d), lambda i: (i, 0))
    vec = pl.BlockSpec((1, d), lambda i: (0, 0))
    return pl.pallas_call(
        body, name=name, grid=(s // tm,),
        in_specs=[pl.BlockSpec((tm, width), lambda i: (i, 0)) for width in widths]
        + [pl.BlockSpec((k, d), lambda i: (0, 0)), row, vec, vec],
        out_specs=[row, row], out_shape=[jax.ShapeDtypeStruct((s, d), F32)] * 2,
        compiler_params=_params("parallel"),
    )(*pieces, w, x, g, b)


def _ln_bwd(dy, u, g, name, tm=1024):
    s, d = dy.shape
    nt = s // tm

    def body(dy_ref, u_ref, g_ref, du_ref, dg_ref, db_ref, g8, b8):
        i = pl.program_id(0)
        dy_ = dy_ref[...]
        du, xhat = _ln_bwd_math(dy_, u_ref[...], g_ref[...])
        du_ref[...] = du

        @pl.when(i == 0)
        def _():
            g8[...] = jnp.zeros_like(g8)
            b8[...] = jnp.zeros_like(b8)

        g8[...] += _rowsum8(dy_ * xhat)
        b8[...] += _rowsum8(dy_)

        @pl.when(i == nt - 1)
        def _():
            dg_ref[...] = jnp.sum(g8[...], axis=0, keepdims=True)
            db_ref[...] = jnp.sum(b8[...], axis=0, keepdims=True)

    row = pl.BlockSpec((tm, d), lambda i: (i, 0))
    vec = pl.BlockSpec((1, d), lambda i: (0, 0))
    return pl.pallas_call(
        body, name=name, grid=(nt,),
        in_specs=[row, row, vec], out_specs=[row, vec, vec],
        out_shape=[jax.ShapeDtypeStruct((s, d), F32), jax.ShapeDtypeStruct((1, d), F32),
                   jax.ShapeDtypeStruct((1, d), F32)],
        scratch_shapes=[pltpu.VMEM((8, d), F32), pltpu.VMEM((8, d), F32)],
        compiler_params=_params("arbitrary"),
    )(dy, u, g)


FF_PAIR = 2 * FF_PAD
N_PAIR = N_DEV // 2
FF_COLS = 256


def _ffn_fwd(x, wgt, wut, wd, g, b, name, tm=1024, gather=()):
    s, d = x.shape

    def body(x_ref, wg_ref, wu_ref, wd_ref, g_ref, b_ref, u_ref, y_ref, a_ref, bb_ref, xb, acc):
        k = pl.program_id(1)

        @pl.when(k == 0)
        def _():
            xb[...] = x_ref[...].astype(BF16)

        a = _dot(xb[...], wg_ref[...], NT)
        bb = _dot(xb[...], wu_ref[...], NT)
        a_ref[...] = a.astype(BF16)
        bb_ref[...] = bb.astype(BF16)
        h = (a * _sigmoid(a) * bb).astype(BF16)
        part = _dot(h, wd_ref[...], NN)

        @pl.when(k == 0)
        def _():
            acc[...] = part

        @pl.when(k > 0)
        def _():
            acc[...] += part

        @pl.when(k == N_PAIR - 1)
        def _():
            u = ALPHA * x_ref[...] + 0.5 * acc[...]
            u_ref[...] = u
            y_ref[...] = _ln_fwd_math(u, g_ref[...], b_ref[...])

    row = pl.BlockSpec((tm, d), lambda i, k: (i, 0))
    vec = pl.BlockSpec((1, d), lambda i, k: (0, 0))
    w_in = pl.BlockSpec((None, FF_PAIR, d), lambda i, k: (k, 0, 0))
    w_dn = w_in
    hid = pl.BlockSpec((tm, FF_PAIR), lambda i, k: (i, k))
    return _call(
        body, name=name, grid=(s // tm, N_PAIR),
        in_specs=[row, w_in, w_in, w_dn, vec, vec], out_specs=[row, row, hid, hid],
        out_shape=[jax.ShapeDtypeStruct((s, d), F32)] * 2 + [jax.ShapeDtypeStruct((s, N_DEV * FF_PAD), BF16)] * 2,
        args=(x, wgt, wut, wd, g, b),
        scratch_shapes=[pltpu.VMEM((tm, d), BF16), pltpu.VMEM((tm, d), F32)],
        sem=("parallel", "arbitrary"), gather=gather)


def _ffn_bwd_x(dy, u, x, wgt, wut, wd, g, a_fwd, b_fwd, name, tm=512, exchange=()):
    s, d = x.shape
    nt = s // tm
    ffp = N_DEV * FF_PAD

    def body(dy_ref, u_ref, x_ref, wg_ref, wu_ref, wd_ref, g_ref, a_ref, bb_ref,
             dx_ref, xb, df_ref, da_ref, db_ref, h_ref, dg_ref, dbl_ref,
             dfb, du_s, acc, g8, b8):
        i = pl.program_id(0)
        k = pl.program_id(1)

        @pl.when(k == 0)
        def _():
            dy_ = dy_ref[...]
            du, xhat = _ln_bwd_math(dy_, u_ref[...], g_ref[...])
            du_s[...] = du
            dfb[...] = (0.5 * du).astype(BF16)
            df_ref[...] = dfb[...]
            xb[...] = x_ref[...].astype(BF16)

            @pl.when(i == 0)
            def _():
                g8[...] = jnp.zeros_like(g8)
                b8[...] = jnp.zeros_like(b8)

            g8[...] += _rowsum8(dy_ * xhat)
            b8[...] += _rowsum8(dy_)

        dh_all = _dot(dfb[...], wd_ref[...], NT)

        def gate_grads(c):
            cs = slice(FF_COLS * c, FF_COLS * (c + 1))
            a = a_ref[:, cs].astype(F32)
            bb = bb_ref[:, cs].astype(F32)
            dh = dh_all[:, cs]
            sig = _sigmoid(a)
            sa = a * sig
            h_ref[:, cs] = (sa * bb).astype(BF16)
            da = (dh * bb * (sig * (1.0 + a * (1.0 - sig)))).astype(BF16)
            db = (dh * sa).astype(BF16)
            da_ref[:, cs] = da
            db_ref[:, cs] = db
            return da, db

        n_chunks = FF_PAIR // FF_COLS
        chunks = [gate_grads(0)]
        part = None
        for c in range(n_chunks):
            if c + 1 < n_chunks:
                chunks.append(gate_grads(c + 1))
            cs = slice(FF_COLS * c, FF_COLS * (c + 1))
            pc = _dot(chunks[c][0], wg_ref[cs, :], NN) + _dot(chunks[c][1], wu_ref[cs, :], NN)
            part = pc if part is None else part + pc

        @pl.when(k == 0)
        def _():
            acc[...] = part

        @pl.when(k > 0)
        def _():
            acc[...] += part

        @pl.when(k == N_PAIR - 1)
        def _():
            dx_ref[...] = ALPHA * du_s[...] + acc[...]

        @pl.when((k == N_PAIR - 1) & (i == nt - 1))
        def _():
            dg_ref[...] = jnp.sum(g8[...], axis=0, keepdims=True)
            dbl_ref[...] = jnp.sum(b8[...], axis=0, keepdims=True)

    row = pl.BlockSpec((tm, d), lambda i, k: (i, 0))
    vec = pl.BlockSpec((1, d), lambda i, k: (0, 0))
    w_in = pl.BlockSpec((None, FF_PAIR, d), lambda i, k: (k, 0, 0))
    hid = pl.BlockSpec((tm, FF_PAIR), lambda i, k: (i, k))
    return _call(
        body, name=name, grid=(nt, N_PAIR),
        in_specs=[row, row, row, w_in, w_in, w_in, vec, hid, hid],
        out_specs=[row, row, row, hid, hid, hid, vec, vec],
        out_shape=[jax.ShapeDtypeStruct((s, d), F32), jax.ShapeDtypeStruct((s, d), BF16),
                   jax.ShapeDtypeStruct((s, d), BF16),
                   jax.ShapeDtypeStruct((s, ffp), BF16), jax.ShapeDtypeStruct((s, ffp), BF16),
                   jax.ShapeDtypeStruct((s, ffp), BF16),
                   jax.ShapeDtypeStruct((1, d), F32), jax.ShapeDtypeStruct((1, d), F32)],
        args=(dy, u, x, wgt, wut, wd, g, a_fwd, b_fwd),
        scratch_shapes=[pltpu.VMEM((tm, d), BF16), pltpu.VMEM((tm, d), F32),
                        pltpu.VMEM((tm, d), F32), pltpu.VMEM((8, d), F32), pltpu.VMEM((8, d), F32)],
        sem=("arbitrary", "arbitrary"), exchange=exchange)


def _ffn_bwd_w(tok, hid, name, *, down, tm=2048, exchange=()):
    s, d = tok.shape
    nt = s // tm

    def body(t_ref, h_ref, dw_ref, acc):
        i = pl.program_id(1)
        part = _dot(h_ref[...], t_ref[...], TN) if down else _dot(t_ref[...], h_ref[...], TN)

        @pl.when(i == 0)
        def _():
            acc[...] = part

        @pl.when(i > 0)
        def _():
            acc[...] += part

        @pl.when(i == nt - 1)
        def _():
            for j in range(2):
                lo = j * FF_PAD
                dw_ref[j] = (acc[lo:lo + FF_SHARD, :] if down else acc[:, lo:lo + FF_SHARD]).astype(BF16)

    blk = (FF_SHARD, d) if down else (d, FF_SHARD)
    return _call(
        body, name=name, grid=(N_PAIR, nt),
        in_specs=[pl.BlockSpec((tm, d), lambda k, i: (i, 0)), pl.BlockSpec((tm, FF_PAIR), lambda k, i: (i, k))],
        out_specs=[pl.BlockSpec((2,) + blk, lambda k, i: (k, 0, 0))],
        out_shape=[jax.ShapeDtypeStruct((N_DEV,) + blk, BF16)], args=(tok, hid),
        scratch_shapes=[pltpu.VMEM((FF_PAIR, d) if down else (d, FF_PAIR), F32)],
        sem=("parallel", "arbitrary"), exchange=exchange)


def _bucket_tables():
    qi = np.arange(BLK)[:, None]
    ki = np.arange(2 * BLK)[None, :]
    off = qi + BLK - ki
    out = []
    for window, dil in DILATED:
        n_keys = window // dil
        dist = dil * np.clip(off, 0, n_keys)
        exact = REL_BUCKETS // 2
        df = np.maximum(dist, 1).astype(np.float32)
        large = exact + (np.log(df / np.float32(exact)) / np.float32(math.log(REL_MAX_DIST / exact))
                         * np.float32(REL_BUCKETS - exact)).astype(np.int32)
        large = np.minimum(large, REL_BUCKETS - 1)
        bucket = np.where(dist < exact, dist, large).astype(np.int32)
        band = (off >= 0) & (off <= n_keys)
        out.append(np.where(band, bucket, -1))
    return np.stack(out).astype(np.int32)


def _bias_fwd(rel_bias, buckets, name="bias_fwd"):
    def body(tbl_ref, bkt_ref, out_ref):
        bkt = bkt_ref[...]
        for h in range(ATT_HEADS):
            acc = jnp.full((BLK, 2 * BLK), NEG, F32)
            for bb in range(REL_BUCKETS):
                acc = jnp.where(bkt == bb, tbl_ref[bb, h], acc)
            out_ref[h] = acc

    nbr = len(DILATED)
    return pl.pallas_call(
        body, name=name, grid=(nbr,),
        in_specs=[pl.BlockSpec(memory_space=pltpu.SMEM),
                  pl.BlockSpec((None, BLK, 2 * BLK), lambda r: (r, 0, 0))],
        out_specs=pl.BlockSpec((None, ATT_HEADS, BLK, 2 * BLK), lambda r: (r, 0, 0, 0)),
        out_shape=jax.ShapeDtypeStruct((nbr, ATT_HEADS, BLK, 2 * BLK), F32),
        compiler_params=_params("parallel"),
    )(rel_bias, buckets)


def _bias_bwd(dbias, buckets, name="bias_bwd"):
    nbr = len(DILATED)

    def body(db_ref, bkt_ref, out_ref):
        r = pl.program_id(0)

        @pl.when(r == 0)
        def _():
            out_ref[...] = jnp.zeros_like(out_ref)

        bkt = bkt_ref[...]
        rowi = lax.broadcasted_iota(jnp.int32, (REL_BUCKETS, LANES), 0)
        coli = lax.broadcasted_iota(jnp.int32, (REL_BUCKETS, LANES), 1)
        acc = jnp.zeros((REL_BUCKETS, LANES), F32)
        for h in range(ATT_HEADS):
            x = db_ref[h]
            for bb in range(REL_BUCKETS):
                part = jnp.sum(jnp.where(bkt == bb, x, 0.0), axis=0, keepdims=True)
                tot = jnp.sum(part, axis=1, keepdims=True)
                acc = acc + jnp.where((rowi == bb) & (coli == h), tot, 0.0)
        out_ref[...] += acc

    return pl.pallas_call(
        body, name=name, grid=(nbr,),
        in_specs=[pl.BlockSpec((None, ATT_HEADS, BLK, 2 * BLK), lambda r: (r, 0, 0, 0)),
                  pl.BlockSpec((None, BLK, 2 * BLK), lambda r: (r, 0, 0))],
        out_specs=pl.BlockSpec((REL_BUCKETS, LANES), lambda r: (0, 0)),
        out_shape=jax.ShapeDtypeStruct((REL_BUCKETS, LANES), F32),
        compiler_params=_params("arbitrary"),
    )(dbias, buckets)


def _stack_heads(pair, lo):
    return jnp.concatenate([jnp.where(lo, pair, 0.0), jnp.where(lo, 0.0, pair)], axis=0)


def _head_cols(pair, lo, reduce):
    fill = -jnp.inf if reduce is jnp.max else 0.0
    return jnp.concatenate([reduce(jnp.where(lo, pair, fill), axis=1, keepdims=True),
                            reduce(jnp.where(lo, fill, pair), axis=1, keepdims=True)], axis=0)


def _unstack_heads(x2, lo):
    return jnp.where(lo, x2[:BLK], x2[BLK:])


def _att_scores(q2, kk, bias2, first_ok):
    sc = _dot(q2, kk, NT) * (64 ** -0.5) + bias2
    return jnp.where(first_ok, sc, NEG)


DIL_TILE = 2048
DIL_COLS = ATT_W // LANES
DIL_UNROLL_FWD = 16
DIL_UNROLL_BWD = 16


def _dil_rows(dil, n, r, base=0):
    start = base + n * (BLK * dil) + r
    return pl.ds(start, BLK, stride=dil) if dil > 1 else pl.ds(start, BLK)


def _dil_in_specs(tile_of):
    cur = lambda col: pl.BlockSpec((DIL_TILE, LANES), lambda p, i: (tile_of(i), col * DIL_COLS + p))
    prev = lambda col: pl.BlockSpec((DIL_TILE, LANES), lambda p, i: (jnp.maximum(tile_of(i) - 1, 0), col * DIL_COLS + p))
    bias = pl.BlockSpec((len(DILATED), None, 2 * BLK, 2 * BLK), lambda p, i: (0, p, 0, 0))
    return [cur(0), prev(1), cur(1), prev(2), cur(2), bias]


def _pair_bias(biasm):
    return biasm.reshape(len(DILATED), DIL_COLS, 2 * BLK, 2 * BLK)


def _dil_fwd(proj, biasm, name="dil_fwd", gather=()):
    s = proj.shape[0]
    nt = s // DIL_TILE
    tt = DIL_TILE

    def body(q_ref, kp_ref, kc_ref, vp_ref, vc_ref, bias_ref, att_ref, lse_ref, k2, v2, ob, lb):
        t = pl.program_id(1)
        k2[0:tt, :] = kp_ref[...]
        k2[tt:2 * tt, :] = kc_ref[...]
        v2[0:tt, :] = vp_ref[...]
        v2[tt:2 * tt, :] = vc_ref[...]
        lo = lax.broadcasted_iota(jnp.int32, (BLK, LANES), 1) < 64
        kidx = lax.broadcasted_iota(jnp.int32, (2 * BLK, 2 * BLK), 1)
        for b, (_, dil) in enumerate(DILATED):
            nblk = tt // (BLK * dil)

            def step(j, carry, b=b, dil=dil, nblk=nblk):
                r, n = j % dil, j // dil
                cur, prev = _dil_rows(dil, n, r, tt), _dil_rows(dil, n - 1, r, tt)
                here = _dil_rows(dil, n, r)
                q_pair = q_ref[here, :]
                kk = jnp.concatenate([k2[prev, :], k2[cur, :]], axis=0).astype(BF16)
                vv = jnp.concatenate([v2[prev, :], v2[cur, :]], axis=0).astype(BF16)
                first_ok = (t > 0) | (n > 0) | (kidx >= BLK)
                sc = _att_scores(_stack_heads(q_pair, lo).astype(BF16), kk, bias_ref[b], first_ok)
                mx = jnp.max(sc, axis=1, keepdims=True)
                pe = jnp.exp(sc - mx)
                l = jnp.sum(pe, axis=1, keepdims=True)
                ob.at[b][here, :] = _unstack_heads(_dot(pe.astype(BF16), vv, NN) / l, lo)
                lb.at[b][here, :] = _unstack_heads(jnp.broadcast_to(mx + jnp.log(l), (2 * BLK, LANES)), lo)
                return carry

            lax.fori_loop(0, tt // BLK, step, 0, unroll=DIL_UNROLL_FWD)
        l0, l1, l2 = lb[0], lb[1], lb[2]
        mx = jnp.maximum(jnp.maximum(l0, l1), l2)
        e0, e1, e2 = jnp.exp(l0 - mx), jnp.exp(l1 - mx), jnp.exp(l2 - mx)
        tot = e0 + e1 + e2
        att_ref[...] = (e0 * ob[0] + e1 * ob[1] + e2 * ob[2]) / tot
        lse_ref[...] = mx + jnp.log(tot)

    out = pl.BlockSpec((tt, LANES), lambda p, i: (i, p))
    return _call(
        body, name=name, grid=(DIL_COLS, nt), in_specs=_dil_in_specs(lambda i: i), out_specs=[out, out],
        out_shape=[jax.ShapeDtypeStruct((s, ATT_W), F32)] * 2, args=(proj, proj, proj, proj, proj, _pair_bias(biasm)),
        scratch_shapes=[pltpu.VMEM((2 * tt, LANES), F32), pltpu.VMEM((2 * tt, LANES), F32),
                        pltpu.VMEM((len(DILATED), tt, LANES), F32), pltpu.VMEM((len(DILATED), tt, LANES), F32)],
        sem=("parallel", "parallel"), gather=gather)


def _dil_bwd(proj, biasm, lse, att, dcat, name="dil_bwd"):
    s = proj.shape[0]
    nt = s // DIL_TILE
    tt = DIL_TILE
    nbr = len(DILATED)

    def body(q_ref, kp_ref, kc_ref, vp_ref, vc_ref, bias_ref, lse_ref, att_ref, datt_ref,
             dq_ref, dk_ref, dv_ref, dbias_ref, k2, v2, dqa, dka, dva, kcar, vcar):
        i = pl.program_id(1)
        t = nt - 1 - i
        k2[0:tt, :] = kp_ref[...]
        k2[tt:2 * tt, :] = kc_ref[...]
        v2[0:tt, :] = vp_ref[...]
        v2[tt:2 * tt, :] = vc_ref[...]

        @pl.when(i == 0)
        def _():
            kcar[...] = jnp.zeros_like(kcar)
            vcar[...] = jnp.zeros_like(vcar)
            dbias_ref[...] = jnp.zeros_like(dbias_ref)

        dqa[...] = jnp.zeros_like(dqa)
        dka[0:tt, :] = jnp.zeros((tt, LANES), F32)
        dva[0:tt, :] = jnp.zeros((tt, LANES), F32)
        dka[tt:2 * tt, :] = kcar[...]
        dva[tt:2 * tt, :] = vcar[...]
        lo = lax.broadcasted_iota(jnp.int32, (BLK, LANES), 1) < 64
        kidx = lax.broadcasted_iota(jnp.int32, (2 * BLK, 2 * BLK), 1)
        for b, (_, dil) in enumerate(DILATED):
            nblk = tt // (BLK * dil)

            def step(j, carry, b=b, dil=dil, nblk=nblk):
                r, n = j % dil, j // dil
                cur, prev = _dil_rows(dil, n, r, tt), _dil_rows(dil, n - 1, r, tt)
                here = _dil_rows(dil, n, r)
                q_pair = q_ref[here, :]
                kk = jnp.concatenate([k2[prev, :], k2[cur, :]], axis=0).astype(BF16)
                vv = jnp.concatenate([v2[prev, :], v2[cur, :]], axis=0).astype(BF16)
                first_ok = (t > 0) | (n > 0) | (kidx >= BLK)
                dat_pair = datt_ref[here, :]
                q2 = _stack_heads(q_pair, lo).astype(BF16)
                dom = _stack_heads(dat_pair, lo).astype(BF16)
                sc = _att_scores(q2, kk, bias_ref[b], first_ok)
                pr = jnp.exp(sc - _head_cols(lse_ref[here, :], lo, jnp.max))
                ds = pr * (_dot(dom, vv, NT) - _head_cols(dat_pair * att_ref[here, :], lo, jnp.sum))
                dbias_ref[b] += ds
                dsb = (ds * (64 ** -0.5)).astype(BF16)
                dk2 = _dot(dsb, q2, TN)
                dv2 = _dot(pr.astype(BF16), dom, TN)
                dqa[here, :] += _unstack_heads(_dot(dsb, kk, NN), lo)
                dka[prev, :] += dk2[:BLK]
                dka[cur, :] += dk2[BLK:]
                dva[prev, :] += dv2[:BLK]
                dva[cur, :] += dv2[BLK:]
                return carry

            lax.fori_loop(0, tt // BLK, step, 0, unroll=DIL_UNROLL_BWD)
        dq_ref[...] = dqa[...].astype(BF16)
        dk_ref[...] = dka[tt:2 * tt, :].astype(BF16)
        dv_ref[...] = dva[tt:2 * tt, :].astype(BF16)
        kcar[...] = dka[0:tt, :]
        vcar[...] = dva[0:tt, :]

    rev = lambda i: nt - 1 - i
    out = pl.BlockSpec((tt, LANES), lambda p, i: (rev(i), p))
    two = lambda: pltpu.VMEM((2 * tt, LANES), F32)
    one = lambda: pltpu.VMEM((tt, LANES), F32)
    return pl.pallas_call(
        body, name=name, grid=(DIL_COLS, nt),
        in_specs=_dil_in_specs(rev) + [out, out, out],
        out_specs=[out, out, out, pl.BlockSpec((nbr, None, 2 * BLK, 2 * BLK), lambda p, i: (0, p, 0, 0))],
        out_shape=[jax.ShapeDtypeStruct((s, ATT_W), BF16)] * 3
        + [jax.ShapeDtypeStruct((nbr, DIL_COLS, 2 * BLK, 2 * BLK), F32)],
        scratch_shapes=[two(), two(), one(), two(), two(), one(), one()],
        compiler_params=_params("arbitrary", "arbitrary"),
    )(proj, proj, proj, proj, proj, _pair_bias(biasm), lse, att, dcat)


QK_COL0 = (3 * ATT_W) // ATT_W


def _conv_shifted(prev, cur, j, row):
    sh = CONV_K - 1 - j
    if sh == 0:
        return cur
    return jnp.where(row < sh, pltpu.roll(prev, sh, 0), pltpu.roll(cur, sh, 0))


def _conv_z(prev, cur, w_ref, b_ref, row):
    z = b_ref[...] + cur * w_ref[CONV_K - 1:CONV_K, :]
    for j in range(CONV_K - 1):
        z = z + _conv_shifted(prev, cur, j, row) * w_ref[j:j + 1, :]
    return z


def _conv_fwd(proj, conv_w, conv_b, name="conv_fwd", tm=512):
    s = proj.shape[0]
    w = ATT_W

    def body(prev_ref, cur_ref, w_ref, b_ref, o_ref):
        i = pl.program_id(1)
        row = lax.broadcasted_iota(jnp.int32, (tm, w), 0)
        prev = jnp.where(i > 0, prev_ref[...], 0.0)
        z = _conv_z(prev, cur_ref[...], w_ref, b_ref, row)
        o_ref[...] = z * _sigmoid(z)

    return pl.pallas_call(
        body, name=name, grid=(2, s // tm),
        in_specs=[pl.BlockSpec((tm, w), lambda j, i: (jnp.maximum(i - 1, 0), QK_COL0 + j)),
                  pl.BlockSpec((tm, w), lambda j, i: (i, QK_COL0 + j)),
                  pl.BlockSpec((CONV_K, w), lambda j, i: (0, j)),
                  pl.BlockSpec((1, w), lambda j, i: (0, j))],
        out_specs=pl.BlockSpec((tm, w), lambda j, i: (i, j)),
        out_shape=jax.ShapeDtypeStruct((s, 2 * ML_W), F32),
        compiler_params=_params("parallel", "parallel"),
    )(proj, proj, conv_w, conv_b)


def _conv_bwd(proj, dqk, conv_w, conv_b, name="conv_bwd", tm=512):
    s = proj.shape[0]
    w = ATT_W
    nt = s // tm

    def body(xp_ref, xc_ref, xn_ref, dc_ref, dn_ref, w_ref, b_ref, dx_ref, dw_ref, db_ref):
        i = pl.program_id(1)
        row = lax.broadcasted_iota(jnp.int32, (tm, w), 0)
        prev = jnp.where(i > 0, xp_ref[...], 0.0)
        cur = xc_ref[...]

        def dz_of(pv, cv, dy):
            z = _conv_z(pv, cv, w_ref, b_ref, row)
            sig = _sigmoid(z)
            return dy * (sig * (1.0 + z * (1.0 - sig)))

        dzc = dz_of(prev, cur, dc_ref[...])
        dzn = jnp.where(i < nt - 1, dz_of(cur, xn_ref[...], dn_ref[...]), 0.0)
        dx = dzc * w_ref[CONV_K - 1:CONV_K, :]
        for j in range(CONV_K - 1):
            sh = CONV_K - 1 - j
            up = jnp.where(row >= tm - sh, pltpu.roll(dzn, tm - sh, 0), pltpu.roll(dzc, tm - sh, 0))
            dx = dx + up * w_ref[j:j + 1, :]
        dx_ref[...] = dx

        @pl.when(i == 0)
        def _():
            dw_ref[...] = jnp.zeros_like(dw_ref)
            db_ref[...] = jnp.zeros_like(db_ref)

        for j in range(CONV_K):
            dw_ref[j:j + 1, :] += jnp.sum(dzc * _conv_shifted(prev, cur, j, row), axis=0, keepdims=True)
        db_ref[...] += jnp.sum(dzc, axis=0, keepdims=True)

    xs = lambda f: pl.BlockSpec((tm, w), lambda j, i: (f(i), QK_COL0 + j))
    ds = lambda f: pl.BlockSpec((tm, w), lambda j, i: (f(i), j))
    return pl.pallas_call(
        body, name=name, grid=(2, nt),
        in_specs=[xs(lambda i: jnp.maximum(i - 1, 0)), xs(lambda i: i), xs(lambda i: jnp.minimum(i + 1, nt - 1)),
                  ds(lambda i: i), ds(lambda i: jnp.minimum(i + 1, nt - 1)),
                  pl.BlockSpec((CONV_K, w), lambda j, i: (0, j)), pl.BlockSpec((1, w), lambda j, i: (0, j))],
        out_specs=[ds(lambda i: i), pl.BlockSpec((CONV_K, w), lambda j, i: (0, j)),
                   pl.BlockSpec((1, w), lambda j, i: (0, j))],
        out_shape=[jax.ShapeDtypeStruct((s, 2 * ML_W), F32), jax.ShapeDtypeStruct((CONV_K, 2 * ML_W), F32),
                   jax.ShapeDtypeStruct((1, 2 * ML_W), F32)],
        compiler_params=_params("parallel", "arbitrary"),
    )(proj, proj, proj, dqk, dqk, conv_w, conv_b)


def _bf16_mm(dims_fwd):
    @jax.custom_vjp
    def mm(a, b):
        return _dot(a.astype(BF16), b.astype(BF16), dims_fwd)

    def fwd(a, b):
        return mm(a, b), (a, b)

    def bwd(res, g):
        a, b = res
        if dims_fwd is NN:
            return _mm_nt(g, b), _mm_tn(a, g)
        if dims_fwd is NT:
            return _mm_nn(g, b), _mm_tn(g, a)
        return _mm_nt(b, g), _mm_nn(a, g)

    mm.defvjp(fwd, bwd)
    return mm


_mm_nn = _bf16_mm(NN)
_mm_nt = _bf16_mm(NT)
_mm_tn = _bf16_mm(TN)


def _tri(lower):
    r = lax.broadcasted_iota(jnp.int32, (CHUNK, CHUNK), 0)
    c = lax.broadcasted_iota(jnp.int32, (CHUNK, CHUNK), 1)
    return ((r >= c) if lower else (r <= c)).astype(F32)


@jax.custom_vjp
def _cumsum_rows(x):
    return lax.dot_general(_tri(True), x, NN, precision=lax.Precision.HIGHEST, preferred_element_type=F32)


def _cumsum_fwd(x):
    return _cumsum_rows(x), None


def _cumsum_bwd(_, g):
    return (lax.dot_general(_tri(False), g, NN, precision=lax.Precision.HIGHEST, preferred_element_type=F32),)


_cumsum_rows.defvjp(_cumsum_fwd, _cumsum_bwd)


def _abs(x):
    return jnp.where(x >= 0, x, -x)


def _log_sigmoid(x):
    return jnp.minimum(x, 0.0) - jnp.log(1.0 + jnp.exp(-_abs(x)))


def _pick_col(x, lane):
    sel = lax.broadcasted_iota(jnp.int32, x.shape, 1) == lane
    return jnp.sum(jnp.where(sel, x, 0.0), axis=1, keepdims=True)


def _pick_row(x, r):
    sel = lax.broadcasted_iota(jnp.int32, x.shape, 0) == r
    return jnp.sum(jnp.where(sel, x, 0.0), axis=0, keepdims=True)


def _mlstm_chunk(qs, ks, vs, oms, gates, gate_bias, mlg, cs, ns, ms):
    gb = gates + gate_bias
    cum = _cumsum_rows(_log_sigmoid(gb))
    gbt = gb.T
    cumt = cum.T
    causal = lax.broadcasted_iota(jnp.int32, (CHUNK, CHUNK), 0) >= lax.broadcasted_iota(jnp.int32, (CHUNK, CHUNK), 1)
    hd = range(ML_HEADS)
    k = [ks[h] * (ML_HD ** -0.5) for h in hd]
    ig_col = [_pick_col(gb, h) for h in hd]
    ig_row = [_pick_row(gbt, h) for h in hd]
    b_col = [_pick_col(cum, ML_HEADS + h) for h in hd]
    b_row = [_pick_row(cumt, ML_HEADS + h) for h in hd]
    g = [_pick_row(b_col[h], CHUNK - 1) for h in hd]
    a = [g[h] - b_col[h] + ig_col[h] for h in hd]
    m_loc = [jnp.max(a[h], axis=0, keepdims=True) for h in hd]
    wa = [jnp.exp(a[h] - m_loc[h]) for h in hd]
    d_log = [jnp.where(causal, b_col[h] - b_row[h] + ig_row[h], -jnp.inf) for h in hd]
    e_log = [b_col[h] + ms[h] for h in hd]
    m_t = [jnp.maximum(e_log[h], jnp.max(d_log[h], axis=1, keepdims=True)) for h in hd]
    d_w = [jnp.exp(d_log[h] - m_t[h]) for h in hd]
    e_w = [jnp.exp(e_log[h] - m_t[h]) for h in hd]
    qk = [_mm_nt(qs[h], k[h]) for h in hd]
    qc = [_mm_nt(qs[h], cs[h]) for h in hd]
    c_loc = [_mm_tn(wa[h] * vs[h], k[h]) for h in hd]
    s_qk = [qk[h] * d_w[h] for h in hd]
    sv = [_mm_nn(s_qk[h], vs[h]) for h in hd]
    n_loc = [jnp.sum(wa[h] * k[h], axis=0, keepdims=True) for h in hd]
    m_out = [jnp.maximum(g[h] + ms[h], m_loc[h]) for h in hd]
    sp = [jnp.exp(g[h] + ms[h] - m_out[h]) for h in hd]
    sl = [jnp.exp(m_loc[h] - m_out[h]) for h in hd]
    c_out = [sp[h] * cs[h] + sl[h] * c_loc[h] for h in hd]
    n_out = [sp[h] * ns[h] + sl[h] * n_loc[h] for h in hd]
    num = [e_w[h] * qc[h] + sv[h] for h in hd]
    den = [e_w[h] * jnp.sum(qs[h] * ns[h], axis=1, keepdims=True) + jnp.sum(s_qk[h], axis=1, keepdims=True) for h in hd]
    hg = [_sigmoid(oms[h]) * (num[h] / jnp.maximum(_abs(den[h]), jnp.exp(-m_t[h]))) for h in hd]
    mu = [jnp.mean(hg[h], axis=1, keepdims=True) for h in hd]
    hc = [hg[h] - mu[h] for h in hd]
    var = [jnp.mean(hc[h] * hc[h], axis=1, keepdims=True) for h in hd]
    ys = [hc[h] * lax.rsqrt(var[h] + LN_EPS) * mlg[h] for h in hd]
    return ys, c_out, n_out, m_out


V_COL = 5
O_COL = 6
ML_SUB = 1


def _mlstm_fwd(qk, proj, gates, gate_bias, mlg, name="mlstm_fwd", gather=()):
    s = qk.shape[0]
    nc = s // CHUNK

    def body(q_ref, k_ref, v_ref, o_ref, g_ref, gb_ref, mlg_ref, y_ref, cp_ref, np_ref, mp_ref, c_s, n_s, m_s):
        ci = pl.program_id(0)

        @pl.when(ci == 0)
        def _():
            c_s[...] = jnp.zeros_like(c_s)
            n_s[...] = jnp.zeros_like(n_s)
            m_s[...] = jnp.zeros_like(m_s)

        for sub in range(ML_SUB):
            rows = slice(CHUNK * sub, CHUNK * (sub + 1))
            hs = lambda ref: [ref[rows, LANES * h:LANES * (h + 1)] for h in range(ML_HEADS)]
            cp_ref[sub] = c_s[...]
            np_ref[sub] = n_s[...]
            mp_ref[sub] = m_s[...]
            ys, c_new, n_new, m_new = _mlstm_chunk(
                hs(q_ref), hs(k_ref), hs(v_ref), hs(o_ref), g_ref[rows, :], gb_ref[...],
                [mlg_ref[:, LANES * h:LANES * (h + 1)] for h in range(ML_HEADS)],
                [c_s[h] for h in range(ML_HEADS)], [n_s[h:h + 1, :] for h in range(ML_HEADS)],
                [m_s[h:h + 1, 0:1] for h in range(ML_HEADS)])
            for h in range(ML_HEADS):
                y_ref[rows, LANES * h:LANES * (h + 1)] = ys[h]
                c_s[h] = c_new[h]
                n_s[h:h + 1, :] = n_new[h]
                m_s[h:h + 1, :] = jnp.broadcast_to(m_new[h], (1, LANES))

    blk = lambda col: pl.BlockSpec((ML_SUB * CHUNK, ML_W), lambda ci: (ci, col))
    vec = lambda w: pl.BlockSpec((1, w), lambda ci: (0, 0))
    return _call(
        body, name=name, grid=(nc // ML_SUB,), args=(qk, qk, proj, proj, gates, gate_bias, mlg), sem=("arbitrary",),
        gather=gather,
        in_specs=[blk(0), blk(1), blk(V_COL), blk(O_COL), pl.BlockSpec((ML_SUB * CHUNK, LANES), lambda ci: (ci, 0)),
                  vec(LANES), vec(ML_W)],
        out_specs=[blk(0), pl.BlockSpec((ML_SUB, ML_HEADS, ML_HD, ML_HD), lambda ci: (ci, 0, 0, 0)),
                   pl.BlockSpec((ML_SUB, 8, LANES), lambda ci: (ci, 0, 0)),
                   pl.BlockSpec((ML_SUB, 8, LANES), lambda ci: (ci, 0, 0))],
        out_shape=[jax.ShapeDtypeStruct((s, ML_W), F32), jax.ShapeDtypeStruct((nc, ML_HEADS, ML_HD, ML_HD), F32),
                   jax.ShapeDtypeStruct((nc, 8, LANES), F32), jax.ShapeDtypeStruct((nc, 8, LANES), F32)],
        scratch_shapes=[pltpu.VMEM((ML_HEADS, ML_HD, ML_HD), F32), pltpu.VMEM((8, LANES), F32),
                        pltpu.VMEM((8, LANES), F32)])


def _mlstm_bwd(qk, proj, gates, gate_bias, mlg, cprev, nprev, mprev, dy, name="mlstm_bwd", exchange=()):
    s = qk.shape[0]
    nc = s // CHUNK

    def body(q_ref, k_ref, v_ref, o_ref, g_ref, gb_ref, mlg_ref, cp_ref, np_ref, mp_ref, dy_ref,
             dqk_ref, dv_ref, do_ref, dg_ref, dgb_ref, dmlg_ref, dc_s, dn_s, dm_s, gb8, mg8):
        ci = pl.program_id(0)

        @pl.when(ci == 0)
        def _():
            dc_s[...] = jnp.zeros_like(dc_s)
            dn_s[...] = jnp.zeros_like(dn_s)
            dm_s[...] = jnp.zeros_like(dm_s)
            gb8[...] = jnp.zeros_like(gb8)
            mg8[...] = jnp.zeros_like(mg8)

        for sub in reversed(range(ML_SUB)):
            rows = slice(CHUNK * sub, CHUNK * (sub + 1))
            hs = lambda ref: [ref[rows, LANES * h:LANES * (h + 1)] for h in range(ML_HEADS)]
            prim = (hs(q_ref), hs(k_ref), hs(v_ref), hs(o_ref), g_ref[rows, :], gb_ref[...],
                    [mlg_ref[:, LANES * h:LANES * (h + 1)] for h in range(ML_HEADS)],
                    [cp_ref[sub, h] for h in range(ML_HEADS)], [np_ref[sub, h:h + 1, :] for h in range(ML_HEADS)],
                    [mp_ref[sub, h:h + 1, 0:1] for h in range(ML_HEADS)])
            _, vjp = jax.vjp(_mlstm_chunk, *prim)
            cot = (hs(dy_ref), [dc_s[h] for h in range(ML_HEADS)], [dn_s[h:h + 1, :] for h in range(ML_HEADS)],
                   [dm_s[h:h + 1, 0:1] for h in range(ML_HEADS)])
            dqs, dks, dvs, dos, dg, dgb, dmlg, dcs, dns, dms = vjp(cot)
            dg_ref[rows, :] = dg
            gb8[0:1, :] += dgb
            for h in range(ML_HEADS):
                sl = slice(LANES * h, LANES * (h + 1))
                dqk_ref[rows, sl] = dqs[h]
                dqk_ref[rows, ML_W + LANES * h:ML_W + LANES * (h + 1)] = dks[h]
                dv_ref[rows, sl] = dvs[h]
                do_ref[rows, sl] = dos[h]
                mg8[0:1, sl] += dmlg[h]
                dc_s[h] = dcs[h]
                dn_s[h:h + 1, :] = dns[h]
                dm_s[h:h + 1, :] = jnp.broadcast_to(dms[h], (1, LANES))

        @pl.when(ci == nb - 1)
        def _():
            dgb_ref[...] = gb8[0:1, :]
            dmlg_ref[...] = mg8[0:1, :]

    nb = nc // ML_SUB
    rev = lambda ci: nb - 1 - ci
    blk = lambda col: pl.BlockSpec((ML_SUB * CHUNK, ML_W), lambda ci: (rev(ci), col))
    vec = lambda w: pl.BlockSpec((1, w), lambda ci: (0, 0))
    st8 = pl.BlockSpec((ML_SUB, 8, LANES), lambda ci: (rev(ci), 0, 0))
    gsp = pl.BlockSpec((ML_SUB * CHUNK, LANES), lambda ci: (rev(ci), 0))
    return _call(
        body, name=name, grid=(nb,), sem=("arbitrary",), exchange=exchange,
        args=(qk, qk, proj, proj, gates, gate_bias, mlg, cprev, nprev, mprev, dy),
        in_specs=[blk(0), blk(1), blk(V_COL), blk(O_COL), gsp, vec(LANES), vec(ML_W),
                  pl.BlockSpec((ML_SUB, ML_HEADS, ML_HD, ML_HD), lambda ci: (rev(ci), 0, 0, 0)), st8, st8, blk(1)],
        out_specs=[pl.BlockSpec((ML_SUB * CHUNK, 2 * ML_W), lambda ci: (rev(ci), 0)), blk(0), blk(0), gsp, vec(LANES),
                   vec(ML_W)],
        out_shape=[jax.ShapeDtypeStruct((s, 2 * ML_W), F32),
                   jax.ShapeDtypeStruct((s, ML_W), F32), jax.ShapeDtypeStruct((s, ML_W), F32),
                   jax.ShapeDtypeStruct((s, LANES), F32), jax.ShapeDtypeStruct((1, LANES), F32),
                   jax.ShapeDtypeStruct((1, ML_W), F32)],
        scratch_shapes=[pltpu.VMEM((ML_HEADS, ML_HD, ML_HD), F32), pltpu.VMEM((8, LANES), F32),
                        pltpu.VMEM((8, LANES), F32), pltpu.VMEM((8, LANES), F32), pltpu.VMEM((8, ML_W), F32)])


def _xattn_tile(qs, ks, vs):
    outs = []
    for q, k, v in zip(qs, ks, vs):
        sc = _mm_nt(q, k) * (XA_HD ** -0.5)
        mx = lax.stop_gradient(jnp.max(sc, axis=1, keepdims=True))
        pe = jnp.exp(sc - mx)
        outs.append(_mm_nn(pe / jnp.sum(pe, axis=1, keepdims=True), v))
    return outs


def _xa_heads(ref):
    return [ref[:, XA_HD * h:XA_HD * (h + 1)] for h in range(XA_HEADS)]


def _xattn_fwd(q, kv, name="xattn_fwd", tm=512):
    s, d = q.shape

    def body(q_ref, k_ref, v_ref, o_ref):
        outs = _xattn_tile(_xa_heads(q_ref), _xa_heads(k_ref), _xa_heads(v_ref))
        for h in range(XA_HEADS):
            o_ref[:, XA_HD * h:XA_HD * (h + 1)] = outs[h]

    row = pl.BlockSpec((tm, d), lambda i: (i, 0))
    return pl.pallas_call(
        body, name=name, grid=(s // tm,),
        in_specs=[row, pl.BlockSpec((MEM_LEN, d), lambda i: (0, 0)), pl.BlockSpec((MEM_LEN, d), lambda i: (0, 1))],
        out_specs=row, out_shape=jax.ShapeDtypeStruct((s, d), F32),
        compiler_params=_params("parallel"),
    )(q, kv, kv)


def _xattn_bwd(q, kv, do, name="xattn_bwd", tm=512):
    s, d = q.shape

    def body(q_ref, k_ref, v_ref, do_ref, dq_ref, dkv_ref):
        i = pl.program_id(0)
        _, vjp = jax.vjp(_xattn_tile, _xa_heads(q_ref), _xa_heads(k_ref), _xa_heads(v_ref))
        dqs, dks, dvs = vjp(_xa_heads(do_ref))

        @pl.when(i == 0)
        def _():
            dkv_ref[...] = jnp.zeros_like(dkv_ref)

        for h in range(XA_HEADS):
            sl = slice(XA_HD * h, XA_HD * (h + 1))
            dq_ref[:, sl] = dqs[h]
            dkv_ref[:, sl] += dks[h]
            dkv_ref[:, d + XA_HD * h:d + XA_HD * (h + 1)] += dvs[h]

    row = pl.BlockSpec((tm, d), lambda i: (i, 0))
    return pl.pallas_call(
        body, name=name, grid=(s // tm,),
        in_specs=[row, pl.BlockSpec((MEM_LEN, d), lambda i: (0, 0)), pl.BlockSpec((MEM_LEN, d), lambda i: (0, 1)), row],
        out_specs=[row, pl.BlockSpec((MEM_LEN, 2 * d), lambda i: (0, 0))],
        out_shape=[jax.ShapeDtypeStruct((s, d), F32), jax.ShapeDtypeStruct((MEM_LEN, 2 * d), F32)],
        compiler_params=_params("arbitrary"),
    )(q, kv, kv, do)


def _loss_head(y, target, name="loss_head", tm=1024):
    s, d = y.shape
    nt = s // tm

    def body(y_ref, t_ref, dy_ref, loss_ref, acc):
        i = pl.program_id(0)
        err = y_ref[...] - t_ref[...]
        dy_ref[...] = err * (1.0 / d)

        @pl.when(i == 0)
        def _():
            acc[...] = jnp.zeros_like(acc)

        acc[...] += _rowsum8(err * err)

        @pl.when(i == nt - 1)
        def _():
            tot = jnp.sum(jnp.sum(acc[...], axis=0, keepdims=True), axis=1, keepdims=True)
            loss_ref[...] = jnp.broadcast_to(tot * (0.5 / d), (1, LANES))

    row = pl.BlockSpec((tm, d), lambda i: (i, 0))
    return pl.pallas_call(
        body, name=name, grid=(nt,),
        in_specs=[row, row], out_specs=[row, pl.BlockSpec((1, LANES), lambda i: (0, 0))],
        out_shape=[jax.ShapeDtypeStruct((s, d), F32), jax.ShapeDtypeStruct((1, LANES), F32)],
        scratch_shapes=[pltpu.VMEM((8, d), F32)],
        compiler_params=_params("arbitrary"),
    )(y, target)


def _adam2d(recv, w, m, v, name, layer=None):
    rows, cols = w.shape[-2:]
    fits = [t for t in range(16, rows + 1, 16) if rows % t == 0 and t * cols <= 128 * 1024]
    tr = max(fits) if fits else rows

    def body(r_ref, w_ref, m_ref, v_ref, g_ref, d_ref, mo_ref, vo_ref):
        g = r_ref[0].astype(F32)
        for j in range(1, N_DEV):
            g = g + r_ref[j].astype(F32)
        mn = ADAM_B1 * m_ref[...] + (1.0 - ADAM_B1) * g
        vn = ADAM_B2 * v_ref[...] + (1.0 - ADAM_B2) * jnp.square(g)
        m_hat = mn / (1.0 - ADAM_B1 ** ADAM_STEP)
        v_hat = vn / (1.0 - ADAM_B2 ** ADAM_STEP)
        g_ref[...] = g
        d_ref[...] = -ADAM_LR * (m_hat / (jnp.sqrt(v_hat) + ADAM_EPS) + ADAM_WD * w_ref[...])
        mo_ref[...] = mn
        vo_ref[...] = vn

    row = pl.BlockSpec((tr, cols), lambda i: (i, 0))
    if layer is None:
        wspec = row
    else:
        wspec = pl.BlockSpec((None, None, tr, cols), lambda i: (0, layer, i, 0))
    return pl.pallas_call(
        body, name=name, grid=(rows // tr,),
        in_specs=[pl.BlockSpec((N_DEV, tr, cols), lambda i: (0, i, 0)), wspec, wspec, wspec],
        out_specs=[row] * 4, out_shape=[jax.ShapeDtypeStruct((rows, cols), F32)] * 4,
        compiler_params=_params("parallel"),
    )(recv, w, m, v)


WEIGHTS = ("rel_bias", "ln_g", "ln_b", "ffn_w_gate", "ffn_w_up", "ffn_w_down", "w_in", "conv_w", "conv_b",
           "ig_bias", "fg_bias", "ml_norm_g", "w_out", "xq_w", "xkv_w", "xo_w")
SMALL = ("rel_bias", "ln_g", "ln_b", "conv_w", "conv_b", "ig_bias", "fg_bias", "ml_norm_g")
SMALL_SHAPES = {
    "rel_bias": (REL_BUCKETS, ATT_HEADS), "ln_g": (1, 4, LANES), "ln_b": (1, 4, LANES), "conv_w": (1, CONV_K, LANES),
    "conv_b": (1, 2 * ML_W), "ig_bias": (1, ML_HEADS), "fg_bias": (1, ML_HEADS), "ml_norm_g": (1, ML_W),
}
SMALL_ROWS = 8


def _pack_small(parts, lead=()):
    out = []
    for p in parts:
        p = jnp.pad(p, [(0, 0)] * len(lead) + [(0, SMALL_ROWS * LANES - p.shape[-1])])
        out.append(p.reshape(lead + (SMALL_ROWS, LANES)))
    return jnp.concatenate(out, axis=len(lead))


def _unpack_small(flat):
    out = {}
    for i, n in enumerate(SMALL):
        cnt = int(np.prod(SMALL_SHAPES[n]))
        out[n] = flat[SMALL_ROWS * i:SMALL_ROWS * (i + 1)].reshape(-1)[:cnt].reshape(SMALL_SHAPES[n])
    return out


def _split8(full, axis):
    shp = full.shape
    t = full.reshape(shp[:axis] + (N_DEV, shp[axis] // N_DEV) + shp[axis + 1:])
    return jnp.moveaxis(t, axis, 0).reshape(N_DEV, -1)


def _rep8(full):
    return jnp.broadcast_to(full.reshape(1, -1), (N_DEV, full.size))


def kernel(x, mem, rel_bias, ln_g, ln_b, ffn_w_gate, ffn_w_up, ffn_w_down, w_in, conv_w, conv_b, ig_bias, fg_bias, ml_norm_g, w_out, xq_w, xkv_w, xo_w, loss_target, m_rel_bias, m_ln_g, m_ln_b, m_ffn_w_gate, m_ffn_w_up, m_ffn_w_down, m_w_in, m_conv_w, m_conv_b, m_ig_bias, m_fg_bias, m_ml_norm_g, m_w_out, m_xq_w, m_xkv_w, m_xo_w, v_rel_bias, v_ln_g, v_ln_b, v_ffn_w_gate, v_ffn_w_up, v_ffn_w_down, v_w_in, v_conv_w, v_conv_b, v_ig_bias, v_fg_bias, v_ml_norm_g, v_w_out, v_xq_w, v_xkv_w, v_xo_w):
    w_tree = dict(rel_bias=rel_bias, ln_g=ln_g, ln_b=ln_b, ffn_w_gate=ffn_w_gate, ffn_w_up=ffn_w_up,
                  ffn_w_down=ffn_w_down, w_in=w_in, conv_w=conv_w, conv_b=conv_b, ig_bias=ig_bias, fg_bias=fg_bias,
                  ml_norm_g=ml_norm_g, w_out=w_out, xq_w=xq_w, xkv_w=xkv_w, xo_w=xo_w)
    m_tree = dict(rel_bias=m_rel_bias, ln_g=m_ln_g, ln_b=m_ln_b, ffn_w_gate=m_ffn_w_gate, ffn_w_up=m_ffn_w_up,
                  ffn_w_down=m_ffn_w_down, w_in=m_w_in, conv_w=m_conv_w, conv_b=m_conv_b, ig_bias=m_ig_bias,
                  fg_bias=m_fg_bias, ml_norm_g=m_ml_norm_g, w_out=m_w_out, xq_w=m_xq_w, xkv_w=m_xkv_w, xo_w=m_xo_w)
    v_tree = dict(rel_bias=v_rel_bias, ln_g=v_ln_g, ln_b=v_ln_b, ffn_w_gate=v_ffn_w_gate, ffn_w_up=v_ffn_w_up,
                  ffn_w_down=v_ffn_w_down, w_in=v_w_in, conv_w=v_conv_w, conv_b=v_conv_b, ig_bias=v_ig_bias,
                  fg_bias=v_fg_bias, ml_norm_g=v_ml_norm_g, w_out=v_w_out, xq_w=v_xq_w, xkv_w=v_xkv_w, xo_w=v_xo_w)
    x0 = x[0]
    pad_ff = FF_PAD - FF_SHARD
    bf = lambda t: t.astype(BF16)

    pad_rows = lambda t: jnp.pad(t, ((0, pad_ff), (0, 0)))
    ffn_shards = [(pad_rows(bf(ffn_w_gate[0, l]).T), pad_rows(bf(ffn_w_up[0, l]).T), pad_rows(bf(ffn_w_down[0, l])))
                  for l in range(2)]
    pairs = lambda t: t.reshape(N_PAIR, FF_PAIR, D_MODEL)
    w_in_shard = jnp.pad(bf(w_in[0]), ((0, 0), (0, ATT_W - W_IN_SHARD)))
    small_shard = jnp.concatenate([ln_g[0], ln_b[0], conv_w[0], jnp.zeros((4, LANES), F32)], axis=0)
    gate_bias = jnp.pad(jnp.concatenate([ig_bias, fg_bias], axis=1), ((0, 0), (0, LANES - 2 * ML_HEADS)))
    buckets = _bucket_tables()

    wg0, wu0, wd0, small_all = _gather_two_level("ffn1_weights_gather", ffn_shards[0] + (small_shard,))
    wg0, wu0, wd0 = pairs(wg0), pairs(wu0), pairs(wd0)
    unshard = lambda t: jnp.moveaxis(t, 0, 1).reshape(4, D_MODEL)
    ln_g_full, ln_b_full, conv_w_full = unshard(small_all[:, 0:4]), unshard(small_all[:, 4:8]), unshard(small_all[:, 8:12])
    lng = lambda i: ln_g_full[i:i + 1]
    lnb = lambda i: ln_b_full[i:i + 1]

    u0, x1, a0, b0, win_all, wout_all, xq_all, xo_all, xkv_all = _ffn_fwd(
        x0, wg0, wu0, wd0, lng(0), lnb(0), "ffn1_fwd",
        gather=(w_in_shard, bf(w_out[0]), bf(xq_w[0]), bf(xo_w[0]), bf(xkv_w[0])))
    w_in_full = jnp.moveaxis(win_all[:, :, :W_IN_SHARD], 0, 1).reshape(D_MODEL, W_IN)
    w_main = w_in_full[:, :W_IN_MAIN]
    w_gate_cols = jnp.pad(w_in_full[:, W_IN_MAIN:], ((0, 0), (0, LANES - 2 * ML_HEADS)))
    w_out_full = wout_all.reshape(D_MODEL, D_MODEL)
    xq_full = xq_all.reshape(D_MODEL, D_MODEL)
    xo_full = xo_all.reshape(D_MODEL, D_MODEL)

    proj, wg1 = _matmul(x1, w_main, "nn", "proj_fwd", tn=W_IN_MAIN // 2, tk=D_MODEL, gather=(ffn_shards[1][0],))
    gates, = _matmul(x1, w_gate_cols, "nn", "gates_fwd", tk=D_MODEL)
    biasm = _bias_fwd(rel_bias, buckets)
    att, lse, wd1 = _dil_fwd(proj, biasm, gather=(ffn_shards[1][2],))
    qk = _conv_fwd(proj, conv_w_full, conv_b)
    y_m, c_prev, n_prev, m_prev, wu1 = _mlstm_fwd(qk, proj, gates, gate_bias, ml_norm_g, gather=(ffn_shards[1][1],))
    u1, x2 = _matmul_resid_ln((att, y_m), w_out_full, x1, lng(1), lnb(1), "w_out_fwd")
    q_x, = _matmul(x2, xq_full, "nn", "xq_fwd", tn=D_MODEL, tk=D_MODEL)
    kv, = _matmul(mem[0], xkv_all, "nn", "xkv_fwd", tk=D_MODEL)
    o_x = _xattn_fwd(q_x, kv)
    u2, x3 = _matmul_resid_ln((o_x,), xo_full, x2, lng(2), lnb(2), "xo_fwd")
    wg1, wu1, wd1 = pairs(wg1), pairs(wu1), pairs(wd1)
    u3, x4, a3, b3 = _ffn_fwd(x3, wg1, wu1, wd1, lng(3), lnb(3), "ffn2_fwd")
    dx4, loss_row = _loss_head(x4, loss_target[0])

    dx3, xb, df, da, db, hh, dg3, db3 = _ffn_bwd_x(dx4, u3, x3, wg1, wu1, wd1, lng(3), a3, b3, "ffn2_bwd_x")
    ffn2_send = (_ffn_bwd_w(xb, da, "ffn2_bwd_wg", down=False)[0], _ffn_bwd_w(xb, db, "ffn2_bwd_wu", down=False)[0],
                 _ffn_bwd_w(df, hh, "ffn2_bwd_wd", down=True)[0])

    du2, dg2, db2 = _ln_bwd(dx3, u2, lng(2), "xattn_ln_bwd")
    do_x, = _matmul(du2, xo_full, "nt", "xo_bwd_x", tn=D_MODEL, tk=D_MODEL)
    g_xo, = _matmul(o_x, du2, "tn", "xo_bwd_w", tm=D_MODEL, tn=D_MODEL, out_dtype=BF16)
    dq_x, dkv = _xattn_bwd(q_x, kv, do_x)
    g_xq, = _matmul(x2, dq_x, "tn", "xq_bwd_w", tm=D_MODEL, tn=D_MODEL, out_dtype=BF16)
    g_xkv, = _matmul(mem[0], dkv, "tn", "xkv_bwd_w", tm=D_MODEL, tn=2 * D_MODEL // N_DEV, tk=MEM_LEN,
                     out_dtype=BF16, blocked_out=True)
    dx2, = _matmul(dq_x, xq_full, "nt", "xq_bwd_x", tn=D_MODEL, tk=D_MODEL, add=du2, add_scale=ALPHA)

    du1, dg1, db1 = _ln_bwd(dx2, u1, lng(1), "mixer_ln_bwd")
    dcat, = _matmul(du1, w_out_full, "nt", "w_out_bwd_x", tn=D_MODEL, tk=D_MODEL)
    g_w_out = jnp.concatenate(
        [_matmul(half, du1, "tn", f"w_out_bwd_w_{i}", tn=D_MODEL, out_dtype=BF16)[0] for i, half in enumerate((att, y_m))],
        axis=0)
    dqk, dv_m, do_m, dgates, dgate_bias, g_mlg, *ffn2_recv = _mlstm_bwd(
        qk, proj, gates, gate_bias, ml_norm_g, c_prev, n_prev, m_prev, dcat, exchange=tuple(ffn2_send))
    dqk_pre, g_conv_w, g_conv_b = _conv_bwd(proj, dqk, conv_w_full, conv_b)
    dq_a, dk_a, dv_a, dbias = _dil_bwd(proj, biasm, lse, att, dcat)
    g_rel = _bias_bwd(dbias.reshape(biasm.shape), buckets)[:, :ATT_HEADS]
    dproj = jnp.concatenate([dq_a, dk_a, dv_a, bf(dqk_pre), bf(dv_m), bf(do_m)], axis=1)
    g_w_main, = _matmul(x1, dproj, "tn", "proj_bwd_w", tm=D_MODEL, tn=W_IN_MAIN // 2, tk=1024, out_dtype=BF16)
    g_w_gates, = _matmul(x1, dgates, "tn", "gates_bwd_w", tm=D_MODEL, out_dtype=BF16)
    g_w_in = jnp.concatenate([g_w_main, g_w_gates[:, :2 * ML_HEADS]], axis=1)
    dx1, = _matmul(dproj, w_main, "nt", "proj_bwd_x", tn=D_MODEL, tk=W_IN_MAIN // 2, add=du1, add_scale=ALPHA)
    dx1, = _matmul(dgates, w_gate_cols, "nt", "gates_bwd_x", tn=D_MODEL, add=dx1)

    rows8 = lambda t: t.reshape(N_DEV, D_MODEL // N_DEV, D_MODEL)
    mid_send = (rows8(g_xo), rows8(g_xq), g_xkv, rows8(g_w_out),
                jnp.moveaxis(g_w_in.reshape(D_MODEL, N_DEV, W_IN_SHARD), 1, 0))
    dx0, xb, df, da, db, hh, dg0, db0, r_xo, r_xq, r_xkv, r_w_out, r_w_in = _ffn_bwd_x(
        dx1, u0, x0, wg0, wu0, wd0, lng(0), a0, b0, "ffn1_bwd_x", exchange=mid_send)
    small_blocks = {
        "rel_bias": _rep8(g_rel),
        "ln_g": _split8(jnp.concatenate([dg0, dg1, dg2, dg3], axis=0), 1),
        "ln_b": _split8(jnp.concatenate([db0, db1, db2, db3], axis=0), 1),
        "conv_w": _split8(g_conv_w, 1),
        "conv_b": _rep8(g_conv_b),
        "ig_bias": _rep8(dgate_bias[:, :ML_HEADS]),
        "fg_bias": _rep8(dgate_bias[:, ML_HEADS:2 * ML_HEADS]),
        "ml_norm_g": _rep8(g_mlg),
    }
    small_send = _pack_small([small_blocks[n] for n in SMALL], lead=(N_DEV,))
    g_wg, r_small = _ffn_bwd_w(xb, da, "ffn1_bwd_wg", down=False, exchange=(small_send,))
    g_wu, r_wg = _ffn_bwd_w(xb, db, "ffn1_bwd_wu", down=False, exchange=(g_wg,))
    g_wd, r_wu = _ffn_bwd_w(df, hh, "ffn1_bwd_wd", down=True, exchange=(g_wu,))
    r_wd, = _exchange_only("ffn1_grads_exchange", exchange=(g_wd,))
    ffn1_recv = [r_wg, r_wu, r_wd]

    res = {}
    for i, n in enumerate(("ffn_w_gate", "ffn_w_up", "ffn_w_down")):
        per_layer = [_adam2d(r[i], w_tree[n], m_tree[n], v_tree[n], f"adamw_{n}_{l}", layer=l)
                     for l, r in enumerate((ffn1_recv, ffn2_recv))]
        res[n] = [jnp.stack([per_layer[0][j], per_layer[1][j]])[None] for j in range(4)]
    for n, r in (("w_in", r_w_in), ("w_out", r_w_out), ("xq_w", r_xq), ("xkv_w", r_xkv), ("xo_w", r_xo)):
        res[n] = [t[None] for t in _adam2d(r, w_tree[n][0], m_tree[n][0], v_tree[n][0], f"adamw_{n}")]
    pack = lambda tree: _pack_small([tree[n].reshape(-1) for n in SMALL])
    small = [_unpack_small(t) for t in _adam2d(r_small, pack(w_tree), pack(m_tree), pack(v_tree), "adamw_small")]
    for n in SMALL:
        res[n] = [small[j][n] for j in range(4)]

    loss = lax.psum(loss_row[0, 0], ("x", "y", "c"))
    return (loss, dx0[None], *[res[n][0] for n in WEIGHTS], *[res[n][1] for n in WEIGHTS],
            *[res[n][2] for n in WEIGHTS], *[res[n][3] for n in WEIGHTS])
```

```python
import functools
import math

import numpy as np
import jax
import jax.numpy as jnp
from jax import lax
from jax.experimental import pallas as pl
from jax.experimental.pallas import tpu as pltpu

F32 = jnp.float32
BF16 = jnp.bfloat16

N_DEV = 8
D_MODEL = 1024
D_FF = 2816
FF_SHARD = D_FF // N_DEV
FF_PAD = 384
ATT_W = 512
ATT_HEADS = 8
DILATED = ((128, 1), (512, 4), (2048, 16))
BLK = 128
ML_W = 512
ML_HEADS = 4
ML_HD = 128
CHUNK = 128
CONV_K = 4
W_IN = 3592
W_IN_SHARD = W_IN // N_DEV
W_IN_MAIN = 3584
XA_HEADS = 4
XA_HD = 256
MEM_LEN = 256
REL_BUCKETS = 32
REL_MAX_DIST = 2048
ALPHA = 2.0 ** 0.25
LN_EPS = 1e-5
NEG = -1e30
ADAM_LR = 0.001
ADAM_B1 = 0.9
ADAM_B2 = 0.999
ADAM_EPS = 1e-08
ADAM_WD = 0.01
ADAM_STEP = 10
LANES = 128
VMEM_LIMIT = 58 * 1024 * 1024

NN = (((1,), (0,)), ((), ()))
NT = (((1,), (1,)), ((), ()))
TN = (((0,), (0,)), ((), ()))


def _dot(a, b, dims):
    return lax.dot_general(a, b, dims, preferred_element_type=F32)


def _params(*sem):
    return pltpu.CompilerParams(dimension_semantics=sem, vmem_limit_bytes=VMEM_LIMIT)


def _sigmoid(x):
    return 1.0 / (1.0 + jnp.exp(-x))


def _rowsum8(x):
    t, c = x.shape
    return jnp.sum(x.reshape(t // 8, 8, c), axis=0)


def _mesh_pos():
    x, y, c = lax.axis_index("x"), lax.axis_index("y"), lax.axis_index("c")
    return x, y, c, 4 * x + 2 * y + c


def _peer(x, y, c, k):
    px = 1 - x if k & 4 else x
    py = 1 - y if k & 2 else y
    pc = 1 - c if k & 1 else c
    return (px, py, pc), 4 * px + 2 * py + pc


def _call(body, *, name, grid, in_specs, out_specs, out_shape, args, scratch_shapes=(), sem=None,
          gather=(), exchange=()):
    in_specs, out_specs, out_shape, scratch = list(in_specs), list(out_specs), list(out_shape), list(scratch_shapes)
    ng, nc = len(gather), len(gather) + len(exchange)
    if nc == 0:
        return pl.pallas_call(body, name=name, grid=grid, in_specs=in_specs, out_specs=out_specs,
                              out_shape=out_shape, scratch_shapes=scratch, compiler_params=_params(*sem))(*args)
    n_in, n_out, n_scr = len(in_specs), len(out_specs), len(scratch)

    def wrapped(*refs):
        ins, cin = refs[:n_in], refs[n_in:n_in + nc]
        outs, cout = refs[n_in + nc:n_in + nc + n_out], refs[n_in + nc + n_out:n_in + 2 * nc + n_out]
        scr = refs[n_in + 2 * nc + n_out:n_in + 2 * nc + n_out + n_scr]
        send_sems, recv_sems, loc_sems = refs[-3:]
        first, last = None, None
        for ax, extent in enumerate(grid):
            f, l = pl.program_id(ax) == 0, pl.program_id(ax) == extent - 1
            first = f if first is None else first & f
            last = l if last is None else last & l

        def copies():
            x, y, c, me = _mesh_pos()
            out = []
            for a in range(nc):
                mine = cin[a] if a < ng else cin[a].at[me]
                out.append(pltpu.make_async_copy(mine, cout[a].at[me], loc_sems.at[a]))
                for k in range(1, N_DEV):
                    peer, pidx = _peer(x, y, c, k)
                    out.append(pltpu.make_async_remote_copy(
                        src_ref=cin[a] if a < ng else cin[a].at[pidx], dst_ref=cout[a].at[me],
                        send_sem=send_sems.at[a, k - 1], recv_sem=recv_sems.at[a, k - 1],
                        device_id=peer, device_id_type=pl.DeviceIdType.MESH))
            return out

        @pl.when(first)
        def _():
            for cp in copies():
                cp.start()

        body(*ins, *outs, *scr)

        @pl.when(last)
        def _():
            for cp in copies():
                cp.wait()

    hbm = pl.BlockSpec(memory_space=pl.ANY)
    comm_shapes = [jax.ShapeDtypeStruct((N_DEV,) + a.shape, a.dtype) for a in gather]
    comm_shapes += [jax.ShapeDtypeStruct(a.shape, a.dtype) for a in exchange]
    return pl.pallas_call(
        wrapped, name=name, grid=grid, in_specs=in_specs + [hbm] * nc, out_specs=out_specs + [hbm] * nc,
        out_shape=out_shape + comm_shapes,
        scratch_shapes=scratch + [pltpu.SemaphoreType.DMA((nc, N_DEV - 1)), pltpu.SemaphoreType.DMA((nc, N_DEV - 1)),
                                  pltpu.SemaphoreType.DMA((nc,))],
        compiler_params=_params(*(("arbitrary",) * len(grid))),
    )(*args, *gather, *exchange)


def _gather_two_level(name, arrays):
    na = len(arrays)

    def body(*refs):
        srcs, outs = refs[:na], refs[na:2 * na]
        send_sems, recv_sems, loc_sems = refs[2 * na:]
        x, y, c, me = _mesh_pos()
        here, sib = (x, y, c), (x, y, 1 - c)
        chips = [(1 - x, y), (x, 1 - y), (1 - x, 1 - y)]
        pos = lambda px, py, pc: 4 * px + 2 * py + pc

        def copy(a, k, block, to, src=None):
            return pltpu.make_async_remote_copy(
                src_ref=outs[a].at[block] if src is None else src, dst_ref=outs[a].at[block],
                send_sem=send_sems.at[a, k], recv_sem=recv_sems.at[a, k], device_id=to,
                device_id_type=pl.DeviceIdType.MESH)

        locs = [pltpu.make_async_copy(srcs[a], outs[a].at[me], loc_sems.at[a]) for a in range(na)]
        for cp in locs:
            cp.start()
        first = []
        for a in range(na):
            first.append(copy(a, 0, me, sib, src=srcs[a]))
            first += [copy(a, 1 + j, me, (*chip, c), src=srcs[a]) for j, chip in enumerate(chips)]
        for cp in first:
            cp.start()
        passed = []
        for a in range(na):
            for j, chip in enumerate(chips):
                copy(a, 1 + j, pos(*chip, c), here).wait_recv()
                passed.append(copy(a, 4 + j, pos(*chip, c), sib))
                passed[-1].start()
        for a in range(na):
            copy(a, 0, pos(x, y, 1 - c), here).wait_recv()
            for j, chip in enumerate(chips):
                copy(a, 4 + j, pos(*chip, 1 - c), here).wait_recv()
        for cp in first + passed:
            cp.wait_send()
        for cp in locs:
            cp.wait()

    hbm = pl.BlockSpec(memory_space=pl.ANY)
    return pl.pallas_call(
        body, name=name, in_specs=[hbm] * na, out_specs=[hbm] * na,
        out_shape=[jax.ShapeDtypeStruct((N_DEV,) + a.shape, a.dtype) for a in arrays],
        scratch_shapes=[pltpu.SemaphoreType.DMA((na, N_DEV - 1)), pltpu.SemaphoreType.DMA((na, N_DEV - 1)),
                        pltpu.SemaphoreType.DMA((na,))],
    )(*arrays)


def _exchange_only(name, gather=(), exchange=()):
    return _call(lambda: None, name=name, grid=(1,), in_specs=[], out_specs=[], out_shape=[], args=(),
                 gather=gather, exchange=exchange)


def _matmul(a, b, mode, name, *, out_dtype=F32, tm=1024, tn=512, tk=512, add=None, add_scale=1.0,
            blocked_out=False, gather=(), exchange=()):
    blocked_b = b.ndim == 3
    if blocked_b:
        (m, k), (nb, _, tn) = a.shape, b.shape
        n = nb * tn
    elif mode == "nn":
        (m, k), (_, n) = a.shape, b.shape
    elif mode == "nt":
        (m, k), (n, _) = a.shape, b.shape
    else:
        (k, m), (_, n) = a.shape, b.shape
    tm, tn, tk = min(tm, m), min(tn, n), min(tk, k)
    nk = k // tk
    dims = {"nn": NN, "nt": NT, "tn": TN}[mode]
    if mode == "tn":
        a_spec = pl.BlockSpec((tk, tm), lambda i, j, kk: (kk, i))
    else:
        a_spec = pl.BlockSpec((tm, tk), lambda i, j, kk: (i, kk))
    if blocked_b:
        b_spec = pl.BlockSpec((None, tk, tn), lambda i, j, kk: (j, kk, 0))
    elif mode == "nt":
        b_spec = pl.BlockSpec((tn, tk), lambda i, j, kk: (j, kk))
    else:
        b_spec = pl.BlockSpec((tk, tn), lambda i, j, kk: (kk, j))
    if blocked_out:
        o_spec = pl.BlockSpec((None, tm, tn), lambda i, j, kk: (j, i, 0))
        o_shape = jax.ShapeDtypeStruct((n // tn, m, tn), out_dtype)
    else:
        o_spec = pl.BlockSpec((tm, tn), lambda i, j, kk: (i, j))
        o_shape = jax.ShapeDtypeStruct((m, n), out_dtype)
    has_add = add is not None
    cache_a = nk == 1 and mode != "tn" and n // tn > 1 and a.dtype != BF16

    def body(*refs):
        if has_add:
            a_ref, b_ref, add_ref, o_ref, s_ref = refs
        else:
            a_ref, b_ref, o_ref, s_ref = refs
        kk = pl.program_id(2)
        if cache_a:
            @pl.when(pl.program_id(1) == 0)
            def _():
                s_ref[...] = a_ref[...].astype(BF16)

            lhs = s_ref[...]
        else:
            lhs = a_ref[...].astype(BF16)
        part = _dot(lhs, b_ref[...].astype(BF16), dims)

        def finish(r):
            if has_add:
                r = r + add_scale * add_ref[...]
            o_ref[...] = r.astype(out_dtype)

        if nk == 1:
            finish(part)
            return

        @pl.when(kk == 0)
        def _():
            s_ref[...] = part

        @pl.when(kk > 0)
        def _():
            s_ref[...] += part

        @pl.when(kk == nk - 1)
        def _():
            finish(s_ref[...])

    if nk > 1:
        scratch = [pltpu.VMEM((tm, tn), F32)]
    else:
        scratch = [pltpu.VMEM((tm, tk), BF16) if cache_a else pltpu.VMEM((8, LANES), F32)]
    return _call(
        body, name=name, grid=(m // tm, n // tn, nk),
        in_specs=[a_spec, b_spec] + ([pl.BlockSpec((tm, tn), lambda i, j, kk: (i, j))] if has_add else []),
        out_specs=[o_spec], out_shape=[o_shape], args=(a, b) + ((add,) if has_add else ()),
        scratch_shapes=scratch, sem=("parallel", "arbitrary", "arbitrary"),
        gather=gather, exchange=exchange)


def _ln_fwd_math(u, g, b):
    mu = jnp.mean(u, axis=-1, keepdims=True)
    uc = u - mu
    var = jnp.mean(uc * uc, axis=-1, keepdims=True)
    return uc * lax.rsqrt(var + LN_EPS) * g + b


def _ln_bwd_math(dy, u, g):
    mu = jnp.mean(u, axis=-1, keepdims=True)
    uc = u - mu
    var = jnp.mean(uc * uc, axis=-1, keepdims=True)
    rstd = lax.rsqrt(var + LN_EPS)
    xhat = uc * rstd
    dxh = dy * g
    m1 = jnp.mean(dxh, axis=-1, keepdims=True)
    m2 = jnp.mean(dxh * xhat, axis=-1, keepdims=True)
    return rstd * (dxh - m1 - xhat * m2), xhat


def _matmul_resid_ln(pieces, w, x, g, b, name, tm=1024):
    s = pieces[0].shape[0]
    k, d = w.shape
    widths = [p.shape[1] for p in pieces]

    def body(*refs):
        a_refs = refs[:len(pieces)]
        w_ref, x_ref, g_ref, b_ref, u_ref, y_ref = refs[len(pieces):]
        u = ALPHA * x_ref[...]
        lo = 0
        for a_ref, width in zip(a_refs, widths):
            u = u + _dot(a_ref[...].astype(BF16), w_ref[lo:lo + width, :], NN)
            lo += width
        u_ref[...] = u
        y_ref[...] = _ln_fwd_math(u, g_ref[...], b_ref[...])

    row = pl.BlockSpec((tm, d), lambda i: (i, 0))
    vec = pl.BlockSpec((1, d), lambda i: (0, 0))
    return pl.pallas_call(
        body, name=name, grid=(s // tm,),
        in_specs=[pl.BlockSpec((tm, width), lambda i: (i, 0)) for width in widths]
        + [pl.BlockSpec((k, d), lambda i: (0, 0)), row, vec, vec],
        out_specs=[row, row], out_shape=[jax.ShapeDtypeStruct((s, d), F32)] * 2,
        compiler_params=_params("parallel"),
    )(*pieces, w, x, g, b)


def _ln_bwd(dy, u, g, name, tm=1024):
    s, d = dy.shape
    nt = s // tm

    def body(dy_ref, u_ref, g_ref, du_ref, dg_ref, db_ref, g8, b8):
        i = pl.program_id(0)
        dy_ = dy_ref[...]
        du, xhat = _ln_bwd_math(dy_, u_ref[...], g_ref[...])
        du_ref[...] = du

        @pl.when(i == 0)
        def _():
            g8[...] = jnp.zeros_like(g8)
            b8[...] = jnp.zeros_like(b8)

        g8[...] += _rowsum8(dy_ * xhat)
        b8[...] += _rowsum8(dy_)

        @pl.when(i == nt - 1)
        def _():
            dg_ref[...] = jnp.sum(g8[...], axis=0, keepdims=True)
            db_ref[...] = jnp.sum(b8[...], axis=0, keepdims=True)

    row = pl.BlockSpec((tm, d), lambda i: (i, 0))
    vec = pl.BlockSpec((1, d), lambda i: (0, 0))
    return pl.pallas_call(
        body, name=name, grid=(nt,),
        in_specs=[row, row, vec], out_specs=[row, vec, vec],
        out_shape=[jax.ShapeDtypeStruct((s, d), F32), jax.ShapeDtypeStruct((1, d), F32),
                   jax.ShapeDtypeStruct((1, d), F32)],
        scratch_shapes=[pltpu.VMEM((8, d), F32), pltpu.VMEM((8, d), F32)],
        compiler_params=_params("arbitrary"),
    )(dy, u, g)


FF_PAIR = 2 * FF_PAD
N_PAIR = N_DEV // 2
FF_COLS = 256


def _ffn_fwd(x, wgt, wut, wd, g, b, name, tm=1024, gather=()):
    s, d = x.shape

    def body(x_ref, wg_ref, wu_ref, wd_ref, g_ref, b_ref, u_ref, y_ref, a_ref, bb_ref, xb, acc):
        k = pl.program_id(1)

        @pl.when(k == 0)
        def _():
            xb[...] = x_ref[...].astype(BF16)

        a = _dot(xb[...], wg_ref[...], NT)
        bb = _dot(xb[...], wu_ref[...], NT)
        a_ref[...] = a.astype(BF16)
        bb_ref[...] = bb.astype(BF16)
        h = (a * _sigmoid(a) * bb).astype(BF16)
        part = _dot(h, wd_ref[...], NN)

        @pl.when(k == 0)
        def _():
            acc[...] = part

        @pl.when(k > 0)
        def _():
            acc[...] += part

        @pl.when(k == N_PAIR - 1)
        def _():
            u = ALPHA * x_ref[...] + 0.5 * acc[...]
            u_ref[...] = u
            y_ref[...] = _ln_fwd_math(u, g_ref[...], b_ref[...])

    row = pl.BlockSpec((tm, d), lambda i, k: (i, 0))
    vec = pl.BlockSpec((1, d), lambda i, k: (0, 0))
    w_in = pl.BlockSpec((None, FF_PAIR, d), lambda i, k: (k, 0, 0))
    w_dn = w_in
    hid = pl.BlockSpec((tm, FF_PAIR), lambda i, k: (i, k))
    return _call(
        body, name=name, grid=(s // tm, N_PAIR),
        in_specs=[row, w_in, w_in, w_dn, vec, vec], out_specs=[row, row, hid, hid],
        out_shape=[jax.ShapeDtypeStruct((s, d), F32)] * 2 + [jax.ShapeDtypeStruct((s, N_DEV * FF_PAD), BF16)] * 2,
        args=(x, wgt, wut, wd, g, b),
        scratch_shapes=[pltpu.VMEM((tm, d), BF16), pltpu.VMEM((tm, d), F32)],
        sem=("parallel", "arbitrary"), gather=gather)


def _ffn_bwd_x(dy, u, x, wgt, wut, wd, g, a_fwd, b_fwd, name, tm=512, exchange=()):
    s, d = x.shape
    nt = s // tm
    ffp = N_DEV * FF_PAD

    def body(dy_ref, u_ref, x_ref, wg_ref, wu_ref, wd_ref, g_ref, a_ref, bb_ref,
             dx_ref, xb, df_ref, da_ref, db_ref, h_ref, dg_ref, dbl_ref,
             dfb, du_s, acc, g8, b8):
        i = pl.program_id(0)
        k = pl.program_id(1)

        @pl.when(k == 0)
        def _():
            dy_ = dy_ref[...]
            du, xhat = _ln_bwd_math(dy_, u_ref[...], g_ref[...])
            du_s[...] = du
            dfb[...] = (0.5 * du).astype(BF16)
            df_ref[...] = dfb[...]
            xb[...] = x_ref[...].astype(BF16)

            @pl.when(i == 0)
            def _():
                g8[...] = jnp.zeros_like(g8)
                b8[...] = jnp.zeros_like(b8)

            g8[...] += _rowsum8(dy_ * xhat)
            b8[...] += _rowsum8(dy_)

        dh_all = _dot(dfb[...], wd_ref[...], NT)

        def gate_grads(c):
            cs = slice(FF_COLS * c, FF_COLS * (c + 1))
            a = a_ref[:, cs].astype(F32)
            bb = bb_ref[:, cs].astype(F32)
            dh = dh_all[:, cs]
            sig = _sigmoid(a)
            sa = a * sig
            h_ref[:, cs] = (sa * bb).astype(BF16)
            da = (dh * bb * (sig * (1.0 + a * (1.0 - sig)))).astype(BF16)
            db = (dh * sa).astype(BF16)
            da_ref[:, cs] = da
            db_ref[:, cs] = db
            return da, db

        n_chunks = FF_PAIR // FF_COLS
        chunks = [gate_grads(0)]
        part = None
        for c in range(n_chunks):
            if c + 1 < n_chunks:
                chunks.append(gate_grads(c + 1))
            cs = slice(FF_COLS * c, FF_COLS * (c + 1))
            pc = _dot(chunks[c][0], wg_ref[cs, :], NN) + _dot(chunks[c][1], wu_ref[cs, :], NN)
            part = pc if part is None else part + pc

        @pl.when(k == 0)
        def _():
            acc[...] = part

        @pl.when(k > 0)
        def _():
            acc[...] += part

        @pl.when(k == N_PAIR - 1)
        def _():
            dx_ref[...] = ALPHA * du_s[...] + acc[...]

        @pl.when((k == N_PAIR - 1) & (i == nt - 1))
        def _():
            dg_ref[...] = jnp.sum(g8[...], axis=0, keepdims=True)
            dbl_ref[...] = jnp.sum(b8[...], axis=0, keepdims=True)

    row = pl.BlockSpec((tm, d), lambda i, k: (i, 0))
    vec = pl.BlockSpec((1, d), lambda i, k: (0, 0))
    w_in = pl.BlockSpec((None, FF_PAIR, d), lambda i, k: (k, 0, 0))
    hid = pl.BlockSpec((tm, FF_PAIR), lambda i, k: (i, k))
    return _call(
        body, name=name, grid=(nt, N_PAIR),
        in_specs=[row, row, row, w_in, w_in, w_in, vec, hid, hid],
        out_specs=[row, row, row, hid, hid, hid, vec, vec],
        out_shape=[jax.ShapeDtypeStruct((s, d), F32), jax.ShapeDtypeStruct((s, d), BF16),
                   jax.ShapeDtypeStruct((s, d), BF16),
                   jax.ShapeDtypeStruct((s, ffp), BF16), jax.ShapeDtypeStruct((s, ffp), BF16),
                   jax.ShapeDtypeStruct((s, ffp), BF16),
                   jax.ShapeDtypeStruct((1, d), F32), jax.ShapeDtypeStruct((1, d), F32)],
        args=(dy, u, x, wgt, wut, wd, g, a_fwd, b_fwd),
        scratch_shapes=[pltpu.VMEM((tm, d), BF16), pltpu.VMEM((tm, d), F32),
                        pltpu.VMEM((tm, d), F32), pltpu.VMEM((8, d), F32), pltpu.VMEM((8, d), F32)],
        sem=("arbitrary", "arbitrary"), exchange=exchange)


def _ffn_bwd_w(tok, hid, name, *, down, tm=2048, exchange=()):
    s, d = tok.shape
    nt = s // tm

    def body(t_ref, h_ref, dw_ref, acc):
        i = pl.program_id(1)
        part = _dot(h_ref[...], t_ref[...], TN) if down else _dot(t_ref[...], h_ref[...], TN)

        @pl.when(i == 0)
        def _():
            acc[...] = part

        @pl.when(i > 0)
        def _():
            acc[...] += part

        @pl.when(i == nt - 1)
        def _():
            for j in range(2):
                lo = j * FF_PAD
                dw_ref[j] = (acc[lo:lo + FF_SHARD, :] if down else acc[:, lo:lo + FF_SHARD]).astype(BF16)

    blk = (FF_SHARD, d) if down else (d, FF_SHARD)
    return _call(
        body, name=name, grid=(N_PAIR, nt),
        in_specs=[pl.BlockSpec((tm, d), lambda k, i: (i, 0)), pl.BlockSpec((tm, FF_PAIR), lambda k, i: (i, k))],
        out_specs=[pl.BlockSpec((2,) + blk, lambda k, i: (k, 0, 0))],
        out_shape=[jax.ShapeDtypeStruct((N_DEV,) + blk, BF16)], args=(tok, hid),
        scratch_shapes=[pltpu.VMEM((FF_PAIR, d) if down else (d, FF_PAIR), F32)],
        sem=("parallel", "arbitrary"), exchange=exchange)


def _bucket_tables():
    qi = np.arange(BLK)[:, None]
    ki = np.arange(2 * BLK)[None, :]
    off = qi + BLK - ki
    out = []
    for window, dil in DILATED:
        n_keys = window // dil
        dist = dil * np.clip(off, 0, n_keys)
        exact = REL_BUCKETS // 2
        df = np.maximum(dist, 1).astype(np.float32)
        large = exact + (np.log(df / np.float32(exact)) / np.float32(math.log(REL_MAX_DIST / exact))
                         * np.float32(REL_BUCKETS - exact)).astype(np.int32)
        large = np.minimum(large, REL_BUCKETS - 1)
        bucket = np.where(dist < exact, dist, large).astype(np.int32)
        band = (off >= 0) & (off <= n_keys)
        out.append(np.where(band, bucket, -1))
    return np.stack(out).astype(np.int32)


def _bias_fwd(rel_bias, buckets, name="bias_fwd"):
    def body(tbl_ref, bkt_ref, out_ref):
        bkt = bkt_ref[...]
        for h in range(ATT_HEADS):
            acc = jnp.full((BLK, 2 * BLK), NEG, F32)
            for bb in range(REL_BUCKETS):
                acc = jnp.where(bkt == bb, tbl_ref[bb, h], acc)
            out_ref[h] = acc

    nbr = len(DILATED)
    return pl.pallas_call(
        body, name=name, grid=(nbr,),
        in_specs=[pl.BlockSpec(memory_space=pltpu.SMEM),
                  pl.BlockSpec((None, BLK, 2 * BLK), lambda r: (r, 0, 0))],
        out_specs=pl.BlockSpec((None, ATT_HEADS, BLK, 2 * BLK), lambda r: (r, 0, 0, 0)),
        out_shape=jax.ShapeDtypeStruct((nbr, ATT_HEADS, BLK, 2 * BLK), F32),
        compiler_params=_params("parallel"),
    )(rel_bias, buckets)


def _bias_bwd(dbias, buckets, name="bias_bwd"):
    nbr = len(DILATED)

    def body(db_ref, bkt_ref, out_ref):
        r = pl.program_id(0)

        @pl.when(r == 0)
        def _():
            out_ref[...] = jnp.zeros_like(out_ref)

        bkt = bkt_ref[...]
        rowi = lax.broadcasted_iota(jnp.int32, (REL_BUCKETS, LANES), 0)
        coli = lax.broadcasted_iota(jnp.int32, (REL_BUCKETS, LANES), 1)
        acc = jnp.zeros((REL_BUCKETS, LANES), F32)
        for h in range(ATT_HEADS):
            x = db_ref[h]
            for bb in range(REL_BUCKETS):
                part = jnp.sum(jnp.where(bkt == bb, x, 0.0), axis=0, keepdims=True)
                tot = jnp.sum(part, axis=1, keepdims=True)
                acc = acc + jnp.where((rowi == bb) & (coli == h), tot, 0.0)
        out_ref[...] += acc

    return pl.pallas_call(
        body, name=name, grid=(nbr,),
        in_specs=[pl.BlockSpec((None, ATT_HEADS, BLK, 2 * BLK), lambda r: (r, 0, 0, 0)),
                  pl.BlockSpec((None, BLK, 2 * BLK), lambda r: (r, 0, 0))],
        out_specs=pl.BlockSpec((REL_BUCKETS, LANES), lambda r: (0, 0)),
        out_shape=jax.ShapeDtypeStruct((REL_BUCKETS, LANES), F32),
        compiler_params=_params("arbitrary"),
    )(dbias, buckets)


def _stack_heads(pair, lo):
    return jnp.concatenate([jnp.where(lo, pair, 0.0), jnp.where(lo, 0.0, pair)], axis=0)


def _head_cols(pair, lo, reduce):
    fill = -jnp.inf if reduce is jnp.max else 0.0
    return jnp.concatenate([reduce(jnp.where(lo, pair, fill), axis=1, keepdims=True),
                            reduce(jnp.where(lo, fill, pair), axis=1, keepdims=True)], axis=0)


def _unstack_heads(x2, lo):
    return jnp.where(lo, x2[:BLK], x2[BLK:])


def _att_scores(q2, kk, bias2, first_ok):
    sc = _dot(q2, kk, NT) * (64 ** -0.5) + bias2
    return jnp.where(first_ok, sc, NEG)


DIL_TILE = 2048
DIL_COLS = ATT_W // LANES
DIL_GROUP = 4


def _dil_rows(dil, n, r, base=0):
    start = base + n * (BLK * dil) + r
    return pl.ds(start, BLK, stride=dil) if dil > 1 else pl.ds(start, BLK)


def _dil_in_specs(tile_of):
    cur = lambda col: pl.BlockSpec((DIL_TILE, LANES), lambda p, i: (tile_of(i), col * DIL_COLS + p))
    prev = lambda col: pl.BlockSpec((DIL_TILE, LANES), lambda p, i: (jnp.maximum(tile_of(i) - 1, 0), col * DIL_COLS + p))
    bias = pl.BlockSpec((len(DILATED), None, 2 * BLK, 2 * BLK), lambda p, i: (0, p, 0, 0))
    return [cur(0), prev(1), cur(1), prev(2), cur(2), bias]


def _pair_bias(biasm):
    return biasm.reshape(len(DILATED), DIL_COLS, 2 * BLK, 2 * BLK)


def _dil_fwd(proj, biasm, name="dil_fwd", gather=()):
    s = proj.shape[0]
    nt = s // DIL_TILE
    tt = DIL_TILE

    def body(q_ref, kp_ref, kc_ref, vp_ref, vc_ref, bias_ref, att_ref, lse_ref, k2, v2, ob, lb):
        t = pl.program_id(1)
        k2[0:tt, :] = kp_ref[...]
        k2[tt:2 * tt, :] = kc_ref[...]
        v2[0:tt, :] = vp_ref[...]
        v2[tt:2 * tt, :] = vc_ref[...]
        lo = lax.broadcasted_iota(jnp.int32, (BLK, LANES), 1) < 64
        kidx = lax.broadcasted_iota(jnp.int32, (2 * BLK, 2 * BLK), 1)
        for b, (_, dil) in enumerate(DILATED):
            for j0 in range(0, tt // BLK, DIL_GROUP):
                grp = range(DIL_GROUP)
                rn = [((j0 + i) % dil, (j0 + i) // dil) for i in grp]
                here = [_dil_rows(dil, n, r) for r, n in rn]
                cur = [_dil_rows(dil, n, r, tt) for r, n in rn]
                prev = [_dil_rows(dil, n - 1, r, tt) for r, n in rn]
                q2 = [_stack_heads(q_ref[here[i], :], lo).astype(BF16) for i in grp]
                kk = [jnp.concatenate([k2[prev[i], :], k2[cur[i], :]], axis=0).astype(BF16) for i in grp]
                vv = [jnp.concatenate([v2[prev[i], :], v2[cur[i], :]], axis=0).astype(BF16) for i in grp]
                sc = [_att_scores(q2[i], kk[i], bias_ref[b], (t > 0) | (rn[i][1] > 0) | (kidx >= BLK)) for i in grp]
                mx = [jnp.max(sc[i], axis=1, keepdims=True) for i in grp]
                pe = [jnp.exp(sc[i] - mx[i]) for i in grp]
                l = [jnp.sum(pe[i], axis=1, keepdims=True) for i in grp]
                o2 = [_dot(pe[i].astype(BF16), vv[i], NN) for i in grp]
                for i in grp:
                    ob.at[b][here[i], :] = _unstack_heads(o2[i] / l[i], lo)
                    lb.at[b][here[i], :] = _unstack_heads(jnp.broadcast_to(mx[i] + jnp.log(l[i]), (2 * BLK, LANES)), lo)
        l0, l1, l2 = lb[0], lb[1], lb[2]
        mx = jnp.maximum(jnp.maximum(l0, l1), l2)
        e0, e1, e2 = jnp.exp(l0 - mx), jnp.exp(l1 - mx), jnp.exp(l2 - mx)
        tot = e0 + e1 + e2
        att_ref[...] = (e0 * ob[0] + e1 * ob[1] + e2 * ob[2]) / tot
        lse_ref[...] = mx + jnp.log(tot)

    out = pl.BlockSpec((tt, LANES), lambda p, i: (i, p))
    return _call(
        body, name=name, grid=(DIL_COLS, nt), in_specs=_dil_in_specs(lambda i: i), out_specs=[out, out],
        out_shape=[jax.ShapeDtypeStruct((s, ATT_W), F32)] * 2, args=(proj, proj, proj, proj, proj, _pair_bias(biasm)),
        scratch_shapes=[pltpu.VMEM((2 * tt, LANES), F32), pltpu.VMEM((2 * tt, LANES), F32),
                        pltpu.VMEM((len(DILATED), tt, LANES), F32), pltpu.VMEM((len(DILATED), tt, LANES), F32)],
        sem=("parallel", "parallel"), gather=gather)


def _dil_bwd(proj, biasm, lse, att, dcat, name="dil_bwd"):
    s = proj.shape[0]
    nt = s // DIL_TILE
    tt = DIL_TILE
    nbr = len(DILATED)

    def body(q_ref, kp_ref, kc_ref, vp_ref, vc_ref, bias_ref, lse_ref, att_ref, datt_ref,
             dq_ref, dk_ref, dv_ref, dbias_ref, k2, v2, dqa, dka, dva, kcar, vcar):
        i = pl.program_id(1)
        t = nt - 1 - i
        k2[0:tt, :] = kp_ref[...]
        k2[tt:2 * tt, :] = kc_ref[...]
        v2[0:tt, :] = vp_ref[...]
        v2[tt:2 * tt, :] = vc_ref[...]

        @pl.when(i == 0)
        def _():
            kcar[...] = jnp.zeros_like(kcar)
            vcar[...] = jnp.zeros_like(vcar)
            dbias_ref[...] = jnp.zeros_like(dbias_ref)

        dqa[...] = jnp.zeros_like(dqa)
        dka[0:tt, :] = jnp.zeros((tt, LANES), F32)
        dva[0:tt, :] = jnp.zeros((tt, LANES), F32)
        dka[tt:2 * tt, :] = kcar[...]
        dva[tt:2 * tt, :] = vcar[...]
        lo = lax.broadcasted_iota(jnp.int32, (BLK, LANES), 1) < 64
        kidx = lax.broadcasted_iota(jnp.int32, (2 * BLK, 2 * BLK), 1)
        for b, (_, dil) in enumerate(DILATED):
            for j0 in range(0, tt // BLK, DIL_GROUP):
                grp = range(DIL_GROUP)
                rn = [((j0 + i) % dil, (j0 + i) // dil) for i in grp]
                here = [_dil_rows(dil, n, r) for r, n in rn]
                cur = [_dil_rows(dil, n, r, tt) for r, n in rn]
                prev = [_dil_rows(dil, n - 1, r, tt) for r, n in rn]
                dat = [datt_ref[here[i], :] for i in grp]
                q2 = [_stack_heads(q_ref[here[i], :], lo).astype(BF16) for i in grp]
                dom = [_stack_heads(dat[i], lo).astype(BF16) for i in grp]
                kk = [jnp.concatenate([k2[prev[i], :], k2[cur[i], :]], axis=0).astype(BF16) for i in grp]
                vv = [jnp.concatenate([v2[prev[i], :], v2[cur[i], :]], axis=0).astype(BF16) for i in grp]
                sc = [_att_scores(q2[i], kk[i], bias_ref[b], (t > 0) | (rn[i][1] > 0) | (kidx >= BLK)) for i in grp]
                dp = [_dot(dom[i], vv[i], NT) for i in grp]
                pr = [jnp.exp(sc[i] - _head_cols(lse_ref[here[i], :], lo, jnp.max)) for i in grp]
                ds = [pr[i] * (dp[i] - _head_cols(dat[i] * att_ref[here[i], :], lo, jnp.sum)) for i in grp]
                dsb = [(ds[i] * (64 ** -0.5)).astype(BF16) for i in grp]
                dq2 = [_dot(dsb[i], kk[i], NN) for i in grp]
                dk2 = [_dot(dsb[i], q2[i], TN) for i in grp]
                dv2 = [_dot(pr[i].astype(BF16), dom[i], TN) for i in grp]
                for i in grp:
                    dbias_ref[b] += ds[i]
                    dqa[here[i], :] += _unstack_heads(dq2[i], lo)
                    dka[prev[i], :] += dk2[i][:BLK]
                    dka[cur[i], :] += dk2[i][BLK:]
                    dva[prev[i], :] += dv2[i][:BLK]
                    dva[cur[i], :] += dv2[i][BLK:]
        dq_ref[...] = dqa[...].astype(BF16)
        dk_ref[...] = dka[tt:2 * tt, :].astype(BF16)
        dv_ref[...] = dva[tt:2 * tt, :].astype(BF16)
        kcar[...] = dka[0:tt, :]
        vcar[...] = dva[0:tt, :]

    rev = lambda i: nt - 1 - i
    out = pl.BlockSpec((tt, LANES), lambda p, i: (rev(i), p))
    two = lambda: pltpu.VMEM((2 * tt, LANES), F32)
    one = lambda: pltpu.VMEM((tt, LANES), F32)
    return pl.pallas_call(
        body, name=name, grid=(DIL_COLS, nt),
        in_specs=_dil_in_specs(rev) + [out, out, out],
        out_specs=[out, out, out, pl.BlockSpec((nbr, None, 2 * BLK, 2 * BLK), lambda p, i: (0, p, 0, 0))],
        out_shape=[jax.ShapeDtypeStruct((s, ATT_W), BF16)] * 3
        + [jax.ShapeDtypeStruct((nbr, DIL_COLS, 2 * BLK, 2 * BLK), F32)],
        scratch_shapes=[two(), two(), one(), two(), two(), one(), one()],
        compiler_params=_params("arbitrary", "arbitrary"),
    )(proj, proj, proj, proj, proj, _pair_bias(biasm), lse, att, dcat)


QK_COL0 = (3 * ATT_W) // ATT_W


def _conv_shifted(prev, cur, j, row):
    sh = CONV_K - 1 - j
    if sh == 0:
        return cur
    return jnp.where(row < sh, pltpu.roll(prev, sh, 0), pltpu.roll(cur, sh, 0))


def _conv_z(prev, cur, w_ref, b_ref, row):
    z = b_ref[...] + cur * w_ref[CONV_K - 1:CONV_K, :]
    for j in range(CONV_K - 1):
        z = z + _conv_shifted(prev, cur, j, row) * w_ref[j:j + 1, :]
    return z


def _conv_fwd(proj, conv_w, conv_b, name="conv_fwd", tm=512):
    s = proj.shape[0]
    w = ATT_W

    def body(prev_ref, cur_ref, w_ref, b_ref, o_ref):
        i = pl.program_id(1)
        row = lax.broadcasted_iota(jnp.int32, (tm, w), 0)
        prev = jnp.where(i > 0, prev_ref[...], 0.0)
        z = _conv_z(prev, cur_ref[...], w_ref, b_ref, row)
        o_ref[...] = z * _sigmoid(z)

    return pl.pallas_call(
        body, name=name, grid=(2, s // tm),
        in_specs=[pl.BlockSpec((tm, w), lambda j, i: (jnp.maximum(i - 1, 0), QK_COL0 + j)),
                  pl.BlockSpec((tm, w), lambda j, i: (i, QK_COL0 + j)),
                  pl.BlockSpec((CONV_K, w), lambda j, i: (0, j)),
                  pl.BlockSpec((1, w), lambda j, i: (0, j))],
        out_specs=pl.BlockSpec((tm, w), lambda j, i: (i, j)),
        out_shape=jax.ShapeDtypeStruct((s, 2 * ML_W), F32),
        compiler_params=_params("parallel", "parallel"),
    )(proj, proj, conv_w, conv_b)


def _conv_bwd(proj, dqk, conv_w, conv_b, name="conv_bwd", tm=512):
    s = proj.shape[0]
    w = ATT_W
    nt = s // tm

    def body(xp_ref, xc_ref, xn_ref, dc_ref, dn_ref, w_ref, b_ref, dx_ref, dw_ref, db_ref):
        i = pl.program_id(1)
        row = lax.broadcasted_iota(jnp.int32, (tm, w), 0)
        prev = jnp.where(i > 0, xp_ref[...], 0.0)
        cur = xc_ref[...]

        def dz_of(pv, cv, dy):
            z = _conv_z(pv, cv, w_ref, b_ref, row)
            sig = _sigmoid(z)
            return dy * (sig * (1.0 + z * (1.0 - sig)))

        dzc = dz_of(prev, cur, dc_ref[...])
        dzn = jnp.where(i < nt - 1, dz_of(cur, xn_ref[...], dn_ref[...]), 0.0)
        dx = dzc * w_ref[CONV_K - 1:CONV_K, :]
        for j in range(CONV_K - 1):
            sh = CONV_K - 1 - j
            up = jnp.where(row >= tm - sh, pltpu.roll(dzn, tm - sh, 0), pltpu.roll(dzc, tm - sh, 0))
            dx = dx + up * w_ref[j:j + 1, :]
        dx_ref[...] = dx

        @pl.when(i == 0)
        def _():
            dw_ref[...] = jnp.zeros_like(dw_ref)
            db_ref[...] = jnp.zeros_like(db_ref)

        for j in range(CONV_K):
            dw_ref[j:j + 1, :] += jnp.sum(dzc * _conv_shifted(prev, cur, j, row), axis=0, keepdims=True)
        db_ref[...] += jnp.sum(dzc, axis=0, keepdims=True)

    xs = lambda f: pl.BlockSpec((tm, w), lambda j, i: (f(i), QK_COL0 + j))
    ds = lambda f: pl.BlockSpec((tm, w), lambda j, i: (f(i), j))
    return pl.pallas_call(
        body, name=name, grid=(2, nt),
        in_specs=[xs(lambda i: jnp.maximum(i - 1, 0)), xs(lambda i: i), xs(lambda i: jnp.minimum(i + 1, nt - 1)),
                  ds(lambda i: i), ds(lambda i: jnp.minimum(i + 1, nt - 1)),
                  pl.BlockSpec((CONV_K, w), lambda j, i: (0, j)), pl.BlockSpec((1, w), lambda j, i: (0, j))],
        out_specs=[ds(lambda i: i), pl.BlockSpec((CONV_K, w), lambda j, i: (0, j)),
                   pl.BlockSpec((1, w), lambda j, i: (0, j))],
        out_shape=[jax.ShapeDtypeStruct((s, 2 * ML_W), F32), jax.ShapeDtypeStruct((CONV_K, 2 * ML_W), F32),
                   jax.ShapeDtypeStruct((1, 2 * ML_W), F32)],
        compiler_params=_params("parallel", "arbitrary"),
    )(proj, proj, proj, dqk, dqk, conv_w, conv_b)


def _bf16_mm(dims_fwd):
    @jax.custom_vjp
    def mm(a, b):
        return _dot(a.astype(BF16), b.astype(BF16), dims_fwd)

    def fwd(a, b):
        return mm(a, b), (a, b)

    def bwd(res, g):
        a, b = res
        if dims_fwd is NN:
            return _mm_nt(g, b), _mm_tn(a, g)
        if dims_fwd is NT:
            return _mm_nn(g, b), _mm_tn(g, a)
        return _mm_nt(b, g), _mm_nn(a, g)

    mm.defvjp(fwd, bwd)
    return mm


_mm_nn = _bf16_mm(NN)
_mm_nt = _bf16_mm(NT)
_mm_tn = _bf16_mm(TN)


def _tri(lower):
    r = lax.broadcasted_iota(jnp.int32, (CHUNK, CHUNK), 0)
    c = lax.broadcasted_iota(jnp.int32, (CHUNK, CHUNK), 1)
    return ((r >= c) if lower else (r <= c)).astype(F32)


@jax.custom_vjp
def _cumsum_rows(x):
    return lax.dot_general(_tri(True), x, NN, precision=lax.Precision.HIGHEST, preferred_element_type=F32)


def _cumsum_fwd(x):
    return _cumsum_rows(x), None


def _cumsum_bwd(_, g):
    return (lax.dot_general(_tri(False), g, NN, precision=lax.Precision.HIGHEST, preferred_element_type=F32),)


_cumsum_rows.defvjp(_cumsum_fwd, _cumsum_bwd)


def _abs(x):
    return jnp.where(x >= 0, x, -x)


def _log_sigmoid(x):
    return jnp.minimum(x, 0.0) - jnp.log(1.0 + jnp.exp(-_abs(x)))


def _pick_col(x, lane):
    sel = lax.broadcasted_iota(jnp.int32, x.shape, 1) == lane
    return jnp.sum(jnp.where(sel, x, 0.0), axis=1, keepdims=True)


def _pick_row(x, r):
    sel = lax.broadcasted_iota(jnp.int32, x.shape, 0) == r
    return jnp.sum(jnp.where(sel, x, 0.0), axis=0, keepdims=True)


def _mlstm_chunk(qs, ks, vs, oms, gates, gate_bias, mlg, cs, ns, ms):
    gb = gates + gate_bias
    cum = _cumsum_rows(_log_sigmoid(gb))
    gbt = gb.T
    cumt = cum.T
    causal = lax.broadcasted_iota(jnp.int32, (CHUNK, CHUNK), 0) >= lax.broadcasted_iota(jnp.int32, (CHUNK, CHUNK), 1)
    hd = range(ML_HEADS)
    k = [ks[h] * (ML_HD ** -0.5) for h in hd]
    ig_col = [_pick_col(gb, h) for h in hd]
    ig_row = [_pick_row(gbt, h) for h in hd]
    b_col = [_pick_col(cum, ML_HEADS + h) for h in hd]
    b_row = [_pick_row(cumt, ML_HEADS + h) for h in hd]
    g = [_pick_row(b_col[h], CHUNK - 1) for h in hd]
    a = [g[h] - b_col[h] + ig_col[h] for h in hd]
    m_loc = [jnp.max(a[h], axis=0, keepdims=True) for h in hd]
    wa = [jnp.exp(a[h] - m_loc[h]) for h in hd]
    d_log = [jnp.where(causal, b_col[h] - b_row[h] + ig_row[h], -jnp.inf) for h in hd]
    e_log = [b_col[h] + ms[h] for h in hd]
    m_t = [jnp.maximum(e_log[h], jnp.max(d_log[h], axis=1, keepdims=True)) for h in hd]
    d_w = [jnp.exp(d_log[h] - m_t[h]) for h in hd]
    e_w = [jnp.exp(e_log[h] - m_t[h]) for h in hd]
    qk = [_mm_nt(qs[h], k[h]) for h in hd]
    qc = [_mm_nt(qs[h], cs[h]) for h in hd]
    c_loc = [_mm_tn(wa[h] * vs[h], k[h]) for h in hd]
    s_qk = [qk[h] * d_w[h] for h in hd]
    sv = [_mm_nn(s_qk[h], vs[h]) for h in hd]
    n_loc = [jnp.sum(wa[h] * k[h], axis=0, keepdims=True) for h in hd]
    m_out = [jnp.maximum(g[h] + ms[h], m_loc[h]) for h in hd]
    sp = [jnp.exp(g[h] + ms[h] - m_out[h]) for h in hd]
    sl = [jnp.exp(m_loc[h] - m_out[h]) for h in hd]
    c_out = [sp[h] * cs[h] + sl[h] * c_loc[h] for h in hd]
    n_out = [sp[h] * ns[h] + sl[h] * n_loc[h] for h in hd]
    num = [e_w[h] * qc[h] + sv[h] for h in hd]
    den = [e_w[h] * jnp.sum(qs[h] * ns[h], axis=1, keepdims=True) + jnp.sum(s_qk[h], axis=1, keepdims=True) for h in hd]
    hg = [_sigmoid(oms[h]) * (num[h] / jnp.maximum(_abs(den[h]), jnp.exp(-m_t[h]))) for h in hd]
    mu = [jnp.mean(hg[h], axis=1, keepdims=True) for h in hd]
    hc = [hg[h] - mu[h] for h in hd]
    var = [jnp.mean(hc[h] * hc[h], axis=1, keepdims=True) for h in hd]
    ys = [hc[h] * lax.rsqrt(var[h] + LN_EPS) * mlg[h] for h in hd]
    return ys, c_out, n_out, m_out


V_COL = 5
O_COL = 6
ML_SUB = 1


def _mlstm_fwd(qk, proj, gates, gate_bias, mlg, name="mlstm_fwd", gather=()):
    s = qk.shape[0]
    nc = s // CHUNK

    def body(q_ref, k_ref, v_ref, o_ref, g_ref, gb_ref, mlg_ref, y_ref, cp_ref, np_ref, mp_ref, c_s, n_s, m_s):
        ci = pl.program_id(0)

        @pl.when(ci == 0)
        def _():
            c_s[...] = jnp.zeros_like(c_s)
            n_s[...] = jnp.zeros_like(n_s)
            m_s[...] = jnp.zeros_like(m_s)

        for sub in range(ML_SUB):
            rows = slice(CHUNK * sub, CHUNK * (sub + 1))
            hs = lambda ref: [ref[rows, LANES * h:LANES * (h + 1)] for h in range(ML_HEADS)]
            cp_ref[sub] = c_s[...]
            np_ref[sub] = n_s[...]
            mp_ref[sub] = m_s[...]
            ys, c_new, n_new, m_new = _mlstm_chunk(
                hs(q_ref), hs(k_ref), hs(v_ref), hs(o_ref), g_ref[rows, :], gb_ref[...],
                [mlg_ref[:, LANES * h:LANES * (h + 1)] for h in range(ML_HEADS)],
                [c_s[h] for h in range(ML_HEADS)], [n_s[h:h + 1, :] for h in range(ML_HEADS)],
                [m_s[h:h + 1, 0:1] for h in range(ML_HEADS)])
            for h in range(ML_HEADS):
                y_ref[rows, LANES * h:LANES * (h + 1)] = ys[h]
                c_s[h] = c_new[h]
                n_s[h:h + 1, :] = n_new[h]
                m_s[h:h + 1, :] = jnp.broadcast_to(m_new[h], (1, LANES))

    blk = lambda col: pl.BlockSpec((ML_SUB * CHUNK, ML_W), lambda ci: (ci, col))
    vec = lambda w: pl.BlockSpec((1, w), lambda ci: (0, 0))
    return _call(
        body, name=name, grid=(nc // ML_SUB,), args=(qk, qk, proj, proj, gates, gate_bias, mlg), sem=("arbitrary",),
        gather=gather,
        in_specs=[blk(0), blk(1), blk(V_COL), blk(O_COL), pl.BlockSpec((ML_SUB * CHUNK, LANES), lambda ci: (ci, 0)),
                  vec(LANES), vec(ML_W)],
        out_specs=[blk(0), pl.BlockSpec((ML_SUB, ML_HEADS, ML_HD, ML_HD), lambda ci: (ci, 0, 0, 0)),
                   pl.BlockSpec((ML_SUB, 8, LANES), lambda ci: (ci, 0, 0)),
                   pl.BlockSpec((ML_SUB, 8, LANES), lambda ci: (ci, 0, 0))],
        out_shape=[jax.ShapeDtypeStruct((s, ML_W), F32), jax.ShapeDtypeStruct((nc, ML_HEADS, ML_HD, ML_HD), F32),
                   jax.ShapeDtypeStruct((nc, 8, LANES), F32), jax.ShapeDtypeStruct((nc, 8, LANES), F32)],
        scratch_shapes=[pltpu.VMEM((ML_HEADS, ML_HD, ML_HD), F32), pltpu.VMEM((8, LANES), F32),
                        pltpu.VMEM((8, LANES), F32)])


def _mlstm_bwd(qk, proj, gates, gate_bias, mlg, cprev, nprev, mprev, dy, name="mlstm_bwd", exchange=()):
    s = qk.shape[0]
    nc = s // CHUNK

    def body(q_ref, k_ref, v_ref, o_ref, g_ref, gb_ref, mlg_ref, cp_ref, np_ref, mp_ref, dy_ref,
             dqk_ref, dv_ref, do_ref, dg_ref, dgb_ref, dmlg_ref, dc_s, dn_s, dm_s, gb8, mg8):
        ci = pl.program_id(0)

        @pl.when(ci == 0)
        def _():
            dc_s[...] = jnp.zeros_like(dc_s)
            dn_s[...] = jnp.zeros_like(dn_s)
            dm_s[...] = jnp.zeros_like(dm_s)
            gb8[...] = jnp.zeros_like(gb8)
            mg8[...] = jnp.zeros_like(mg8)

        for sub in reversed(range(ML_SUB)):
            rows = slice(CHUNK * sub, CHUNK * (sub + 1))
            hs = lambda ref: [ref[rows, LANES * h:LANES * (h + 1)] for h in range(ML_HEADS)]
            prim = (hs(q_ref), hs(k_ref), hs(v_ref), hs(o_ref), g_ref[rows, :], gb_ref[...],
                    [mlg_ref[:, LANES * h:LANES * (h + 1)] for h in range(ML_HEADS)],
                    [cp_ref[sub, h] for h in range(ML_HEADS)], [np_ref[sub, h:h + 1, :] for h in range(ML_HEADS)],
                    [mp_ref[sub, h:h + 1, 0:1] for h in range(ML_HEADS)])
            _, vjp = jax.vjp(_mlstm_chunk, *prim)
            cot = (hs(dy_ref), [dc_s[h] for h in range(ML_HEADS)], [dn_s[h:h + 1, :] for h in range(ML_HEADS)],
                   [dm_s[h:h + 1, 0:1] for h in range(ML_HEADS)])
            dqs, dks, dvs, dos, dg, dgb, dmlg, dcs, dns, dms = vjp(cot)
            dg_ref[rows, :] = dg
            gb8[0:1, :] += dgb
            for h in range(ML_HEADS):
                sl = slice(LANES * h, LANES * (h + 1))
                dqk_ref[rows, sl] = dqs[h]
                dqk_ref[rows, ML_W + LANES * h:ML_W + LANES * (h + 1)] = dks[h]
                dv_ref[rows, sl] = dvs[h]
                do_ref[rows, sl] = dos[h]
                mg8[0:1, sl] += dmlg[h]
                dc_s[h] = dcs[h]
                dn_s[h:h + 1, :] = dns[h]
                dm_s[h:h + 1, :] = jnp.broadcast_to(dms[h], (1, LANES))

        @pl.when(ci == nb - 1)
        def _():
            dgb_ref[...] = gb8[0:1, :]
            dmlg_ref[...] = mg8[0:1, :]

    nb = nc // ML_SUB
    rev = lambda ci: nb - 1 - ci
    blk = lambda col: pl.BlockSpec((ML_SUB * CHUNK, ML_W), lambda ci: (rev(ci), col))
    vec = lambda w: pl.BlockSpec((1, w), lambda ci: (0, 0))
    st8 = pl.BlockSpec((ML_SUB, 8, LANES), lambda ci: (rev(ci), 0, 0))
    gsp = pl.BlockSpec((ML_SUB * CHUNK, LANES), lambda ci: (rev(ci), 0))
    return _call(
        body, name=name, grid=(nb,), sem=("arbitrary",), exchange=exchange,
        args=(qk, qk, proj, proj, gates, gate_bias, mlg, cprev, nprev, mprev, dy),
        in_specs=[blk(0), blk(1), blk(V_COL), blk(O_COL), gsp, vec(LANES), vec(ML_W),
                  pl.BlockSpec((ML_SUB, ML_HEADS, ML_HD, ML_HD), lambda ci: (rev(ci), 0, 0, 0)), st8, st8, blk(1)],
        out_specs=[pl.BlockSpec((ML_SUB * CHUNK, 2 * ML_W), lambda ci: (rev(ci), 0)), blk(0), blk(0), gsp, vec(LANES),
                   vec(ML_W)],
        out_shape=[jax.ShapeDtypeStruct((s, 2 * ML_W), F32),
                   jax.ShapeDtypeStruct((s, ML_W), F32), jax.ShapeDtypeStruct((s, ML_W), F32),
                   jax.ShapeDtypeStruct((s, LANES), F32), jax.ShapeDtypeStruct((1, LANES), F32),
                   jax.ShapeDtypeStruct((1, ML_W), F32)],
        scratch_shapes=[pltpu.VMEM((ML_HEADS, ML_HD, ML_HD), F32), pltpu.VMEM((8, LANES), F32),
                        pltpu.VMEM((8, LANES), F32), pltpu.VMEM((8, LANES), F32), pltpu.VMEM((8, ML_W), F32)])


def _xattn_tile(qs, ks, vs):
    hd = range(XA_HEADS)
    sc = [_mm_nt(qs[h], ks[h]) * (XA_HD ** -0.5) for h in hd]
    mx = [lax.stop_gradient(jnp.max(sc[h], axis=1, keepdims=True)) for h in hd]
    pe = [jnp.exp(sc[h] - mx[h]) for h in hd]
    pn = [pe[h] / jnp.sum(pe[h], axis=1, keepdims=True) for h in hd]
    return [_mm_nn(pn[h], vs[h]) for h in hd]


def _xa_heads(ref):
    return [ref[:, XA_HD * h:XA_HD * (h + 1)] for h in range(XA_HEADS)]


def _xattn_fwd(q, kv, name="xattn_fwd", tm=512):
    s, d = q.shape

    def body(q_ref, k_ref, v_ref, o_ref):
        outs = _xattn_tile(_xa_heads(q_ref), _xa_heads(k_ref), _xa_heads(v_ref))
        for h in range(XA_HEADS):
            o_ref[:, XA_HD * h:XA_HD * (h + 1)] = outs[h]

    row = pl.BlockSpec((tm, d), lambda i: (i, 0))
    return pl.pallas_call(
        body, name=name, grid=(s // tm,),
        in_specs=[row, pl.BlockSpec((MEM_LEN, d), lambda i: (0, 0)), pl.BlockSpec((MEM_LEN, d), lambda i: (0, 1))],
        out_specs=row, out_shape=jax.ShapeDtypeStruct((s, d), F32),
        compiler_params=_params("parallel"),
    )(q, kv, kv)


def _xattn_bwd(q, kv, do, name="xattn_bwd", tm=512):
    s, d = q.shape

    def body(q_ref, k_ref, v_ref, do_ref, dq_ref, dkv_ref):
        i = pl.program_id(0)
        _, vjp = jax.vjp(_xattn_tile, _xa_heads(q_ref), _xa_heads(k_ref), _xa_heads(v_ref))
        dqs, dks, dvs = vjp(_xa_heads(do_ref))

        @pl.when(i == 0)
        def _():
            dkv_ref[...] = jnp.zeros_like(dkv_ref)

        for h in range(XA_HEADS):
            sl = slice(XA_HD * h, XA_HD * (h + 1))
            dq_ref[:, sl] = dqs[h]
            dkv_ref[:, sl] += dks[h]
            dkv_ref[:, d + XA_HD * h:d + XA_HD * (h + 1)] += dvs[h]

    row = pl.BlockSpec((tm, d), lambda i: (i, 0))
    return pl.pallas_call(
        body, name=name, grid=(s // tm,),
        in_specs=[row, pl.BlockSpec((MEM_LEN, d), lambda i: (0, 0)), pl.BlockSpec((MEM_LEN, d), lambda i: (0, 1)), row],
        out_specs=[row, pl.BlockSpec((MEM_LEN, 2 * d), lambda i: (0, 0))],
        out_shape=[jax.ShapeDtypeStruct((s, d), F32), jax.ShapeDtypeStruct((MEM_LEN, 2 * d), F32)],
        compiler_params=_params("arbitrary"),
    )(q, kv, kv, do)


def _loss_head(y, target, name="loss_head", tm=1024):
    s, d = y.shape
    nt = s // tm

    def body(y_ref, t_ref, dy_ref, loss_ref, acc):
        i = pl.program_id(0)
        err = y_ref[...] - t_ref[...]
        dy_ref[...] = err * (1.0 / d)

        @pl.when(i == 0)
        def _():
            acc[...] = jnp.zeros_like(acc)

        acc[...] += _rowsum8(err * err)

        @pl.when(i == nt - 1)
        def _():
            tot = jnp.sum(jnp.sum(acc[...], axis=0, keepdims=True), axis=1, keepdims=True)
            loss_ref[...] = jnp.broadcast_to(tot * (0.5 / d), (1, LANES))

    row = pl.BlockSpec((tm, d), lambda i: (i, 0))
    return pl.pallas_call(
        body, name=name, grid=(nt,),
        in_specs=[row, row], out_specs=[row, pl.BlockSpec((1, LANES), lambda i: (0, 0))],
        out_shape=[jax.ShapeDtypeStruct((s, d), F32), jax.ShapeDtypeStruct((1, LANES), F32)],
        scratch_shapes=[pltpu.VMEM((8, d), F32)],
        compiler_params=_params("arbitrary"),
    )(y, target)


def _adam2d(recv, w, m, v, name, layer=None):
    rows, cols = w.shape[-2:]
    fits = [t for t in range(16, rows + 1, 16) if rows % t == 0 and t * cols <= 128 * 1024]
    tr = max(fits) if fits else rows

    def body(r_ref, w_ref, m_ref, v_ref, g_ref, d_ref, mo_ref, vo_ref):
        g = r_ref[0].astype(F32)
        for j in range(1, N_DEV):
            g = g + r_ref[j].astype(F32)
        mn = ADAM_B1 * m_ref[...] + (1.0 - ADAM_B1) * g
        vn = ADAM_B2 * v_ref[...] + (1.0 - ADAM_B2) * jnp.square(g)
        m_hat = mn / (1.0 - ADAM_B1 ** ADAM_STEP)
        v_hat = vn / (1.0 - ADAM_B2 ** ADAM_STEP)
        g_ref[...] = g
        d_ref[...] = -ADAM_LR * (m_hat / (jnp.sqrt(v_hat) + ADAM_EPS) + ADAM_WD * w_ref[...])
        mo_ref[...] = mn
        vo_ref[...] = vn

    row = pl.BlockSpec((tr, cols), lambda i: (i, 0))
    if layer is None:
        wspec = row
    else:
        wspec = pl.BlockSpec((None, None, tr, cols), lambda i: (0, layer, i, 0))
    return pl.pallas_call(
        body, name=name, grid=(rows // tr,),
        in_specs=[pl.BlockSpec((N_DEV, tr, cols), lambda i: (0, i, 0)), wspec, wspec, wspec],
        out_specs=[row] * 4, out_shape=[jax.ShapeDtypeStruct((rows, cols), F32)] * 4,
        compiler_params=_params("parallel"),
    )(recv, w, m, v)


WEIGHTS = ("rel_bias", "ln_g", "ln_b", "ffn_w_gate", "ffn_w_up", "ffn_w_down", "w_in", "conv_w", "conv_b",
           "ig_bias", "fg_bias", "ml_norm_g", "w_out", "xq_w", "xkv_w", "xo_w")
SMALL = ("rel_bias", "ln_g", "ln_b", "conv_w", "conv_b", "ig_bias", "fg_bias", "ml_norm_g")
SMALL_SHAPES = {
    "rel_bias": (REL_BUCKETS, ATT_HEADS), "ln_g": (1, 4, LANES), "ln_b": (1, 4, LANES), "conv_w": (1, CONV_K, LANES),
    "conv_b": (1, 2 * ML_W), "ig_bias": (1, ML_HEADS), "fg_bias": (1, ML_HEADS), "ml_norm_g": (1, ML_W),
}
SMALL_ROWS = 8


def _pack_small(parts, lead=()):
    out = []
    for p in parts:
        p = jnp.pad(p, [(0, 0)] * len(lead) + [(0, SMALL_ROWS * LANES - p.shape[-1])])
        out.append(p.reshape(lead + (SMALL_ROWS, LANES)))
    return jnp.concatenate(out, axis=len(lead))


def _unpack_small(flat):
    out = {}
    for i, n in enumerate(SMALL):
        cnt = int(np.prod(SMALL_SHAPES[n]))
        out[n] = flat[SMALL_ROWS * i:SMALL_ROWS * (i + 1)].reshape(-1)[:cnt].reshape(SMALL_SHAPES[n])
    return out


def _split8(full, axis):
    shp = full.shape
    t = full.reshape(shp[:axis] + (N_DEV, shp[axis] // N_DEV) + shp[axis + 1:])
    return jnp.moveaxis(t, axis, 0).reshape(N_DEV, -1)


def _rep8(full):
    return jnp.broadcast_to(full.reshape(1, -1), (N_DEV, full.size))


def kernel(x, mem, rel_bias, ln_g, ln_b, ffn_w_gate, ffn_w_up, ffn_w_down, w_in, conv_w, conv_b, ig_bias, fg_bias, ml_norm_g, w_out, xq_w, xkv_w, xo_w, loss_target, m_rel_bias, m_ln_g, m_ln_b, m_ffn_w_gate, m_ffn_w_up, m_ffn_w_down, m_w_in, m_conv_w, m_conv_b, m_ig_bias, m_fg_bias, m_ml_norm_g, m_w_out, m_xq_w, m_xkv_w, m_xo_w, v_rel_bias, v_ln_g, v_ln_b, v_ffn_w_gate, v_ffn_w_up, v_ffn_w_down, v_w_in, v_conv_w, v_conv_b, v_ig_bias, v_fg_bias, v_ml_norm_g, v_w_out, v_xq_w, v_xkv_w, v_xo_w):
    w_tree = dict(rel_bias=rel_bias, ln_g=ln_g, ln_b=ln_b, ffn_w_gate=ffn_w_gate, ffn_w_up=ffn_w_up,
                  ffn_w_down=ffn_w_down, w_in=w_in, conv_w=conv_w, conv_b=conv_b, ig_bias=ig_bias, fg_bias=fg_bias,
                  ml_norm_g=ml_norm_g, w_out=w_out, xq_w=xq_w, xkv_w=xkv_w, xo_w=xo_w)
    m_tree = dict(rel_bias=m_rel_bias, ln_g=m_ln_g, ln_b=m_ln_b, ffn_w_gate=m_ffn_w_gate, ffn_w_up=m_ffn_w_up,
                  ffn_w_down=m_ffn_w_down, w_in=m_w_in, conv_w=m_conv_w, conv_b=m_conv_b, ig_bias=m_ig_bias,
                  fg_bias=m_fg_bias, ml_norm_g=m_ml_norm_g, w_out=m_w_out, xq_w=m_xq_w, xkv_w=m_xkv_w, xo_w=m_xo_w)
    v_tree = dict(rel_bias=v_rel_bias, ln_g=v_ln_g, ln_b=v_ln_b, ffn_w_gate=v_ffn_w_gate, ffn_w_up=v_ffn_w_up,
                  ffn_w_down=v_ffn_w_down, w_in=v_w_in, conv_w=v_conv_w, conv_b=v_conv_b, ig_bias=v_ig_bias,
                  fg_bias=v_fg_bias, ml_norm_g=v_ml_norm_g, w_out=v_w_out, xq_w=v_xq_w, xkv_w=v_xkv_w, xo_w=v_xo_w)
    x0 = x[0]
    pad_ff = FF_PAD - FF_SHARD
    bf = lambda t: t.astype(BF16)

    pad_rows = lambda t: jnp.pad(t, ((0, pad_ff), (0, 0)))
    ffn_shards = [(pad_rows(bf(ffn_w_gate[0, l]).T), pad_rows(bf(ffn_w_up[0, l]).T), pad_rows(bf(ffn_w_down[0, l])))
                  for l in range(2)]
    pairs = lambda t: t.reshape(N_PAIR, FF_PAIR, D_MODEL)
    w_in_shard = jnp.pad(bf(w_in[0]), ((0, 0), (0, ATT_W - W_IN_SHARD)))
    small_shard = jnp.concatenate([ln_g[0], ln_b[0], conv_w[0], jnp.zeros((4, LANES), F32)], axis=0)
    gate_bias = jnp.pad(jnp.concatenate([ig_bias, fg_bias], axis=1), ((0, 0), (0, LANES - 2 * ML_HEADS)))
    buckets = _bucket_tables()

    wg0, wu0, wd0, small_all = _gather_two_level("ffn1_weights_gather", ffn_shards[0] + (small_shard,))
    wg0, wu0, wd0 = pairs(wg0), pairs(wu0), pairs(wd0)
    unshard = lambda t: jnp.moveaxis(t, 0, 1).reshape(4, D_MODEL)
    ln_g_full, ln_b_full, conv_w_full = unshard(small_all[:, 0:4]), unshard(small_all[:, 4:8]), unshard(small_all[:, 8:12])
    lng = lambda i: ln_g_full[i:i + 1]
    lnb = lambda i: ln_b_full[i:i + 1]

    u0, x1, a0, b0, win_all, wout_all, xq_all, xo_all, xkv_all = _ffn_fwd(
        x0, wg0, wu0, wd0, lng(0), lnb(0), "ffn1_fwd",
        gather=(w_in_shard, bf(w_out[0]), bf(xq_w[0]), bf(xo_w[0]), bf(xkv_w[0])))
    w_in_full = jnp.moveaxis(win_all[:, :, :W_IN_SHARD], 0, 1).reshape(D_MODEL, W_IN)
    w_main = w_in_full[:, :W_IN_MAIN]
    w_gate_cols = jnp.pad(w_in_full[:, W_IN_MAIN:], ((0, 0), (0, LANES - 2 * ML_HEADS)))
    w_out_full = wout_all.reshape(D_MODEL, D_MODEL)
    xq_full = xq_all.reshape(D_MODEL, D_MODEL)
    xo_full = xo_all.reshape(D_MODEL, D_MODEL)

    proj, wg1 = _matmul(x1, w_main, "nn", "proj_fwd", tn=W_IN_MAIN // 2, tk=D_MODEL, gather=(ffn_shards[1][0],))
    gates, = _matmul(x1, w_gate_cols, "nn", "gates_fwd", tk=D_MODEL)
    biasm = _bias_fwd(rel_bias, buckets)
    att, lse, wd1 = _dil_fwd(proj, biasm, gather=(ffn_shards[1][2],))
    qk = _conv_fwd(proj, conv_w_full, conv_b)
    y_m, c_prev, n_prev, m_prev, wu1 = _mlstm_fwd(qk, proj, gates, gate_bias, ml_norm_g, gather=(ffn_shards[1][1],))
    u1, x2 = _matmul_resid_ln((att, y_m), w_out_full, x1, lng(1), lnb(1), "w_out_fwd")
    q_x, = _matmul(x2, xq_full, "nn", "xq_fwd", tn=D_MODEL, tk=D_MODEL)
    kv, = _matmul(mem[0], xkv_all, "nn", "xkv_fwd", tk=D_MODEL)
    o_x = _xattn_fwd(q_x, kv)
    u2, x3 = _matmul_resid_ln((o_x,), xo_full, x2, lng(2), lnb(2), "xo_fwd")
    wg1, wu1, wd1 = pairs(wg1), pairs(wu1), pairs(wd1)
    u3, x4, a3, b3 = _ffn_fwd(x3, wg1, wu1, wd1, lng(3), lnb(3), "ffn2_fwd")
    dx4, loss_row = _loss_head(x4, loss_target[0])

    dx3, xb, df, da, db, hh, dg3, db3 = _ffn_bwd_x(dx4, u3, x3, wg1, wu1, wd1, lng(3), a3, b3, "ffn2_bwd_x")
    ffn2_send = (_ffn_bwd_w(xb, da, "ffn2_bwd_wg", down=False)[0], _ffn_bwd_w(xb, db, "ffn2_bwd_wu", down=False)[0],
                 _ffn_bwd_w(df, hh, "ffn2_bwd_wd", down=True)[0])

    du2, dg2, db2 = _ln_bwd(dx3, u2, lng(2), "xattn_ln_bwd")
    do_x, = _matmul(du2, xo_full, "nt", "xo_bwd_x", tn=D_MODEL, tk=D_MODEL)
    g_xo, = _matmul(o_x, du2, "tn", "xo_bwd_w", tm=D_MODEL, tn=D_MODEL, out_dtype=BF16)
    dq_x, dkv = _xattn_bwd(q_x, kv, do_x)
    g_xq, = _matmul(x2, dq_x, "tn", "xq_bwd_w", tm=D_MODEL, tn=D_MODEL, out_dtype=BF16)
    g_xkv, = _matmul(mem[0], dkv, "tn", "xkv_bwd_w", tm=D_MODEL, tn=2 * D_MODEL // N_DEV, tk=MEM_LEN,
                     out_dtype=BF16, blocked_out=True)
    dx2, = _matmul(dq_x, xq_full, "nt", "xq_bwd_x", tn=D_MODEL, tk=D_MODEL, add=du2, add_scale=ALPHA)

    du1, dg1, db1 = _ln_bwd(dx2, u1, lng(1), "mixer_ln_bwd")
    dcat, = _matmul(du1, w_out_full, "nt", "w_out_bwd_x", tn=D_MODEL, tk=D_MODEL)
    g_w_out = jnp.concatenate(
        [_matmul(half, du1, "tn", f"w_out_bwd_w_{i}", tn=D_MODEL, out_dtype=BF16)[0] for i, half in enumerate((att, y_m))],
        axis=0)
    dqk, dv_m, do_m, dgates, dgate_bias, g_mlg, *ffn2_recv = _mlstm_bwd(
        qk, proj, gates, gate_bias, ml_norm_g, c_prev, n_prev, m_prev, dcat, exchange=tuple(ffn2_send))
    dqk_pre, g_conv_w, g_conv_b = _conv_bwd(proj, dqk, conv_w_full, conv_b)
    dq_a, dk_a, dv_a, dbias = _dil_bwd(proj, biasm, lse, att, dcat)
    g_rel = _bias_bwd(dbias.reshape(biasm.shape), buckets)[:, :ATT_HEADS]
    dproj = jnp.concatenate([dq_a, dk_a, dv_a, bf(dqk_pre), bf(dv_m), bf(do_m)], axis=1)
    g_w_main, = _matmul(x1, dproj, "tn", "proj_bwd_w", tm=D_MODEL, tn=W_IN_MAIN // 2, tk=1024, out_dtype=BF16)
    g_w_gates, = _matmul(x1, dgates, "tn", "gates_bwd_w", tm=D_MODEL, out_dtype=BF16)
    g_w_in = jnp.concatenate([g_w_main, g_w_gates[:, :2 * ML_HEADS]], axis=1)
    dx1, = _matmul(dproj, w_main, "nt", "proj_bwd_x", tn=D_MODEL, tk=W_IN_MAIN // 2, add=du1, add_scale=ALPHA)
    dx1, = _matmul(dgates, w_gate_cols, "nt", "gates_bwd_x", tn=D_MODEL, add=dx1)

    rows8 = lambda t: t.reshape(N_DEV, D_MODEL // N_DEV, D_MODEL)
    mid_send = (rows8(g_xo), rows8(g_xq), g_xkv, rows8(g_w_out),
                jnp.moveaxis(g_w_in.reshape(D_MODEL, N_DEV, W_IN_SHARD), 1, 0))
    dx0, xb, df, da, db, hh, dg0, db0, r_xo, r_xq, r_xkv, r_w_out, r_w_in = _ffn_bwd_x(
        dx1, u0, x0, wg0, wu0, wd0, lng(0), a0, b0, "ffn1_bwd_x", exchange=mid_send)
    small_blocks = {
        "rel_bias": _rep8(g_rel),
        "ln_g": _split8(jnp.concatenate([dg0, dg1, dg2, dg3], axis=0), 1),
        "ln_b": _split8(jnp.concatenate([db0, db1, db2, db3], axis=0), 1),
        "conv_w": _split8(g_conv_w, 1),
        "conv_b": _rep8(g_conv_b),
        "ig_bias": _rep8(dgate_bias[:, :ML_HEADS]),
        "fg_bias": _rep8(dgate_bias[:, ML_HEADS:2 * ML_HEADS]),
        "ml_norm_g": _rep8(g_mlg),
    }
    small_send = _pack_small([small_blocks[n] for n in SMALL], lead=(N_DEV,))
    g_wg, r_small = _ffn_bwd_w(xb, da, "ffn1_bwd_wg", down=False, exchange=(small_send,))
    g_wu, r_wg = _ffn_bwd_w(xb, db, "ffn1_bwd_wu", down=False, exchange=(g_wg,))
    g_wd, r_wu = _ffn_bwd_w(df, hh, "ffn1_bwd_wd", down=True, exchange=(g_wu,))
    r_wd, = _exchange_only("ffn1_grads_exchange", exchange=(g_wd,))
    ffn1_recv = [r_wg, r_wu, r_wd]

    res = {}
    for i, n in enumerate(("ffn_w_gate", "ffn_w_up", "ffn_w_down")):
        per_layer = [_adam2d(r[i], w_tree[n], m_tree[n], v_tree[n], f"adamw_{n}_{l}", layer=l)
                     for l, r in enumerate((ffn1_recv, ffn2_recv))]
        res[n] = [jnp.stack([per_layer[0][j], per_layer[1][j]])[None] for j in range(4)]
    for n, r in (("w_in", r_w_in), ("w_out", r_w_out), ("xq_w", r_xq), ("xkv_w", r_xkv), ("xo_w", r_xo)):
        res[n] = [t[None] for t in _adam2d(r, w_tree[n][0], m_tree[n][0], v_tree[n][0], f"adamw_{n}")]
    pack = lambda tree: _pack_small([tree[n].reshape(-1) for n in SMALL])
    small = [_unpack_small(t) for t in _adam2d(r_small, pack(w_tree), pack(m_tree), pack(v_tree), "adamw_small")]
    for n in SMALL:
        res[n] = [small[j][n] for j in range(4)]

    loss = lax.psum(loss_row[0, 0], ("x", "y", "c"))
    return (loss, dx0[None], *[res[n][0] for n in WEIGHTS], *[res[n][1] for n in WEIGHTS],
            *[res[n][2] for n in WEIGHTS], *[res[n][3] for n in WEIGHTS])
```

```python
import functools
import math

import numpy as np
import jax
import jax.numpy as jnp
from jax import lax
from jax.experimental import pallas as pl
from jax.experimental.pallas import tpu as pltpu

F32 = jnp.float32
BF16 = jnp.bfloat16

N_DEV = 8
D_MODEL = 1024
D_FF = 2816
FF_SHARD = D_FF // N_DEV
FF_PAD = 384
ATT_W = 512
ATT_HEADS = 8
DILATED = ((128, 1), (512, 4), (2048, 16))
BLK = 128
ML_W = 512
ML_HEADS = 4
ML_HD = 128
CHUNK = 128
CONV_K = 4
W_IN = 3592
W_IN_SHARD = W_IN // N_DEV
W_IN_MAIN = 3584
XA_HEADS = 4
XA_HD = 256
MEM_LEN = 256
REL_BUCKETS = 32
REL_MAX_DIST = 2048
ALPHA = 2.0 ** 0.25
LN_EPS = 1e-5
NEG = -1e30
ADAM_LR = 0.001
ADAM_B1 = 0.9
ADAM_B2 = 0.999
ADAM_EPS = 1e-08
ADAM_WD = 0.01
ADAM_STEP = 10
LANES = 128
VMEM_LIMIT = 58 * 1024 * 1024

NN = (((1,), (0,)), ((), ()))
NT = (((1,), (1,)), ((), ()))
TN = (((0,), (0,)), ((), ()))


def _dot(a, b, dims):
    return lax.dot_general(a, b, dims, preferred_element_type=F32)


def _params(*sem):
    return pltpu.CompilerParams(dimension_semantics=sem, vmem_limit_bytes=VMEM_LIMIT)


def _sigmoid(x):
    return 1.0 / (1.0 + jnp.exp(-x))


def _rowsum8(x):
    t, c = x.shape
    return jnp.sum(x.reshape(t // 8, 8, c), axis=0)


def _mesh_pos():
    x, y, c = lax.axis_index("x"), lax.axis_index("y"), lax.axis_index("c")
    return x, y, c, 4 * x + 2 * y + c


def _peer(x, y, c, k):
    px = 1 - x if k & 4 else x
    py = 1 - y if k & 2 else y
    pc = 1 - c if k & 1 else c
    return (px, py, pc), 4 * px + 2 * py + pc


def _call(body, *, name, grid, in_specs, out_specs, out_shape, args, scratch_shapes=(), sem=None,
          gather=(), exchange=()):
    in_specs, out_specs, out_shape, scratch = list(in_specs), list(out_specs), list(out_shape), list(scratch_shapes)
    ng, nc = len(gather), len(gather) + len(exchange)
    if nc == 0:
        return pl.pallas_call(body, name=name, grid=grid, in_specs=in_specs, out_specs=out_specs,
                              out_shape=out_shape, scratch_shapes=scratch, compiler_params=_params(*sem))(*args)
    n_in, n_out, n_scr = len(in_specs), len(out_specs), len(scratch)

    def wrapped(*refs):
        ins, cin = refs[:n_in], refs[n_in:n_in + nc]
        outs, cout = refs[n_in + nc:n_in + nc + n_out], refs[n_in + nc + n_out:n_in + 2 * nc + n_out]
        scr = refs[n_in + 2 * nc + n_out:n_in + 2 * nc + n_out + n_scr]
        send_sems, recv_sems, loc_sems = refs[-3:]
        first, last = None, None
        for ax, extent in enumerate(grid):
            f, l = pl.program_id(ax) == 0, pl.program_id(ax) == extent - 1
            first = f if first is None else first & f
            last = l if last is None else last & l

        def copies():
            x, y, c, me = _mesh_pos()
            out = []
            for a in range(nc):
                mine = cin[a] if a < ng else cin[a].at[me]
                out.append(pltpu.make_async_copy(mine, cout[a].at[me], loc_sems.at[a]))
                for k in range(1, N_DEV):
                    peer, pidx = _peer(x, y, c, k)
                    out.append(pltpu.make_async_remote_copy(
                        src_ref=cin[a] if a < ng else cin[a].at[pidx], dst_ref=cout[a].at[me],
                        send_sem=send_sems.at[a, k - 1], recv_sem=recv_sems.at[a, k - 1],
                        device_id=peer, device_id_type=pl.DeviceIdType.MESH))
            return out

        @pl.when(first)
        def _():
            for cp in copies():
                cp.start()

        body(*ins, *outs, *scr)

        @pl.when(last)
        def _():
            for cp in copies():
                cp.wait()

    hbm = pl.BlockSpec(memory_space=pl.ANY)
    comm_shapes = [jax.ShapeDtypeStruct((N_DEV,) + a.shape, a.dtype) for a in gather]
    comm_shapes += [jax.ShapeDtypeStruct(a.shape, a.dtype) for a in exchange]
    return pl.pallas_call(
        wrapped, name=name, grid=grid, in_specs=in_specs + [hbm] * nc, out_specs=out_specs + [hbm] * nc,
        out_shape=out_shape + comm_shapes,
        scratch_shapes=scratch + [pltpu.SemaphoreType.DMA((nc, N_DEV - 1)), pltpu.SemaphoreType.DMA((nc, N_DEV - 1)),
                                  pltpu.SemaphoreType.DMA((nc,))],
        compiler_params=_params(*(("arbitrary",) * len(grid))),
    )(*args, *gather, *exchange)


def _gather_two_level(name, arrays):
    na = len(arrays)

    def body(*refs):
        srcs, outs = refs[:na], refs[na:2 * na]
        send_sems, recv_sems, loc_sems = refs[2 * na:]
        x, y, c, me = _mesh_pos()
        here, sib = (x, y, c), (x, y, 1 - c)
        chips = [(1 - x, y), (x, 1 - y), (1 - x, 1 - y)]
        pos = lambda px, py, pc: 4 * px + 2 * py + pc

        def copy(a, k, block, to, src=None):
            return pltpu.make_async_remote_copy(
                src_ref=outs[a].at[block] if src is None else src, dst_ref=outs[a].at[block],
                send_sem=send_sems.at[a, k], recv_sem=recv_sems.at[a, k], device_id=to,
                device_id_type=pl.DeviceIdType.MESH)

        locs = [pltpu.make_async_copy(srcs[a], outs[a].at[me], loc_sems.at[a]) for a in range(na)]
        for cp in locs:
            cp.start()
        first = []
        for a in range(na):
            first.append(copy(a, 0, me, sib, src=srcs[a]))
            first += [copy(a, 1 + j, me, (*chip, c), src=srcs[a]) for j, chip in enumerate(chips)]
        for cp in first:
            cp.start()
        passed = []
        for a in range(na):
            for j, chip in enumerate(chips):
                copy(a, 1 + j, pos(*chip, c), here).wait_recv()
                passed.append(copy(a, 4 + j, pos(*chip, c), sib))
                passed[-1].start()
        for a in range(na):
            copy(a, 0, pos(x, y, 1 - c), here).wait_recv()
            for j, chip in enumerate(chips):
                copy(a, 4 + j, pos(*chip, 1 - c), here).wait_recv()
        for cp in first + passed:
            cp.wait_send()
        for cp in locs:
            cp.wait()

    hbm = pl.BlockSpec(memory_space=pl.ANY)
    return pl.pallas_call(
        body, name=name, in_specs=[hbm] * na, out_specs=[hbm] * na,
        out_shape=[jax.ShapeDtypeStruct((N_DEV,) + a.shape, a.dtype) for a in arrays],
        scratch_shapes=[pltpu.SemaphoreType.DMA((na, N_DEV - 1)), pltpu.SemaphoreType.DMA((na, N_DEV - 1)),
                        pltpu.SemaphoreType.DMA((na,))],
    )(*arrays)


def _exchange_only(name, gather=(), exchange=()):
    return _call(lambda: None, name=name, grid=(1,), in_specs=[], out_specs=[], out_shape=[], args=(),
                 gather=gather, exchange=exchange)


def _matmul(a, b, mode, name, *, out_dtype=F32, tm=1024, tn=512, tk=512, add=None, add_scale=1.0,
            blocked_out=False, gather=(), exchange=()):
    blocked_b = b.ndim == 3
    if blocked_b:
        (m, k), (nb, _, tn) = a.shape, b.shape
        n = nb * tn
    elif mode == "nn":
        (m, k), (_, n) = a.shape, b.shape
    elif mode == "nt":
        (m, k), (n, _) = a.shape, b.shape
    else:
        (k, m), (_, n) = a.shape, b.shape
    tm, tn, tk = min(tm, m), min(tn, n), min(tk, k)
    nk = k // tk
    dims = {"nn": NN, "nt": NT, "tn": TN}[mode]
    if mode == "tn":
        a_spec = pl.BlockSpec((tk, tm), lambda i, j, kk: (kk, i))
    else:
        a_spec = pl.BlockSpec((tm, tk), lambda i, j, kk: (i, kk))
    if blocked_b:
        b_spec = pl.BlockSpec((None, tk, tn), lambda i, j, kk: (j, kk, 0))
    elif mode == "nt":
        b_spec = pl.BlockSpec((tn, tk), lambda i, j, kk: (j, kk))
    else:
        b_spec = pl.BlockSpec((tk, tn), lambda i, j, kk: (kk, j))
    if blocked_out:
        o_spec = pl.BlockSpec((None, tm, tn), lambda i, j, kk: (j, i, 0))
        o_shape = jax.ShapeDtypeStruct((n // tn, m, tn), out_dtype)
    else:
        o_spec = pl.BlockSpec((tm, tn), lambda i, j, kk: (i, j))
        o_shape = jax.ShapeDtypeStruct((m, n), out_dtype)
    has_add = add is not None
    cache_a = nk == 1 and mode != "tn" and n // tn > 1 and a.dtype != BF16

    def body(*refs):
        if has_add:
            a_ref, b_ref, add_ref, o_ref, s_ref = refs
        else:
            a_ref, b_ref, o_ref, s_ref = refs
        kk = pl.program_id(2)
        if cache_a:
            @pl.when(pl.program_id(1) == 0)
            def _():
                s_ref[...] = a_ref[...].astype(BF16)

            lhs = s_ref[...]
        else:
            lhs = a_ref[...].astype(BF16)
        part = _dot(lhs, b_ref[...].astype(BF16), dims)

        def finish(r):
            if has_add:
                r = r + add_scale * add_ref[...]
            o_ref[...] = r.astype(out_dtype)

        if nk == 1:
            finish(part)
            return

        @pl.when(kk == 0)
        def _():
            s_ref[...] = part

        @pl.when(kk > 0)
        def _():
            s_ref[...] += part

        @pl.when(kk == nk - 1)
        def _():
            finish(s_ref[...])

    if nk > 1:
        scratch = [pltpu.VMEM((tm, tn), F32)]
    else:
        scratch = [pltpu.VMEM((tm, tk), BF16) if cache_a else pltpu.VMEM((8, LANES), F32)]
    return _call(
        body, name=name, grid=(m // tm, n // tn, nk),
        in_specs=[a_spec, b_spec] + ([pl.BlockSpec((tm, tn), lambda i, j, kk: (i, j))] if has_add else []),
        out_specs=[o_spec], out_shape=[o_shape], args=(a, b) + ((add,) if has_add else ()),
        scratch_shapes=scratch, sem=("parallel", "arbitrary", "arbitrary"),
        gather=gather, exchange=exchange)


def _ln_fwd_math(u, g, b):
    mu = jnp.mean(u, axis=-1, keepdims=True)
    uc = u - mu
    var = jnp.mean(uc * uc, axis=-1, keepdims=True)
    return uc * lax.rsqrt(var + LN_EPS) * g + b


def _ln_bwd_math(dy, u, g):
    mu = jnp.mean(u, axis=-1, keepdims=True)
    uc = u - mu
    var = jnp.mean(uc * uc, axis=-1, keepdims=True)
    rstd = lax.rsqrt(var + LN_EPS)
    xhat = uc * rstd
    dxh = dy * g
    m1 = jnp.mean(dxh, axis=-1, keepdims=True)
    m2 = jnp.mean(dxh * xhat, axis=-1, keepdims=True)
    return rstd * (dxh - m1 - xhat * m2), xhat


def _matmul_resid_ln(pieces, w, x, g, b, name, tm=1024):
    s = pieces[0].shape[0]
    k, d = w.shape
    widths = [p.shape[1] for p in pieces]

    def body(*refs):
        a_refs = refs[:len(pieces)]
        w_ref, x_ref, g_ref, b_ref, u_ref, y_ref = refs[len(pieces):]
        u = ALPHA * x_ref[...]
        lo = 0
        for a_ref, width in zip(a_refs, widths):
            u = u + _dot(a_ref[...].astype(BF16), w_ref[lo:lo + width, :], NN)
            lo += width
        u_ref[...] = u
        y_ref[...] = _ln_fwd_math(u, g_ref[...], b_ref[...])

    row = pl.BlockSpec((tm, d), lambda i: (i, 0))
    vec = pl.BlockSpec((1, d), lambda i: (0, 0))
    return pl.pallas_call(
        body, name=name, grid=(s // tm,),
        in_specs=[pl.BlockSpec((tm, width), lambda i: (i, 0)) for width in widths]
        + [pl.BlockSpec((k, d), lambda i: (0, 0)), row, vec, vec],
        out_specs=[row, row], out_shape=[jax.ShapeDtypeStruct((s, d), F32)] * 2,
        compiler_params=_params("parallel"),
    )(*pieces, w, x, g, b)


def _ln_bwd(dy, u, g, name, tm=1024):
    s, d = dy.shape
    nt = s // tm

    def body(dy_ref, u_ref, g_ref, du_ref, dg_ref, db_ref, g8, b8):
        i = pl.program_id(0)
        dy_ = dy_ref[...]
        du, xhat = _ln_bwd_math(dy_, u_ref[...], g_ref[...])
        du_ref[...] = du

        @pl.when(i == 0)
        def _():
            g8[...] = jnp.zeros_like(g8)
            b8[...] = jnp.zeros_like(b8)

        g8[...] += _rowsum8(dy_ * xhat)
        b8[...] += _rowsum8(dy_)

        @pl.when(i == nt - 1)
        def _():
            dg_ref[...] = jnp.sum(g8[...], axis=0, keepdims=True)
            db_ref[...] = jnp.sum(b8[...], axis=0, keepdims=True)

    row = pl.BlockSpec((tm, d), lambda i: (i, 0))
    vec = pl.BlockSpec((1, d), lambda i: (0, 0))
    return pl.pallas_call(
        body, name=name, grid=(nt,),
        in_specs=[row, row, vec], out_specs=[row, vec, vec],
        out_shape=[jax.ShapeDtypeStruct((s, d), F32), jax.ShapeDtypeStruct((1, d), F32),
                   jax.ShapeDtypeStruct((1, d), F32)],
        scratch_shapes=[pltpu.VMEM((8, d), F32), pltpu.VMEM((8, d), F32)],
        compiler_params=_params("arbitrary"),
    )(dy, u, g)


FF_PAIR = 2 * FF_PAD
N_PAIR = N_DEV // 2
FF_COLS = 256


def _ffn_fwd(x, wgt, wut, wd, g, b, name, tm=1024, gather=()):
    s, d = x.shape

    def body(x_ref, wg_ref, wu_ref, wd_ref, g_ref, b_ref, u_ref, y_ref, a_ref, bb_ref, xb, acc):
        k = pl.program_id(1)

        @pl.when(k == 0)
        def _():
            xb[...] = x_ref[...].astype(BF16)

        a = _dot(xb[...], wg_ref[...], NT)
        bb = _dot(xb[...], wu_ref[...], NT)
        a_ref[...] = a.astype(BF16)
        bb_ref[...] = bb.astype(BF16)
        h = (a * _sigmoid(a) * bb).astype(BF16)
        part = _dot(h, wd_ref[...], NN)

        @pl.when(k == 0)
        def _():
            acc[...] = part

        @pl.when(k > 0)
        def _():
            acc[...] += part

        @pl.when(k == N_PAIR - 1)
        def _():
            u = ALPHA * x_ref[...] + 0.5 * acc[...]
            u_ref[...] = u
            y_ref[...] = _ln_fwd_math(u, g_ref[...], b_ref[...])

    row = pl.BlockSpec((tm, d), lambda i, k: (i, 0))
    vec = pl.BlockSpec((1, d), lambda i, k: (0, 0))
    w_in = pl.BlockSpec((None, FF_PAIR, d), lambda i, k: (k, 0, 0))
    w_dn = w_in
    hid = pl.BlockSpec((tm, FF_PAIR), lambda i, k: (i, k))
    return _call(
        body, name=name, grid=(s // tm, N_PAIR),
        in_specs=[row, w_in, w_in, w_dn, vec, vec], out_specs=[row, row, hid, hid],
        out_shape=[jax.ShapeDtypeStruct((s, d), F32)] * 2 + [jax.ShapeDtypeStruct((s, N_DEV * FF_PAD), BF16)] * 2,
        args=(x, wgt, wut, wd, g, b),
        scratch_shapes=[pltpu.VMEM((tm, d), BF16), pltpu.VMEM((tm, d), F32)],
        sem=("parallel", "arbitrary"), gather=gather)


def _ffn_bwd_x(dy, u, x, wgt, wut, wd, g, a_fwd, b_fwd, name, tm=512, exchange=()):
    s, d = x.shape
    nt = s // tm
    ffp = N_DEV * FF_PAD

    def body(dy_ref, u_ref, x_ref, wg_ref, wu_ref, wd_ref, g_ref, a_ref, bb_ref,
             dx_ref, xb, df_ref, da_ref, db_ref, h_ref, dg_ref, dbl_ref,
             dfb, du_s, acc, g8, b8):
        i = pl.program_id(0)
        k = pl.program_id(1)

        @pl.when(k == 0)
        def _():
            dy_ = dy_ref[...]
            du, xhat = _ln_bwd_math(dy_, u_ref[...], g_ref[...])
            du_s[...] = du
            dfb[...] = (0.5 * du).astype(BF16)
            df_ref[...] = dfb[...]
            xb[...] = x_ref[...].astype(BF16)

            @pl.when(i == 0)
            def _():
                g8[...] = jnp.zeros_like(g8)
                b8[...] = jnp.zeros_like(b8)

            g8[...] += _rowsum8(dy_ * xhat)
            b8[...] += _rowsum8(dy_)

        dh_all = _dot(dfb[...], wd_ref[...], NT)

        def gate_grads(c):
            cs = slice(FF_COLS * c, FF_COLS * (c + 1))
            a = a_ref[:, cs].astype(F32)
            bb = bb_ref[:, cs].astype(F32)
            dh = dh_all[:, cs]
            sig = _sigmoid(a)
            sa = a * sig
            h_ref[:, cs] = (sa * bb).astype(BF16)
            da = (dh * bb * (sig * (1.0 + a * (1.0 - sig)))).astype(BF16)
            db = (dh * sa).astype(BF16)
            da_ref[:, cs] = da
            db_ref[:, cs] = db
            return da, db

        n_chunks = FF_PAIR // FF_COLS
        chunks = [gate_grads(0)]
        part = None
        for c in range(n_chunks):
            if c + 1 < n_chunks:
                chunks.append(gate_grads(c + 1))
            cs = slice(FF_COLS * c, FF_COLS * (c + 1))
            pc = _dot(chunks[c][0], wg_ref[cs, :], NN) + _dot(chunks[c][1], wu_ref[cs, :], NN)
            part = pc if part is None else part + pc

        @pl.when(k == 0)
        def _():
            acc[...] = part

        @pl.when(k > 0)
        def _():
            acc[...] += part

        @pl.when(k == N_PAIR - 1)
        def _():
            dx_ref[...] = ALPHA * du_s[...] + acc[...]

        @pl.when((k == N_PAIR - 1) & (i == nt - 1))
        def _():
            dg_ref[...] = jnp.sum(g8[...], axis=0, keepdims=True)
            dbl_ref[...] = jnp.sum(b8[...], axis=0, keepdims=True)

    row = pl.BlockSpec((tm, d), lambda i, k: (i, 0))
    vec = pl.BlockSpec((1, d), lambda i, k: (0, 0))
    w_in = pl.BlockSpec((None, FF_PAIR, d), lambda i, k: (k, 0, 0))
    hid = pl.BlockSpec((tm, FF_PAIR), lambda i, k: (i, k))
    return _call(
        body, name=name, grid=(nt, N_PAIR),
        in_specs=[row, row, row, w_in, w_in, w_in, vec, hid, hid],
        out_specs=[row, row, row, hid, hid, hid, vec, vec],
        out_shape=[jax.ShapeDtypeStruct((s, d), F32), jax.ShapeDtypeStruct((s, d), BF16),
                   jax.ShapeDtypeStruct((s, d), BF16),
                   jax.ShapeDtypeStruct((s, ffp), BF16), jax.ShapeDtypeStruct((s, ffp), BF16),
                   jax.ShapeDtypeStruct((s, ffp), BF16),
                   jax.ShapeDtypeStruct((1, d), F32), jax.ShapeDtypeStruct((1, d), F32)],
        args=(dy, u, x, wgt, wut, wd, g, a_fwd, b_fwd),
        scratch_shapes=[pltpu.VMEM((tm, d), BF16), pltpu.VMEM((tm, d), F32),
                        pltpu.VMEM((tm, d), F32), pltpu.VMEM((8, d), F32), pltpu.VMEM((8, d), F32)],
        sem=("arbitrary", "arbitrary"), exchange=exchange)


def _ffn_bwd_w(tok, hid, name, *, down, tm=2048, exchange=()):
    s, d = tok.shape
    nt = s // tm

    def body(t_ref, h_ref, dw_ref, acc):
        i = pl.program_id(1)
        part = _dot(h_ref[...], t_ref[...], TN) if down else _dot(t_ref[...], h_ref[...], TN)

        @pl.when(i == 0)
        def _():
            acc[...] = part

        @pl.when(i > 0)
        def _():
            acc[...] += part

        @pl.when(i == nt - 1)
        def _():
            for j in range(2):
                lo = j * FF_PAD
                dw_ref[j] = (acc[lo:lo + FF_SHARD, :] if down else acc[:, lo:lo + FF_SHARD]).astype(BF16)

    blk = (FF_SHARD, d) if down else (d, FF_SHARD)
    return _call(
        body, name=name, grid=(N_PAIR, nt),
        in_specs=[pl.BlockSpec((tm, d), lambda k, i: (i, 0)), pl.BlockSpec((tm, FF_PAIR), lambda k, i: (i, k))],
        out_specs=[pl.BlockSpec((2,) + blk, lambda k, i: (k, 0, 0))],
        out_shape=[jax.ShapeDtypeStruct((N_DEV,) + blk, BF16)], args=(tok, hid),
        scratch_shapes=[pltpu.VMEM((FF_PAIR, d) if down else (d, FF_PAIR), F32)],
        sem=("parallel", "arbitrary"), exchange=exchange)


def _bucket_tables():
    qi = np.arange(BLK)[:, None]
    ki = np.arange(2 * BLK)[None, :]
    off = qi + BLK - ki
    out = []
    for window, dil in DILATED:
        n_keys = window // dil
        dist = dil * np.clip(off, 0, n_keys)
        exact = REL_BUCKETS // 2
        df = np.maximum(dist, 1).astype(np.float32)
        large = exact + (np.log(df / np.float32(exact)) / np.float32(math.log(REL_MAX_DIST / exact))
                         * np.float32(REL_BUCKETS - exact)).astype(np.int32)
        large = np.minimum(large, REL_BUCKETS - 1)
        bucket = np.where(dist < exact, dist, large).astype(np.int32)
        band = (off >= 0) & (off <= n_keys)
        out.append(np.where(band, bucket, -1))
    return np.stack(out).astype(np.int32)


def _bias_fwd(rel_bias, buckets, name="bias_fwd"):
    def body(tbl_ref, bkt_ref, out_ref):
        bkt = bkt_ref[...]
        for h in range(ATT_HEADS):
            acc = jnp.full((BLK, 2 * BLK), NEG, F32)
            for bb in range(REL_BUCKETS):
                acc = jnp.where(bkt == bb, tbl_ref[bb, h], acc)
            out_ref[h] = acc

    nbr = len(DILATED)
    return pl.pallas_call(
        body, name=name, grid=(nbr,),
        in_specs=[pl.BlockSpec(memory_space=pltpu.SMEM),
                  pl.BlockSpec((None, BLK, 2 * BLK), lambda r: (r, 0, 0))],
        out_specs=pl.BlockSpec((None, ATT_HEADS, BLK, 2 * BLK), lambda r: (r, 0, 0, 0)),
        out_shape=jax.ShapeDtypeStruct((nbr, ATT_HEADS, BLK, 2 * BLK), F32),
        compiler_params=_params("parallel"),
    )(rel_bias, buckets)


def _bias_bwd(dbias, buckets, name="bias_bwd"):
    nbr = len(DILATED)

    def body(db_ref, bkt_ref, out_ref):
        r = pl.program_id(0)

        @pl.when(r == 0)
        def _():
            out_ref[...] = jnp.zeros_like(out_ref)

        bkt = bkt_ref[...]
        rowi = lax.broadcasted_iota(jnp.int32, (REL_BUCKETS, LANES), 0)
        coli = lax.broadcasted_iota(jnp.int32, (REL_BUCKETS, LANES), 1)
        acc = jnp.zeros((REL_BUCKETS, LANES), F32)
        for h in range(ATT_HEADS):
            x = db_ref[h]
            for bb in range(REL_BUCKETS):
                part = jnp.sum(jnp.where(bkt == bb, x, 0.0), axis=0, keepdims=True)
                tot = jnp.sum(part, axis=1, keepdims=True)
                acc = acc + jnp.where((rowi == bb) & (coli == h), tot, 0.0)
        out_ref[...] += acc

    return pl.pallas_call(
        body, name=name, grid=(nbr,),
        in_specs=[pl.BlockSpec((None, ATT_HEADS, BLK, 2 * BLK), lambda r: (r, 0, 0, 0)),
                  pl.BlockSpec((None, BLK, 2 * BLK), lambda r: (r, 0, 0))],
        out_specs=pl.BlockSpec((REL_BUCKETS, LANES), lambda r: (0, 0)),
        out_shape=jax.ShapeDtypeStruct((REL_BUCKETS, LANES), F32),
        compiler_params=_params("arbitrary"),
    )(dbias, buckets)


def _stack_heads(pair, lo):
    return jnp.concatenate([jnp.where(lo, pair, 0.0), jnp.where(lo, 0.0, pair)], axis=0)


def _head_cols(pair, lo, reduce):
    fill = -jnp.inf if reduce is jnp.max else 0.0
    return jnp.concatenate([reduce(jnp.where(lo, pair, fill), axis=1, keepdims=True),
                            reduce(jnp.where(lo, fill, pair), axis=1, keepdims=True)], axis=0)


def _unstack_heads(x2, lo):
    return jnp.where(lo, x2[:BLK], x2[BLK:])


def _att_scores(q2, kk, bias2, first_ok):
    sc = _dot(q2, kk, NT) * (64 ** -0.5) + bias2
    return jnp.where(first_ok, sc, NEG)


DIL_TILE = 2048
DIL_COLS = ATT_W // LANES
DIL_GROUP = 4


def _dil_rows(dil, n, r, base=0):
    start = base + n * (BLK * dil) + r
    return pl.ds(start, BLK, stride=dil) if dil > 1 else pl.ds(start, BLK)


def _dil_in_specs(tile_of):
    cur = lambda col: pl.BlockSpec((DIL_TILE, LANES), lambda p, i: (tile_of(i), col * DIL_COLS + p))
    prev = lambda col: pl.BlockSpec((DIL_TILE, LANES), lambda p, i: (jnp.maximum(tile_of(i) - 1, 0), col * DIL_COLS + p))
    bias = pl.BlockSpec((len(DILATED), None, 2 * BLK, 2 * BLK), lambda p, i: (0, p, 0, 0))
    return [cur(0), prev(1), cur(1), prev(2), cur(2), bias]


def _pair_bias(biasm):
    return biasm.reshape(len(DILATED), DIL_COLS, 2 * BLK, 2 * BLK)


def _dil_fwd(proj, biasm, name="dil_fwd", gather=()):
    s = proj.shape[0]
    nt = s // DIL_TILE
    tt = DIL_TILE

    def body(q_ref, kp_ref, kc_ref, vp_ref, vc_ref, bias_ref, att_ref, lse_ref, k2, v2, ob, lb):
        t = pl.program_id(1)
        k2[0:tt, :] = kp_ref[...]
        k2[tt:2 * tt, :] = kc_ref[...]
        v2[0:tt, :] = vp_ref[...]
        v2[tt:2 * tt, :] = vc_ref[...]
        lo = lax.broadcasted_iota(jnp.int32, (BLK, LANES), 1) < 64
        kidx = lax.broadcasted_iota(jnp.int32, (2 * BLK, 2 * BLK), 1)
        for b, (_, dil) in enumerate(DILATED):
            for j0 in range(0, tt // BLK, DIL_GROUP):
                grp = range(DIL_GROUP)
                rn = [((j0 + i) % dil, (j0 + i) // dil) for i in grp]
                here = [_dil_rows(dil, n, r) for r, n in rn]
                cur = [_dil_rows(dil, n, r, tt) for r, n in rn]
                prev = [_dil_rows(dil, n - 1, r, tt) for r, n in rn]
                q2 = [_stack_heads(q_ref[here[i], :], lo).astype(BF16) for i in grp]
                kk = [jnp.concatenate([k2[prev[i], :], k2[cur[i], :]], axis=0).astype(BF16) for i in grp]
                vv = [jnp.concatenate([v2[prev[i], :], v2[cur[i], :]], axis=0).astype(BF16) for i in grp]
                sc = [_att_scores(q2[i], kk[i], bias_ref[b], (t > 0) | (rn[i][1] > 0) | (kidx >= BLK)) for i in grp]
                mx = [jnp.max(sc[i], axis=1, keepdims=True) for i in grp]
                pe = [jnp.exp(sc[i] - mx[i]) for i in grp]
                l = [jnp.sum(pe[i], axis=1, keepdims=True) for i in grp]
                o2 = [_dot(pe[i].astype(BF16), vv[i], NN) for i in grp]
                for i in grp:
                    ob.at[b][here[i], :] = _unstack_heads(o2[i] / l[i], lo)
                    lb.at[b][here[i], :] = _unstack_heads(jnp.broadcast_to(mx[i] + jnp.log(l[i]), (2 * BLK, LANES)), lo)
        l0, l1, l2 = lb[0], lb[1], lb[2]
        mx = jnp.maximum(jnp.maximum(l0, l1), l2)
        e0, e1, e2 = jnp.exp(l0 - mx), jnp.exp(l1 - mx), jnp.exp(l2 - mx)
        tot = e0 + e1 + e2
        att_ref[...] = (e0 * ob[0] + e1 * ob[1] + e2 * ob[2]) / tot
        lse_ref[...] = mx + jnp.log(tot)

    out = pl.BlockSpec((tt, LANES), lambda p, i: (i, p))
    return _call(
        body, name=name, grid=(DIL_COLS, nt), in_specs=_dil_in_specs(lambda i: i), out_specs=[out, out],
        out_shape=[jax.ShapeDtypeStruct((s, ATT_W), F32)] * 2, args=(proj, proj, proj, proj, proj, _pair_bias(biasm)),
        scratch_shapes=[pltpu.VMEM((2 * tt, LANES), F32), pltpu.VMEM((2 * tt, LANES), F32),
                        pltpu.VMEM((len(DILATED), tt, LANES), F32), pltpu.VMEM((len(DILATED), tt, LANES), F32)],
        sem=("parallel", "parallel"), gather=gather)


def _dil_bwd(proj, biasm, lse, att, dcat, name="dil_bwd"):
    s = proj.shape[0]
    nt = s // DIL_TILE
    tt = DIL_TILE
    nbr = len(DILATED)

    def body(q_ref, kp_ref, kc_ref, vp_ref, vc_ref, bias_ref, lse_ref, att_ref, datt_ref,
             dq_ref, dk_ref, dv_ref, dbias_ref, k2, v2, dqa, dka, dva, kcar, vcar):
        i = pl.program_id(1)
        t = nt - 1 - i
        k2[0:tt, :] = kp_ref[...]
        k2[tt:2 * tt, :] = kc_ref[...]
        v2[0:tt, :] = vp_ref[...]
        v2[tt:2 * tt, :] = vc_ref[...]

        @pl.when(i == 0)
        def _():
            kcar[...] = jnp.zeros_like(kcar)
            vcar[...] = jnp.zeros_like(vcar)
            dbias_ref[...] = jnp.zeros_like(dbias_ref)

        dqa[...] = jnp.zeros_like(dqa)
        dka[0:tt, :] = jnp.zeros((tt, LANES), F32)
        dva[0:tt, :] = jnp.zeros((tt, LANES), F32)
        dka[tt:2 * tt, :] = kcar[...]
        dva[tt:2 * tt, :] = vcar[...]
        lo = lax.broadcasted_iota(jnp.int32, (BLK, LANES), 1) < 64
        kidx = lax.broadcasted_iota(jnp.int32, (2 * BLK, 2 * BLK), 1)
        for b, (_, dil) in enumerate(DILATED):
            for j0 in range(0, tt // BLK, DIL_GROUP):
                grp = range(DIL_GROUP)
                rn = [((j0 + i) % dil, (j0 + i) // dil) for i in grp]
                here = [_dil_rows(dil, n, r) for r, n in rn]
                cur = [_dil_rows(dil, n, r, tt) for r, n in rn]
                prev = [_dil_rows(dil, n - 1, r, tt) for r, n in rn]
                dat = [datt_ref[here[i], :] for i in grp]
                q2 = [_stack_heads(q_ref[here[i], :], lo).astype(BF16) for i in grp]
                dom = [_stack_heads(dat[i], lo).astype(BF16) for i in grp]
                kk = [jnp.concatenate([k2[prev[i], :], k2[cur[i], :]], axis=0).astype(BF16) for i in grp]
                vv = [jnp.concatenate([v2[prev[i], :], v2[cur[i], :]], axis=0).astype(BF16) for i in grp]
                sc = [_att_scores(q2[i], kk[i], bias_ref[b], (t > 0) | (rn[i][1] > 0) | (kidx >= BLK)) for i in grp]
                dp = [_dot(dom[i], vv[i], NT) for i in grp]
                pr = [jnp.exp(sc[i] - _head_cols(lse_ref[here[i], :], lo, jnp.max)) for i in grp]
                ds = [pr[i] * (dp[i] - _head_cols(dat[i] * att_ref[here[i], :], lo, jnp.sum)) for i in grp]
                dsb = [(ds[i] * (64 ** -0.5)).astype(BF16) for i in grp]
                dq2 = [_dot(dsb[i], kk[i], NN) for i in grp]
                dk2 = [_dot(dsb[i], q2[i], TN) for i in grp]
                dv2 = [_dot(pr[i].astype(BF16), dom[i], TN) for i in grp]
                for i in grp:
                    dbias_ref[b] += ds[i]
                    dqa[here[i], :] += _unstack_heads(dq2[i], lo)
                    dka[prev[i], :] += dk2[i][:BLK]
                    dka[cur[i], :] += dk2[i][BLK:]
                    dva[prev[i], :] += dv2[i][:BLK]
                    dva[cur[i], :] += dv2[i][BLK:]
        dq_ref[...] = dqa[...].astype(BF16)
        dk_ref[...] = dka[tt:2 * tt, :].astype(BF16)
        dv_ref[...] = dva[tt:2 * tt, :].astype(BF16)
        kcar[...] = dka[0:tt, :]
        vcar[...] = dva[0:tt, :]

    rev = lambda i: nt - 1 - i
    out = pl.BlockSpec((tt, LANES), lambda p, i: (rev(i), p))
    two = lambda: pltpu.VMEM((2 * tt, LANES), F32)
    one = lambda: pltpu.VMEM((tt, LANES), F32)
    return pl.pallas_call(
        body, name=name, grid=(DIL_COLS, nt),
        in_specs=_dil_in_specs(rev) + [out, out, out],
        out_specs=[out, out, out, pl.BlockSpec((nbr, None, 2 * BLK, 2 * BLK), lambda p, i: (0, p, 0, 0))],
        out_shape=[jax.ShapeDtypeStruct((s, ATT_W), BF16)] * 3
        + [jax.ShapeDtypeStruct((nbr, DIL_COLS, 2 * BLK, 2 * BLK), F32)],
        scratch_shapes=[two(), two(), one(), two(), two(), one(), one()],
        compiler_params=_params("arbitrary", "arbitrary"),
    )(proj, proj, proj, proj, proj, _pair_bias(biasm), lse, att, dcat)


QK_COL0 = (3 * ATT_W) // ATT_W


HALO = 8


def _conv_shifted(prev8, cur, j):
    sh = CONV_K - 1 - j
    if sh == 0:
        return cur
    rolled = pltpu.roll(cur, sh, 0)
    row8 = lax.broadcasted_iota(jnp.int32, prev8.shape, 0)
    top = jnp.where(row8 < sh, pltpu.roll(prev8, sh, 0), rolled[:HALO])
    return top if cur.shape[0] == HALO else jnp.concatenate([top, rolled[HALO:]], axis=0)


def _conv_z(prev8, cur, w_ref, b_ref):
    z = b_ref[...] + cur * w_ref[CONV_K - 1:CONV_K, :]
    for j in range(CONV_K - 1):
        z = z + _conv_shifted(prev8, cur, j) * w_ref[j:j + 1, :]
    return z


def _silu_grad(z):
    sig = _sigmoid(z)
    return sig * (1.0 + z * (1.0 - sig))


def _conv_fwd(proj, conv_w, conv_b, name="conv_fwd", tm=512):
    s = proj.shape[0]
    w = ATT_W
    per = tm // HALO

    def body(prev_ref, cur_ref, w_ref, b_ref, o_ref):
        i = pl.program_id(1)
        prev8 = jnp.where(i > 0, prev_ref[...], 0.0)
        z = _conv_z(prev8, cur_ref[...], w_ref, b_ref)
        o_ref[...] = z * _sigmoid(z)

    return pl.pallas_call(
        body, name=name, grid=(2, s // tm),
        in_specs=[pl.BlockSpec((HALO, w), lambda j, i: (jnp.maximum(i * per - 1, 0), QK_COL0 + j)),
                  pl.BlockSpec((tm, w), lambda j, i: (i, QK_COL0 + j)),
                  pl.BlockSpec((CONV_K, w), lambda j, i: (0, j)),
                  pl.BlockSpec((1, w), lambda j, i: (0, j))],
        out_specs=pl.BlockSpec((tm, w), lambda j, i: (i, j)),
        out_shape=jax.ShapeDtypeStruct((s, 2 * ML_W), F32),
        compiler_params=_params("parallel", "parallel"),
    )(proj, proj, conv_w, conv_b)


def _conv_bwd(proj, dqk, conv_w, conv_b, name="conv_bwd", tm=512):
    s = proj.shape[0]
    w = ATT_W
    nt = s // tm
    per = tm // HALO

    def body(xp_ref, xc_ref, xn_ref, dc_ref, dn_ref, w_ref, b_ref, dx_ref, dw_ref, db_ref):
        i = pl.program_id(1)
        prev8 = jnp.where(i > 0, xp_ref[...], 0.0)
        cur = xc_ref[...]
        dzc = dc_ref[...] * _silu_grad(_conv_z(prev8, cur, w_ref, b_ref))
        dzn8 = dn_ref[...] * _silu_grad(_conv_z(cur[tm - HALO:], xn_ref[...], w_ref, b_ref))
        dzn8 = jnp.where(i < nt - 1, dzn8, 0.0)
        row8 = lax.broadcasted_iota(jnp.int32, (HALO, w), 0)
        dx = dzc * w_ref[CONV_K - 1:CONV_K, :]
        for j in range(CONV_K - 1):
            sh = CONV_K - 1 - j
            rolled = pltpu.roll(dzc, tm - sh, 0)
            bottom = jnp.where(row8 >= HALO - sh, pltpu.roll(dzn8, HALO - sh, 0), rolled[tm - HALO:])
            dx = dx + jnp.concatenate([rolled[:tm - HALO], bottom], axis=0) * w_ref[j:j + 1, :]
        dx_ref[...] = dx

        @pl.when(i == 0)
        def _():
            dw_ref[...] = jnp.zeros_like(dw_ref)
            db_ref[...] = jnp.zeros_like(db_ref)

        for j in range(CONV_K):
            dw_ref[j:j + 1, :] += jnp.sum(dzc * _conv_shifted(prev8, cur, j), axis=0, keepdims=True)
        db_ref[...] += jnp.sum(dzc, axis=0, keepdims=True)

    last = s // HALO - 1
    halo_before = lambda col0: pl.BlockSpec((HALO, w), lambda j, i: (jnp.maximum(i * per - 1, 0), col0 + j))
    halo_after = lambda col0: pl.BlockSpec((HALO, w), lambda j, i: (jnp.minimum((i + 1) * per, last), col0 + j))
    tile = lambda col0: pl.BlockSpec((tm, w), lambda j, i: (i, col0 + j))
    return pl.pallas_call(
        body, name=name, grid=(2, nt),
        in_specs=[halo_before(QK_COL0), tile(QK_COL0), halo_after(QK_COL0), tile(0), halo_after(0),
                  pl.BlockSpec((CONV_K, w), lambda j, i: (0, j)), pl.BlockSpec((1, w), lambda j, i: (0, j))],
        out_specs=[tile(0), pl.BlockSpec((CONV_K, w), lambda j, i: (0, j)),
                   pl.BlockSpec((1, w), lambda j, i: (0, j))],
        out_shape=[jax.ShapeDtypeStruct((s, 2 * ML_W), F32), jax.ShapeDtypeStruct((CONV_K, 2 * ML_W), F32),
                   jax.ShapeDtypeStruct((1, 2 * ML_W), F32)],
        compiler_params=_params("parallel", "arbitrary"),
    )(proj, proj, proj, dqk, dqk, conv_w, conv_b)


def _bf16_mm(dims_fwd):
    @jax.custom_vjp
    def mm(a, b):
        return _dot(a.astype(BF16), b.astype(BF16), dims_fwd)

    def fwd(a, b):
        return mm(a, b), (a, b)

    def bwd(res, g):
        a, b = res
        if dims_fwd is NN:
            return _mm_nt(g, b), _mm_tn(a, g)
        if dims_fwd is NT:
            return _mm_nn(g, b), _mm_tn(g, a)
        return _mm_nt(b, g), _mm_nn(a, g)

    mm.defvjp(fwd, bwd)
    return mm


_mm_nn = _bf16_mm(NN)
_mm_nt = _bf16_mm(NT)
_mm_tn = _bf16_mm(TN)


def _tri(lower):
    r = lax.broadcasted_iota(jnp.int32, (CHUNK, CHUNK), 0)
    c = lax.broadcasted_iota(jnp.int32, (CHUNK, CHUNK), 1)
    return ((r >= c) if lower else (r <= c)).astype(F32)


@jax.custom_vjp
def _cumsum_rows(x):
    return lax.dot_general(_tri(True), x, NN, precision=lax.Precision.HIGHEST, preferred_element_type=F32)


def _cumsum_fwd(x):
    return _cumsum_rows(x), None


def _cumsum_bwd(_, g):
    return (lax.dot_general(_tri(False), g, NN, precision=lax.Precision.HIGHEST, preferred_element_type=F32),)


_cumsum_rows.defvjp(_cumsum_fwd, _cumsum_bwd)


def _abs(x):
    return jnp.where(x >= 0, x, -x)


def _log_sigmoid(x):
    return jnp.minimum(x, 0.0) - jnp.log(1.0 + jnp.exp(-_abs(x)))


def _pick_col(x, lane):
    sel = lax.broadcasted_iota(jnp.int32, x.shape, 1) == lane
    return jnp.sum(jnp.where(sel, x, 0.0), axis=1, keepdims=True)


def _pick_row(x, r):
    sel = lax.broadcasted_iota(jnp.int32, x.shape, 0) == r
    return jnp.sum(jnp.where(sel, x, 0.0), axis=0, keepdims=True)


def _mlstm_chunk(qs, ks, vs, oms, gates, gate_bias, mlg, cs, ns, ms):
    gb = gates + gate_bias
    cum = _cumsum_rows(_log_sigmoid(gb))
    gbt = gb.T
    cumt = cum.T
    causal = lax.broadcasted_iota(jnp.int32, (CHUNK, CHUNK), 0) >= lax.broadcasted_iota(jnp.int32, (CHUNK, CHUNK), 1)
    hd = range(ML_HEADS)
    k = [ks[h] * (ML_HD ** -0.5) for h in hd]
    ig_col = [_pick_col(gb, h) for h in hd]
    ig_row = [_pick_row(gbt, h) for h in hd]
    b_col = [_pick_col(cum, ML_HEADS + h) for h in hd]
    b_row = [_pick_row(cumt, ML_HEADS + h) for h in hd]
    g = [_pick_row(b_col[h], CHUNK - 1) for h in hd]
    a = [g[h] - b_col[h] + ig_col[h] for h in hd]
    m_loc = [jnp.max(a[h], axis=0, keepdims=True) for h in hd]
    wa = [jnp.exp(a[h] - m_loc[h]) for h in hd]
    d_log = [jnp.where(causal, b_col[h] - b_row[h] + ig_row[h], -jnp.inf) for h in hd]
    e_log = [b_col[h] + ms[h] for h in hd]
    m_t = [jnp.maximum(e_log[h], jnp.max(d_log[h], axis=1, keepdims=True)) for h in hd]
    d_w = [jnp.exp(d_log[h] - m_t[h]) for h in hd]
    e_w = [jnp.exp(e_log[h] - m_t[h]) for h in hd]
    qk = [_mm_nt(qs[h], k[h]) for h in hd]
    qc = [_mm_nt(qs[h], cs[h]) for h in hd]
    c_loc = [_mm_tn(wa[h] * vs[h], k[h]) for h in hd]
    s_qk = [qk[h] * d_w[h] for h in hd]
    sv = [_mm_nn(s_qk[h], vs[h]) for h in hd]
    n_loc = [jnp.sum(wa[h] * k[h], axis=0, keepdims=True) for h in hd]
    m_out = [jnp.maximum(g[h] + ms[h], m_loc[h]) for h in hd]
    sp = [jnp.exp(g[h] + ms[h] - m_out[h]) for h in hd]
    sl = [jnp.exp(m_loc[h] - m_out[h]) for h in hd]
    c_out = [sp[h] * cs[h] + sl[h] * c_loc[h] for h in hd]
    n_out = [sp[h] * ns[h] + sl[h] * n_loc[h] for h in hd]
    num = [e_w[h] * qc[h] + sv[h] for h in hd]
    den = [e_w[h] * jnp.sum(qs[h] * ns[h], axis=1, keepdims=True) + jnp.sum(s_qk[h], axis=1, keepdims=True) for h in hd]
    hg = [_sigmoid(oms[h]) * (num[h] / jnp.maximum(_abs(den[h]), jnp.exp(-m_t[h]))) for h in hd]
    mu = [jnp.mean(hg[h], axis=1, keepdims=True) for h in hd]
    hc = [hg[h] - mu[h] for h in hd]
    var = [jnp.mean(hc[h] * hc[h], axis=1, keepdims=True) for h in hd]
    ys = [hc[h] * lax.rsqrt(var[h] + LN_EPS) * mlg[h] for h in hd]
    return ys, c_out, n_out, m_out


V_COL = 5
O_COL = 6
ML_SUB = 1


def _mlstm_fwd(qk, proj, gates, gate_bias, mlg, name="mlstm_fwd", gather=()):
    s = qk.shape[0]
    nc = s // CHUNK

    def body(q_ref, k_ref, v_ref, o_ref, g_ref, gb_ref, mlg_ref, y_ref, cp_ref, np_ref, mp_ref, c_s, n_s, m_s):
        ci = pl.program_id(0)

        @pl.when(ci == 0)
        def _():
            c_s[...] = jnp.zeros_like(c_s)
            n_s[...] = jnp.zeros_like(n_s)
            m_s[...] = jnp.zeros_like(m_s)

        for sub in range(ML_SUB):
            rows = slice(CHUNK * sub, CHUNK * (sub + 1))
            hs = lambda ref: [ref[rows, LANES * h:LANES * (h + 1)] for h in range(ML_HEADS)]
            cp_ref[sub] = c_s[...]
            np_ref[sub] = n_s[...]
            mp_ref[sub] = m_s[...]
            ys, c_new, n_new, m_new = _mlstm_chunk(
                hs(q_ref), hs(k_ref), hs(v_ref), hs(o_ref), g_ref[rows, :], gb_ref[...],
                [mlg_ref[:, LANES * h:LANES * (h + 1)] for h in range(ML_HEADS)],
                [c_s[h] for h in range(ML_HEADS)], [n_s[h:h + 1, :] for h in range(ML_HEADS)],
                [m_s[h:h + 1, 0:1] for h in range(ML_HEADS)])
            for h in range(ML_HEADS):
                y_ref[rows, LANES * h:LANES * (h + 1)] = ys[h]
                c_s[h] = c_new[h]
                n_s[h:h + 1, :] = n_new[h]
                m_s[h:h + 1, :] = jnp.broadcast_to(m_new[h], (1, LANES))

    blk = lambda col: pl.BlockSpec((ML_SUB * CHUNK, ML_W), lambda ci: (ci, col))
    vec = lambda w: pl.BlockSpec((1, w), lambda ci: (0, 0))
    return _call(
        body, name=name, grid=(nc // ML_SUB,), args=(qk, qk, proj, proj, gates, gate_bias, mlg), sem=("arbitrary",),
        gather=gather,
        in_specs=[blk(0), blk(1), blk(V_COL), blk(O_COL), pl.BlockSpec((ML_SUB * CHUNK, LANES), lambda ci: (ci, 0)),
                  vec(LANES), vec(ML_W)],
        out_specs=[blk(0), pl.BlockSpec((ML_SUB, ML_HEADS, ML_HD, ML_HD), lambda ci: (ci, 0, 0, 0)),
                   pl.BlockSpec((ML_SUB, 8, LANES), lambda ci: (ci, 0, 0)),
                   pl.BlockSpec((ML_SUB, 8, LANES), lambda ci: (ci, 0, 0))],
        out_shape=[jax.ShapeDtypeStruct((s, ML_W), F32), jax.ShapeDtypeStruct((nc, ML_HEADS, ML_HD, ML_HD), F32),
                   jax.ShapeDtypeStruct((nc, 8, LANES), F32), jax.ShapeDtypeStruct((nc, 8, LANES), F32)],
        scratch_shapes=[pltpu.VMEM((ML_HEADS, ML_HD, ML_HD), F32), pltpu.VMEM((8, LANES), F32),
                        pltpu.VMEM((8, LANES), F32)])


def _mlstm_bwd(qk, proj, gates, gate_bias, mlg, cprev, nprev, mprev, dy, name="mlstm_bwd", exchange=()):
    s = qk.shape[0]
    nc = s // CHUNK

    def body(q_ref, k_ref, v_ref, o_ref, g_ref, gb_ref, mlg_ref, cp_ref, np_ref, mp_ref, dy_ref,
             dqk_ref, dv_ref, do_ref, dg_ref, dgb_ref, dmlg_ref, dc_s, dn_s, dm_s, gb8, mg8):
        ci = pl.program_id(0)

        @pl.when(ci == 0)
        def _():
            dc_s[...] = jnp.zeros_like(dc_s)
            dn_s[...] = jnp.zeros_like(dn_s)
            dm_s[...] = jnp.zeros_like(dm_s)
            gb8[...] = jnp.zeros_like(gb8)
            mg8[...] = jnp.zeros_like(mg8)

        for sub in reversed(range(ML_SUB)):
            rows = slice(CHUNK * sub, CHUNK * (sub + 1))
            hs = lambda ref: [ref[rows, LANES * h:LANES * (h + 1)] for h in range(ML_HEADS)]
            prim = (hs(q_ref), hs(k_ref), hs(v_ref), hs(o_ref), g_ref[rows, :], gb_ref[...],
                    [mlg_ref[:, LANES * h:LANES * (h + 1)] for h in range(ML_HEADS)],
                    [cp_ref[sub, h] for h in range(ML_HEADS)], [np_ref[sub, h:h + 1, :] for h in range(ML_HEADS)],
                    [mp_ref[sub, h:h + 1, 0:1] for h in range(ML_HEADS)])
            _, vjp = jax.vjp(_mlstm_chunk, *prim)
            cot = (hs(dy_ref), [dc_s[h] for h in range(ML_HEADS)], [dn_s[h:h + 1, :] for h in range(ML_HEADS)],
                   [dm_s[h:h + 1, 0:1] for h in range(ML_HEADS)])
            dqs, dks, dvs, dos, dg, dgb, dmlg, dcs, dns, dms = vjp(cot)
            dg_ref[rows, :] = dg
            gb8[0:1, :] += dgb
            for h in range(ML_HEADS):
                sl = slice(LANES * h, LANES * (h + 1))
                dqk_ref[rows, sl] = dqs[h]
                dqk_ref[rows, ML_W + LANES * h:ML_W + LANES * (h + 1)] = dks[h]
                dv_ref[rows, sl] = dvs[h]
                do_ref[rows, sl] = dos[h]
                mg8[0:1, sl] += dmlg[h]
                dc_s[h] = dcs[h]
                dn_s[h:h + 1, :] = dns[h]
                dm_s[h:h + 1, :] = jnp.broadcast_to(dms[h], (1, LANES))

        @pl.when(ci == nb - 1)
        def _():
            dgb_ref[...] = gb8[0:1, :]
            dmlg_ref[...] = mg8[0:1, :]

    nb = nc // ML_SUB
    rev = lambda ci: nb - 1 - ci
    blk = lambda col: pl.BlockSpec((ML_SUB * CHUNK, ML_W), lambda ci: (rev(ci), col))
    vec = lambda w: pl.BlockSpec((1, w), lambda ci: (0, 0))
    st8 = pl.BlockSpec((ML_SUB, 8, LANES), lambda ci: (rev(ci), 0, 0))
    gsp = pl.BlockSpec((ML_SUB * CHUNK, LANES), lambda ci: (rev(ci), 0))
    return _call(
        body, name=name, grid=(nb,), sem=("arbitrary",), exchange=exchange,
        args=(qk, qk, proj, proj, gates, gate_bias, mlg, cprev, nprev, mprev, dy),
        in_specs=[blk(0), blk(1), blk(V_COL), blk(O_COL), gsp, vec(LANES), vec(ML_W),
                  pl.BlockSpec((ML_SUB, ML_HEADS, ML_HD, ML_HD), lambda ci: (rev(ci), 0, 0, 0)), st8, st8, blk(1)],
        out_specs=[pl.BlockSpec((ML_SUB * CHUNK, 2 * ML_W), lambda ci: (rev(ci), 0)), blk(0), blk(0), gsp, vec(LANES),
                   vec(ML_W)],
        out_shape=[jax.ShapeDtypeStruct((s, 2 * ML_W), F32),
                   jax.ShapeDtypeStruct((s, ML_W), F32), jax.ShapeDtypeStruct((s, ML_W), F32),
                   jax.ShapeDtypeStruct((s, LANES), F32), jax.ShapeDtypeStruct((1, LANES), F32),
                   jax.ShapeDtypeStruct((1, ML_W), F32)],
        scratch_shapes=[pltpu.VMEM((ML_HEADS, ML_HD, ML_HD), F32), pltpu.VMEM((8, LANES), F32),
                        pltpu.VMEM((8, LANES), F32), pltpu.VMEM((8, LANES), F32), pltpu.VMEM((8, ML_W), F32)])


def _xattn_tile(qs, ks, vs):
    hd = range(XA_HEADS)
    sc = [_mm_nt(qs[h], ks[h]) * (XA_HD ** -0.5) for h in hd]
    mx = [lax.stop_gradient(jnp.max(sc[h], axis=1, keepdims=True)) for h in hd]
    pe = [jnp.exp(sc[h] - mx[h]) for h in hd]
    pn = [pe[h] / jnp.sum(pe[h], axis=1, keepdims=True) for h in hd]
    return [_mm_nn(pn[h], vs[h]) for h in hd]


def _xa_heads(ref):
    return [ref[:, XA_HD * h:XA_HD * (h + 1)] for h in range(XA_HEADS)]


def _xattn_fwd(q, kv, name="xattn_fwd", tm=512):
    s, d = q.shape

    def body(q_ref, k_ref, v_ref, o_ref):
        outs = _xattn_tile(_xa_heads(q_ref), _xa_heads(k_ref), _xa_heads(v_ref))
        for h in range(XA_HEADS):
            o_ref[:, XA_HD * h:XA_HD * (h + 1)] = outs[h]

    row = pl.BlockSpec((tm, d), lambda i: (i, 0))
    return pl.pallas_call(
        body, name=name, grid=(s // tm,),
        in_specs=[row, pl.BlockSpec((MEM_LEN, d), lambda i: (0, 0)), pl.BlockSpec((MEM_LEN, d), lambda i: (0, 1))],
        out_specs=row, out_shape=jax.ShapeDtypeStruct((s, d), F32),
        compiler_params=_params("parallel"),
    )(q, kv, kv)


def _xattn_bwd(q, kv, do, name="xattn_bwd", tm=512):
    s, d = q.shape

    def body(q_ref, k_ref, v_ref, do_ref, dq_ref, dkv_ref):
        i = pl.program_id(0)
        _, vjp = jax.vjp(_xattn_tile, _xa_heads(q_ref), _xa_heads(k_ref), _xa_heads(v_ref))
        dqs, dks, dvs = vjp(_xa_heads(do_ref))

        @pl.when(i == 0)
        def _():
            dkv_ref[...] = jnp.zeros_like(dkv_ref)

        for h in range(XA_HEADS):
            sl = slice(XA_HD * h, XA_HD * (h + 1))
            dq_ref[:, sl] = dqs[h]
            dkv_ref[:, sl] += dks[h]
            dkv_ref[:, d + XA_HD * h:d + XA_HD * (h + 1)] += dvs[h]

    row = pl.BlockSpec((tm, d), lambda i: (i, 0))
    return pl.pallas_call(
        body, name=name, grid=(s // tm,),
        in_specs=[row, pl.BlockSpec((MEM_LEN, d), lambda i: (0, 0)), pl.BlockSpec((MEM_LEN, d), lambda i: (0, 1)), row],
        out_specs=[row, pl.BlockSpec((MEM_LEN, 2 * d), lambda i: (0, 0))],
        out_shape=[jax.ShapeDtypeStruct((s, d), F32), jax.ShapeDtypeStruct((MEM_LEN, 2 * d), F32)],
        compiler_params=_params("arbitrary"),
    )(q, kv, kv, do)


def _loss_head(y, target, name="loss_head", tm=1024):
    s, d = y.shape
    nt = s // tm

    def body(y_ref, t_ref, dy_ref, loss_ref, acc):
        i = pl.program_id(0)
        err = y_ref[...] - t_ref[...]
        dy_ref[...] = err * (1.0 / d)

        @pl.when(i == 0)
        def _():
            acc[...] = jnp.zeros_like(acc)

        acc[...] += _rowsum8(err * err)

        @pl.when(i == nt - 1)
        def _():
            tot = jnp.sum(jnp.sum(acc[...], axis=0, keepdims=True), axis=1, keepdims=True)
            loss_ref[...] = jnp.broadcast_to(tot * (0.5 / d), (1, LANES))

    row = pl.BlockSpec((tm, d), lambda i: (i, 0))
    return pl.pallas_call(
        body, name=name, grid=(nt,),
        in_specs=[row, row], out_specs=[row, pl.BlockSpec((1, LANES), lambda i: (0, 0))],
        out_shape=[jax.ShapeDtypeStruct((s, d), F32), jax.ShapeDtypeStruct((1, LANES), F32)],
        scratch_shapes=[pltpu.VMEM((8, d), F32)],
        compiler_params=_params("arbitrary"),
    )(y, target)


def _adam2d(recv, w, m, v, name, layer=None):
    rows, cols = w.shape[-2:]
    fits = [t for t in range(16, rows + 1, 16) if rows % t == 0 and t * cols <= 128 * 1024]
    tr = max(fits) if fits else rows

    def body(r_ref, w_ref, m_ref, v_ref, g_ref, d_ref, mo_ref, vo_ref):
        g = r_ref[0].astype(F32)
        for j in range(1, N_DEV):
            g = g + r_ref[j].astype(F32)
        mn = ADAM_B1 * m_ref[...] + (1.0 - ADAM_B1) * g
        vn = ADAM_B2 * v_ref[...] + (1.0 - ADAM_B2) * jnp.square(g)
        m_hat = mn / (1.0 - ADAM_B1 ** ADAM_STEP)
        v_hat = vn / (1.0 - ADAM_B2 ** ADAM_STEP)
        g_ref[...] = g
        d_ref[...] = -ADAM_LR * (m_hat / (jnp.sqrt(v_hat) + ADAM_EPS) + ADAM_WD * w_ref[...])
        mo_ref[...] = mn
        vo_ref[...] = vn

    row = pl.BlockSpec((tr, cols), lambda i: (i, 0))
    if layer is None:
        wspec = row
    else:
        wspec = pl.BlockSpec((None, None, tr, cols), lambda i: (0, layer, i, 0))
    return pl.pallas_call(
        body, name=name, grid=(rows // tr,),
        in_specs=[pl.BlockSpec((N_DEV, tr, cols), lambda i: (0, i, 0)), wspec, wspec, wspec],
        out_specs=[row] * 4, out_shape=[jax.ShapeDtypeStruct((rows, cols), F32)] * 4,
        compiler_params=_params("parallel"),
    )(recv, w, m, v)


WEIGHTS = ("rel_bias", "ln_g", "ln_b", "ffn_w_gate", "ffn_w_up", "ffn_w_down", "w_in", "conv_w", "conv_b",
           "ig_bias", "fg_bias", "ml_norm_g", "w_out", "xq_w", "xkv_w", "xo_w")
SMALL = ("rel_bias", "ln_g", "ln_b", "conv_w", "conv_b", "ig_bias", "fg_bias", "ml_norm_g")
SMALL_SHAPES = {
    "rel_bias": (REL_BUCKETS, ATT_HEADS), "ln_g": (1, 4, LANES), "ln_b": (1, 4, LANES), "conv_w": (1, CONV_K, LANES),
    "conv_b": (1, 2 * ML_W), "ig_bias": (1, ML_HEADS), "fg_bias": (1, ML_HEADS), "ml_norm_g": (1, ML_W),
}
SMALL_ROWS = 8


def _pack_small(parts, lead=()):
    out = []
    for p in parts:
        p = jnp.pad(p, [(0, 0)] * len(lead) + [(0, SMALL_ROWS * LANES - p.shape[-1])])
        out.append(p.reshape(lead + (SMALL_ROWS, LANES)))
    return jnp.concatenate(out, axis=len(lead))


def _unpack_small(flat):
    out = {}
    for i, n in enumerate(SMALL):
        cnt = int(np.prod(SMALL_SHAPES[n]))
        out[n] = flat[SMALL_ROWS * i:SMALL_ROWS * (i + 1)].reshape(-1)[:cnt].reshape(SMALL_SHAPES[n])
    return out


def _split8(full, axis):
    shp = full.shape
    t = full.reshape(shp[:axis] + (N_DEV, shp[axis] // N_DEV) + shp[axis + 1:])
    return jnp.moveaxis(t, axis, 0).reshape(N_DEV, -1)


def _rep8(full):
    return jnp.broadcast_to(full.reshape(1, -1), (N_DEV, full.size))


def kernel(x, mem, rel_bias, ln_g, ln_b, ffn_w_gate, ffn_w_up, ffn_w_down, w_in, conv_w, conv_b, ig_bias, fg_bias, ml_norm_g, w_out, xq_w, xkv_w, xo_w, loss_target, m_rel_bias, m_ln_g, m_ln_b, m_ffn_w_gate, m_ffn_w_up, m_ffn_w_down, m_w_in, m_conv_w, m_conv_b, m_ig_bias, m_fg_bias, m_ml_norm_g, m_w_out, m_xq_w, m_xkv_w, m_xo_w, v_rel_bias, v_ln_g, v_ln_b, v_ffn_w_gate, v_ffn_w_up, v_ffn_w_down, v_w_in, v_conv_w, v_conv_b, v_ig_bias, v_fg_bias, v_ml_norm_g, v_w_out, v_xq_w, v_xkv_w, v_xo_w):
    w_tree = dict(rel_bias=rel_bias, ln_g=ln_g, ln_b=ln_b, ffn_w_gate=ffn_w_gate, ffn_w_up=ffn_w_up,
                  ffn_w_down=ffn_w_down, w_in=w_in, conv_w=conv_w, conv_b=conv_b, ig_bias=ig_bias, fg_bias=fg_bias,
                  ml_norm_g=ml_norm_g, w_out=w_out, xq_w=xq_w, xkv_w=xkv_w, xo_w=xo_w)
    m_tree = dict(rel_bias=m_rel_bias, ln_g=m_ln_g, ln_b=m_ln_b, ffn_w_gate=m_ffn_w_gate, ffn_w_up=m_ffn_w_up,
                  ffn_w_down=m_ffn_w_down, w_in=m_w_in, conv_w=m_conv_w, conv_b=m_conv_b, ig_bias=m_ig_bias,
                  fg_bias=m_fg_bias, ml_norm_g=m_ml_norm_g, w_out=m_w_out, xq_w=m_xq_w, xkv_w=m_xkv_w, xo_w=m_xo_w)
    v_tree = dict(rel_bias=v_rel_bias, ln_g=v_ln_g, ln_b=v_ln_b, ffn_w_gate=v_ffn_w_gate, ffn_w_up=v_ffn_w_up,
                  ffn_w_down=v_ffn_w_down, w_in=v_w_in, conv_w=v_conv_w, conv_b=v_conv_b, ig_bias=v_ig_bias,
                  fg_bias=v_fg_bias, ml_norm_g=v_ml_norm_g, w_out=v_w_out, xq_w=v_xq_w, xkv_w=v_xkv_w, xo_w=v_xo_w)
    x0 = x[0]
    pad_ff = FF_PAD - FF_SHARD
    bf = lambda t: t.astype(BF16)

    pad_rows = lambda t: jnp.pad(t, ((0, pad_ff), (0, 0)))
    ffn_shards = [(pad_rows(bf(ffn_w_gate[0, l]).T), pad_rows(bf(ffn_w_up[0, l]).T), pad_rows(bf(ffn_w_down[0, l])))
                  for l in range(2)]
    pairs = lambda t: t.reshape(N_PAIR, FF_PAIR, D_MODEL)
    w_in_shard = jnp.pad(bf(w_in[0]), ((0, 0), (0, ATT_W - W_IN_SHARD)))
    small_shard = jnp.concatenate([ln_g[0], ln_b[0], conv_w[0], jnp.zeros((4, LANES), F32)], axis=0)
    gate_bias = jnp.pad(jnp.concatenate([ig_bias, fg_bias], axis=1), ((0, 0), (0, LANES - 2 * ML_HEADS)))
    buckets = _bucket_tables()

    wg0, wu0, wd0, small_all = _gather_two_level("ffn1_weights_gather", ffn_shards[0] + (small_shard,))
    wg0, wu0, wd0 = pairs(wg0), pairs(wu0), pairs(wd0)
    unshard = lambda t: jnp.moveaxis(t, 0, 1).reshape(4, D_MODEL)
    ln_g_full, ln_b_full, conv_w_full = unshard(small_all[:, 0:4]), unshard(small_all[:, 4:8]), unshard(small_all[:, 8:12])
    lng = lambda i: ln_g_full[i:i + 1]
    lnb = lambda i: ln_b_full[i:i + 1]

    u0, x1, a0, b0, win_all, wout_all, xq_all, xo_all, xkv_all = _ffn_fwd(
        x0, wg0, wu0, wd0, lng(0), lnb(0), "ffn1_fwd",
        gather=(w_in_shard, bf(w_out[0]), bf(xq_w[0]), bf(xo_w[0]), bf(xkv_w[0])))
    w_in_full = jnp.moveaxis(win_all[:, :, :W_IN_SHARD], 0, 1).reshape(D_MODEL, W_IN)
    w_main = w_in_full[:, :W_IN_MAIN]
    w_gate_cols = jnp.pad(w_in_full[:, W_IN_MAIN:], ((0, 0), (0, LANES - 2 * ML_HEADS)))
    w_out_full = wout_all.reshape(D_MODEL, D_MODEL)
    xq_full = xq_all.reshape(D_MODEL, D_MODEL)
    xo_full = xo_all.reshape(D_MODEL, D_MODEL)

    proj, wg1 = _matmul(x1, w_main, "nn", "proj_fwd", tn=W_IN_MAIN // 2, tk=D_MODEL, gather=(ffn_shards[1][0],))
    gates, = _matmul(x1, w_gate_cols, "nn", "gates_fwd", tk=D_MODEL)
    biasm = _bias_fwd(rel_bias, buckets)
    att, lse, wd1 = _dil_fwd(proj, biasm, gather=(ffn_shards[1][2],))
    qk = _conv_fwd(proj, conv_w_full, conv_b)
    y_m, c_prev, n_prev, m_prev, wu1 = _mlstm_fwd(qk, proj, gates, gate_bias, ml_norm_g, gather=(ffn_shards[1][1],))
    u1, x2 = _matmul_resid_ln((att, y_m), w_out_full, x1, lng(1), lnb(1), "w_out_fwd")
    q_x, = _matmul(x2, xq_full, "nn", "xq_fwd", tn=D_MODEL, tk=D_MODEL)
    kv, = _matmul(mem[0], xkv_all, "nn", "xkv_fwd", tk=D_MODEL)
    o_x = _xattn_fwd(q_x, kv)
    u2, x3 = _matmul_resid_ln((o_x,), xo_full, x2, lng(2), lnb(2), "xo_fwd")
    wg1, wu1, wd1 = pairs(wg1), pairs(wu1), pairs(wd1)
    u3, x4, a3, b3 = _ffn_fwd(x3, wg1, wu1, wd1, lng(3), lnb(3), "ffn2_fwd")
    dx4, loss_row = _loss_head(x4, loss_target[0])

    dx3, xb, df, da, db, hh, dg3, db3 = _ffn_bwd_x(dx4, u3, x3, wg1, wu1, wd1, lng(3), a3, b3, "ffn2_bwd_x")
    ffn2_send = (_ffn_bwd_w(xb, da, "ffn2_bwd_wg", down=False)[0], _ffn_bwd_w(xb, db, "ffn2_bwd_wu", down=False)[0],
                 _ffn_bwd_w(df, hh, "ffn2_bwd_wd", down=True)[0])

    du2, dg2, db2 = _ln_bwd(dx3, u2, lng(2), "xattn_ln_bwd")
    do_x, = _matmul(du2, xo_full, "nt", "xo_bwd_x", tn=D_MODEL, tk=D_MODEL)
    g_xo, = _matmul(o_x, du2, "tn", "xo_bwd_w", tm=D_MODEL, tn=D_MODEL, out_dtype=BF16)
    dq_x, dkv = _xattn_bwd(q_x, kv, do_x)
    g_xq, = _matmul(x2, dq_x, "tn", "xq_bwd_w", tm=D_MODEL, tn=D_MODEL, out_dtype=BF16)
    g_xkv, = _matmul(mem[0], dkv, "tn", "xkv_bwd_w", tm=D_MODEL, tn=2 * D_MODEL // N_DEV, tk=MEM_LEN,
                     out_dtype=BF16, blocked_out=True)
    dx2, = _matmul(dq_x, xq_full, "nt", "xq_bwd_x", tn=D_MODEL, tk=D_MODEL, add=du2, add_scale=ALPHA)

    du1, dg1, db1 = _ln_bwd(dx2, u1, lng(1), "mixer_ln_bwd")
    dcat, = _matmul(du1, w_out_full, "nt", "w_out_bwd_x", tn=D_MODEL, tk=D_MODEL)
    g_w_out = jnp.concatenate(
        [_matmul(half, du1, "tn", f"w_out_bwd_w_{i}", tn=D_MODEL, out_dtype=BF16)[0] for i, half in enumerate((att, y_m))],
        axis=0)
    dqk, dv_m, do_m, dgates, dgate_bias, g_mlg, *ffn2_recv = _mlstm_bwd(
        qk, proj, gates, gate_bias, ml_norm_g, c_prev, n_prev, m_prev, dcat, exchange=tuple(ffn2_send))
    dqk_pre, g_conv_w, g_conv_b = _conv_bwd(proj, dqk, conv_w_full, conv_b)
    dq_a, dk_a, dv_a, dbias = _dil_bwd(proj, biasm, lse, att, dcat)
    g_rel = _bias_bwd(dbias.reshape(biasm.shape), buckets)[:, :ATT_HEADS]
    dproj = jnp.concatenate([dq_a, dk_a, dv_a, bf(dqk_pre), bf(dv_m), bf(do_m)], axis=1)
    g_w_main, = _matmul(x1, dproj, "tn", "proj_bwd_w", tm=D_MODEL, tn=W_IN_MAIN // 2, tk=1024, out_dtype=BF16)
    g_w_gates, = _matmul(x1, dgates, "tn", "gates_bwd_w", tm=D_MODEL, out_dtype=BF16)
    g_w_in = jnp.concatenate([g_w_main, g_w_gates[:, :2 * ML_HEADS]], axis=1)
    dx1, = _matmul(dproj, w_main, "nt", "proj_bwd_x", tn=D_MODEL, tk=W_IN_MAIN // 2, add=du1, add_scale=ALPHA)
    dx1, = _matmul(dgates, w_gate_cols, "nt", "gates_bwd_x", tn=D_MODEL, add=dx1)

    rows8 = lambda t: t.reshape(N_DEV, D_MODEL // N_DEV, D_MODEL)
    mid_send = (rows8(g_xo), rows8(g_xq), g_xkv, rows8(g_w_out),
                jnp.moveaxis(g_w_in.reshape(D_MODEL, N_DEV, W_IN_SHARD), 1, 0))
    dx0, xb, df, da, db, hh, dg0, db0, r_xo, r_xq, r_xkv, r_w_out, r_w_in = _ffn_bwd_x(
        dx1, u0, x0, wg0, wu0, wd0, lng(0), a0, b0, "ffn1_bwd_x", exchange=mid_send)
    small_blocks = {
        "rel_bias": _rep8(g_rel),
        "ln_g": _split8(jnp.concatenate([dg0, dg1, dg2, dg3], axis=0), 1),
        "ln_b": _split8(jnp.concatenate([db0, db1, db2, db3], axis=0), 1),
        "conv_w": _split8(g_conv_w, 1),
        "conv_b": _rep8(g_conv_b),
        "ig_bias": _rep8(dgate_bias[:, :ML_HEADS]),
        "fg_bias": _rep8(dgate_bias[:, ML_HEADS:2 * ML_HEADS]),
        "ml_norm_g": _rep8(g_mlg),
    }
    small_send = _pack_small([small_blocks[n] for n in SMALL], lead=(N_DEV,))
    g_wg, r_small = _ffn_bwd_w(xb, da, "ffn1_bwd_wg", down=False, exchange=(small_send,))
    g_wu, r_wg = _ffn_bwd_w(xb, db, "ffn1_bwd_wu", down=False, exchange=(g_wg,))
    g_wd, r_wu = _ffn_bwd_w(df, hh, "ffn1_bwd_wd", down=True, exchange=(g_wu,))
    r_wd, = _exchange_only("ffn1_grads_exchange", exchange=(g_wd,))
    ffn1_recv = [r_wg, r_wu, r_wd]

    res = {}
    for i, n in enumerate(("ffn_w_gate", "ffn_w_up", "ffn_w_down")):
        per_layer = [_adam2d(r[i], w_tree[n], m_tree[n], v_tree[n], f"adamw_{n}_{l}", layer=l)
                     for l, r in enumerate((ffn1_recv, ffn2_recv))]
        res[n] = [jnp.stack([per_layer[0][j], per_layer[1][j]])[None] for j in range(4)]
    for n, r in (("w_in", r_w_in), ("w_out", r_w_out), ("xq_w", r_xq), ("xkv_w", r_xkv), ("xo_w", r_xo)):
        res[n] = [t[None] for t in _adam2d(r, w_tree[n][0], m_tree[n][0], v_tree[n][0], f"adamw_{n}")]
    pack = lambda tree: _pack_small([tree[n].reshape(-1) for n in SMALL])
    small = [_unpack_small(t) for t in _adam2d(r_small, pack(w_tree), pack(m_tree), pack(v_tree), "adamw_small")]
    for n in SMALL:
        res[n] = [small[j][n] for j in range(4)]

    loss = lax.psum(loss_row[0, 0], ("x", "y", "c"))
    return (loss, dx0[None], *[res[n][0] for n in WEIGHTS], *[res[n][1] for n in WEIGHTS],
            *[res[n][2] for n in WEIGHTS], *[res[n][3] for n in WEIGHTS])
```

```python
import functools
import math

import numpy as np
import jax
import jax.numpy as jnp
from jax import lax
from jax.experimental import pallas as pl
from jax.experimental.pallas import tpu as pltpu

F32 = jnp.float32
BF16 = jnp.bfloat16

N_DEV = 8
D_MODEL = 1024
D_FF = 2816
FF_SHARD = D_FF // N_DEV
FF_PAD = 384
ATT_W = 512
ATT_HEADS = 8
DILATED = ((128, 1), (512, 4), (2048, 16))
BLK = 128
ML_W = 512
ML_HEADS = 4
ML_HD = 128
CHUNK = 128
CONV_K = 4
W_IN = 3592
W_IN_SHARD = W_IN // N_DEV
W_IN_MAIN = 3584
XA_HEADS = 4
XA_HD = 256
MEM_LEN = 256
REL_BUCKETS = 32
REL_MAX_DIST = 2048
ALPHA = 2.0 ** 0.25
LN_EPS = 1e-5
NEG = -1e30
ADAM_LR = 0.001
ADAM_B1 = 0.9
ADAM_B2 = 0.999
ADAM_EPS = 1e-08
ADAM_WD = 0.01
ADAM_STEP = 10
LANES = 128
VMEM_LIMIT = 58 * 1024 * 1024

NN = (((1,), (0,)), ((), ()))
NT = (((1,), (1,)), ((), ()))
TN = (((0,), (0,)), ((), ()))


def _dot(a, b, dims):
    return lax.dot_general(a, b, dims, preferred_element_type=F32)


def _params(*sem):
    return pltpu.CompilerParams(dimension_semantics=sem, vmem_limit_bytes=VMEM_LIMIT)


def _sigmoid(x):
    return 0.5 * jnp.tanh(0.5 * x) + 0.5


def _rowsum8(x):
    t, c = x.shape
    return jnp.sum(x.reshape(t // 8, 8, c), axis=0)


def _mesh_pos():
    x, y, c = lax.axis_index("x"), lax.axis_index("y"), lax.axis_index("c")
    return x, y, c, 4 * x + 2 * y + c


def _peer(x, y, c, k):
    px = 1 - x if k & 4 else x
    py = 1 - y if k & 2 else y
    pc = 1 - c if k & 1 else c
    return (px, py, pc), 4 * px + 2 * py + pc


def _call(body, *, name, grid, in_specs, out_specs, out_shape, args, scratch_shapes=(), sem=None,
          gather=(), exchange=()):
    in_specs, out_specs, out_shape, scratch = list(in_specs), list(out_specs), list(out_shape), list(scratch_shapes)
    ng, nc = len(gather), len(gather) + len(exchange)
    if nc == 0:
        return pl.pallas_call(body, name=name, grid=grid, in_specs=in_specs, out_specs=out_specs,
                              out_shape=out_shape, scratch_shapes=scratch, compiler_params=_params(*sem))(*args)
    n_in, n_out, n_scr = len(in_specs), len(out_specs), len(scratch)

    def wrapped(*refs):
        ins, cin = refs[:n_in], refs[n_in:n_in + nc]
        outs, cout = refs[n_in + nc:n_in + nc + n_out], refs[n_in + nc + n_out:n_in + 2 * nc + n_out]
        scr = refs[n_in + 2 * nc + n_out:n_in + 2 * nc + n_out + n_scr]
        send_sems, recv_sems, loc_sems = refs[-3:]
        first, last = None, None
        for ax, extent in enumerate(grid):
            f, l = pl.program_id(ax) == 0, pl.program_id(ax) == extent - 1
            first = f if first is None else first & f
            last = l if last is None else last & l

        def copies():
            x, y, c, me = _mesh_pos()
            out = []
            for a in range(nc):
                mine = cin[a] if a < ng else cin[a].at[me]
                out.append(pltpu.make_async_copy(mine, cout[a].at[me], loc_sems.at[a]))
                for k in range(1, N_DEV):
                    peer, pidx = _peer(x, y, c, k)
                    out.append(pltpu.make_async_remote_copy(
                        src_ref=cin[a] if a < ng else cin[a].at[pidx], dst_ref=cout[a].at[me],
                        send_sem=send_sems.at[a, k - 1], recv_sem=recv_sems.at[a, k - 1],
                        device_id=peer, device_id_type=pl.DeviceIdType.MESH))
            return out

        @pl.when(first)
        def _():
            for cp in copies():
                cp.start()

        body(*ins, *outs, *scr)

        @pl.when(last)
        def _():
            for cp in copies():
                cp.wait()

    hbm = pl.BlockSpec(memory_space=pl.ANY)
    comm_shapes = [jax.ShapeDtypeStruct((N_DEV,) + a.shape, a.dtype) for a in gather]
    comm_shapes += [jax.ShapeDtypeStruct(a.shape, a.dtype) for a in exchange]
    return pl.pallas_call(
        wrapped, name=name, grid=grid, in_specs=in_specs + [hbm] * nc, out_specs=out_specs + [hbm] * nc,
        out_shape=out_shape + comm_shapes,
        scratch_shapes=scratch + [pltpu.SemaphoreType.DMA((nc, N_DEV - 1)), pltpu.SemaphoreType.DMA((nc, N_DEV - 1)),
                                  pltpu.SemaphoreType.DMA((nc,))],
        compiler_params=_params(*(("arbitrary",) * len(grid))),
    )(*args, *gather, *exchange)


def _gather_two_level(name, arrays):
    na = len(arrays)

    def body(*refs):
        srcs, outs = refs[:na], refs[na:2 * na]
        send_sems, recv_sems, loc_sems = refs[2 * na:]
        x, y, c, me = _mesh_pos()
        here, sib = (x, y, c), (x, y, 1 - c)
        chips = [(1 - x, y), (x, 1 - y), (1 - x, 1 - y)]
        pos = lambda px, py, pc: 4 * px + 2 * py + pc

        def copy(a, k, block, to, src=None):
            return pltpu.make_async_remote_copy(
                src_ref=outs[a].at[block] if src is None else src, dst_ref=outs[a].at[block],
                send_sem=send_sems.at[a, k], recv_sem=recv_sems.at[a, k], device_id=to,
                device_id_type=pl.DeviceIdType.MESH)

        locs = [pltpu.make_async_copy(srcs[a], outs[a].at[me], loc_sems.at[a]) for a in range(na)]
        for cp in locs:
            cp.start()
        first = []
        for a in range(na):
            first.append(copy(a, 0, me, sib, src=srcs[a]))
            first += [copy(a, 1 + j, me, (*chip, c), src=srcs[a]) for j, chip in enumerate(chips)]
        for cp in first:
            cp.start()
        passed = []
        for a in range(na):
            for j, chip in enumerate(chips):
                copy(a, 1 + j, pos(*chip, c), here).wait_recv()
                passed.append(copy(a, 4 + j, pos(*chip, c), sib))
                passed[-1].start()
        for a in range(na):
            copy(a, 0, pos(x, y, 1 - c), here).wait_recv()
            for j, chip in enumerate(chips):
                copy(a, 4 + j, pos(*chip, 1 - c), here).wait_recv()
        for cp in first + passed:
            cp.wait_send()
        for cp in locs:
            cp.wait()

    hbm = pl.BlockSpec(memory_space=pl.ANY)
    return pl.pallas_call(
        body, name=name, in_specs=[hbm] * na, out_specs=[hbm] * na,
        out_shape=[jax.ShapeDtypeStruct((N_DEV,) + a.shape, a.dtype) for a in arrays],
        scratch_shapes=[pltpu.SemaphoreType.DMA((na, N_DEV - 1)), pltpu.SemaphoreType.DMA((na, N_DEV - 1)),
                        pltpu.SemaphoreType.DMA((na,))],
    )(*arrays)


def _exchange_only(name, gather=(), exchange=()):
    return _call(lambda: None, name=name, grid=(1,), in_specs=[], out_specs=[], out_shape=[], args=(),
                 gather=gather, exchange=exchange)


def _matmul(a, b, mode, name, *, out_dtype=F32, tm=1024, tn=512, tk=512, add=None, add_scale=1.0,
            blocked_out=False, gather=(), exchange=()):
    blocked_b = b.ndim == 3
    if blocked_b:
        (m, k), (nb, _, tn) = a.shape, b.shape
        n = nb * tn
    elif mode == "nn":
        (m, k), (_, n) = a.shape, b.shape
    elif mode == "nt":
        (m, k), (n, _) = a.shape, b.shape
    else:
        (k, m), (_, n) = a.shape, b.shape
    tm, tn, tk = min(tm, m), min(tn, n), min(tk, k)
    nk = k // tk
    dims = {"nn": NN, "nt": NT, "tn": TN}[mode]
    if mode == "tn":
        a_spec = pl.BlockSpec((tk, tm), lambda i, j, kk: (kk, i))
    else:
        a_spec = pl.BlockSpec((tm, tk), lambda i, j, kk: (i, kk))
    if blocked_b:
        b_spec = pl.BlockSpec((None, tk, tn), lambda i, j, kk: (j, kk, 0))
    elif mode == "nt":
        b_spec = pl.BlockSpec((tn, tk), lambda i, j, kk: (j, kk))
    else:
        b_spec = pl.BlockSpec((tk, tn), lambda i, j, kk: (kk, j))
    if blocked_out:
        o_spec = pl.BlockSpec((None, tm, tn), lambda i, j, kk: (j, i, 0))
        o_shape = jax.ShapeDtypeStruct((n // tn, m, tn), out_dtype)
    else:
        o_spec = pl.BlockSpec((tm, tn), lambda i, j, kk: (i, j))
        o_shape = jax.ShapeDtypeStruct((m, n), out_dtype)
    has_add = add is not None
    cache_a = nk == 1 and mode != "tn" and n // tn > 1 and a.dtype != BF16

    def body(*refs):
        if has_add:
            a_ref, b_ref, add_ref, o_ref, s_ref = refs
        else:
            a_ref, b_ref, o_ref, s_ref = refs
        kk = pl.program_id(2)
        if cache_a:
            @pl.when(pl.program_id(1) == 0)
            def _():
                s_ref[...] = a_ref[...].astype(BF16)

            lhs = s_ref[...]
        else:
            lhs = a_ref[...].astype(BF16)
        part = _dot(lhs, b_ref[...].astype(BF16), dims)

        def finish(r):
            if has_add:
                r = r + add_scale * add_ref[...]
            o_ref[...] = r.astype(out_dtype)

        if nk == 1:
            finish(part)
            return

        @pl.when(kk == 0)
        def _():
            s_ref[...] = part

        @pl.when(kk > 0)
        def _():
            s_ref[...] += part

        @pl.when(kk == nk - 1)
        def _():
            finish(s_ref[...])

    if nk > 1:
        scratch = [pltpu.VMEM((tm, tn), F32)]
    else:
        scratch = [pltpu.VMEM((tm, tk), BF16) if cache_a else pltpu.VMEM((8, LANES), F32)]
    return _call(
        body, name=name, grid=(m // tm, n // tn, nk),
        in_specs=[a_spec, b_spec] + ([pl.BlockSpec((tm, tn), lambda i, j, kk: (i, j))] if has_add else []),
        out_specs=[o_spec], out_shape=[o_shape], args=(a, b) + ((add,) if has_add else ()),
        scratch_shapes=scratch, sem=("parallel", "arbitrary", "arbitrary"),
        gather=gather, exchange=exchange)


def _ln_fwd_math(u, g, b):
    mu = jnp.mean(u, axis=-1, keepdims=True)
    uc = u - mu
    var = jnp.mean(uc * uc, axis=-1, keepdims=True)
    return uc * lax.rsqrt(var + LN_EPS) * g + b


def _ln_bwd_math(dy, u, g):
    mu = jnp.mean(u, axis=-1, keepdims=True)
    uc = u - mu
    var = jnp.mean(uc * uc, axis=-1, keepdims=True)
    rstd = lax.rsqrt(var + LN_EPS)
    xhat = uc * rstd
    dxh = dy * g
    m1 = jnp.mean(dxh, axis=-1, keepdims=True)
    m2 = jnp.mean(dxh * xhat, axis=-1, keepdims=True)
    return rstd * (dxh - m1 - xhat * m2), xhat


def _matmul_resid_ln(pieces, w, x, g, b, name, tm=1024):
    s = pieces[0].shape[0]
    k, d = w.shape
    widths = [p.shape[1] for p in pieces]

    def body(*refs):
        a_refs = refs[:len(pieces)]
        w_ref, x_ref, g_ref, b_ref, u_ref, y_ref = refs[len(pieces):]
        u = ALPHA * x_ref[...]
        lo = 0
        for a_ref, width in zip(a_refs, widths):
            u = u + _dot(a_ref[...].astype(BF16), w_ref[lo:lo + width, :], NN)
            lo += width
        u_ref[...] = u
        y_ref[...] = _ln_fwd_math(u, g_ref[...], b_ref[...])

    row = pl.BlockSpec((tm, d), lambda i: (i, 0))
    vec = pl.BlockSpec((1, d), lambda i: (0, 0))
    return pl.pallas_call(
        body, name=name, grid=(s // tm,),
        in_specs=[pl.BlockSpec((tm, width), lambda i: (i, 0)) for width in widths]
        + [pl.BlockSpec((k, d), lambda i: (0, 0)), row, vec, vec],
        out_specs=[row, row], out_shape=[jax.ShapeDtypeStruct((s, d), F32)] * 2,
        compiler_params=_params("parallel"),
    )(*pieces, w, x, g, b)


def _ln_bwd(dy, u, g, w, name, tm=1024):
    s, d = dy.shape
    n = w.shape[0]
    nt = s // tm

    def body(dy_ref, u_ref, g_ref, w_ref, du_ref, dz_ref, dg_ref, db_ref, g8, b8):
        i = pl.program_id(0)
        dy_ = dy_ref[...]
        du, xhat = _ln_bwd_math(dy_, u_ref[...], g_ref[...])
        du_ref[...] = du
        dz_ref[...] = _dot(du.astype(BF16), w_ref[...], NT)

        @pl.when(i == 0)
        def _():
            g8[...] = jnp.zeros_like(g8)
            b8[...] = jnp.zeros_like(b8)

        g8[...] += _rowsum8(dy_ * xhat)
        b8[...] += _rowsum8(dy_)

        @pl.when(i == nt - 1)
        def _():
            dg_ref[...] = jnp.sum(g8[...], axis=0, keepdims=True)
            db_ref[...] = jnp.sum(b8[...], axis=0, keepdims=True)

    row = pl.BlockSpec((tm, d), lambda i: (i, 0))
    vec = pl.BlockSpec((1, d), lambda i: (0, 0))
    return pl.pallas_call(
        body, name=name, grid=(nt,),
        in_specs=[row, row, vec, pl.BlockSpec((n, d), lambda i: (0, 0))],
        out_specs=[row, pl.BlockSpec((tm, n), lambda i: (i, 0)), vec, vec],
        out_shape=[jax.ShapeDtypeStruct((s, d), F32), jax.ShapeDtypeStruct((s, n), F32),
                   jax.ShapeDtypeStruct((1, d), F32), jax.ShapeDtypeStruct((1, d), F32)],
        scratch_shapes=[pltpu.VMEM((8, d), F32), pltpu.VMEM((8, d), F32)],
        compiler_params=_params("arbitrary"),
    )(dy, u, g, w)


FF_PAIR = 2 * FF_PAD
N_PAIR = N_DEV // 2
FF_COLS = 256


def _ffn_fwd(x, wgt, wut, wd, g, b, name, tm=1024, gather=()):
    s, d = x.shape

    def body(x_ref, wg_ref, wu_ref, wd_ref, g_ref, b_ref, u_ref, y_ref, a_ref, bb_ref, xb, acc):
        k = pl.program_id(1)

        @pl.when(k == 0)
        def _():
            xb[...] = x_ref[...].astype(BF16)

        a = _dot(xb[...], wg_ref[...], NT)
        bb = _dot(xb[...], wu_ref[...], NT)
        a_ref[...] = a.astype(BF16)
        bb_ref[...] = bb.astype(BF16)
        h = (a * _sigmoid(a) * bb).astype(BF16)
        part = _dot(h, wd_ref[...], NN)

        @pl.when(k == 0)
        def _():
            acc[...] = part

        @pl.when(k > 0)
        def _():
            acc[...] += part

        @pl.when(k == N_PAIR - 1)
        def _():
            u = ALPHA * x_ref[...] + 0.5 * acc[...]
            u_ref[...] = u
            y_ref[...] = _ln_fwd_math(u, g_ref[...], b_ref[...])

    row = pl.BlockSpec((tm, d), lambda i, k: (i, 0))
    vec = pl.BlockSpec((1, d), lambda i, k: (0, 0))
    w_in = pl.BlockSpec((None, FF_PAIR, d), lambda i, k: (k, 0, 0))
    w_dn = w_in
    hid = pl.BlockSpec((tm, FF_PAIR), lambda i, k: (i, k))
    return _call(
        body, name=name, grid=(s // tm, N_PAIR),
        in_specs=[row, w_in, w_in, w_dn, vec, vec], out_specs=[row, row, hid, hid],
        out_shape=[jax.ShapeDtypeStruct((s, d), F32)] * 2 + [jax.ShapeDtypeStruct((s, N_DEV * FF_PAD), BF16)] * 2,
        args=(x, wgt, wut, wd, g, b),
        scratch_shapes=[pltpu.VMEM((tm, d), BF16), pltpu.VMEM((tm, d), F32)],
        sem=("parallel", "arbitrary"), gather=gather)


def _ffn_bwd_x(dy, u, x, wgt, wut, wd, g, a_fwd, b_fwd, name, tm=512, exchange=()):
    s, d = x.shape
    nt = s // tm
    ffp = N_DEV * FF_PAD

    def body(dy_ref, u_ref, x_ref, wg_ref, wu_ref, wd_ref, g_ref, a_ref, bb_ref,
             dx_ref, xb, df_ref, da_ref, db_ref, h_ref, dg_ref, dbl_ref,
             dfb, du_s, acc, g8, b8):
        i = pl.program_id(0)
        k = pl.program_id(1)

        @pl.when(k == 0)
        def _():
            dy_ = dy_ref[...]
            du, xhat = _ln_bwd_math(dy_, u_ref[...], g_ref[...])
            du_s[...] = du
            dfb[...] = (0.5 * du).astype(BF16)
            df_ref[...] = dfb[...]
            xb[...] = x_ref[...].astype(BF16)

            @pl.when(i == 0)
            def _():
                g8[...] = jnp.zeros_like(g8)
                b8[...] = jnp.zeros_like(b8)

            g8[...] += _rowsum8(dy_ * xhat)
            b8[...] += _rowsum8(dy_)

        dh_all = _dot(dfb[...], wd_ref[...], NT)

        def gate_grads(c):
            cs = slice(FF_COLS * c, FF_COLS * (c + 1))
            a = a_ref[:, cs].astype(F32)
            bb = bb_ref[:, cs].astype(F32)
            dh = dh_all[:, cs]
            sig = _sigmoid(a)
            sa = a * sig
            h_ref[:, cs] = (sa * bb).astype(BF16)
            da = (dh * bb * (sig * (1.0 + a * (1.0 - sig)))).astype(BF16)
            db = (dh * sa).astype(BF16)
            da_ref[:, cs] = da
            db_ref[:, cs] = db
            return da, db

        n_chunks = FF_PAIR // FF_COLS
        chunks = [gate_grads(0)]
        part = None
        for c in range(n_chunks):
            if c + 1 < n_chunks:
                chunks.append(gate_grads(c + 1))
            cs = slice(FF_COLS * c, FF_COLS * (c + 1))
            pc = _dot(chunks[c][0], wg_ref[cs, :], NN) + _dot(chunks[c][1], wu_ref[cs, :], NN)
            part = pc if part is None else part + pc

        @pl.when(k == 0)
        def _():
            acc[...] = part

        @pl.when(k > 0)
        def _():
            acc[...] += part

        @pl.when(k == N_PAIR - 1)
        def _():
            dx_ref[...] = ALPHA * du_s[...] + acc[...]

        @pl.when((k == N_PAIR - 1) & (i == nt - 1))
        def _():
            dg_ref[...] = jnp.sum(g8[...], axis=0, keepdims=True)
            dbl_ref[...] = jnp.sum(b8[...], axis=0, keepdims=True)

    row = pl.BlockSpec((tm, d), lambda i, k: (i, 0))
    vec = pl.BlockSpec((1, d), lambda i, k: (0, 0))
    w_in = pl.BlockSpec((None, FF_PAIR, d), lambda i, k: (k, 0, 0))
    hid = pl.BlockSpec((tm, FF_PAIR), lambda i, k: (i, k))
    return _call(
        body, name=name, grid=(nt, N_PAIR),
        in_specs=[row, row, row, w_in, w_in, w_in, vec, hid, hid],
        out_specs=[row, row, row, hid, hid, hid, vec, vec],
        out_shape=[jax.ShapeDtypeStruct((s, d), F32), jax.ShapeDtypeStruct((s, d), BF16),
                   jax.ShapeDtypeStruct((s, d), BF16),
                   jax.ShapeDtypeStruct((s, ffp), BF16), jax.ShapeDtypeStruct((s, ffp), BF16),
                   jax.ShapeDtypeStruct((s, ffp), BF16),
                   jax.ShapeDtypeStruct((1, d), F32), jax.ShapeDtypeStruct((1, d), F32)],
        args=(dy, u, x, wgt, wut, wd, g, a_fwd, b_fwd),
        scratch_shapes=[pltpu.VMEM((tm, d), BF16), pltpu.VMEM((tm, d), F32),
                        pltpu.VMEM((tm, d), F32), pltpu.VMEM((8, d), F32), pltpu.VMEM((8, d), F32)],
        sem=("arbitrary", "arbitrary"), exchange=exchange)


def _ffn_bwd_w(tok, hid, name, *, down, tm=2048, exchange=()):
    s, d = tok.shape
    nt = s // tm

    def body(t_ref, h_ref, dw_ref, acc):
        i = pl.program_id(1)
        part = _dot(h_ref[...], t_ref[...], TN) if down else _dot(t_ref[...], h_ref[...], TN)

        @pl.when(i == 0)
        def _():
            acc[...] = part

        @pl.when(i > 0)
        def _():
            acc[...] += part

        @pl.when(i == nt - 1)
        def _():
            for j in range(2):
                lo = j * FF_PAD
                dw_ref[j] = (acc[lo:lo + FF_SHARD, :] if down else acc[:, lo:lo + FF_SHARD]).astype(BF16)

    blk = (FF_SHARD, d) if down else (d, FF_SHARD)
    return _call(
        body, name=name, grid=(N_PAIR, nt),
        in_specs=[pl.BlockSpec((tm, d), lambda k, i: (i, 0)), pl.BlockSpec((tm, FF_PAIR), lambda k, i: (i, k))],
        out_specs=[pl.BlockSpec((2,) + blk, lambda k, i: (k, 0, 0))],
        out_shape=[jax.ShapeDtypeStruct((N_DEV,) + blk, BF16)], args=(tok, hid),
        scratch_shapes=[pltpu.VMEM((FF_PAIR, d) if down else (d, FF_PAIR), F32)],
        sem=("parallel", "arbitrary"), exchange=exchange)


def _bucket_tables():
    qi = np.arange(BLK)[:, None]
    ki = np.arange(2 * BLK)[None, :]
    off = qi + BLK - ki
    out = []
    for window, dil in DILATED:
        n_keys = window // dil
        dist = dil * np.clip(off, 0, n_keys)
        exact = REL_BUCKETS // 2
        df = np.maximum(dist, 1).astype(np.float32)
        large = exact + (np.log(df / np.float32(exact)) / np.float32(math.log(REL_MAX_DIST / exact))
                         * np.float32(REL_BUCKETS - exact)).astype(np.int32)
        large = np.minimum(large, REL_BUCKETS - 1)
        bucket = np.where(dist < exact, dist, large).astype(np.int32)
        band = (off >= 0) & (off <= n_keys)
        out.append(np.where(band, bucket, -1))
    return np.stack(out).astype(np.int32)


def _bias_fwd(rel_bias, buckets, name="bias_fwd"):
    def body(tbl_ref, bkt_ref, out_ref):
        bkt = bkt_ref[...]
        for h in range(ATT_HEADS):
            acc = jnp.full((BLK, 2 * BLK), NEG, F32)
            for bb in range(REL_BUCKETS):
                acc = jnp.where(bkt == bb, tbl_ref[bb, h], acc)
            out_ref[h] = acc

    nbr = len(DILATED)
    return pl.pallas_call(
        body, name=name, grid=(nbr,),
        in_specs=[pl.BlockSpec(memory_space=pltpu.SMEM),
                  pl.BlockSpec((None, BLK, 2 * BLK), lambda r: (r, 0, 0))],
        out_specs=pl.BlockSpec((None, ATT_HEADS, BLK, 2 * BLK), lambda r: (r, 0, 0, 0)),
        out_shape=jax.ShapeDtypeStruct((nbr, ATT_HEADS, BLK, 2 * BLK), F32),
        compiler_params=_params("parallel"),
    )(rel_bias, buckets)


def _bias_bwd(dbias, buckets, name="bias_bwd"):
    nbr = len(DILATED)

    def body(db_ref, bkt_ref, out_ref):
        r = pl.program_id(0)

        @pl.when(r == 0)
        def _():
            out_ref[...] = jnp.zeros_like(out_ref)

        bkt = bkt_ref[...]
        rowi = lax.broadcasted_iota(jnp.int32, (REL_BUCKETS, LANES), 0)
        coli = lax.broadcasted_iota(jnp.int32, (REL_BUCKETS, LANES), 1)
        acc = jnp.zeros((REL_BUCKETS, LANES), F32)
        for h in range(ATT_HEADS):
            x = db_ref[h]
            for bb in range(REL_BUCKETS):
                part = jnp.sum(jnp.where(bkt == bb, x, 0.0), axis=0, keepdims=True)
                tot = jnp.sum(part, axis=1, keepdims=True)
                acc = acc + jnp.where((rowi == bb) & (coli == h), tot, 0.0)
        out_ref[...] += acc

    return pl.pallas_call(
        body, name=name, grid=(nbr,),
        in_specs=[pl.BlockSpec((None, ATT_HEADS, BLK, 2 * BLK), lambda r: (r, 0, 0, 0)),
                  pl.BlockSpec((None, BLK, 2 * BLK), lambda r: (r, 0, 0))],
        out_specs=pl.BlockSpec((REL_BUCKETS, LANES), lambda r: (0, 0)),
        out_shape=jax.ShapeDtypeStruct((REL_BUCKETS, LANES), F32),
        compiler_params=_params("arbitrary"),
    )(dbias, buckets)


def _stack_heads(pair, lo):
    return jnp.concatenate([jnp.where(lo, pair, 0.0), jnp.where(lo, 0.0, pair)], axis=0)


def _head_cols(pair, lo, reduce):
    fill = -jnp.inf if reduce is jnp.max else 0.0
    return jnp.concatenate([reduce(jnp.where(lo, pair, fill), axis=1, keepdims=True),
                            reduce(jnp.where(lo, fill, pair), axis=1, keepdims=True)], axis=0)


def _unstack_heads(x2, lo):
    return jnp.where(lo, x2[:BLK], x2[BLK:])


def _att_scores(q2, kk, bias2, first_ok):
    sc = _dot(q2, kk, NT) * (64 ** -0.5) + bias2
    return jnp.where(first_ok, sc, NEG)


DIL_TILE = 2048
DIL_COLS = ATT_W // LANES
DIL_GROUP = 4


def _dil_rows(dil, n, r, base=0):
    start = base + n * (BLK * dil) + r
    return pl.ds(start, BLK, stride=dil) if dil > 1 else pl.ds(start, BLK)


def _dil_in_specs(tile_of):
    cur = lambda col: pl.BlockSpec((DIL_TILE, LANES), lambda p, i: (tile_of(i), col * DIL_COLS + p))
    prev = lambda col: pl.BlockSpec((DIL_TILE, LANES), lambda p, i: (jnp.maximum(tile_of(i) - 1, 0), col * DIL_COLS + p))
    bias = pl.BlockSpec((len(DILATED), None, 2 * BLK, 2 * BLK), lambda p, i: (0, p, 0, 0))
    return [cur(0), prev(1), cur(1), prev(2), cur(2), bias]


def _pair_bias(biasm):
    return biasm.reshape(len(DILATED), DIL_COLS, 2 * BLK, 2 * BLK)


def _dil_fwd(proj, biasm, name="dil_fwd", gather=()):
    s = proj.shape[0]
    nt = s // DIL_TILE
    tt = DIL_TILE

    def body(q_ref, kp_ref, kc_ref, vp_ref, vc_ref, bias_ref, att_ref, lse_ref, k2, v2, ob, lb):
        t = pl.program_id(1)
        k2[0:tt, :] = kp_ref[...]
        k2[tt:2 * tt, :] = kc_ref[...]
        v2[0:tt, :] = vp_ref[...]
        v2[tt:2 * tt, :] = vc_ref[...]
        lo = lax.broadcasted_iota(jnp.int32, (BLK, LANES), 1) < 64
        kidx = lax.broadcasted_iota(jnp.int32, (2 * BLK, 2 * BLK), 1)
        for b, (_, dil) in enumerate(DILATED):
            for j0 in range(0, tt // BLK, DIL_GROUP):
                grp = range(DIL_GROUP)
                rn = [((j0 + i) % dil, (j0 + i) // dil) for i in grp]
                here = [_dil_rows(dil, n, r) for r, n in rn]
                cur = [_dil_rows(dil, n, r, tt) for r, n in rn]
                prev = [_dil_rows(dil, n - 1, r, tt) for r, n in rn]
                q2 = [_stack_heads(q_ref[here[i], :], lo).astype(BF16) for i in grp]
                kk = [jnp.concatenate([k2[prev[i], :], k2[cur[i], :]], axis=0).astype(BF16) for i in grp]
                vv = [jnp.concatenate([v2[prev[i], :], v2[cur[i], :]], axis=0).astype(BF16) for i in grp]
                sc = [_att_scores(q2[i], kk[i], bias_ref[b], (t > 0) | (rn[i][1] > 0) | (kidx >= BLK)) for i in grp]
                mx = [jnp.max(sc[i], axis=1, keepdims=True) for i in grp]
                pe = [jnp.exp(sc[i] - mx[i]) for i in grp]
                l = [jnp.sum(pe[i], axis=1, keepdims=True) for i in grp]
                o2 = [_dot(pe[i].astype(BF16), vv[i], NN) for i in grp]
                for i in grp:
                    ob.at[b][here[i], :] = _unstack_heads(o2[i] / l[i], lo)
                    lb.at[b][here[i], :] = _unstack_heads(jnp.broadcast_to(mx[i] + jnp.log(l[i]), (2 * BLK, LANES)), lo)
        l0, l1, l2 = lb[0], lb[1], lb[2]
        mx = jnp.maximum(jnp.maximum(l0, l1), l2)
        e0, e1, e2 = jnp.exp(l0 - mx), jnp.exp(l1 - mx), jnp.exp(l2 - mx)
        tot = e0 + e1 + e2
        att_ref[...] = (e0 * ob[0] + e1 * ob[1] + e2 * ob[2]) / tot
        lse_ref[...] = mx + jnp.log(tot)

    out = pl.BlockSpec((tt, LANES), lambda p, i: (i, p))
    return _call(
        body, name=name, grid=(DIL_COLS, nt), in_specs=_dil_in_specs(lambda i: i), out_specs=[out, out],
        out_shape=[jax.ShapeDtypeStruct((s, ATT_W), F32)] * 2, args=(proj, proj, proj, proj, proj, _pair_bias(biasm)),
        scratch_shapes=[pltpu.VMEM((2 * tt, LANES), F32), pltpu.VMEM((2 * tt, LANES), F32),
                        pltpu.VMEM((len(DILATED), tt, LANES), F32), pltpu.VMEM((len(DILATED), tt, LANES), F32)],
        sem=("parallel", "parallel"), gather=gather)


def _dil_bwd(proj, biasm, lse, att, dcat, name="dil_bwd"):
    s = proj.shape[0]
    nt = s // DIL_TILE
    tt = DIL_TILE
    nbr = len(DILATED)

    def body(q_ref, kp_ref, kc_ref, vp_ref, vc_ref, bias_ref, lse_ref, att_ref, datt_ref,
             dq_ref, dk_ref, dv_ref, dbias_ref, k2, v2, dqa, dka, dva, kcar, vcar):
        i = pl.program_id(1)
        t = nt - 1 - i
        k2[0:tt, :] = kp_ref[...]
        k2[tt:2 * tt, :] = kc_ref[...]
        v2[0:tt, :] = vp_ref[...]
        v2[tt:2 * tt, :] = vc_ref[...]

        @pl.when(i == 0)
        def _():
            kcar[...] = jnp.zeros_like(kcar)
            vcar[...] = jnp.zeros_like(vcar)
            dbias_ref[...] = jnp.zeros_like(dbias_ref)

        dqa[...] = jnp.zeros_like(dqa)
        dka[0:tt, :] = jnp.zeros((tt, LANES), F32)
        dva[0:tt, :] = jnp.zeros((tt, LANES), F32)
        dka[tt:2 * tt, :] = kcar[...]
        dva[tt:2 * tt, :] = vcar[...]
        lo = lax.broadcasted_iota(jnp.int32, (BLK, LANES), 1) < 64
        kidx = lax.broadcasted_iota(jnp.int32, (2 * BLK, 2 * BLK), 1)
        for b, (_, dil) in enumerate(DILATED):
            for j0 in range(0, tt // BLK, DIL_GROUP):
                grp = range(DIL_GROUP)
                rn = [((j0 + i) % dil, (j0 + i) // dil) for i in grp]
                here = [_dil_rows(dil, n, r) for r, n in rn]
                cur = [_dil_rows(dil, n, r, tt) for r, n in rn]
                prev = [_dil_rows(dil, n - 1, r, tt) for r, n in rn]
                dat = [datt_ref[here[i], :] for i in grp]
                q2 = [_stack_heads(q_ref[here[i], :], lo).astype(BF16) for i in grp]
                dom = [_stack_heads(dat[i], lo).astype(BF16) for i in grp]
                kk = [jnp.concatenate([k2[prev[i], :], k2[cur[i], :]], axis=0).astype(BF16) for i in grp]
                vv = [jnp.concatenate([v2[prev[i], :], v2[cur[i], :]], axis=0).astype(BF16) for i in grp]
                sc = [_att_scores(q2[i], kk[i], bias_ref[b], (t > 0) | (rn[i][1] > 0) | (kidx >= BLK)) for i in grp]
                dp = [_dot(dom[i], vv[i], NT) for i in grp]
                pr = [jnp.exp(sc[i] - _head_cols(lse_ref[here[i], :], lo, jnp.max)) for i in grp]
                ds = [pr[i] * (dp[i] - _head_cols(dat[i] * att_ref[here[i], :], lo, jnp.sum)) for i in grp]
                dsb = [(ds[i] * (64 ** -0.5)).astype(BF16) for i in grp]
                dq2 = [_dot(dsb[i], kk[i], NN) for i in grp]
                dk2 = [_dot(dsb[i], q2[i], TN) for i in grp]
                dv2 = [_dot(pr[i].astype(BF16), dom[i], TN) for i in grp]
                for i in grp:
                    dbias_ref[b] += ds[i]
                    dqa[here[i], :] += _unstack_heads(dq2[i], lo)
                    dka[prev[i], :] += dk2[i][:BLK]
                    dka[cur[i], :] += dk2[i][BLK:]
                    dva[prev[i], :] += dv2[i][:BLK]
                    dva[cur[i], :] += dv2[i][BLK:]
        dq_ref[...] = dqa[...].astype(BF16)
        dk_ref[...] = dka[tt:2 * tt, :].astype(BF16)
        dv_ref[...] = dva[tt:2 * tt, :].astype(BF16)
        kcar[...] = dka[0:tt, :]
        vcar[...] = dva[0:tt, :]

    rev = lambda i: nt - 1 - i
    out = pl.BlockSpec((tt, LANES), lambda p, i: (rev(i), p))
    two = lambda: pltpu.VMEM((2 * tt, LANES), F32)
    one = lambda: pltpu.VMEM((tt, LANES), F32)
    return pl.pallas_call(
        body, name=name, grid=(DIL_COLS, nt),
        in_specs=_dil_in_specs(rev) + [out, out, out],
        out_specs=[out, out, out, pl.BlockSpec((nbr, None, 2 * BLK, 2 * BLK), lambda p, i: (0, p, 0, 0))],
        out_shape=[jax.ShapeDtypeStruct((s, ATT_W), BF16)] * 3
        + [jax.ShapeDtypeStruct((nbr, DIL_COLS, 2 * BLK, 2 * BLK), F32)],
        scratch_shapes=[two(), two(), one(), two(), two(), one(), one()],
        compiler_params=_params("arbitrary", "arbitrary"),
    )(proj, proj, proj, proj, proj, _pair_bias(biasm), lse, att, dcat)


QK_COL0 = (3 * ATT_W) // ATT_W


HALO = 8


def _conv_shifted(prev8, cur, j):
    sh = CONV_K - 1 - j
    if sh == 0:
        return cur
    rolled = pltpu.roll(cur, sh, 0)
    row8 = lax.broadcasted_iota(jnp.int32, prev8.shape, 0)
    top = jnp.where(row8 < sh, pltpu.roll(prev8, sh, 0), rolled[:HALO])
    return top if cur.shape[0] == HALO else jnp.concatenate([top, rolled[HALO:]], axis=0)


def _conv_z(prev8, cur, w_ref, b_ref, taps=None):
    z = b_ref[...]
    for j in range(CONV_K):
        tap = _conv_shifted(prev8, cur, j)
        if taps is not None:
            taps.append(tap)
        z = z + tap * w_ref[j:j + 1, :]
    return z


def _silu_grad(z):
    sig = _sigmoid(z)
    return sig * (1.0 + z * (1.0 - sig))


def _conv_fwd(proj, conv_w, conv_b, name="conv_fwd", tm=512):
    s = proj.shape[0]
    w = ATT_W
    per = tm // HALO

    def body(prev_ref, cur_ref, w_ref, b_ref, o_ref):
        i = pl.program_id(1)
        prev8 = jnp.where(i > 0, prev_ref[...], 0.0)
        z = _conv_z(prev8, cur_ref[...], w_ref, b_ref)
        o_ref[...] = z * _sigmoid(z)

    return pl.pallas_call(
        body, name=name, grid=(2, s // tm),
        in_specs=[pl.BlockSpec((HALO, w), lambda j, i: (jnp.maximum(i * per - 1, 0), QK_COL0 + j)),
                  pl.BlockSpec((tm, w), lambda j, i: (i, QK_COL0 + j)),
                  pl.BlockSpec((CONV_K, w), lambda j, i: (0, j)),
                  pl.BlockSpec((1, w), lambda j, i: (0, j))],
        out_specs=pl.BlockSpec((tm, w), lambda j, i: (i, j)),
        out_shape=jax.ShapeDtypeStruct((s, 2 * ML_W), F32),
        compiler_params=_params("parallel", "parallel"),
    )(proj, proj, conv_w, conv_b)


def _conv_bwd(proj, dqk, conv_w, conv_b, name="conv_bwd", tm=512):
    s = proj.shape[0]
    w = ATT_W
    nt = s // tm
    per = tm // HALO

    def body(xp_ref, xc_ref, xn_ref, dc_ref, dn_ref, w_ref, b_ref, dx_ref, dw_ref, db_ref):
        i = pl.program_id(1)
        prev8 = jnp.where(i > 0, xp_ref[...], 0.0)
        cur = xc_ref[...]
        taps = []
        dzc = dc_ref[...] * _silu_grad(_conv_z(prev8, cur, w_ref, b_ref, taps))
        dzn8 = dn_ref[...] * _silu_grad(_conv_z(cur[tm - HALO:], xn_ref[...], w_ref, b_ref))
        dzn8 = jnp.where(i < nt - 1, dzn8, 0.0)
        row8 = lax.broadcasted_iota(jnp.int32, (HALO, w), 0)
        dx = dzc * w_ref[CONV_K - 1:CONV_K, :]
        for j in range(CONV_K - 1):
            sh = CONV_K - 1 - j
            rolled = pltpu.roll(dzc, tm - sh, 0)
            bottom = jnp.where(row8 >= HALO - sh, pltpu.roll(dzn8, HALO - sh, 0), rolled[tm - HALO:])
            dx = dx + jnp.concatenate([rolled[:tm - HALO], bottom], axis=0) * w_ref[j:j + 1, :]
        dx_ref[...] = dx

        @pl.when(i == 0)
        def _():
            dw_ref[...] = jnp.zeros_like(dw_ref)
            db_ref[...] = jnp.zeros_like(db_ref)

        for j in range(CONV_K):
            dw_ref[j:j + 1, :] += jnp.sum(dzc * taps[j], axis=0, keepdims=True)
        db_ref[...] += jnp.sum(dzc, axis=0, keepdims=True)

    last = s // HALO - 1
    halo_before = lambda col0: pl.BlockSpec((HALO, w), lambda j, i: (jnp.maximum(i * per - 1, 0), col0 + j))
    halo_after = lambda col0: pl.BlockSpec((HALO, w), lambda j, i: (jnp.minimum((i + 1) * per, last), col0 + j))
    tile = lambda col0: pl.BlockSpec((tm, w), lambda j, i: (i, col0 + j))
    return pl.pallas_call(
        body, name=name, grid=(2, nt),
        in_specs=[halo_before(QK_COL0), tile(QK_COL0), halo_after(QK_COL0), tile(0), halo_after(0),
                  pl.BlockSpec((CONV_K, w), lambda j, i: (0, j)), pl.BlockSpec((1, w), lambda j, i: (0, j))],
        out_specs=[tile(0), pl.BlockSpec((CONV_K, w), lambda j, i: (0, j)),
                   pl.BlockSpec((1, w), lambda j, i: (0, j))],
        out_shape=[jax.ShapeDtypeStruct((s, 2 * ML_W), F32), jax.ShapeDtypeStruct((CONV_K, 2 * ML_W), F32),
                   jax.ShapeDtypeStruct((1, 2 * ML_W), F32)],
        compiler_params=_params("parallel", "arbitrary"),
    )(proj, proj, proj, dqk, dqk, conv_w, conv_b)


def _bf16_mm(dims_fwd):
    @jax.custom_vjp
    def mm(a, b):
        return _dot(a.astype(BF16), b.astype(BF16), dims_fwd)

    def fwd(a, b):
        return mm(a, b), (a, b)

    def bwd(res, g):
        a, b = res
        if dims_fwd is NN:
            return _mm_nt(g, b), _mm_tn(a, g)
        if dims_fwd is NT:
            return _mm_nn(g, b), _mm_tn(g, a)
        return _mm_nt(b, g), _mm_nn(a, g)

    mm.defvjp(fwd, bwd)
    return mm


_mm_nn = _bf16_mm(NN)
_mm_nt = _bf16_mm(NT)
_mm_tn = _bf16_mm(TN)


def _tri(lower):
    r = lax.broadcasted_iota(jnp.int32, (CHUNK, CHUNK), 0)
    c = lax.broadcasted_iota(jnp.int32, (CHUNK, CHUNK), 1)
    return ((r >= c) if lower else (r <= c)).astype(F32)


@jax.custom_vjp
def _cumsum_rows(x):
    return lax.dot_general(_tri(True), x, NN, precision=lax.Precision.HIGHEST, preferred_element_type=F32)


def _cumsum_fwd(x):
    return _cumsum_rows(x), None


def _cumsum_bwd(_, g):
    return (lax.dot_general(_tri(False), g, NN, precision=lax.Precision.HIGHEST, preferred_element_type=F32),)


_cumsum_rows.defvjp(_cumsum_fwd, _cumsum_bwd)


def _abs(x):
    return jnp.where(x >= 0, x, -x)


def _log_sigmoid(x):
    return jnp.minimum(x, 0.0) - jnp.log(1.0 + jnp.exp(-_abs(x)))


def _pick_col(x, lane):
    sel = lax.broadcasted_iota(jnp.int32, x.shape, 1) == lane
    return jnp.sum(jnp.where(sel, x, 0.0), axis=1, keepdims=True)


def _pick_row(x, r):
    sel = lax.broadcasted_iota(jnp.int32, x.shape, 0) == r
    return jnp.sum(jnp.where(sel, x, 0.0), axis=0, keepdims=True)


def _mlstm_chunk(qs, ks, vs, oms, gates, gate_bias, mlg, cs, ns, ms):
    gb = gates + gate_bias
    cum = _cumsum_rows(_log_sigmoid(gb))
    gbt = gb.T
    cumt = cum.T
    causal = lax.broadcasted_iota(jnp.int32, (CHUNK, CHUNK), 0) >= lax.broadcasted_iota(jnp.int32, (CHUNK, CHUNK), 1)
    hd = range(ML_HEADS)
    k = [ks[h] * (ML_HD ** -0.5) for h in hd]
    ig_col = [_pick_col(gb, h) for h in hd]
    ig_row = [_pick_row(gbt, h) for h in hd]
    b_col = [_pick_col(cum, ML_HEADS + h) for h in hd]
    b_row = [_pick_row(cumt, ML_HEADS + h) for h in hd]
    g = [_pick_row(b_col[h], CHUNK - 1) for h in hd]
    a = [g[h] - b_col[h] + ig_col[h] for h in hd]
    m_loc = [jnp.max(a[h], axis=0, keepdims=True) for h in hd]
    wa = [jnp.exp(a[h] - m_loc[h]) for h in hd]
    d_log = [jnp.where(causal, b_col[h] - b_row[h] + ig_row[h], -jnp.inf) for h in hd]
    e_log = [b_col[h] + ms[h] for h in hd]
    m_t = [jnp.maximum(e_log[h], jnp.max(d_log[h], axis=1, keepdims=True)) for h in hd]
    d_w = [jnp.exp(d_log[h] - m_t[h]) for h in hd]
    e_w = [jnp.exp(e_log[h] - m_t[h]) for h in hd]
    qk = [_mm_nt(qs[h], k[h]) for h in hd]
    qc = [_mm_nt(qs[h], cs[h]) for h in hd]
    c_loc = [_mm_tn(wa[h] * vs[h], k[h]) for h in hd]
    s_qk = [qk[h] * d_w[h] for h in hd]
    sv = [_mm_nn(s_qk[h], vs[h]) for h in hd]
    n_loc = [jnp.sum(wa[h] * k[h], axis=0, keepdims=True) for h in hd]
    m_out = [jnp.maximum(g[h] + ms[h], m_loc[h]) for h in hd]
    sp = [jnp.exp(g[h] + ms[h] - m_out[h]) for h in hd]
    sl = [jnp.exp(m_loc[h] - m_out[h]) for h in hd]
    c_out = [sp[h] * cs[h] + sl[h] * c_loc[h] for h in hd]
    n_out = [sp[h] * ns[h] + sl[h] * n_loc[h] for h in hd]
    num = [e_w[h] * qc[h] + sv[h] for h in hd]
    den = [e_w[h] * jnp.sum(qs[h] * ns[h], axis=1, keepdims=True) + jnp.sum(s_qk[h], axis=1, keepdims=True) for h in hd]
    hg = [_sigmoid(oms[h]) * (num[h] / jnp.maximum(_abs(den[h]), jnp.exp(-m_t[h]))) for h in hd]
    mu = [jnp.mean(hg[h], axis=1, keepdims=True) for h in hd]
    hc = [hg[h] - mu[h] for h in hd]
    var = [jnp.mean(hc[h] * hc[h], axis=1, keepdims=True) for h in hd]
    ys = [hc[h] * lax.rsqrt(var[h] + LN_EPS) * mlg[h] for h in hd]
    return ys, c_out, n_out, m_out


V_COL = 5
O_COL = 6
ML_SUB = 1


def _mlstm_fwd(qk, proj, gates, gate_bias, mlg, name="mlstm_fwd", gather=()):
    s = qk.shape[0]
    nc = s // CHUNK

    def body(q_ref, k_ref, v_ref, o_ref, g_ref, gb_ref, mlg_ref, y_ref, cp_ref, np_ref, mp_ref, c_s, n_s, m_s):
        ci = pl.program_id(0)

        @pl.when(ci == 0)
        def _():
            c_s[...] = jnp.zeros_like(c_s)
            n_s[...] = jnp.zeros_like(n_s)
            m_s[...] = jnp.zeros_like(m_s)

        for sub in range(ML_SUB):
            rows = slice(CHUNK * sub, CHUNK * (sub + 1))
            hs = lambda ref: [ref[rows, LANES * h:LANES * (h + 1)] for h in range(ML_HEADS)]
            cp_ref[sub] = c_s[...]
            np_ref[sub] = n_s[...]
            mp_ref[sub] = m_s[...]
            ys, c_new, n_new, m_new = _mlstm_chunk(
                hs(q_ref), hs(k_ref), hs(v_ref), hs(o_ref), g_ref[rows, :], gb_ref[...],
                [mlg_ref[:, LANES * h:LANES * (h + 1)] for h in range(ML_HEADS)],
                [c_s[h] for h in range(ML_HEADS)], [n_s[h:h + 1, :] for h in range(ML_HEADS)],
                [m_s[h:h + 1, 0:1] for h in range(ML_HEADS)])
            for h in range(ML_HEADS):
                y_ref[rows, LANES * h:LANES * (h + 1)] = ys[h]
                c_s[h] = c_new[h]
                n_s[h:h + 1, :] = n_new[h]
                m_s[h:h + 1, :] = jnp.broadcast_to(m_new[h], (1, LANES))

    blk = lambda col: pl.BlockSpec((ML_SUB * CHUNK, ML_W), lambda ci: (ci, col))
    vec = lambda w: pl.BlockSpec((1, w), lambda ci: (0, 0))
    return _call(
        body, name=name, grid=(nc // ML_SUB,), args=(qk, qk, proj, proj, gates, gate_bias, mlg), sem=("arbitrary",),
        gather=gather,
        in_specs=[blk(0), blk(1), blk(V_COL), blk(O_COL), pl.BlockSpec((ML_SUB * CHUNK, LANES), lambda ci: (ci, 0)),
                  vec(LANES), vec(ML_W)],
        out_specs=[blk(0), pl.BlockSpec((ML_SUB, ML_HEADS, ML_HD, ML_HD), lambda ci: (ci, 0, 0, 0)),
                   pl.BlockSpec((ML_SUB, 8, LANES), lambda ci: (ci, 0, 0)),
                   pl.BlockSpec((ML_SUB, 8, LANES), lambda ci: (ci, 0, 0))],
        out_shape=[jax.ShapeDtypeStruct((s, ML_W), F32), jax.ShapeDtypeStruct((nc, ML_HEADS, ML_HD, ML_HD), F32),
                   jax.ShapeDtypeStruct((nc, 8, LANES), F32), jax.ShapeDtypeStruct((nc, 8, LANES), F32)],
        scratch_shapes=[pltpu.VMEM((ML_HEADS, ML_HD, ML_HD), F32), pltpu.VMEM((8, LANES), F32),
                        pltpu.VMEM((8, LANES), F32)])


def _mlstm_bwd(qk, proj, gates, gate_bias, mlg, cprev, nprev, mprev, dy, name="mlstm_bwd", exchange=()):
    s = qk.shape[0]
    nc = s // CHUNK

    def body(q_ref, k_ref, v_ref, o_ref, g_ref, gb_ref, mlg_ref, cp_ref, np_ref, mp_ref, dy_ref,
             dqk_ref, dv_ref, do_ref, dg_ref, dgb_ref, dmlg_ref, dc_s, dn_s, dm_s, gb8, mg8):
        ci = pl.program_id(0)

        @pl.when(ci == 0)
        def _():
            dc_s[...] = jnp.zeros_like(dc_s)
            dn_s[...] = jnp.zeros_like(dn_s)
            dm_s[...] = jnp.zeros_like(dm_s)
            gb8[...] = jnp.zeros_like(gb8)
            mg8[...] = jnp.zeros_like(mg8)

        for sub in reversed(range(ML_SUB)):
            rows = slice(CHUNK * sub, CHUNK * (sub + 1))
            hs = lambda ref: [ref[rows, LANES * h:LANES * (h + 1)] for h in range(ML_HEADS)]
            prim = (hs(q_ref), hs(k_ref), hs(v_ref), hs(o_ref), g_ref[rows, :], gb_ref[...],
                    [mlg_ref[:, LANES * h:LANES * (h + 1)] for h in range(ML_HEADS)],
                    [cp_ref[sub, h] for h in range(ML_HEADS)], [np_ref[sub, h:h + 1, :] for h in range(ML_HEADS)],
                    [mp_ref[sub, h:h + 1, 0:1] for h in range(ML_HEADS)])
            _, vjp = jax.vjp(_mlstm_chunk, *prim)
            cot = (hs(dy_ref), [dc_s[h] for h in range(ML_HEADS)], [dn_s[h:h + 1, :] for h in range(ML_HEADS)],
                   [dm_s[h:h + 1, 0:1] for h in range(ML_HEADS)])
            dqs, dks, dvs, dos, dg, dgb, dmlg, dcs, dns, dms = vjp(cot)
            dg_ref[rows, :] = dg
            gb8[0:1, :] += dgb
            for h in range(ML_HEADS):
                sl = slice(LANES * h, LANES * (h + 1))
                dqk_ref[rows, sl] = dqs[h]
                dqk_ref[rows, ML_W + LANES * h:ML_W + LANES * (h + 1)] = dks[h]
                dv_ref[rows, sl] = dvs[h]
                do_ref[rows, sl] = dos[h]
                mg8[0:1, sl] += dmlg[h]
                dc_s[h] = dcs[h]
                dn_s[h:h + 1, :] = dns[h]
                dm_s[h:h + 1, :] = jnp.broadcast_to(dms[h], (1, LANES))

        @pl.when(ci == nb - 1)
        def _():
            dgb_ref[...] = gb8[0:1, :]
            dmlg_ref[...] = mg8[0:1, :]

    nb = nc // ML_SUB
    rev = lambda ci: nb - 1 - ci
    blk = lambda col: pl.BlockSpec((ML_SUB * CHUNK, ML_W), lambda ci: (rev(ci), col))
    vec = lambda w: pl.BlockSpec((1, w), lambda ci: (0, 0))
    st8 = pl.BlockSpec((ML_SUB, 8, LANES), lambda ci: (rev(ci), 0, 0))
    gsp = pl.BlockSpec((ML_SUB * CHUNK, LANES), lambda ci: (rev(ci), 0))
    return _call(
        body, name=name, grid=(nb,), sem=("arbitrary",), exchange=exchange,
        args=(qk, qk, proj, proj, gates, gate_bias, mlg, cprev, nprev, mprev, dy),
        in_specs=[blk(0), blk(1), blk(V_COL), blk(O_COL), gsp, vec(LANES), vec(ML_W),
                  pl.BlockSpec((ML_SUB, ML_HEADS, ML_HD, ML_HD), lambda ci: (rev(ci), 0, 0, 0)), st8, st8, blk(1)],
        out_specs=[pl.BlockSpec((ML_SUB * CHUNK, 2 * ML_W), lambda ci: (rev(ci), 0)), blk(0), blk(0), gsp, vec(LANES),
                   vec(ML_W)],
        out_shape=[jax.ShapeDtypeStruct((s, 2 * ML_W), F32),
                   jax.ShapeDtypeStruct((s, ML_W), F32), jax.ShapeDtypeStruct((s, ML_W), F32),
                   jax.ShapeDtypeStruct((s, LANES), F32), jax.ShapeDtypeStruct((1, LANES), F32),
                   jax.ShapeDtypeStruct((1, ML_W), F32)],
        scratch_shapes=[pltpu.VMEM((ML_HEADS, ML_HD, ML_HD), F32), pltpu.VMEM((8, LANES), F32),
                        pltpu.VMEM((8, LANES), F32), pltpu.VMEM((8, LANES), F32), pltpu.VMEM((8, ML_W), F32)])


def _xattn_tile(qs, ks, vs):
    hd = range(XA_HEADS)
    sc = [_mm_nt(qs[h], ks[h]) * (XA_HD ** -0.5) for h in hd]
    mx = [lax.stop_gradient(jnp.max(sc[h], axis=1, keepdims=True)) for h in hd]
    pe = [jnp.exp(sc[h] - mx[h]) for h in hd]
    pn = [pe[h] / jnp.sum(pe[h], axis=1, keepdims=True) for h in hd]
    return [_mm_nn(pn[h], vs[h]) for h in hd]


def _xa_heads(ref):
    return [ref[:, XA_HD * h:XA_HD * (h + 1)] for h in range(XA_HEADS)]


def _xattn_fwd(q, kv, name="xattn_fwd", tm=512):
    s, d = q.shape

    def body(q_ref, k_ref, v_ref, o_ref):
        outs = _xattn_tile(_xa_heads(q_ref), _xa_heads(k_ref), _xa_heads(v_ref))
        for h in range(XA_HEADS):
            o_ref[:, XA_HD * h:XA_HD * (h + 1)] = outs[h]

    row = pl.BlockSpec((tm, d), lambda i: (i, 0))
    return pl.pallas_call(
        body, name=name, grid=(s // tm,),
        in_specs=[row, pl.BlockSpec((MEM_LEN, d), lambda i: (0, 0)), pl.BlockSpec((MEM_LEN, d), lambda i: (0, 1))],
        out_specs=row, out_shape=jax.ShapeDtypeStruct((s, d), F32),
        compiler_params=_params("parallel"),
    )(q, kv, kv)


def _xattn_bwd(q, kv, do, name="xattn_bwd", tm=512):
    s, d = q.shape

    def body(q_ref, k_ref, v_ref, do_ref, dq_ref, dkv_ref):
        i = pl.program_id(0)
        _, vjp = jax.vjp(_xattn_tile, _xa_heads(q_ref), _xa_heads(k_ref), _xa_heads(v_ref))
        dqs, dks, dvs = vjp(_xa_heads(do_ref))

        @pl.when(i == 0)
        def _():
            dkv_ref[...] = jnp.zeros_like(dkv_ref)

        for h in range(XA_HEADS):
            sl = slice(XA_HD * h, XA_HD * (h + 1))
            dq_ref[:, sl] = dqs[h]
            dkv_ref[:, sl] += dks[h]
            dkv_ref[:, d + XA_HD * h:d + XA_HD * (h + 1)] += dvs[h]

    row = pl.BlockSpec((tm, d), lambda i: (i, 0))
    return pl.pallas_call(
        body, name=name, grid=(s // tm,),
        in_specs=[row, pl.BlockSpec((MEM_LEN, d), lambda i: (0, 0)), pl.BlockSpec((MEM_LEN, d), lambda i: (0, 1)), row],
        out_specs=[row, pl.BlockSpec((MEM_LEN, 2 * d), lambda i: (0, 0))],
        out_shape=[jax.ShapeDtypeStruct((s, d), F32), jax.ShapeDtypeStruct((MEM_LEN, 2 * d), F32)],
        compiler_params=_params("arbitrary"),
    )(q, kv, kv, do)


def _loss_head(y, target, name="loss_head", tm=1024):
    s, d = y.shape
    nt = s // tm

    def body(y_ref, t_ref, dy_ref, loss_ref, acc):
        i = pl.program_id(0)
        err = y_ref[...] - t_ref[...]
        dy_ref[...] = err * (1.0 / d)

        @pl.when(i == 0)
        def _():
            acc[...] = jnp.zeros_like(acc)

        acc[...] += _rowsum8(err * err)

        @pl.when(i == nt - 1)
        def _():
            tot = jnp.sum(jnp.sum(acc[...], axis=0, keepdims=True), axis=1, keepdims=True)
            loss_ref[...] = jnp.broadcast_to(tot * (0.5 / d), (1, LANES))

    row = pl.BlockSpec((tm, d), lambda i: (i, 0))
    return pl.pallas_call(
        body, name=name, grid=(nt,),
        in_specs=[row, row], out_specs=[row, pl.BlockSpec((1, LANES), lambda i: (0, 0))],
        out_shape=[jax.ShapeDtypeStruct((s, d), F32), jax.ShapeDtypeStruct((1, LANES), F32)],
        scratch_shapes=[pltpu.VMEM((8, d), F32)],
        compiler_params=_params("arbitrary"),
    )(y, target)


def _adam2d(recv, w, m, v, name, layer=None):
    rows, cols = w.shape[-2:]
    fits = [t for t in range(16, rows + 1, 16) if rows % t == 0 and t * cols <= 128 * 1024]
    tr = max(fits) if fits else rows

    def body(r_ref, w_ref, m_ref, v_ref, g_ref, d_ref, mo_ref, vo_ref):
        g = r_ref[0].astype(F32)
        for j in range(1, N_DEV):
            g = g + r_ref[j].astype(F32)
        mn = ADAM_B1 * m_ref[...] + (1.0 - ADAM_B1) * g
        vn = ADAM_B2 * v_ref[...] + (1.0 - ADAM_B2) * jnp.square(g)
        m_hat = mn / (1.0 - ADAM_B1 ** ADAM_STEP)
        v_hat = vn / (1.0 - ADAM_B2 ** ADAM_STEP)
        g_ref[...] = g
        d_ref[...] = -ADAM_LR * (m_hat / (jnp.sqrt(v_hat) + ADAM_EPS) + ADAM_WD * w_ref[...])
        mo_ref[...] = mn
        vo_ref[...] = vn

    row = pl.BlockSpec((tr, cols), lambda i: (i, 0))
    if layer is None:
        wspec = row
    else:
        wspec = pl.BlockSpec((None, None, tr, cols), lambda i: (0, layer, i, 0))
    return pl.pallas_call(
        body, name=name, grid=(rows // tr,),
        in_specs=[pl.BlockSpec((N_DEV, tr, cols), lambda i: (0, i, 0)), wspec, wspec, wspec],
        out_specs=[row] * 4, out_shape=[jax.ShapeDtypeStruct((rows, cols), F32)] * 4,
        compiler_params=_params("parallel"),
    )(recv, w, m, v)


WEIGHTS = ("rel_bias", "ln_g", "ln_b", "ffn_w_gate", "ffn_w_up", "ffn_w_down", "w_in", "conv_w", "conv_b",
           "ig_bias", "fg_bias", "ml_norm_g", "w_out", "xq_w", "xkv_w", "xo_w")
SMALL = ("rel_bias", "ln_g", "ln_b", "conv_w", "conv_b", "ig_bias", "fg_bias", "ml_norm_g")
SMALL_SHAPES = {
    "rel_bias": (REL_BUCKETS, ATT_HEADS), "ln_g": (1, 4, LANES), "ln_b": (1, 4, LANES), "conv_w": (1, CONV_K, LANES),
    "conv_b": (1, 2 * ML_W), "ig_bias": (1, ML_HEADS), "fg_bias": (1, ML_HEADS), "ml_norm_g": (1, ML_W),
}
SMALL_ROWS = 8


def _pack_small(parts, lead=()):
    out = []
    for p in parts:
        p = jnp.pad(p, [(0, 0)] * len(lead) + [(0, SMALL_ROWS * LANES - p.shape[-1])])
        out.append(p.reshape(lead + (SMALL_ROWS, LANES)))
    return jnp.concatenate(out, axis=len(lead))


def _unpack_small(flat):
    out = {}
    for i, n in enumerate(SMALL):
        cnt = int(np.prod(SMALL_SHAPES[n]))
        out[n] = flat[SMALL_ROWS * i:SMALL_ROWS * (i + 1)].reshape(-1)[:cnt].reshape(SMALL_SHAPES[n])
    return out


def _split8(full, axis):
    shp = full.shape
    t = full.reshape(shp[:axis] + (N_DEV, shp[axis] // N_DEV) + shp[axis + 1:])
    return jnp.moveaxis(t, axis, 0).reshape(N_DEV, -1)


def _rep8(full):
    return jnp.broadcast_to(full.reshape(1, -1), (N_DEV, full.size))


def kernel(x, mem, rel_bias, ln_g, ln_b, ffn_w_gate, ffn_w_up, ffn_w_down, w_in, conv_w, conv_b, ig_bias, fg_bias, ml_norm_g, w_out, xq_w, xkv_w, xo_w, loss_target, m_rel_bias, m_ln_g, m_ln_b, m_ffn_w_gate, m_ffn_w_up, m_ffn_w_down, m_w_in, m_conv_w, m_conv_b, m_ig_bias, m_fg_bias, m_ml_norm_g, m_w_out, m_xq_w, m_xkv_w, m_xo_w, v_rel_bias, v_ln_g, v_ln_b, v_ffn_w_gate, v_ffn_w_up, v_ffn_w_down, v_w_in, v_conv_w, v_conv_b, v_ig_bias, v_fg_bias, v_ml_norm_g, v_w_out, v_xq_w, v_xkv_w, v_xo_w):
    w_tree = dict(rel_bias=rel_bias, ln_g=ln_g, ln_b=ln_b, ffn_w_gate=ffn_w_gate, ffn_w_up=ffn_w_up,
                  ffn_w_down=ffn_w_down, w_in=w_in, conv_w=conv_w, conv_b=conv_b, ig_bias=ig_bias, fg_bias=fg_bias,
                  ml_norm_g=ml_norm_g, w_out=w_out, xq_w=xq_w, xkv_w=xkv_w, xo_w=xo_w)
    m_tree = dict(rel_bias=m_rel_bias, ln_g=m_ln_g, ln_b=m_ln_b, ffn_w_gate=m_ffn_w_gate, ffn_w_up=m_ffn_w_up,
                  ffn_w_down=m_ffn_w_down, w_in=m_w_in, conv_w=m_conv_w, conv_b=m_conv_b, ig_bias=m_ig_bias,
                  fg_bias=m_fg_bias, ml_norm_g=m_ml_norm_g, w_out=m_w_out, xq_w=m_xq_w, xkv_w=m_xkv_w, xo_w=m_xo_w)
    v_tree = dict(rel_bias=v_rel_bias, ln_g=v_ln_g, ln_b=v_ln_b, ffn_w_gate=v_ffn_w_gate, ffn_w_up=v_ffn_w_up,
                  ffn_w_down=v_ffn_w_down, w_in=v_w_in, conv_w=v_conv_w, conv_b=v_conv_b, ig_bias=v_ig_bias,
                  fg_bias=v_fg_bias, ml_norm_g=v_ml_norm_g, w_out=v_w_out, xq_w=v_xq_w, xkv_w=v_xkv_w, xo_w=v_xo_w)
    x0 = x[0]
    pad_ff = FF_PAD - FF_SHARD
    bf = lambda t: t.astype(BF16)

    pad_rows = lambda t: jnp.pad(t, ((0, pad_ff), (0, 0)))
    ffn_shards = [(pad_rows(bf(ffn_w_gate[0, l]).T), pad_rows(bf(ffn_w_up[0, l]).T), pad_rows(bf(ffn_w_down[0, l])))
                  for l in range(2)]
    pairs = lambda t: t.reshape(N_PAIR, FF_PAIR, D_MODEL)
    w_in_shard = jnp.pad(bf(w_in[0]), ((0, 0), (0, ATT_W - W_IN_SHARD)))
    small_shard = jnp.concatenate([ln_g[0], ln_b[0], conv_w[0], jnp.zeros((4, LANES), F32)], axis=0)
    gate_bias = jnp.pad(jnp.concatenate([ig_bias, fg_bias], axis=1), ((0, 0), (0, LANES - 2 * ML_HEADS)))
    buckets = _bucket_tables()

    wg0, wu0, wd0, small_all = _gather_two_level("ffn1_weights_gather", ffn_shards[0] + (small_shard,))
    wg0, wu0, wd0 = pairs(wg0), pairs(wu0), pairs(wd0)
    unshard = lambda t: jnp.moveaxis(t, 0, 1).reshape(4, D_MODEL)
    ln_g_full, ln_b_full, conv_w_full = unshard(small_all[:, 0:4]), unshard(small_all[:, 4:8]), unshard(small_all[:, 8:12])
    lng = lambda i: ln_g_full[i:i + 1]
    lnb = lambda i: ln_b_full[i:i + 1]

    u0, x1, a0, b0, win_all, wout_all, xq_all, xo_all, xkv_all = _ffn_fwd(
        x0, wg0, wu0, wd0, lng(0), lnb(0), "ffn1_fwd",
        gather=(w_in_shard, bf(w_out[0]), bf(xq_w[0]), bf(xo_w[0]), bf(xkv_w[0])))
    w_in_full = jnp.moveaxis(win_all[:, :, :W_IN_SHARD], 0, 1).reshape(D_MODEL, W_IN)
    w_main = w_in_full[:, :W_IN_MAIN]
    w_gate_cols = jnp.pad(w_in_full[:, W_IN_MAIN:], ((0, 0), (0, LANES - 2 * ML_HEADS)))
    w_out_full = wout_all.reshape(D_MODEL, D_MODEL)
    xq_full = xq_all.reshape(D_MODEL, D_MODEL)
    xo_full = xo_all.reshape(D_MODEL, D_MODEL)

    proj, wg1 = _matmul(x1, w_main, "nn", "proj_fwd", tn=W_IN_MAIN // 2, tk=D_MODEL, gather=(ffn_shards[1][0],))
    gates, = _matmul(x1, w_gate_cols, "nn", "gates_fwd", tk=D_MODEL)
    biasm = _bias_fwd(rel_bias, buckets)
    att, lse, wd1 = _dil_fwd(proj, biasm, gather=(ffn_shards[1][2],))
    qk = _conv_fwd(proj, conv_w_full, conv_b)
    y_m, c_prev, n_prev, m_prev, wu1 = _mlstm_fwd(qk, proj, gates, gate_bias, ml_norm_g, gather=(ffn_shards[1][1],))
    u1, x2 = _matmul_resid_ln((att, y_m), w_out_full, x1, lng(1), lnb(1), "w_out_fwd")
    q_x, = _matmul(x2, xq_full, "nn", "xq_fwd", tn=D_MODEL, tk=D_MODEL)
    kv, = _matmul(mem[0], xkv_all, "nn", "xkv_fwd", tk=D_MODEL)
    o_x = _xattn_fwd(q_x, kv)
    u2, x3 = _matmul_resid_ln((o_x,), xo_full, x2, lng(2), lnb(2), "xo_fwd")
    wg1, wu1, wd1 = pairs(wg1), pairs(wu1), pairs(wd1)
    u3, x4, a3, b3 = _ffn_fwd(x3, wg1, wu1, wd1, lng(3), lnb(3), "ffn2_fwd")
    dx4, loss_row = _loss_head(x4, loss_target[0])

    dx3, xb, df, da, db, hh, dg3, db3 = _ffn_bwd_x(dx4, u3, x3, wg1, wu1, wd1, lng(3), a3, b3, "ffn2_bwd_x")
    ffn2_send = (_ffn_bwd_w(xb, da, "ffn2_bwd_wg", down=False)[0], _ffn_bwd_w(xb, db, "ffn2_bwd_wu", down=False)[0],
                 _ffn_bwd_w(df, hh, "ffn2_bwd_wd", down=True)[0])

    du2, do_x, dg2, db2 = _ln_bwd(dx3, u2, lng(2), xo_full, "xattn_ln_bwd")
    g_xo, = _matmul(o_x, du2, "tn", "xo_bwd_w", tm=D_MODEL, tn=D_MODEL, out_dtype=BF16)
    dq_x, dkv = _xattn_bwd(q_x, kv, do_x)
    g_xq, = _matmul(x2, dq_x, "tn", "xq_bwd_w", tm=D_MODEL, tn=D_MODEL, out_dtype=BF16)
    g_xkv, = _matmul(mem[0], dkv, "tn", "xkv_bwd_w", tm=D_MODEL, tn=2 * D_MODEL // N_DEV, tk=MEM_LEN,
                     out_dtype=BF16, blocked_out=True)
    dx2, = _matmul(dq_x, xq_full, "nt", "xq_bwd_x", tn=D_MODEL, tk=D_MODEL, add=du2, add_scale=ALPHA)

    du1, dcat, dg1, db1 = _ln_bwd(dx2, u1, lng(1), w_out_full, "mixer_ln_bwd")
    g_w_out = jnp.concatenate(
        [_matmul(half, du1, "tn", f"w_out_bwd_w_{i}", tn=D_MODEL, out_dtype=BF16)[0] for i, half in enumerate((att, y_m))],
        axis=0)
    dqk, dv_m, do_m, dgates, dgate_bias, g_mlg, *ffn2_recv = _mlstm_bwd(
        qk, proj, gates, gate_bias, ml_norm_g, c_prev, n_prev, m_prev, dcat, exchange=tuple(ffn2_send))
    dqk_pre, g_conv_w, g_conv_b = _conv_bwd(proj, dqk, conv_w_full, conv_b)
    dq_a, dk_a, dv_a, dbias = _dil_bwd(proj, biasm, lse, att, dcat)
    g_rel = _bias_bwd(dbias.reshape(biasm.shape), buckets)[:, :ATT_HEADS]
    dproj = jnp.concatenate([dq_a, dk_a, dv_a, bf(dqk_pre), bf(dv_m), bf(do_m)], axis=1)
    g_w_main, = _matmul(x1, dproj, "tn", "proj_bwd_w", tm=D_MODEL, tn=W_IN_MAIN // 2, tk=1024, out_dtype=BF16)
    g_w_gates, = _matmul(x1, dgates, "tn", "gates_bwd_w", tm=D_MODEL, out_dtype=BF16)
    g_w_in = jnp.concatenate([g_w_main, g_w_gates[:, :2 * ML_HEADS]], axis=1)
    dx1, = _matmul(dproj, w_main, "nt", "proj_bwd_x", tn=D_MODEL, tk=W_IN_MAIN // 2, add=du1, add_scale=ALPHA)
    dx1, = _matmul(dgates, w_gate_cols, "nt", "gates_bwd_x", tn=D_MODEL, add=dx1)

    rows8 = lambda t: t.reshape(N_DEV, D_MODEL // N_DEV, D_MODEL)
    mid_send = (rows8(g_xo), rows8(g_xq), g_xkv, rows8(g_w_out),
                jnp.moveaxis(g_w_in.reshape(D_MODEL, N_DEV, W_IN_SHARD), 1, 0))
    dx0, xb, df, da, db, hh, dg0, db0, r_xo, r_xq, r_xkv, r_w_out, r_w_in = _ffn_bwd_x(
        dx1, u0, x0, wg0, wu0, wd0, lng(0), a0, b0, "ffn1_bwd_x", exchange=mid_send)
    small_blocks = {
        "rel_bias": _rep8(g_rel),
        "ln_g": _split8(jnp.concatenate([dg0, dg1, dg2, dg3], axis=0), 1),
        "ln_b": _split8(jnp.concatenate([db0, db1, db2, db3], axis=0), 1),
        "conv_w": _split8(g_conv_w, 1),
        "conv_b": _rep8(g_conv_b),
        "ig_bias": _rep8(dgate_bias[:, :ML_HEADS]),
        "fg_bias": _rep8(dgate_bias[:, ML_HEADS:2 * ML_HEADS]),
        "ml_norm_g": _rep8(g_mlg),
    }
    small_send = _pack_small([small_blocks[n] for n in SMALL], lead=(N_DEV,))
    g_wg, r_small = _ffn_bwd_w(xb, da, "ffn1_bwd_wg", down=False, exchange=(small_send,))
    g_wu, r_wg = _ffn_bwd_w(xb, db, "ffn1_bwd_wu", down=False, exchange=(g_wg,))
    g_wd, r_wu = _ffn_bwd_w(df, hh, "ffn1_bwd_wd", down=True, exchange=(g_wu,))
    r_wd, = _exchange_only("ffn1_grads_exchange", exchange=(g_wd,))
    ffn1_recv = [r_wg, r_wu, r_wd]

    res = {}
    for i, n in enumerate(("ffn_w_gate", "ffn_w_up", "ffn_w_down")):
        per_layer = [_adam2d(r[i], w_tree[n], m_tree[n], v_tree[n], f"adamw_{n}_{l}", layer=l)
                     for l, r in enumerate((ffn1_recv, ffn2_recv))]
        res[n] = [jnp.stack([per_layer[0][j], per_layer[1][j]])[None] for j in range(4)]
    for n, r in (("w_in", r_w_in), ("w_out", r_w_out), ("xq_w", r_xq), ("xkv_w", r_xkv), ("xo_w", r_xo)):
        res[n] = [t[None] for t in _adam2d(r, w_tree[n][0], m_tree[n][0], v_tree[n][0], f"adamw_{n}")]
    pack = lambda tree: _pack_small([tree[n].reshape(-1) for n in SMALL])
    small = [_unpack_small(t) for t in _adam2d(r_small, pack(w_tree), pack(m_tree), pack(v_tree), "adamw_small")]
    for n in SMALL:
        res[n] = [small[j][n] for j in range(4)]

    loss = lax.psum(loss_row[0, 0], ("x", "y", "c"))
    return (loss, dx0[None], *[res[n][0] for n in WEIGHTS], *[res[n][1] for n in WEIGHTS],
            *[res[n][2] for n in WEIGHTS], *[res[n][3] for n in WEIGHTS])
```

```python
import math

import numpy as np
import jax
import jax.numpy as jnp
from jax import lax
from jax.experimental import pallas as pl
from jax.experimental.pallas import tpu as pltpu

F32 = jnp.float32
BF16 = jnp.bfloat16

N_DEV = 8
D_MODEL = 1024
D_FF = 2816
FF_SHARD = D_FF // N_DEV
FF_PAD = 384
ATT_W = 512
ATT_HEADS = 8
DILATED = ((128, 1), (512, 4), (2048, 16))
BLK = 128
ML_W = 512
ML_HEADS = 4
ML_HD = 128
CHUNK = 128
CONV_K = 4
W_IN = 3592
W_IN_SHARD = W_IN // N_DEV
W_IN_MAIN = 3584
XA_HEADS = 4
XA_HD = 256
MEM_LEN = 256
REL_BUCKETS = 32
REL_MAX_DIST = 2048
ALPHA = 2.0 ** 0.25
LN_EPS = 1e-5
NEG = -1e30
ADAM_LR = 0.001
ADAM_B1 = 0.9
ADAM_B2 = 0.999
ADAM_EPS = 1e-08
ADAM_WD = 0.01
ADAM_STEP = 10
LANES = 128
VMEM_LIMIT = 58 * 1024 * 1024

NN = (((1,), (0,)), ((), ()))
NT = (((1,), (1,)), ((), ()))
TN = (((0,), (0,)), ((), ()))


def _dot(a, b, dims):
    return lax.dot_general(a, b, dims, preferred_element_type=F32)


def _params(*sem):
    return pltpu.CompilerParams(dimension_semantics=sem, vmem_limit_bytes=VMEM_LIMIT)


def _sigmoid(x):
    return 0.5 * jnp.tanh(0.5 * x) + 0.5


def _rowsum8(x):
    t, c = x.shape
    return jnp.sum(x.reshape(t // 8, 8, c), axis=0)


def _mesh_pos():
    x, y, c = lax.axis_index("x"), lax.axis_index("y"), lax.axis_index("c")
    return x, y, c, 4 * x + 2 * y + c


def _peer(x, y, c, k):
    px = 1 - x if k & 4 else x
    py = 1 - y if k & 2 else y
    pc = 1 - c if k & 1 else c
    return (px, py, pc), 4 * px + 2 * py + pc


def _call(body, *, name, grid, in_specs, out_specs, out_shape, args, scratch_shapes=(), sem=None,
          gather=(), exchange=()):
    in_specs, out_specs, out_shape, scratch = list(in_specs), list(out_specs), list(out_shape), list(scratch_shapes)
    ng, nc = len(gather), len(gather) + len(exchange)
    if nc == 0:
        return pl.pallas_call(body, name=name, grid=grid, in_specs=in_specs, out_specs=out_specs,
                              out_shape=out_shape, scratch_shapes=scratch, compiler_params=_params(*sem))(*args)
    n_in, n_out, n_scr = len(in_specs), len(out_specs), len(scratch)

    def wrapped(*refs):
        ins, cin = refs[:n_in], refs[n_in:n_in + nc]
        outs, cout = refs[n_in + nc:n_in + nc + n_out], refs[n_in + nc + n_out:n_in + 2 * nc + n_out]
        scr = refs[n_in + 2 * nc + n_out:n_in + 2 * nc + n_out + n_scr]
        send_sems, recv_sems, loc_sems = refs[-3:]
        first, last = None, None
        for ax, extent in enumerate(grid):
            f, l = pl.program_id(ax) == 0, pl.program_id(ax) == extent - 1
            first = f if first is None else first & f
            last = l if last is None else last & l

        def copies():
            x, y, c, me = _mesh_pos()
            out = []
            for a in range(nc):
                mine = cin[a] if a < ng else cin[a].at[me]
                out.append(pltpu.make_async_copy(mine, cout[a].at[me], loc_sems.at[a]))
                for k in range(1, N_DEV):
                    peer, pidx = _peer(x, y, c, k)
                    out.append(pltpu.make_async_remote_copy(
                        src_ref=cin[a] if a < ng else cin[a].at[pidx], dst_ref=cout[a].at[me],
                        send_sem=send_sems.at[a, k - 1], recv_sem=recv_sems.at[a, k - 1],
                        device_id=peer, device_id_type=pl.DeviceIdType.MESH))
            return out

        @pl.when(first)
        def _():
            for cp in copies():
                cp.start()

        body(*ins, *outs, *scr)

        @pl.when(last)
        def _():
            for cp in copies():
                cp.wait()

    hbm = pl.BlockSpec(memory_space=pl.ANY)
    comm_shapes = [jax.ShapeDtypeStruct((N_DEV,) + a.shape, a.dtype) for a in gather]
    comm_shapes += [jax.ShapeDtypeStruct(a.shape, a.dtype) for a in exchange]
    return pl.pallas_call(
        wrapped, name=name, grid=grid, in_specs=in_specs + [hbm] * nc, out_specs=out_specs + [hbm] * nc,
        out_shape=out_shape + comm_shapes,
        scratch_shapes=scratch + [pltpu.SemaphoreType.DMA((nc, N_DEV - 1)), pltpu.SemaphoreType.DMA((nc, N_DEV - 1)),
                                  pltpu.SemaphoreType.DMA((nc,))],
        compiler_params=_params(*(("arbitrary",) * len(grid))),
    )(*args, *gather, *exchange)


def _gather_two_level(name, arrays):
    na = len(arrays)

    def body(*refs):
        srcs, outs = refs[:na], refs[na:2 * na]
        send_sems, recv_sems, loc_sems = refs[2 * na:]
        x, y, c, me = _mesh_pos()
        here, sib = (x, y, c), (x, y, 1 - c)
        chips = [(1 - x, y), (x, 1 - y), (1 - x, 1 - y)]
        pos = lambda px, py, pc: 4 * px + 2 * py + pc

        def copy(a, k, block, to, src=None):
            return pltpu.make_async_remote_copy(
                src_ref=outs[a].at[block] if src is None else src, dst_ref=outs[a].at[block],
                send_sem=send_sems.at[a, k], recv_sem=recv_sems.at[a, k], device_id=to,
                device_id_type=pl.DeviceIdType.MESH)

        locs = [pltpu.make_async_copy(srcs[a], outs[a].at[me], loc_sems.at[a]) for a in range(na)]
        for cp in locs:
            cp.start()
        first = []
        for a in range(na):
            first.append(copy(a, 0, me, sib, src=srcs[a]))
            first += [copy(a, 1 + j, me, (*chip, c), src=srcs[a]) for j, chip in enumerate(chips)]
        for cp in first:
            cp.start()
        passed = []
        for a in range(na):
            for j, chip in enumerate(chips):
                copy(a, 1 + j, pos(*chip, c), here).wait_recv()
                passed.append(copy(a, 4 + j, pos(*chip, c), sib))
                passed[-1].start()
        for a in range(na):
            copy(a, 0, pos(x, y, 1 - c), here).wait_recv()
            for j, chip in enumerate(chips):
                copy(a, 4 + j, pos(*chip, 1 - c), here).wait_recv()
        for cp in first + passed:
            cp.wait_send()
        for cp in locs:
            cp.wait()

    hbm = pl.BlockSpec(memory_space=pl.ANY)
    return pl.pallas_call(
        body, name=name, in_specs=[hbm] * na, out_specs=[hbm] * na,
        out_shape=[jax.ShapeDtypeStruct((N_DEV,) + a.shape, a.dtype) for a in arrays],
        scratch_shapes=[pltpu.SemaphoreType.DMA((na, N_DEV - 1)), pltpu.SemaphoreType.DMA((na, N_DEV - 1)),
                        pltpu.SemaphoreType.DMA((na,))],
    )(*arrays)


def _exchange_only(name, gather=(), exchange=()):
    return _call(lambda: None, name=name, grid=(1,), in_specs=[], out_specs=[], out_shape=[], args=(),
                 gather=gather, exchange=exchange)


def _matmul(a, b, mode, name, *, out_dtype=F32, tm=1024, tn=512, tk=512, add=None, add_scale=1.0,
            blocked_out=False, gather=(), exchange=()):
    blocked_b = b.ndim == 3
    if blocked_b:
        (m, k), (nb, _, tn) = a.shape, b.shape
        n = nb * tn
    elif mode == "nn":
        (m, k), (_, n) = a.shape, b.shape
    elif mode == "nt":
        (m, k), (n, _) = a.shape, b.shape
    else:
        (k, m), (_, n) = a.shape, b.shape
    tm, tn, tk = min(tm, m), min(tn, n), min(tk, k)
    nk = k // tk
    dims = {"nn": NN, "nt": NT, "tn": TN}[mode]
    if mode == "tn":
        a_spec = pl.BlockSpec((tk, tm), lambda i, j, kk: (kk, i))
    else:
        a_spec = pl.BlockSpec((tm, tk), lambda i, j, kk: (i, kk))
    if blocked_b:
        b_spec = pl.BlockSpec((None, tk, tn), lambda i, j, kk: (j, kk, 0))
    elif mode == "nt":
        b_spec = pl.BlockSpec((tn, tk), lambda i, j, kk: (j, kk))
    else:
        b_spec = pl.BlockSpec((tk, tn), lambda i, j, kk: (kk, j))
    if blocked_out:
        o_spec = pl.BlockSpec((None, tm, tn), lambda i, j, kk: (j, i, 0))
        o_shape = jax.ShapeDtypeStruct((n // tn, m, tn), out_dtype)
    else:
        o_spec = pl.BlockSpec((tm, tn), lambda i, j, kk: (i, j))
        o_shape = jax.ShapeDtypeStruct((m, n), out_dtype)
    has_add = add is not None
    cache_a = nk == 1 and mode != "tn" and n // tn > 1 and a.dtype != BF16

    def body(*refs):
        if has_add:
            a_ref, b_ref, add_ref, o_ref, s_ref = refs
        else:
            a_ref, b_ref, o_ref, s_ref = refs
        kk = pl.program_id(2)
        if cache_a:
            @pl.when(pl.program_id(1) == 0)
            def _():
                s_ref[...] = a_ref[...].astype(BF16)

            lhs = s_ref[...]
        else:
            lhs = a_ref[...].astype(BF16)
        part = _dot(lhs, b_ref[...].astype(BF16), dims)

        def finish(r):
            if has_add:
                r = r + add_scale * add_ref[...]
            o_ref[...] = r.astype(out_dtype)

        if nk == 1:
            finish(part)
            return

        @pl.when(kk == 0)
        def _():
            s_ref[...] = part

        @pl.when(kk > 0)
        def _():
            s_ref[...] += part

        @pl.when(kk == nk - 1)
        def _():
            finish(s_ref[...])

    if nk > 1:
        scratch = [pltpu.VMEM((tm, tn), F32)]
    else:
        scratch = [pltpu.VMEM((tm, tk), BF16) if cache_a else pltpu.VMEM((8, LANES), F32)]
    return _call(
        body, name=name, grid=(m // tm, n // tn, nk),
        in_specs=[a_spec, b_spec] + ([pl.BlockSpec((tm, tn), lambda i, j, kk: (i, j))] if has_add else []),
        out_specs=[o_spec], out_shape=[o_shape], args=(a, b) + ((add,) if has_add else ()),
        scratch_shapes=scratch, sem=("parallel", "arbitrary", "arbitrary"),
        gather=gather, exchange=exchange)


def _ln_fwd_math(u, g, b):
    mu = jnp.mean(u, axis=-1, keepdims=True)
    uc = u - mu
    var = jnp.mean(uc * uc, axis=-1, keepdims=True)
    return uc * lax.rsqrt(var + LN_EPS) * g + b


def _ln_bwd_math(dy, u, g):
    mu = jnp.mean(u, axis=-1, keepdims=True)
    uc = u - mu
    var = jnp.mean(uc * uc, axis=-1, keepdims=True)
    rstd = lax.rsqrt(var + LN_EPS)
    xhat = uc * rstd
    dxh = dy * g
    m1 = jnp.mean(dxh, axis=-1, keepdims=True)
    m2 = jnp.mean(dxh * xhat, axis=-1, keepdims=True)
    return rstd * (dxh - m1 - xhat * m2), xhat


def _matmul_resid_ln(pieces, w, x, g, b, name, tm=1024):
    s = pieces[0].shape[0]
    k, d = w.shape
    widths = [p.shape[1] for p in pieces]

    def body(*refs):
        a_refs = refs[:len(pieces)]
        w_ref, x_ref, g_ref, b_ref, u_ref, y_ref = refs[len(pieces):]
        u = ALPHA * x_ref[...]
        lo = 0
        for a_ref, width in zip(a_refs, widths):
            u = u + _dot(a_ref[...].astype(BF16), w_ref[lo:lo + width, :], NN)
            lo += width
        u_ref[...] = u
        y_ref[...] = _ln_fwd_math(u, g_ref[...], b_ref[...])

    row = pl.BlockSpec((tm, d), lambda i: (i, 0))
    vec = pl.BlockSpec((1, d), lambda i: (0, 0))
    return pl.pallas_call(
        body, name=name, grid=(s // tm,),
        in_specs=[pl.BlockSpec((tm, width), lambda i: (i, 0)) for width in widths]
        + [pl.BlockSpec((k, d), lambda i: (0, 0)), row, vec, vec],
        out_specs=[row, row], out_shape=[jax.ShapeDtypeStruct((s, d), F32)] * 2,
        compiler_params=_params("parallel"),
    )(*pieces, w, x, g, b)


def _ln_bwd(dy, u, g, w, name, tm=1024):
    s, d = dy.shape
    n = w.shape[0]
    nt = s // tm

    def body(dy_ref, u_ref, g_ref, w_ref, du_ref, dz_ref, dg_ref, db_ref, g8, b8):
        i = pl.program_id(0)
        dy_ = dy_ref[...]
        du, xhat = _ln_bwd_math(dy_, u_ref[...], g_ref[...])
        du_ref[...] = du
        dz_ref[...] = _dot(du.astype(BF16), w_ref[...], NT)

        @pl.when(i == 0)
        def _():
            g8[...] = jnp.zeros_like(g8)
            b8[...] = jnp.zeros_like(b8)

        g8[...] += _rowsum8(dy_ * xhat)
        b8[...] += _rowsum8(dy_)

        @pl.when(i == nt - 1)
        def _():
            dg_ref[...] = jnp.sum(g8[...], axis=0, keepdims=True)
            db_ref[...] = jnp.sum(b8[...], axis=0, keepdims=True)

    row = pl.BlockSpec((tm, d), lambda i: (i, 0))
    vec = pl.BlockSpec((1, d), lambda i: (0, 0))
    return pl.pallas_call(
        body, name=name, grid=(nt,),
        in_specs=[row, row, vec, pl.BlockSpec((n, d), lambda i: (0, 0))],
        out_specs=[row, pl.BlockSpec((tm, n), lambda i: (i, 0)), vec, vec],
        out_shape=[jax.ShapeDtypeStruct((s, d), F32), jax.ShapeDtypeStruct((s, n), F32),
                   jax.ShapeDtypeStruct((1, d), F32), jax.ShapeDtypeStruct((1, d), F32)],
        scratch_shapes=[pltpu.VMEM((8, d), F32), pltpu.VMEM((8, d), F32)],
        compiler_params=_params("arbitrary"),
    )(dy, u, g, w)


FF_PAIR = 2 * FF_PAD
N_PAIR = N_DEV // 2
FF_COLS = 256


def _ffn_fwd(x, wgt, wut, wd, g, b, name, tm=1024, gather=()):
    s, d = x.shape

    def body(x_ref, wg_ref, wu_ref, wd_ref, g_ref, b_ref, u_ref, y_ref, a_ref, bb_ref, xb, acc):
        k = pl.program_id(1)

        @pl.when(k == 0)
        def _():
            xb[...] = x_ref[...].astype(BF16)

        a = _dot(xb[...], wg_ref[...], NT)
        bb = _dot(xb[...], wu_ref[...], NT)
        a_ref[...] = a.astype(BF16)
        bb_ref[...] = bb.astype(BF16)
        h = (a * _sigmoid(a) * bb).astype(BF16)
        part = _dot(h, wd_ref[...], NN)

        @pl.when(k == 0)
        def _():
            acc[...] = part

        @pl.when(k > 0)
        def _():
            acc[...] += part

        @pl.when(k == N_PAIR - 1)
        def _():
            u = ALPHA * x_ref[...] + 0.5 * acc[...]
            u_ref[...] = u
            y_ref[...] = _ln_fwd_math(u, g_ref[...], b_ref[...])

    row = pl.BlockSpec((tm, d), lambda i, k: (i, 0))
    vec = pl.BlockSpec((1, d), lambda i, k: (0, 0))
    w_in = pl.BlockSpec((None, FF_PAIR, d), lambda i, k: (k, 0, 0))
    w_dn = w_in
    hid = pl.BlockSpec((tm, FF_PAIR), lambda i, k: (i, k))
    return _call(
        body, name=name, grid=(s // tm, N_PAIR),
        in_specs=[row, w_in, w_in, w_dn, vec, vec], out_specs=[row, row, hid, hid],
        out_shape=[jax.ShapeDtypeStruct((s, d), F32)] * 2 + [jax.ShapeDtypeStruct((s, N_DEV * FF_PAD), BF16)] * 2,
        args=(x, wgt, wut, wd, g, b),
        scratch_shapes=[pltpu.VMEM((tm, d), BF16), pltpu.VMEM((tm, d), F32)],
        sem=("parallel", "arbitrary"), gather=gather)


def _ffn_bwd_x(dy, u, x, wgt, wut, wd, g, a_fwd, b_fwd, name, tm=512, exchange=()):
    s, d = x.shape
    nt = s // tm
    ffp = N_DEV * FF_PAD

    def body(dy_ref, u_ref, x_ref, wg_ref, wu_ref, wd_ref, g_ref, a_ref, bb_ref,
             dx_ref, xb, df_ref, da_ref, db_ref, h_ref, dg_ref, dbl_ref,
             dfb, du_s, acc, g8, b8):
        i = pl.program_id(0)
        k = pl.program_id(1)

        @pl.when(k == 0)
        def _():
            dy_ = dy_ref[...]
            du, xhat = _ln_bwd_math(dy_, u_ref[...], g_ref[...])
            du_s[...] = du
            dfb[...] = (0.5 * du).astype(BF16)
            df_ref[...] = dfb[...]
            xb[...] = x_ref[...].astype(BF16)

            @pl.when(i == 0)
            def _():
                g8[...] = jnp.zeros_like(g8)
                b8[...] = jnp.zeros_like(b8)

            g8[...] += _rowsum8(dy_ * xhat)
            b8[...] += _rowsum8(dy_)

        dh_all = _dot(dfb[...], wd_ref[...], NT)

        def gate_grads(c):
            cs = slice(FF_COLS * c, FF_COLS * (c + 1))
            a = a_ref[:, cs].astype(F32)
            bb = bb_ref[:, cs].astype(F32)
            dh = dh_all[:, cs]
            sig = _sigmoid(a)
            sa = a * sig
            h_ref[:, cs] = (sa * bb).astype(BF16)
            da = (dh * bb * (sig * (1.0 + a * (1.0 - sig)))).astype(BF16)
            db = (dh * sa).astype(BF16)
            da_ref[:, cs] = da
            db_ref[:, cs] = db
            return da, db

        n_chunks = FF_PAIR // FF_COLS
        chunks = [gate_grads(0)]
        part = None
        for c in range(n_chunks):
            if c + 1 < n_chunks:
                chunks.append(gate_grads(c + 1))
            cs = slice(FF_COLS * c, FF_COLS * (c + 1))
            pc = _dot(chunks[c][0], wg_ref[cs, :], NN) + _dot(chunks[c][1], wu_ref[cs, :], NN)
            part = pc if part is None else part + pc

        @pl.when(k == 0)
        def _():
            acc[...] = part

        @pl.when(k > 0)
        def _():
            acc[...] += part

        @pl.when(k == N_PAIR - 1)
        def _():
            dx_ref[...] = ALPHA * du_s[...] + acc[...]

        @pl.when((k == N_PAIR - 1) & (i == nt - 1))
        def _():
            dg_ref[...] = jnp.sum(g8[...], axis=0, keepdims=True)
            dbl_ref[...] = jnp.sum(b8[...], axis=0, keepdims=True)

    row = pl.BlockSpec((tm, d), lambda i, k: (i, 0))
    vec = pl.BlockSpec((1, d), lambda i, k: (0, 0))
    w_in = pl.BlockSpec((None, FF_PAIR, d), lambda i, k: (k, 0, 0))
    hid = pl.BlockSpec((tm, FF_PAIR), lambda i, k: (i, k))
    return _call(
        body, name=name, grid=(nt, N_PAIR),
        in_specs=[row, row, row, w_in, w_in, w_in, vec, hid, hid],
        out_specs=[row, row, row, hid, hid, hid, vec, vec],
        out_shape=[jax.ShapeDtypeStruct((s, d), F32), jax.ShapeDtypeStruct((s, d), BF16),
                   jax.ShapeDtypeStruct((s, d), BF16),
                   jax.ShapeDtypeStruct((s, ffp), BF16), jax.ShapeDtypeStruct((s, ffp), BF16),
                   jax.ShapeDtypeStruct((s, ffp), BF16),
                   jax.ShapeDtypeStruct((1, d), F32), jax.ShapeDtypeStruct((1, d), F32)],
        args=(dy, u, x, wgt, wut, wd, g, a_fwd, b_fwd),
        scratch_shapes=[pltpu.VMEM((tm, d), BF16), pltpu.VMEM((tm, d), F32),
                        pltpu.VMEM((tm, d), F32), pltpu.VMEM((8, d), F32), pltpu.VMEM((8, d), F32)],
        sem=("arbitrary", "arbitrary"), exchange=exchange)


def _ffn_bwd_w(tok, hid, name, *, down, tm=4096, exchange=()):
    s, d = tok.shape
    tm = min(tm, s)
    nt = s // tm

    def body(t_ref, h_ref, dw_ref, acc):
        i = pl.program_id(1)
        part = _dot(h_ref[...], t_ref[...], TN) if down else _dot(t_ref[...], h_ref[...], TN)

        @pl.when(i == 0)
        def _():
            acc[...] = part

        @pl.when(i > 0)
        def _():
            acc[...] += part

        @pl.when(i == nt - 1)
        def _():
            for j in range(2):
                lo = j * FF_PAD
                dw_ref[j] = (acc[lo:lo + FF_SHARD, :] if down else acc[:, lo:lo + FF_SHARD]).astype(BF16)

    blk = (FF_SHARD, d) if down else (d, FF_SHARD)
    return _call(
        body, name=name, grid=(N_PAIR, nt),
        in_specs=[pl.BlockSpec((tm, d), lambda k, i: (i, 0)), pl.BlockSpec((tm, FF_PAIR), lambda k, i: (i, k))],
        out_specs=[pl.BlockSpec((2,) + blk, lambda k, i: (k, 0, 0))],
        out_shape=[jax.ShapeDtypeStruct((N_DEV,) + blk, BF16)], args=(tok, hid),
        scratch_shapes=[pltpu.VMEM((FF_PAIR, d) if down else (d, FF_PAIR), F32)],
        sem=("parallel", "arbitrary"), exchange=exchange)


def _bucket_tables():
    qi = np.arange(BLK)[:, None]
    ki = np.arange(2 * BLK)[None, :]
    off = qi + BLK - ki
    out = []
    for window, dil in DILATED:
        n_keys = window // dil
        dist = dil * np.clip(off, 0, n_keys)
        exact = REL_BUCKETS // 2
        df = np.maximum(dist, 1).astype(np.float32)
        large = exact + (np.log(df / np.float32(exact)) / np.float32(math.log(REL_MAX_DIST / exact))
                         * np.float32(REL_BUCKETS - exact)).astype(np.int32)
        large = np.minimum(large, REL_BUCKETS - 1)
        bucket = np.where(dist < exact, dist, large).astype(np.int32)
        band = (off >= 0) & (off <= n_keys)
        out.append(np.where(band, bucket, -1))
    return np.stack(out).astype(np.int32)


def _bias_fwd(rel_bias, buckets, name="bias_fwd"):
    def body(tbl_ref, bkt_ref, out_ref):
        bkt = bkt_ref[...]
        for h in range(ATT_HEADS):
            acc = jnp.full((BLK, 2 * BLK), NEG, F32)
            for bb in range(REL_BUCKETS):
                acc = jnp.where(bkt == bb, tbl_ref[bb, h], acc)
            out_ref[h] = acc

    nbr = len(DILATED)
    return pl.pallas_call(
        body, name=name, grid=(nbr,),
        in_specs=[pl.BlockSpec(memory_space=pltpu.SMEM),
                  pl.BlockSpec((None, BLK, 2 * BLK), lambda r: (r, 0, 0))],
        out_specs=pl.BlockSpec((None, ATT_HEADS, BLK, 2 * BLK), lambda r: (r, 0, 0, 0)),
        out_shape=jax.ShapeDtypeStruct((nbr, ATT_HEADS, BLK, 2 * BLK), F32),
        compiler_params=_params("parallel"),
    )(rel_bias, buckets)


def _bias_bwd(dbias, buckets, name="bias_bwd"):
    nbr = len(DILATED)

    def body(db_ref, bkt_ref, out_ref):
        r = pl.program_id(0)

        @pl.when(r == 0)
        def _():
            out_ref[...] = jnp.zeros_like(out_ref)

        bkt = bkt_ref[...]
        rowi = lax.broadcasted_iota(jnp.int32, (REL_BUCKETS, LANES), 0)
        coli = lax.broadcasted_iota(jnp.int32, (REL_BUCKETS, LANES), 1)
        acc = jnp.zeros((REL_BUCKETS, LANES), F32)
        for h in range(ATT_HEADS):
            x = db_ref[h]
            for bb in range(REL_BUCKETS):
                part = jnp.sum(jnp.where(bkt == bb, x, 0.0), axis=0, keepdims=True)
                tot = jnp.sum(part, axis=1, keepdims=True)
                acc = acc + jnp.where((rowi == bb) & (coli == h), tot, 0.0)
        out_ref[...] += acc

    return pl.pallas_call(
        body, name=name, grid=(nbr,),
        in_specs=[pl.BlockSpec((None, ATT_HEADS, BLK, 2 * BLK), lambda r: (r, 0, 0, 0)),
                  pl.BlockSpec((None, BLK, 2 * BLK), lambda r: (r, 0, 0))],
        out_specs=pl.BlockSpec((REL_BUCKETS, LANES), lambda r: (0, 0)),
        out_shape=jax.ShapeDtypeStruct((REL_BUCKETS, LANES), F32),
        compiler_params=_params("arbitrary"),
    )(dbias, buckets)


def _stack_heads(pair, lo):
    return jnp.concatenate([jnp.where(lo, pair, 0.0), jnp.where(lo, 0.0, pair)], axis=0)


def _head_cols(pair, lo, reduce):
    fill = -jnp.inf if reduce is jnp.max else 0.0
    return jnp.concatenate([reduce(jnp.where(lo, pair, fill), axis=1, keepdims=True),
                            reduce(jnp.where(lo, fill, pair), axis=1, keepdims=True)], axis=0)


def _unstack_heads(x2, lo):
    return jnp.where(lo, x2[:BLK], x2[BLK:])


def _att_scores(q2, kk, bias2, first_ok):
    sc = _dot(q2, kk, NT) * (64 ** -0.5) + bias2
    return jnp.where(first_ok, sc, NEG)


DIL_TILE = 2048
DIL_COLS = ATT_W // LANES
DIL_GROUP = 4


def _dil_rows(dil, n, r, base=0):
    start = base + n * (BLK * dil) + r
    return pl.ds(start, BLK, stride=dil) if dil > 1 else pl.ds(start, BLK)


def _dil_in_specs(tile_of):
    cur = lambda col: pl.BlockSpec((DIL_TILE, LANES), lambda p, i: (tile_of(i), col * DIL_COLS + p))
    prev = lambda col: pl.BlockSpec((DIL_TILE, LANES), lambda p, i: (jnp.maximum(tile_of(i) - 1, 0), col * DIL_COLS + p))
    bias = pl.BlockSpec((len(DILATED), None, 2 * BLK, 2 * BLK), lambda p, i: (0, p, 0, 0))
    return [cur(0), prev(1), cur(1), prev(2), cur(2), bias]


def _pair_bias(biasm):
    return biasm.reshape(len(DILATED), DIL_COLS, 2 * BLK, 2 * BLK)


def _dil_fwd(proj, biasm, name="dil_fwd", gather=()):
    s = proj.shape[0]
    nt = s // DIL_TILE
    tt = DIL_TILE

    def body(q_ref, kp_ref, kc_ref, vp_ref, vc_ref, bias_ref, att_ref, lse_ref, k2, v2, ob, lb):
        t = pl.program_id(1)
        k2[0:tt, :] = kp_ref[...]
        k2[tt:2 * tt, :] = kc_ref[...]
        v2[0:tt, :] = vp_ref[...]
        v2[tt:2 * tt, :] = vc_ref[...]
        lo = lax.broadcasted_iota(jnp.int32, (BLK, LANES), 1) < 64
        kidx = lax.broadcasted_iota(jnp.int32, (2 * BLK, 2 * BLK), 1)
        for b, (_, dil) in enumerate(DILATED):
            for j0 in range(0, tt // BLK, DIL_GROUP):
                grp = range(DIL_GROUP)
                rn = [((j0 + i) % dil, (j0 + i) // dil) for i in grp]
                here = [_dil_rows(dil, n, r) for r, n in rn]
                cur = [_dil_rows(dil, n, r, tt) for r, n in rn]
                prev = [_dil_rows(dil, n - 1, r, tt) for r, n in rn]
                q2 = [_stack_heads(q_ref[here[i], :], lo).astype(BF16) for i in grp]
                kk = [jnp.concatenate([k2[prev[i], :], k2[cur[i], :]], axis=0).astype(BF16) for i in grp]
                vv = [jnp.concatenate([v2[prev[i], :], v2[cur[i], :]], axis=0).astype(BF16) for i in grp]
                sc = [_att_scores(q2[i], kk[i], bias_ref[b], (t > 0) | (rn[i][1] > 0) | (kidx >= BLK)) for i in grp]
                mx = [jnp.max(sc[i], axis=1, keepdims=True) for i in grp]
                pe = [jnp.exp(sc[i] - mx[i]) for i in grp]
                l = [jnp.sum(pe[i], axis=1, keepdims=True) for i in grp]
                o2 = [_dot(pe[i].astype(BF16), vv[i], NN) for i in grp]
                for i in grp:
                    ob.at[b][here[i], :] = _unstack_heads(o2[i] / l[i], lo)
                    lb.at[b][here[i], :] = _unstack_heads(jnp.broadcast_to(mx[i] + jnp.log(l[i]), (2 * BLK, LANES)), lo)
        l0, l1, l2 = lb[0], lb[1], lb[2]
        mx = jnp.maximum(jnp.maximum(l0, l1), l2)
        e0, e1, e2 = jnp.exp(l0 - mx), jnp.exp(l1 - mx), jnp.exp(l2 - mx)
        tot = e0 + e1 + e2
        att_ref[...] = (e0 * ob[0] + e1 * ob[1] + e2 * ob[2]) / tot
        lse_ref[...] = mx + jnp.log(tot)

    out = pl.BlockSpec((tt, LANES), lambda p, i: (i, p))
    return _call(
        body, name=name, grid=(DIL_COLS, nt), in_specs=_dil_in_specs(lambda i: i), out_specs=[out, out],
        out_shape=[jax.ShapeDtypeStruct((s, ATT_W), F32)] * 2, args=(proj, proj, proj, proj, proj, _pair_bias(biasm)),
        scratch_shapes=[pltpu.VMEM((2 * tt, LANES), F32), pltpu.VMEM((2 * tt, LANES), F32),
                        pltpu.VMEM((len(DILATED), tt, LANES), F32), pltpu.VMEM((len(DILATED), tt, LANES), F32)],
        sem=("parallel", "parallel"), gather=gather)


def _dil_bwd(proj, biasm, lse, att, dcat, name="dil_bwd"):
    s = proj.shape[0]
    nt = s // DIL_TILE
    tt = DIL_TILE
    nbr = len(DILATED)

    def body(q_ref, kp_ref, kc_ref, vp_ref, vc_ref, bias_ref, lse_ref, att_ref, datt_ref,
             dq_ref, dk_ref, dv_ref, dbias_ref, k2, v2, dqa, dka, dva, kcar, vcar):
        i = pl.program_id(1)
        t = nt - 1 - i
        k2[0:tt, :] = kp_ref[...]
        k2[tt:2 * tt, :] = kc_ref[...]
        v2[0:tt, :] = vp_ref[...]
        v2[tt:2 * tt, :] = vc_ref[...]

        @pl.when(i == 0)
        def _():
            kcar[...] = jnp.zeros_like(kcar)
            vcar[...] = jnp.zeros_like(vcar)
            dbias_ref[...] = jnp.zeros_like(dbias_ref)

        dqa[...] = jnp.zeros_like(dqa)
        dka[0:tt, :] = jnp.zeros((tt, LANES), F32)
        dva[0:tt, :] = jnp.zeros((tt, LANES), F32)
        dka[tt:2 * tt, :] = kcar[...]
        dva[tt:2 * tt, :] = vcar[...]
        lo = lax.broadcasted_iota(jnp.int32, (BLK, LANES), 1) < 64
        kidx = lax.broadcasted_iota(jnp.int32, (2 * BLK, 2 * BLK), 1)
        for b, (_, dil) in enumerate(DILATED):
            for j0 in range(0, tt // BLK, DIL_GROUP):
                grp = range(DIL_GROUP)
                rn = [((j0 + i) % dil, (j0 + i) // dil) for i in grp]
                here = [_dil_rows(dil, n, r) for r, n in rn]
                cur = [_dil_rows(dil, n, r, tt) for r, n in rn]
                prev = [_dil_rows(dil, n - 1, r, tt) for r, n in rn]
                dat = [datt_ref[here[i], :] for i in grp]
                q2 = [_stack_heads(q_ref[here[i], :], lo).astype(BF16) for i in grp]
                dom = [_stack_heads(dat[i], lo).astype(BF16) for i in grp]
                kk = [jnp.concatenate([k2[prev[i], :], k2[cur[i], :]], axis=0).astype(BF16) for i in grp]
                vv = [jnp.concatenate([v2[prev[i], :], v2[cur[i], :]], axis=0).astype(BF16) for i in grp]
                sc = [_att_scores(q2[i], kk[i], bias_ref[b], (t > 0) | (rn[i][1] > 0) | (kidx >= BLK)) for i in grp]
                dp = [_dot(dom[i], vv[i], NT) for i in grp]
                pr = [jnp.exp(sc[i] - _head_cols(lse_ref[here[i], :], lo, jnp.max)) for i in grp]
                ds = [pr[i] * (dp[i] - _head_cols(dat[i] * att_ref[here[i], :], lo, jnp.sum)) for i in grp]
                dsb = [(ds[i] * (64 ** -0.5)).astype(BF16) for i in grp]
                dq2 = [_dot(dsb[i], kk[i], NN) for i in grp]
                dk2 = [_dot(dsb[i], q2[i], TN) for i in grp]
                dv2 = [_dot(pr[i].astype(BF16), dom[i], TN) for i in grp]
                for i in grp:
                    dbias_ref[b] += ds[i]
                    dqa[here[i], :] += _unstack_heads(dq2[i], lo)
                    dka[prev[i], :] += dk2[i][:BLK]
                    dka[cur[i], :] += dk2[i][BLK:]
                    dva[prev[i], :] += dv2[i][:BLK]
                    dva[cur[i], :] += dv2[i][BLK:]
        dq_ref[...] = dqa[...].astype(BF16)
        dk_ref[...] = dka[tt:2 * tt, :].astype(BF16)
        dv_ref[...] = dva[tt:2 * tt, :].astype(BF16)
        kcar[...] = dka[0:tt, :]
        vcar[...] = dva[0:tt, :]

    rev = lambda i: nt - 1 - i
    out = pl.BlockSpec((tt, LANES), lambda p, i: (rev(i), p))
    two = lambda: pltpu.VMEM((2 * tt, LANES), F32)
    one = lambda: pltpu.VMEM((tt, LANES), F32)
    return pl.pallas_call(
        body, name=name, grid=(DIL_COLS, nt),
        in_specs=_dil_in_specs(rev) + [out, out, out],
        out_specs=[out, out, out, pl.BlockSpec((nbr, None, 2 * BLK, 2 * BLK), lambda p, i: (0, p, 0, 0))],
        out_shape=[jax.ShapeDtypeStruct((s, ATT_W), BF16)] * 3
        + [jax.ShapeDtypeStruct((nbr, DIL_COLS, 2 * BLK, 2 * BLK), F32)],
        scratch_shapes=[two(), two(), one(), two(), two(), one(), one()],
        compiler_params=_params("arbitrary", "arbitrary"),
    )(proj, proj, proj, proj, proj, _pair_bias(biasm), lse, att, dcat)


QK_COL0 = (3 * ATT_W) // ATT_W


HALO = 8


def _conv_shifted(prev8, cur, j):
    sh = CONV_K - 1 - j
    if sh == 0:
        return cur
    rolled = pltpu.roll(cur, sh, 0)
    row8 = lax.broadcasted_iota(jnp.int32, prev8.shape, 0)
    top = jnp.where(row8 < sh, pltpu.roll(prev8, sh, 0), rolled[:HALO])
    return top if cur.shape[0] == HALO else jnp.concatenate([top, rolled[HALO:]], axis=0)


def _conv_z(prev8, cur, w_ref, b_ref, taps=None):
    z = b_ref[...]
    for j in range(CONV_K):
        tap = _conv_shifted(prev8, cur, j)
        if taps is not None:
            taps.append(tap)
        z = z + tap * w_ref[j:j + 1, :]
    return z


def _silu_grad(z):
    sig = _sigmoid(z)
    return sig * (1.0 + z * (1.0 - sig))


def _conv_fwd(proj, conv_w, conv_b, name="conv_fwd", tm=512):
    s = proj.shape[0]
    w = ATT_W
    per = tm // HALO

    def body(prev_ref, cur_ref, w_ref, b_ref, o_ref):
        i = pl.program_id(1)
        prev8 = jnp.where(i > 0, prev_ref[...], 0.0)
        z = _conv_z(prev8, cur_ref[...], w_ref, b_ref)
        o_ref[...] = z * _sigmoid(z)

    return pl.pallas_call(
        body, name=name, grid=(2, s // tm),
        in_specs=[pl.BlockSpec((HALO, w), lambda j, i: (jnp.maximum(i * per - 1, 0), QK_COL0 + j)),
                  pl.BlockSpec((tm, w), lambda j, i: (i, QK_COL0 + j)),
                  pl.BlockSpec((CONV_K, w), lambda j, i: (0, j)),
                  pl.BlockSpec((1, w), lambda j, i: (0, j))],
        out_specs=pl.BlockSpec((tm, w), lambda j, i: (i, j)),
        out_shape=jax.ShapeDtypeStruct((s, 2 * ML_W), F32),
        compiler_params=_params("parallel", "parallel"),
    )(proj, proj, conv_w, conv_b)


def _conv_bwd(proj, dqk, conv_w, conv_b, name="conv_bwd", tm=512):
    s = proj.shape[0]
    w = ATT_W
    nt = s // tm
    per = tm // HALO

    def body(xp_ref, xc_ref, xn_ref, dc_ref, dn_ref, w_ref, b_ref, dx_ref, dw_ref, db_ref):
        i = pl.program_id(1)
        prev8 = jnp.where(i > 0, xp_ref[...], 0.0)
        cur = xc_ref[...]
        taps = []
        dzc = dc_ref[...] * _silu_grad(_conv_z(prev8, cur, w_ref, b_ref, taps))
        dzn8 = dn_ref[...] * _silu_grad(_conv_z(cur[tm - HALO:], xn_ref[...], w_ref, b_ref))
        dzn8 = jnp.where(i < nt - 1, dzn8, 0.0)
        row8 = lax.broadcasted_iota(jnp.int32, (HALO, w), 0)
        dx = dzc * w_ref[CONV_K - 1:CONV_K, :]
        for j in range(CONV_K - 1):
            sh = CONV_K - 1 - j
            rolled = pltpu.roll(dzc, tm - sh, 0)
            bottom = jnp.where(row8 >= HALO - sh, pltpu.roll(dzn8, HALO - sh, 0), rolled[tm - HALO:])
            dx = dx + jnp.concatenate([rolled[:tm - HALO], bottom], axis=0) * w_ref[j:j + 1, :]
        dx_ref[...] = dx

        @pl.when(i == 0)
        def _():
            dw_ref[...] = jnp.zeros_like(dw_ref)
            db_ref[...] = jnp.zeros_like(db_ref)

        for j in range(CONV_K):
            dw_ref[j:j + 1, :] += jnp.sum(dzc * taps[j], axis=0, keepdims=True)
        db_ref[...] += jnp.sum(dzc, axis=0, keepdims=True)

    last = s // HALO - 1
    halo_before = lambda col0: pl.BlockSpec((HALO, w), lambda j, i: (jnp.maximum(i * per - 1, 0), col0 + j))
    halo_after = lambda col0: pl.BlockSpec((HALO, w), lambda j, i: (jnp.minimum((i + 1) * per, last), col0 + j))
    tile = lambda col0: pl.BlockSpec((tm, w), lambda j, i: (i, col0 + j))
    return pl.pallas_call(
        body, name=name, grid=(2, nt),
        in_specs=[halo_before(QK_COL0), tile(QK_COL0), halo_after(QK_COL0), tile(0), halo_after(0),
                  pl.BlockSpec((CONV_K, w), lambda j, i: (0, j)), pl.BlockSpec((1, w), lambda j, i: (0, j))],
        out_specs=[tile(0), pl.BlockSpec((CONV_K, w), lambda j, i: (0, j)),
                   pl.BlockSpec((1, w), lambda j, i: (0, j))],
        out_shape=[jax.ShapeDtypeStruct((s, 2 * ML_W), F32), jax.ShapeDtypeStruct((CONV_K, 2 * ML_W), F32),
                   jax.ShapeDtypeStruct((1, 2 * ML_W), F32)],
        compiler_params=_params("parallel", "arbitrary"),
    )(proj, proj, proj, dqk, dqk, conv_w, conv_b)


def _bf16_mm(dims_fwd):
    @jax.custom_vjp
    def mm(a, b):
        return _dot(a.astype(BF16), b.astype(BF16), dims_fwd)

    def fwd(a, b):
        return mm(a, b), (a, b)

    def bwd(res, g):
        a, b = res
        if dims_fwd is NN:
            return _mm_nt(g, b), _mm_tn(a, g)
        if dims_fwd is NT:
            return _mm_nn(g, b), _mm_tn(g, a)
        return _mm_nt(b, g), _mm_nn(a, g)

    mm.defvjp(fwd, bwd)
    return mm


_mm_nn = _bf16_mm(NN)
_mm_nt = _bf16_mm(NT)
_mm_tn = _bf16_mm(TN)


def _tri(lower):
    r = lax.broadcasted_iota(jnp.int32, (CHUNK, CHUNK), 0)
    c = lax.broadcasted_iota(jnp.int32, (CHUNK, CHUNK), 1)
    return ((r >= c) if lower else (r <= c)).astype(F32)


@jax.custom_vjp
def _cumsum_rows(x):
    return lax.dot_general(_tri(True), x, NN, precision=lax.Precision.HIGHEST, preferred_element_type=F32)


def _cumsum_fwd(x):
    return _cumsum_rows(x), None


def _cumsum_bwd(_, g):
    return (lax.dot_general(_tri(False), g, NN, precision=lax.Precision.HIGHEST, preferred_element_type=F32),)


_cumsum_rows.defvjp(_cumsum_fwd, _cumsum_bwd)


def _abs(x):
    return jnp.where(x >= 0, x, -x)


def _log_sigmoid(x):
    return jnp.minimum(x, 0.0) - jnp.log(1.0 + jnp.exp(-_abs(x)))


def _pick_col(x, lane):
    sel = lax.broadcasted_iota(jnp.int32, x.shape, 1) == lane
    return jnp.sum(jnp.where(sel, x, 0.0), axis=1, keepdims=True)


def _pick_row(x, r):
    sel = lax.broadcasted_iota(jnp.int32, x.shape, 0) == r
    return jnp.sum(jnp.where(sel, x, 0.0), axis=0, keepdims=True)


def _mlstm_chunk(qs, ks, vs, oms, gates, gate_bias, mlg, cs, ns, ms):
    gb = gates + gate_bias
    cum = _cumsum_rows(_log_sigmoid(gb))
    gbt = gb.T
    cumt = cum.T
    causal = lax.broadcasted_iota(jnp.int32, (CHUNK, CHUNK), 0) >= lax.broadcasted_iota(jnp.int32, (CHUNK, CHUNK), 1)
    hd = range(ML_HEADS)
    k = [ks[h] * (ML_HD ** -0.5) for h in hd]
    ig_col = [_pick_col(gb, h) for h in hd]
    ig_row = [_pick_row(gbt, h) for h in hd]
    b_col = [_pick_col(cum, ML_HEADS + h) for h in hd]
    b_row = [_pick_row(cumt, ML_HEADS + h) for h in hd]
    g = [_pick_row(b_col[h], CHUNK - 1) for h in hd]
    a = [g[h] - b_col[h] + ig_col[h] for h in hd]
    m_loc = [jnp.max(a[h], axis=0, keepdims=True) for h in hd]
    wa = [jnp.exp(a[h] - m_loc[h]) for h in hd]
    d_log = [jnp.where(causal, b_col[h] - b_row[h] + ig_row[h], -jnp.inf) for h in hd]
    e_log = [b_col[h] + ms[h] for h in hd]
    m_t = [jnp.maximum(e_log[h], jnp.max(d_log[h], axis=1, keepdims=True)) for h in hd]
    d_w = [jnp.exp(d_log[h] - m_t[h]) for h in hd]
    e_w = [jnp.exp(e_log[h] - m_t[h]) for h in hd]
    qk = [_mm_nt(qs[h], k[h]) for h in hd]
    qc = [_mm_nt(qs[h], cs[h]) for h in hd]
    c_loc = [_mm_tn(wa[h] * vs[h], k[h]) for h in hd]
    s_qk = [qk[h] * d_w[h] for h in hd]
    sv = [_mm_nn(s_qk[h], vs[h]) for h in hd]
    n_loc = [jnp.sum(wa[h] * k[h], axis=0, keepdims=True) for h in hd]
    m_out = [jnp.maximum(g[h] + ms[h], m_loc[h]) for h in hd]
    sp = [jnp.exp(g[h] + ms[h] - m_out[h]) for h in hd]
    sl = [jnp.exp(m_loc[h] - m_out[h]) for h in hd]
    c_out = [sp[h] * cs[h] + sl[h] * c_loc[h] for h in hd]
    n_out = [sp[h] * ns[h] + sl[h] * n_loc[h] for h in hd]
    num = [e_w[h] * qc[h] + sv[h] for h in hd]
    den = [e_w[h] * jnp.sum(qs[h] * ns[h], axis=1, keepdims=True) + jnp.sum(s_qk[h], axis=1, keepdims=True) for h in hd]
    hg = [_sigmoid(oms[h]) * (num[h] / jnp.maximum(_abs(den[h]), jnp.exp(-m_t[h]))) for h in hd]
    mu = [jnp.mean(hg[h], axis=1, keepdims=True) for h in hd]
    hc = [hg[h] - mu[h] for h in hd]
    var = [jnp.mean(hc[h] * hc[h], axis=1, keepdims=True) for h in hd]
    ys = [hc[h] * lax.rsqrt(var[h] + LN_EPS) * mlg[h] for h in hd]
    return ys, c_out, n_out, m_out


V_COL = 5
O_COL = 6
ML_SUB = 1


def _mlstm_fwd(qk, proj, gates, gate_bias, mlg, name="mlstm_fwd", gather=()):
    s = qk.shape[0]
    nc = s // CHUNK

    def body(q_ref, k_ref, v_ref, o_ref, g_ref, gb_ref, mlg_ref, y_ref, cp_ref, np_ref, mp_ref, c_s, n_s, m_s):
        ci = pl.program_id(0)

        @pl.when(ci == 0)
        def _():
            c_s[...] = jnp.zeros_like(c_s)
            n_s[...] = jnp.zeros_like(n_s)
            m_s[...] = jnp.zeros_like(m_s)

        for sub in range(ML_SUB):
            rows = slice(CHUNK * sub, CHUNK * (sub + 1))
            hs = lambda ref: [ref[rows, LANES * h:LANES * (h + 1)] for h in range(ML_HEADS)]
            cp_ref[sub] = c_s[...]
            np_ref[sub] = n_s[...]
            mp_ref[sub] = m_s[...]
            ys, c_new, n_new, m_new = _mlstm_chunk(
                hs(q_ref), hs(k_ref), hs(v_ref), hs(o_ref), g_ref[rows, :], gb_ref[...],
                [mlg_ref[:, LANES * h:LANES * (h + 1)] for h in range(ML_HEADS)],
                [c_s[h] for h in range(ML_HEADS)], [n_s[h:h + 1, :] for h in range(ML_HEADS)],
                [m_s[h:h + 1, 0:1] for h in range(ML_HEADS)])
            for h in range(ML_HEADS):
                y_ref[rows, LANES * h:LANES * (h + 1)] = ys[h]
                c_s[h] = c_new[h]
                n_s[h:h + 1, :] = n_new[h]
                m_s[h:h + 1, :] = jnp.broadcast_to(m_new[h], (1, LANES))

    blk = lambda col: pl.BlockSpec((ML_SUB * CHUNK, ML_W), lambda ci: (ci, col))
    vec = lambda w: pl.BlockSpec((1, w), lambda ci: (0, 0))
    return _call(
        body, name=name, grid=(nc // ML_SUB,), args=(qk, qk, proj, proj, gates, gate_bias, mlg), sem=("arbitrary",),
        gather=gather,
        in_specs=[blk(0), blk(1), blk(V_COL), blk(O_COL), pl.BlockSpec((ML_SUB * CHUNK, LANES), lambda ci: (ci, 0)),
                  vec(LANES), vec(ML_W)],
        out_specs=[blk(0), pl.BlockSpec((ML_SUB, ML_HEADS, ML_HD, ML_HD), lambda ci: (ci, 0, 0, 0)),
                   pl.BlockSpec((ML_SUB, 8, LANES), lambda ci: (ci, 0, 0)),
                   pl.BlockSpec((ML_SUB, 8, LANES), lambda ci: (ci, 0, 0))],
        out_shape=[jax.ShapeDtypeStruct((s, ML_W), F32), jax.ShapeDtypeStruct((nc, ML_HEADS, ML_HD, ML_HD), F32),
                   jax.ShapeDtypeStruct((nc, 8, LANES), F32), jax.ShapeDtypeStruct((nc, 8, LANES), F32)],
        scratch_shapes=[pltpu.VMEM((ML_HEADS, ML_HD, ML_HD), F32), pltpu.VMEM((8, LANES), F32),
                        pltpu.VMEM((8, LANES), F32)])


def _mlstm_bwd(qk, proj, gates, gate_bias, mlg, cprev, nprev, mprev, dy, name="mlstm_bwd", exchange=()):
    s = qk.shape[0]
    nc = s // CHUNK

    def body(q_ref, k_ref, v_ref, o_ref, g_ref, gb_ref, mlg_ref, cp_ref, np_ref, mp_ref, dy_ref,
             dqk_ref, dv_ref, do_ref, dg_ref, dgb_ref, dmlg_ref, dc_s, dn_s, dm_s, gb8, mg8):
        ci = pl.program_id(0)

        @pl.when(ci == 0)
        def _():
            dc_s[...] = jnp.zeros_like(dc_s)
            dn_s[...] = jnp.zeros_like(dn_s)
            dm_s[...] = jnp.zeros_like(dm_s)
            gb8[...] = jnp.zeros_like(gb8)
            mg8[...] = jnp.zeros_like(mg8)

        for sub in reversed(range(ML_SUB)):
            rows = slice(CHUNK * sub, CHUNK * (sub + 1))
            hs = lambda ref: [ref[rows, LANES * h:LANES * (h + 1)] for h in range(ML_HEADS)]
            prim = (hs(q_ref), hs(k_ref), hs(v_ref), hs(o_ref), g_ref[rows, :], gb_ref[...],
                    [mlg_ref[:, LANES * h:LANES * (h + 1)] for h in range(ML_HEADS)],
                    [cp_ref[sub, h] for h in range(ML_HEADS)], [np_ref[sub, h:h + 1, :] for h in range(ML_HEADS)],
                    [mp_ref[sub, h:h + 1, 0:1] for h in range(ML_HEADS)])
            _, vjp = jax.vjp(_mlstm_chunk, *prim)
            cot = (hs(dy_ref), [dc_s[h] for h in range(ML_HEADS)], [dn_s[h:h + 1, :] for h in range(ML_HEADS)],
                   [dm_s[h:h + 1, 0:1] for h in range(ML_HEADS)])
            dqs, dks, dvs, dos, dg, dgb, dmlg, dcs, dns, dms = vjp(cot)
            dg_ref[rows, :] = dg
            gb8[0:1, :] += dgb
            for h in range(ML_HEADS):
                sl = slice(LANES * h, LANES * (h + 1))
                dqk_ref[rows, sl] = dqs[h]
                dqk_ref[rows, ML_W + LANES * h:ML_W + LANES * (h + 1)] = dks[h]
                dv_ref[rows, sl] = dvs[h]
                do_ref[rows, sl] = dos[h]
                mg8[0:1, sl] += dmlg[h]
                dc_s[h] = dcs[h]
                dn_s[h:h + 1, :] = dns[h]
                dm_s[h:h + 1, :] = jnp.broadcast_to(dms[h], (1, LANES))

        @pl.when(ci == nb - 1)
        def _():
            dgb_ref[...] = gb8[0:1, :]
            dmlg_ref[...] = mg8[0:1, :]

    nb = nc // ML_SUB
    rev = lambda ci: nb - 1 - ci
    blk = lambda col: pl.BlockSpec((ML_SUB * CHUNK, ML_W), lambda ci: (rev(ci), col))
    vec = lambda w: pl.BlockSpec((1, w), lambda ci: (0, 0))
    st8 = pl.BlockSpec((ML_SUB, 8, LANES), lambda ci: (rev(ci), 0, 0))
    gsp = pl.BlockSpec((ML_SUB * CHUNK, LANES), lambda ci: (rev(ci), 0))
    return _call(
        body, name=name, grid=(nb,), sem=("arbitrary",), exchange=exchange,
        args=(qk, qk, proj, proj, gates, gate_bias, mlg, cprev, nprev, mprev, dy),
        in_specs=[blk(0), blk(1), blk(V_COL), blk(O_COL), gsp, vec(LANES), vec(ML_W),
                  pl.BlockSpec((ML_SUB, ML_HEADS, ML_HD, ML_HD), lambda ci: (rev(ci), 0, 0, 0)), st8, st8, blk(1)],
        out_specs=[pl.BlockSpec((ML_SUB * CHUNK, 2 * ML_W), lambda ci: (rev(ci), 0)), blk(0), blk(0), gsp, vec(LANES),
                   vec(ML_W)],
        out_shape=[jax.ShapeDtypeStruct((s, 2 * ML_W), F32),
                   jax.ShapeDtypeStruct((s, ML_W), F32), jax.ShapeDtypeStruct((s, ML_W), F32),
                   jax.ShapeDtypeStruct((s, LANES), F32), jax.ShapeDtypeStruct((1, LANES), F32),
                   jax.ShapeDtypeStruct((1, ML_W), F32)],
        scratch_shapes=[pltpu.VMEM((ML_HEADS, ML_HD, ML_HD), F32), pltpu.VMEM((8, LANES), F32),
                        pltpu.VMEM((8, LANES), F32), pltpu.VMEM((8, LANES), F32), pltpu.VMEM((8, ML_W), F32)])


def _xattn_tile(qs, ks, vs):
    hd = range(XA_HEADS)
    sc = [_mm_nt(qs[h], ks[h]) * (XA_HD ** -0.5) for h in hd]
    mx = [lax.stop_gradient(jnp.max(sc[h], axis=1, keepdims=True)) for h in hd]
    pe = [jnp.exp(sc[h] - mx[h]) for h in hd]
    pn = [pe[h] / jnp.sum(pe[h], axis=1, keepdims=True) for h in hd]
    return [_mm_nn(pn[h], vs[h]) for h in hd]


def _xa_heads(ref):
    return [ref[:, XA_HD * h:XA_HD * (h + 1)] for h in range(XA_HEADS)]


def _xattn_fwd(q, kv, name="xattn_fwd", tm=512):
    s, d = q.shape

    def body(q_ref, k_ref, v_ref, o_ref):
        outs = _xattn_tile(_xa_heads(q_ref), _xa_heads(k_ref), _xa_heads(v_ref))
        for h in range(XA_HEADS):
            o_ref[:, XA_HD * h:XA_HD * (h + 1)] = outs[h]

    row = pl.BlockSpec((tm, d), lambda i: (i, 0))
    return pl.pallas_call(
        body, name=name, grid=(s // tm,),
        in_specs=[row, pl.BlockSpec((MEM_LEN, d), lambda i: (0, 0)), pl.BlockSpec((MEM_LEN, d), lambda i: (0, 1))],
        out_specs=row, out_shape=jax.ShapeDtypeStruct((s, d), F32),
        compiler_params=_params("parallel"),
    )(q, kv, kv)


def _xattn_bwd(q, kv, do, name="xattn_bwd", tm=512):
    s, d = q.shape

    def body(q_ref, k_ref, v_ref, do_ref, dq_ref, dkv_ref):
        i = pl.program_id(0)
        _, vjp = jax.vjp(_xattn_tile, _xa_heads(q_ref), _xa_heads(k_ref), _xa_heads(v_ref))
        dqs, dks, dvs = vjp(_xa_heads(do_ref))

        @pl.when(i == 0)
        def _():
            dkv_ref[...] = jnp.zeros_like(dkv_ref)

        for h in range(XA_HEADS):
            sl = slice(XA_HD * h, XA_HD * (h + 1))
            dq_ref[:, sl] = dqs[h]
            dkv_ref[:, sl] += dks[h]
            dkv_ref[:, d + XA_HD * h:d + XA_HD * (h + 1)] += dvs[h]

    row = pl.BlockSpec((tm, d), lambda i: (i, 0))
    return pl.pallas_call(
        body, name=name, grid=(s // tm,),
        in_specs=[row, pl.BlockSpec((MEM_LEN, d), lambda i: (0, 0)), pl.BlockSpec((MEM_LEN, d), lambda i: (0, 1)), row],
        out_specs=[row, pl.BlockSpec((MEM_LEN, 2 * d), lambda i: (0, 0))],
        out_shape=[jax.ShapeDtypeStruct((s, d), F32), jax.ShapeDtypeStruct((MEM_LEN, 2 * d), F32)],
        compiler_params=_params("arbitrary"),
    )(q, kv, kv, do)


def _loss_head(y, target, name="loss_head", tm=1024):
    s, d = y.shape
    nt = s // tm

    def body(y_ref, t_ref, dy_ref, loss_ref, acc):
        i = pl.program_id(0)
        err = y_ref[...] - t_ref[...]
        dy_ref[...] = err * (1.0 / d)

        @pl.when(i == 0)
        def _():
            acc[...] = jnp.zeros_like(acc)

        acc[...] += _rowsum8(err * err)

        @pl.when(i == nt - 1)
        def _():
            tot = jnp.sum(jnp.sum(acc[...], axis=0, keepdims=True), axis=1, keepdims=True)
            loss_ref[...] = jnp.broadcast_to(tot * (0.5 / d), (1, LANES))

    row = pl.BlockSpec((tm, d), lambda i: (i, 0))
    return pl.pallas_call(
        body, name=name, grid=(nt,),
        in_specs=[row, row], out_specs=[row, pl.BlockSpec((1, LANES), lambda i: (0, 0))],
        out_shape=[jax.ShapeDtypeStruct((s, d), F32), jax.ShapeDtypeStruct((1, LANES), F32)],
        scratch_shapes=[pltpu.VMEM((8, d), F32)],
        compiler_params=_params("arbitrary"),
    )(y, target)


def _adam2d(recv, w, m, v, name, layer=None):
    rows, cols = w.shape[-2:]
    fits = [t for t in range(16, rows + 1, 16) if rows % t == 0 and t * cols <= 128 * 1024]
    tr = max(fits) if fits else rows

    def body(r_ref, w_ref, m_ref, v_ref, g_ref, d_ref, mo_ref, vo_ref):
        g = r_ref[0].astype(F32)
        for j in range(1, N_DEV):
            g = g + r_ref[j].astype(F32)
        mn = ADAM_B1 * m_ref[...] + (1.0 - ADAM_B1) * g
        vn = ADAM_B2 * v_ref[...] + (1.0 - ADAM_B2) * jnp.square(g)
        m_hat = mn / (1.0 - ADAM_B1 ** ADAM_STEP)
        v_hat = vn / (1.0 - ADAM_B2 ** ADAM_STEP)
        g_ref[...] = g
        d_ref[...] = -ADAM_LR * (m_hat / (jnp.sqrt(v_hat) + ADAM_EPS) + ADAM_WD * w_ref[...])
        mo_ref[...] = mn
        vo_ref[...] = vn

    row = pl.BlockSpec((tr, cols), lambda i: (i, 0))
    if layer is None:
        wspec = row
    else:
        wspec = pl.BlockSpec((None, None, tr, cols), lambda i: (0, layer, i, 0))
    return pl.pallas_call(
        body, name=name, grid=(rows // tr,),
        in_specs=[pl.BlockSpec((N_DEV, tr, cols), lambda i: (0, i, 0)), wspec, wspec, wspec],
        out_specs=[row] * 4, out_shape=[jax.ShapeDtypeStruct((rows, cols), F32)] * 4,
        compiler_params=_params("parallel"),
    )(recv, w, m, v)


WEIGHTS = ("rel_bias", "ln_g", "ln_b", "ffn_w_gate", "ffn_w_up", "ffn_w_down", "w_in", "conv_w", "conv_b",
           "ig_bias", "fg_bias", "ml_norm_g", "w_out", "xq_w", "xkv_w", "xo_w")
SMALL = ("rel_bias", "ln_g", "ln_b", "conv_w", "conv_b", "ig_bias", "fg_bias", "ml_norm_g")
SMALL_SHAPES = {
    "rel_bias": (REL_BUCKETS, ATT_HEADS), "ln_g": (1, 4, LANES), "ln_b": (1, 4, LANES), "conv_w": (1, CONV_K, LANES),
    "conv_b": (1, 2 * ML_W), "ig_bias": (1, ML_HEADS), "fg_bias": (1, ML_HEADS), "ml_norm_g": (1, ML_W),
}
SMALL_ROWS = 8


def _pack_small(parts, lead=()):
    out = []
    for p in parts:
        p = jnp.pad(p, [(0, 0)] * len(lead) + [(0, SMALL_ROWS * LANES - p.shape[-1])])
        out.append(p.reshape(lead + (SMALL_ROWS, LANES)))
    return jnp.concatenate(out, axis=len(lead))


def _unpack_small(flat):
    out = {}
    for i, n in enumerate(SMALL):
        cnt = int(np.prod(SMALL_SHAPES[n]))
        out[n] = flat[SMALL_ROWS * i:SMALL_ROWS * (i + 1)].reshape(-1)[:cnt].reshape(SMALL_SHAPES[n])
    return out


def _split8(full, axis):
    shp = full.shape
    t = full.reshape(shp[:axis] + (N_DEV, shp[axis] // N_DEV) + shp[axis + 1:])
    return jnp.moveaxis(t, axis, 0).reshape(N_DEV, -1)


def _rep8(full):
    return jnp.broadcast_to(full.reshape(1, -1), (N_DEV, full.size))


def kernel(x, mem, rel_bias, ln_g, ln_b, ffn_w_gate, ffn_w_up, ffn_w_down, w_in, conv_w, conv_b, ig_bias, fg_bias, ml_norm_g, w_out, xq_w, xkv_w, xo_w, loss_target, m_rel_bias, m_ln_g, m_ln_b, m_ffn_w_gate, m_ffn_w_up, m_ffn_w_down, m_w_in, m_conv_w, m_conv_b, m_ig_bias, m_fg_bias, m_ml_norm_g, m_w_out, m_xq_w, m_xkv_w, m_xo_w, v_rel_bias, v_ln_g, v_ln_b, v_ffn_w_gate, v_ffn_w_up, v_ffn_w_down, v_w_in, v_conv_w, v_conv_b, v_ig_bias, v_fg_bias, v_ml_norm_g, v_w_out, v_xq_w, v_xkv_w, v_xo_w):
    w_tree = dict(rel_bias=rel_bias, ln_g=ln_g, ln_b=ln_b, ffn_w_gate=ffn_w_gate, ffn_w_up=ffn_w_up,
                  ffn_w_down=ffn_w_down, w_in=w_in, conv_w=conv_w, conv_b=conv_b, ig_bias=ig_bias, fg_bias=fg_bias,
                  ml_norm_g=ml_norm_g, w_out=w_out, xq_w=xq_w, xkv_w=xkv_w, xo_w=xo_w)
    m_tree = dict(rel_bias=m_rel_bias, ln_g=m_ln_g, ln_b=m_ln_b, ffn_w_gate=m_ffn_w_gate, ffn_w_up=m_ffn_w_up,
                  ffn_w_down=m_ffn_w_down, w_in=m_w_in, conv_w=m_conv_w, conv_b=m_conv_b, ig_bias=m_ig_bias,
                  fg_bias=m_fg_bias, ml_norm_g=m_ml_norm_g, w_out=m_w_out, xq_w=m_xq_w, xkv_w=m_xkv_w, xo_w=m_xo_w)
    v_tree = dict(rel_bias=v_rel_bias, ln_g=v_ln_g, ln_b=v_ln_b, ffn_w_gate=v_ffn_w_gate, ffn_w_up=v_ffn_w_up,
                  ffn_w_down=v_ffn_w_down, w_in=v_w_in, conv_w=v_conv_w, conv_b=v_conv_b, ig_bias=v_ig_bias,
                  fg_bias=v_fg_bias, ml_norm_g=v_ml_norm_g, w_out=v_w_out, xq_w=v_xq_w, xkv_w=v_xkv_w, xo_w=v_xo_w)
    x0 = x[0]
    pad_ff = FF_PAD - FF_SHARD
    bf = lambda t: t.astype(BF16)

    pad_rows = lambda t: jnp.pad(t, ((0, pad_ff), (0, 0)))
    ffn_shards = [(pad_rows(bf(ffn_w_gate[0, l]).T), pad_rows(bf(ffn_w_up[0, l]).T), pad_rows(bf(ffn_w_down[0, l])))
                  for l in range(2)]
    pairs = lambda t: t.reshape(N_PAIR, FF_PAIR, D_MODEL)
    w_in_shard = jnp.pad(bf(w_in[0]), ((0, 0), (0, ATT_W - W_IN_SHARD)))
    small_shard = jnp.concatenate([ln_g[0], ln_b[0], conv_w[0], jnp.zeros((4, LANES), F32)], axis=0)
    gate_bias = jnp.pad(jnp.concatenate([ig_bias, fg_bias], axis=1), ((0, 0), (0, LANES - 2 * ML_HEADS)))
    buckets = _bucket_tables()

    wg0, wu0, wd0, small_all = _gather_two_level("ffn1_weights_gather", ffn_shards[0] + (small_shard,))
    wg0, wu0, wd0 = pairs(wg0), pairs(wu0), pairs(wd0)
    unshard = lambda t: jnp.moveaxis(t, 0, 1).reshape(4, D_MODEL)
    ln_g_full, ln_b_full, conv_w_full = unshard(small_all[:, 0:4]), unshard(small_all[:, 4:8]), unshard(small_all[:, 8:12])
    lng = lambda i: ln_g_full[i:i + 1]
    lnb = lambda i: ln_b_full[i:i + 1]

    u0, x1, a0, b0, win_all, wout_all, xq_all, xo_all, xkv_all = _ffn_fwd(
        x0, wg0, wu0, wd0, lng(0), lnb(0), "ffn1_fwd",
        gather=(w_in_shard, bf(w_out[0]), bf(xq_w[0]), bf(xo_w[0]), bf(xkv_w[0])))
    w_in_full = jnp.moveaxis(win_all[:, :, :W_IN_SHARD], 0, 1).reshape(D_MODEL, W_IN)
    w_main = w_in_full[:, :W_IN_MAIN]
    w_gate_cols = jnp.pad(w_in_full[:, W_IN_MAIN:], ((0, 0), (0, LANES - 2 * ML_HEADS)))
    w_out_full = wout_all.reshape(D_MODEL, D_MODEL)
    xq_full = xq_all.reshape(D_MODEL, D_MODEL)
    xo_full = xo_all.reshape(D_MODEL, D_MODEL)

    proj, wg1 = _matmul(x1, w_main, "nn", "proj_fwd", tn=W_IN_MAIN // 2, tk=D_MODEL, gather=(ffn_shards[1][0],))
    gates, = _matmul(x1, w_gate_cols, "nn", "gates_fwd", tk=D_MODEL)
    biasm = _bias_fwd(rel_bias, buckets)
    att, lse, wd1 = _dil_fwd(proj, biasm, gather=(ffn_shards[1][2],))
    qk = _conv_fwd(proj, conv_w_full, conv_b)
    y_m, c_prev, n_prev, m_prev, wu1 = _mlstm_fwd(qk, proj, gates, gate_bias, ml_norm_g, gather=(ffn_shards[1][1],))
    u1, x2 = _matmul_resid_ln((att, y_m), w_out_full, x1, lng(1), lnb(1), "w_out_fwd")
    q_x, = _matmul(x2, xq_full, "nn", "xq_fwd", tn=D_MODEL, tk=D_MODEL)
    kv, = _matmul(mem[0], xkv_all, "nn", "xkv_fwd", tk=D_MODEL)
    o_x = _xattn_fwd(q_x, kv)
    u2, x3 = _matmul_resid_ln((o_x,), xo_full, x2, lng(2), lnb(2), "xo_fwd")
    wg1, wu1, wd1 = pairs(wg1), pairs(wu1), pairs(wd1)
    u3, x4, a3, b3 = _ffn_fwd(x3, wg1, wu1, wd1, lng(3), lnb(3), "ffn2_fwd")
    dx4, loss_row = _loss_head(x4, loss_target[0])

    dx3, xb, df, da, db, hh, dg3, db3 = _ffn_bwd_x(dx4, u3, x3, wg1, wu1, wd1, lng(3), a3, b3, "ffn2_bwd_x")
    ffn2_send = (_ffn_bwd_w(xb, da, "ffn2_bwd_wg", down=False)[0], _ffn_bwd_w(xb, db, "ffn2_bwd_wu", down=False)[0],
                 _ffn_bwd_w(df, hh, "ffn2_bwd_wd", down=True)[0])

    du2, do_x, dg2, db2 = _ln_bwd(dx3, u2, lng(2), xo_full, "xattn_ln_bwd")
    g_xo, = _matmul(o_x, du2, "tn", "xo_bwd_w", tm=D_MODEL, tn=D_MODEL, out_dtype=BF16)
    dq_x, dkv = _xattn_bwd(q_x, kv, do_x)
    g_xq, = _matmul(x2, dq_x, "tn", "xq_bwd_w", tm=D_MODEL, tn=D_MODEL, out_dtype=BF16)
    g_xkv, = _matmul(mem[0], dkv, "tn", "xkv_bwd_w", tm=D_MODEL, tn=2 * D_MODEL // N_DEV, tk=MEM_LEN,
                     out_dtype=BF16, blocked_out=True)
    dx2, = _matmul(dq_x, xq_full, "nt", "xq_bwd_x", tn=D_MODEL, tk=D_MODEL, add=du2, add_scale=ALPHA)

    du1, dcat, dg1, db1 = _ln_bwd(dx2, u1, lng(1), w_out_full, "mixer_ln_bwd")
    g_w_out = jnp.concatenate(
        [_matmul(half, du1, "tn", f"w_out_bwd_w_{i}", tn=D_MODEL, out_dtype=BF16)[0] for i, half in enumerate((att, y_m))],
        axis=0)
    dqk, dv_m, do_m, dgates, dgate_bias, g_mlg, *ffn2_recv = _mlstm_bwd(
        qk, proj, gates, gate_bias, ml_norm_g, c_prev, n_prev, m_prev, dcat, exchange=tuple(ffn2_send))
    dqk_pre, g_conv_w, g_conv_b = _conv_bwd(proj, dqk, conv_w_full, conv_b)
    dq_a, dk_a, dv_a, dbias = _dil_bwd(proj, biasm, lse, att, dcat)
    g_rel = _bias_bwd(dbias.reshape(biasm.shape), buckets)[:, :ATT_HEADS]
    dproj = jnp.concatenate([dq_a, dk_a, dv_a, bf(dqk_pre), bf(dv_m), bf(do_m)], axis=1)
    g_w_main, = _matmul(x1, dproj, "tn", "proj_bwd_w", tm=D_MODEL, tn=W_IN_MAIN // 2, tk=1024, out_dtype=BF16)
    g_w_gates, = _matmul(x1, dgates, "tn", "gates_bwd_w", tm=D_MODEL, out_dtype=BF16)
    g_w_in = jnp.concatenate([g_w_main, g_w_gates[:, :2 * ML_HEADS]], axis=1)
    dx1, = _matmul(dproj, w_main, "nt", "proj_bwd_x", tn=D_MODEL, tk=W_IN_MAIN // 2, add=du1, add_scale=ALPHA)
    dx1, = _matmul(dgates, w_gate_cols, "nt", "gates_bwd_x", tn=D_MODEL, add=dx1)

    rows8 = lambda t: t.reshape(N_DEV, D_MODEL // N_DEV, D_MODEL)
    mid_send = (rows8(g_xo), rows8(g_xq), g_xkv, rows8(g_w_out),
                jnp.moveaxis(g_w_in.reshape(D_MODEL, N_DEV, W_IN_SHARD), 1, 0))
    dx0, xb, df, da, db, hh, dg0, db0, r_xo, r_xq, r_xkv, r_w_out, r_w_in = _ffn_bwd_x(
        dx1, u0, x0, wg0, wu0, wd0, lng(0), a0, b0, "ffn1_bwd_x", exchange=mid_send)
    small_blocks = {
        "rel_bias": _rep8(g_rel),
        "ln_g": _split8(jnp.concatenate([dg0, dg1, dg2, dg3], axis=0), 1),
        "ln_b": _split8(jnp.concatenate([db0, db1, db2, db3], axis=0), 1),
        "conv_w": _split8(g_conv_w, 1),
        "conv_b": _rep8(g_conv_b),
        "ig_bias": _rep8(dgate_bias[:, :ML_HEADS]),
        "fg_bias": _rep8(dgate_bias[:, ML_HEADS:2 * ML_HEADS]),
        "ml_norm_g": _rep8(g_mlg),
    }
    small_send = _pack_small([small_blocks[n] for n in SMALL], lead=(N_DEV,))
    g_wg, r_small = _ffn_bwd_w(xb, da, "ffn1_bwd_wg", down=False, exchange=(small_send,))
    g_wu, r_wg = _ffn_bwd_w(xb, db, "ffn1_bwd_wu", down=False, exchange=(g_wg,))
    g_wd, r_wu = _ffn_bwd_w(df, hh, "ffn1_bwd_wd", down=True, exchange=(g_wu,))
    r_wd, = _exchange_only("ffn1_grads_exchange", exchange=(g_wd,))
    ffn1_recv = [r_wg, r_wu, r_wd]

    res = {}
    for i, n in enumerate(("ffn_w_gate", "ffn_w_up", "ffn_w_down")):
        per_layer = [_adam2d(r[i], w_tree[n], m_tree[n], v_tree[n], f"adamw_{n}_{l}", layer=l)
                     for l, r in enumerate((ffn1_recv, ffn2_recv))]
        res[n] = [jnp.stack([per_layer[0][j], per_layer[1][j]])[None] for j in range(4)]
    for n, r in (("w_in", r_w_in), ("w_out", r_w_out), ("xq_w", r_xq), ("xkv_w", r_xkv), ("xo_w", r_xo)):
        res[n] = [t[None] for t in _adam2d(r, w_tree[n][0], m_tree[n][0], v_tree[n][0], f"adamw_{n}")]
    pack = lambda tree: _pack_small([tree[n].reshape(-1) for n in SMALL])
    small = [_unpack_small(t) for t in _adam2d(r_small, pack(w_tree), pack(m_tree), pack(v_tree), "adamw_small")]
    for n in SMALL:
        res[n] = [small[j][n] for j in range(4)]

    loss = lax.psum(loss_row[0, 0], ("x", "y", "c"))
    return (loss, dx0[None], *[res[n][0] for n in WEIGHTS], *[res[n][1] for n in WEIGHTS],
            *[res[n][2] for n in WEIGHTS], *[res[n][3] for n in WEIGHTS])
```

```python
import math

import numpy as np
import jax
import jax.numpy as jnp
from jax import lax
from jax.experimental import pallas as pl
from jax.experimental.pallas import tpu as pltpu

F32 = jnp.float32
BF16 = jnp.bfloat16

N_DEV = 8
D_MODEL = 1024
D_FF = 2816
FF_SHARD = D_FF // N_DEV
FF_PAD = 384
ATT_W = 512
ATT_HEADS = 8
DILATED = ((128, 1), (512, 4), (2048, 16))
BLK = 128
ML_W = 512
ML_HEADS = 4
ML_HD = 128
CHUNK = 128
CONV_K = 4
W_IN = 3592
W_IN_SHARD = W_IN // N_DEV
W_IN_MAIN = 3584
XA_HEADS = 4
XA_HD = 256
MEM_LEN = 256
REL_BUCKETS = 32
REL_MAX_DIST = 2048
ALPHA = 2.0 ** 0.25
LN_EPS = 1e-5
NEG = -1e30
ADAM_LR = 0.001
ADAM_B1 = 0.9
ADAM_B2 = 0.999
ADAM_EPS = 1e-08
ADAM_WD = 0.01
ADAM_STEP = 10
LANES = 128
VMEM_LIMIT = 58 * 1024 * 1024

NN = (((1,), (0,)), ((), ()))
NT = (((1,), (1,)), ((), ()))
TN = (((0,), (0,)), ((), ()))


def _dot(a, b, dims):
    return lax.dot_general(a, b, dims, preferred_element_type=F32)


def _params(*sem):
    return pltpu.CompilerParams(dimension_semantics=sem, vmem_limit_bytes=VMEM_LIMIT)


def _sigmoid(x):
    return 0.5 * jnp.tanh(0.5 * x) + 0.5


def _rowsum8(x):
    t, c = x.shape
    return jnp.sum(x.reshape(t // 8, 8, c), axis=0)


def _mesh_pos():
    x, y, c = lax.axis_index("x"), lax.axis_index("y"), lax.axis_index("c")
    return x, y, c, 4 * x + 2 * y + c


def _peer(x, y, c, k):
    px = 1 - x if k & 4 else x
    py = 1 - y if k & 2 else y
    pc = 1 - c if k & 1 else c
    return (px, py, pc), 4 * px + 2 * py + pc


def _call(body, *, name, grid, in_specs, out_specs, out_shape, args, scratch_shapes=(), sem=None,
          gather=(), exchange=()):
    in_specs, out_specs, out_shape, scratch = list(in_specs), list(out_specs), list(out_shape), list(scratch_shapes)
    ng, nc = len(gather), len(gather) + len(exchange)
    if nc == 0:
        return pl.pallas_call(body, name=name, grid=grid, in_specs=in_specs, out_specs=out_specs,
                              out_shape=out_shape, scratch_shapes=scratch, compiler_params=_params(*sem))(*args)
    n_in, n_out, n_scr = len(in_specs), len(out_specs), len(scratch)

    def wrapped(*refs):
        ins, cin = refs[:n_in], refs[n_in:n_in + nc]
        outs, cout = refs[n_in + nc:n_in + nc + n_out], refs[n_in + nc + n_out:n_in + 2 * nc + n_out]
        scr = refs[n_in + 2 * nc + n_out:n_in + 2 * nc + n_out + n_scr]
        send_sems, recv_sems, loc_sems = refs[-3:]
        first, last = None, None
        for ax, extent in enumerate(grid):
            f, l = pl.program_id(ax) == 0, pl.program_id(ax) == extent - 1
            first = f if first is None else first & f
            last = l if last is None else last & l

        def copies():
            x, y, c, me = _mesh_pos()
            out = []
            for a in range(nc):
                mine = cin[a] if a < ng else cin[a].at[me]
                out.append(pltpu.make_async_copy(mine, cout[a].at[me], loc_sems.at[a]))
                for k in range(1, N_DEV):
                    peer, pidx = _peer(x, y, c, k)
                    out.append(pltpu.make_async_remote_copy(
                        src_ref=cin[a] if a < ng else cin[a].at[pidx], dst_ref=cout[a].at[me],
                        send_sem=send_sems.at[a, k - 1], recv_sem=recv_sems.at[a, k - 1],
                        device_id=peer, device_id_type=pl.DeviceIdType.MESH))
            return out

        @pl.when(first)
        def _():
            for cp in copies():
                cp.start()

        body(*ins, *outs, *scr)

        @pl.when(last)
        def _():
            for cp in copies():
                cp.wait()

    hbm = pl.BlockSpec(memory_space=pl.ANY)
    comm_shapes = [jax.ShapeDtypeStruct((N_DEV,) + a.shape, a.dtype) for a in gather]
    comm_shapes += [jax.ShapeDtypeStruct(a.shape, a.dtype) for a in exchange]
    return pl.pallas_call(
        wrapped, name=name, grid=grid, in_specs=in_specs + [hbm] * nc, out_specs=out_specs + [hbm] * nc,
        out_shape=out_shape + comm_shapes,
        scratch_shapes=scratch + [pltpu.SemaphoreType.DMA((nc, N_DEV - 1)), pltpu.SemaphoreType.DMA((nc, N_DEV - 1)),
                                  pltpu.SemaphoreType.DMA((nc,))],
        compiler_params=_params(*(("arbitrary",) * len(grid))),
    )(*args, *gather, *exchange)


def _gather_two_level(name, arrays):
    na = len(arrays)

    def body(*refs):
        srcs, outs = refs[:na], refs[na:2 * na]
        send_sems, recv_sems, loc_sems = refs[2 * na:]
        x, y, c, me = _mesh_pos()
        here, sib = (x, y, c), (x, y, 1 - c)
        chips = [(1 - x, y), (x, 1 - y), (1 - x, 1 - y)]
        pos = lambda px, py, pc: 4 * px + 2 * py + pc

        def copy(a, k, block, to, src=None):
            return pltpu.make_async_remote_copy(
                src_ref=outs[a].at[block] if src is None else src, dst_ref=outs[a].at[block],
                send_sem=send_sems.at[a, k], recv_sem=recv_sems.at[a, k], device_id=to,
                device_id_type=pl.DeviceIdType.MESH)

        locs = [pltpu.make_async_copy(srcs[a], outs[a].at[me], loc_sems.at[a]) for a in range(na)]
        for cp in locs:
            cp.start()
        first = []
        for a in range(na):
            first.append(copy(a, 0, me, sib, src=srcs[a]))
            first += [copy(a, 1 + j, me, (*chip, c), src=srcs[a]) for j, chip in enumerate(chips)]
        for cp in first:
            cp.start()
        passed = []
        for a in range(na):
            for j, chip in enumerate(chips):
                copy(a, 1 + j, pos(*chip, c), here).wait_recv()
                passed.append(copy(a, 4 + j, pos(*chip, c), sib))
                passed[-1].start()
        for a in range(na):
            copy(a, 0, pos(x, y, 1 - c), here).wait_recv()
            for j, chip in enumerate(chips):
                copy(a, 4 + j, pos(*chip, 1 - c), here).wait_recv()
        for cp in first + passed:
            cp.wait_send()
        for cp in locs:
            cp.wait()

    hbm = pl.BlockSpec(memory_space=pl.ANY)
    return pl.pallas_call(
        body, name=name, in_specs=[hbm] * na, out_specs=[hbm] * na,
        out_shape=[jax.ShapeDtypeStruct((N_DEV,) + a.shape, a.dtype) for a in arrays],
        scratch_shapes=[pltpu.SemaphoreType.DMA((na, N_DEV - 1)), pltpu.SemaphoreType.DMA((na, N_DEV - 1)),
                        pltpu.SemaphoreType.DMA((na,))],
    )(*arrays)


def _exchange_only(name, gather=(), exchange=()):
    return _call(lambda: None, name=name, grid=(1,), in_specs=[], out_specs=[], out_shape=[], args=(),
                 gather=gather, exchange=exchange)


def _matmul(a, b, mode, name, *, out_dtype=F32, tm=1024, tn=512, tk=512, add=None, add_scale=1.0,
            blocked_out=False, gather=(), exchange=()):
    blocked_b = b.ndim == 3
    if blocked_b:
        (m, k), (nb, _, tn) = a.shape, b.shape
        n = nb * tn
    elif mode == "nn":
        (m, k), (_, n) = a.shape, b.shape
    elif mode == "nt":
        (m, k), (n, _) = a.shape, b.shape
    else:
        (k, m), (_, n) = a.shape, b.shape
    tm, tn, tk = min(tm, m), min(tn, n), min(tk, k)
    nk = k // tk
    dims = {"nn": NN, "nt": NT, "tn": TN}[mode]
    if mode == "tn":
        a_spec = pl.BlockSpec((tk, tm), lambda i, j, kk: (kk, i))
    else:
        a_spec = pl.BlockSpec((tm, tk), lambda i, j, kk: (i, kk))
    if blocked_b:
        b_spec = pl.BlockSpec((None, tk, tn), lambda i, j, kk: (j, kk, 0))
    elif mode == "nt":
        b_spec = pl.BlockSpec((tn, tk), lambda i, j, kk: (j, kk))
    else:
        b_spec = pl.BlockSpec((tk, tn), lambda i, j, kk: (kk, j))
    if blocked_out:
        o_spec = pl.BlockSpec((None, tm, tn), lambda i, j, kk: (j, i, 0))
        o_shape = jax.ShapeDtypeStruct((n // tn, m, tn), out_dtype)
    else:
        o_spec = pl.BlockSpec((tm, tn), lambda i, j, kk: (i, j))
        o_shape = jax.ShapeDtypeStruct((m, n), out_dtype)
    has_add = add is not None
    cache_a = nk == 1 and mode != "tn" and n // tn > 1 and a.dtype != BF16

    def body(*refs):
        if has_add:
            a_ref, b_ref, add_ref, o_ref, s_ref = refs
        else:
            a_ref, b_ref, o_ref, s_ref = refs
        kk = pl.program_id(2)
        if cache_a:
            @pl.when(pl.program_id(1) == 0)
            def _():
                s_ref[...] = a_ref[...].astype(BF16)

            lhs = s_ref[...]
        else:
            lhs = a_ref[...].astype(BF16)
        part = _dot(lhs, b_ref[...].astype(BF16), dims)

        def finish(r):
            if has_add:
                r = r + add_scale * add_ref[...]
            o_ref[...] = r.astype(out_dtype)

        if nk == 1:
            finish(part)
            return

        @pl.when(kk == 0)
        def _():
            s_ref[...] = part

        @pl.when(kk > 0)
        def _():
            s_ref[...] += part

        @pl.when(kk == nk - 1)
        def _():
            finish(s_ref[...])

    if nk > 1:
        scratch = [pltpu.VMEM((tm, tn), F32)]
    else:
        scratch = [pltpu.VMEM((tm, tk), BF16) if cache_a else pltpu.VMEM((8, LANES), F32)]
    return _call(
        body, name=name, grid=(m // tm, n // tn, nk),
        in_specs=[a_spec, b_spec] + ([pl.BlockSpec((tm, tn), lambda i, j, kk: (i, j))] if has_add else []),
        out_specs=[o_spec], out_shape=[o_shape], args=(a, b) + ((add,) if has_add else ()),
        scratch_shapes=scratch, sem=("parallel", "arbitrary", "arbitrary"),
        gather=gather, exchange=exchange)


def _ln_fwd_math(u, g, b):
    mu = jnp.mean(u, axis=-1, keepdims=True)
    uc = u - mu
    var = jnp.mean(uc * uc, axis=-1, keepdims=True)
    return uc * lax.rsqrt(var + LN_EPS) * g + b


def _ln_bwd_math(dy, u, g):
    mu = jnp.mean(u, axis=-1, keepdims=True)
    uc = u - mu
    var = jnp.mean(uc * uc, axis=-1, keepdims=True)
    rstd = lax.rsqrt(var + LN_EPS)
    xhat = uc * rstd
    dxh = dy * g
    m1 = jnp.mean(dxh, axis=-1, keepdims=True)
    m2 = jnp.mean(dxh * xhat, axis=-1, keepdims=True)
    return rstd * (dxh - m1 - xhat * m2), xhat


def _matmul_resid_ln(pieces, w, x, g, b, name, tm=1024):
    s = pieces[0].shape[0]
    k, d = w.shape
    widths = [p.shape[1] for p in pieces]

    def body(*refs):
        a_refs = refs[:len(pieces)]
        w_ref, x_ref, g_ref, b_ref, u_ref, y_ref = refs[len(pieces):]
        u = ALPHA * x_ref[...]
        lo = 0
        for a_ref, width in zip(a_refs, widths):
            u = u + _dot(a_ref[...].astype(BF16), w_ref[lo:lo + width, :], NN)
            lo += width
        u_ref[...] = u
        y_ref[...] = _ln_fwd_math(u, g_ref[...], b_ref[...])

    row = pl.BlockSpec((tm, d), lambda i: (i, 0))
    vec = pl.BlockSpec((1, d), lambda i: (0, 0))
    return pl.pallas_call(
        body, name=name, grid=(s // tm,),
        in_specs=[pl.BlockSpec((tm, width), lambda i: (i, 0)) for width in widths]
        + [pl.BlockSpec((k, d), lambda i: (0, 0)), row, vec, vec],
        out_specs=[row, row], out_shape=[jax.ShapeDtypeStruct((s, d), F32)] * 2,
        compiler_params=_params("parallel"),
    )(*pieces, w, x, g, b)


def _ln_bwd(dy, u, g, w, name, tm=1024):
    s, d = dy.shape
    n = w.shape[0]
    nt = s // tm

    def body(dy_ref, u_ref, g_ref, w_ref, du_ref, dz_ref, dg_ref, db_ref, g8, b8):
        i = pl.program_id(0)
        dy_ = dy_ref[...]
        du, xhat = _ln_bwd_math(dy_, u_ref[...], g_ref[...])
        du_ref[...] = du
        dz_ref[...] = _dot(du.astype(BF16), w_ref[...], NT)

        @pl.when(i == 0)
        def _():
            g8[...] = jnp.zeros_like(g8)
            b8[...] = jnp.zeros_like(b8)

        g8[...] += _rowsum8(dy_ * xhat)
        b8[...] += _rowsum8(dy_)

        @pl.when(i == nt - 1)
        def _():
            dg_ref[...] = jnp.sum(g8[...], axis=0, keepdims=True)
            db_ref[...] = jnp.sum(b8[...], axis=0, keepdims=True)

    row = pl.BlockSpec((tm, d), lambda i: (i, 0))
    vec = pl.BlockSpec((1, d), lambda i: (0, 0))
    return pl.pallas_call(
        body, name=name, grid=(nt,),
        in_specs=[row, row, vec, pl.BlockSpec((n, d), lambda i: (0, 0))],
        out_specs=[row, pl.BlockSpec((tm, n), lambda i: (i, 0)), vec, vec],
        out_shape=[jax.ShapeDtypeStruct((s, d), F32), jax.ShapeDtypeStruct((s, n), F32),
                   jax.ShapeDtypeStruct((1, d), F32), jax.ShapeDtypeStruct((1, d), F32)],
        scratch_shapes=[pltpu.VMEM((8, d), F32), pltpu.VMEM((8, d), F32)],
        compiler_params=_params("arbitrary"),
    )(dy, u, g, w)


FF_PAIR = 2 * FF_PAD
N_PAIR = N_DEV // 2
FF_COLS = 256


def _ffn_fwd(x, wgt, wut, wd, g, b, name, tm=1024, gather=()):
    s, d = x.shape

    def body(x_ref, wg_ref, wu_ref, wd_ref, g_ref, b_ref, u_ref, y_ref, a_ref, bb_ref, xb, acc):
        k = pl.program_id(1)

        @pl.when(k == 0)
        def _():
            xb[...] = x_ref[...].astype(BF16)

        a = _dot(xb[...], wg_ref[...], NT)
        bb = _dot(xb[...], wu_ref[...], NT)
        a_ref[...] = a.astype(BF16)
        bb_ref[...] = bb.astype(BF16)
        h = (a * _sigmoid(a) * bb).astype(BF16)
        part = _dot(h, wd_ref[...], NN)

        @pl.when(k == 0)
        def _():
            acc[...] = part

        @pl.when(k > 0)
        def _():
            acc[...] += part

        @pl.when(k == N_PAIR - 1)
        def _():
            u = ALPHA * x_ref[...] + 0.5 * acc[...]
            u_ref[...] = u
            y_ref[...] = _ln_fwd_math(u, g_ref[...], b_ref[...])

    row = pl.BlockSpec((tm, d), lambda i, k: (i, 0))
    vec = pl.BlockSpec((1, d), lambda i, k: (0, 0))
    w_in = pl.BlockSpec((None, FF_PAIR, d), lambda i, k: (k, 0, 0))
    w_dn = w_in
    hid = pl.BlockSpec((tm, FF_PAIR), lambda i, k: (i, k))
    return _call(
        body, name=name, grid=(s // tm, N_PAIR),
        in_specs=[row, w_in, w_in, w_dn, vec, vec], out_specs=[row, row, hid, hid],
        out_shape=[jax.ShapeDtypeStruct((s, d), F32)] * 2 + [jax.ShapeDtypeStruct((s, N_DEV * FF_PAD), BF16)] * 2,
        args=(x, wgt, wut, wd, g, b),
        scratch_shapes=[pltpu.VMEM((tm, d), BF16), pltpu.VMEM((tm, d), F32)],
        sem=("parallel", "arbitrary"), gather=gather)


def _ffn_bwd_x(dy, u, x, wgt, wut, wd, g, a_fwd, b_fwd, name, tm=512, exchange=()):
    s, d = x.shape
    nt = s // tm
    ffp = N_DEV * FF_PAD

    def body(dy_ref, u_ref, x_ref, wg_ref, wu_ref, wd_ref, g_ref, a_ref, bb_ref,
             dx_ref, xb, df_ref, da_ref, db_ref, h_ref, dg_ref, dbl_ref,
             dfb, du_s, acc, g8, b8):
        i = pl.program_id(0)
        k = pl.program_id(1)

        @pl.when(k == 0)
        def _():
            dy_ = dy_ref[...]
            du, xhat = _ln_bwd_math(dy_, u_ref[...], g_ref[...])
            du_s[...] = du
            dfb[...] = (0.5 * du).astype(BF16)
            df_ref[...] = dfb[...]
            xb[...] = x_ref[...].astype(BF16)

            @pl.when(i == 0)
            def _():
                g8[...] = jnp.zeros_like(g8)
                b8[...] = jnp.zeros_like(b8)

            g8[...] += _rowsum8(dy_ * xhat)
            b8[...] += _rowsum8(dy_)

        dh_all = _dot(dfb[...], wd_ref[...], NT)

        def gate_grads(c):
            cs = slice(FF_COLS * c, FF_COLS * (c + 1))
            a = a_ref[:, cs].astype(F32)
            bb = bb_ref[:, cs].astype(F32)
            dh = dh_all[:, cs]
            sig = _sigmoid(a)
            sa = a * sig
            h_ref[:, cs] = (sa * bb).astype(BF16)
            da = (dh * bb * (sig * (1.0 + a * (1.0 - sig)))).astype(BF16)
            db = (dh * sa).astype(BF16)
            da_ref[:, cs] = da
            db_ref[:, cs] = db
            return da, db

        n_chunks = FF_PAIR // FF_COLS
        chunks = [gate_grads(0)]
        part = None
        for c in range(n_chunks):
            if c + 1 < n_chunks:
                chunks.append(gate_grads(c + 1))
            cs = slice(FF_COLS * c, FF_COLS * (c + 1))
            pc = _dot(chunks[c][0], wg_ref[cs, :], NN) + _dot(chunks[c][1], wu_ref[cs, :], NN)
            part = pc if part is None else part + pc

        @pl.when(k == 0)
        def _():
            acc[...] = part

        @pl.when(k > 0)
        def _():
            acc[...] += part

        @pl.when(k == N_PAIR - 1)
        def _():
            dx_ref[...] = ALPHA * du_s[...] + acc[...]

        @pl.when((k == N_PAIR - 1) & (i == nt - 1))
        def _():
            dg_ref[...] = jnp.sum(g8[...], axis=0, keepdims=True)
            dbl_ref[...] = jnp.sum(b8[...], axis=0, keepdims=True)

    row = pl.BlockSpec((tm, d), lambda i, k: (i, 0))
    vec = pl.BlockSpec((1, d), lambda i, k: (0, 0))
    w_in = pl.BlockSpec((None, FF_PAIR, d), lambda i, k: (k, 0, 0))
    hid = pl.BlockSpec((tm, FF_PAIR), lambda i, k: (i, k))
    return _call(
        body, name=name, grid=(nt, N_PAIR),
        in_specs=[row, row, row, w_in, w_in, w_in, vec, hid, hid],
        out_specs=[row, row, row, hid, hid, hid, vec, vec],
        out_shape=[jax.ShapeDtypeStruct((s, d), F32), jax.ShapeDtypeStruct((s, d), BF16),
                   jax.ShapeDtypeStruct((s, d), BF16),
                   jax.ShapeDtypeStruct((s, ffp), BF16), jax.ShapeDtypeStruct((s, ffp), BF16),
                   jax.ShapeDtypeStruct((s, ffp), BF16),
                   jax.ShapeDtypeStruct((1, d), F32), jax.ShapeDtypeStruct((1, d), F32)],
        args=(dy, u, x, wgt, wut, wd, g, a_fwd, b_fwd),
        scratch_shapes=[pltpu.VMEM((tm, d), BF16), pltpu.VMEM((tm, d), F32),
                        pltpu.VMEM((tm, d), F32), pltpu.VMEM((8, d), F32), pltpu.VMEM((8, d), F32)],
        sem=("arbitrary", "arbitrary"), exchange=exchange)


def _ffn_bwd_w(tok, hid, name, *, down, tm=4096, exchange=()):
    s, d = tok.shape
    tm = min(tm, s)
    nt = s // tm

    def body(t_ref, h_ref, dw_ref, acc):
        i = pl.program_id(1)
        part = _dot(h_ref[...], t_ref[...], TN) if down else _dot(t_ref[...], h_ref[...], TN)

        @pl.when(i == 0)
        def _():
            acc[...] = part

        @pl.when(i > 0)
        def _():
            acc[...] += part

        @pl.when(i == nt - 1)
        def _():
            for j in range(2):
                lo = j * FF_PAD
                dw_ref[j] = (acc[lo:lo + FF_SHARD, :] if down else acc[:, lo:lo + FF_SHARD]).astype(BF16)

    blk = (FF_SHARD, d) if down else (d, FF_SHARD)
    return _call(
        body, name=name, grid=(N_PAIR, nt),
        in_specs=[pl.BlockSpec((tm, d), lambda k, i: (i, 0)), pl.BlockSpec((tm, FF_PAIR), lambda k, i: (i, k))],
        out_specs=[pl.BlockSpec((2,) + blk, lambda k, i: (k, 0, 0))],
        out_shape=[jax.ShapeDtypeStruct((N_DEV,) + blk, BF16)], args=(tok, hid),
        scratch_shapes=[pltpu.VMEM((FF_PAIR, d) if down else (d, FF_PAIR), F32)],
        sem=("parallel", "arbitrary"), exchange=exchange)


def _bucket_tables():
    qi = np.arange(BLK)[:, None]
    ki = np.arange(2 * BLK)[None, :]
    off = qi + BLK - ki
    out = []
    for window, dil in DILATED:
        n_keys = window // dil
        dist = dil * np.clip(off, 0, n_keys)
        exact = REL_BUCKETS // 2
        df = np.maximum(dist, 1).astype(np.float32)
        large = exact + (np.log(df / np.float32(exact)) / np.float32(math.log(REL_MAX_DIST / exact))
                         * np.float32(REL_BUCKETS - exact)).astype(np.int32)
        large = np.minimum(large, REL_BUCKETS - 1)
        bucket = np.where(dist < exact, dist, large).astype(np.int32)
        band = (off >= 0) & (off <= n_keys)
        out.append(np.where(band, bucket, -1))
    return np.stack(out).astype(np.int32)


def _bias_fwd(rel_bias, buckets, name="bias_fwd"):
    def body(tbl_ref, bkt_ref, out_ref):
        bkt = bkt_ref[...]
        for h in range(ATT_HEADS):
            acc = jnp.full((BLK, 2 * BLK), NEG, F32)
            for bb in range(REL_BUCKETS):
                acc = jnp.where(bkt == bb, tbl_ref[bb, h], acc)
            out_ref[h] = acc

    nbr = len(DILATED)
    return pl.pallas_call(
        body, name=name, grid=(nbr,),
        in_specs=[pl.BlockSpec(memory_space=pltpu.SMEM),
                  pl.BlockSpec((None, BLK, 2 * BLK), lambda r: (r, 0, 0))],
        out_specs=pl.BlockSpec((None, ATT_HEADS, BLK, 2 * BLK), lambda r: (r, 0, 0, 0)),
        out_shape=jax.ShapeDtypeStruct((nbr, ATT_HEADS, BLK, 2 * BLK), F32),
        compiler_params=_params("parallel"),
    )(rel_bias, buckets)


def _bias_bwd(dbias, buckets, name="bias_bwd"):
    nbr = len(DILATED)

    def body(db_ref, bkt_ref, out_ref):
        r = pl.program_id(0)

        @pl.when(r == 0)
        def _():
            out_ref[...] = jnp.zeros_like(out_ref)

        bkt = bkt_ref[...]
        rowi = lax.broadcasted_iota(jnp.int32, (REL_BUCKETS, LANES), 0)
        coli = lax.broadcasted_iota(jnp.int32, (REL_BUCKETS, LANES), 1)
        acc = jnp.zeros((REL_BUCKETS, LANES), F32)
        for h in range(ATT_HEADS):
            x = db_ref[h]
            for bb in range(REL_BUCKETS):
                part = jnp.sum(jnp.where(bkt == bb, x, 0.0), axis=0, keepdims=True)
                tot = jnp.sum(part, axis=1, keepdims=True)
                acc = acc + jnp.where((rowi == bb) & (coli == h), tot, 0.0)
        out_ref[...] += acc

    return pl.pallas_call(
        body, name=name, grid=(nbr,),
        in_specs=[pl.BlockSpec((None, ATT_HEADS, BLK, 2 * BLK), lambda r: (r, 0, 0, 0)),
                  pl.BlockSpec((None, BLK, 2 * BLK), lambda r: (r, 0, 0))],
        out_specs=pl.BlockSpec((REL_BUCKETS, LANES), lambda r: (0, 0)),
        out_shape=jax.ShapeDtypeStruct((REL_BUCKETS, LANES), F32),
        compiler_params=_params("arbitrary"),
    )(dbias, buckets)


def _stack_heads(pair, lo):
    return jnp.concatenate([jnp.where(lo, pair, 0.0), jnp.where(lo, 0.0, pair)], axis=0)


def _head_cols(pair, lo, reduce):
    fill = -jnp.inf if reduce is jnp.max else 0.0
    return jnp.concatenate([reduce(jnp.where(lo, pair, fill), axis=1, keepdims=True),
                            reduce(jnp.where(lo, fill, pair), axis=1, keepdims=True)], axis=0)


def _unstack_heads(x2, lo):
    return jnp.where(lo, x2[:BLK], x2[BLK:])


ATT_SCALE = 64 ** -0.5


def _stack_scaled(pair, lo):
    return _stack_heads(pair * ATT_SCALE, lo).astype(BF16)


def _first_block_bias(bias2, t):
    kidx = lax.broadcasted_iota(jnp.int32, (2 * BLK, 2 * BLK), 1)
    return jnp.where((t > 0) | (kidx >= BLK), bias2, NEG)


DIL_TILE = 2048
DIL_COLS = ATT_W // LANES
DIL_GROUP = 4


def _dil_rows(dil, n, r, base=0):
    start = base + n * (BLK * dil) + r
    return pl.ds(start, BLK, stride=dil) if dil > 1 else pl.ds(start, BLK)


def _dil_in_specs(tile_of):
    cur = lambda col: pl.BlockSpec((DIL_TILE, LANES), lambda p, i: (tile_of(i), col * DIL_COLS + p))
    prev = lambda col: pl.BlockSpec((DIL_TILE, LANES), lambda p, i: (jnp.maximum(tile_of(i) - 1, 0), col * DIL_COLS + p))
    bias = pl.BlockSpec((len(DILATED), None, 2 * BLK, 2 * BLK), lambda p, i: (0, p, 0, 0))
    return [cur(0), prev(1), cur(1), prev(2), cur(2), bias]


def _pair_bias(biasm):
    return biasm.reshape(len(DILATED), DIL_COLS, 2 * BLK, 2 * BLK)


def _dil_fwd(proj, biasm, name="dil_fwd", gather=()):
    s = proj.shape[0]
    nt = s // DIL_TILE
    tt = DIL_TILE

    def body(q_ref, kp_ref, kc_ref, vp_ref, vc_ref, bias_ref, att_ref, lse_ref, k2, v2, ob, lb):
        t = pl.program_id(1)
        k2[0:tt, :] = kp_ref[...]
        k2[tt:2 * tt, :] = kc_ref[...]
        v2[0:tt, :] = vp_ref[...]
        v2[tt:2 * tt, :] = vc_ref[...]
        lo = lax.broadcasted_iota(jnp.int32, (BLK, LANES), 1) < 64
        for b, (_, dil) in enumerate(DILATED):
            bias = [_first_block_bias(bias_ref[b], t), bias_ref[b]]
            for j0 in range(0, tt // BLK, DIL_GROUP):
                grp = range(DIL_GROUP)
                rn = [((j0 + i) % dil, (j0 + i) // dil) for i in grp]
                here = [_dil_rows(dil, n, r) for r, n in rn]
                cur = [_dil_rows(dil, n, r, tt) for r, n in rn]
                prev = [_dil_rows(dil, n - 1, r, tt) for r, n in rn]
                q2 = [_stack_scaled(q_ref[here[i], :], lo) for i in grp]
                kk = [jnp.concatenate([k2[prev[i], :], k2[cur[i], :]], axis=0).astype(BF16) for i in grp]
                vv = [jnp.concatenate([v2[prev[i], :], v2[cur[i], :]], axis=0).astype(BF16) for i in grp]
                sc = [_dot(q2[i], kk[i], NT) + bias[min(rn[i][1], 1)] for i in grp]
                mx = [jnp.max(sc[i], axis=1, keepdims=True) for i in grp]
                pe = [jnp.exp(sc[i] - mx[i]) for i in grp]
                l = [jnp.sum(pe[i], axis=1, keepdims=True) for i in grp]
                o2 = [_dot(pe[i].astype(BF16), vv[i], NN) for i in grp]
                for i in grp:
                    ob.at[b][here[i], :] = _unstack_heads(o2[i] / l[i], lo)
                    lb.at[b][here[i], :] = _unstack_heads(jnp.broadcast_to(mx[i] + jnp.log(l[i]), (2 * BLK, LANES)), lo)
        l0, l1, l2 = lb[0], lb[1], lb[2]
        mx = jnp.maximum(jnp.maximum(l0, l1), l2)
        e0, e1, e2 = jnp.exp(l0 - mx), jnp.exp(l1 - mx), jnp.exp(l2 - mx)
        tot = e0 + e1 + e2
        att_ref[...] = (e0 * ob[0] + e1 * ob[1] + e2 * ob[2]) / tot
        lse_ref[...] = mx + jnp.log(tot)

    out = pl.BlockSpec((tt, LANES), lambda p, i: (i, p))
    return _call(
        body, name=name, grid=(DIL_COLS, nt), in_specs=_dil_in_specs(lambda i: i), out_specs=[out, out],
        out_shape=[jax.ShapeDtypeStruct((s, ATT_W), F32)] * 2, args=(proj, proj, proj, proj, proj, _pair_bias(biasm)),
        scratch_shapes=[pltpu.VMEM((2 * tt, LANES), F32), pltpu.VMEM((2 * tt, LANES), F32),
                        pltpu.VMEM((len(DILATED), tt, LANES), F32), pltpu.VMEM((len(DILATED), tt, LANES), F32)],
        sem=("parallel", "parallel"), gather=gather)


def _dil_bwd(proj, biasm, lse, att, dcat, name="dil_bwd"):
    s = proj.shape[0]
    nt = s // DIL_TILE
    tt = DIL_TILE
    nbr = len(DILATED)

    def body(q_ref, kp_ref, kc_ref, vp_ref, vc_ref, bias_ref, lse_ref, att_ref, datt_ref,
             dq_ref, dk_ref, dv_ref, dbias_ref, k2, v2, dqa, dka, dva, kcar, vcar):
        i = pl.program_id(1)
        t = nt - 1 - i
        k2[0:tt, :] = kp_ref[...]
        k2[tt:2 * tt, :] = kc_ref[...]
        v2[0:tt, :] = vp_ref[...]
        v2[tt:2 * tt, :] = vc_ref[...]

        @pl.when(i == 0)
        def _():
            kcar[...] = jnp.zeros_like(kcar)
            vcar[...] = jnp.zeros_like(vcar)
            dbias_ref[...] = jnp.zeros_like(dbias_ref)

        dqa[...] = jnp.zeros_like(dqa)
        dka[0:tt, :] = jnp.zeros((tt, LANES), F32)
        dva[0:tt, :] = jnp.zeros((tt, LANES), F32)
        dka[tt:2 * tt, :] = kcar[...]
        dva[tt:2 * tt, :] = vcar[...]
        lo = lax.broadcasted_iota(jnp.int32, (BLK, LANES), 1) < 64
        for b, (_, dil) in enumerate(DILATED):
            bias = [_first_block_bias(bias_ref[b], t), bias_ref[b]]
            for j0 in range(0, tt // BLK, DIL_GROUP):
                grp = range(DIL_GROUP)
                rn = [((j0 + i) % dil, (j0 + i) // dil) for i in grp]
                here = [_dil_rows(dil, n, r) for r, n in rn]
                cur = [_dil_rows(dil, n, r, tt) for r, n in rn]
                prev = [_dil_rows(dil, n - 1, r, tt) for r, n in rn]
                dat = [datt_ref[here[i], :] for i in grp]
                q2 = [_stack_scaled(q_ref[here[i], :], lo) for i in grp]
                dom = [_stack_heads(dat[i], lo).astype(BF16) for i in grp]
                kk = [jnp.concatenate([k2[prev[i], :], k2[cur[i], :]], axis=0).astype(BF16) for i in grp]
                vv = [jnp.concatenate([v2[prev[i], :], v2[cur[i], :]], axis=0).astype(BF16) for i in grp]
                sc = [_dot(q2[i], kk[i], NT) + bias[min(rn[i][1], 1)] for i in grp]
                dp = [_dot(dom[i], vv[i], NT) for i in grp]
                pr = [jnp.exp(sc[i] - _head_cols(lse_ref[here[i], :], lo, jnp.max)) for i in grp]
                ds = [pr[i] * (dp[i] - _head_cols(dat[i] * att_ref[here[i], :], lo, jnp.sum)) for i in grp]
                dsb = [ds[i].astype(BF16) for i in grp]
                dq2 = [_dot(dsb[i], kk[i], NN) * ATT_SCALE for i in grp]
                dk2 = [_dot(dsb[i], q2[i], TN) for i in grp]
                dv2 = [_dot(pr[i].astype(BF16), dom[i], TN) for i in grp]
                for i in grp:
                    dbias_ref[b] += ds[i]
                    dqa[here[i], :] += _unstack_heads(dq2[i], lo)
                    dka[prev[i], :] += dk2[i][:BLK]
                    dka[cur[i], :] += dk2[i][BLK:]
                    dva[prev[i], :] += dv2[i][:BLK]
                    dva[cur[i], :] += dv2[i][BLK:]
        dq_ref[...] = dqa[...].astype(BF16)
        dk_ref[...] = dka[tt:2 * tt, :].astype(BF16)
        dv_ref[...] = dva[tt:2 * tt, :].astype(BF16)
        kcar[...] = dka[0:tt, :]
        vcar[...] = dva[0:tt, :]

    rev = lambda i: nt - 1 - i
    out = pl.BlockSpec((tt, LANES), lambda p, i: (rev(i), p))
    two = lambda: pltpu.VMEM((2 * tt, LANES), F32)
    one = lambda: pltpu.VMEM((tt, LANES), F32)
    return pl.pallas_call(
        body, name=name, grid=(DIL_COLS, nt),
        in_specs=_dil_in_specs(rev) + [out, out, out],
        out_specs=[out, out, out, pl.BlockSpec((nbr, None, 2 * BLK, 2 * BLK), lambda p, i: (0, p, 0, 0))],
        out_shape=[jax.ShapeDtypeStruct((s, ATT_W), BF16)] * 3
        + [jax.ShapeDtypeStruct((nbr, DIL_COLS, 2 * BLK, 2 * BLK), F32)],
        scratch_shapes=[two(), two(), one(), two(), two(), one(), one()],
        compiler_params=_params("arbitrary", "arbitrary"),
    )(proj, proj, proj, proj, proj, _pair_bias(biasm), lse, att, dcat)


QK_COL0 = (3 * ATT_W) // ATT_W


HALO = 8


def _conv_shifted(prev8, cur, j):
    sh = CONV_K - 1 - j
    if sh == 0:
        return cur
    rolled = pltpu.roll(cur, sh, 0)
    row8 = lax.broadcasted_iota(jnp.int32, prev8.shape, 0)
    top = jnp.where(row8 < sh, pltpu.roll(prev8, sh, 0), rolled[:HALO])
    return top if cur.shape[0] == HALO else jnp.concatenate([top, rolled[HALO:]], axis=0)


def _conv_z(prev8, cur, w_ref, b_ref, taps=None):
    z = b_ref[...]
    for j in range(CONV_K):
        tap = _conv_shifted(prev8, cur, j)
        if taps is not None:
            taps.append(tap)
        z = z + tap * w_ref[j:j + 1, :]
    return z


def _silu_grad(z):
    sig = _sigmoid(z)
    return sig * (1.0 + z * (1.0 - sig))


def _conv_fwd(proj, conv_w, conv_b, name="conv_fwd", tm=512):
    s = proj.shape[0]
    w = ATT_W
    per = tm // HALO

    def body(prev_ref, cur_ref, w_ref, b_ref, o_ref):
        i = pl.program_id(1)
        prev8 = jnp.where(i > 0, prev_ref[...], 0.0)
        z = _conv_z(prev8, cur_ref[...], w_ref, b_ref)
        o_ref[...] = z * _sigmoid(z)

    return pl.pallas_call(
        body, name=name, grid=(2, s // tm),
        in_specs=[pl.BlockSpec((HALO, w), lambda j, i: (jnp.maximum(i * per - 1, 0), QK_COL0 + j)),
                  pl.BlockSpec((tm, w), lambda j, i: (i, QK_COL0 + j)),
                  pl.BlockSpec((CONV_K, w), lambda j, i: (0, j)),
                  pl.BlockSpec((1, w), lambda j, i: (0, j))],
        out_specs=pl.BlockSpec((tm, w), lambda j, i: (i, j)),
        out_shape=jax.ShapeDtypeStruct((s, 2 * ML_W), F32),
        compiler_params=_params("parallel", "parallel"),
    )(proj, proj, conv_w, conv_b)


def _conv_bwd(proj, dqk, conv_w, conv_b, name="conv_bwd", tm=512):
    s = proj.shape[0]
    w = ATT_W
    nt = s // tm
    per = tm // HALO

    def body(xp_ref, xc_ref, xn_ref, dc_ref, dn_ref, w_ref, b_ref, dx_ref, dw_ref, db_ref):
        i = pl.program_id(1)
        prev8 = jnp.where(i > 0, xp_ref[...], 0.0)
        cur = xc_ref[...]
        taps = []
        dzc = dc_ref[...] * _silu_grad(_conv_z(prev8, cur, w_ref, b_ref, taps))
        dzn8 = dn_ref[...] * _silu_grad(_conv_z(cur[tm - HALO:], xn_ref[...], w_ref, b_ref))
        dzn8 = jnp.where(i < nt - 1, dzn8, 0.0)
        row8 = lax.broadcasted_iota(jnp.int32, (HALO, w), 0)
        dx = dzc * w_ref[CONV_K - 1:CONV_K, :]
        for j in range(CONV_K - 1):
            sh = CONV_K - 1 - j
            rolled = pltpu.roll(dzc, tm - sh, 0)
            bottom = jnp.where(row8 >= HALO - sh, pltpu.roll(dzn8, HALO - sh, 0), rolled[tm - HALO:])
            dx = dx + jnp.concatenate([rolled[:tm - HALO], bottom], axis=0) * w_ref[j:j + 1, :]
        dx_ref[...] = dx

        @pl.when(i == 0)
        def _():
            dw_ref[...] = jnp.zeros_like(dw_ref)
            db_ref[...] = jnp.zeros_like(db_ref)

        for j in range(CONV_K):
            dw_ref[j:j + 1, :] += jnp.sum(dzc * taps[j], axis=0, keepdims=True)
        db_ref[...] += jnp.sum(dzc, axis=0, keepdims=True)

    last = s // HALO - 1
    halo_before = lambda col0: pl.BlockSpec((HALO, w), lambda j, i: (jnp.maximum(i * per - 1, 0), col0 + j))
    halo_after = lambda col0: pl.BlockSpec((HALO, w), lambda j, i: (jnp.minimum((i + 1) * per, last), col0 + j))
    tile = lambda col0: pl.BlockSpec((tm, w), lambda j, i: (i, col0 + j))
    return pl.pallas_call(
        body, name=name, grid=(2, nt),
        in_specs=[halo_before(QK_COL0), tile(QK_COL0), halo_after(QK_COL0), tile(0), halo_after(0),
                  pl.BlockSpec((CONV_K, w), lambda j, i: (0, j)), pl.BlockSpec((1, w), lambda j, i: (0, j))],
        out_specs=[tile(0), pl.BlockSpec((CONV_K, w), lambda j, i: (0, j)),
                   pl.BlockSpec((1, w), lambda j, i: (0, j))],
        out_shape=[jax.ShapeDtypeStruct((s, 2 * ML_W), F32), jax.ShapeDtypeStruct((CONV_K, 2 * ML_W), F32),
                   jax.ShapeDtypeStruct((1, 2 * ML_W), F32)],
        compiler_params=_params("parallel", "arbitrary"),
    )(proj, proj, proj, dqk, dqk, conv_w, conv_b)


def _bf16_mm(dims_fwd):
    @jax.custom_vjp
    def mm(a, b):
        return _dot(a.astype(BF16), b.astype(BF16), dims_fwd)

    def fwd(a, b):
        return mm(a, b), (a, b)

    def bwd(res, g):
        a, b = res
        if dims_fwd is NN:
            return _mm_nt(g, b), _mm_tn(a, g)
        if dims_fwd is NT:
            return _mm_nn(g, b), _mm_tn(g, a)
        return _mm_nt(b, g), _mm_nn(a, g)

    mm.defvjp(fwd, bwd)
    return mm


_mm_nn = _bf16_mm(NN)
_mm_nt = _bf16_mm(NT)
_mm_tn = _bf16_mm(TN)


def _tri(lower):
    r = lax.broadcasted_iota(jnp.int32, (CHUNK, CHUNK), 0)
    c = lax.broadcasted_iota(jnp.int32, (CHUNK, CHUNK), 1)
    return ((r >= c) if lower else (r <= c)).astype(F32)


@jax.custom_vjp
def _cumsum_rows(x):
    return lax.dot_general(_tri(True), x, NN, precision=lax.Precision.HIGHEST, preferred_element_type=F32)


def _cumsum_fwd(x):
    return _cumsum_rows(x), None


def _cumsum_bwd(_, g):
    return (lax.dot_general(_tri(False), g, NN, precision=lax.Precision.HIGHEST, preferred_element_type=F32),)


_cumsum_rows.defvjp(_cumsum_fwd, _cumsum_bwd)


def _abs(x):
    return jnp.where(x >= 0, x, -x)


def _log_sigmoid(x):
    return jnp.minimum(x, 0.0) - jnp.log(1.0 + jnp.exp(-_abs(x)))


def _pick_col(x, lane):
    sel = lax.broadcasted_iota(jnp.int32, x.shape, 1) == lane
    return jnp.sum(jnp.where(sel, x, 0.0), axis=1, keepdims=True)


def _pick_row(x, r):
    sel = lax.broadcasted_iota(jnp.int32, x.shape, 0) == r
    return jnp.sum(jnp.where(sel, x, 0.0), axis=0, keepdims=True)


def _mlstm_chunk(qs, ks, vs, oms, gates, gate_bias, mlg, cs, ns, ms):
    gb = gates + gate_bias
    cum = _cumsum_rows(_log_sigmoid(gb))
    gbt = gb.T
    cumt = cum.T
    causal = lax.broadcasted_iota(jnp.int32, (CHUNK, CHUNK), 0) >= lax.broadcasted_iota(jnp.int32, (CHUNK, CHUNK), 1)
    hd = range(ML_HEADS)
    k = [ks[h] * (ML_HD ** -0.5) for h in hd]
    ig_col = [_pick_col(gb, h) for h in hd]
    ig_row = [_pick_row(gbt, h) for h in hd]
    b_col = [_pick_col(cum, ML_HEADS + h) for h in hd]
    b_row = [_pick_row(cumt, ML_HEADS + h) for h in hd]
    g = [_pick_row(b_col[h], CHUNK - 1) for h in hd]
    a = [g[h] - b_col[h] + ig_col[h] for h in hd]
    m_loc = [jnp.max(a[h], axis=0, keepdims=True) for h in hd]
    wa = [jnp.exp(a[h] - m_loc[h]) for h in hd]
    d_log = [jnp.where(causal, b_col[h] - b_row[h] + ig_row[h], -jnp.inf) for h in hd]
    e_log = [b_col[h] + ms[h] for h in hd]
    m_t = [jnp.maximum(e_log[h], jnp.max(d_log[h], axis=1, keepdims=True)) for h in hd]
    d_w = [jnp.exp(d_log[h] - m_t[h]) for h in hd]
    e_w = [jnp.exp(e_log[h] - m_t[h]) for h in hd]
    qk = [_mm_nt(qs[h], k[h]) for h in hd]
    qc = [_mm_nt(qs[h], cs[h]) for h in hd]
    c_loc = [_mm_tn(wa[h] * vs[h], k[h]) for h in hd]
    s_qk = [qk[h] * d_w[h] for h in hd]
    sv = [_mm_nn(s_qk[h], vs[h]) for h in hd]
    n_loc = [jnp.sum(wa[h] * k[h], axis=0, keepdims=True) for h in hd]
    m_out = [jnp.maximum(g[h] + ms[h], m_loc[h]) for h in hd]
    sp = [jnp.exp(g[h] + ms[h] - m_out[h]) for h in hd]
    sl = [jnp.exp(m_loc[h] - m_out[h]) for h in hd]
    c_out = [sp[h] * cs[h] + sl[h] * c_loc[h] for h in hd]
    n_out = [sp[h] * ns[h] + sl[h] * n_loc[h] for h in hd]
    num = [e_w[h] * qc[h] + sv[h] for h in hd]
    den = [e_w[h] * jnp.sum(qs[h] * ns[h], axis=1, keepdims=True) + jnp.sum(s_qk[h], axis=1, keepdims=True) for h in hd]
    hg = [_sigmoid(oms[h]) * (num[h] / jnp.maximum(_abs(den[h]), jnp.exp(-m_t[h]))) for h in hd]
    mu = [jnp.mean(hg[h], axis=1, keepdims=True) for h in hd]
    hc = [hg[h] - mu[h] for h in hd]
    var = [jnp.mean(hc[h] * hc[h], axis=1, keepdims=True) for h in hd]
    ys = [hc[h] * lax.rsqrt(var[h] + LN_EPS) * mlg[h] for h in hd]
    return ys, c_out, n_out, m_out


V_COL = 5
O_COL = 6
ML_SUB = 1


def _mlstm_fwd(qk, proj, gates, gate_bias, mlg, name="mlstm_fwd", gather=()):
    s = qk.shape[0]
    nc = s // CHUNK

    def body(q_ref, k_ref, v_ref, o_ref, g_ref, gb_ref, mlg_ref, y_ref, cp_ref, np_ref, mp_ref, c_s, n_s, m_s):
        ci = pl.program_id(0)

        @pl.when(ci == 0)
        def _():
            c_s[...] = jnp.zeros_like(c_s)
            n_s[...] = jnp.zeros_like(n_s)
            m_s[...] = jnp.zeros_like(m_s)

        for sub in range(ML_SUB):
            rows = slice(CHUNK * sub, CHUNK * (sub + 1))
            hs = lambda ref: [ref[rows, LANES * h:LANES * (h + 1)] for h in range(ML_HEADS)]
            cp_ref[sub] = c_s[...]
            np_ref[sub] = n_s[...]
            mp_ref[sub] = m_s[...]
            ys, c_new, n_new, m_new = _mlstm_chunk(
                hs(q_ref), hs(k_ref), hs(v_ref), hs(o_ref), g_ref[rows, :], gb_ref[...],
                [mlg_ref[:, LANES * h:LANES * (h + 1)] for h in range(ML_HEADS)],
                [c_s[h] for h in range(ML_HEADS)], [n_s[h:h + 1, :] for h in range(ML_HEADS)],
                [m_s[h:h + 1, 0:1] for h in range(ML_HEADS)])
            for h in range(ML_HEADS):
                y_ref[rows, LANES * h:LANES * (h + 1)] = ys[h]
                c_s[h] = c_new[h]
                n_s[h:h + 1, :] = n_new[h]
                m_s[h:h + 1, :] = jnp.broadcast_to(m_new[h], (1, LANES))

    blk = lambda col: pl.BlockSpec((ML_SUB * CHUNK, ML_W), lambda ci: (ci, col))
    vec = lambda w: pl.BlockSpec((1, w), lambda ci: (0, 0))
    return _call(
        body, name=name, grid=(nc // ML_SUB,), args=(qk, qk, proj, proj, gates, gate_bias, mlg), sem=("arbitrary",),
        gather=gather,
        in_specs=[blk(0), blk(1), blk(V_COL), blk(O_COL), pl.BlockSpec((ML_SUB * CHUNK, LANES), lambda ci: (ci, 0)),
                  vec(LANES), vec(ML_W)],
        out_specs=[blk(0), pl.BlockSpec((ML_SUB, ML_HEADS, ML_HD, ML_HD), lambda ci: (ci, 0, 0, 0)),
                   pl.BlockSpec((ML_SUB, 8, LANES), lambda ci: (ci, 0, 0)),
                   pl.BlockSpec((ML_SUB, 8, LANES), lambda ci: (ci, 0, 0))],
        out_shape=[jax.ShapeDtypeStruct((s, ML_W), F32), jax.ShapeDtypeStruct((nc, ML_HEADS, ML_HD, ML_HD), F32),
                   jax.ShapeDtypeStruct((nc, 8, LANES), F32), jax.ShapeDtypeStruct((nc, 8, LANES), F32)],
        scratch_shapes=[pltpu.VMEM((ML_HEADS, ML_HD, ML_HD), F32), pltpu.VMEM((8, LANES), F32),
                        pltpu.VMEM((8, LANES), F32)])


def _mlstm_bwd(qk, proj, gates, gate_bias, mlg, cprev, nprev, mprev, dy, name="mlstm_bwd", exchange=()):
    s = qk.shape[0]
    nc = s // CHUNK

    def body(q_ref, k_ref, v_ref, o_ref, g_ref, gb_ref, mlg_ref, cp_ref, np_ref, mp_ref, dy_ref,
             dqk_ref, dv_ref, do_ref, dg_ref, dgb_ref, dmlg_ref, dc_s, dn_s, dm_s, gb8, mg8):
        ci = pl.program_id(0)

        @pl.when(ci == 0)
        def _():
            dc_s[...] = jnp.zeros_like(dc_s)
            dn_s[...] = jnp.zeros_like(dn_s)
            dm_s[...] = jnp.zeros_like(dm_s)
            gb8[...] = jnp.zeros_like(gb8)
            mg8[...] = jnp.zeros_like(mg8)

        for sub in reversed(range(ML_SUB)):
            rows = slice(CHUNK * sub, CHUNK * (sub + 1))
            hs = lambda ref: [ref[rows, LANES * h:LANES * (h + 1)] for h in range(ML_HEADS)]
            prim = (hs(q_ref), hs(k_ref), hs(v_ref), hs(o_ref), g_ref[rows, :], gb_ref[...],
                    [mlg_ref[:, LANES * h:LANES * (h + 1)] for h in range(ML_HEADS)],
                    [cp_ref[sub, h] for h in range(ML_HEADS)], [np_ref[sub, h:h + 1, :] for h in range(ML_HEADS)],
                    [mp_ref[sub, h:h + 1, 0:1] for h in range(ML_HEADS)])
            _, vjp = jax.vjp(_mlstm_chunk, *prim)
            cot = (hs(dy_ref), [dc_s[h] for h in range(ML_HEADS)], [dn_s[h:h + 1, :] for h in range(ML_HEADS)],
                   [dm_s[h:h + 1, 0:1] for h in range(ML_HEADS)])
            dqs, dks, dvs, dos, dg, dgb, dmlg, dcs, dns, dms = vjp(cot)
            dg_ref[rows, :] = dg
            gb8[0:1, :] += dgb
            for h in range(ML_HEADS):
                sl = slice(LANES * h, LANES * (h + 1))
                dqk_ref[rows, sl] = dqs[h]
                dqk_ref[rows, ML_W + LANES * h:ML_W + LANES * (h + 1)] = dks[h]
                dv_ref[rows, sl] = dvs[h]
                do_ref[rows, sl] = dos[h]
                mg8[0:1, sl] += dmlg[h]
                dc_s[h] = dcs[h]
                dn_s[h:h + 1, :] = dns[h]
                dm_s[h:h + 1, :] = jnp.broadcast_to(dms[h], (1, LANES))

        @pl.when(ci == nb - 1)
        def _():
            dgb_ref[...] = gb8[0:1, :]
            dmlg_ref[...] = mg8[0:1, :]

    nb = nc // ML_SUB
    rev = lambda ci: nb - 1 - ci
    blk = lambda col: pl.BlockSpec((ML_SUB * CHUNK, ML_W), lambda ci: (rev(ci), col))
    vec = lambda w: pl.BlockSpec((1, w), lambda ci: (0, 0))
    st8 = pl.BlockSpec((ML_SUB, 8, LANES), lambda ci: (rev(ci), 0, 0))
    gsp = pl.BlockSpec((ML_SUB * CHUNK, LANES), lambda ci: (rev(ci), 0))
    return _call(
        body, name=name, grid=(nb,), sem=("arbitrary",), exchange=exchange,
        args=(qk, qk, proj, proj, gates, gate_bias, mlg, cprev, nprev, mprev, dy),
        in_specs=[blk(0), blk(1), blk(V_COL), blk(O_COL), gsp, vec(LANES), vec(ML_W),
                  pl.BlockSpec((ML_SUB, ML_HEADS, ML_HD, ML_HD), lambda ci: (rev(ci), 0, 0, 0)), st8, st8, blk(1)],
        out_specs=[pl.BlockSpec((ML_SUB * CHUNK, 2 * ML_W), lambda ci: (rev(ci), 0)), blk(0), blk(0), gsp, vec(LANES),
                   vec(ML_W)],
        out_shape=[jax.ShapeDtypeStruct((s, 2 * ML_W), F32),
                   jax.ShapeDtypeStruct((s, ML_W), F32), jax.ShapeDtypeStruct((s, ML_W), F32),
                   jax.ShapeDtypeStruct((s, LANES), F32), jax.ShapeDtypeStruct((1, LANES), F32),
                   jax.ShapeDtypeStruct((1, ML_W), F32)],
        scratch_shapes=[pltpu.VMEM((ML_HEADS, ML_HD, ML_HD), F32), pltpu.VMEM((8, LANES), F32),
                        pltpu.VMEM((8, LANES), F32), pltpu.VMEM((8, LANES), F32), pltpu.VMEM((8, ML_W), F32)])


def _xattn_tile(qs, ks, vs):
    hd = range(XA_HEADS)
    sc = [_mm_nt(qs[h], ks[h]) * (XA_HD ** -0.5) for h in hd]
    mx = [lax.stop_gradient(jnp.max(sc[h], axis=1, keepdims=True)) for h in hd]
    pe = [jnp.exp(sc[h] - mx[h]) for h in hd]
    pn = [pe[h] / jnp.sum(pe[h], axis=1, keepdims=True) for h in hd]
    return [_mm_nn(pn[h], vs[h]) for h in hd]


def _xa_heads(ref):
    return [ref[:, XA_HD * h:XA_HD * (h + 1)] for h in range(XA_HEADS)]


def _xattn_fwd(q, kv, name="xattn_fwd", tm=512):
    s, d = q.shape

    def body(q_ref, k_ref, v_ref, o_ref):
        outs = _xattn_tile(_xa_heads(q_ref), _xa_heads(k_ref), _xa_heads(v_ref))
        for h in range(XA_HEADS):
            o_ref[:, XA_HD * h:XA_HD * (h + 1)] = outs[h]

    row = pl.BlockSpec((tm, d), lambda i: (i, 0))
    return pl.pallas_call(
        body, name=name, grid=(s // tm,),
        in_specs=[row, pl.BlockSpec((MEM_LEN, d), lambda i: (0, 0)), pl.BlockSpec((MEM_LEN, d), lambda i: (0, 1))],
        out_specs=row, out_shape=jax.ShapeDtypeStruct((s, d), F32),
        compiler_params=_params("parallel"),
    )(q, kv, kv)


def _xattn_bwd(q, kv, do, name="xattn_bwd", tm=512):
    s, d = q.shape

    def body(q_ref, k_ref, v_ref, do_ref, dq_ref, dkv_ref):
        i = pl.program_id(0)
        _, vjp = jax.vjp(_xattn_tile, _xa_heads(q_ref), _xa_heads(k_ref), _xa_heads(v_ref))
        dqs, dks, dvs = vjp(_xa_heads(do_ref))

        @pl.when(i == 0)
        def _():
            dkv_ref[...] = jnp.zeros_like(dkv_ref)

        for h in range(XA_HEADS):
            sl = slice(XA_HD * h, XA_HD * (h + 1))
            dq_ref[:, sl] = dqs[h]
            dkv_ref[:, sl] += dks[h]
            dkv_ref[:, d + XA_HD * h:d + XA_HD * (h + 1)] += dvs[h]

    row = pl.BlockSpec((tm, d), lambda i: (i, 0))
    return pl.pallas_call(
        body, name=name, grid=(s // tm,),
        in_specs=[row, pl.BlockSpec((MEM_LEN, d), lambda i: (0, 0)), pl.BlockSpec((MEM_LEN, d), lambda i: (0, 1)), row],
        out_specs=[row, pl.BlockSpec((MEM_LEN, 2 * d), lambda i: (0, 0))],
        out_shape=[jax.ShapeDtypeStruct((s, d), F32), jax.ShapeDtypeStruct((MEM_LEN, 2 * d), F32)],
        compiler_params=_params("arbitrary"),
    )(q, kv, kv, do)


def _loss_head(y, target, name="loss_head", tm=1024):
    s, d = y.shape
    nt = s // tm

    def body(y_ref, t_ref, dy_ref, loss_ref, acc):
        i = pl.program_id(0)
        err = y_ref[...] - t_ref[...]
        dy_ref[...] = err * (1.0 / d)

        @pl.when(i == 0)
        def _():
            acc[...] = jnp.zeros_like(acc)

        acc[...] += _rowsum8(err * err)

        @pl.when(i == nt - 1)
        def _():
            tot = jnp.sum(jnp.sum(acc[...], axis=0, keepdims=True), axis=1, keepdims=True)
            loss_ref[...] = jnp.broadcast_to(tot * (0.5 / d), (1, LANES))

    row = pl.BlockSpec((tm, d), lambda i: (i, 0))
    return pl.pallas_call(
        body, name=name, grid=(nt,),
        in_specs=[row, row], out_specs=[row, pl.BlockSpec((1, LANES), lambda i: (0, 0))],
        out_shape=[jax.ShapeDtypeStruct((s, d), F32), jax.ShapeDtypeStruct((1, LANES), F32)],
        scratch_shapes=[pltpu.VMEM((8, d), F32)],
        compiler_params=_params("arbitrary"),
    )(y, target)


def _adam2d(recv, w, m, v, name, layer=None):
    rows, cols = w.shape[-2:]
    fits = [t for t in range(16, rows + 1, 16) if rows % t == 0 and t * cols <= 128 * 1024]
    tr = max(fits) if fits else rows

    def body(r_ref, w_ref, m_ref, v_ref, g_ref, d_ref, mo_ref, vo_ref):
        g = r_ref[0].astype(F32)
        for j in range(1, N_DEV):
            g = g + r_ref[j].astype(F32)
        mn = ADAM_B1 * m_ref[...] + (1.0 - ADAM_B1) * g
        vn = ADAM_B2 * v_ref[...] + (1.0 - ADAM_B2) * jnp.square(g)
        m_hat = mn / (1.0 - ADAM_B1 ** ADAM_STEP)
        v_hat = vn / (1.0 - ADAM_B2 ** ADAM_STEP)
        g_ref[...] = g
        d_ref[...] = -ADAM_LR * (m_hat / (jnp.sqrt(v_hat) + ADAM_EPS) + ADAM_WD * w_ref[...])
        mo_ref[...] = mn
        vo_ref[...] = vn

    row = pl.BlockSpec((tr, cols), lambda i: (i, 0))
    if layer is None:
        wspec = row
    else:
        wspec = pl.BlockSpec((None, None, tr, cols), lambda i: (0, layer, i, 0))
    return pl.pallas_call(
        body, name=name, grid=(rows // tr,),
        in_specs=[pl.BlockSpec((N_DEV, tr, cols), lambda i: (0, i, 0)), wspec, wspec, wspec],
        out_specs=[row] * 4, out_shape=[jax.ShapeDtypeStruct((rows, cols), F32)] * 4,
        compiler_params=_params("parallel"),
    )(recv, w, m, v)


WEIGHTS = ("rel_bias", "ln_g", "ln_b", "ffn_w_gate", "ffn_w_up", "ffn_w_down", "w_in", "conv_w", "conv_b",
           "ig_bias", "fg_bias", "ml_norm_g", "w_out", "xq_w", "xkv_w", "xo_w")
SMALL = ("rel_bias", "ln_g", "ln_b", "conv_w", "conv_b", "ig_bias", "fg_bias", "ml_norm_g")
SMALL_SHAPES = {
    "rel_bias": (REL_BUCKETS, ATT_HEADS), "ln_g": (1, 4, LANES), "ln_b": (1, 4, LANES), "conv_w": (1, CONV_K, LANES),
    "conv_b": (1, 2 * ML_W), "ig_bias": (1, ML_HEADS), "fg_bias": (1, ML_HEADS), "ml_norm_g": (1, ML_W),
}
SMALL_ROWS = 8


def _pack_small(parts, lead=()):
    out = []
    for p in parts:
        p = jnp.pad(p, [(0, 0)] * len(lead) + [(0, SMALL_ROWS * LANES - p.shape[-1])])
        out.append(p.reshape(lead + (SMALL_ROWS, LANES)))
    return jnp.concatenate(out, axis=len(lead))


def _unpack_small(flat):
    out = {}
    for i, n in enumerate(SMALL):
        cnt = int(np.prod(SMALL_SHAPES[n]))
        out[n] = flat[SMALL_ROWS * i:SMALL_ROWS * (i + 1)].reshape(-1)[:cnt].reshape(SMALL_SHAPES[n])
    return out


def _split8(full, axis):
    shp = full.shape
    t = full.reshape(shp[:axis] + (N_DEV, shp[axis] // N_DEV) + shp[axis + 1:])
    return jnp.moveaxis(t, axis, 0).reshape(N_DEV, -1)


def _rep8(full):
    return jnp.broadcast_to(full.reshape(1, -1), (N_DEV, full.size))


def kernel(x, mem, rel_bias, ln_g, ln_b, ffn_w_gate, ffn_w_up, ffn_w_down, w_in, conv_w, conv_b, ig_bias, fg_bias, ml_norm_g, w_out, xq_w, xkv_w, xo_w, loss_target, m_rel_bias, m_ln_g, m_ln_b, m_ffn_w_gate, m_ffn_w_up, m_ffn_w_down, m_w_in, m_conv_w, m_conv_b, m_ig_bias, m_fg_bias, m_ml_norm_g, m_w_out, m_xq_w, m_xkv_w, m_xo_w, v_rel_bias, v_ln_g, v_ln_b, v_ffn_w_gate, v_ffn_w_up, v_ffn_w_down, v_w_in, v_conv_w, v_conv_b, v_ig_bias, v_fg_bias, v_ml_norm_g, v_w_out, v_xq_w, v_xkv_w, v_xo_w):
    w_tree = dict(rel_bias=rel_bias, ln_g=ln_g, ln_b=ln_b, ffn_w_gate=ffn_w_gate, ffn_w_up=ffn_w_up,
                  ffn_w_down=ffn_w_down, w_in=w_in, conv_w=conv_w, conv_b=conv_b, ig_bias=ig_bias, fg_bias=fg_bias,
                  ml_norm_g=ml_norm_g, w_out=w_out, xq_w=xq_w, xkv_w=xkv_w, xo_w=xo_w)
    m_tree = dict(rel_bias=m_rel_bias, ln_g=m_ln_g, ln_b=m_ln_b, ffn_w_gate=m_ffn_w_gate, ffn_w_up=m_ffn_w_up,
                  ffn_w_down=m_ffn_w_down, w_in=m_w_in, conv_w=m_conv_w, conv_b=m_conv_b, ig_bias=m_ig_bias,
                  fg_bias=m_fg_bias, ml_norm_g=m_ml_norm_g, w_out=m_w_out, xq_w=m_xq_w, xkv_w=m_xkv_w, xo_w=m_xo_w)
    v_tree = dict(rel_bias=v_rel_bias, ln_g=v_ln_g, ln_b=v_ln_b, ffn_w_gate=v_ffn_w_gate, ffn_w_up=v_ffn_w_up,
                  ffn_w_down=v_ffn_w_down, w_in=v_w_in, conv_w=v_conv_w, conv_b=v_conv_b, ig_bias=v_ig_bias,
                  fg_bias=v_fg_bias, ml_norm_g=v_ml_norm_g, w_out=v_w_out, xq_w=v_xq_w, xkv_w=v_xkv_w, xo_w=v_xo_w)
    x0 = x[0]
    pad_ff = FF_PAD - FF_SHARD
    bf = lambda t: t.astype(BF16)

    pad_rows = lambda t: jnp.pad(t, ((0, pad_ff), (0, 0)))
    ffn_shards = [(pad_rows(bf(ffn_w_gate[0, l]).T), pad_rows(bf(ffn_w_up[0, l]).T), pad_rows(bf(ffn_w_down[0, l])))
                  for l in range(2)]
    pairs = lambda t: t.reshape(N_PAIR, FF_PAIR, D_MODEL)
    w_in_shard = jnp.pad(bf(w_in[0]), ((0, 0), (0, ATT_W - W_IN_SHARD)))
    small_shard = jnp.concatenate([ln_g[0], ln_b[0], conv_w[0], jnp.zeros((4, LANES), F32)], axis=0)
    gate_bias = jnp.pad(jnp.concatenate([ig_bias, fg_bias], axis=1), ((0, 0), (0, LANES - 2 * ML_HEADS)))
    buckets = _bucket_tables()

    wg0, wu0, wd0, small_all = _gather_two_level("ffn1_weights_gather", ffn_shards[0] + (small_shard,))
    wg0, wu0, wd0 = pairs(wg0), pairs(wu0), pairs(wd0)
    unshard = lambda t: jnp.moveaxis(t, 0, 1).reshape(4, D_MODEL)
    ln_g_full, ln_b_full, conv_w_full = unshard(small_all[:, 0:4]), unshard(small_all[:, 4:8]), unshard(small_all[:, 8:12])
    lng = lambda i: ln_g_full[i:i + 1]
    lnb = lambda i: ln_b_full[i:i + 1]

    u0, x1, a0, b0, win_all, wout_all, xq_all, xo_all, xkv_all = _ffn_fwd(
        x0, wg0, wu0, wd0, lng(0), lnb(0), "ffn1_fwd",
        gather=(w_in_shard, bf(w_out[0]), bf(xq_w[0]), bf(xo_w[0]), bf(xkv_w[0])))
    w_in_full = jnp.moveaxis(win_all[:, :, :W_IN_SHARD], 0, 1).reshape(D_MODEL, W_IN)
    w_main = w_in_full[:, :W_IN_MAIN]
    w_gate_cols = jnp.pad(w_in_full[:, W_IN_MAIN:], ((0, 0), (0, LANES - 2 * ML_HEADS)))
    w_out_full = wout_all.reshape(D_MODEL, D_MODEL)
    xq_full = xq_all.reshape(D_MODEL, D_MODEL)
    xo_full = xo_all.reshape(D_MODEL, D_MODEL)

    proj, wg1 = _matmul(x1, w_main, "nn", "proj_fwd", tn=W_IN_MAIN // 2, tk=D_MODEL, gather=(ffn_shards[1][0],))
    gates, = _matmul(x1, w_gate_cols, "nn", "gates_fwd", tk=D_MODEL)
    biasm = _bias_fwd(rel_bias, buckets)
    att, lse, wd1 = _dil_fwd(proj, biasm, gather=(ffn_shards[1][2],))
    qk = _conv_fwd(proj, conv_w_full, conv_b)
    y_m, c_prev, n_prev, m_prev, wu1 = _mlstm_fwd(qk, proj, gates, gate_bias, ml_norm_g, gather=(ffn_shards[1][1],))
    u1, x2 = _matmul_resid_ln((att, y_m), w_out_full, x1, lng(1), lnb(1), "w_out_fwd")
    q_x, = _matmul(x2, xq_full, "nn", "xq_fwd", tn=D_MODEL, tk=D_MODEL)
    kv, = _matmul(mem[0], xkv_all, "nn", "xkv_fwd", tk=D_MODEL)
    o_x = _xattn_fwd(q_x, kv)
    u2, x3 = _matmul_resid_ln((o_x,), xo_full, x2, lng(2), lnb(2), "xo_fwd")
    wg1, wu1, wd1 = pairs(wg1), pairs(wu1), pairs(wd1)
    u3, x4, a3, b3 = _ffn_fwd(x3, wg1, wu1, wd1, lng(3), lnb(3), "ffn2_fwd")
    dx4, loss_row = _loss_head(x4, loss_target[0])

    dx3, xb, df, da, db, hh, dg3, db3 = _ffn_bwd_x(dx4, u3, x3, wg1, wu1, wd1, lng(3), a3, b3, "ffn2_bwd_x")
    ffn2_send = (_ffn_bwd_w(xb, da, "ffn2_bwd_wg", down=False)[0], _ffn_bwd_w(xb, db, "ffn2_bwd_wu", down=False)[0],
                 _ffn_bwd_w(df, hh, "ffn2_bwd_wd", down=True)[0])

    du2, do_x, dg2, db2 = _ln_bwd(dx3, u2, lng(2), xo_full, "xattn_ln_bwd")
    g_xo, = _matmul(o_x, du2, "tn", "xo_bwd_w", tm=D_MODEL, tn=D_MODEL, out_dtype=BF16)
    dq_x, dkv = _xattn_bwd(q_x, kv, do_x)
    g_xq, = _matmul(x2, dq_x, "tn", "xq_bwd_w", tm=D_MODEL, tn=D_MODEL, out_dtype=BF16)
    g_xkv, = _matmul(mem[0], dkv, "tn", "xkv_bwd_w", tm=D_MODEL, tn=2 * D_MODEL // N_DEV, tk=MEM_LEN,
                     out_dtype=BF16, blocked_out=True)
    dx2, = _matmul(dq_x, xq_full, "nt", "xq_bwd_x", tn=D_MODEL, tk=D_MODEL, add=du2, add_scale=ALPHA)

    du1, dcat, dg1, db1 = _ln_bwd(dx2, u1, lng(1), w_out_full, "mixer_ln_bwd")
    g_w_out = jnp.concatenate(
        [_matmul(half, du1, "tn", f"w_out_bwd_w_{i}", tn=D_MODEL, out_dtype=BF16)[0] for i, half in enumerate((att, y_m))],
        axis=0)
    dqk, dv_m, do_m, dgates, dgate_bias, g_mlg, *ffn2_recv = _mlstm_bwd(
        qk, proj, gates, gate_bias, ml_norm_g, c_prev, n_prev, m_prev, dcat, exchange=tuple(ffn2_send))
    dqk_pre, g_conv_w, g_conv_b = _conv_bwd(proj, dqk, conv_w_full, conv_b)
    dq_a, dk_a, dv_a, dbias = _dil_bwd(proj, biasm, lse, att, dcat)
    g_rel = _bias_bwd(dbias.reshape(biasm.shape), buckets)[:, :ATT_HEADS]
    dproj = jnp.concatenate([dq_a, dk_a, dv_a, bf(dqk_pre), bf(dv_m), bf(do_m)], axis=1)
    g_w_main, = _matmul(x1, dproj, "tn", "proj_bwd_w", tm=D_MODEL, tn=W_IN_MAIN // 2, tk=1024, out_dtype=BF16)
    g_w_gates, = _matmul(x1, dgates, "tn", "gates_bwd_w", tm=D_MODEL, out_dtype=BF16)
    g_w_in = jnp.concatenate([g_w_main, g_w_gates[:, :2 * ML_HEADS]], axis=1)
    dx1, = _matmul(dproj, w_main, "nt", "proj_bwd_x", tn=D_MODEL, tk=W_IN_MAIN // 2, add=du1, add_scale=ALPHA)
    dx1, = _matmul(dgates, w_gate_cols, "nt", "gates_bwd_x", tn=D_MODEL, add=dx1)

    rows8 = lambda t: t.reshape(N_DEV, D_MODEL // N_DEV, D_MODEL)
    mid_send = (rows8(g_xo), rows8(g_xq), g_xkv, rows8(g_w_out),
                jnp.moveaxis(g_w_in.reshape(D_MODEL, N_DEV, W_IN_SHARD), 1, 0))
    dx0, xb, df, da, db, hh, dg0, db0, r_xo, r_xq, r_xkv, r_w_out, r_w_in = _ffn_bwd_x(
        dx1, u0, x0, wg0, wu0, wd0, lng(0), a0, b0, "ffn1_bwd_x", exchange=mid_send)
    small_blocks = {
        "rel_bias": _rep8(g_rel),
        "ln_g": _split8(jnp.concatenate([dg0, dg1, dg2, dg3], axis=0), 1),
        "ln_b": _split8(jnp.concatenate([db0, db1, db2, db3], axis=0), 1),
        "conv_w": _split8(g_conv_w, 1),
        "conv_b": _rep8(g_conv_b),
        "ig_bias": _rep8(dgate_bias[:, :ML_HEADS]),
        "fg_bias": _rep8(dgate_bias[:, ML_HEADS:2 * ML_HEADS]),
        "ml_norm_g": _rep8(g_mlg),
    }
    small_send = _pack_small([small_blocks[n] for n in SMALL], lead=(N_DEV,))
    g_wg, r_small = _ffn_bwd_w(xb, da, "ffn1_bwd_wg", down=False, exchange=(small_send,))
    g_wu, r_wg = _ffn_bwd_w(xb, db, "ffn1_bwd_wu", down=False, exchange=(g_wg,))
    g_wd, r_wu = _ffn_bwd_w(df, hh, "ffn1_bwd_wd", down=True, exchange=(g_wu,))
    r_wd, = _exchange_only("ffn1_grads_exchange", exchange=(g_wd,))
    ffn1_recv = [r_wg, r_wu, r_wd]

    res = {}
    for i, n in enumerate(("ffn_w_gate", "ffn_w_up", "ffn_w_down")):
        per_layer = [_adam2d(r[i], w_tree[n], m_tree[n], v_tree[n], f"adamw_{n}_{l}", layer=l)
                     for l, r in enumerate((ffn1_recv, ffn2_recv))]
        res[n] = [jnp.stack([per_layer[0][j], per_layer[1][j]])[None] for j in range(4)]
    for n, r in (("w_in", r_w_in), ("w_out", r_w_out), ("xq_w", r_xq), ("xkv_w", r_xkv), ("xo_w", r_xo)):
        res[n] = [t[None] for t in _adam2d(r, w_tree[n][0], m_tree[n][0], v_tree[n][0], f"adamw_{n}")]
    pack = lambda tree: _pack_small([tree[n].reshape(-1) for n in SMALL])
    small = [_unpack_small(t) for t in _adam2d(r_small, pack(w_tree), pack(m_tree), pack(v_tree), "adamw_small")]
    for n in SMALL:
        res[n] = [small[j][n] for j in range(4)]

    loss = lax.psum(loss_row[0, 0], ("x", "y", "c"))
    return (loss, dx0[None], *[res[n][0] for n in WEIGHTS], *[res[n][1] for n in WEIGHTS],
            *[res[n][2] for n in WEIGHTS], *[res[n][3] for n in WEIGHTS])
```

```python
import math

import numpy as np
import jax
import jax.numpy as jnp
from jax import lax
from jax.experimental import pallas as pl
from jax.experimental.pallas import tpu as pltpu

F32 = jnp.float32
BF16 = jnp.bfloat16

N_DEV = 8
D_MODEL = 1024
D_FF = 2816
FF_SHARD = D_FF // N_DEV
FF_PAD = 384
ATT_W = 512
ATT_HEADS = 8
DILATED = ((128, 1), (512, 4), (2048, 16))
BLK = 128
ML_W = 512
ML_HEADS = 4
ML_HD = 128
CHUNK = 128
CONV_K = 4
W_IN = 3592
W_IN_SHARD = W_IN // N_DEV
W_IN_MAIN = 3584
XA_HEADS = 4
XA_HD = 256
MEM_LEN = 256
REL_BUCKETS = 32
REL_MAX_DIST = 2048
ALPHA = 2.0 ** 0.25
LN_EPS = 1e-5
NEG = -1e30
ADAM_LR = 0.001
ADAM_B1 = 0.9
ADAM_B2 = 0.999
ADAM_EPS = 1e-08
ADAM_WD = 0.01
ADAM_STEP = 10
LANES = 128
VMEM_LIMIT = 58 * 1024 * 1024

NN = (((1,), (0,)), ((), ()))
NT = (((1,), (1,)), ((), ()))
TN = (((0,), (0,)), ((), ()))


def _dot(a, b, dims):
    return lax.dot_general(a, b, dims, preferred_element_type=F32)


def _params(*sem):
    return pltpu.CompilerParams(dimension_semantics=sem, vmem_limit_bytes=VMEM_LIMIT)


def _sigmoid(x):
    return 0.5 * jnp.tanh(0.5 * x) + 0.5


def _rowsum8(x):
    t, c = x.shape
    return jnp.sum(x.reshape(t // 8, 8, c), axis=0)


def _mesh_pos():
    x, y, c = lax.axis_index("x"), lax.axis_index("y"), lax.axis_index("c")
    return x, y, c, 4 * x + 2 * y + c


def _peer(x, y, c, k):
    px = 1 - x if k & 4 else x
    py = 1 - y if k & 2 else y
    pc = 1 - c if k & 1 else c
    return (px, py, pc), 4 * px + 2 * py + pc


def _call(body, *, name, grid, in_specs, out_specs, out_shape, args, scratch_shapes=(), sem=None,
          gather=(), exchange=()):
    in_specs, out_specs, out_shape, scratch = list(in_specs), list(out_specs), list(out_shape), list(scratch_shapes)
    ng, nc = len(gather), len(gather) + len(exchange)
    if nc == 0:
        return pl.pallas_call(body, name=name, grid=grid, in_specs=in_specs, out_specs=out_specs,
                              out_shape=out_shape, scratch_shapes=scratch, compiler_params=_params(*sem))(*args)
    n_in, n_out, n_scr = len(in_specs), len(out_specs), len(scratch)

    def wrapped(*refs):
        ins, cin = refs[:n_in], refs[n_in:n_in + nc]
        outs, cout = refs[n_in + nc:n_in + nc + n_out], refs[n_in + nc + n_out:n_in + 2 * nc + n_out]
        scr = refs[n_in + 2 * nc + n_out:n_in + 2 * nc + n_out + n_scr]
        send_sems, recv_sems, loc_sems = refs[-3:]
        first, last = None, None
        for ax, extent in enumerate(grid):
            f, l = pl.program_id(ax) == 0, pl.program_id(ax) == extent - 1
            first = f if first is None else first & f
            last = l if last is None else last & l

        def copies():
            x, y, c, me = _mesh_pos()
            out = []
            for a in range(nc):
                mine = cin[a] if a < ng else cin[a].at[me]
                out.append(pltpu.make_async_copy(mine, cout[a].at[me], loc_sems.at[a]))
                for k in range(1, N_DEV):
                    peer, pidx = _peer(x, y, c, k)
                    out.append(pltpu.make_async_remote_copy(
                        src_ref=cin[a] if a < ng else cin[a].at[pidx], dst_ref=cout[a].at[me],
                        send_sem=send_sems.at[a, k - 1], recv_sem=recv_sems.at[a, k - 1],
                        device_id=peer, device_id_type=pl.DeviceIdType.MESH))
            return out

        @pl.when(first)
        def _():
            for cp in copies():
                cp.start()

        body(*ins, *outs, *scr)

        @pl.when(last)
        def _():
            for cp in copies():
                cp.wait()

    hbm = pl.BlockSpec(memory_space=pl.ANY)
    comm_shapes = [jax.ShapeDtypeStruct((N_DEV,) + a.shape, a.dtype) for a in gather]
    comm_shapes += [jax.ShapeDtypeStruct(a.shape, a.dtype) for a in exchange]
    return pl.pallas_call(
        wrapped, name=name, grid=grid, in_specs=in_specs + [hbm] * nc, out_specs=out_specs + [hbm] * nc,
        out_shape=out_shape + comm_shapes,
        scratch_shapes=scratch + [pltpu.SemaphoreType.DMA((nc, N_DEV - 1)), pltpu.SemaphoreType.DMA((nc, N_DEV - 1)),
                                  pltpu.SemaphoreType.DMA((nc,))],
        compiler_params=_params(*(("arbitrary",) * len(grid))),
    )(*args, *gather, *exchange)


def _gather_two_level(name, arrays):
    na = len(arrays)

    def body(*refs):
        srcs, outs = refs[:na], refs[na:2 * na]
        send_sems, recv_sems, loc_sems = refs[2 * na:]
        x, y, c, me = _mesh_pos()
        here, sib = (x, y, c), (x, y, 1 - c)
        chips = [(1 - x, y), (x, 1 - y), (1 - x, 1 - y)]
        pos = lambda px, py, pc: 4 * px + 2 * py + pc

        def copy(a, k, block, to, src=None):
            return pltpu.make_async_remote_copy(
                src_ref=outs[a].at[block] if src is None else src, dst_ref=outs[a].at[block],
                send_sem=send_sems.at[a, k], recv_sem=recv_sems.at[a, k], device_id=to,
                device_id_type=pl.DeviceIdType.MESH)

        locs = [pltpu.make_async_copy(srcs[a], outs[a].at[me], loc_sems.at[a]) for a in range(na)]
        for cp in locs:
            cp.start()
        first = []
        for a in range(na):
            first.append(copy(a, 0, me, sib, src=srcs[a]))
            first += [copy(a, 1 + j, me, (*chip, c), src=srcs[a]) for j, chip in enumerate(chips)]
        for cp in first:
            cp.start()
        passed = []
        for a in range(na):
            for j, chip in enumerate(chips):
                copy(a, 1 + j, pos(*chip, c), here).wait_recv()
                passed.append(copy(a, 4 + j, pos(*chip, c), sib))
                passed[-1].start()
        for a in range(na):
            copy(a, 0, pos(x, y, 1 - c), here).wait_recv()
            for j, chip in enumerate(chips):
                copy(a, 4 + j, pos(*chip, 1 - c), here).wait_recv()
        for cp in first + passed:
            cp.wait_send()
        for cp in locs:
            cp.wait()

    hbm = pl.BlockSpec(memory_space=pl.ANY)
    return pl.pallas_call(
        body, name=name, in_specs=[hbm] * na, out_specs=[hbm] * na,
        out_shape=[jax.ShapeDtypeStruct((N_DEV,) + a.shape, a.dtype) for a in arrays],
        scratch_shapes=[pltpu.SemaphoreType.DMA((na, N_DEV - 1)), pltpu.SemaphoreType.DMA((na, N_DEV - 1)),
                        pltpu.SemaphoreType.DMA((na,))],
    )(*arrays)


def _exchange_copies(s_ref, land_ref, send_sems, recv_sems):
    x, y, c, me = _mesh_pos()
    out = []
    for k in range(1, N_DEV):
        peer, pidx = _peer(x, y, c, k)
        out.append(pltpu.make_async_remote_copy(
            src_ref=s_ref.at[pidx], dst_ref=land_ref.at[me], send_sem=send_sems.at[k - 1],
            recv_sem=recv_sems.at[k - 1], device_id=peer, device_id_type=pl.DeviceIdType.MESH))
    return out


def _exchange_start(send, name):
    hbm = pl.BlockSpec(memory_space=pltpu.HBM)
    sem = pl.BlockSpec(memory_space=pltpu.SEMAPHORE)

    def body(s_ref, land_ref, send_sems, recv_sems, s_thru, land_thru, token):
        for cp in _exchange_copies(s_ref, land_ref, send_sems, recv_sems):
            cp.start()
        token[...] = jnp.zeros_like(token)

    return pl.pallas_call(
        body, name=name,
        out_shape=(pltpu.SemaphoreType.DMA((N_DEV - 1,)), pltpu.SemaphoreType.DMA((N_DEV - 1,)),
                   pltpu.HBM(send.shape, send.dtype), pltpu.HBM(send.shape, send.dtype),
                   jax.ShapeDtypeStruct((8, LANES), F32)),
        in_specs=(hbm, hbm), out_specs=(sem, sem, hbm, hbm, pl.BlockSpec(memory_space=pltpu.VMEM)),
        input_output_aliases={0: 2, 1: 3},
        compiler_params=pltpu.CompilerParams(has_side_effects=pltpu.SideEffectType.DATAFLOW_SIDE_EFFECTING),
    )(pltpu.with_memory_space_constraint(send, pltpu.HBM),
      pltpu.with_memory_space_constraint(lax.empty(send.shape, send.dtype), pltpu.HBM))


def _exchange_wait(send_sems, recv_sems, s_thru, land_thru, after, name):
    hbm = pl.BlockSpec(memory_space=pltpu.HBM)
    sem = pl.BlockSpec(memory_space=pltpu.SEMAPHORE)

    def body(s_ref, land_ref, send_sems, recv_sems, after_ref, s_out, got_ref):
        for cp in _exchange_copies(s_ref, land_ref, send_sems, recv_sems):
            cp.wait_send()
            cp.wait_recv()

    return pl.pallas_call(
        body, name=name,
        out_shape=(pltpu.HBM(s_thru.shape, s_thru.dtype), pltpu.HBM(s_thru.shape, s_thru.dtype)),
        in_specs=(hbm, hbm, sem, sem, pl.BlockSpec(memory_space=pl.ANY)), out_specs=(hbm, hbm),
        input_output_aliases={0: 0, 1: 1},
        compiler_params=pltpu.CompilerParams(has_side_effects=pltpu.SideEffectType.DATAFLOW_SIDE_EFFECTING),
    )(s_thru, land_thru, send_sems, recv_sems, after)


def _exchange_only(name, gather=(), exchange=()):
    return _call(lambda: None, name=name, grid=(1,), in_specs=[], out_specs=[], out_shape=[], args=(),
                 gather=gather, exchange=exchange)


def _matmul(a, b, mode, name, *, out_dtype=F32, tm=1024, tn=512, tk=512, add=None, add_scale=1.0,
            blocked_out=False, gather=(), exchange=()):
    blocked_b = b.ndim == 3
    if blocked_b:
        (m, k), (nb, _, tn) = a.shape, b.shape
        n = nb * tn
    elif mode == "nn":
        (m, k), (_, n) = a.shape, b.shape
    elif mode == "nt":
        (m, k), (n, _) = a.shape, b.shape
    else:
        (k, m), (_, n) = a.shape, b.shape
    tm, tn, tk = min(tm, m), min(tn, n), min(tk, k)
    nk = k // tk
    dims = {"nn": NN, "nt": NT, "tn": TN}[mode]
    if mode == "tn":
        a_spec = pl.BlockSpec((tk, tm), lambda i, j, kk: (kk, i))
    else:
        a_spec = pl.BlockSpec((tm, tk), lambda i, j, kk: (i, kk))
    if blocked_b:
        b_spec = pl.BlockSpec((None, tk, tn), lambda i, j, kk: (j, kk, 0))
    elif mode == "nt":
        b_spec = pl.BlockSpec((tn, tk), lambda i, j, kk: (j, kk))
    else:
        b_spec = pl.BlockSpec((tk, tn), lambda i, j, kk: (kk, j))
    if blocked_out:
        o_spec = pl.BlockSpec((None, tm, tn), lambda i, j, kk: (j, i, 0))
        o_shape = jax.ShapeDtypeStruct((n // tn, m, tn), out_dtype)
    else:
        o_spec = pl.BlockSpec((tm, tn), lambda i, j, kk: (i, j))
        o_shape = jax.ShapeDtypeStruct((m, n), out_dtype)
    has_add = add is not None
    cache_a = nk == 1 and mode != "tn" and n // tn > 1 and a.dtype != BF16

    def body(*refs):
        if has_add:
            a_ref, b_ref, add_ref, o_ref, s_ref = refs
        else:
            a_ref, b_ref, o_ref, s_ref = refs
        kk = pl.program_id(2)
        if cache_a:
            @pl.when(pl.program_id(1) == 0)
            def _():
                s_ref[...] = a_ref[...].astype(BF16)

            lhs = s_ref[...]
        else:
            lhs = a_ref[...].astype(BF16)
        part = _dot(lhs, b_ref[...].astype(BF16), dims)

        def finish(r):
            if has_add:
                r = r + add_scale * add_ref[...]
            o_ref[...] = r.astype(out_dtype)

        if nk == 1:
            finish(part)
            return

        @pl.when(kk == 0)
        def _():
            s_ref[...] = part

        @pl.when(kk > 0)
        def _():
            s_ref[...] += part

        @pl.when(kk == nk - 1)
        def _():
            finish(s_ref[...])

    if nk > 1:
        scratch = [pltpu.VMEM((tm, tn), F32)]
    else:
        scratch = [pltpu.VMEM((tm, tk), BF16) if cache_a else pltpu.VMEM((8, LANES), F32)]
    return _call(
        body, name=name, grid=(m // tm, n // tn, nk),
        in_specs=[a_spec, b_spec] + ([pl.BlockSpec((tm, tn), lambda i, j, kk: (i, j))] if has_add else []),
        out_specs=[o_spec], out_shape=[o_shape], args=(a, b) + ((add,) if has_add else ()),
        scratch_shapes=scratch, sem=("parallel", "arbitrary", "arbitrary"),
        gather=gather, exchange=exchange)


def _ln_fwd_math(u, g, b):
    mu = jnp.mean(u, axis=-1, keepdims=True)
    uc = u - mu
    var = jnp.mean(uc * uc, axis=-1, keepdims=True)
    return uc * lax.rsqrt(var + LN_EPS) * g + b


def _ln_bwd_math(dy, u, g):
    mu = jnp.mean(u, axis=-1, keepdims=True)
    uc = u - mu
    var = jnp.mean(uc * uc, axis=-1, keepdims=True)
    rstd = lax.rsqrt(var + LN_EPS)
    xhat = uc * rstd
    dxh = dy * g
    m1 = jnp.mean(dxh, axis=-1, keepdims=True)
    m2 = jnp.mean(dxh * xhat, axis=-1, keepdims=True)
    return rstd * (dxh - m1 - xhat * m2), xhat


def _matmul_resid_ln(pieces, w, x, g, b, name, tm=1024):
    s = pieces[0].shape[0]
    k, d = w.shape
    widths = [p.shape[1] for p in pieces]

    def body(*refs):
        a_refs = refs[:len(pieces)]
        w_ref, x_ref, g_ref, b_ref, u_ref, y_ref = refs[len(pieces):]
        u = ALPHA * x_ref[...]
        lo = 0
        for a_ref, width in zip(a_refs, widths):
            u = u + _dot(a_ref[...].astype(BF16), w_ref[lo:lo + width, :], NN)
            lo += width
        u_ref[...] = u
        y_ref[...] = _ln_fwd_math(u, g_ref[...], b_ref[...])

    row = pl.BlockSpec((tm, d), lambda i: (i, 0))
    vec = pl.BlockSpec((1, d), lambda i: (0, 0))
    return pl.pallas_call(
        body, name=name, grid=(s // tm,),
        in_specs=[pl.BlockSpec((tm, width), lambda i: (i, 0)) for width in widths]
        + [pl.BlockSpec((k, d), lambda i: (0, 0)), row, vec, vec],
        out_specs=[row, row], out_shape=[jax.ShapeDtypeStruct((s, d), F32)] * 2,
        compiler_params=_params("parallel"),
    )(*pieces, w, x, g, b)


def _ln_bwd(dy, u, g, w, name, tm=1024):
    s, d = dy.shape
    n = w.shape[0]
    nt = s // tm

    def body(dy_ref, u_ref, g_ref, w_ref, du_ref, dz_ref, dg_ref, db_ref, g8, b8):
        i = pl.program_id(0)
        dy_ = dy_ref[...]
        du, xhat = _ln_bwd_math(dy_, u_ref[...], g_ref[...])
        du_ref[...] = du
        dz_ref[...] = _dot(du.astype(BF16), w_ref[...], NT)

        @pl.when(i == 0)
        def _():
            g8[...] = jnp.zeros_like(g8)
            b8[...] = jnp.zeros_like(b8)

        g8[...] += _rowsum8(dy_ * xhat)
        b8[...] += _rowsum8(dy_)

        @pl.when(i == nt - 1)
        def _():
            dg_ref[...] = jnp.sum(g8[...], axis=0, keepdims=True)
            db_ref[...] = jnp.sum(b8[...], axis=0, keepdims=True)

    row = pl.BlockSpec((tm, d), lambda i: (i, 0))
    vec = pl.BlockSpec((1, d), lambda i: (0, 0))
    return pl.pallas_call(
        body, name=name, grid=(nt,),
        in_specs=[row, row, vec, pl.BlockSpec((n, d), lambda i: (0, 0))],
        out_specs=[row, pl.BlockSpec((tm, n), lambda i: (i, 0)), vec, vec],
        out_shape=[jax.ShapeDtypeStruct((s, d), F32), jax.ShapeDtypeStruct((s, n), F32),
                   jax.ShapeDtypeStruct((1, d), F32), jax.ShapeDtypeStruct((1, d), F32)],
        scratch_shapes=[pltpu.VMEM((8, d), F32), pltpu.VMEM((8, d), F32)],
        compiler_params=_params("arbitrary"),
    )(dy, u, g, w)


FF_PAIR = 2 * FF_PAD
N_PAIR = N_DEV // 2
FF_COLS = 256


def _ffn_fwd(x, wgt, wut, wd, g, b, name, tm=1024, gather=()):
    s, d = x.shape

    def body(x_ref, wg_ref, wu_ref, wd_ref, g_ref, b_ref, u_ref, y_ref, a_ref, bb_ref, xb, acc):
        k = pl.program_id(1)

        @pl.when(k == 0)
        def _():
            xb[...] = x_ref[...].astype(BF16)

        a = _dot(xb[...], wg_ref[...], NT)
        bb = _dot(xb[...], wu_ref[...], NT)
        a_ref[...] = a.astype(BF16)
        bb_ref[...] = bb.astype(BF16)
        h = (a * _sigmoid(a) * bb).astype(BF16)
        part = _dot(h, wd_ref[...], NN)

        @pl.when(k == 0)
        def _():
            acc[...] = part

        @pl.when(k > 0)
        def _():
            acc[...] += part

        @pl.when(k == N_PAIR - 1)
        def _():
            u = ALPHA * x_ref[...] + 0.5 * acc[...]
            u_ref[...] = u
            y_ref[...] = _ln_fwd_math(u, g_ref[...], b_ref[...])

    row = pl.BlockSpec((tm, d), lambda i, k: (i, 0))
    vec = pl.BlockSpec((1, d), lambda i, k: (0, 0))
    w_in = pl.BlockSpec((None, FF_PAIR, d), lambda i, k: (k, 0, 0))
    w_dn = w_in
    hid = pl.BlockSpec((tm, FF_PAIR), lambda i, k: (i, k))
    return _call(
        body, name=name, grid=(s // tm, N_PAIR),
        in_specs=[row, w_in, w_in, w_dn, vec, vec], out_specs=[row, row, hid, hid],
        out_shape=[jax.ShapeDtypeStruct((s, d), F32)] * 2 + [jax.ShapeDtypeStruct((s, N_DEV * FF_PAD), BF16)] * 2,
        args=(x, wgt, wut, wd, g, b),
        scratch_shapes=[pltpu.VMEM((tm, d), BF16), pltpu.VMEM((tm, d), F32)],
        sem=("parallel", "arbitrary"), gather=gather)


def _ffn_bwd_x(dy, u, x, wgt, wut, wd, g, a_fwd, b_fwd, name, tm=512, exchange=()):
    s, d = x.shape
    nt = s // tm
    ffp = N_DEV * FF_PAD

    def body(dy_ref, u_ref, x_ref, wg_ref, wu_ref, wd_ref, g_ref, a_ref, bb_ref,
             dx_ref, xb, df_ref, da_ref, db_ref, h_ref, dg_ref, dbl_ref,
             dfb, du_s, acc, g8, b8):
        i = pl.program_id(0)
        k = pl.program_id(1)

        @pl.when(k == 0)
        def _():
            dy_ = dy_ref[...]
            du, xhat = _ln_bwd_math(dy_, u_ref[...], g_ref[...])
            du_s[...] = du
            dfb[...] = (0.5 * du).astype(BF16)
            df_ref[...] = dfb[...]
            xb[...] = x_ref[...].astype(BF16)

            @pl.when(i == 0)
            def _():
                g8[...] = jnp.zeros_like(g8)
                b8[...] = jnp.zeros_like(b8)

            g8[...] += _rowsum8(dy_ * xhat)
            b8[...] += _rowsum8(dy_)

        dh_all = _dot(dfb[...], wd_ref[...], NT)

        def gate_grads(c):
            cs = slice(FF_COLS * c, FF_COLS * (c + 1))
            a = a_ref[:, cs].astype(F32)
            bb = bb_ref[:, cs].astype(F32)
            dh = dh_all[:, cs]
            sig = _sigmoid(a)
            sa = a * sig
            h_ref[:, cs] = (sa * bb).astype(BF16)
            da = (dh * bb * (sig * (1.0 + a * (1.0 - sig)))).astype(BF16)
            db = (dh * sa).astype(BF16)
            da_ref[:, cs] = da
            db_ref[:, cs] = db
            return da, db

        n_chunks = FF_PAIR // FF_COLS
        chunks = [gate_grads(0)]
        part = None
        for c in range(n_chunks):
            if c + 1 < n_chunks:
                chunks.append(gate_grads(c + 1))
            cs = slice(FF_COLS * c, FF_COLS * (c + 1))
            pc = _dot(chunks[c][0], wg_ref[cs, :], NN) + _dot(chunks[c][1], wu_ref[cs, :], NN)
            part = pc if part is None else part + pc

        @pl.when(k == 0)
        def _():
            acc[...] = part

        @pl.when(k > 0)
        def _():
            acc[...] += part

        @pl.when(k == N_PAIR - 1)
        def _():
            dx_ref[...] = ALPHA * du_s[...] + acc[...]

        @pl.when((k == N_PAIR - 1) & (i == nt - 1))
        def _():
            dg_ref[...] = jnp.sum(g8[...], axis=0, keepdims=True)
            dbl_ref[...] = jnp.sum(b8[...], axis=0, keepdims=True)

    row = pl.BlockSpec((tm, d), lambda i, k: (i, 0))
    vec = pl.BlockSpec((1, d), lambda i, k: (0, 0))
    w_in = pl.BlockSpec((None, FF_PAIR, d), lambda i, k: (k, 0, 0))
    hid = pl.BlockSpec((tm, FF_PAIR), lambda i, k: (i, k))
    return _call(
        body, name=name, grid=(nt, N_PAIR),
        in_specs=[row, row, row, w_in, w_in, w_in, vec, hid, hid],
        out_specs=[row, row, row, hid, hid, hid, vec, vec],
        out_shape=[jax.ShapeDtypeStruct((s, d), F32), jax.ShapeDtypeStruct((s, d), BF16),
                   jax.ShapeDtypeStruct((s, d), BF16),
                   jax.ShapeDtypeStruct((s, ffp), BF16), jax.ShapeDtypeStruct((s, ffp), BF16),
                   jax.ShapeDtypeStruct((s, ffp), BF16),
                   jax.ShapeDtypeStruct((1, d), F32), jax.ShapeDtypeStruct((1, d), F32)],
        args=(dy, u, x, wgt, wut, wd, g, a_fwd, b_fwd),
        scratch_shapes=[pltpu.VMEM((tm, d), BF16), pltpu.VMEM((tm, d), F32),
                        pltpu.VMEM((tm, d), F32), pltpu.VMEM((8, d), F32), pltpu.VMEM((8, d), F32)],
        sem=("arbitrary", "arbitrary"), exchange=exchange)


def _ffn_bwd_w(tok, hid, name, *, down, tm=4096, exchange=()):
    s, d = tok.shape
    tm = min(tm, s)
    nt = s // tm

    def body(t_ref, h_ref, dw_ref, acc):
        i = pl.program_id(1)
        part = _dot(h_ref[...], t_ref[...], TN) if down else _dot(t_ref[...], h_ref[...], TN)

        @pl.when(i == 0)
        def _():
            acc[...] = part

        @pl.when(i > 0)
        def _():
            acc[...] += part

        @pl.when(i == nt - 1)
        def _():
            for j in range(2):
                lo = j * FF_PAD
                dw_ref[j] = (acc[lo:lo + FF_SHARD, :] if down else acc[:, lo:lo + FF_SHARD]).astype(BF16)

    blk = (FF_SHARD, d) if down else (d, FF_SHARD)
    return _call(
        body, name=name, grid=(N_PAIR, nt),
        in_specs=[pl.BlockSpec((tm, d), lambda k, i: (i, 0)), pl.BlockSpec((tm, FF_PAIR), lambda k, i: (i, k))],
        out_specs=[pl.BlockSpec((2,) + blk, lambda k, i: (k, 0, 0))],
        out_shape=[jax.ShapeDtypeStruct((N_DEV,) + blk, BF16)], args=(tok, hid),
        scratch_shapes=[pltpu.VMEM((FF_PAIR, d) if down else (d, FF_PAIR), F32)],
        sem=("parallel", "arbitrary"), exchange=exchange)


def _bucket_tables():
    qi = np.arange(BLK)[:, None]
    ki = np.arange(2 * BLK)[None, :]
    off = qi + BLK - ki
    out = []
    for window, dil in DILATED:
        n_keys = window // dil
        dist = dil * np.clip(off, 0, n_keys)
        exact = REL_BUCKETS // 2
        df = np.maximum(dist, 1).astype(np.float32)
        large = exact + (np.log(df / np.float32(exact)) / np.float32(math.log(REL_MAX_DIST / exact))
                         * np.float32(REL_BUCKETS - exact)).astype(np.int32)
        large = np.minimum(large, REL_BUCKETS - 1)
        bucket = np.where(dist < exact, dist, large).astype(np.int32)
        band = (off >= 0) & (off <= n_keys)
        out.append(np.where(band, bucket, -1))
    return np.stack(out).astype(np.int32)


def _bias_fwd(rel_bias, buckets, name="bias_fwd"):
    def body(tbl_ref, bkt_ref, out_ref):
        bkt = bkt_ref[...]
        for h in range(ATT_HEADS):
            acc = jnp.full((BLK, 2 * BLK), NEG, F32)
            for bb in range(REL_BUCKETS):
                acc = jnp.where(bkt == bb, tbl_ref[bb, h], acc)
            out_ref[h] = acc

    nbr = len(DILATED)
    return pl.pallas_call(
        body, name=name, grid=(nbr,),
        in_specs=[pl.BlockSpec(memory_space=pltpu.SMEM),
                  pl.BlockSpec((None, BLK, 2 * BLK), lambda r: (r, 0, 0))],
        out_specs=pl.BlockSpec((None, ATT_HEADS, BLK, 2 * BLK), lambda r: (r, 0, 0, 0)),
        out_shape=jax.ShapeDtypeStruct((nbr, ATT_HEADS, BLK, 2 * BLK), F32),
        compiler_params=_params("parallel"),
    )(rel_bias, buckets)


def _bias_bwd(dbias, buckets, name="bias_bwd"):
    nbr = len(DILATED)

    def body(db_ref, bkt_ref, out_ref):
        r = pl.program_id(0)

        @pl.when(r == 0)
        def _():
            out_ref[...] = jnp.zeros_like(out_ref)

        bkt = bkt_ref[...]
        rowi = lax.broadcasted_iota(jnp.int32, (REL_BUCKETS, LANES), 0)
        coli = lax.broadcasted_iota(jnp.int32, (REL_BUCKETS, LANES), 1)
        acc = jnp.zeros((REL_BUCKETS, LANES), F32)
        for h in range(ATT_HEADS):
            x = db_ref[h]
            for bb in range(REL_BUCKETS):
                part = jnp.sum(jnp.where(bkt == bb, x, 0.0), axis=0, keepdims=True)
                tot = jnp.sum(part, axis=1, keepdims=True)
                acc = acc + jnp.where((rowi == bb) & (coli == h), tot, 0.0)
        out_ref[...] += acc

    return pl.pallas_call(
        body, name=name, grid=(nbr,),
        in_specs=[pl.BlockSpec((None, ATT_HEADS, BLK, 2 * BLK), lambda r: (r, 0, 0, 0)),
                  pl.BlockSpec((None, BLK, 2 * BLK), lambda r: (r, 0, 0))],
        out_specs=pl.BlockSpec((REL_BUCKETS, LANES), lambda r: (0, 0)),
        out_shape=jax.ShapeDtypeStruct((REL_BUCKETS, LANES), F32),
        compiler_params=_params("arbitrary"),
    )(dbias, buckets)


def _stack_heads(pair, lo):
    return jnp.concatenate([jnp.where(lo, pair, 0.0), jnp.where(lo, 0.0, pair)], axis=0)


def _head_cols(pair, lo, reduce):
    fill = -jnp.inf if reduce is jnp.max else 0.0
    return jnp.concatenate([reduce(jnp.where(lo, pair, fill), axis=1, keepdims=True),
                            reduce(jnp.where(lo, fill, pair), axis=1, keepdims=True)], axis=0)


def _unstack_heads(x2, lo):
    return jnp.where(lo, x2[:BLK], x2[BLK:])


def _att_scores(q2, kk, bias2, first_ok):
    sc = _dot(q2, kk, NT) * (64 ** -0.5) + bias2
    return jnp.where(first_ok, sc, NEG)


DIL_TILE = 2048
DIL_COLS = ATT_W // LANES
DIL_GROUP = 4


def _dil_rows(dil, n, r, base=0):
    start = base + n * (BLK * dil) + r
    return pl.ds(start, BLK, stride=dil) if dil > 1 else pl.ds(start, BLK)


def _dil_in_specs(tile_of):
    cur = lambda col: pl.BlockSpec((DIL_TILE, LANES), lambda p, i: (tile_of(i), col * DIL_COLS + p))
    prev = lambda col: pl.BlockSpec((DIL_TILE, LANES), lambda p, i: (jnp.maximum(tile_of(i) - 1, 0), col * DIL_COLS + p))
    bias = pl.BlockSpec((len(DILATED), None, 2 * BLK, 2 * BLK), lambda p, i: (0, p, 0, 0))
    return [cur(0), prev(1), cur(1), prev(2), cur(2), bias]


def _pair_bias(biasm):
    return biasm.reshape(len(DILATED), DIL_COLS, 2 * BLK, 2 * BLK)


def _dil_fwd(proj, biasm, name="dil_fwd", gather=()):
    s = proj.shape[0]
    nt = s // DIL_TILE
    tt = DIL_TILE

    def body(q_ref, kp_ref, kc_ref, vp_ref, vc_ref, bias_ref, att_ref, lse_ref, k2, v2, ob, lb):
        t = pl.program_id(1)
        k2[0:tt, :] = kp_ref[...]
        k2[tt:2 * tt, :] = kc_ref[...]
        v2[0:tt, :] = vp_ref[...]
        v2[tt:2 * tt, :] = vc_ref[...]
        lo = lax.broadcasted_iota(jnp.int32, (BLK, LANES), 1) < 64
        kidx = lax.broadcasted_iota(jnp.int32, (2 * BLK, 2 * BLK), 1)
        for b, (_, dil) in enumerate(DILATED):
            for j0 in range(0, tt // BLK, DIL_GROUP):
                grp = range(DIL_GROUP)
                rn = [((j0 + i) % dil, (j0 + i) // dil) for i in grp]
                here = [_dil_rows(dil, n, r) for r, n in rn]
                cur = [_dil_rows(dil, n, r, tt) for r, n in rn]
                prev = [_dil_rows(dil, n - 1, r, tt) for r, n in rn]
                q2 = [_stack_heads(q_ref[here[i], :], lo).astype(BF16) for i in grp]
                kk = [jnp.concatenate([k2[prev[i], :], k2[cur[i], :]], axis=0).astype(BF16) for i in grp]
                vv = [jnp.concatenate([v2[prev[i], :], v2[cur[i], :]], axis=0).astype(BF16) for i in grp]
                sc = [_att_scores(q2[i], kk[i], bias_ref[b], (t > 0) | (rn[i][1] > 0) | (kidx >= BLK)) for i in grp]
                mx = [jnp.max(sc[i], axis=1, keepdims=True) for i in grp]
                pe = [jnp.exp(sc[i] - mx[i]) for i in grp]
                l = [jnp.sum(pe[i], axis=1, keepdims=True) for i in grp]
                o2 = [_dot(pe[i].astype(BF16), vv[i], NN) for i in grp]
                for i in grp:
                    ob.at[b][here[i], :] = _unstack_heads(o2[i] / l[i], lo)
                    lb.at[b][here[i], :] = _unstack_heads(jnp.broadcast_to(mx[i] + jnp.log(l[i]), (2 * BLK, LANES)), lo)
        l0, l1, l2 = lb[0], lb[1], lb[2]
        mx = jnp.maximum(jnp.maximum(l0, l1), l2)
        e0, e1, e2 = jnp.exp(l0 - mx), jnp.exp(l1 - mx), jnp.exp(l2 - mx)
        tot = e0 + e1 + e2
        att_ref[...] = (e0 * ob[0] + e1 * ob[1] + e2 * ob[2]) / tot
        lse_ref[...] = mx + jnp.log(tot)

    out = pl.BlockSpec((tt, LANES), lambda p, i: (i, p))
    return _call(
        body, name=name, grid=(DIL_COLS, nt), in_specs=_dil_in_specs(lambda i: i), out_specs=[out, out],
        out_shape=[jax.ShapeDtypeStruct((s, ATT_W), F32)] * 2, args=(proj, proj, proj, proj, proj, _pair_bias(biasm)),
        scratch_shapes=[pltpu.VMEM((2 * tt, LANES), F32), pltpu.VMEM((2 * tt, LANES), F32),
                        pltpu.VMEM((len(DILATED), tt, LANES), F32), pltpu.VMEM((len(DILATED), tt, LANES), F32)],
        sem=("parallel", "parallel"), gather=gather)


def _dil_bwd(proj, biasm, lse, att, dcat, name="dil_bwd"):
    s = proj.shape[0]
    nt = s // DIL_TILE
    tt = DIL_TILE
    nbr = len(DILATED)

    def body(q_ref, kp_ref, kc_ref, vp_ref, vc_ref, bias_ref, lse_ref, att_ref, datt_ref,
             dq_ref, dk_ref, dv_ref, dbias_ref, k2, v2, dqa, dka, dva, kcar, vcar):
        i = pl.program_id(1)
        t = nt - 1 - i
        k2[0:tt, :] = kp_ref[...]
        k2[tt:2 * tt, :] = kc_ref[...]
        v2[0:tt, :] = vp_ref[...]
        v2[tt:2 * tt, :] = vc_ref[...]

        @pl.when(i == 0)
        def _():
            kcar[...] = jnp.zeros_like(kcar)
            vcar[...] = jnp.zeros_like(vcar)
            dbias_ref[...] = jnp.zeros_like(dbias_ref)

        dqa[...] = jnp.zeros_like(dqa)
        dka[0:tt, :] = jnp.zeros((tt, LANES), F32)
        dva[0:tt, :] = jnp.zeros((tt, LANES), F32)
        dka[tt:2 * tt, :] = kcar[...]
        dva[tt:2 * tt, :] = vcar[...]
        lo = lax.broadcasted_iota(jnp.int32, (BLK, LANES), 1) < 64
        kidx = lax.broadcasted_iota(jnp.int32, (2 * BLK, 2 * BLK), 1)
        for b, (_, dil) in enumerate(DILATED):
            for j0 in range(0, tt // BLK, DIL_GROUP):
                grp = range(DIL_GROUP)
                rn = [((j0 + i) % dil, (j0 + i) // dil) for i in grp]
                here = [_dil_rows(dil, n, r) for r, n in rn]
                cur = [_dil_rows(dil, n, r, tt) for r, n in rn]
                prev = [_dil_rows(dil, n - 1, r, tt) for r, n in rn]
                dat = [datt_ref[here[i], :] for i in grp]
                q2 = [_stack_heads(q_ref[here[i], :], lo).astype(BF16) for i in grp]
                dom = [_stack_heads(dat[i], lo).astype(BF16) for i in grp]
                kk = [jnp.concatenate([k2[prev[i], :], k2[cur[i], :]], axis=0).astype(BF16) for i in grp]
                vv = [jnp.concatenate([v2[prev[i], :], v2[cur[i], :]], axis=0).astype(BF16) for i in grp]
                sc = [_att_scores(q2[i], kk[i], bias_ref[b], (t > 0) | (rn[i][1] > 0) | (kidx >= BLK)) for i in grp]
                dp = [_dot(dom[i], vv[i], NT) for i in grp]
                pr = [jnp.exp(sc[i] - _head_cols(lse_ref[here[i], :], lo, jnp.max)) for i in grp]
                ds = [pr[i] * (dp[i] - _head_cols(dat[i] * att_ref[here[i], :], lo, jnp.sum)) for i in grp]
                dsb = [(ds[i] * (64 ** -0.5)).astype(BF16) for i in grp]
                dq2 = [_dot(dsb[i], kk[i], NN) for i in grp]
                dk2 = [_dot(dsb[i], q2[i], TN) for i in grp]
                dv2 = [_dot(pr[i].astype(BF16), dom[i], TN) for i in grp]
                for i in grp:
                    dbias_ref[b] += ds[i]
                    dqa[here[i], :] += _unstack_heads(dq2[i], lo)
                    dka[prev[i], :] += dk2[i][:BLK]
                    dka[cur[i], :] += dk2[i][BLK:]
                    dva[prev[i], :] += dv2[i][:BLK]
                    dva[cur[i], :] += dv2[i][BLK:]
        dq_ref[...] = dqa[...].astype(BF16)
        dk_ref[...] = dka[tt:2 * tt, :].astype(BF16)
        dv_ref[...] = dva[tt:2 * tt, :].astype(BF16)
        kcar[...] = dka[0:tt, :]
        vcar[...] = dva[0:tt, :]

    rev = lambda i: nt - 1 - i
    out = pl.BlockSpec((tt, LANES), lambda p, i: (rev(i), p))
    two = lambda: pltpu.VMEM((2 * tt, LANES), F32)
    one = lambda: pltpu.VMEM((tt, LANES), F32)
    return pl.pallas_call(
        body, name=name, grid=(DIL_COLS, nt),
        in_specs=_dil_in_specs(rev) + [out, out, out],
        out_specs=[out, out, out, pl.BlockSpec((nbr, None, 2 * BLK, 2 * BLK), lambda p, i: (0, p, 0, 0))],
        out_shape=[jax.ShapeDtypeStruct((s, ATT_W), BF16)] * 3
        + [jax.ShapeDtypeStruct((nbr, DIL_COLS, 2 * BLK, 2 * BLK), F32)],
        scratch_shapes=[two(), two(), one(), two(), two(), one(), one()],
        compiler_params=_params("arbitrary", "arbitrary"),
    )(proj, proj, proj, proj, proj, _pair_bias(biasm), lse, att, dcat)


QK_COL0 = (3 * ATT_W) // ATT_W


HALO = 8


def _conv_shifted(prev8, cur, j):
    sh = CONV_K - 1 - j
    if sh == 0:
        return cur
    rolled = pltpu.roll(cur, sh, 0)
    row8 = lax.broadcasted_iota(jnp.int32, prev8.shape, 0)
    top = jnp.where(row8 < sh, pltpu.roll(prev8, sh, 0), rolled[:HALO])
    return top if cur.shape[0] == HALO else jnp.concatenate([top, rolled[HALO:]], axis=0)


def _conv_z(prev8, cur, w_ref, b_ref, taps=None):
    z = b_ref[...]
    for j in range(CONV_K):
        tap = _conv_shifted(prev8, cur, j)
        if taps is not None:
            taps.append(tap)
        z = z + tap * w_ref[j:j + 1, :]
    return z


def _silu_grad(z):
    sig = _sigmoid(z)
    return sig * (1.0 + z * (1.0 - sig))


def _conv_fwd(proj, conv_w, conv_b, name="conv_fwd", tm=512):
    s = proj.shape[0]
    w = ATT_W
    per = tm // HALO

    def body(prev_ref, cur_ref, w_ref, b_ref, o_ref):
        i = pl.program_id(1)
        prev8 = jnp.where(i > 0, prev_ref[...], 0.0)
        z = _conv_z(prev8, cur_ref[...], w_ref, b_ref)
        o_ref[...] = z * _sigmoid(z)

    return pl.pallas_call(
        body, name=name, grid=(2, s // tm),
        in_specs=[pl.BlockSpec((HALO, w), lambda j, i: (jnp.maximum(i * per - 1, 0), QK_COL0 + j)),
                  pl.BlockSpec((tm, w), lambda j, i: (i, QK_COL0 + j)),
                  pl.BlockSpec((CONV_K, w), lambda j, i: (0, j)),
                  pl.BlockSpec((1, w), lambda j, i: (0, j))],
        out_specs=pl.BlockSpec((tm, w), lambda j, i: (i, j)),
        out_shape=jax.ShapeDtypeStruct((s, 2 * ML_W), F32),
        compiler_params=_params("parallel", "parallel"),
    )(proj, proj, conv_w, conv_b)


def _conv_bwd(proj, dqk, conv_w, conv_b, name="conv_bwd", tm=512):
    s = proj.shape[0]
    w = ATT_W
    nt = s // tm
    per = tm // HALO

    def body(xp_ref, xc_ref, xn_ref, dc_ref, dn_ref, w_ref, b_ref, dx_ref, dw_ref, db_ref):
        i = pl.program_id(1)
        prev8 = jnp.where(i > 0, xp_ref[...], 0.0)
        cur = xc_ref[...]
        taps = []
        dzc = dc_ref[...] * _silu_grad(_conv_z(prev8, cur, w_ref, b_ref, taps))
        dzn8 = dn_ref[...] * _silu_grad(_conv_z(cur[tm - HALO:], xn_ref[...], w_ref, b_ref))
        dzn8 = jnp.where(i < nt - 1, dzn8, 0.0)
        row8 = lax.broadcasted_iota(jnp.int32, (HALO, w), 0)
        dx = dzc * w_ref[CONV_K - 1:CONV_K, :]
        for j in range(CONV_K - 1):
            sh = CONV_K - 1 - j
            rolled = pltpu.roll(dzc, tm - sh, 0)
            bottom = jnp.where(row8 >= HALO - sh, pltpu.roll(dzn8, HALO - sh, 0), rolled[tm - HALO:])
            dx = dx + jnp.concatenate([rolled[:tm - HALO], bottom], axis=0) * w_ref[j:j + 1, :]
        dx_ref[...] = dx

        @pl.when(i == 0)
        def _():
            dw_ref[...] = jnp.zeros_like(dw_ref)
            db_ref[...] = jnp.zeros_like(db_ref)

        for j in range(CONV_K):
            dw_ref[j:j + 1, :] += jnp.sum(dzc * taps[j], axis=0, keepdims=True)
        db_ref[...] += jnp.sum(dzc, axis=0, keepdims=True)

    last = s // HALO - 1
    halo_before = lambda col0: pl.BlockSpec((HALO, w), lambda j, i: (jnp.maximum(i * per - 1, 0), col0 + j))
    halo_after = lambda col0: pl.BlockSpec((HALO, w), lambda j, i: (jnp.minimum((i + 1) * per, last), col0 + j))
    tile = lambda col0: pl.BlockSpec((tm, w), lambda j, i: (i, col0 + j))
    return pl.pallas_call(
        body, name=name, grid=(2, nt),
        in_specs=[halo_before(QK_COL0), tile(QK_COL0), halo_after(QK_COL0), tile(0), halo_after(0),
                  pl.BlockSpec((CONV_K, w), lambda j, i: (0, j)), pl.BlockSpec((1, w), lambda j, i: (0, j))],
        out_specs=[tile(0), pl.BlockSpec((CONV_K, w), lambda j, i: (0, j)),
                   pl.BlockSpec((1, w), lambda j, i: (0, j))],
        out_shape=[jax.ShapeDtypeStruct((s, 2 * ML_W), F32), jax.ShapeDtypeStruct((CONV_K, 2 * ML_W), F32),
                   jax.ShapeDtypeStruct((1, 2 * ML_W), F32)],
        compiler_params=_params("parallel", "arbitrary"),
    )(proj, proj, proj, dqk, dqk, conv_w, conv_b)


def _bf16_mm(dims_fwd):
    @jax.custom_vjp
    def mm(a, b):
        return _dot(a.astype(BF16), b.astype(BF16), dims_fwd)

    def fwd(a, b):
        return mm(a, b), (a, b)

    def bwd(res, g):
        a, b = res
        if dims_fwd is NN:
            return _mm_nt(g, b), _mm_tn(a, g)
        if dims_fwd is NT:
            return _mm_nn(g, b), _mm_tn(g, a)
        return _mm_nt(b, g), _mm_nn(a, g)

    mm.defvjp(fwd, bwd)
    return mm


_mm_nn = _bf16_mm(NN)
_mm_nt = _bf16_mm(NT)
_mm_tn = _bf16_mm(TN)


def _tri(lower):
    r = lax.broadcasted_iota(jnp.int32, (CHUNK, CHUNK), 0)
    c = lax.broadcasted_iota(jnp.int32, (CHUNK, CHUNK), 1)
    return ((r >= c) if lower else (r <= c)).astype(F32)


@jax.custom_vjp
def _cumsum_rows(x):
    return lax.dot_general(_tri(True), x, NN, precision=lax.Precision.HIGHEST, preferred_element_type=F32)


def _cumsum_fwd(x):
    return _cumsum_rows(x), None


def _cumsum_bwd(_, g):
    return (lax.dot_general(_tri(False), g, NN, precision=lax.Precision.HIGHEST, preferred_element_type=F32),)


_cumsum_rows.defvjp(_cumsum_fwd, _cumsum_bwd)


def _abs(x):
    return jnp.where(x >= 0, x, -x)


def _log_sigmoid(x):
    return jnp.minimum(x, 0.0) - jnp.log(1.0 + jnp.exp(-_abs(x)))


def _pick_col(x, lane):
    sel = lax.broadcasted_iota(jnp.int32, x.shape, 1) == lane
    return jnp.sum(jnp.where(sel, x, 0.0), axis=1, keepdims=True)


def _pick_row(x, r):
    sel = lax.broadcasted_iota(jnp.int32, x.shape, 0) == r
    return jnp.sum(jnp.where(sel, x, 0.0), axis=0, keepdims=True)


def _mlstm_chunk(qs, ks, vs, oms, gates, gate_bias, mlg, cs, ns, ms):
    gb = gates + gate_bias
    cum = _cumsum_rows(_log_sigmoid(gb))
    gbt = gb.T
    cumt = cum.T
    causal = lax.broadcasted_iota(jnp.int32, (CHUNK, CHUNK), 0) >= lax.broadcasted_iota(jnp.int32, (CHUNK, CHUNK), 1)
    hd = range(ML_HEADS)
    k = [ks[h] * (ML_HD ** -0.5) for h in hd]
    ig_col = [_pick_col(gb, h) for h in hd]
    ig_row = [_pick_row(gbt, h) for h in hd]
    b_col = [_pick_col(cum, ML_HEADS + h) for h in hd]
    b_row = [_pick_row(cumt, ML_HEADS + h) for h in hd]
    g = [_pick_row(b_col[h], CHUNK - 1) for h in hd]
    a = [g[h] - b_col[h] + ig_col[h] for h in hd]
    m_loc = [jnp.max(a[h], axis=0, keepdims=True) for h in hd]
    wa = [jnp.exp(a[h] - m_loc[h]) for h in hd]
    d_log = [jnp.where(causal, b_col[h] - b_row[h] + ig_row[h], -jnp.inf) for h in hd]
    e_log = [b_col[h] + ms[h] for h in hd]
    m_t = [jnp.maximum(e_log[h], jnp.max(d_log[h], axis=1, keepdims=True)) for h in hd]
    d_w = [jnp.exp(d_log[h] - m_t[h]) for h in hd]
    e_w = [jnp.exp(e_log[h] - m_t[h]) for h in hd]
    qk = [_mm_nt(qs[h], k[h]) for h in hd]
    qc = [_mm_nt(qs[h], cs[h]) for h in hd]
    c_loc = [_mm_tn(wa[h] * vs[h], k[h]) for h in hd]
    s_qk = [qk[h] * d_w[h] for h in hd]
    sv = [_mm_nn(s_qk[h], vs[h]) for h in hd]
    n_loc = [jnp.sum(wa[h] * k[h], axis=0, keepdims=True) for h in hd]
    m_out = [jnp.maximum(g[h] + ms[h], m_loc[h]) for h in hd]
    sp = [jnp.exp(g[h] + ms[h] - m_out[h]) for h in hd]
    sl = [jnp.exp(m_loc[h] - m_out[h]) for h in hd]
    c_out = [sp[h] * cs[h] + sl[h] * c_loc[h] for h in hd]
    n_out = [sp[h] * ns[h] + sl[h] * n_loc[h] for h in hd]
    num = [e_w[h] * qc[h] + sv[h] for h in hd]
    den = [e_w[h] * jnp.sum(qs[h] * ns[h], axis=1, keepdims=True) + jnp.sum(s_qk[h], axis=1, keepdims=True) for h in hd]
    hg = [_sigmoid(oms[h]) * (num[h] / jnp.maximum(_abs(den[h]), jnp.exp(-m_t[h]))) for h in hd]
    mu = [jnp.mean(hg[h], axis=1, keepdims=True) for h in hd]
    hc = [hg[h] - mu[h] for h in hd]
    var = [jnp.mean(hc[h] * hc[h], axis=1, keepdims=True) for h in hd]
    ys = [hc[h] * lax.rsqrt(var[h] + LN_EPS) * mlg[h] for h in hd]
    return ys, c_out, n_out, m_out


V_COL = 5
O_COL = 6
ML_SUB = 1


def _mlstm_fwd(qk, proj, gates, gate_bias, mlg, name="mlstm_fwd", gather=()):
    s = qk.shape[0]
    nc = s // CHUNK

    def body(q_ref, k_ref, v_ref, o_ref, g_ref, gb_ref, mlg_ref, y_ref, cp_ref, np_ref, mp_ref, c_s, n_s, m_s):
        ci = pl.program_id(0)

        @pl.when(ci == 0)
        def _():
            c_s[...] = jnp.zeros_like(c_s)
            n_s[...] = jnp.zeros_like(n_s)
            m_s[...] = jnp.zeros_like(m_s)

        for sub in range(ML_SUB):
            rows = slice(CHUNK * sub, CHUNK * (sub + 1))
            hs = lambda ref: [ref[rows, LANES * h:LANES * (h + 1)] for h in range(ML_HEADS)]
            cp_ref[sub] = c_s[...]
            np_ref[sub] = n_s[...]
            mp_ref[sub] = m_s[...]
            ys, c_new, n_new, m_new = _mlstm_chunk(
                hs(q_ref), hs(k_ref), hs(v_ref), hs(o_ref), g_ref[rows, :], gb_ref[...],
                [mlg_ref[:, LANES * h:LANES * (h + 1)] for h in range(ML_HEADS)],
                [c_s[h] for h in range(ML_HEADS)], [n_s[h:h + 1, :] for h in range(ML_HEADS)],
                [m_s[h:h + 1, 0:1] for h in range(ML_HEADS)])
            for h in range(ML_HEADS):
                y_ref[rows, LANES * h:LANES * (h + 1)] = ys[h]
                c_s[h] = c_new[h]
                n_s[h:h + 1, :] = n_new[h]
                m_s[h:h + 1, :] = jnp.broadcast_to(m_new[h], (1, LANES))

    blk = lambda col: pl.BlockSpec((ML_SUB * CHUNK, ML_W), lambda ci: (ci, col))
    vec = lambda w: pl.BlockSpec((1, w), lambda ci: (0, 0))
    return _call(
        body, name=name, grid=(nc // ML_SUB,), args=(qk, qk, proj, proj, gates, gate_bias, mlg), sem=("arbitrary",),
        gather=gather,
        in_specs=[blk(0), blk(1), blk(V_COL), blk(O_COL), pl.BlockSpec((ML_SUB * CHUNK, LANES), lambda ci: (ci, 0)),
                  vec(LANES), vec(ML_W)],
        out_specs=[blk(0), pl.BlockSpec((ML_SUB, ML_HEADS, ML_HD, ML_HD), lambda ci: (ci, 0, 0, 0)),
                   pl.BlockSpec((ML_SUB, 8, LANES), lambda ci: (ci, 0, 0)),
                   pl.BlockSpec((ML_SUB, 8, LANES), lambda ci: (ci, 0, 0))],
        out_shape=[jax.ShapeDtypeStruct((s, ML_W), F32), jax.ShapeDtypeStruct((nc, ML_HEADS, ML_HD, ML_HD), F32),
                   jax.ShapeDtypeStruct((nc, 8, LANES), F32), jax.ShapeDtypeStruct((nc, 8, LANES), F32)],
        scratch_shapes=[pltpu.VMEM((ML_HEADS, ML_HD, ML_HD), F32), pltpu.VMEM((8, LANES), F32),
                        pltpu.VMEM((8, LANES), F32)])


def _mlstm_bwd(qk, proj, gates, gate_bias, mlg, cprev, nprev, mprev, dy, name="mlstm_bwd", exchange=()):
    s = qk.shape[0]
    nc = s // CHUNK

    def body(q_ref, k_ref, v_ref, o_ref, g_ref, gb_ref, mlg_ref, cp_ref, np_ref, mp_ref, dy_ref,
             dqk_ref, dv_ref, do_ref, dg_ref, dgb_ref, dmlg_ref, dc_s, dn_s, dm_s, gb8, mg8):
        ci = pl.program_id(0)

        @pl.when(ci == 0)
        def _():
            dc_s[...] = jnp.zeros_like(dc_s)
            dn_s[...] = jnp.zeros_like(dn_s)
            dm_s[...] = jnp.zeros_like(dm_s)
            gb8[...] = jnp.zeros_like(gb8)
            mg8[...] = jnp.zeros_like(mg8)

        for sub in reversed(range(ML_SUB)):
            rows = slice(CHUNK * sub, CHUNK * (sub + 1))
            hs = lambda ref: [ref[rows, LANES * h:LANES * (h + 1)] for h in range(ML_HEADS)]
            prim = (hs(q_ref), hs(k_ref), hs(v_ref), hs(o_ref), g_ref[rows, :], gb_ref[...],
                    [mlg_ref[:, LANES * h:LANES * (h + 1)] for h in range(ML_HEADS)],
                    [cp_ref[sub, h] for h in range(ML_HEADS)], [np_ref[sub, h:h + 1, :] for h in range(ML_HEADS)],
                    [mp_ref[sub, h:h + 1, 0:1] for h in range(ML_HEADS)])
            _, vjp = jax.vjp(_mlstm_chunk, *prim)
            cot = (hs(dy_ref), [dc_s[h] for h in range(ML_HEADS)], [dn_s[h:h + 1, :] for h in range(ML_HEADS)],
                   [dm_s[h:h + 1, 0:1] for h in range(ML_HEADS)])
            dqs, dks, dvs, dos, dg, dgb, dmlg, dcs, dns, dms = vjp(cot)
            dg_ref[rows, :] = dg
            gb8[0:1, :] += dgb
            for h in range(ML_HEADS):
                sl = slice(LANES * h, LANES * (h + 1))
                dqk_ref[rows, sl] = dqs[h]
                dqk_ref[rows, ML_W + LANES * h:ML_W + LANES * (h + 1)] = dks[h]
                dv_ref[rows, sl] = dvs[h]
                do_ref[rows, sl] = dos[h]
                mg8[0:1, sl] += dmlg[h]
                dc_s[h] = dcs[h]
                dn_s[h:h + 1, :] = dns[h]
                dm_s[h:h + 1, :] = jnp.broadcast_to(dms[h], (1, LANES))

        @pl.when(ci == nb - 1)
        def _():
            dgb_ref[...] = gb8[0:1, :]
            dmlg_ref[...] = mg8[0:1, :]

    nb = nc // ML_SUB
    rev = lambda ci: nb - 1 - ci
    blk = lambda col: pl.BlockSpec((ML_SUB * CHUNK, ML_W), lambda ci: (rev(ci), col))
    vec = lambda w: pl.BlockSpec((1, w), lambda ci: (0, 0))
    st8 = pl.BlockSpec((ML_SUB, 8, LANES), lambda ci: (rev(ci), 0, 0))
    gsp = pl.BlockSpec((ML_SUB * CHUNK, LANES), lambda ci: (rev(ci), 0))
    return _call(
        body, name=name, grid=(nb,), sem=("arbitrary",), exchange=exchange,
        args=(qk, qk, proj, proj, gates, gate_bias, mlg, cprev, nprev, mprev, dy),
        in_specs=[blk(0), blk(1), blk(V_COL), blk(O_COL), gsp, vec(LANES), vec(ML_W),
                  pl.BlockSpec((ML_SUB, ML_HEADS, ML_HD, ML_HD), lambda ci: (rev(ci), 0, 0, 0)), st8, st8, blk(1)],
        out_specs=[pl.BlockSpec((ML_SUB * CHUNK, 2 * ML_W), lambda ci: (rev(ci), 0)), blk(0), blk(0), gsp, vec(LANES),
                   vec(ML_W)],
        out_shape=[jax.ShapeDtypeStruct((s, 2 * ML_W), F32),
                   jax.ShapeDtypeStruct((s, ML_W), F32), jax.ShapeDtypeStruct((s, ML_W), F32),
                   jax.ShapeDtypeStruct((s, LANES), F32), jax.ShapeDtypeStruct((1, LANES), F32),
                   jax.ShapeDtypeStruct((1, ML_W), F32)],
        scratch_shapes=[pltpu.VMEM((ML_HEADS, ML_HD, ML_HD), F32), pltpu.VMEM((8, LANES), F32),
                        pltpu.VMEM((8, LANES), F32), pltpu.VMEM((8, LANES), F32), pltpu.VMEM((8, ML_W), F32)])


def _xattn_tile(qs, ks, vs):
    hd = range(XA_HEADS)
    sc = [_mm_nt(qs[h], ks[h]) * (XA_HD ** -0.5) for h in hd]
    mx = [lax.stop_gradient(jnp.max(sc[h], axis=1, keepdims=True)) for h in hd]
    pe = [jnp.exp(sc[h] - mx[h]) for h in hd]
    pn = [pe[h] / jnp.sum(pe[h], axis=1, keepdims=True) for h in hd]
    return [_mm_nn(pn[h], vs[h]) for h in hd]


def _xa_heads(ref):
    return [ref[:, XA_HD * h:XA_HD * (h + 1)] for h in range(XA_HEADS)]


def _xattn_fwd(q, kv, name="xattn_fwd", tm=512):
    s, d = q.shape

    def body(q_ref, k_ref, v_ref, o_ref):
        outs = _xattn_tile(_xa_heads(q_ref), _xa_heads(k_ref), _xa_heads(v_ref))
        for h in range(XA_HEADS):
            o_ref[:, XA_HD * h:XA_HD * (h + 1)] = outs[h]

    row = pl.BlockSpec((tm, d), lambda i: (i, 0))
    return pl.pallas_call(
        body, name=name, grid=(s // tm,),
        in_specs=[row, pl.BlockSpec((MEM_LEN, d), lambda i: (0, 0)), pl.BlockSpec((MEM_LEN, d), lambda i: (0, 1))],
        out_specs=row, out_shape=jax.ShapeDtypeStruct((s, d), F32),
        compiler_params=_params("parallel"),
    )(q, kv, kv)


def _xattn_bwd(q, kv, do, name="xattn_bwd", tm=512):
    s, d = q.shape

    def body(q_ref, k_ref, v_ref, do_ref, dq_ref, dkv_ref):
        i = pl.program_id(0)
        _, vjp = jax.vjp(_xattn_tile, _xa_heads(q_ref), _xa_heads(k_ref), _xa_heads(v_ref))
        dqs, dks, dvs = vjp(_xa_heads(do_ref))

        @pl.when(i == 0)
        def _():
            dkv_ref[...] = jnp.zeros_like(dkv_ref)

        for h in range(XA_HEADS):
            sl = slice(XA_HD * h, XA_HD * (h + 1))
            dq_ref[:, sl] = dqs[h]
            dkv_ref[:, sl] += dks[h]
            dkv_ref[:, d + XA_HD * h:d + XA_HD * (h + 1)] += dvs[h]

    row = pl.BlockSpec((tm, d), lambda i: (i, 0))
    return pl.pallas_call(
        body, name=name, grid=(s // tm,),
        in_specs=[row, pl.BlockSpec((MEM_LEN, d), lambda i: (0, 0)), pl.BlockSpec((MEM_LEN, d), lambda i: (0, 1)), row],
        out_specs=[row, pl.BlockSpec((MEM_LEN, 2 * d), lambda i: (0, 0))],
        out_shape=[jax.ShapeDtypeStruct((s, d), F32), jax.ShapeDtypeStruct((MEM_LEN, 2 * d), F32)],
        compiler_params=_params("arbitrary"),
    )(q, kv, kv, do)


def _loss_head(y, target, name="loss_head", tm=1024):
    s, d = y.shape
    nt = s // tm

    def body(y_ref, t_ref, dy_ref, loss_ref, acc):
        i = pl.program_id(0)
        err = y_ref[...] - t_ref[...]
        dy_ref[...] = err * (1.0 / d)

        @pl.when(i == 0)
        def _():
            acc[...] = jnp.zeros_like(acc)

        acc[...] += _rowsum8(err * err)

        @pl.when(i == nt - 1)
        def _():
            tot = jnp.sum(jnp.sum(acc[...], axis=0, keepdims=True), axis=1, keepdims=True)
            loss_ref[...] = jnp.broadcast_to(tot * (0.5 / d), (1, LANES))

    row = pl.BlockSpec((tm, d), lambda i: (i, 0))
    return pl.pallas_call(
        body, name=name, grid=(nt,),
        in_specs=[row, row], out_specs=[row, pl.BlockSpec((1, LANES), lambda i: (0, 0))],
        out_shape=[jax.ShapeDtypeStruct((s, d), F32), jax.ShapeDtypeStruct((1, LANES), F32)],
        scratch_shapes=[pltpu.VMEM((8, d), F32)],
        compiler_params=_params("arbitrary"),
    )(y, target)


def _adam2d(recv, w, m, v, name, layer=None):
    rows, cols = w.shape[-2:]
    fits = [t for t in range(16, rows + 1, 16) if rows % t == 0 and t * cols <= 128 * 1024]
    tr = max(fits) if fits else rows

    def body(r_ref, w_ref, m_ref, v_ref, g_ref, d_ref, mo_ref, vo_ref):
        g = r_ref[0].astype(F32)
        for j in range(1, N_DEV):
            g = g + r_ref[j].astype(F32)
        mn = ADAM_B1 * m_ref[...] + (1.0 - ADAM_B1) * g
        vn = ADAM_B2 * v_ref[...] + (1.0 - ADAM_B2) * jnp.square(g)
        m_hat = mn / (1.0 - ADAM_B1 ** ADAM_STEP)
        v_hat = vn / (1.0 - ADAM_B2 ** ADAM_STEP)
        g_ref[...] = g
        d_ref[...] = -ADAM_LR * (m_hat / (jnp.sqrt(v_hat) + ADAM_EPS) + ADAM_WD * w_ref[...])
        mo_ref[...] = mn
        vo_ref[...] = vn

    row = pl.BlockSpec((tr, cols), lambda i: (i, 0))
    if layer is None:
        wspec = row
    else:
        wspec = pl.BlockSpec((None, None, tr, cols), lambda i: (0, layer, i, 0))
    return pl.pallas_call(
        body, name=name, grid=(rows // tr,),
        in_specs=[pl.BlockSpec((N_DEV, tr, cols), lambda i: (0, i, 0)), wspec, wspec, wspec],
        out_specs=[row] * 4, out_shape=[jax.ShapeDtypeStruct((rows, cols), F32)] * 4,
        compiler_params=_params("parallel"),
    )(recv, w, m, v)


WEIGHTS = ("rel_bias", "ln_g", "ln_b", "ffn_w_gate", "ffn_w_up", "ffn_w_down", "w_in", "conv_w", "conv_b",
           "ig_bias", "fg_bias", "ml_norm_g", "w_out", "xq_w", "xkv_w", "xo_w")
SMALL = ("rel_bias", "ln_g", "ln_b", "conv_w", "conv_b", "ig_bias", "fg_bias", "ml_norm_g")
SMALL_SHAPES = {
    "rel_bias": (REL_BUCKETS, ATT_HEADS), "ln_g": (1, 4, LANES), "ln_b": (1, 4, LANES), "conv_w": (1, CONV_K, LANES),
    "conv_b": (1, 2 * ML_W), "ig_bias": (1, ML_HEADS), "fg_bias": (1, ML_HEADS), "ml_norm_g": (1, ML_W),
}
SMALL_ROWS = 8


def _pack_small(parts, lead=()):
    out = []
    for p in parts:
        p = jnp.pad(p, [(0, 0)] * len(lead) + [(0, SMALL_ROWS * LANES - p.shape[-1])])
        out.append(p.reshape(lead + (SMALL_ROWS, LANES)))
    return jnp.concatenate(out, axis=len(lead))


def _unpack_small(flat):
    out = {}
    for i, n in enumerate(SMALL):
        cnt = int(np.prod(SMALL_SHAPES[n]))
        out[n] = flat[SMALL_ROWS * i:SMALL_ROWS * (i + 1)].reshape(-1)[:cnt].reshape(SMALL_SHAPES[n])
    return out


def _split8(full, axis):
    shp = full.shape
    t = full.reshape(shp[:axis] + (N_DEV, shp[axis] // N_DEV) + shp[axis + 1:])
    return jnp.moveaxis(t, axis, 0).reshape(N_DEV, -1)


def _rep8(full):
    return jnp.broadcast_to(full.reshape(1, -1), (N_DEV, full.size))


def kernel(x, mem, rel_bias, ln_g, ln_b, ffn_w_gate, ffn_w_up, ffn_w_down, w_in, conv_w, conv_b, ig_bias, fg_bias, ml_norm_g, w_out, xq_w, xkv_w, xo_w, loss_target, m_rel_bias, m_ln_g, m_ln_b, m_ffn_w_gate, m_ffn_w_up, m_ffn_w_down, m_w_in, m_conv_w, m_conv_b, m_ig_bias, m_fg_bias, m_ml_norm_g, m_w_out, m_xq_w, m_xkv_w, m_xo_w, v_rel_bias, v_ln_g, v_ln_b, v_ffn_w_gate, v_ffn_w_up, v_ffn_w_down, v_w_in, v_conv_w, v_conv_b, v_ig_bias, v_fg_bias, v_ml_norm_g, v_w_out, v_xq_w, v_xkv_w, v_xo_w):
    w_tree = dict(rel_bias=rel_bias, ln_g=ln_g, ln_b=ln_b, ffn_w_gate=ffn_w_gate, ffn_w_up=ffn_w_up,
                  ffn_w_down=ffn_w_down, w_in=w_in, conv_w=conv_w, conv_b=conv_b, ig_bias=ig_bias, fg_bias=fg_bias,
                  ml_norm_g=ml_norm_g, w_out=w_out, xq_w=xq_w, xkv_w=xkv_w, xo_w=xo_w)
    m_tree = dict(rel_bias=m_rel_bias, ln_g=m_ln_g, ln_b=m_ln_b, ffn_w_gate=m_ffn_w_gate, ffn_w_up=m_ffn_w_up,
                  ffn_w_down=m_ffn_w_down, w_in=m_w_in, conv_w=m_conv_w, conv_b=m_conv_b, ig_bias=m_ig_bias,
                  fg_bias=m_fg_bias, ml_norm_g=m_ml_norm_g, w_out=m_w_out, xq_w=m_xq_w, xkv_w=m_xkv_w, xo_w=m_xo_w)
    v_tree = dict(rel_bias=v_rel_bias, ln_g=v_ln_g, ln_b=v_ln_b, ffn_w_gate=v_ffn_w_gate, ffn_w_up=v_ffn_w_up,
                  ffn_w_down=v_ffn_w_down, w_in=v_w_in, conv_w=v_conv_w, conv_b=v_conv_b, ig_bias=v_ig_bias,
                  fg_bias=v_fg_bias, ml_norm_g=v_ml_norm_g, w_out=v_w_out, xq_w=v_xq_w, xkv_w=v_xkv_w, xo_w=v_xo_w)
    x0 = x[0]
    pad_ff = FF_PAD - FF_SHARD
    bf = lambda t: t.astype(BF16)

    pad_rows = lambda t: jnp.pad(t, ((0, pad_ff), (0, 0)))
    ffn_shards = [(pad_rows(bf(ffn_w_gate[0, l]).T), pad_rows(bf(ffn_w_up[0, l]).T), pad_rows(bf(ffn_w_down[0, l])))
                  for l in range(2)]
    pairs = lambda t: t.reshape(N_PAIR, FF_PAIR, D_MODEL)
    w_in_shard = jnp.pad(bf(w_in[0]), ((0, 0), (0, ATT_W - W_IN_SHARD)))
    small_shard = jnp.concatenate([ln_g[0], ln_b[0], conv_w[0], jnp.zeros((4, LANES), F32)], axis=0)
    gate_bias = jnp.pad(jnp.concatenate([ig_bias, fg_bias], axis=1), ((0, 0), (0, LANES - 2 * ML_HEADS)))
    buckets = _bucket_tables()

    wg0, wu0, wd0, small_all = _gather_two_level("ffn1_weights_gather", ffn_shards[0] + (small_shard,))
    wg0, wu0, wd0 = pairs(wg0), pairs(wu0), pairs(wd0)
    unshard = lambda t: jnp.moveaxis(t, 0, 1).reshape(4, D_MODEL)
    ln_g_full, ln_b_full, conv_w_full = unshard(small_all[:, 0:4]), unshard(small_all[:, 4:8]), unshard(small_all[:, 8:12])
    lng = lambda i: ln_g_full[i:i + 1]
    lnb = lambda i: ln_b_full[i:i + 1]

    u0, x1, a0, b0, win_all, wout_all, xq_all, xo_all, xkv_all = _ffn_fwd(
        x0, wg0, wu0, wd0, lng(0), lnb(0), "ffn1_fwd",
        gather=(w_in_shard, bf(w_out[0]), bf(xq_w[0]), bf(xo_w[0]), bf(xkv_w[0])))
    w_in_full = jnp.moveaxis(win_all[:, :, :W_IN_SHARD], 0, 1).reshape(D_MODEL, W_IN)
    w_main = w_in_full[:, :W_IN_MAIN]
    w_gate_cols = jnp.pad(w_in_full[:, W_IN_MAIN:], ((0, 0), (0, LANES - 2 * ML_HEADS)))
    w_out_full = wout_all.reshape(D_MODEL, D_MODEL)
    xq_full = xq_all.reshape(D_MODEL, D_MODEL)
    xo_full = xo_all.reshape(D_MODEL, D_MODEL)

    proj, wg1 = _matmul(x1, w_main, "nn", "proj_fwd", tn=W_IN_MAIN // 2, tk=D_MODEL, gather=(ffn_shards[1][0],))
    gates, = _matmul(x1, w_gate_cols, "nn", "gates_fwd", tk=D_MODEL)
    biasm = _bias_fwd(rel_bias, buckets)
    att, lse, wd1 = _dil_fwd(proj, biasm, gather=(ffn_shards[1][2],))
    qk = _conv_fwd(proj, conv_w_full, conv_b)
    y_m, c_prev, n_prev, m_prev, wu1 = _mlstm_fwd(qk, proj, gates, gate_bias, ml_norm_g, gather=(ffn_shards[1][1],))
    u1, x2 = _matmul_resid_ln((att, y_m), w_out_full, x1, lng(1), lnb(1), "w_out_fwd")
    q_x, = _matmul(x2, xq_full, "nn", "xq_fwd", tn=D_MODEL, tk=D_MODEL)
    kv, = _matmul(mem[0], xkv_all, "nn", "xkv_fwd", tk=D_MODEL)
    o_x = _xattn_fwd(q_x, kv)
    u2, x3 = _matmul_resid_ln((o_x,), xo_full, x2, lng(2), lnb(2), "xo_fwd")
    wg1, wu1, wd1 = pairs(wg1), pairs(wu1), pairs(wd1)
    u3, x4, a3, b3 = _ffn_fwd(x3, wg1, wu1, wd1, lng(3), lnb(3), "ffn2_fwd")
    dx4, loss_row = _loss_head(x4, loss_target[0])

    dx3, xb, df, da, db, hh, dg3, db3 = _ffn_bwd_x(dx4, u3, x3, wg1, wu1, wd1, lng(3), a3, b3, "ffn2_bwd_x")
    ffn2_send = (_ffn_bwd_w(xb, da, "ffn2_bwd_wg", down=False)[0], _ffn_bwd_w(xb, db, "ffn2_bwd_wu", down=False)[0],
                 _ffn_bwd_w(df, hh, "ffn2_bwd_wd", down=True)[0])

    du2, do_x, dg2, db2 = _ln_bwd(dx3, u2, lng(2), xo_full, "xattn_ln_bwd")
    g_xo, = _matmul(o_x, du2, "tn", "xo_bwd_w", tm=D_MODEL, tn=D_MODEL, out_dtype=BF16)
    dq_x, dkv = _xattn_bwd(q_x, kv, do_x)
    g_xq, = _matmul(x2, dq_x, "tn", "xq_bwd_w", tm=D_MODEL, tn=D_MODEL, out_dtype=BF16)
    g_xkv, = _matmul(mem[0], dkv, "tn", "xkv_bwd_w", tm=D_MODEL, tn=2 * D_MODEL // N_DEV, tk=MEM_LEN,
                     out_dtype=BF16, blocked_out=True)
    dx2, = _matmul(dq_x, xq_full, "nt", "xq_bwd_x", tn=D_MODEL, tk=D_MODEL, add=du2, add_scale=ALPHA)

    du1, dcat, dg1, db1 = _ln_bwd(dx2, u1, lng(1), w_out_full, "mixer_ln_bwd")
    g_w_out = jnp.concatenate(
        [_matmul(half, du1, "tn", f"w_out_bwd_w_{i}", tn=D_MODEL, out_dtype=BF16)[0] for i, half in enumerate((att, y_m))],
        axis=0)
    dqk, dv_m, do_m, dgates, dgate_bias, g_mlg, *ffn2_recv = _mlstm_bwd(
        qk, proj, gates, gate_bias, ml_norm_g, c_prev, n_prev, m_prev, dcat, exchange=tuple(ffn2_send))
    dqk_pre, g_conv_w, g_conv_b = _conv_bwd(proj, dqk, conv_w_full, conv_b)
    dq_a, dk_a, dv_a, dbias = _dil_bwd(proj, biasm, lse, att, dcat)
    g_rel = _bias_bwd(dbias.reshape(biasm.shape), buckets)[:, :ATT_HEADS]
    dproj = jnp.concatenate([dq_a, dk_a, dv_a, bf(dqk_pre), bf(dv_m), bf(do_m)], axis=1)
    g_w_main, = _matmul(x1, dproj, "tn", "proj_bwd_w", tm=D_MODEL, tn=W_IN_MAIN // 2, tk=1024, out_dtype=BF16)
    g_w_gates, = _matmul(x1, dgates, "tn", "gates_bwd_w", tm=D_MODEL, out_dtype=BF16)
    g_w_in = jnp.concatenate([g_w_main, g_w_gates[:, :2 * ML_HEADS]], axis=1)
    dx1, = _matmul(dproj, w_main, "nt", "proj_bwd_x", tn=D_MODEL, tk=W_IN_MAIN // 2, add=du1, add_scale=ALPHA)
    dx1, = _matmul(dgates, w_gate_cols, "nt", "gates_bwd_x", tn=D_MODEL, add=dx1)

    rows8 = lambda t: t.reshape(N_DEV, D_MODEL // N_DEV, D_MODEL)
    mid_send = (rows8(g_xo), rows8(g_xq), g_xkv, rows8(g_w_out),
                jnp.moveaxis(g_w_in.reshape(D_MODEL, N_DEV, W_IN_SHARD), 1, 0))
    dx0, xb, df, da, db, hh, dg0, db0, r_xo, r_xq, r_xkv, r_w_out, r_w_in = _ffn_bwd_x(
        dx1, u0, x0, wg0, wu0, wd0, lng(0), a0, b0, "ffn1_bwd_x", exchange=mid_send)
    small_blocks = {
        "rel_bias": _rep8(g_rel),
        "ln_g": _split8(jnp.concatenate([dg0, dg1, dg2, dg3], axis=0), 1),
        "ln_b": _split8(jnp.concatenate([db0, db1, db2, db3], axis=0), 1),
        "conv_w": _split8(g_conv_w, 1),
        "conv_b": _rep8(g_conv_b),
        "ig_bias": _rep8(dgate_bias[:, :ML_HEADS]),
        "fg_bias": _rep8(dgate_bias[:, ML_HEADS:2 * ML_HEADS]),
        "ml_norm_g": _rep8(g_mlg),
    }
    small_send = _pack_small([small_blocks[n] for n in SMALL], lead=(N_DEV,))
    g_wg, r_small = _ffn_bwd_w(xb, da, "ffn1_bwd_wg", down=False, exchange=(small_send,))
    g_wu, r_wg = _ffn_bwd_w(xb, db, "ffn1_bwd_wu", down=False, exchange=(g_wg,))
    g_wd, r_wu = _ffn_bwd_w(df, hh, "ffn1_bwd_wd", down=True, exchange=(g_wu,))
    send_sems, recv_sems, wd_thru, wd_land, token = _exchange_start(g_wd, "ffn1_grads_exchange_start")
    zero = token[0, 0]

    res = {}
    for n, r in (("w_in", r_w_in), ("w_out", r_w_out), ("xq_w", r_xq), ("xkv_w", r_xkv), ("xo_w", r_xo)):
        res[n] = [t[None] for t in _adam2d(r, w_tree[n][0] + zero, m_tree[n][0], v_tree[n][0], f"adamw_{n}")]
    ffn_w = {n: w_tree[n] + zero for n in ("ffn_w_gate", "ffn_w_up", "ffn_w_down")}
    early = {(n, l): _adam2d(r, ffn_w[n], m_tree[n], v_tree[n], f"adamw_{n}_{l}", layer=l)
             for n, l, r in (("ffn_w_gate", 0, r_wg), ("ffn_w_gate", 1, ffn2_recv[0]), ("ffn_w_up", 0, r_wu),
                             ("ffn_w_up", 1, ffn2_recv[1]), ("ffn_w_down", 1, ffn2_recv[2]))}
    wd_thru, r_wd = _exchange_wait(send_sems, recv_sems, wd_thru, wd_land, early[("ffn_w_down", 1)][1],
                                   "ffn1_grads_exchange_wait")
    me = 4 * lax.axis_index("x") + 2 * lax.axis_index("y") + lax.axis_index("c")
    r_wd = lax.dynamic_update_slice_in_dim(r_wd, lax.dynamic_slice_in_dim(wd_thru, me, 1, 0), me, 0)
    early[("ffn_w_down", 0)] = _adam2d(r_wd, ffn_w["ffn_w_down"], m_tree["ffn_w_down"], v_tree["ffn_w_down"],
                                       "adamw_ffn_w_down_0", layer=0)
    for n in ("ffn_w_gate", "ffn_w_up", "ffn_w_down"):
        res[n] = [jnp.stack([early[(n, 0)][j], early[(n, 1)][j]])[None] for j in range(4)]
    pack = lambda tree: _pack_small([tree[n].reshape(-1) for n in SMALL])
    small = [_unpack_small(t) for t in _adam2d(r_small, pack(w_tree), pack(m_tree), pack(v_tree), "adamw_small")]
    for n in SMALL:
        res[n] = [small[j][n] for j in range(4)]

    loss = lax.psum(loss_row[0, 0], ("x", "y", "c"))
    return (loss, dx0[None], *[res[n][0] for n in WEIGHTS], *[res[n][1] for n in WEIGHTS],
            *[res[n][2] for n in WEIGHTS], *[res[n][3] for n in WEIGHTS])
```
